```python
import jax, jax.numpy as jnp
from jax import lax
import numpy as np

D_MODEL = 1024
BATCH = 16
SEQ = 2048
DEPTH = 2

N_META = 16
EPS = 1e-6
SSM_D_INNER = 2 * D_MODEL
SSM_HEAD_DIM = 64
SSM_HEADS = SSM_D_INNER // SSM_HEAD_DIM
SSM_GROUPS = 4
SSM_STATE = 128
SSM_CONV = 4
SSM_CHUNK = 128
SSM_CONV_DIM = SSM_D_INNER + 2 * SSM_GROUPS * SSM_STATE
MLA_HEADS = 8
MLA_Q_LORA = D_MODEL // 2
MLA_KV_LORA = D_MODEL // 4
MLA_NOPE = 128
MLA_ROPE = 64
MLA_V = 128
ROPE_THETA = 10000.0
Q_BLOCK = 128
D_FF = 4 * D_MODEL
IN_SPLITS = [SSM_D_INNER, SSM_CONV_DIM, SSM_HEADS, MLA_Q_LORA, MLA_KV_LORA, MLA_ROPE, D_MODEL, D_MODEL]
IN_DIM = sum(IN_SPLITS)

kernel_name = "hybrid_ssd_mla_meta_block"


def rms_norm(x, w):
    xf = x.astype(jnp.float32)
    y = xf * lax.rsqrt(jnp.mean(xf * xf, axis=-1, keepdims=True) + EPS)
    return (y * w.astype(jnp.float32)).astype(x.dtype)


def rope_tables(n_pos, dim):
    inv = ROPE_THETA ** (-jnp.arange(0, dim, 2, dtype=jnp.float32) / dim)
    ang = jnp.arange(n_pos, dtype=jnp.float32)[:, None] * inv[None, :]
    return jnp.cos(ang), jnp.sin(ang)


def apply_rope(x, cos, sin):
    x1, x2 = jnp.split(x.astype(jnp.float32), 2, axis=-1)
    return jnp.concatenate([x1 * cos - x2 * sin, x2 * cos + x1 * sin], axis=-1).astype(x.dtype)


def causal_dwconv(x, w, b):
    k, c = w.shape
    y = lax.conv_general_dilated(x, w[:, None, :].astype(x.dtype), window_strides=(1,),
                                 padding=[(k - 1, 0)], dimension_numbers=("NWC", "WIO", "NWC"),
                                 feature_group_count=c)
    return y + b.astype(x.dtype)


def ssd_chunked(xdt, adt, bm, cm):
    b, t, h, p = xdt.shape
    g, n = bm.shape[-2:]
    e = h // g
    q = SSM_CHUNK
    c = t // q
    xc = xdt.reshape(b, c, q, g, e, p)
    a = adt.astype(jnp.float32).reshape(b, c, q, g, e).transpose(0, 3, 4, 1, 2)
    bc = bm.reshape(b, c, q, g, n)
    cc = cm.reshape(b, c, q, g, n)
    a_cs = jnp.cumsum(a, axis=-1)
    causal = np.tril(np.ones((q, q), dtype=bool))
    l_dec = jnp.exp(jnp.where(causal, a_cs[..., :, None] - a_cs[..., None, :], -jnp.inf))
    cb = jnp.einsum("bclgn,bcsgn->bcgls", cc, bc)
    y_diag = jnp.einsum("bcgls,bgecls,bcsgep->bclgep", cb, l_dec, xc)
    decay_states = jnp.exp(a_cs[..., -1:] - a_cs)
    states = jnp.einsum("bcsgn,bgecs,bcsgep->bcgepn", bc, decay_states, xc)
    chunk_decay = jnp.exp(a_cs[..., -1])

    def step(hs, inp):
        s_c, d_c = inp
        return hs * d_c[..., None, None] + s_c, hs

    h0 = jnp.zeros((b, g, e, p, n), jnp.float32)
    _, prev = lax.scan(step, h0, (states.astype(jnp.float32).transpose(1, 0, 2, 3, 4, 5),
                                  chunk_decay.transpose(3, 0, 1, 2)))
    prev = prev.transpose(1, 0, 2, 3, 4, 5)
    y_off = jnp.einsum("bclgn,bcgepn,bgecl->bclgep", cc, prev, jnp.exp(a_cs))
    return (y_diag + y_off).reshape(b, t, h, p).astype(xdt.dtype)


def ssd_mixer(z, xbc, dt, conv_w, conv_b, dt_bias, a_log, d_skip, norm_w):
    bsz, L, _ = xbc.shape
    xbc = jax.nn.silu(causal_dwconv(xbc, conv_w, conv_b))
    xs, bm, cm = jnp.split(xbc, [SSM_D_INNER, SSM_D_INNER + SSM_GROUPS * SSM_STATE], axis=-1)
    xs = xs.reshape(bsz, L, SSM_HEADS, SSM_HEAD_DIM)
    bm = bm.reshape(bsz, L, SSM_GROUPS, SSM_STATE)
    cm = cm.reshape(bsz, L, SSM_GROUPS, SSM_STATE)
    dt = jax.nn.softplus(dt.astype(jnp.float32) + dt_bias.astype(jnp.float32))
    a = -jnp.exp(a_log.astype(jnp.float32))
    pad = SSM_CHUNK - N_META

    def lpad(t):
        return jnp.pad(t, [(0, 0), (pad, 0)] + [(0, 0)] * (t.ndim - 2))

    y = ssd_chunked(lpad(xs * dt[..., None]), lpad(dt * a), lpad(bm), lpad(cm))[:, pad:]
    y = y + xs * d_skip[:, None].astype(xs.dtype)
    y = y.reshape(bsz, L, SSM_D_INNER) * jax.nn.silu(z)
    gsz = SSM_D_INNER // SSM_GROUPS
    y = rms_norm(y.reshape(bsz, L, SSM_GROUPS, gsz), norm_w.reshape(SSM_GROUPS, gsz))
    return y.reshape(bsz, L, SSM_D_INNER)


def mla_mixer(c_q, c_kv, k_rope, q_norm_w, kv_norm_w, w_uq, w_ukv, cos, sin):
    bsz, L, _ = c_q.shape
    qf = (rms_norm(c_q, q_norm_w) @ w_uq).reshape(bsz, L, MLA_HEADS, MLA_NOPE + MLA_ROPE)
    q_nope, q_pe = jnp.split(qf, [MLA_NOPE], axis=-1)
    q_pe = apply_rope(q_pe, cos[:, None, :], sin[:, None, :])
    kv = (rms_norm(c_kv, kv_norm_w) @ w_ukv).reshape(bsz, L, MLA_HEADS, MLA_NOPE + MLA_V)
    k_nope, v = jnp.split(kv, [MLA_NOPE], axis=-1)
    k_pe = apply_rope(k_rope, cos, sin)
    scale = (MLA_NOPE + MLA_ROPE) ** -0.5
    bounds = [0, N_META] + list(range(N_META + Q_BLOCK, L + 1, Q_BLOCK))
    outs = []
    for qs, qe in zip(bounds[:-1], bounds[1:]):
        s = (jnp.einsum("bqhd,bkhd->bhqk", q_nope[:, qs:qe], k_nope[:, :qe])
             + jnp.einsum("bqhr,bkr->bhqk", q_pe[:, qs:qe], k_pe[:, :qe])).astype(jnp.float32) * scale
        mask = np.arange(qs, qe)[:, None] >= np.arange(qe)[None, :]
        p = jax.nn.softmax(jnp.where(mask, s, -jnp.inf), axis=-1).astype(v.dtype)
        outs.append(jnp.einsum("bhqk,bkhv->bqhv", p, v[:, :qe]))
    o = jnp.concatenate(outs, axis=1)
    return o.reshape(bsz, L, MLA_HEADS * MLA_V)


def _fwd_setup_inputs(seed: int = 0) -> dict:
    key = jax.random.key(seed)
    ks = jax.random.split(key, 24)
    f32 = jnp.float32
    nrm = lambda k, shape, s: jax.random.normal(k, shape, f32) * s
    gain = lambda k, shape: 1.0 + 0.02 * jax.random.normal(k, shape, f32)
    res_scale = (2 * DEPTH) ** -0.5
    dt0 = jnp.exp(jax.random.uniform(ks[5], (DEPTH, SSM_HEADS), f32, np.log(1e-3), np.log(1e-1)))
    dt_bias = dt0 + jnp.log(-jnp.expm1(-dt0))
    return {
        "x": nrm(ks[0], (BATCH, SEQ, D_MODEL), 1.0),
        "meta_tokens": nrm(ks[1], (N_META, D_MODEL), 1.0),
        "norm_mix_w": gain(ks[2], (DEPTH, D_MODEL)),
        "w_in": nrm(ks[3], (DEPTH, D_MODEL, IN_DIM), D_MODEL ** -0.5),
        "conv_w": nrm(ks[4], (DEPTH, SSM_CONV, SSM_CONV_DIM), SSM_CONV ** -0.5),
        "conv_b": nrm(ks[6], (DEPTH, SSM_CONV_DIM), 0.01),
        "dt_bias": dt_bias,
        "a_log": jnp.log(jax.random.uniform(ks[7], (DEPTH, SSM_HEADS), f32, 1.0, 16.0)),
        "d_skip": 1.0 + 0.1 * jax.random.normal(ks[8], (DEPTH, SSM_HEADS), f32),
        "ssm_norm_w": gain(ks[9], (DEPTH, SSM_D_INNER)),
        "q_norm_w": gain(ks[10], (DEPTH, MLA_Q_LORA)),
        "kv_norm_w": gain(ks[11], (DEPTH, MLA_KV_LORA)),
        "w_uq": nrm(ks[12], (DEPTH, MLA_Q_LORA, MLA_HEADS * (MLA_NOPE + MLA_ROPE)), MLA_Q_LORA ** -0.5),
        "w_ukv": nrm(ks[13], (DEPTH, MLA_KV_LORA, MLA_HEADS * (MLA_NOPE + MLA_V)), MLA_KV_LORA ** -0.5),
        "w_branch_ssm": nrm(ks[14], (DEPTH, SSM_D_INNER, D_MODEL), SSM_D_INNER ** -0.5),
        "w_branch_mla": nrm(ks[15], (DEPTH, MLA_HEADS * MLA_V, D_MODEL), (MLA_HEADS * MLA_V) ** -0.5),
        "w_out": nrm(ks[16], (DEPTH, D_MODEL, D_MODEL), D_MODEL ** -0.5 * res_scale),
        "norm_mlp_w": gain(ks[17], (DEPTH, D_MODEL)),
        "w_mlp_up": nrm(ks[18], (DEPTH, D_MODEL, D_FF), D_MODEL ** -0.5),
        "w_mlp_down": nrm(ks[19], (DEPTH, D_FF, D_MODEL), D_FF ** -0.5 * res_scale),
        "final_norm_w": gain(ks[20], (D_MODEL,)),
    }


def _fwd_reference(x, meta_tokens, norm_mix_w, w_in, conv_w, conv_b, dt_bias, a_log, d_skip, ssm_norm_w,
              q_norm_w, kv_norm_w, w_uq, w_ukv, w_branch_ssm, w_branch_mla, w_out, norm_mlp_w,
              w_mlp_up, w_mlp_down, final_norm_w):
    bsz = x.shape[0]
    meta = jnp.broadcast_to(meta_tokens.astype(x.dtype)[None], (bsz, N_META, D_MODEL))
    h = jnp.concatenate([meta, x], axis=1)
    L = h.shape[1]
    cos, sin = rope_tables(L, MLA_ROPE)
    split_idx = np.cumsum(IN_SPLITS)[:-1].tolist()
    for i in range(DEPTH):
        u = rms_norm(h, norm_mix_w[i])
        z, xbc, dt, c_q, c_kv, k_rope, g_ssm, g_mla = jnp.split(u @ w_in[i], split_idx, axis=-1)
        y_ssm = ssd_mixer(z, xbc, dt, conv_w[i], conv_b[i], dt_bias[i], a_log[i], d_skip[i], ssm_norm_w[i])
        y_mla = mla_mixer(c_q, c_kv, k_rope, q_norm_w[i], kv_norm_w[i], w_uq[i], w_ukv[i], cos, sin)
        mixed = (jax.nn.sigmoid(g_ssm) * (y_ssm @ w_branch_ssm[i])
                 + jax.nn.sigmoid(g_mla) * (y_mla @ w_branch_mla[i]))
        h = h + mixed @ w_out[i]
        v = rms_norm(h, norm_mlp_w[i])
        h = h + jnp.square(jax.nn.relu(v @ w_mlp_up[i])) @ w_mlp_down[i]
    return rms_norm(h, final_norm_w)[:, N_META:]


import jax as _jax
import jax.numpy as _jnp

TWIN_FORMAT = 'train_step'
FWD_PARAMS = ['x', 'meta_tokens', 'norm_mix_w', 'w_in', 'conv_w', 'conv_b', 'dt_bias', 'a_log', 'd_skip', 'ssm_norm_w', 'q_norm_w', 'kv_norm_w', 'w_uq', 'w_ukv', 'w_branch_ssm', 'w_branch_mla', 'w_out', 'norm_mlp_w', 'w_mlp_up', 'w_mlp_down', 'final_norm_w']
TWIN_WEIGHTS = ['meta_tokens', 'norm_mix_w', 'w_in', 'conv_w', 'conv_b', 'dt_bias', 'a_log', 'd_skip', 'ssm_norm_w', 'q_norm_w', 'kv_norm_w', 'w_uq', 'w_ukv', 'w_branch_ssm', 'w_branch_mla', 'w_out', 'norm_mlp_w', 'w_mlp_up', 'w_mlp_down', 'final_norm_w']
TWIN_DIFF_INPUT = 'x'
TWIN_INPUTS = ['x', 'meta_tokens', 'norm_mix_w', 'w_in', 'conv_w', 'conv_b', 'dt_bias', 'a_log', 'd_skip', 'ssm_norm_w', 'q_norm_w', 'kv_norm_w', 'w_uq', 'w_ukv', 'w_branch_ssm', 'w_branch_mla', 'w_out', 'norm_mlp_w', 'w_mlp_up', 'w_mlp_down', 'final_norm_w', 'loss_target', 'm_meta_tokens', 'm_norm_mix_w', 'm_w_in', 'm_conv_w', 'm_conv_b', 'm_dt_bias', 'm_a_log', 'm_d_skip', 'm_ssm_norm_w', 'm_q_norm_w', 'm_kv_norm_w', 'm_w_uq', 'm_w_ukv', 'm_w_branch_ssm', 'm_w_branch_mla', 'm_w_out', 'm_norm_mlp_w', 'm_w_mlp_up', 'm_w_mlp_down', 'm_final_norm_w', 'v_meta_tokens', 'v_norm_mix_w', 'v_w_in', 'v_conv_w', 'v_conv_b', 'v_dt_bias', 'v_a_log', 'v_d_skip', 'v_ssm_norm_w', 'v_q_norm_w', 'v_kv_norm_w', 'v_w_uq', 'v_w_ukv', 'v_w_branch_ssm', 'v_w_branch_mla', 'v_w_out', 'v_norm_mlp_w', 'v_w_mlp_up', 'v_w_mlp_down', 'v_final_norm_w']
TWIN_OUTPUTS = ['loss', 'grad_x', 'grad_meta_tokens', 'grad_norm_mix_w', 'grad_w_in', 'grad_conv_w', 'grad_conv_b', 'grad_dt_bias', 'grad_a_log', 'grad_d_skip', 'grad_ssm_norm_w', 'grad_q_norm_w', 'grad_kv_norm_w', 'grad_w_uq', 'grad_w_ukv', 'grad_w_branch_ssm', 'grad_w_branch_mla', 'grad_w_out', 'grad_norm_mlp_w', 'grad_w_mlp_up', 'grad_w_mlp_down', 'grad_final_norm_w', 'delta_meta_tokens', 'delta_norm_mix_w', 'delta_w_in', 'delta_conv_w', 'delta_conv_b', 'delta_dt_bias', 'delta_a_log', 'delta_d_skip', 'delta_ssm_norm_w', 'delta_q_norm_w', 'delta_kv_norm_w', 'delta_w_uq', 'delta_w_ukv', 'delta_w_branch_ssm', 'delta_w_branch_mla', 'delta_w_out', 'delta_norm_mlp_w', 'delta_w_mlp_up', 'delta_w_mlp_down', 'delta_final_norm_w', 'new_m_meta_tokens', 'new_m_norm_mix_w', 'new_m_w_in', 'new_m_conv_w', 'new_m_conv_b', 'new_m_dt_bias', 'new_m_a_log', 'new_m_d_skip', 'new_m_ssm_norm_w', 'new_m_q_norm_w', 'new_m_kv_norm_w', 'new_m_w_uq', 'new_m_w_ukv', 'new_m_w_branch_ssm', 'new_m_w_branch_mla', 'new_m_w_out', 'new_m_norm_mlp_w', 'new_m_w_mlp_up', 'new_m_w_mlp_down', 'new_m_final_norm_w', 'new_v_meta_tokens', 'new_v_norm_mix_w', 'new_v_w_in', 'new_v_conv_w', 'new_v_conv_b', 'new_v_dt_bias', 'new_v_a_log', 'new_v_d_skip', 'new_v_ssm_norm_w', 'new_v_q_norm_w', 'new_v_kv_norm_w', 'new_v_w_uq', 'new_v_w_ukv', 'new_v_w_branch_ssm', 'new_v_w_branch_mla', 'new_v_w_out', 'new_v_norm_mlp_w', 'new_v_w_mlp_up', 'new_v_w_mlp_down', 'new_v_final_norm_w']
TWIN_LEAF_KINDS = {'loss': 'loss', 'grad_x': 'grad_x', 'grad_meta_tokens': 'grad_w', 'grad_norm_mix_w': 'grad_w', 'grad_w_in': 'grad_w', 'grad_conv_w': 'grad_w', 'grad_conv_b': 'grad_w', 'grad_dt_bias': 'grad_w', 'grad_a_log': 'grad_w', 'grad_d_skip': 'grad_w', 'grad_ssm_norm_w': 'grad_w', 'grad_q_norm_w': 'grad_w', 'grad_kv_norm_w': 'grad_w', 'grad_w_uq': 'grad_w', 'grad_w_ukv': 'grad_w', 'grad_w_branch_ssm': 'grad_w', 'grad_w_branch_mla': 'grad_w', 'grad_w_out': 'grad_w', 'grad_norm_mlp_w': 'grad_w', 'grad_w_mlp_up': 'grad_w', 'grad_w_mlp_down': 'grad_w', 'grad_final_norm_w': 'grad_w', 'delta_meta_tokens': 'delta_w', 'delta_norm_mix_w': 'delta_w', 'delta_w_in': 'delta_w', 'delta_conv_w': 'delta_w', 'delta_conv_b': 'delta_w', 'delta_dt_bias': 'delta_w', 'delta_a_log': 'delta_w', 'delta_d_skip': 'delta_w', 'delta_ssm_norm_w': 'delta_w', 'delta_q_norm_w': 'delta_w', 'delta_kv_norm_w': 'delta_w', 'delta_w_uq': 'delta_w', 'delta_w_ukv': 'delta_w', 'delta_w_branch_ssm': 'delta_w', 'delta_w_branch_mla': 'delta_w', 'delta_w_out': 'delta_w', 'delta_norm_mlp_w': 'delta_w', 'delta_w_mlp_up': 'delta_w', 'delta_w_mlp_down': 'delta_w', 'delta_final_norm_w': 'delta_w', 'new_m_meta_tokens': 'new_m', 'new_m_norm_mix_w': 'new_m', 'new_m_w_in': 'new_m', 'new_m_conv_w': 'new_m', 'new_m_conv_b': 'new_m', 'new_m_dt_bias': 'new_m', 'new_m_a_log': 'new_m', 'new_m_d_skip': 'new_m', 'new_m_ssm_norm_w': 'new_m', 'new_m_q_norm_w': 'new_m', 'new_m_kv_norm_w': 'new_m', 'new_m_w_uq': 'new_m', 'new_m_w_ukv': 'new_m', 'new_m_w_branch_ssm': 'new_m', 'new_m_w_branch_mla': 'new_m', 'new_m_w_out': 'new_m', 'new_m_norm_mlp_w': 'new_m', 'new_m_w_mlp_up': 'new_m', 'new_m_w_mlp_down': 'new_m', 'new_m_final_norm_w': 'new_m', 'new_v_meta_tokens': 'new_v', 'new_v_norm_mix_w': 'new_v', 'new_v_w_in': 'new_v', 'new_v_conv_w': 'new_v', 'new_v_conv_b': 'new_v', 'new_v_dt_bias': 'new_v', 'new_v_a_log': 'new_v', 'new_v_d_skip': 'new_v', 'new_v_ssm_norm_w': 'new_v', 'new_v_q_norm_w': 'new_v', 'new_v_kv_norm_w': 'new_v', 'new_v_w_uq': 'new_v', 'new_v_w_ukv': 'new_v', 'new_v_w_branch_ssm': 'new_v', 'new_v_w_branch_mla': 'new_v', 'new_v_w_out': 'new_v', 'new_v_norm_mlp_w': 'new_v', 'new_v_w_mlp_up': 'new_v', 'new_v_w_mlp_down': 'new_v', 'new_v_final_norm_w': 'new_v'}


def _forward(args):
    return _fwd_reference(*[args[k] for k in FWD_PARAMS])


def _output_shape():
    out = _jax.eval_shape(lambda: _forward(_fwd_setup_inputs(0)))
    return out.shape, out.dtype

N_MICROBATCH = 1
ADAM_LR = 0.001
ADAM_B1 = 0.9
ADAM_B2 = 0.999
ADAM_EPS = 1e-08
ADAM_WD = 0.01
ADAM_STEP = 10
PER_EXAMPLE_BATCH_AXIS = {'x': 0, 'loss_target': 0}
SHARED_INPUTS = []
_WEIGHT_DTYPES = {'meta_tokens': _jnp.float32, 'norm_mix_w': _jnp.float32, 'w_in': _jnp.float32, 'conv_w': _jnp.float32, 'conv_b': _jnp.float32, 'dt_bias': _jnp.float32, 'a_log': _jnp.float32, 'd_skip': _jnp.float32, 'ssm_norm_w': _jnp.float32, 'q_norm_w': _jnp.float32, 'kv_norm_w': _jnp.float32, 'w_uq': _jnp.float32, 'w_ukv': _jnp.float32, 'w_branch_ssm': _jnp.float32, 'w_branch_mla': _jnp.float32, 'w_out': _jnp.float32, 'norm_mlp_w': _jnp.float32, 'w_mlp_up': _jnp.float32, 'w_mlp_down': _jnp.float32, 'final_norm_w': _jnp.float32}
MOMENT_SCALE = {'meta_tokens': 2.728942e-03, 'norm_mix_w': 7.532652e-02, 'w_in': 2.657200e-02, 'conv_w': 2.885035e-02, 'conv_b': 3.986646e-02, 'dt_bias': 1.000839e-01, 'a_log': 1.297269e-01, 'd_skip': 1.628548e-01, 'ssm_norm_w': 3.374594e-02, 'q_norm_w': 1.015297e-02, 'kv_norm_w': 2.416259e-02, 'w_uq': 6.155350e-03, 'w_ukv': 8.685995e-03, 'w_branch_ssm': 4.693612e-02, 'w_branch_mla': 1.066514e-02, 'w_out': 9.485060e-02, 'norm_mlp_w': 1.037298e-01, 'w_mlp_up': 5.098847e-02, 'w_mlp_down': 1.883625e-01, 'final_norm_w': 3.212171e+01}


def _to_microbatches(a, axis):
    t = _jnp.moveaxis(a, axis, 0)
    t = t.reshape((N_MICROBATCH, t.shape[0] // N_MICROBATCH) + t.shape[1:])
    return _jnp.moveaxis(t, 1, axis + 1)


def setup_inputs(seed: int = 0) -> dict:
    inp = _fwd_setup_inputs(seed)
    key = _jax.random.fold_in(_jax.random.key(seed), 7919)
    shape, _ = _output_shape()
    out = dict(inp)
    out["loss_target"] = _jax.random.normal(_jax.random.fold_in(key, 0), shape, _jnp.float32)
    for i, name in enumerate(TWIN_WEIGHTS):
        w = inp[name].astype(_jnp.float32)
        if MOMENT_SCALE is None:
            s = _jnp.sqrt(_jnp.mean(_jnp.square(w)) + 1e-30)
        else:
            s = MOMENT_SCALE[name]
        km, kv = _jax.random.split(_jax.random.fold_in(key, i + 1))
        out[name] = w
        out["m_" + name] = s * _jax.random.normal(km, w.shape, _jnp.float32)
        out["v_" + name] = (s * s) * _jax.random.uniform(kv, w.shape, _jnp.float32, 0.5, 1.5)
    if N_MICROBATCH > 1:
        for name, axis in PER_EXAMPLE_BATCH_AXIS.items():
            out[name] = _to_microbatches(out[name], axis)
    return {'x': out['x'], 'meta_tokens': out['meta_tokens'], 'norm_mix_w': out['norm_mix_w'], 'w_in': out['w_in'], 'conv_w': out['conv_w'], 'conv_b': out['conv_b'], 'dt_bias': out['dt_bias'], 'a_log': out['a_log'], 'd_skip': out['d_skip'], 'ssm_norm_w': out['ssm_norm_w'], 'q_norm_w': out['q_norm_w'], 'kv_norm_w': out['kv_norm_w'], 'w_uq': out['w_uq'], 'w_ukv': out['w_ukv'], 'w_branch_ssm': out['w_branch_ssm'], 'w_branch_mla': out['w_branch_mla'], 'w_out': out['w_out'], 'norm_mlp_w': out['norm_mlp_w'], 'w_mlp_up': out['w_mlp_up'], 'w_mlp_down': out['w_mlp_down'], 'final_norm_w': out['final_norm_w'], 'loss_target': out['loss_target'], 'm_meta_tokens': out['m_meta_tokens'], 'm_norm_mix_w': out['m_norm_mix_w'], 'm_w_in': out['m_w_in'], 'm_conv_w': out['m_conv_w'], 'm_conv_b': out['m_conv_b'], 'm_dt_bias': out['m_dt_bias'], 'm_a_log': out['m_a_log'], 'm_d_skip': out['m_d_skip'], 'm_ssm_norm_w': out['m_ssm_norm_w'], 'm_q_norm_w': out['m_q_norm_w'], 'm_kv_norm_w': out['m_kv_norm_w'], 'm_w_uq': out['m_w_uq'], 'm_w_ukv': out['m_w_ukv'], 'm_w_branch_ssm': out['m_w_branch_ssm'], 'm_w_branch_mla': out['m_w_branch_mla'], 'm_w_out': out['m_w_out'], 'm_norm_mlp_w': out['m_norm_mlp_w'], 'm_w_mlp_up': out['m_w_mlp_up'], 'm_w_mlp_down': out['m_w_mlp_down'], 'm_final_norm_w': out['m_final_norm_w'], 'v_meta_tokens': out['v_meta_tokens'], 'v_norm_mix_w': out['v_norm_mix_w'], 'v_w_in': out['v_w_in'], 'v_conv_w': out['v_conv_w'], 'v_conv_b': out['v_conv_b'], 'v_dt_bias': out['v_dt_bias'], 'v_a_log': out['v_a_log'], 'v_d_skip': out['v_d_skip'], 'v_ssm_norm_w': out['v_ssm_norm_w'], 'v_q_norm_w': out['v_q_norm_w'], 'v_kv_norm_w': out['v_kv_norm_w'], 'v_w_uq': out['v_w_uq'], 'v_w_ukv': out['v_w_ukv'], 'v_w_branch_ssm': out['v_w_branch_ssm'], 'v_w_branch_mla': out['v_w_branch_mla'], 'v_w_out': out['v_w_out'], 'v_norm_mlp_w': out['v_norm_mlp_w'], 'v_w_mlp_up': out['v_w_mlp_up'], 'v_w_mlp_down': out['v_w_mlp_down'], 'v_final_norm_w': out['v_final_norm_w']}


def _loss(weights, diff, rest, loss_target):
    with _jax.named_scope("forward"):
        args = {**rest, TWIN_DIFF_INPUT: diff, **{k: w.astype(_WEIGHT_DTYPES[k]) for k, w in weights.items()}}
        y = _forward(args)
    with _jax.named_scope("loss_head"):
        err = _jnp.square(y.astype(_jnp.float32) - loss_target)
        return 0.5 * _jnp.sum(_jnp.mean(err, axis=-1)) if err.ndim else 0.5 * err


def _adamw(w, g, m, v):
    m = ADAM_B1 * m + (1.0 - ADAM_B1) * g
    v = ADAM_B2 * v + (1.0 - ADAM_B2) * _jnp.square(g)
    m_hat = m / (1.0 - ADAM_B1 ** ADAM_STEP)
    v_hat = v / (1.0 - ADAM_B2 ** ADAM_STEP)
    delta = -ADAM_LR * (m_hat / (_jnp.sqrt(v_hat) + ADAM_EPS) + ADAM_WD * w)
    return delta, m, v


def reference(x, meta_tokens, norm_mix_w, w_in, conv_w, conv_b, dt_bias, a_log, d_skip, ssm_norm_w, q_norm_w, kv_norm_w, w_uq, w_ukv, w_branch_ssm, w_branch_mla, w_out, norm_mlp_w, w_mlp_up, w_mlp_down, final_norm_w, loss_target, m_meta_tokens, m_norm_mix_w, m_w_in, m_conv_w, m_conv_b, m_dt_bias, m_a_log, m_d_skip, m_ssm_norm_w, m_q_norm_w, m_kv_norm_w, m_w_uq, m_w_ukv, m_w_branch_ssm, m_w_branch_mla, m_w_out, m_norm_mlp_w, m_w_mlp_up, m_w_mlp_down, m_final_norm_w, v_meta_tokens, v_norm_mix_w, v_w_in, v_conv_w, v_conv_b, v_dt_bias, v_a_log, v_d_skip, v_ssm_norm_w, v_q_norm_w, v_kv_norm_w, v_w_uq, v_w_ukv, v_w_branch_ssm, v_w_branch_mla, v_w_out, v_norm_mlp_w, v_w_mlp_up, v_w_mlp_down, v_final_norm_w):
    given = dict(x=x, meta_tokens=meta_tokens, norm_mix_w=norm_mix_w, w_in=w_in, conv_w=conv_w, conv_b=conv_b, dt_bias=dt_bias, a_log=a_log, d_skip=d_skip, ssm_norm_w=ssm_norm_w, q_norm_w=q_norm_w, kv_norm_w=kv_norm_w, w_uq=w_uq, w_ukv=w_ukv, w_branch_ssm=w_branch_ssm, w_branch_mla=w_branch_mla, w_out=w_out, norm_mlp_w=norm_mlp_w, w_mlp_up=w_mlp_up, w_mlp_down=w_mlp_down, final_norm_w=final_norm_w, loss_target=loss_target, m_meta_tokens=m_meta_tokens, m_norm_mix_w=m_norm_mix_w, m_w_in=m_w_in, m_conv_w=m_conv_w, m_conv_b=m_conv_b, m_dt_bias=m_dt_bias, m_a_log=m_a_log, m_d_skip=m_d_skip, m_ssm_norm_w=m_ssm_norm_w, m_q_norm_w=m_q_norm_w, m_kv_norm_w=m_kv_norm_w, m_w_uq=m_w_uq, m_w_ukv=m_w_ukv, m_w_branch_ssm=m_w_branch_ssm, m_w_branch_mla=m_w_branch_mla, m_w_out=m_w_out, m_norm_mlp_w=m_norm_mlp_w, m_w_mlp_up=m_w_mlp_up, m_w_mlp_down=m_w_mlp_down, m_final_norm_w=m_final_norm_w, v_meta_tokens=v_meta_tokens, v_norm_mix_w=v_norm_mix_w, v_w_in=v_w_in, v_conv_w=v_conv_w, v_conv_b=v_conv_b, v_dt_bias=v_dt_bias, v_a_log=v_a_log, v_d_skip=v_d_skip, v_ssm_norm_w=v_ssm_norm_w, v_q_norm_w=v_q_norm_w, v_kv_norm_w=v_kv_norm_w, v_w_uq=v_w_uq, v_w_ukv=v_w_ukv, v_w_branch_ssm=v_w_branch_ssm, v_w_branch_mla=v_w_branch_mla, v_w_out=v_w_out, v_norm_mlp_w=v_norm_mlp_w, v_w_mlp_up=v_w_mlp_up, v_w_mlp_down=v_w_mlp_down, v_final_norm_w=v_final_norm_w)
    weights = {n: given[n] for n in TWIN_WEIGHTS}
    shared = {n: given[n] for n in SHARED_INPUTS}
    per_example = {n: given[n] for n in ['x']}
    grad_fn = _jax.value_and_grad(_loss, argnums=(0, 1))

    def one_microbatch(ex, loss_target):
        ex = dict(ex)
        diff = ex.pop(TWIN_DIFF_INPUT)
        return grad_fn(weights, diff, {**shared, **ex}, loss_target)

    if N_MICROBATCH == 1:
        loss, (grad_w, grad_x) = one_microbatch(per_example, given["loss_target"])
    else:
        def body(carry, xs):
            loss_sum, grad_sum = carry
            l_k, (gw_k, gx_k) = one_microbatch(xs[0], xs[1])
            with _jax.named_scope("update"):
                return (loss_sum + l_k, _jax.tree.map(_jnp.add, grad_sum, gw_k)), gx_k

        init = (_jnp.zeros((), _jnp.float32), _jax.tree.map(_jnp.zeros_like, weights))
        (loss, grad_w), grad_x = _jax.lax.scan(body, init, (per_example, given["loss_target"]))
    with _jax.named_scope("update"):
        delta_w, new_m, new_v = {}, {}, {}
        for n in TWIN_WEIGHTS:
            delta_w[n], new_m[n], new_v[n] = _adamw(weights[n], grad_w[n], given["m_" + n], given["v_" + n])
    return (loss, grad_x, *[grad_w[n] for n in TWIN_WEIGHTS], *[delta_w[n] for n in TWIN_WEIGHTS],
            *[new_m[n] for n in TWIN_WEIGHTS], *[new_v[n] for n in TWIN_WEIGHTS])
```

```python
import functools
import math
from typing import NamedTuple

import numpy as np
import jax
import jax.numpy as jnp
from jax import lax
from jax.experimental import pallas as pl
from jax.experimental.pallas import tpu as pltpu

F32 = jnp.float32
BF16 = jnp.bfloat16
HI = lax.Precision.HIGHEST
EPS = 1e-6
ROPE_THETA = 10000.0
LANE = 128
VMEM_LIMIT = 56 * 1024 * 1024
MASK_VALUE = -1e30
ADAM_LR, ADAM_B1, ADAM_B2, ADAM_EPS, ADAM_WD, ADAM_STEP = 0.001, 0.9, 0.999, 1e-08, 0.01, 10
MESH = pl.DeviceIdType.MESH


class Cfg(NamedTuple):
    d: int = 1024
    seq: int = 2048
    bsz: int = 2
    n_meta: int = 16
    inner: int = 2048
    hd: int = 64
    groups: int = 4
    state: int = 128
    convk: int = 4
    chunk: int = 128
    mh: int = 8
    ql: int = 512
    kvl: int = 256
    nope: int = 128
    rope: int = 64
    vd: int = 128
    ff: int = 4096

    @property
    def heads(self): return self.inner // self.hd
    @property
    def gw(self): return self.inner // self.groups
    @property
    def conv_dim(self): return self.inner + 2 * self.groups * self.state
    @property
    def pad(self): return self.chunk - self.n_meta
    @property
    def lp(self): return self.chunk + self.seq
    @property
    def t(self): return self.bsz * self.lp
    @property
    def nchunks(self): return self.lp // self.chunk
    @property
    def sw(self): return self.ql + self.kvl + 2 * LANE
    @property
    def kt(self): return (self.ql + self.kvl) // LANE
    @property
    def dtt(self): return self.kt + 1
    @property
    def qw(self): return self.mh * 2 * LANE
    @property
    def in_splits(self):
        return [self.inner, self.conv_dim, self.heads, self.ql, self.kvl, self.rope, self.d, self.d]


CFG = Cfg()


def _pick(dim, pref, mult):
    best = None
    for t in range(mult, min(dim, pref) + 1, mult):
        if dim % t == 0:
            best = t
    return best if best is not None else dim


def _cp(**kw):
    return pltpu.CompilerParams(vmem_limit_bytes=VMEM_LIMIT, **kw)


def _sds(shape, dtype):
    return jax.ShapeDtypeStruct(tuple(shape), dtype)


def _silu(x):
    return x * jax.nn.sigmoid(x)


def _dsilu(x):
    s = jax.nn.sigmoid(x)
    return s * (1.0 + x * (1.0 - s))


def matmul(a, b, *, ta=False, tb=False, out_dtype=F32, add=None, name, tm=None, tn=None, tk=None):
    if ta:
        k_dim, m_dim = a.shape
    else:
        m_dim, k_dim = a.shape
    if tb:
        n_dim, k2 = b.shape
    else:
        k2, n_dim = b.shape
    assert k_dim == k2, (a.shape, b.shape, ta, tb)
    if ta:
        tm = tm or _pick(m_dim, 1024, LANE)
        tk = tk or _pick(k_dim, 1088, 16)
        tn = tn or _pick(n_dim, 1024, LANE)
    else:
        tm = tm or _pick(m_dim, 1088, 16)
        tk = tk or _pick(k_dim, 1024 if a.dtype == F32 else 2048, LANE)
        tn = tn or _pick(n_dim, 512, LANE)
    nm, nn, nk = m_dim // tm, n_dim // tn, k_dim // tk
    dn = (((0 if ta else 1,), (1 if tb else 0,)), ((), ()))
    has_add = add is not None

    def body(*refs):
        if has_add:
            a_ref, b_ref, add_ref, o_ref = refs[:4]
            scr = refs[4:]
        else:
            a_ref, b_ref, o_ref = refs[:3]
            add_ref = None
            scr = refs[3:]
        p = lax.dot_general(a_ref[...].astype(BF16), b_ref[...].astype(BF16), dn, preferred_element_type=F32)

        def finish(r):
            if has_add:
                r = r + add_ref[...].astype(F32)
            o_ref[...] = r.astype(out_dtype)

        if nk == 1:
            finish(p)
        else:
            acc = scr[0]
            k = pl.program_id(2)

            @pl.when(k == 0)
            def _():
                acc[...] = p

            @pl.when(k > 0)
            def _():
                acc[...] += p

            @pl.when(k == nk - 1)
            def _():
                finish(acc[...])

    a_spec = pl.BlockSpec((tk, tm), lambda i, j, k: (k, i)) if ta else pl.BlockSpec((tm, tk), lambda i, j, k: (i, k))
    b_spec = pl.BlockSpec((tn, tk), lambda i, j, k: (j, k)) if tb else pl.BlockSpec((tk, tn), lambda i, j, k: (k, j))
    o_spec = pl.BlockSpec((tm, tn), lambda i, j, k: (i, j))
    in_specs = [a_spec, b_spec] + ([o_spec] if has_add else [])
    args = [a, b] + ([add] if has_add else [])
    return pl.pallas_call(
        body, name=name, grid=(nm, nn, nk), in_specs=in_specs, out_specs=o_spec,
        out_shape=_sds((m_dim, n_dim), out_dtype),
        scratch_shapes=[pltpu.VMEM((tm, tn), F32)] if nk > 1 else [],
        compiler_params=_cp(dimension_semantics=("parallel", "parallel", "arbitrary")),
    )(*args)


def rmsnorm_fwd(x, w, *, cw=None, ci=0, name):
    t = x.shape[0]
    cw = cw or x.shape[1]
    tr = _pick(t, 544, 16)

    def body(x_ref, w_ref, o_ref):
        xv = x_ref[...].astype(F32)
        r = lax.rsqrt(jnp.mean(xv * xv, axis=-1, keepdims=True) + EPS)
        o_ref[...] = (xv * r * w_ref[...]).astype(BF16)

    return pl.pallas_call(
        body, name=name, grid=(t // tr,),
        in_specs=[pl.BlockSpec((tr, cw), lambda i: (i, ci)), pl.BlockSpec((1, cw), lambda i: (0, 0))],
        out_specs=pl.BlockSpec((tr, cw), lambda i: (i, 0)),
        out_shape=_sds((t, cw), BF16), compiler_params=_cp(),
    )(x, w.reshape(1, cw))


def rmsnorm_bwd(dy, x, w, *, cw=None, ci=0, res=None, out_dtype=F32, name):
    t = x.shape[0]
    cw = cw or x.shape[1]
    tr = _pick(t, 544, 16)
    has_res = res is not None

    def body(*refs):
        if has_res:
            dy_ref, x_ref, w_ref, res_ref, dx_ref, dw_ref = refs
        else:
            dy_ref, x_ref, w_ref, dx_ref, dw_ref = refs
        xv = x_ref[...].astype(F32)
        dyv = dy_ref[...].astype(F32)
        r = lax.rsqrt(jnp.mean(xv * xv, axis=-1, keepdims=True) + EPS)
        xh = xv * r
        g = dyv * w_ref[...]
        dx = r * (g - xh * jnp.mean(g * xh, axis=-1, keepdims=True))
        if has_res:
            dx = dx + res_ref[...]
        dx_ref[...] = dx.astype(out_dtype)

        @pl.when(pl.program_id(0) == 0)
        def _():
            dw_ref[...] = jnp.zeros_like(dw_ref)

        dw_ref[...] += jnp.sum(dyv * xh, axis=0, keepdims=True)

    row = pl.BlockSpec((tr, cw), lambda i: (i, 0))
    in_specs = [row, pl.BlockSpec((tr, cw), lambda i: (i, ci)), pl.BlockSpec((1, cw), lambda i: (0, 0))]
    args = [dy, x, w.reshape(1, cw)]
    if has_res:
        in_specs.append(row)
        args.append(res)
    dx, dw = pl.pallas_call(
        body, name=name, grid=(t // tr,), in_specs=in_specs,
        out_specs=[row, pl.BlockSpec((1, cw), lambda i: (0, 0))],
        out_shape=[_sds((t, cw), out_dtype), _sds((1, cw), F32)], compiler_params=_cp(),
    )(*args)
    return dx, dw[0]


def _shift_down(x, s, rows):
    if s == 0:
        return x
    return jnp.where(rows >= s, pltpu.roll(x, s, 0), 0.0)


def _shift_up(x, s, rows):
    if s == 0:
        return x
    n = x.shape[0]
    return jnp.where(rows < n - s, pltpu.roll(x, n - s, 0), 0.0)


def _conv_pre(x, w_ref, b_ref, rows, kk):
    pre = b_ref[...] + jnp.zeros_like(x)
    for k in range(kk):
        pre = pre + w_ref[k:k + 1, :] * _shift_down(x, kk - 1 - k, rows)
    return pre


def conv_fwd(cfg, xbc, w, b, *, name):
    lp, cd, kk = cfg.lp, cfg.conv_dim, cfg.convk
    cb = _pick(cd, 512, LANE)

    def body(x_ref, w_ref, b_ref, o_ref):
        x = x_ref[...]
        rows = lax.broadcasted_iota(jnp.int32, x.shape, 0)
        o_ref[...] = _silu(_conv_pre(x, w_ref, b_ref, rows, kk))

    blk = pl.BlockSpec((lp, cb), lambda j, bb: (bb, j))
    return pl.pallas_call(
        body, name=name, grid=(cd // cb, cfg.bsz),
        in_specs=[blk, pl.BlockSpec((kk, cb), lambda j, bb: (0, j)), pl.BlockSpec((1, cb), lambda j, bb: (0, j))],
        out_specs=blk, out_shape=_sds((cfg.t, cd), F32), compiler_params=_cp(),
    )(xbc, w, b.reshape(1, cd))


def conv_bwd(cfg, xbc, w, b, dxc, *, name):
    lp, cd, kk = cfg.lp, cfg.conv_dim, cfg.convk
    cb = _pick(cd, 512, LANE)

    def body(x_ref, w_ref, b_ref, d_ref, dx_ref, dw_ref, db_ref):
        x = x_ref[...]
        rows = lax.broadcasted_iota(jnp.int32, x.shape, 0)
        pre = _conv_pre(x, w_ref, b_ref, rows, kk)
        dpre = d_ref[...] * _dsilu(pre)
        dx = jnp.zeros_like(x)
        dws = []
        for k in range(kk):
            s = kk - 1 - k
            dx = dx + w_ref[k:k + 1, :] * _shift_up(dpre, s, rows)
            dws.append(jnp.sum(dpre * _shift_down(x, s, rows), axis=0, keepdims=True))
        dx_ref[...] = dx.astype(BF16)

        @pl.when(pl.program_id(1) == 0)
        def _():
            dw_ref[...] = jnp.zeros_like(dw_ref)
            db_ref[...] = jnp.zeros_like(db_ref)

        for k in range(kk):
            dw_ref[k:k + 1, :] += dws[k]
        db_ref[...] += jnp.sum(dpre, axis=0, keepdims=True)

    blk = pl.BlockSpec((lp, cb), lambda j, bb: (bb, j))
    wsp = pl.BlockSpec((kk, cb), lambda j, bb: (0, j))
    bsp = pl.BlockSpec((1, cb), lambda j, bb: (0, j))
    dx, dw, db = pl.pallas_call(
        body, name=name, grid=(cd // cb, cfg.bsz),
        in_specs=[blk, wsp, bsp, blk], out_specs=[blk, wsp, bsp],
        out_shape=[_sds((cfg.t, cd), BF16), _sds((kk, cd), F32), _sds((1, cd), F32)], compiler_params=_cp(),
    )(xbc, w, b.reshape(1, cd), dxc)
    return dx, dw, db[0]


def _softplus(x):
    return jnp.maximum(x, 0.0) + jnp.log(1.0 + jnp.exp(-jnp.abs(x)))


def _ssd_consts(cfg):
    q = cfg.chunk
    i0 = np.arange(q)[:, None]
    i1 = np.arange(q)[None, :]
    ltri = (i1 <= i0).astype(np.float32)
    rexp = np.zeros((LANE, cfg.inner), np.float32)
    for h in range(cfg.heads):
        rexp[h, h * cfg.hd:(h + 1) * cfg.hd] = 1.0
    return jnp.asarray(ltri), jnp.asarray(rexp)


def _ssd_chunk_common(cfg, raw, bias, avec, c_idx, ltri, rexp):
    q = cfg.chunk
    rows = lax.broadcasted_iota(jnp.int32, (q, LANE), 0)
    live = jnp.logical_or(c_idx > 0, rows >= cfg.pad)
    pre = raw + bias
    dt = jnp.where(live, _softplus(pre), 0.0)
    adt = dt * avec
    cs = jnp.dot(ltri, adt, precision=HI, preferred_element_type=F32)
    cs_t = cs.T
    cs_last = cs[q - 1:q, :]
    e_in = jnp.exp(cs)
    w0 = jnp.exp(cs_last - cs)
    decay = jnp.exp(cs_last)
    ex = functools.partial(jnp.dot, precision=HI, preferred_element_type=F32)
    return dict(live=live, pre=pre, dt=dt, adt=adt, cs=cs, cs_t=cs_t, e_in=e_in, w0=w0, decay=decay,
                DT=ex(dt, rexp), E=ex(e_in, rexp), W0=ex(w0, rexp),
                DEC=ex(jnp.broadcast_to(decay, (8, LANE)), rexp)[0:1, :])


def _tri_masks(q):
    r = lax.broadcasted_iota(jnp.int32, (q, q), 0)
    c = lax.broadcasted_iota(jnp.int32, (q, q), 1)
    return c <= r, r <= c


def _head_l(cq, h, tri, tri_t):
    col = cq["cs"][:, h:h + 1]
    row = cq["cs_t"][h:h + 1, :]
    lmat = jnp.where(tri, jnp.exp(jnp.minimum(col - row, 0.0)), 0.0)
    lmat_t = jnp.where(tri_t, jnp.exp(jnp.minimum(row - col, 0.0)), 0.0)
    return lmat, lmat_t


def _nt(a, b):
    return lax.dot_general(a, b, (((1,), (1,)), ((), ())), preferred_element_type=F32)


def _tn(a, b):
    return lax.dot_general(a, b, (((0,), (0,)), ((), ())), preferred_element_type=F32)


def _nn(a, b):
    return jnp.dot(a, b, preferred_element_type=F32)


def ssd_fwd(cfg, xc, small, dt_bias, avec, dexp, *, name):
    q, inner, st, gw, g_n = cfg.chunk, cfg.inner, cfg.state, cfg.gw, cfg.groups
    nc = cfg.nchunks
    ltri, rexp = _ssd_consts(cfg)
    hpt = LANE // cfg.hd
    tiles_per_group = gw // LANE

    def body(x_ref, b_ref, c_ref, dt_ref, bias_ref, a_ref, d_ref, ltri_ref, rexp_ref, y_ref, sin_ref, s_scr):
        c_idx = pl.program_id(1)

        @pl.when(c_idx == 0)
        def _():
            s_scr[...] = jnp.zeros_like(s_scr)

        ltri_v = ltri_ref[...]
        tri, tri_t = _tri_masks(q)
        cq = _ssd_chunk_common(cfg, dt_ref[...], bias_ref[...], a_ref[...], c_idx, ltri_v, rexp_ref[...])
        xs = x_ref[...]
        xdt = (xs * cq["DT"]).astype(BF16)
        xw = (xs * cq["DT"] * cq["W0"]).astype(BF16)
        s_in = s_scr[...]
        sin_ref[0] = s_in
        lane = lax.broadcasted_iota(jnp.int32, (q, LANE), 1)
        for g in range(g_n):
            bg = b_ref[:, g * st:(g + 1) * st].astype(BF16)
            cg = c_ref[:, g * st:(g + 1) * st].astype(BF16)
            gmat = _nt(cg, bg)
            gs = slice(g * gw, (g + 1) * gw)
            y0 = _nn(cg, s_in[:, gs].astype(BF16))
            for tt in range(tiles_per_group):
                tile = g * tiles_per_group + tt
                ts = slice(tile * LANE, (tile + 1) * LANE)
                xt = xdt[:, ts]
                yd = None
                for hh in range(hpt):
                    h = tile * hpt + hh
                    lmat, _ = _head_l(cq, h, tri, tri_t)
                    part = _nn((gmat * lmat).astype(BF16), xt)
                    if yd is None:
                        yd = part
                    else:
                        yd = jnp.where(lane < (hh * cfg.hd), yd, part)
                y_ref[:, ts] = yd + y0[:, tt * LANE:(tt + 1) * LANE] * cq["E"][:, ts] + xs[:, ts] * d_ref[:, ts]
            s_scr[:, gs] = s_in[:, gs] * cq["DEC"][:, gs] + _tn(bg, xw[:, gs])

    def rowblk(width, col):
        return pl.BlockSpec((q, width), lambda b, c: (b * nc + c, col))

    def const(shape):
        return pl.BlockSpec(shape, lambda b, c: (0, 0))

    y, sin = pl.pallas_call(
        body, name=name, grid=(cfg.bsz, nc),
        in_specs=[rowblk(inner, 0),
                  pl.BlockSpec((q, g_n * st), lambda b, c: (b * nc + c, inner // (g_n * st))),
                  pl.BlockSpec((q, g_n * st), lambda b, c: (b * nc + c, inner // (g_n * st) + 1)),
                  rowblk(LANE, cfg.dtt), const((1, LANE)), const((1, LANE)), const((1, inner)),
                  const((q, q)), const((LANE, inner))],
        out_specs=[rowblk(inner, 0), pl.BlockSpec((1, st, inner), lambda b, c: (b * nc + c, 0, 0))],
        out_shape=[_sds((cfg.t, inner), F32), _sds((cfg.bsz * nc, st, inner), F32)],
        scratch_shapes=[pltpu.VMEM((st, inner), F32)], compiler_params=_cp(),
    )(xc, xc, xc, small, dt_bias, avec, dexp, ltri, rexp)
    return y, sin


def ssd_bwd(cfg, xc, small, dt_bias, avec, dexp, sin, dy, *, name):
    q, inner, st, gw, g_n = cfg.chunk, cfg.inner, cfg.state, cfg.gw, cfg.groups
    nc = cfg.nchunks
    ltri, rexp = _ssd_consts(cfg)
    rexp_t = rexp.T
    hpt = LANE // cfg.hd
    tiles_per_group = gw // LANE
    bcw = g_n * st

    def body(x_ref, b_ref, c_ref, dt_ref, bias_ref, a_ref, d_ref, ltri_ref, rexp_ref, rexpt_ref, sin_ref, dy_ref,
             dx_ref, db_ref, dc_ref, ddt_ref, dd_ref, da_ref, dbias_ref, ds_scr):
        step = pl.program_id(1)
        c_idx = nc - 1 - step

        @pl.when(step == 0)
        def _():
            ds_scr[...] = jnp.zeros_like(ds_scr)

        @pl.when(jnp.logical_and(step == 0, pl.program_id(0) == 0))
        def _():
            dd_ref[...] = jnp.zeros_like(dd_ref)
            da_ref[...] = jnp.zeros_like(da_ref)
            dbias_ref[...] = jnp.zeros_like(dbias_ref)

        ltri_v = ltri_ref[...]
        tri, tri_t = _tri_masks(q)
        red = functools.partial(jnp.dot, precision=HI, preferred_element_type=F32)
        rexpt = rexpt_ref[...]
        cq = _ssd_chunk_common(cfg, dt_ref[...], bias_ref[...], a_ref[...], c_idx, ltri_v, rexp_ref[...])
        xs = x_ref[...]
        dyv = dy_ref[...]
        s_in = sin_ref[0]
        d_s = ds_scr[...]
        xdt_f = xs * cq["DT"]
        xdt = xdt_f.astype(BF16)
        xw_f = xdt_f * cq["W0"]
        xw = xw_f.astype(BF16)
        lane = lax.broadcasted_iota(jnp.int32, (q, LANE), 1)
        sub = lax.broadcasted_iota(jnp.int32, (LANE, q), 0)

        dd_ref[...] += jnp.sum(dyv * xs, axis=0, keepdims=True)
        dy0 = dyv * cq["E"]
        dcs = jnp.zeros((q, LANE), F32)
        dcs_t = jnp.zeros((LANE, q), F32)
        for g in range(g_n):
            bg_f = b_ref[:, g * st:(g + 1) * st]
            cg_f = c_ref[:, g * st:(g + 1) * st]
            bg = bg_f.astype(BF16)
            cg = cg_f.astype(BF16)
            gs = slice(g * gw, (g + 1) * gw)
            gmat = _nt(cg, bg)
            gmat_t = _nt(bg, cg)
            sing = s_in[:, gs].astype(BF16)
            dsg = d_s[:, gs].astype(BF16)
            y0 = _nn(cg, sing)
            dxw = _nn(bg, dsg)
            d_bg = _nt(xw[:, gs], dsg)
            d_cg = _nt(dy0[:, gs].astype(BF16), sing)
            ds_in_g = _tn(cg, dy0[:, gs].astype(BF16))
            dg = jnp.zeros((q, q), F32)
            dxdt_g = []
            for tt in range(tiles_per_group):
                tile = g * tiles_per_group + tt
                ts = slice(tile * LANE, (tile + 1) * LANE)
                xt = xdt[:, ts]
                dyt = dyv[:, ts]
                dxdt_t = None
                for hh in range(hpt):
                    h = tile * hpt + hh
                    lmat, lmat_t = _head_l(cq, h, tri, tri_t)
                    inhead = jnp.logical_and(lane >= hh * cfg.hd, lane < (hh + 1) * cfg.hd)
                    dyh = jnp.where(inhead, dyt, 0.0).astype(BF16)
                    dm = _nt(dyh, xt)
                    dg = dg + dm * lmat
                    qm = dm * gmat * lmat
                    rs = jnp.sum(qm, axis=1, keepdims=True)
                    csum = jnp.sum(qm, axis=0, keepdims=True)
                    dcs = dcs + jnp.where(lane == h, rs, 0.0)
                    dcs_t = dcs_t + jnp.where(sub == h, csum, 0.0)
                    part = _nn((gmat_t * lmat_t).astype(BF16), dyh)
                    dxdt_t = part if dxdt_t is None else dxdt_t + part
                dxdt_g.append(dxdt_t)
            dxdt_diag = jnp.concatenate(dxdt_g, axis=1) if len(dxdt_g) > 1 else dxdt_g[0]
            dgb = dg.astype(BF16)
            d_cg = d_cg + _nn(dgb, bg)
            d_bg = d_bg + _tn(dgb, cg)
            db_ref[:, g * st:(g + 1) * st] = d_bg
            dc_ref[:, g * st:(g + 1) * st] = d_cg
            dxdt = dxdt_diag + dxw * cq["W0"][:, gs]
            dx_ref[:, gs] = dyv[:, gs] * d_ref[:, gs] + dxdt * cq["DT"][:, gs]
            rt = rexpt[gs, :]
            dcs = dcs + red(dyv[:, gs] * y0 * cq["E"][:, gs], rt)
            r_w = red(dxw * xw_f[:, gs], rt)
            dcs = dcs - r_w
            dcs_last_g = jnp.sum(r_w, axis=0, keepdims=True)
            ddec = red(jnp.broadcast_to(jnp.sum(d_s[:, gs] * s_in[:, gs], axis=0, keepdims=True), (8, gw)), rt)[0:1, :]
            dcs_last_g = dcs_last_g + ddec * cq["decay"]
            dcs = dcs + jnp.where(lax.broadcasted_iota(jnp.int32, (q, LANE), 0) == q - 1, dcs_last_g, 0.0)
            ddt_part = red(dxdt * xs[:, gs], rt)
            if g == 0:
                ddt = ddt_part
            else:
                ddt = ddt + ddt_part
            ds_scr[:, gs] = d_s[:, gs] * cq["DEC"][:, gs] + ds_in_g
        dcs = dcs - dcs_t.T
        dadt = lax.dot_general(ltri_v, dcs, (((0,), (0,)), ((), ())), precision=HI,
                               preferred_element_type=F32)
        ddt = ddt + dadt * a_ref[...]
        da_ref[...] += jnp.sum(dadt * cq["dt"], axis=0, keepdims=True)
        draw = jnp.where(cq["live"], ddt * jax.nn.sigmoid(cq["pre"]), 0.0)
        ddt_ref[...] = draw
        dbias_ref[...] += jnp.sum(draw, axis=0, keepdims=True)

    def rowblk(width, col):
        return pl.BlockSpec((q, width), lambda b, s: (b * nc + nc - 1 - s, col))

    def const(shape):
        return pl.BlockSpec(shape, lambda b, s: (0, 0))

    bcol = inner // bcw
    outs = pl.pallas_call(
        body, name=name, grid=(cfg.bsz, nc),
        in_specs=[rowblk(inner, 0), rowblk(bcw, bcol), rowblk(bcw, bcol + 1), rowblk(LANE, cfg.dtt),
                  const((1, LANE)), const((1, LANE)), const((1, inner)), const((q, q)), const((LANE, inner)),
                  const((inner, LANE)),
                  pl.BlockSpec((1, st, inner), lambda b, s: (b * nc + nc - 1 - s, 0, 0)), rowblk(inner, 0)],
        out_specs=[rowblk(inner, 0), rowblk(bcw, 0), rowblk(bcw, 0), rowblk(LANE, 0),
                   const((1, inner)), const((1, LANE)), const((1, LANE))],
        out_shape=[_sds((cfg.t, inner), F32), _sds((cfg.t, bcw), F32), _sds((cfg.t, bcw), F32),
                   _sds((cfg.t, LANE), F32), _sds((1, inner), F32), _sds((1, LANE), F32), _sds((1, LANE), F32)],
        scratch_shapes=[pltpu.VMEM((st, inner), F32)], compiler_params=_cp(),
    )(xc, xc, xc, small, dt_bias, avec, dexp, ltri, rexp, rexp_t, sin, dy)
    return outs


def tail_fwd(cfg, y, z, w, *, name):
    t, inner, gw = cfg.t, cfg.inner, cfg.gw
    tr = _pick(t, 272, 16)

    def body(y_ref, z_ref, w_ref, o_ref):
        for g in range(cfg.groups):
            gs = slice(g * gw, (g + 1) * gw)
            yg = y_ref[:, gs] * _silu(z_ref[:, gs])
            r = lax.rsqrt(jnp.mean(yg * yg, axis=-1, keepdims=True) + EPS)
            o_ref[:, gs] = (yg * r * w_ref[:, gs]).astype(BF16)

    row = pl.BlockSpec((tr, inner), lambda i: (i, 0))
    return pl.pallas_call(
        body, name=name, grid=(t // tr,), in_specs=[row, row, pl.BlockSpec((1, inner), lambda i: (0, 0))],
        out_specs=row, out_shape=_sds((t, inner), BF16), compiler_params=_cp(),
    )(y, z, w.reshape(1, inner))


def tail_bwd(cfg, do, y, z, w, *, name):
    t, inner, gw = cfg.t, cfg.inner, cfg.gw
    tr = _pick(t, 272, 16)

    def body(do_ref, y_ref, z_ref, w_ref, dy_ref, dz_ref, dw_ref):
        @pl.when(pl.program_id(0) == 0)
        def _():
            dw_ref[...] = jnp.zeros_like(dw_ref)

        for g in range(cfg.groups):
            gs = slice(g * gw, (g + 1) * gw)
            yv = y_ref[:, gs]
            zv = z_ref[:, gs]
            dov = do_ref[:, gs]
            sz = _silu(zv)
            yg = yv * sz
            r = lax.rsqrt(jnp.mean(yg * yg, axis=-1, keepdims=True) + EPS)
            xh = yg * r
            gg = dov * w_ref[:, gs]
            dyg = r * (gg - xh * jnp.mean(gg * xh, axis=-1, keepdims=True))
            dw_ref[:, gs] += jnp.sum(dov * xh, axis=0, keepdims=True)
            dy_ref[:, gs] = dyg * sz
            dz_ref[:, gs] = (dyg * yv * _dsilu(zv)).astype(BF16)

    row = pl.BlockSpec((tr, inner), lambda i: (i, 0))
    vec = pl.BlockSpec((1, inner), lambda i: (0, 0))
    dy, dz, dw = pl.pallas_call(
        body, name=name, grid=(t // tr,), in_specs=[row, row, row, vec], out_specs=[row, row, vec],
        out_shape=[_sds((t, inner), F32), _sds((t, inner), BF16), _sds((1, inner), F32)], compiler_params=_cp(),
    )(do, y, z, w.reshape(1, inner))
    return dy, dz, dw[0]


def rope_tables(cfg):
    half = cfg.rope // 2
    pos = np.maximum(np.arange(cfg.lp) - cfg.pad, 0).astype(np.float32)
    inv = ROPE_THETA ** (-jnp.arange(0, cfg.rope, 2, dtype=F32) / cfg.rope)
    ang = jnp.asarray(pos)[:, None] * inv[None, :]
    cos, sin = jnp.cos(ang), jnp.sin(ang)
    zero = jnp.zeros((cfg.lp, LANE - 2 * half), F32)
    zh = jnp.zeros((cfg.lp, half), F32)
    ctab = jnp.concatenate([cos, cos, zero], axis=1)
    s1 = jnp.concatenate([-sin, zh, zero], axis=1)
    s2 = jnp.concatenate([zh, sin, zero], axis=1)
    return ctab, s1, s2


def _rope(x, c, s1, s2, half):
    return x * c + pltpu.roll(x, LANE - half, 1) * s1 + pltpu.roll(x, half, 1) * s2


def _rope_t(dy, c, s1, s2, half):
    return dy * c + pltpu.roll(dy * s1, half, 1) + pltpu.roll(dy * s2, LANE - half, 1)


def rope_fwd(cfg, qf, small, tabs, *, name):
    t, qw, lp = cfg.t, cfg.qw, cfg.lp
    tr = _pick(lp, 544, 16)
    nrb = lp // tr
    half = cfg.rope // 2

    def body(q_ref, k_ref, c_ref, s1_ref, s2_ref, qo_ref, ko_ref):
        c, s1, s2 = c_ref[...], s1_ref[...], s2_ref[...]
        for h in range(cfg.mh):
            a = h * 2 * LANE
            qo_ref[:, a:a + LANE] = q_ref[:, a:a + LANE].astype(BF16)
            qo_ref[:, a + LANE:a + 2 * LANE] = _rope(q_ref[:, a + LANE:a + 2 * LANE], c, s1, s2, half).astype(BF16)
        ko_ref[...] = _rope(k_ref[...], c, s1, s2, half).astype(BF16)

    tab = pl.BlockSpec((tr, LANE), lambda i: (i % nrb, 0))
    return pl.pallas_call(
        body, name=name, grid=(t // tr,),
        in_specs=[pl.BlockSpec((tr, qw), lambda i: (i, 0)), pl.BlockSpec((tr, LANE), lambda i: (i, cfg.kt)), tab, tab, tab],
        out_specs=[pl.BlockSpec((tr, qw), lambda i: (i, 0)), pl.BlockSpec((tr, LANE), lambda i: (i, 0))],
        out_shape=[_sds((t, qw), BF16), _sds((t, LANE), BF16)], compiler_params=_cp(),
    )(qf, small, *tabs)


def rope_bwd(cfg, dq, dkpe, tabs, *, name):
    t, qw, lp = cfg.t, cfg.qw, cfg.lp
    tr = _pick(lp, 544, 16)
    nrb = lp // tr
    half = cfg.rope // 2

    def body(dq_ref, dk_ref, c_ref, s1_ref, s2_ref, qo_ref, ko_ref):
        c, s1, s2 = c_ref[...], s1_ref[...], s2_ref[...]
        for h in range(cfg.mh):
            a = h * 2 * LANE
            qo_ref[:, a:a + LANE] = dq_ref[:, a:a + LANE].astype(BF16)
            qo_ref[:, a + LANE:a + 2 * LANE] = _rope_t(dq_ref[:, a + LANE:a + 2 * LANE], c, s1, s2, half).astype(BF16)
        dk = dk_ref[0]
        for h in range(1, cfg.mh):
            dk = dk + dk_ref[h]
        ko_ref[...] = _rope_t(dk, c, s1, s2, half)

    tab = pl.BlockSpec((tr, LANE), lambda i: (i % nrb, 0))
    return pl.pallas_call(
        body, name=name, grid=(t // tr,),
        in_specs=[pl.BlockSpec((tr, qw), lambda i: (i, 0)), pl.BlockSpec((cfg.mh, tr, LANE), lambda i: (0, i, 0)),
                  tab, tab, tab],
        out_specs=[pl.BlockSpec((tr, qw), lambda i: (i, 0)), pl.BlockSpec((tr, LANE), lambda i: (i, 0))],
        out_shape=[_sds((t, qw), BF16), _sds((t, LANE), F32)], compiler_params=_cp(),
    )(dq, dkpe, *tabs)


def _q_blocks(cfg):
    bounds = [0, cfg.chunk] + list(range(cfg.chunk + 256, cfg.lp + 1, 256))
    assert bounds[-1] == cfg.lp, "SEQ must be a multiple of 256"
    return list(zip(bounds[:-1], bounds[1:]))


def _attn_mask(cfg, qs, qe):
    rows = qs + lax.broadcasted_iota(jnp.int32, (qe - qs, qe), 0)
    cols = lax.broadcasted_iota(jnp.int32, (qe - qs, qe), 1)
    return jnp.logical_and(cols <= rows, jnp.logical_or(cols >= cfg.pad, rows < cfg.pad))


def attn_fwd(cfg, qr, kv, kpe, *, name):
    lp, t, mh = cfg.lp, cfg.t, cfg.mh
    scale = (cfg.nope + cfg.rope) ** -0.5
    blocks = _q_blocks(cfg)

    def body(q_ref, kv_ref, kp_ref, o_ref, l_ref):
        for qs, qe in blocks:
            n = qe
            q = q_ref[qs:qe, :]
            k2 = jnp.concatenate([kv_ref[0:n, 0:LANE], kp_ref[0:n, :]], axis=1)
            s = _nt(q, k2) * scale
            s = jnp.where(_attn_mask(cfg, qs, qe), s, MASK_VALUE)
            m = jnp.max(s, axis=-1, keepdims=True)
            p = jnp.exp(s - m)
            l = jnp.sum(p, axis=-1, keepdims=True)
            pn = (p * (1.0 / l)).astype(BF16)
            o_ref[qs:qe, :] = _nn(pn, kv_ref[0:n, LANE:2 * LANE])
            l_ref[qs:qe, :] = jnp.broadcast_to(m + jnp.log(l), (qe - qs, LANE))

    hb = pl.BlockSpec((lp, 2 * LANE), lambda b, h: (b, h))
    ob = pl.BlockSpec((lp, LANE), lambda b, h: (b, h))
    return pl.pallas_call(
        body, name=name, grid=(cfg.bsz, mh),
        in_specs=[hb, hb, pl.BlockSpec((lp, LANE), lambda b, h: (b, 0))], out_specs=[ob, ob],
        out_shape=[_sds((t, mh * LANE), F32), _sds((t, mh * LANE), F32)], compiler_params=_cp(),
    )(qr, kv, kpe)


def attn_bwd(cfg, qr, kv, kpe, o, lse, do, *, name):
    lp, t, mh = cfg.lp, cfg.t, cfg.mh
    scale = (cfg.nope + cfg.rope) ** -0.5
    blocks = _q_blocks(cfg)

    def body(q_ref, kv_ref, kp_ref, o_ref, l_ref, do_ref, dq_ref, dkv_ref, dkp_ref, dk_acc, dv_acc):
        dk_acc[...] = jnp.zeros_like(dk_acc)
        dv_acc[...] = jnp.zeros_like(dv_acc)
        for qs, qe in blocks:
            n = qe
            q = q_ref[qs:qe, :]
            k2 = jnp.concatenate([kv_ref[0:n, 0:LANE], kp_ref[0:n, :]], axis=1)
            dov = do_ref[qs:qe, :]
            delta = jnp.sum(dov * o_ref[qs:qe, :], axis=-1, keepdims=True)
            dob = dov.astype(BF16)
            s = _nt(q, k2) * scale
            s = jnp.where(_attn_mask(cfg, qs, qe), s, MASK_VALUE)
            p = jnp.exp(s - l_ref[qs:qe, 0:1])
            dp = _nt(dob, kv_ref[0:n, LANE:2 * LANE])
            ds = (p * (dp - delta) * scale).astype(BF16)
            dq_ref[qs:qe, :] = _nn(ds, k2)
            dv_acc[0:n, :] += _tn(p.astype(BF16), dob)
            dk_acc[0:n, :] += _tn(ds, q)
        dkv_ref[:, 0:LANE] = dk_acc[:, 0:LANE].astype(BF16)
        dkv_ref[:, LANE:2 * LANE] = dv_acc[...].astype(BF16)
        dkp_ref[0] = dk_acc[:, LANE:2 * LANE]

    hb = pl.BlockSpec((lp, 2 * LANE), lambda b, h: (b, h))
    ob = pl.BlockSpec((lp, LANE), lambda b, h: (b, h))
    return pl.pallas_call(
        body, name=name, grid=(cfg.bsz, mh),
        in_specs=[hb, hb, pl.BlockSpec((lp, LANE), lambda b, h: (b, 0)), ob, ob, ob],
        out_specs=[hb, hb, pl.BlockSpec((1, lp, LANE), lambda b, h: (h, b, 0))],
        out_shape=[_sds((t, cfg.qw), F32), _sds((t, mh * 2 * LANE), BF16), _sds((mh, t, LANE), F32)],
        scratch_shapes=[pltpu.VMEM((lp, 2 * LANE), F32), pltpu.VMEM((lp, LANE), F32)], compiler_params=_cp(),
    )(qr, kv, kpe, o, lse, do)


def _live_rows(cfg, tr, shape):
    rows = pl.program_id(1) * tr + lax.broadcasted_iota(jnp.int32, shape, 0)
    return rows >= cfg.pad


def gate_fwd(cfg, ya, yb, g, *, name):
    d, lp = cfg.d, cfg.lp
    tr = _pick(lp, 544, 16)
    nrb = lp // tr

    def body(ya_ref, yb_ref, ga_ref, gb_ref, o_ref):
        mix = jax.nn.sigmoid(ga_ref[...]) * ya_ref[...] + jax.nn.sigmoid(gb_ref[...]) * yb_ref[...]
        o_ref[...] = jnp.where(_live_rows(cfg, tr, mix.shape), mix, 0.0).astype(BF16)

    row = pl.BlockSpec((tr, d), lambda b, j: (b * nrb + j, 0))
    row1 = pl.BlockSpec((tr, d), lambda b, j: (b * nrb + j, 1))
    return pl.pallas_call(
        body, name=name, grid=(cfg.bsz, nrb), in_specs=[row, row, row, row1], out_specs=row,
        out_shape=_sds((cfg.t, d), BF16), compiler_params=_cp(),
    )(ya, yb, g, g)


def gate_bwd(cfg, dmix, ya, yb, g, *, name):
    d, lp = cfg.d, cfg.lp
    tr = _pick(lp, 544, 16)
    nrb = lp // tr

    def body(dm_ref, ya_ref, yb_ref, ga_ref, gb_ref, dya_ref, dyb_ref, dg_ref):
        dm = dm_ref[...]
        dm = jnp.where(_live_rows(cfg, tr, dm.shape), dm, 0.0)
        sa = jax.nn.sigmoid(ga_ref[...])
        sb = jax.nn.sigmoid(gb_ref[...])
        dya_ref[...] = (dm * sa).astype(BF16)
        dyb_ref[...] = (dm * sb).astype(BF16)
        dg_ref[:, 0:d] = (dm * ya_ref[...] * sa * (1.0 - sa)).astype(BF16)
        dg_ref[:, d:2 * d] = (dm * yb_ref[...] * sb * (1.0 - sb)).astype(BF16)

    row = pl.BlockSpec((tr, d), lambda b, j: (b * nrb + j, 0))
    row1 = pl.BlockSpec((tr, d), lambda b, j: (b * nrb + j, 1))
    row2 = pl.BlockSpec((tr, 2 * d), lambda b, j: (b * nrb + j, 0))
    return pl.pallas_call(
        body, name=name, grid=(cfg.bsz, nrb), in_specs=[row, row, row, row, row1], out_specs=[row, row, row2],
        out_shape=[_sds((cfg.t, d), BF16), _sds((cfg.t, d), BF16), _sds((cfg.t, 2 * d), BF16)], compiler_params=_cp(),
    )(dmix, ya, yb, g, g)


def relu2_fwd(a, *, name):
    t, f = a.shape
    tr = _pick(t, 272, 16)

    def body(a_ref, o_ref):
        r = jnp.maximum(a_ref[...], 0.0)
        o_ref[...] = (r * r).astype(BF16)

    row = pl.BlockSpec((tr, f), lambda i: (i, 0))
    return pl.pallas_call(body, name=name, grid=(t // tr,), in_specs=[row], out_specs=row,
                          out_shape=_sds((t, f), BF16), compiler_params=_cp())(a)


def relu2_bwd(dact, a, *, name):
    t, f = a.shape
    tr = _pick(t, 272, 16)

    def body(d_ref, a_ref, o_ref):
        o_ref[...] = (d_ref[...] * 2.0 * jnp.maximum(a_ref[...], 0.0)).astype(BF16)

    row = pl.BlockSpec((tr, f), lambda i: (i, 0))
    return pl.pallas_call(body, name=name, grid=(t // tr,), in_specs=[row, row], out_specs=row,
                          out_shape=_sds((t, f), BF16), compiler_params=_cp())(dact, a)


def loss_head(cfg, h, target, w, *, name):
    d, q, nc = cfg.d, cfg.chunk, cfg.nchunks
    tpb = cfg.seq // q

    def body(h_ref, t_ref, w_ref, loss_ref, dh_ref, dw_ref):
        j = pl.program_id(1)

        @pl.when(jnp.logical_and(j == 0, pl.program_id(0) == 0))
        def _():
            loss_ref[...] = jnp.zeros_like(loss_ref)
            dw_ref[...] = jnp.zeros_like(dw_ref)

        @pl.when(j == 0)
        def _():
            dh_ref[...] = jnp.zeros_like(dh_ref)

        @pl.when(j > 0)
        def _():
            xv = h_ref[...]
            r = lax.rsqrt(jnp.mean(xv * xv, axis=-1, keepdims=True) + EPS)
            xh = xv * r
            err = xh * w_ref[...] - t_ref[...]
            loss_ref[...] += 0.5 * jnp.sum(jnp.sum(err * err, axis=-1, keepdims=True) / d, axis=0, keepdims=True)
            dy = err * (1.0 / d)
            g = dy * w_ref[...]
            dh_ref[...] = r * (g - xh * jnp.mean(g * xh, axis=-1, keepdims=True))
            dw_ref[...] += jnp.sum(dy * xh, axis=0, keepdims=True)

    row = pl.BlockSpec((q, d), lambda b, j: (b * nc + j, 0))
    loss, dh, dw = pl.pallas_call(
        body, name=name, grid=(cfg.bsz, nc),
        in_specs=[row, pl.BlockSpec((q, d), lambda b, j: (b * tpb + jnp.maximum(j - 1, 0), 0)),
                  pl.BlockSpec((1, d), lambda b, j: (0, 0))],
        out_specs=[pl.BlockSpec((8, LANE), lambda b, j: (0, 0)), row, pl.BlockSpec((1, d), lambda b, j: (0, 0))],
        out_shape=[_sds((8, LANE), F32), _sds((cfg.t, d), F32), _sds((1, d), F32)], compiler_params=_cp(),
    )(h, target, w.reshape(1, d))
    return loss[0, 0], dh, dw[0]


def adamw(w, g, m, v, *, name):
    r, c = w.shape
    tr = _pick(r, max(8, (1 << 18) // max(c, 1) // 8 * 8), 8)
    c1 = 1.0 - ADAM_B1 ** ADAM_STEP
    c2 = 1.0 - ADAM_B2 ** ADAM_STEP

    def body(w_ref, g_ref, m_ref, v_ref, d_ref, mo_ref, vo_ref):
        gv = g_ref[...]
        mn = ADAM_B1 * m_ref[...] + (1.0 - ADAM_B1) * gv
        vn = ADAM_B2 * v_ref[...] + (1.0 - ADAM_B2) * (gv * gv)
        mo_ref[...] = mn
        vo_ref[...] = vn
        d_ref[...] = -ADAM_LR * ((mn / c1) / (jnp.sqrt(vn / c2) + ADAM_EPS) + ADAM_WD * w_ref[...])

    blk = pl.BlockSpec((tr, c), lambda i: (i, 0))
    return pl.pallas_call(
        body, name=name, grid=(r // tr,), in_specs=[blk] * 4, out_specs=[blk] * 3,
        out_shape=[_sds((r, c), F32)] * 3, compiler_params=_cp(),
    )(w, g, m, v)


def sum_slabs(x, *, name):
    n, r, c = x.shape
    tr = _pick(r, max(8, (1 << 18) // max(c, 1) // 8 * 8), 8)

    def body(x_ref, o_ref):
        acc = x_ref[0]
        for k in range(1, n):
            acc = acc + x_ref[k]
        o_ref[...] = acc

    return pl.pallas_call(
        body, name=name, grid=(r // tr,), in_specs=[pl.BlockSpec((n, tr, c), lambda i: (0, i, 0))],
        out_specs=pl.BlockSpec((tr, c), lambda i: (i, 0)), out_shape=_sds((r, c), F32), compiler_params=_cp(),
    )(x)


def add_pair(g2, other, layer, *, name):
    _, r, c = g2.shape
    tr = _pick(r, max(8, (1 << 18) // max(c, 1) // 8 * 8), 8)

    def body(l_ref, a_ref, b_ref, o_ref):
        o_ref[...] = a_ref[0] + b_ref[...]

    grid_spec = pltpu.PrefetchScalarGridSpec(
        num_scalar_prefetch=1, grid=(r // tr,),
        in_specs=[pl.BlockSpec((1, tr, c), lambda i, l: (l[0], i, 0)), pl.BlockSpec((tr, c), lambda i, l: (i, 0))],
        out_specs=pl.BlockSpec((tr, c), lambda i, l: (i, 0)))
    return pl.pallas_call(body, name=name, grid_spec=grid_spec, out_shape=_sds((r, c), F32),
                          compiler_params=_cp())(layer, g2, other)


def _coords():
    return lax.axis_index("x"), lax.axis_index("y"), lax.axis_index("c")


def _other_chips(x, y):
    return [(1 - x, y), (x, 1 - y), (1 - x, 1 - y)]


def gather_chips(arrs, *, name):
    n = len(arrs)
    anyspec = pl.BlockSpec(memory_space=pl.ANY)

    def body(*refs):
        ins, outs = refs[:n], refs[n:2 * n]
        send_sems, recv_sems, local_sems = refs[2 * n:]
        x, y, c = _coords()
        me = 2 * x + y
        chips = _other_chips(x, y)
        copies = []
        for k in range(n):
            loc = pltpu.make_async_copy(ins[k], outs[k].at[me], local_sems.at[k])
            loc.start()
            copies.append(loc)
        sends = []
        for k in range(n):
            for j, (px, py) in enumerate(chips):
                cp = pltpu.make_async_remote_copy(
                    src_ref=ins[k], dst_ref=outs[k].at[me], send_sem=send_sems.at[k, j], recv_sem=recv_sems.at[k, j],
                    device_id=(px, py, c), device_id_type=MESH)
                cp.start()
                sends.append(cp)
        for k in range(n):
            for j, (px, py) in enumerate(chips):
                pltpu.make_async_remote_copy(
                    src_ref=ins[k], dst_ref=outs[k].at[2 * px + py], send_sem=send_sems.at[k, j],
                    recv_sem=recv_sems.at[k, j], device_id=(px, py, c), device_id_type=MESH).wait_recv()
        for cp in sends:
            cp.wait_send()
        for cp in copies:
            cp.wait()

    return pl.pallas_call(
        body, name=name, in_specs=[anyspec] * n, out_specs=[anyspec] * n,
        out_shape=[_sds((4,) + a.shape, a.dtype) for a in arrs],
        scratch_shapes=[pltpu.SemaphoreType.DMA((n, 3)), pltpu.SemaphoreType.DMA((n, 3)), pltpu.SemaphoreType.DMA((n,))],
        compiler_params=_cp(has_side_effects=True),
    )(*arrs)


def allreduce_small(vec, *, name):
    r, c = vec.shape

    def body(v_ref, o_ref, buf, send_sems, recv_sems):
        x, y, cc = _coords()
        me = 4 * x + 2 * y + cc
        buf[me] = v_ref[...]
        sends = []
        flips = [(fx, fy, fc) for fx in (0, 1) for fy in (0, 1) for fc in (0, 1)][1:]
        for j, (fx, fy, fc) in enumerate(flips):
            peer = ((1 - x) if fx else x, (1 - y) if fy else y, (1 - cc) if fc else cc)
            cp = pltpu.make_async_remote_copy(
                src_ref=v_ref, dst_ref=buf.at[me], send_sem=send_sems.at[j], recv_sem=recv_sems.at[j],
                device_id=peer, device_id_type=MESH)
            cp.start()
            sends.append(cp)
        for j, (fx, fy, fc) in enumerate(flips):
            px, py, pc = ((1 - x) if fx else x, (1 - y) if fy else y, (1 - cc) if fc else cc)
            pltpu.make_async_remote_copy(
                src_ref=v_ref, dst_ref=buf.at[4 * px + 2 * py + pc], send_sem=send_sems.at[j],
                recv_sem=recv_sems.at[j], device_id=(px, py, pc), device_id_type=MESH).wait_recv()
        for cp in sends:
            cp.wait_send()
        acc = buf[0]
        for k in range(1, 8):
            acc = acc + buf[k]
        o_ref[...] = acc

    vm = pl.BlockSpec(memory_space=pltpu.VMEM)
    return pl.pallas_call(
        body, name=name, in_specs=[vm], out_specs=vm, out_shape=_sds((r, c), F32),
        scratch_shapes=[pltpu.VMEM((8, r, c), F32), pltpu.SemaphoreType.DMA((7,)), pltpu.SemaphoreType.DMA((7,))],
        compiler_params=_cp(has_side_effects=True),
    )(vec)


def pair_exchange(arrs, *, name):
    n = len(arrs)
    anyspec = pl.BlockSpec(memory_space=pl.ANY)

    def body(*refs):
        ins, outs = refs[:n], refs[n:2 * n]
        send_sems, recv_sems = refs[2 * n:]
        x, y, c = _coords()
        sends = []
        for k in range(n):
            cp = pltpu.make_async_remote_copy(
                src_ref=ins[k].at[1 - c], dst_ref=outs[k], send_sem=send_sems.at[k], recv_sem=recv_sems.at[k],
                device_id=(x, y, 1 - c), device_id_type=MESH)
            cp.start()
            sends.append(cp)
        for cp in sends:
            cp.wait()

    return pl.pallas_call(
        body, name=name, in_specs=[anyspec] * n, out_specs=[anyspec] * n,
        out_shape=[_sds(a.shape[1:], a.dtype) for a in arrs],
        scratch_shapes=[pltpu.SemaphoreType.DMA((n,)), pltpu.SemaphoreType.DMA((n,))],
        compiler_params=_cp(has_side_effects=True),
    )(*arrs)


def scatter_chips(arrs, *, name):
    n = len(arrs)
    anyspec = pl.BlockSpec(memory_space=pl.ANY)

    def body(*refs):
        ins, outs = refs[:n], refs[n:2 * n]
        send_sems, recv_sems, local_sems = refs[2 * n:]
        x, y, c = _coords()
        me = 2 * x + y
        chips = _other_chips(x, y)
        copies = []
        for k in range(n):
            loc = pltpu.make_async_copy(ins[k].at[me], outs[k].at[me], local_sems.at[k])
            loc.start()
            copies.append(loc)
        sends = []
        for k in range(n):
            for j, (px, py) in enumerate(chips):
                cp = pltpu.make_async_remote_copy(
                    src_ref=ins[k].at[2 * px + py], dst_ref=outs[k].at[me], send_sem=send_sems.at[k, j],
                    recv_sem=recv_sems.at[k, j], device_id=(px, py, c), device_id_type=MESH)
                cp.start()
                sends.append(cp)
        for k in range(n):
            for j, (px, py) in enumerate(chips):
                pltpu.make_async_remote_copy(
                    src_ref=ins[k].at[me], dst_ref=outs[k].at[2 * px + py], send_sem=send_sems.at[k, j],
                    recv_sem=recv_sems.at[k, j], device_id=(px, py, c), device_id_type=MESH).wait_recv()
        for cp in sends:
            cp.wait_send()
        for cp in copies:
            cp.wait()

    return pl.pallas_call(
        body, name=name, in_specs=[anyspec] * n, out_specs=[anyspec] * n,
        out_shape=[_sds(a.shape, a.dtype) for a in arrs],
        scratch_shapes=[pltpu.SemaphoreType.DMA((n, 3)), pltpu.SemaphoreType.DMA((n, 3)), pltpu.SemaphoreType.DMA((n,))],
        compiler_params=_cp(has_side_effects=True),
    )(*arrs)


def pair_allgather(arrs, *, name):
    n = len(arrs)
    anyspec = pl.BlockSpec(memory_space=pl.ANY)

    def body(*refs):
        ins, outs = refs[:n], refs[n:2 * n]
        send_sems, recv_sems, local_sems = refs[2 * n:]
        x, y, c = _coords()
        copies, sends = [], []
        for k in range(n):
            loc = pltpu.make_async_copy(ins[k], outs[k].at[c], local_sems.at[k])
            loc.start()
            copies.append(loc)
            cp = pltpu.make_async_remote_copy(
                src_ref=ins[k], dst_ref=outs[k].at[c], send_sem=send_sems.at[k], recv_sem=recv_sems.at[k],
                device_id=(x, y, 1 - c), device_id_type=MESH)
            cp.start()
            sends.append(cp)
        for k in range(n):
            pltpu.make_async_remote_copy(
                src_ref=ins[k], dst_ref=outs[k].at[1 - c], send_sem=send_sems.at[k], recv_sem=recv_sems.at[k],
                device_id=(x, y, 1 - c), device_id_type=MESH).wait_recv()
        for cp in sends:
            cp.wait_send()
        for cp in copies:
            cp.wait()

    return pl.pallas_call(
        body, name=name, in_specs=[anyspec] * n, out_specs=[anyspec] * n,
        out_shape=[_sds((2,) + a.shape, a.dtype) for a in arrs],
        scratch_shapes=[pltpu.SemaphoreType.DMA((n,)), pltpu.SemaphoreType.DMA((n,)), pltpu.SemaphoreType.DMA((n,))],
        compiler_params=_cp(has_side_effects=True),
    )(*arrs)


BIG = ["w_in", "w_uq", "w_ukv", "w_branch_ssm", "w_branch_mla", "w_out", "w_mlp_up", "w_mlp_down"]
COL_SHARDED = {"w_in", "w_uq", "w_ukv", "w_mlp_up"}
SMALL_REPL = ["norm_mix_w", "conv_b", "dt_bias", "a_log", "d_skip", "ssm_norm_w", "q_norm_w", "kv_norm_w", "norm_mlp_w"]


def _unshard(name, g):
    if name in COL_SHARDED:
        _, l, r, c = g.shape
        return jnp.transpose(g, (1, 2, 0, 3)).reshape(l, r, 4 * c)
    _, l, r, c = g.shape
    return jnp.transpose(g, (1, 0, 2, 3)).reshape(l, 4 * r, c)


def _to_shards(name, full):
    r, c = full.shape
    if name in COL_SHARDED:
        return jnp.transpose(full.reshape(r, 4, c // 4), (1, 0, 2))
    return full.reshape(4, r // 4, c)


def prep_layer(cfg, w):
    sp = np.cumsum(cfg.in_splits)[:-1].tolist()
    z, xbc, dt, cq, ckv, kr, gs, gm = jnp.split(w["w_in"], sp, axis=1)
    zpad = lambda n: jnp.zeros((cfg.d, n), z.dtype)
    out = dict(
        w_z=z, w_xbc=xbc, w_g=jnp.concatenate([gs, gm], axis=1),
        w_s=jnp.concatenate([cq, ckv, kr, zpad(LANE - cfg.rope), dt, zpad(LANE - cfg.heads)], axis=1),
        w_uq=jnp.pad(w["w_uq"].reshape(cfg.ql, cfg.mh, cfg.nope + cfg.rope),
                     ((0, 0), (0, 0), (0, 2 * LANE - cfg.nope - cfg.rope))).reshape(cfg.ql, cfg.qw),
        w_ukv=w["w_ukv"], w_bs=w["w_branch_ssm"], w_bm=w["w_branch_mla"], w_out=w["w_out"],
        w_up=w["w_mlp_up"], w_down=w["w_mlp_down"])
    return {k: v.astype(BF16) for k, v in out.items()}


def unprep_grads(cfg, g):
    ql, kvl = cfg.ql, cfg.kvl
    ds_ = g["w_s"]
    cq, ckv = ds_[:, :ql], ds_[:, ql:ql + kvl]
    kr = ds_[:, ql + kvl:ql + kvl + cfg.rope]
    dt = ds_[:, ql + kvl + LANE:ql + kvl + LANE + cfg.heads]
    w_in = jnp.concatenate([g["w_z"], g["w_xbc"], dt, cq, ckv, kr, g["w_g"]], axis=1)
    w_uq = g["w_uq"].reshape(cfg.ql, cfg.mh, 2 * LANE)[:, :, :cfg.nope + cfg.rope].reshape(cfg.ql, -1)
    return dict(w_in=w_in, w_uq=w_uq, w_ukv=g["w_ukv"], w_branch_ssm=g["w_bs"], w_branch_mla=g["w_bm"],
                w_out=g["w_out"], w_mlp_up=g["w_up"], w_mlp_down=g["w_down"])


def layer_fwd(cfg, h, pw, sm, tabs, li):
    n = lambda s: f"l{li}_{s}"
    u = rmsnorm_fwd(h, sm["norm_mix_w"], name=n("norm_mix"))
    z = matmul(u, pw["w_z"], name=n("in_z"))
    xbc = matmul(u, pw["w_xbc"], name=n("in_xbc"))
    g = matmul(u, pw["w_g"], name=n("in_g"))
    small = matmul(u, pw["w_s"], name=n("in_s"), tn=cfg.sw)
    xc = conv_fwd(cfg, xbc, sm["conv_w"], sm["conv_b"], name=n("conv"))
    y, sin = ssd_fwd(cfg, xc, small, sm["dt_bias_p"], sm["avec"], sm["dexp"], name=n("ssd"))
    y_ssm = tail_fwd(cfg, y, z, sm["ssm_norm_w"], name=n("tail"))
    cqn = rmsnorm_fwd(small, sm["q_norm_w"], cw=cfg.ql, ci=0, name=n("q_norm"))
    ckvn = rmsnorm_fwd(small, sm["kv_norm_w"], cw=cfg.kvl, ci=cfg.ql // cfg.kvl, name=n("kv_norm"))
    qf = matmul(cqn, pw["w_uq"], name=n("uq"))
    kv = matmul(ckvn, pw["w_ukv"], out_dtype=BF16, name=n("ukv"))
    qr, kpe = rope_fwd(cfg, qf, small, tabs, name=n("rope"))
    o, lse = attn_fwd(cfg, qr, kv, kpe, name=n("attn"))
    ya = matmul(y_ssm, pw["w_bs"], name=n("branch_ssm"))
    yb = matmul(o, pw["w_bm"], name=n("branch_mla"))
    mixed = gate_fwd(cfg, ya, yb, g, name=n("gate"))
    h1 = matmul(mixed, pw["w_out"], add=h, name=n("out"))
    v = rmsnorm_fwd(h1, sm["norm_mlp_w"], name=n("norm_mlp"))
    a = matmul(v, pw["w_up"], name=n("up"))
    act = relu2_fwd(a, name=n("relu2"))
    h2 = matmul(act, pw["w_down"], add=h1, name=n("down"))
    saved = dict(h=h, u=u, z=z, xbc=xbc, g=g, small=small, xc=xc, y=y, sin=sin, y_ssm=y_ssm, cqn=cqn, ckvn=ckvn,
                 qr=qr, kv=kv, kpe=kpe, o=o, lse=lse, ya=ya, yb=yb, mixed=mixed, h1=h1, v=v, a=a, act=act)
    return h2, saved


def layer_bwd(cfg, dh2, pw, sm, tabs, s, li):
    n = lambda t: f"l{li}_b_{t}"
    gw, gs = {}, {}
    gw["w_down"] = matmul(s["act"], dh2, ta=True, name=n("dw_down"))
    dact = matmul(dh2, pw["w_down"], tb=True, name=n("dact"))
    da = relu2_bwd(dact, s["a"], name=n("relu2"))
    gw["w_up"] = matmul(s["v"], da, ta=True, name=n("dw_up"))
    dv = matmul(da, pw["w_up"], tb=True, name=n("dv"))
    dh1, gs["norm_mlp_w"] = rmsnorm_bwd(dv, s["h1"], sm["norm_mlp_w"], res=dh2, name=n("norm_mlp"))
    gw["w_out"] = matmul(s["mixed"], dh1, ta=True, name=n("dw_out"))
    dmix = matmul(dh1, pw["w_out"], tb=True, name=n("dmix"))
    dya, dyb, dg = gate_bwd(cfg, dmix, s["ya"], s["yb"], s["g"], name=n("gate"))
    gw["w_bs"] = matmul(s["y_ssm"], dya, ta=True, name=n("dw_bs"))
    gw["w_bm"] = matmul(s["o"], dyb, ta=True, name=n("dw_bm"))
    dy_ssm = matmul(dya, pw["w_bs"], tb=True, name=n("dy_ssm"))
    do = matmul(dyb, pw["w_bm"], tb=True, name=n("do"))
    dq, dkv, dkpe = attn_bwd(cfg, s["qr"], s["kv"], s["kpe"], s["o"], s["lse"], do, name=n("attn"))
    dqf, dkr = rope_bwd(cfg, dq, dkpe, tabs, name=n("rope"))
    gw["w_uq"] = matmul(s["cqn"], dqf, ta=True, name=n("dw_uq"))
    gw["w_ukv"] = matmul(s["ckvn"], dkv, ta=True, name=n("dw_ukv"))
    dcqn = matmul(dqf, pw["w_uq"], tb=True, name=n("dcqn"))
    dckvn = matmul(dkv, pw["w_ukv"], tb=True, name=n("dckvn"))
    dcq, gs["q_norm_w"] = rmsnorm_bwd(dcqn, s["small"], sm["q_norm_w"], cw=cfg.ql, ci=0, out_dtype=BF16, name=n("q_norm"))
    dckv, gs["kv_norm_w"] = rmsnorm_bwd(dckvn, s["small"], sm["kv_norm_w"], cw=cfg.kvl, ci=cfg.ql // cfg.kvl,
                                        out_dtype=BF16, name=n("kv_norm"))
    dy, dz, gs["ssm_norm_w"] = tail_bwd(cfg, dy_ssm, s["y"], s["z"], sm["ssm_norm_w"], name=n("tail"))
    dxs, db, dc, ddt, ddexp, dav, dbias = ssd_bwd(cfg, s["xc"], s["small"], sm["dt_bias_p"], sm["avec"], sm["dexp"],
                                                  s["sin"], dy, name=n("ssd"))
    dxc = jnp.concatenate([dxs, db, dc], axis=1)
    dxbc, gs["conv_w"], gs["conv_b"] = conv_bwd(cfg, s["xbc"], sm["conv_w"], sm["conv_b"], dxc, name=n("conv"))
    gs["d_skip"] = ddexp.reshape(cfg.heads, cfg.hd).sum(axis=1)
    gs["a_log"] = (dav[0] * sm["avec"][0])[:cfg.heads]
    gs["dt_bias"] = dbias[0, :cfg.heads]
    dsmall = jnp.concatenate([dcq, dckv, dkr.astype(BF16), ddt.astype(BF16)], axis=1)
    gw["w_z"] = matmul(s["u"], dz, ta=True, name=n("dw_z"))
    gw["w_xbc"] = matmul(s["u"], dxbc, ta=True, name=n("dw_xbc"))
    gw["w_g"] = matmul(s["u"], dg, ta=True, name=n("dw_g"))
    gw["w_s"] = matmul(s["u"], dsmall, ta=True, name=n("dw_s"))
    du = matmul(dz, pw["w_z"], tb=True, name=n("du_z"))
    du = matmul(dxbc, pw["w_xbc"], tb=True, add=du, name=n("du_xbc"))
    du = matmul(dg, pw["w_g"], tb=True, add=du, name=n("du_g"))
    du = matmul(dsmall, pw["w_s"], tb=True, add=du, name=n("du_s"))
    dh, gs["norm_mix_w"] = rmsnorm_bwd(du, s["h"], sm["norm_mix_w"], res=dh1, name=n("norm_mix"))
    return dh, gw, gs


def small_params(cfg, p, li):
    pad_l = lambda v: jnp.pad(v, (0, LANE - v.shape[0])).reshape(1, LANE)
    return dict(
        norm_mix_w=p["norm_mix_w"][li], conv_w=p["conv_w"][li], conv_b=p["conv_b"][li],
        dt_bias_p=pad_l(p["dt_bias"][li]), avec=pad_l(-jnp.exp(p["a_log"][li])),
        dexp=jnp.repeat(p["d_skip"][li], cfg.hd).reshape(1, cfg.inner),
        ssm_norm_w=p["ssm_norm_w"][li], q_norm_w=p["q_norm_w"][li], kv_norm_w=p["kv_norm_w"][li],
        norm_mlp_w=p["norm_mlp_w"][li])


def local_step(cfg, x, target, p, depth=2):
    bsz, d = cfg.bsz, cfg.d
    lead = jnp.zeros((bsz, cfg.pad, d), F32)
    meta = jnp.broadcast_to(p["meta_tokens"][None], (bsz, cfg.n_meta, d))
    h = jnp.concatenate([lead, meta, x], axis=1).reshape(cfg.t, d)
    tabs = rope_tables(cfg)
    saved, sms = [], []
    for li in range(depth):
        sm = small_params(cfg, p, li)
        h, s = layer_fwd(cfg, h, p["pw"][li], sm, tabs, li)
        saved.append(s)
        sms.append(sm)
    loss, dh, dfw = loss_head(cfg, h, target.reshape(bsz * cfg.seq, d), p["final_norm_w"], name="loss_head")
    gws, gss = [None] * depth, [None] * depth
    for li in reversed(range(depth)):
        dh, gws[li], gss[li] = layer_bwd(cfg, dh, p["pw"][li], sms[li], tabs, saved[li], li)
    dh = dh.reshape(bsz, cfg.lp, d)
    grad_x = dh[:, cfg.chunk:, :]
    gmeta = jnp.sum(dh[:, cfg.pad:cfg.chunk, :], axis=0)
    return loss, grad_x, gmeta, gws, gss, dfw


def _pack_small(parts):
    flat = jnp.concatenate([a.reshape(-1) for a in parts])
    n = flat.shape[0]
    npad = -n % (8 * LANE)
    return jnp.pad(flat, (0, npad)).reshape(-1, LANE), n


def _unpack_small(vec, shapes):
    flat = vec.reshape(-1)
    out, off = [], 0
    for sh in shapes:
        sz = int(np.prod(sh))
        out.append(flat[off:off + sz].reshape(sh))
        off += sz
    return out


def _as2d(a):
    return a.reshape(-1, a.shape[-1])


def kernel(x, meta_tokens, norm_mix_w, w_in, conv_w, conv_b, dt_bias, a_log, d_skip, ssm_norm_w, q_norm_w, kv_norm_w, w_uq, w_ukv, w_branch_ssm, w_branch_mla, w_out, norm_mlp_w, w_mlp_up, w_mlp_down, final_norm_w, loss_target, m_meta_tokens, m_norm_mix_w, m_w_in, m_conv_w, m_conv_b, m_dt_bias, m_a_log, m_d_skip, m_ssm_norm_w, m_q_norm_w, m_kv_norm_w, m_w_uq, m_w_ukv, m_w_branch_ssm, m_w_branch_mla, m_w_out, m_norm_mlp_w, m_w_mlp_up, m_w_mlp_down, m_final_norm_w, v_meta_tokens, v_norm_mix_w, v_w_in, v_conv_w, v_conv_b, v_dt_bias, v_a_log, v_d_skip, v_ssm_norm_w, v_q_norm_w, v_kv_norm_w, v_w_uq, v_w_ukv, v_w_branch_ssm, v_w_branch_mla, v_w_out, v_norm_mlp_w, v_w_mlp_up, v_w_mlp_down, v_final_norm_w):
    cfg = CFG
    names = ["meta_tokens", "norm_mix_w", "w_in", "conv_w", "conv_b", "dt_bias", "a_log", "d_skip", "ssm_norm_w",
             "q_norm_w", "kv_norm_w", "w_uq", "w_ukv", "w_branch_ssm", "w_branch_mla", "w_out", "norm_mlp_w",
             "w_mlp_up", "w_mlp_down", "final_norm_w"]
    wts = dict(zip(names, [meta_tokens, norm_mix_w, w_in, conv_w, conv_b, dt_bias, a_log, d_skip, ssm_norm_w,
                           q_norm_w, kv_norm_w, w_uq, w_ukv, w_branch_ssm, w_branch_mla, w_out, norm_mlp_w,
                           w_mlp_up, w_mlp_down, final_norm_w]))
    ms = dict(zip(names, [m_meta_tokens, m_norm_mix_w, m_w_in, m_conv_w, m_conv_b, m_dt_bias, m_a_log, m_d_skip,
                          m_ssm_norm_w, m_q_norm_w, m_kv_norm_w, m_w_uq, m_w_ukv, m_w_branch_ssm, m_w_branch_mla,
                          m_w_out, m_norm_mlp_w, m_w_mlp_up, m_w_mlp_down, m_final_norm_w]))
    vs = dict(zip(names, [v_meta_tokens, v_norm_mix_w, v_w_in, v_conv_w, v_conv_b, v_dt_bias, v_a_log, v_d_skip,
                          v_ssm_norm_w, v_q_norm_w, v_kv_norm_w, v_w_uq, v_w_ukv, v_w_branch_ssm, v_w_branch_mla,
                          v_w_out, v_norm_mlp_w, v_w_mlp_up, v_w_mlp_down, v_final_norm_w]))
    cx, cy, cc = _coords()
    chip = 2 * cx + cy

    gathered = gather_chips([wts[k].astype(BF16) for k in BIG] + [meta_tokens, conv_w], name="gather_weights")
    full = {k: _unshard(k, g) for k, g in zip(BIG, gathered[:len(BIG)])}
    meta_full = jnp.transpose(gathered[len(BIG)], (1, 0, 2)).reshape(cfg.n_meta, cfg.d)
    conv_w_full = jnp.transpose(gathered[len(BIG) + 1], (1, 2, 0, 3)).reshape(2, cfg.convk, cfg.conv_dim)
    p = dict(wts)
    p["meta_tokens"] = meta_full
    p["conv_w"] = conv_w_full
    p["pw"] = [prep_layer(cfg, {k: full[k][li] for k in BIG}) for li in range(2)]

    loss, grad_x, gmeta, gws, gss, dfw = local_step(cfg, x, loss_target, p)
    loss = lax.psum(loss, ("x", "y", "c"))

    small_names = SMALL_REPL + ["conv_w"]
    parts = [jnp.stack([gss[0][k], gss[1][k]]) for k in small_names] + [dfw, gmeta]
    shapes = [a.shape for a in parts]
    vec, _ = _pack_small(parts)
    red = _unpack_small(allreduce_small(vec, name="allreduce_small"), shapes)
    sg = dict(zip(small_names + ["final_norm_w", "meta_tokens"], red))
    sg["conv_w"] = lax.dynamic_slice_in_dim(sg["conv_w"], chip * (cfg.conv_dim // 4), cfg.conv_dim // 4, axis=2)
    sg["meta_tokens"] = lax.dynamic_slice_in_dim(sg["meta_tokens"], chip * (cfg.d // 4), cfg.d // 4, axis=1)

    ug = [unprep_grads(cfg, gws[li]) for li in range(2)]
    slabs = [jnp.stack([_to_shards(k, ug[0][k]), _to_shards(k, ug[1][k])]) for k in BIG]
    layer = jnp.reshape(cc, (1,)).astype(jnp.int32)
    theirs = pair_exchange(slabs, name="grad_pair_exchange")
    chip_sum = []
    for k, sl, th in zip(BIG, slabs, theirs):
        sh = sl.shape
        s2 = add_pair(sl.reshape(2, sh[1] * sh[2], sh[3]), th.reshape(sh[1] * sh[2], sh[3]), layer, name=f"pair_add_{k}")
        chip_sum.append(s2.reshape(sh[1:]))
    from_chips = scatter_chips(chip_sum, name="grad_scatter")
    mine = []
    for k, fc in zip(BIG, from_chips):
        mine.append(sum_slabs(fc, name=f"chip_sum_{k}"))
    both = pair_allgather(mine, name="grad_pair_allgather")
    bg = dict(zip(BIG, both))

    grads, deltas, new_m, new_v = {}, {}, {}, {}
    for k in names:
        g = bg[k] if k in bg else sg[k]
        w2, g2, m2, v2 = _as2d(wts[k]), _as2d(g), _as2d(ms[k]), _as2d(vs[k])
        if k in bg:
            dl, mn, vn = adamw(w2, g2, m2, v2, name=f"adamw_{k}")
        else:
            dl, mn, vn = adamw_small(w2, g2, m2, v2, name=f"adamw_{k}")
        grads[k] = g.reshape(wts[k].shape)
        deltas[k], new_m[k], new_v[k] = (t.reshape(wts[k].shape) for t in (dl, mn, vn))
    return (loss, grad_x, *[grads[k] for k in names], *[deltas[k] for k in names],
            *[new_m[k] for k in names], *[new_v[k] for k in names])


def adamw_small(w, g, m, v, *, name):
    c1 = 1.0 - ADAM_B1 ** ADAM_STEP
    c2 = 1.0 - ADAM_B2 ** ADAM_STEP

    def body(w_ref, g_ref, m_ref, v_ref, d_ref, mo_ref, vo_ref):
        gv = g_ref[...]
        mn = ADAM_B1 * m_ref[...] + (1.0 - ADAM_B1) * gv
        vn = ADAM_B2 * v_ref[...] + (1.0 - ADAM_B2) * (gv * gv)
        mo_ref[...] = mn
        vo_ref[...] = vn
        d_ref[...] = -ADAM_LR * ((mn / c1) / (jnp.sqrt(vn / c2) + ADAM_EPS) + ADAM_WD * w_ref[...])

    vm = pl.BlockSpec(memory_space=pltpu.VMEM)
    return pl.pallas_call(body, name=name, in_specs=[vm] * 4, out_specs=[vm] * 3,
                          out_shape=[_sds(w.shape, F32)] * 3, compiler_params=_cp())(w, g, m, v)
```

```python
import functools
import math
from typing import NamedTuple

import numpy as np
import jax
import jax.numpy as jnp
from jax import lax
from jax.experimental import pallas as pl
from jax.experimental.pallas import tpu as pltpu

F32 = jnp.float32
BF16 = jnp.bfloat16
HI = lax.Precision.HIGHEST
EPS = 1e-6
ROPE_THETA = 10000.0
LANE = 128
VMEM_LIMIT = 56 * 1024 * 1024
MASK_VALUE = -1e30
ADAM_LR, ADAM_B1, ADAM_B2, ADAM_EPS, ADAM_WD, ADAM_STEP = 0.001, 0.9, 0.999, 1e-08, 0.01, 10
MESH = pl.DeviceIdType.MESH


class Cfg(NamedTuple):
    d: int = 1024
    seq: int = 2048
    bsz: int = 2
    n_meta: int = 16
    inner: int = 2048
    hd: int = 64
    groups: int = 4
    state: int = 128
    convk: int = 4
    chunk: int = 128
    mh: int = 8
    ql: int = 512
    kvl: int = 256
    nope: int = 128
    rope: int = 64
    vd: int = 128
    ff: int = 4096

    @property
    def heads(self): return self.inner // self.hd
    @property
    def gw(self): return self.inner // self.groups
    @property
    def conv_dim(self): return self.inner + 2 * self.groups * self.state
    @property
    def pad(self): return self.chunk - self.n_meta
    @property
    def lp(self): return self.chunk + self.seq
    @property
    def t(self): return self.bsz * self.lp
    @property
    def nchunks(self): return self.lp // self.chunk
    @property
    def sw(self): return self.ql + self.kvl + 2 * LANE
    @property
    def kt(self): return (self.ql + self.kvl) // LANE
    @property
    def dtt(self): return self.kt + 1
    @property
    def qw(self): return self.mh * 2 * LANE
    @property
    def in_splits(self):
        return [self.inner, self.conv_dim, self.heads, self.ql, self.kvl, self.rope, self.d, self.d]


CFG = Cfg()


def _pick(dim, pref, mult):
    best = None
    for t in range(mult, min(dim, pref) + 1, mult):
        if dim % t == 0:
            best = t
    return best if best is not None else dim


def _cp(**kw):
    return pltpu.CompilerParams(vmem_limit_bytes=VMEM_LIMIT, **kw)


def _sds(shape, dtype):
    return jax.ShapeDtypeStruct(tuple(shape), dtype)


def _silu(x):
    return x * jax.nn.sigmoid(x)


def _dsilu(x):
    s = jax.nn.sigmoid(x)
    return s * (1.0 + x * (1.0 - s))


def matmul(a, b, *, ta=False, tb=False, out_dtype=F32, add=None, name, tm=None, tn=None, tk=None):
    if ta:
        k_dim, m_dim = a.shape
    else:
        m_dim, k_dim = a.shape
    if tb:
        n_dim, k2 = b.shape
    else:
        k2, n_dim = b.shape
    assert k_dim == k2, (a.shape, b.shape, ta, tb)
    if ta:
        tm = tm or _pick(m_dim, 1024, LANE)
        tk = tk or _pick(k_dim, 1088, 16)
        tn = tn or _pick(n_dim, 1024, LANE)
    else:
        tm = tm or _pick(m_dim, 1088, 16)
        tk = tk or _pick(k_dim, 1024 if a.dtype == F32 else 2048, LANE)
        tn = tn or _pick(n_dim, 512, LANE)
    nm, nn, nk = m_dim // tm, n_dim // tn, k_dim // tk
    dn = (((0 if ta else 1,), (1 if tb else 0,)), ((), ()))
    has_add = add is not None

    def body(*refs):
        if has_add:
            a_ref, b_ref, add_ref, o_ref = refs[:4]
            scr = refs[4:]
        else:
            a_ref, b_ref, o_ref = refs[:3]
            add_ref = None
            scr = refs[3:]
        p = lax.dot_general(a_ref[...].astype(BF16), b_ref[...].astype(BF16), dn, preferred_element_type=F32)

        def finish(r):
            if has_add:
                r = r + add_ref[...].astype(F32)
            o_ref[...] = r.astype(out_dtype)

        if nk == 1:
            finish(p)
        else:
            acc = scr[0]
            k = pl.program_id(2)

            @pl.when(k == 0)
            def _():
                acc[...] = p

            @pl.when(k > 0)
            def _():
                acc[...] += p

            @pl.when(k == nk - 1)
            def _():
                finish(acc[...])

    a_spec = pl.BlockSpec((tk, tm), lambda i, j, k: (k, i)) if ta else pl.BlockSpec((tm, tk), lambda i, j, k: (i, k))
    b_spec = pl.BlockSpec((tn, tk), lambda i, j, k: (j, k)) if tb else pl.BlockSpec((tk, tn), lambda i, j, k: (k, j))
    o_spec = pl.BlockSpec((tm, tn), lambda i, j, k: (i, j))
    in_specs = [a_spec, b_spec] + ([o_spec] if has_add else [])
    args = [a, b] + ([add] if has_add else [])
    return pl.pallas_call(
        body, name=name, grid=(nm, nn, nk), in_specs=in_specs, out_specs=o_spec,
        out_shape=_sds((m_dim, n_dim), out_dtype),
        scratch_shapes=[pltpu.VMEM((tm, tn), F32)] if nk > 1 else [],
        compiler_params=_cp(dimension_semantics=("parallel", "parallel", "arbitrary")),
    )(*args)


def rmsnorm_fwd(x, w, *, cw=None, ci=0, name):
    t = x.shape[0]
    cw = cw or x.shape[1]
    tr = _pick(t, 544, 16)

    def body(x_ref, w_ref, o_ref):
        xv = x_ref[...].astype(F32)
        r = lax.rsqrt(jnp.mean(xv * xv, axis=-1, keepdims=True) + EPS)
        o_ref[...] = (xv * r * w_ref[...]).astype(BF16)

    return pl.pallas_call(
        body, name=name, grid=(t // tr,),
        in_specs=[pl.BlockSpec((tr, cw), lambda i: (i, ci)), pl.BlockSpec((1, cw), lambda i: (0, 0))],
        out_specs=pl.BlockSpec((tr, cw), lambda i: (i, 0)),
        out_shape=_sds((t, cw), BF16), compiler_params=_cp(),
    )(x, w.reshape(1, cw))


def rmsnorm_bwd(dy, x, w, *, cw=None, ci=0, res=None, out_dtype=F32, name):
    t = x.shape[0]
    cw = cw or x.shape[1]
    tr = _pick(t, 544, 16)
    has_res = res is not None

    def body(*refs):
        if has_res:
            dy_ref, x_ref, w_ref, res_ref, dx_ref, dw_ref = refs
        else:
            dy_ref, x_ref, w_ref, dx_ref, dw_ref = refs
        xv = x_ref[...].astype(F32)
        dyv = dy_ref[...].astype(F32)
        r = lax.rsqrt(jnp.mean(xv * xv, axis=-1, keepdims=True) + EPS)
        xh = xv * r
        g = dyv * w_ref[...]
        dx = r * (g - xh * jnp.mean(g * xh, axis=-1, keepdims=True))
        if has_res:
            dx = dx + res_ref[...]
        dx_ref[...] = dx.astype(out_dtype)

        @pl.when(pl.program_id(0) == 0)
        def _():
            dw_ref[...] = jnp.zeros_like(dw_ref)

        dw_ref[...] += jnp.sum(dyv * xh, axis=0, keepdims=True)

    row = pl.BlockSpec((tr, cw), lambda i: (i, 0))
    in_specs = [row, pl.BlockSpec((tr, cw), lambda i: (i, ci)), pl.BlockSpec((1, cw), lambda i: (0, 0))]
    args = [dy, x, w.reshape(1, cw)]
    if has_res:
        in_specs.append(row)
        args.append(res)
    dx, dw = pl.pallas_call(
        body, name=name, grid=(t // tr,), in_specs=in_specs,
        out_specs=[row, pl.BlockSpec((1, cw), lambda i: (0, 0))],
        out_shape=[_sds((t, cw), out_dtype), _sds((1, cw), F32)], compiler_params=_cp(),
    )(*args)
    return dx, dw[0]


def _shift_down(x, s, rows):
    if s == 0:
        return x
    return jnp.where(rows >= s, pltpu.roll(x, s, 0), 0.0)


def _shift_up(x, s, rows):
    if s == 0:
        return x
    n = x.shape[0]
    return jnp.where(rows < n - s, pltpu.roll(x, n - s, 0), 0.0)


def _conv_pre(x, w_ref, b_ref, rows, kk):
    pre = b_ref[...] + jnp.zeros_like(x)
    for k in range(kk):
        pre = pre + w_ref[k:k + 1, :] * _shift_down(x, kk - 1 - k, rows)
    return pre


def conv_fwd(cfg, xbc, w, b, *, name):
    lp, cd, kk = cfg.lp, cfg.conv_dim, cfg.convk
    cb = _pick(cd, 512, LANE)

    def body(x_ref, w_ref, b_ref, o_ref):
        x = x_ref[...]
        rows = lax.broadcasted_iota(jnp.int32, x.shape, 0)
        o_ref[...] = _silu(_conv_pre(x, w_ref, b_ref, rows, kk))

    blk = pl.BlockSpec((lp, cb), lambda j, bb: (bb, j))
    return pl.pallas_call(
        body, name=name, grid=(cd // cb, cfg.bsz),
        in_specs=[blk, pl.BlockSpec((kk, cb), lambda j, bb: (0, j)), pl.BlockSpec((1, cb), lambda j, bb: (0, j))],
        out_specs=blk, out_shape=_sds((cfg.t, cd), F32), compiler_params=_cp(),
    )(xbc, w, b.reshape(1, cd))


def conv_bwd(cfg, xbc, w, b, dxc, *, name):
    lp, cd, kk = cfg.lp, cfg.conv_dim, cfg.convk
    cb = _pick(cd, 512, LANE)

    def body(x_ref, w_ref, b_ref, d_ref, dx_ref, dw_ref, db_ref):
        x = x_ref[...]
        rows = lax.broadcasted_iota(jnp.int32, x.shape, 0)
        pre = _conv_pre(x, w_ref, b_ref, rows, kk)
        dpre = d_ref[...] * _dsilu(pre)
        dx = jnp.zeros_like(x)
        dws = []
        for k in range(kk):
            s = kk - 1 - k
            dx = dx + w_ref[k:k + 1, :] * _shift_up(dpre, s, rows)
            dws.append(jnp.sum(dpre * _shift_down(x, s, rows), axis=0, keepdims=True))
        dx_ref[...] = dx.astype(BF16)

        @pl.when(pl.program_id(1) == 0)
        def _():
            dw_ref[...] = jnp.zeros_like(dw_ref)
            db_ref[...] = jnp.zeros_like(db_ref)

        for k in range(kk):
            dw_ref[k:k + 1, :] += dws[k]
        db_ref[...] += jnp.sum(dpre, axis=0, keepdims=True)

    blk = pl.BlockSpec((lp, cb), lambda j, bb: (bb, j))
    wsp = pl.BlockSpec((kk, cb), lambda j, bb: (0, j))
    bsp = pl.BlockSpec((1, cb), lambda j, bb: (0, j))
    dx, dw, db = pl.pallas_call(
        body, name=name, grid=(cd // cb, cfg.bsz),
        in_specs=[blk, wsp, bsp, blk], out_specs=[blk, wsp, bsp],
        out_shape=[_sds((cfg.t, cd), BF16), _sds((kk, cd), F32), _sds((1, cd), F32)], compiler_params=_cp(),
    )(xbc, w, b.reshape(1, cd), dxc)
    return dx, dw, db[0]


def _softplus(x):
    return jnp.maximum(x, 0.0) + jnp.log(1.0 + jnp.exp(-jnp.abs(x)))


def _ssd_consts(cfg):
    q = cfg.chunk
    i0 = np.arange(q)[:, None]
    i1 = np.arange(q)[None, :]
    ltri = (i1 <= i0).astype(np.float32)
    rexp = np.zeros((LANE, cfg.inner), np.float32)
    for h in range(cfg.heads):
        rexp[h, h * cfg.hd:(h + 1) * cfg.hd] = 1.0
    return jnp.asarray(ltri), jnp.asarray(rexp)


def _ssd_chunk_common(cfg, raw, bias, avec, c_idx, ltri, rexp):
    q = cfg.chunk
    rows = lax.broadcasted_iota(jnp.int32, (q, LANE), 0)
    live = jnp.logical_or(c_idx > 0, rows >= cfg.pad)
    pre = raw + bias
    dt = jnp.where(live, _softplus(pre), 0.0)
    adt = dt * avec
    cs = jnp.dot(ltri, adt, precision=HI, preferred_element_type=F32)
    cs_t = cs.T
    cs_last = cs[q - 1:q, :]
    e_in = jnp.exp(cs)
    w0 = jnp.exp(cs_last - cs)
    decay = jnp.exp(cs_last)
    ex = functools.partial(jnp.dot, precision=HI, preferred_element_type=F32)
    return dict(live=live, pre=pre, dt=dt, adt=adt, cs=cs, cs_t=cs_t, e_in=e_in, w0=w0, decay=decay,
                DT=ex(dt, rexp), E=ex(e_in, rexp), W0=ex(w0, rexp),
                DEC=ex(jnp.broadcast_to(decay, (8, LANE)), rexp)[0:1, :])


def _tri_masks(q):
    r = lax.broadcasted_iota(jnp.int32, (q, q), 0)
    c = lax.broadcasted_iota(jnp.int32, (q, q), 1)
    return c <= r, r <= c


def _head_l(cq, h, tri, tri_t):
    col = cq["cs"][:, h:h + 1]
    row = cq["cs_t"][h:h + 1, :]
    lmat = jnp.where(tri, jnp.exp(jnp.minimum(col - row, 0.0)), 0.0)
    lmat_t = jnp.where(tri_t, jnp.exp(jnp.minimum(row - col, 0.0)), 0.0)
    return lmat, lmat_t


def _nt(a, b):
    return lax.dot_general(a, b, (((1,), (1,)), ((), ())), preferred_element_type=F32)


def _tn(a, b):
    return lax.dot_general(a, b, (((0,), (0,)), ((), ())), preferred_element_type=F32)


def _nn(a, b):
    return jnp.dot(a, b, preferred_element_type=F32)


def ssd_fwd(cfg, xc, small, dt_bias, avec, dexp, *, name):
    q, inner, st, gw, g_n = cfg.chunk, cfg.inner, cfg.state, cfg.gw, cfg.groups
    nc = cfg.nchunks
    ltri, rexp = _ssd_consts(cfg)
    hpt = LANE // cfg.hd
    tiles_per_group = gw // LANE

    def body(x_ref, b_ref, c_ref, dt_ref, bias_ref, a_ref, d_ref, ltri_ref, rexp_ref, y_ref, sin_ref, s_scr):
        c_idx = pl.program_id(1)

        @pl.when(c_idx == 0)
        def _():
            s_scr[...] = jnp.zeros_like(s_scr)

        ltri_v = ltri_ref[...]
        tri, tri_t = _tri_masks(q)
        cq = _ssd_chunk_common(cfg, dt_ref[...], bias_ref[...], a_ref[...], c_idx, ltri_v, rexp_ref[...])
        xs = x_ref[...]
        xdt = (xs * cq["DT"]).astype(BF16)
        xw = (xs * cq["DT"] * cq["W0"]).astype(BF16)
        s_in = s_scr[...]
        sin_ref[0] = s_in
        lane = lax.broadcasted_iota(jnp.int32, (q, LANE), 1)
        for g in range(g_n):
            bg = b_ref[:, g * st:(g + 1) * st].astype(BF16)
            cg = c_ref[:, g * st:(g + 1) * st].astype(BF16)
            gmat = _nt(cg, bg)
            gs = slice(g * gw, (g + 1) * gw)
            y0 = _nn(cg, s_in[:, gs].astype(BF16))
            for tt in range(tiles_per_group):
                tile = g * tiles_per_group + tt
                ts = slice(tile * LANE, (tile + 1) * LANE)
                xt = xdt[:, ts]
                yd = None
                for hh in range(hpt):
                    h = tile * hpt + hh
                    lmat, _ = _head_l(cq, h, tri, tri_t)
                    part = _nn((gmat * lmat).astype(BF16), xt)
                    if yd is None:
                        yd = part
                    else:
                        yd = jnp.where(lane < (hh * cfg.hd), yd, part)
                y_ref[:, ts] = yd + y0[:, tt * LANE:(tt + 1) * LANE] * cq["E"][:, ts] + xs[:, ts] * d_ref[:, ts]
            s_scr[:, gs] = s_in[:, gs] * cq["DEC"][:, gs] + _tn(bg, xw[:, gs])

    def rowblk(width, col):
        return pl.BlockSpec((q, width), lambda b, c: (b * nc + c, col))

    def const(shape):
        return pl.BlockSpec(shape, lambda b, c: (0, 0))

    y, sin = pl.pallas_call(
        body, name=name, grid=(cfg.bsz, nc),
        in_specs=[rowblk(inner, 0),
                  pl.BlockSpec((q, g_n * st), lambda b, c: (b * nc + c, inner // (g_n * st))),
                  pl.BlockSpec((q, g_n * st), lambda b, c: (b * nc + c, inner // (g_n * st) + 1)),
                  rowblk(LANE, cfg.dtt), const((1, LANE)), const((1, LANE)), const((1, inner)),
                  const((q, q)), const((LANE, inner))],
        out_specs=[rowblk(inner, 0), pl.BlockSpec((1, st, inner), lambda b, c: (b * nc + c, 0, 0))],
        out_shape=[_sds((cfg.t, inner), F32), _sds((cfg.bsz * nc, st, inner), F32)],
        scratch_shapes=[pltpu.VMEM((st, inner), F32)], compiler_params=_cp(),
    )(xc, xc, xc, small, dt_bias, avec, dexp, ltri, rexp)
    return y, sin


def ssd_bwd(cfg, xc, small, dt_bias, avec, dexp, sin, dy, *, name):
    q, inner, st, gw, g_n = cfg.chunk, cfg.inner, cfg.state, cfg.gw, cfg.groups
    nc = cfg.nchunks
    ltri, rexp = _ssd_consts(cfg)
    rexp_t = rexp.T
    hpt = LANE // cfg.hd
    tiles_per_group = gw // LANE
    bcw = g_n * st

    def body(x_ref, b_ref, c_ref, dt_ref, bias_ref, a_ref, d_ref, ltri_ref, rexp_ref, rexpt_ref, sin_ref, dy_ref,
             dx_ref, db_ref, dc_ref, ddt_ref, dd_ref, da_ref, dbias_ref, ds_scr):
        step = pl.program_id(1)
        c_idx = nc - 1 - step

        @pl.when(step == 0)
        def _():
            ds_scr[...] = jnp.zeros_like(ds_scr)

        @pl.when(jnp.logical_and(step == 0, pl.program_id(0) == 0))
        def _():
            dd_ref[...] = jnp.zeros_like(dd_ref)
            da_ref[...] = jnp.zeros_like(da_ref)
            dbias_ref[...] = jnp.zeros_like(dbias_ref)

        ltri_v = ltri_ref[...]
        tri, tri_t = _tri_masks(q)
        red = functools.partial(jnp.dot, precision=HI, preferred_element_type=F32)
        rexpt = rexpt_ref[...]
        cq = _ssd_chunk_common(cfg, dt_ref[...], bias_ref[...], a_ref[...], c_idx, ltri_v, rexp_ref[...])
        xs = x_ref[...]
        dyv = dy_ref[...]
        s_in = sin_ref[0]
        d_s = ds_scr[...]
        xdt_f = xs * cq["DT"]
        xdt = xdt_f.astype(BF16)
        xw_f = xdt_f * cq["W0"]
        xw = xw_f.astype(BF16)
        lane = lax.broadcasted_iota(jnp.int32, (q, LANE), 1)
        sub = lax.broadcasted_iota(jnp.int32, (LANE, q), 0)

        dd_ref[...] += jnp.sum(dyv * xs, axis=0, keepdims=True)
        dy0 = dyv * cq["E"]
        dcs = jnp.zeros((q, LANE), F32)
        dcs_t = jnp.zeros((LANE, q), F32)
        for g in range(g_n):
            bg_f = b_ref[:, g * st:(g + 1) * st]
            cg_f = c_ref[:, g * st:(g + 1) * st]
            bg = bg_f.astype(BF16)
            cg = cg_f.astype(BF16)
            gs = slice(g * gw, (g + 1) * gw)
            gmat = _nt(cg, bg)
            gmat_t = _nt(bg, cg)
            sing = s_in[:, gs].astype(BF16)
            dsg = d_s[:, gs].astype(BF16)
            y0 = _nn(cg, sing)
            dxw = _nn(bg, dsg)
            d_bg = _nt(xw[:, gs], dsg)
            d_cg = _nt(dy0[:, gs].astype(BF16), sing)
            ds_in_g = _tn(cg, dy0[:, gs].astype(BF16))
            dg = jnp.zeros((q, q), F32)
            dxdt_g = []
            for tt in range(tiles_per_group):
                tile = g * tiles_per_group + tt
                ts = slice(tile * LANE, (tile + 1) * LANE)
                xt = xdt[:, ts]
                dyt = dyv[:, ts]
                dxdt_t = None
                for hh in range(hpt):
                    h = tile * hpt + hh
                    lmat, lmat_t = _head_l(cq, h, tri, tri_t)
                    inhead = jnp.logical_and(lane >= hh * cfg.hd, lane < (hh + 1) * cfg.hd)
                    dyh = jnp.where(inhead, dyt, 0.0).astype(BF16)
                    dm = _nt(dyh, xt)
                    dg = dg + dm * lmat
                    qm = dm * gmat * lmat
                    rs = jnp.sum(qm, axis=1, keepdims=True)
                    csum = jnp.sum(qm, axis=0, keepdims=True)
                    dcs = dcs + jnp.where(lane == h, rs, 0.0)
                    dcs_t = dcs_t + jnp.where(sub == h, csum, 0.0)
                    part = _nn((gmat_t * lmat_t).astype(BF16), dyh)
                    dxdt_t = part if dxdt_t is None else dxdt_t + part
                dxdt_g.append(dxdt_t)
            dxdt_diag = jnp.concatenate(dxdt_g, axis=1) if len(dxdt_g) > 1 else dxdt_g[0]
            dgb = dg.astype(BF16)
            d_cg = d_cg + _nn(dgb, bg)
            d_bg = d_bg + _tn(dgb, cg)
            db_ref[:, g * st:(g + 1) * st] = d_bg
            dc_ref[:, g * st:(g + 1) * st] = d_cg
            dxdt = dxdt_diag + dxw * cq["W0"][:, gs]
            dx_ref[:, gs] = dyv[:, gs] * d_ref[:, gs] + dxdt * cq["DT"][:, gs]
            rt = rexpt[gs, :]
            dcs = dcs + red(dyv[:, gs] * y0 * cq["E"][:, gs], rt)
            r_w = red(dxw * xw_f[:, gs], rt)
            dcs = dcs - r_w
            dcs_last_g = jnp.sum(r_w, axis=0, keepdims=True)
            ddec = red(jnp.broadcast_to(jnp.sum(d_s[:, gs] * s_in[:, gs], axis=0, keepdims=True), (8, gw)), rt)[0:1, :]
            dcs_last_g = dcs_last_g + ddec * cq["decay"]
            dcs = dcs + jnp.where(lax.broadcasted_iota(jnp.int32, (q, LANE), 0) == q - 1, dcs_last_g, 0.0)
            ddt_part = red(dxdt * xs[:, gs], rt)
            if g == 0:
                ddt = ddt_part
            else:
                ddt = ddt + ddt_part
            ds_scr[:, gs] = d_s[:, gs] * cq["DEC"][:, gs] + ds_in_g
        dcs = dcs - dcs_t.T
        dadt = lax.dot_general(ltri_v, dcs, (((0,), (0,)), ((), ())), precision=HI,
                               preferred_element_type=F32)
        ddt = ddt + dadt * a_ref[...]
        da_ref[...] += jnp.sum(dadt * cq["dt"], axis=0, keepdims=True)
        draw = jnp.where(cq["live"], ddt * jax.nn.sigmoid(cq["pre"]), 0.0)
        ddt_ref[...] = draw
        dbias_ref[...] += jnp.sum(draw, axis=0, keepdims=True)

    def rowblk(width, col):
        return pl.BlockSpec((q, width), lambda b, s: (b * nc + nc - 1 - s, col))

    def const(shape):
        return pl.BlockSpec(shape, lambda b, s: (0, 0))

    bcol = inner // bcw
    outs = pl.pallas_call(
        body, name=name, grid=(cfg.bsz, nc),
        in_specs=[rowblk(inner, 0), rowblk(bcw, bcol), rowblk(bcw, bcol + 1), rowblk(LANE, cfg.dtt),
                  const((1, LANE)), const((1, LANE)), const((1, inner)), const((q, q)), const((LANE, inner)),
                  const((inner, LANE)),
                  pl.BlockSpec((1, st, inner), lambda b, s: (b * nc + nc - 1 - s, 0, 0)), rowblk(inner, 0)],
        out_specs=[rowblk(inner, 0), rowblk(bcw, 0), rowblk(bcw, 0), rowblk(LANE, 0),
                   const((1, inner)), const((1, LANE)), const((1, LANE))],
        out_shape=[_sds((cfg.t, inner), F32), _sds((cfg.t, bcw), F32), _sds((cfg.t, bcw), F32),
                   _sds((cfg.t, LANE), F32), _sds((1, inner), F32), _sds((1, LANE), F32), _sds((1, LANE), F32)],
        scratch_shapes=[pltpu.VMEM((st, inner), F32)], compiler_params=_cp(),
    )(xc, xc, xc, small, dt_bias, avec, dexp, ltri, rexp, rexp_t, sin, dy)
    return outs


def tail_fwd(cfg, y, z, w, *, name):
    t, inner, gw = cfg.t, cfg.inner, cfg.gw
    tr = _pick(t, 272, 16)

    def body(y_ref, z_ref, w_ref, o_ref):
        for g in range(cfg.groups):
            gs = slice(g * gw, (g + 1) * gw)
            yg = y_ref[:, gs] * _silu(z_ref[:, gs])
            r = lax.rsqrt(jnp.mean(yg * yg, axis=-1, keepdims=True) + EPS)
            o_ref[:, gs] = (yg * r * w_ref[:, gs]).astype(BF16)

    row = pl.BlockSpec((tr, inner), lambda i: (i, 0))
    return pl.pallas_call(
        body, name=name, grid=(t // tr,), in_specs=[row, row, pl.BlockSpec((1, inner), lambda i: (0, 0))],
        out_specs=row, out_shape=_sds((t, inner), BF16), compiler_params=_cp(),
    )(y, z, w.reshape(1, inner))


def tail_bwd(cfg, do, y, z, w, *, name):
    t, inner, gw = cfg.t, cfg.inner, cfg.gw
    tr = _pick(t, 272, 16)

    def body(do_ref, y_ref, z_ref, w_ref, dy_ref, dz_ref, dw_ref):
        @pl.when(pl.program_id(0) == 0)
        def _():
            dw_ref[...] = jnp.zeros_like(dw_ref)

        for g in range(cfg.groups):
            gs = slice(g * gw, (g + 1) * gw)
            yv = y_ref[:, gs]
            zv = z_ref[:, gs]
            dov = do_ref[:, gs]
            sz = _silu(zv)
            yg = yv * sz
            r = lax.rsqrt(jnp.mean(yg * yg, axis=-1, keepdims=True) + EPS)
            xh = yg * r
            gg = dov * w_ref[:, gs]
            dyg = r * (gg - xh * jnp.mean(gg * xh, axis=-1, keepdims=True))
            dw_ref[:, gs] += jnp.sum(dov * xh, axis=0, keepdims=True)
            dy_ref[:, gs] = dyg * sz
            dz_ref[:, gs] = (dyg * yv * _dsilu(zv)).astype(BF16)

    row = pl.BlockSpec((tr, inner), lambda i: (i, 0))
    vec = pl.BlockSpec((1, inner), lambda i: (0, 0))
    dy, dz, dw = pl.pallas_call(
        body, name=name, grid=(t // tr,), in_specs=[row, row, row, vec], out_specs=[row, row, vec],
        out_shape=[_sds((t, inner), F32), _sds((t, inner), BF16), _sds((1, inner), F32)], compiler_params=_cp(),
    )(do, y, z, w.reshape(1, inner))
    return dy, dz, dw[0]


def rope_tables(cfg):
    half = cfg.rope // 2
    pos = np.maximum(np.arange(cfg.lp) - cfg.pad, 0).astype(np.float32)
    inv = ROPE_THETA ** (-jnp.arange(0, cfg.rope, 2, dtype=F32) / cfg.rope)
    ang = jnp.asarray(pos)[:, None] * inv[None, :]
    cos, sin = jnp.cos(ang), jnp.sin(ang)
    zero = jnp.zeros((cfg.lp, LANE - 2 * half), F32)
    zh = jnp.zeros((cfg.lp, half), F32)
    ctab = jnp.concatenate([cos, cos, zero], axis=1)
    s1 = jnp.concatenate([-sin, zh, zero], axis=1)
    s2 = jnp.concatenate([zh, sin, zero], axis=1)
    return ctab, s1, s2


def _rope(x, c, s1, s2, half):
    return x * c + pltpu.roll(x, LANE - half, 1) * s1 + pltpu.roll(x, half, 1) * s2


def _rope_t(dy, c, s1, s2, half):
    return dy * c + pltpu.roll(dy * s1, half, 1) + pltpu.roll(dy * s2, LANE - half, 1)


def rope_fwd(cfg, qf, small, tabs, *, name):
    t, qw, lp = cfg.t, cfg.qw, cfg.lp
    tr = _pick(lp, 544, 16)
    nrb = lp // tr
    half = cfg.rope // 2

    def body(q_ref, k_ref, c_ref, s1_ref, s2_ref, qo_ref, ko_ref):
        c, s1, s2 = c_ref[...], s1_ref[...], s2_ref[...]
        for h in range(cfg.mh):
            a = h * 2 * LANE
            qo_ref[:, a:a + LANE] = q_ref[:, a:a + LANE].astype(BF16)
            qo_ref[:, a + LANE:a + 2 * LANE] = _rope(q_ref[:, a + LANE:a + 2 * LANE], c, s1, s2, half).astype(BF16)
        ko_ref[...] = _rope(k_ref[...], c, s1, s2, half).astype(BF16)

    tab = pl.BlockSpec((tr, LANE), lambda i: (i % nrb, 0))
    return pl.pallas_call(
        body, name=name, grid=(t // tr,),
        in_specs=[pl.BlockSpec((tr, qw), lambda i: (i, 0)), pl.BlockSpec((tr, LANE), lambda i: (i, cfg.kt)), tab, tab, tab],
        out_specs=[pl.BlockSpec((tr, qw), lambda i: (i, 0)), pl.BlockSpec((tr, LANE), lambda i: (i, 0))],
        out_shape=[_sds((t, qw), BF16), _sds((t, LANE), BF16)], compiler_params=_cp(),
    )(qf, small, *tabs)


def rope_bwd(cfg, dq, dkpe, tabs, *, name):
    t, qw, lp = cfg.t, cfg.qw, cfg.lp
    tr = _pick(lp, 544, 16)
    nrb = lp // tr
    half = cfg.rope // 2

    def body(dq_ref, dk_ref, c_ref, s1_ref, s2_ref, qo_ref, ko_ref):
        c, s1, s2 = c_ref[...], s1_ref[...], s2_ref[...]
        for h in range(cfg.mh):
            a = h * 2 * LANE
            qo_ref[:, a:a + LANE] = dq_ref[:, a:a + LANE].astype(BF16)
            qo_ref[:, a + LANE:a + 2 * LANE] = _rope_t(dq_ref[:, a + LANE:a + 2 * LANE], c, s1, s2, half).astype(BF16)
        dk = dk_ref[0]
        for h in range(1, cfg.mh):
            dk = dk + dk_ref[h]
        ko_ref[...] = _rope_t(dk, c, s1, s2, half)

    tab = pl.BlockSpec((tr, LANE), lambda i: (i % nrb, 0))
    return pl.pallas_call(
        body, name=name, grid=(t // tr,),
        in_specs=[pl.BlockSpec((tr, qw), lambda i: (i, 0)), pl.BlockSpec((cfg.mh, tr, LANE), lambda i: (0, i, 0)),
                  tab, tab, tab],
        out_specs=[pl.BlockSpec((tr, qw), lambda i: (i, 0)), pl.BlockSpec((tr, LANE), lambda i: (i, 0))],
        out_shape=[_sds((t, qw), BF16), _sds((t, LANE), F32)], compiler_params=_cp(),
    )(dq, dkpe, *tabs)


def _q_blocks(cfg):
    bounds = [0, cfg.chunk] + list(range(cfg.chunk + 256, cfg.lp + 1, 256))
    assert bounds[-1] == cfg.lp, "SEQ must be a multiple of 256"
    return list(zip(bounds[:-1], bounds[1:]))


def _attn_mask(cfg, qs, qe):
    rows = qs + lax.broadcasted_iota(jnp.int32, (qe - qs, qe), 0)
    cols = lax.broadcasted_iota(jnp.int32, (qe - qs, qe), 1)
    return jnp.logical_and(cols <= rows, jnp.logical_or(cols >= cfg.pad, rows < cfg.pad))


def attn_fwd(cfg, qr, kv, kpe, *, name):
    lp, t, mh = cfg.lp, cfg.t, cfg.mh
    scale = (cfg.nope + cfg.rope) ** -0.5
    blocks = _q_blocks(cfg)

    def body(q_ref, kv_ref, kp_ref, o_ref, l_ref):
        for qs, qe in blocks:
            n = qe
            q = q_ref[qs:qe, :]
            k2 = jnp.concatenate([kv_ref[0:n, 0:LANE], kp_ref[0:n, :]], axis=1)
            s = _nt(q, k2) * scale
            s = jnp.where(_attn_mask(cfg, qs, qe), s, MASK_VALUE)
            m = jnp.max(s, axis=-1, keepdims=True)
            p = jnp.exp(s - m)
            l = jnp.sum(p, axis=-1, keepdims=True)
            pn = (p * (1.0 / l)).astype(BF16)
            o_ref[qs:qe, :] = _nn(pn, kv_ref[0:n, LANE:2 * LANE])
            l_ref[qs:qe, :] = jnp.broadcast_to(m + jnp.log(l), (qe - qs, LANE))

    hb = pl.BlockSpec((lp, 2 * LANE), lambda b, h: (b, h))
    ob = pl.BlockSpec((lp, LANE), lambda b, h: (b, h))
    return pl.pallas_call(
        body, name=name, grid=(cfg.bsz, mh),
        in_specs=[hb, hb, pl.BlockSpec((lp, LANE), lambda b, h: (b, 0))], out_specs=[ob, ob],
        out_shape=[_sds((t, mh * LANE), F32), _sds((t, mh * LANE), F32)], compiler_params=_cp(),
    )(qr, kv, kpe)


def attn_bwd(cfg, qr, kv, kpe, o, lse, do, *, name):
    lp, t, mh = cfg.lp, cfg.t, cfg.mh
    scale = (cfg.nope + cfg.rope) ** -0.5
    blocks = _q_blocks(cfg)

    def body(q_ref, kv_ref, kp_ref, o_ref, l_ref, do_ref, dq_ref, dkv_ref, dkp_ref, dk_acc, dv_acc):
        dk_acc[...] = jnp.zeros_like(dk_acc)
        dv_acc[...] = jnp.zeros_like(dv_acc)
        for qs, qe in blocks:
            n = qe
            q = q_ref[qs:qe, :]
            k2 = jnp.concatenate([kv_ref[0:n, 0:LANE], kp_ref[0:n, :]], axis=1)
            dov = do_ref[qs:qe, :]
            delta = jnp.sum(dov * o_ref[qs:qe, :], axis=-1, keepdims=True)
            dob = dov.astype(BF16)
            s = _nt(q, k2) * scale
            s = jnp.where(_attn_mask(cfg, qs, qe), s, MASK_VALUE)
            p = jnp.exp(s - l_ref[qs:qe, 0:1])
            dp = _nt(dob, kv_ref[0:n, LANE:2 * LANE])
            ds = (p * (dp - delta) * scale).astype(BF16)
            dq_ref[qs:qe, :] = _nn(ds, k2)
            dv_acc[0:n, :] += _tn(p.astype(BF16), dob)
            dk_acc[0:n, :] += _tn(ds, q)
        dkv_ref[:, 0:LANE] = dk_acc[:, 0:LANE].astype(BF16)
        dkv_ref[:, LANE:2 * LANE] = dv_acc[...].astype(BF16)
        dkp_ref[0] = dk_acc[:, LANE:2 * LANE]

    hb = pl.BlockSpec((lp, 2 * LANE), lambda b, h: (b, h))
    ob = pl.BlockSpec((lp, LANE), lambda b, h: (b, h))
    return pl.pallas_call(
        body, name=name, grid=(cfg.bsz, mh),
        in_specs=[hb, hb, pl.BlockSpec((lp, LANE), lambda b, h: (b, 0)), ob, ob, ob],
        out_specs=[hb, hb, pl.BlockSpec((1, lp, LANE), lambda b, h: (h, b, 0))],
        out_shape=[_sds((t, cfg.qw), F32), _sds((t, mh * 2 * LANE), BF16), _sds((mh, t, LANE), F32)],
        scratch_shapes=[pltpu.VMEM((lp, 2 * LANE), F32), pltpu.VMEM((lp, LANE), F32)], compiler_params=_cp(),
    )(qr, kv, kpe, o, lse, do)


def _live_rows(cfg, tr, shape):
    rows = pl.program_id(1) * tr + lax.broadcasted_iota(jnp.int32, shape, 0)
    return rows >= cfg.pad


def gate_fwd(cfg, ya, yb, g, *, name):
    d, lp = cfg.d, cfg.lp
    tr = _pick(lp, 544, 16)
    nrb = lp // tr

    def body(ya_ref, yb_ref, ga_ref, gb_ref, o_ref):
        mix = jax.nn.sigmoid(ga_ref[...]) * ya_ref[...] + jax.nn.sigmoid(gb_ref[...]) * yb_ref[...]
        o_ref[...] = jnp.where(_live_rows(cfg, tr, mix.shape), mix, 0.0).astype(BF16)

    row = pl.BlockSpec((tr, d), lambda b, j: (b * nrb + j, 0))
    row1 = pl.BlockSpec((tr, d), lambda b, j: (b * nrb + j, 1))
    return pl.pallas_call(
        body, name=name, grid=(cfg.bsz, nrb), in_specs=[row, row, row, row1], out_specs=row,
        out_shape=_sds((cfg.t, d), BF16), compiler_params=_cp(),
    )(ya, yb, g, g)


def gate_bwd(cfg, dmix, ya, yb, g, *, name):
    d, lp = cfg.d, cfg.lp
    tr = _pick(lp, 544, 16)
    nrb = lp // tr

    def body(dm_ref, ya_ref, yb_ref, ga_ref, gb_ref, dya_ref, dyb_ref, dg_ref):
        dm = dm_ref[...]
        dm = jnp.where(_live_rows(cfg, tr, dm.shape), dm, 0.0)
        sa = jax.nn.sigmoid(ga_ref[...])
        sb = jax.nn.sigmoid(gb_ref[...])
        dya_ref[...] = (dm * sa).astype(BF16)
        dyb_ref[...] = (dm * sb).astype(BF16)
        dg_ref[:, 0:d] = (dm * ya_ref[...] * sa * (1.0 - sa)).astype(BF16)
        dg_ref[:, d:2 * d] = (dm * yb_ref[...] * sb * (1.0 - sb)).astype(BF16)

    row = pl.BlockSpec((tr, d), lambda b, j: (b * nrb + j, 0))
    row1 = pl.BlockSpec((tr, d), lambda b, j: (b * nrb + j, 1))
    row2 = pl.BlockSpec((tr, 2 * d), lambda b, j: (b * nrb + j, 0))
    return pl.pallas_call(
        body, name=name, grid=(cfg.bsz, nrb), in_specs=[row, row, row, row, row1], out_specs=[row, row, row2],
        out_shape=[_sds((cfg.t, d), BF16), _sds((cfg.t, d), BF16), _sds((cfg.t, 2 * d), BF16)], compiler_params=_cp(),
    )(dmix, ya, yb, g, g)


def relu2_fwd(a, *, name):
    t, f = a.shape
    tr = _pick(t, 272, 16)

    def body(a_ref, o_ref):
        r = jnp.maximum(a_ref[...], 0.0)
        o_ref[...] = (r * r).astype(BF16)

    row = pl.BlockSpec((tr, f), lambda i: (i, 0))
    return pl.pallas_call(body, name=name, grid=(t // tr,), in_specs=[row], out_specs=row,
                          out_shape=_sds((t, f), BF16), compiler_params=_cp())(a)


def relu2_bwd(dact, a, *, name):
    t, f = a.shape
    tr = _pick(t, 272, 16)

    def body(d_ref, a_ref, o_ref):
        o_ref[...] = (d_ref[...] * 2.0 * jnp.maximum(a_ref[...], 0.0)).astype(BF16)

    row = pl.BlockSpec((tr, f), lambda i: (i, 0))
    return pl.pallas_call(body, name=name, grid=(t // tr,), in_specs=[row, row], out_specs=row,
                          out_shape=_sds((t, f), BF16), compiler_params=_cp())(dact, a)


def loss_head(cfg, h, target, w, *, name):
    d, q, nc = cfg.d, cfg.chunk, cfg.nchunks
    tpb = cfg.seq // q

    def body(h_ref, t_ref, w_ref, loss_ref, dh_ref, dw_ref):
        j = pl.program_id(1)

        @pl.when(jnp.logical_and(j == 0, pl.program_id(0) == 0))
        def _():
            loss_ref[...] = jnp.zeros_like(loss_ref)
            dw_ref[...] = jnp.zeros_like(dw_ref)

        @pl.when(j == 0)
        def _():
            dh_ref[...] = jnp.zeros_like(dh_ref)

        @pl.when(j > 0)
        def _():
            xv = h_ref[...]
            r = lax.rsqrt(jnp.mean(xv * xv, axis=-1, keepdims=True) + EPS)
            xh = xv * r
            err = xh * w_ref[...] - t_ref[...]
            loss_ref[...] += 0.5 * jnp.sum(jnp.sum(err * err, axis=-1, keepdims=True) / d, axis=0, keepdims=True)
            dy = err * (1.0 / d)
            g = dy * w_ref[...]
            dh_ref[...] = r * (g - xh * jnp.mean(g * xh, axis=-1, keepdims=True))
            dw_ref[...] += jnp.sum(dy * xh, axis=0, keepdims=True)

    row = pl.BlockSpec((q, d), lambda b, j: (b * nc + j, 0))
    loss, dh, dw = pl.pallas_call(
        body, name=name, grid=(cfg.bsz, nc),
        in_specs=[row, pl.BlockSpec((q, d), lambda b, j: (b * tpb + jnp.maximum(j - 1, 0), 0)),
                  pl.BlockSpec((1, d), lambda b, j: (0, 0))],
        out_specs=[pl.BlockSpec((8, LANE), lambda b, j: (0, 0)), row, pl.BlockSpec((1, d), lambda b, j: (0, 0))],
        out_shape=[_sds((8, LANE), F32), _sds((cfg.t, d), F32), _sds((1, d), F32)], compiler_params=_cp(),
    )(h, target, w.reshape(1, d))
    return loss[0, 0], dh, dw[0]


def _rows_tile(r, c):
    return _pick(r, max(8, (1 << 18) // max(c, 1) // 8 * 8), 8)


def _adam_update(w, g, m, v):
    c1 = 1.0 - ADAM_B1 ** ADAM_STEP
    c2 = 1.0 - ADAM_B2 ** ADAM_STEP
    mn = ADAM_B1 * m + (1.0 - ADAM_B1) * g
    vn = ADAM_B2 * v + (1.0 - ADAM_B2) * (g * g)
    delta = -ADAM_LR * ((mn / c1) / (jnp.sqrt(vn / c2) + ADAM_EPS) + ADAM_WD * w)
    return delta, mn, vn


def adamw_layer(w, m, v, g, li, prev, dep, *, name):
    _, r, c = w.shape
    tr = _rows_tile(r, c)

    def body(*refs):
        w_ref, m_ref, v_ref, g_ref = refs[:4]
        go_ref, d_ref, mo_ref, vo_ref = refs[-4:]
        gv = g_ref[...]
        delta, mn, vn = _adam_update(w_ref[0], gv, m_ref[0], v_ref[0])
        go_ref[0] = gv
        d_ref[0] = delta
        mo_ref[0] = mn
        vo_ref[0] = vn

    blk3 = pl.BlockSpec((1, tr, c), lambda i: (li, i, 0))
    anyspec = pl.BlockSpec(memory_space=pl.ANY)
    in_specs = [blk3, blk3, blk3, pl.BlockSpec((tr, c), lambda i: (i, 0)), anyspec]
    args = [w, m, v, g, dep]
    aliases = {}
    if prev is not None:
        in_specs += [anyspec] * 4
        args += list(prev)
        aliases = {5 + i: i for i in range(4)}
    return pl.pallas_call(
        body, name=name, grid=(r // tr,), in_specs=in_specs, out_specs=[blk3] * 4,
        out_shape=[_sds(w.shape, F32)] * 4, input_output_aliases=aliases, compiler_params=_cp(),
    )(*args)


def pair_add(g4, other, half, *, name):
    n, _, r, c = g4.shape
    tr = _rows_tile(r, c)

    def body(h_ref, a_ref, b_ref, o_ref):
        o_ref[0] = a_ref[0, 0] + b_ref[0]

    blk = pl.BlockSpec((1, tr, c), lambda j, i, h: (j, i, 0))
    grid_spec = pltpu.PrefetchScalarGridSpec(
        num_scalar_prefetch=1, grid=(n, r // tr),
        in_specs=[pl.BlockSpec((1, 1, tr, c), lambda j, i, h: (j, h[0], i, 0)), blk], out_specs=blk)
    return pl.pallas_call(body, name=name, grid_spec=grid_spec, out_shape=_sds((n, r, c), F32),
                          compiler_params=_cp())(half, g4, other)


def chip_sum(recv, part, where, *, name):
    n, r, c = recv.shape
    tr = _rows_tile(r, c)

    def body(s_ref, *refs):
        own_ref, o_ref = refs[n], refs[n + 1]
        acc = None
        for j in range(n):
            term = jnp.where(s_ref[0] == j, own_ref[0], refs[j][0])
            acc = term if acc is None else acc + term
        o_ref[0] = acc

    def slot(j):
        return pl.BlockSpec((1, tr, c), lambda i, s: (jnp.where(s[0] == j, (j + 1) % n, j), i, 0))

    grid_spec = pltpu.PrefetchScalarGridSpec(
        num_scalar_prefetch=1, grid=(r // tr,),
        in_specs=[slot(j) for j in range(n)] + [pl.BlockSpec((1, tr, c), lambda i, s: (s[0], i, 0))],
        out_specs=pl.BlockSpec((1, tr, c), lambda i, s: (s[1], i, 0)))
    return pl.pallas_call(body, name=name, grid_spec=grid_spec, out_shape=_sds((2, r, c), F32),
                          compiler_params=_cp())(where, *([recv] * n), part)


def _coords():
    return lax.axis_index("x"), lax.axis_index("y"), lax.axis_index("c")


def _other_chips(x, y):
    return [(1 - x, y), (x, 1 - y), (1 - x, 1 - y)]


def gather_chips(arrs, *, name):
    n = len(arrs)
    anyspec = pl.BlockSpec(memory_space=pl.ANY)

    def body(*refs):
        ins, outs = refs[:n], refs[n:2 * n]
        send_sems, recv_sems, local_sems = refs[2 * n:]
        x, y, c = _coords()
        me = 2 * x + y
        chips = _other_chips(x, y)
        copies = []
        for k in range(n):
            loc = pltpu.make_async_copy(ins[k], outs[k].at[me], local_sems.at[k])
            loc.start()
            copies.append(loc)
        sends = []
        for k in range(n):
            for j, (px, py) in enumerate(chips):
                cp = pltpu.make_async_remote_copy(
                    src_ref=ins[k], dst_ref=outs[k].at[me], send_sem=send_sems.at[k, j], recv_sem=recv_sems.at[k, j],
                    device_id=(px, py, c), device_id_type=MESH)
                cp.start()
                sends.append(cp)
        for k in range(n):
            for j, (px, py) in enumerate(chips):
                pltpu.make_async_remote_copy(
                    src_ref=ins[k], dst_ref=outs[k].at[2 * px + py], send_sem=send_sems.at[k, j],
                    recv_sem=recv_sems.at[k, j], device_id=(px, py, c), device_id_type=MESH).wait_recv()
        for cp in sends:
            cp.wait_send()
        for cp in copies:
            cp.wait()

    return pl.pallas_call(
        body, name=name, in_specs=[anyspec] * n, out_specs=[anyspec] * n,
        out_shape=[_sds((4,) + a.shape, a.dtype) for a in arrs],
        scratch_shapes=[pltpu.SemaphoreType.DMA((n, 3)), pltpu.SemaphoreType.DMA((n, 3)), pltpu.SemaphoreType.DMA((n,))],
        compiler_params=_cp(has_side_effects=True),
    )(*arrs)


def allreduce_small(vec, after, *, name):
    r, c = vec.shape

    def body(v_ref, after_ref, o_ref, buf, send_sems, recv_sems):
        x, y, cc = _coords()
        me = 4 * x + 2 * y + cc
        buf[me] = v_ref[...]
        sends = []
        flips = [(fx, fy, fc) for fx in (0, 1) for fy in (0, 1) for fc in (0, 1)][1:]
        for j, (fx, fy, fc) in enumerate(flips):
            peer = ((1 - x) if fx else x, (1 - y) if fy else y, (1 - cc) if fc else cc)
            cp = pltpu.make_async_remote_copy(
                src_ref=v_ref, dst_ref=buf.at[me], send_sem=send_sems.at[j], recv_sem=recv_sems.at[j],
                device_id=peer, device_id_type=MESH)
            cp.start()
            sends.append(cp)
        for j, (fx, fy, fc) in enumerate(flips):
            px, py, pc = ((1 - x) if fx else x, (1 - y) if fy else y, (1 - cc) if fc else cc)
            pltpu.make_async_remote_copy(
                src_ref=v_ref, dst_ref=buf.at[4 * px + 2 * py + pc], send_sem=send_sems.at[j],
                recv_sem=recv_sems.at[j], device_id=(px, py, pc), device_id_type=MESH).wait_recv()
        for cp in sends:
            cp.wait_send()
        acc = buf[0]
        for k in range(1, 8):
            acc = acc + buf[k]
        o_ref[...] = acc

    vm = pl.BlockSpec(memory_space=pltpu.VMEM)
    return pl.pallas_call(
        body, name=name, in_specs=[vm, pl.BlockSpec(memory_space=pl.ANY)], out_specs=vm, out_shape=_sds((r, c), F32),
        scratch_shapes=[pltpu.VMEM((8, r, c), F32), pltpu.SemaphoreType.DMA((7,)), pltpu.SemaphoreType.DMA((7,))],
        compiler_params=_cp(has_side_effects=True),
    )(vec, after)


def pair_exchange(arrs, *, name):
    n = len(arrs)
    anyspec = pl.BlockSpec(memory_space=pl.ANY)

    def body(*refs):
        ins, outs = refs[:n], refs[n:2 * n]
        send_sems, recv_sems = refs[2 * n:]
        x, y, c = _coords()
        sends = []
        for k in range(n):
            for j in range(4):
                cp = pltpu.make_async_remote_copy(
                    src_ref=ins[k].at[j, 1 - c], dst_ref=outs[k].at[j], send_sem=send_sems.at[k, j],
                    recv_sem=recv_sems.at[k, j], device_id=(x, y, 1 - c), device_id_type=MESH)
                cp.start()
                sends.append(cp)
        for cp in sends:
            cp.wait()

    return pl.pallas_call(
        body, name=name, in_specs=[anyspec] * n, out_specs=[anyspec] * n,
        out_shape=[_sds((a.shape[0],) + a.shape[2:], a.dtype) for a in arrs],
        scratch_shapes=[pltpu.SemaphoreType.DMA((n, 4)), pltpu.SemaphoreType.DMA((n, 4))],
        compiler_params=_cp(has_side_effects=True),
    )(*arrs)


def pair_share(lands, *, name):
    n = len(lands)
    anyspec = pl.BlockSpec(memory_space=pl.ANY)

    def body(*refs):
        ins, outs = refs[:n], refs[n:2 * n]
        send_sems, recv_sems = refs[2 * n:]
        x, y, c = _coords()
        sends = []
        for k in range(n):
            for j, (px, py) in enumerate(_other_chips(x, y)):
                cp = pltpu.make_async_remote_copy(
                    src_ref=ins[k].at[2 * px + py, c], dst_ref=outs[k].at[2 * px + py, c], send_sem=send_sems.at[k, j],
                    recv_sem=recv_sems.at[k, j], device_id=(x, y, 1 - c), device_id_type=MESH)
                cp.start()
                sends.append(cp)
        for k in range(n):
            for j, (px, py) in enumerate(_other_chips(x, y)):
                pltpu.make_async_remote_copy(
                    src_ref=ins[k].at[2 * px + py, c], dst_ref=outs[k].at[2 * px + py, 1 - c],
                    send_sem=send_sems.at[k, j], recv_sem=recv_sems.at[k, j], device_id=(x, y, 1 - c),
                    device_id_type=MESH).wait_recv()
        for cp in sends:
            cp.wait_send()

    return pl.pallas_call(
        body, name=name, in_specs=[anyspec] * n, out_specs=[anyspec] * n,
        out_shape=[_sds(a.shape, a.dtype) for a in lands], input_output_aliases={k: k for k in range(n)},
        scratch_shapes=[pltpu.SemaphoreType.DMA((n, 3)), pltpu.SemaphoreType.DMA((n, 3))],
        compiler_params=_cp(has_side_effects=True),
    )(*lands)


def pair_fill(arrs, *, name):
    n = len(arrs)
    anyspec = pl.BlockSpec(memory_space=pl.ANY)

    def body(*refs):
        ins, outs = refs[:n], refs[n:2 * n]
        send_sems, recv_sems = refs[2 * n:]
        x, y, c = _coords()
        sends = []
        for k in range(n):
            cp = pltpu.make_async_remote_copy(
                src_ref=ins[k].at[c], dst_ref=outs[k].at[c], send_sem=send_sems.at[k], recv_sem=recv_sems.at[k],
                device_id=(x, y, 1 - c), device_id_type=MESH)
            cp.start()
            sends.append(cp)
        for k in range(n):
            pltpu.make_async_remote_copy(
                src_ref=ins[k].at[c], dst_ref=outs[k].at[1 - c], send_sem=send_sems.at[k], recv_sem=recv_sems.at[k],
                device_id=(x, y, 1 - c), device_id_type=MESH).wait_recv()
        for cp in sends:
            cp.wait_send()

    return pl.pallas_call(
        body, name=name, in_specs=[anyspec] * n, out_specs=[anyspec] * n,
        out_shape=[_sds(a.shape, a.dtype) for a in arrs], input_output_aliases={k: k for k in range(n)},
        scratch_shapes=[pltpu.SemaphoreType.DMA((n,)), pltpu.SemaphoreType.DMA((n,))],
        compiler_params=_cp(has_side_effects=True),
    )(*arrs)


_HBM = pl.BlockSpec(memory_space=pltpu.HBM)
_SEM = pl.BlockSpec(memory_space=pltpu.SEMAPHORE)


def _ici_copies(kind, srcs, lands, send_sems, recv_sems):
    x, y, c = _coords()
    me = 2 * x + y
    sends, recvs = [], []
    for k in range(len(srcs)):
        for j, (px, py) in enumerate(_other_chips(x, y)):
            peer = 2 * px + py
            if kind == "gather":
                src, there, here = srcs[k].at[c], lands[k].at[me, c], lands[k].at[peer, c]
            else:
                src, there, here = srcs[k].at[peer], lands[k].at[me], lands[k].at[peer]
            sem = 3 * k + j
            mk = functools.partial(pltpu.make_async_remote_copy, src_ref=src, send_sem=send_sems.at[sem],
                                   recv_sem=recv_sems.at[sem], device_id=(px, py, c), device_id_type=MESH)
            sends.append(mk(dst_ref=there))
            recvs.append(mk(dst_ref=here))
    return sends, recvs


def ici_start(kind, srcs, lands, after, *, name):
    n = len(srcs)

    def body(*refs):
        src_refs, land_refs = refs[:n], refs[n:2 * n]
        send_sems, recv_sems = refs[2 * n + 1], refs[2 * n + 2]
        token = refs[-1]
        sends, _ = _ici_copies(kind, src_refs, land_refs, send_sems, recv_sems)
        for cp in sends:
            cp.start()
        token[...] = jnp.zeros_like(token)

    both = list(srcs) + list(lands)
    out = pl.pallas_call(
        body, name=name,
        in_specs=[_HBM] * (2 * n) + [pl.BlockSpec(memory_space=pl.ANY)],
        out_shape=(pltpu.SemaphoreType.DMA((3 * n,)), pltpu.SemaphoreType.DMA((3 * n,)),
                   *[pltpu.HBM(a.shape, a.dtype) for a in both], _sds((8, LANE), F32)),
        out_specs=(_SEM, _SEM, *([_HBM] * (2 * n)), pl.BlockSpec(memory_space=pltpu.VMEM)),
        input_output_aliases={i: 2 + i for i in range(2 * n)},
        compiler_params=_cp(has_side_effects=pltpu.SideEffectType.DATAFLOW_SIDE_EFFECTING),
    )(*[pltpu.with_memory_space_constraint(a, pltpu.HBM) for a in both], after)
    return out[0], out[1], list(out[2:2 + n]), list(out[2 + n:2 + 2 * n]), out[-1]


def ici_wait(kind, started, after, *, name):
    send_sems, recv_sems, srcs, lands, _ = started
    n = len(srcs)

    def body(*refs):
        src_refs, land_refs = refs[:n], refs[n:2 * n]
        sends, recvs = _ici_copies(kind, src_refs, land_refs, refs[2 * n], refs[2 * n + 1])
        for cp in sends:
            cp.wait_send()
        for cp in recvs:
            cp.wait_recv()

    both = list(srcs) + list(lands)
    out = pl.pallas_call(
        body, name=name,
        in_specs=[_HBM] * (2 * n) + [_SEM, _SEM, pl.BlockSpec(memory_space=pl.ANY)],
        out_shape=tuple(pltpu.HBM(a.shape, a.dtype) for a in both), out_specs=tuple([_HBM] * (2 * n)),
        input_output_aliases={i: i for i in range(2 * n)},
        compiler_params=_cp(has_side_effects=pltpu.SideEffectType.DATAFLOW_SIDE_EFFECTING),
    )(*both, send_sems, recv_sems, after)
    return list(out[:n]), list(out[n:])


BIG = ["w_in", "w_uq", "w_ukv", "w_branch_ssm", "w_branch_mla", "w_out", "w_mlp_up", "w_mlp_down"]
COL_SHARDED = {"w_in", "w_uq", "w_ukv", "w_mlp_up"}
SMALL_REPL = ["norm_mix_w", "conv_b", "dt_bias", "a_log", "d_skip", "ssm_norm_w", "q_norm_w", "kv_norm_w", "norm_mlp_w"]


def _unshard_layer(name, g):
    _, r, c = g.shape
    if name in COL_SHARDED:
        return jnp.transpose(g, (1, 0, 2)).reshape(r, 4 * c)
    return g.reshape(4 * r, c)


def _to_shards(name, full):
    r, c = full.shape
    if name in COL_SHARDED:
        return jnp.transpose(full.reshape(r, 4, c // 4), (1, 0, 2))
    return full.reshape(4, r // 4, c)


def prep_layer(cfg, w):
    sp = np.cumsum(cfg.in_splits)[:-1].tolist()
    z, xbc, dt, cq, ckv, kr, gs, gm = jnp.split(w["w_in"], sp, axis=1)
    zpad = lambda n: jnp.zeros((cfg.d, n), z.dtype)
    out = dict(
        w_z=z, w_xbc=xbc, w_g=jnp.concatenate([gs, gm], axis=1),
        w_s=jnp.concatenate([cq, ckv, kr, zpad(LANE - cfg.rope), dt, zpad(LANE - cfg.heads)], axis=1),
        w_uq=jnp.pad(w["w_uq"].reshape(cfg.ql, cfg.mh, cfg.nope + cfg.rope),
                     ((0, 0), (0, 0), (0, 2 * LANE - cfg.nope - cfg.rope))).reshape(cfg.ql, cfg.qw),
        w_ukv=w["w_ukv"], w_bs=w["w_branch_ssm"], w_bm=w["w_branch_mla"], w_out=w["w_out"],
        w_up=w["w_mlp_up"], w_down=w["w_mlp_down"])
    return {k: v.astype(BF16) for k, v in out.items()}


def unprep_grads(cfg, g):
    ql, kvl = cfg.ql, cfg.kvl
    ds_ = g["w_s"]
    cq, ckv = ds_[:, :ql], ds_[:, ql:ql + kvl]
    kr = ds_[:, ql + kvl:ql + kvl + cfg.rope]
    dt = ds_[:, ql + kvl + LANE:ql + kvl + LANE + cfg.heads]
    w_in = jnp.concatenate([g["w_z"], g["w_xbc"], dt, cq, ckv, kr, g["w_g"]], axis=1)
    w_uq = g["w_uq"].reshape(cfg.ql, cfg.mh, 2 * LANE)[:, :, :cfg.nope + cfg.rope].reshape(cfg.ql, -1)
    return dict(w_in=w_in, w_uq=w_uq, w_ukv=g["w_ukv"], w_branch_ssm=g["w_bs"], w_branch_mla=g["w_bm"],
                w_out=g["w_out"], w_mlp_up=g["w_up"], w_mlp_down=g["w_down"])


def layer_fwd(cfg, h, pw, sm, tabs, li):
    n = lambda s: f"l{li}_{s}"
    u = rmsnorm_fwd(h, sm["norm_mix_w"], name=n("norm_mix"))
    z = matmul(u, pw["w_z"], name=n("in_z"))
    xbc = matmul(u, pw["w_xbc"], name=n("in_xbc"))
    g = matmul(u, pw["w_g"], name=n("in_g"))
    small = matmul(u, pw["w_s"], name=n("in_s"), tn=cfg.sw)
    xc = conv_fwd(cfg, xbc, sm["conv_w"], sm["conv_b"], name=n("conv"))
    y, sin = ssd_fwd(cfg, xc, small, sm["dt_bias_p"], sm["avec"], sm["dexp"], name=n("ssd"))
    y_ssm = tail_fwd(cfg, y, z, sm["ssm_norm_w"], name=n("tail"))
    cqn = rmsnorm_fwd(small, sm["q_norm_w"], cw=cfg.ql, ci=0, name=n("q_norm"))
    ckvn = rmsnorm_fwd(small, sm["kv_norm_w"], cw=cfg.kvl, ci=cfg.ql // cfg.kvl, name=n("kv_norm"))
    qf = matmul(cqn, pw["w_uq"], name=n("uq"))
    kv = matmul(ckvn, pw["w_ukv"], out_dtype=BF16, name=n("ukv"))
    qr, kpe = rope_fwd(cfg, qf, small, tabs, name=n("rope"))
    o, lse = attn_fwd(cfg, qr, kv, kpe, name=n("attn"))
    ya = matmul(y_ssm, pw["w_bs"], name=n("branch_ssm"))
    yb = matmul(o, pw["w_bm"], name=n("branch_mla"))
    mixed = gate_fwd(cfg, ya, yb, g, name=n("gate"))
    h1 = matmul(mixed, pw["w_out"], add=h, name=n("out"))
    v = rmsnorm_fwd(h1, sm["norm_mlp_w"], name=n("norm_mlp"))
    a = matmul(v, pw["w_up"], name=n("up"))
    act = relu2_fwd(a, name=n("relu2"))
    h2 = matmul(act, pw["w_down"], add=h1, name=n("down"))
    saved = dict(h=h, u=u, z=z, xbc=xbc, g=g, small=small, xc=xc, y=y, sin=sin, y_ssm=y_ssm, cqn=cqn, ckvn=ckvn,
                 qr=qr, kv=kv, kpe=kpe, o=o, lse=lse, ya=ya, yb=yb, mixed=mixed, h1=h1, v=v, a=a, act=act)
    return h2, saved


def layer_bwd(cfg, dh2, pw, sm, tabs, s, li):
    n = lambda t: f"l{li}_b_{t}"
    gw, gs = {}, {}
    gw["w_down"] = matmul(s["act"], dh2, ta=True, name=n("dw_down"))
    dact = matmul(dh2, pw["w_down"], tb=True, name=n("dact"))
    da = relu2_bwd(dact, s["a"], name=n("relu2"))
    gw["w_up"] = matmul(s["v"], da, ta=True, name=n("dw_up"))
    dv = matmul(da, pw["w_up"], tb=True, name=n("dv"))
    dh1, gs["norm_mlp_w"] = rmsnorm_bwd(dv, s["h1"], sm["norm_mlp_w"], res=dh2, name=n("norm_mlp"))
    gw["w_out"] = matmul(s["mixed"], dh1, ta=True, name=n("dw_out"))
    dmix = matmul(dh1, pw["w_out"], tb=True, name=n("dmix"))
    dya, dyb, dg = gate_bwd(cfg, dmix, s["ya"], s["yb"], s["g"], name=n("gate"))
    gw["w_bs"] = matmul(s["y_ssm"], dya, ta=True, name=n("dw_bs"))
    gw["w_bm"] = matmul(s["o"], dyb, ta=True, name=n("dw_bm"))
    dy_ssm = matmul(dya, pw["w_bs"], tb=True, name=n("dy_ssm"))
    do = matmul(dyb, pw["w_bm"], tb=True, name=n("do"))
    dq, dkv, dkpe = attn_bwd(cfg, s["qr"], s["kv"], s["kpe"], s["o"], s["lse"], do, name=n("attn"))
    dqf, dkr = rope_bwd(cfg, dq, dkpe, tabs, name=n("rope"))
    gw["w_uq"] = matmul(s["cqn"], dqf, ta=True, name=n("dw_uq"))
    gw["w_ukv"] = matmul(s["ckvn"], dkv, ta=True, name=n("dw_ukv"))
    dcqn = matmul(dqf, pw["w_uq"], tb=True, name=n("dcqn"))
    dckvn = matmul(dkv, pw["w_ukv"], tb=True, name=n("dckvn"))
    dcq, gs["q_norm_w"] = rmsnorm_bwd(dcqn, s["small"], sm["q_norm_w"], cw=cfg.ql, ci=0, out_dtype=BF16, name=n("q_norm"))
    dckv, gs["kv_norm_w"] = rmsnorm_bwd(dckvn, s["small"], sm["kv_norm_w"], cw=cfg.kvl, ci=cfg.ql // cfg.kvl,
                                        out_dtype=BF16, name=n("kv_norm"))
    dy, dz, gs["ssm_norm_w"] = tail_bwd(cfg, dy_ssm, s["y"], s["z"], sm["ssm_norm_w"], name=n("tail"))
    dxs, db, dc, ddt, ddexp, dav, dbias = ssd_bwd(cfg, s["xc"], s["small"], sm["dt_bias_p"], sm["avec"], sm["dexp"],
                                                  s["sin"], dy, name=n("ssd"))
    dxc = jnp.concatenate([dxs, db, dc], axis=1)
    dxbc, gs["conv_w"], gs["conv_b"] = conv_bwd(cfg, s["xbc"], sm["conv_w"], sm["conv_b"], dxc, name=n("conv"))
    gs["d_skip"] = ddexp.reshape(cfg.heads, cfg.hd).sum(axis=1)
    gs["a_log"] = (dav[0] * sm["avec"][0])[:cfg.heads]
    gs["dt_bias"] = dbias[0, :cfg.heads]
    dsmall = jnp.concatenate([dcq, dckv, dkr.astype(BF16), ddt.astype(BF16)], axis=1)
    gw["w_z"] = matmul(s["u"], dz, ta=True, name=n("dw_z"))
    gw["w_xbc"] = matmul(s["u"], dxbc, ta=True, name=n("dw_xbc"))
    gw["w_g"] = matmul(s["u"], dg, ta=True, name=n("dw_g"))
    gw["w_s"] = matmul(s["u"], dsmall, ta=True, name=n("dw_s"))
    du = matmul(dz, pw["w_z"], tb=True, name=n("du_z"))
    du = matmul(dxbc, pw["w_xbc"], tb=True, add=du, name=n("du_xbc"))
    du = matmul(dg, pw["w_g"], tb=True, add=du, name=n("du_g"))
    du = matmul(dsmall, pw["w_s"], tb=True, add=du, name=n("du_s"))
    dh, gs["norm_mix_w"] = rmsnorm_bwd(du, s["h"], sm["norm_mix_w"], res=dh1, name=n("norm_mix"))
    return dh, gw, gs


def small_params(cfg, p, li):
    pad_l = lambda v: jnp.pad(v, (0, LANE - v.shape[0])).reshape(1, LANE)
    return dict(
        norm_mix_w=p["norm_mix_w"][li], conv_w=p["conv_w"][li], conv_b=p["conv_b"][li],
        dt_bias_p=pad_l(p["dt_bias"][li]), avec=pad_l(-jnp.exp(p["a_log"][li])),
        dexp=jnp.repeat(p["d_skip"][li], cfg.hd).reshape(1, cfg.inner),
        ssm_norm_w=p["ssm_norm_w"][li], q_norm_w=p["q_norm_w"][li], kv_norm_w=p["kv_norm_w"][li],
        norm_mlp_w=p["norm_mlp_w"][li])


def local_step(cfg, x, target, p, depth=2):
    bsz, d = cfg.bsz, cfg.d
    lead = jnp.zeros((bsz, cfg.pad, d), F32)
    meta = jnp.broadcast_to(p["meta_tokens"][None], (bsz, cfg.n_meta, d))
    h = jnp.concatenate([lead, meta, x], axis=1).reshape(cfg.t, d)
    tabs = rope_tables(cfg)
    saved, sms = [], []
    for li in range(depth):
        sm = small_params(cfg, p, li)
        h, s = layer_fwd(cfg, h, p["pw"][li], sm, tabs, li)
        saved.append(s)
        sms.append(sm)
    loss, dh, dfw = loss_head(cfg, h, target.reshape(bsz * cfg.seq, d), p["final_norm_w"], name="loss_head")
    gws, gss = [None] * depth, [None] * depth
    for li in reversed(range(depth)):
        dh, gws[li], gss[li] = layer_bwd(cfg, dh, p["pw"][li], sms[li], tabs, saved[li], li)
    dh = dh.reshape(bsz, cfg.lp, d)
    grad_x = dh[:, cfg.chunk:, :]
    gmeta = jnp.sum(dh[:, cfg.pad:cfg.chunk, :], axis=0)
    return loss, grad_x, gmeta, gws, gss, dfw


def _pack_small(parts):
    flat = jnp.concatenate([a.reshape(-1) for a in parts])
    n = flat.shape[0]
    npad = -n % (8 * LANE)
    return jnp.pad(flat, (0, npad)).reshape(-1, LANE), n


def _unpack_small(vec, shapes):
    flat = vec.reshape(-1)
    out, off = [], 0
    for sh in shapes:
        sz = int(np.prod(sh))
        out.append(flat[off:off + sz].reshape(sh))
        off += sz
    return out


def _as2d(a):
    return a.reshape(-1, a.shape[-1])


def kernel(x, meta_tokens, norm_mix_w, w_in, conv_w, conv_b, dt_bias, a_log, d_skip, ssm_norm_w, q_norm_w, kv_norm_w, w_uq, w_ukv, w_branch_ssm, w_branch_mla, w_out, norm_mlp_w, w_mlp_up, w_mlp_down, final_norm_w, loss_target, m_meta_tokens, m_norm_mix_w, m_w_in, m_conv_w, m_conv_b, m_dt_bias, m_a_log, m_d_skip, m_ssm_norm_w, m_q_norm_w, m_kv_norm_w, m_w_uq, m_w_ukv, m_w_branch_ssm, m_w_branch_mla, m_w_out, m_norm_mlp_w, m_w_mlp_up, m_w_mlp_down, m_final_norm_w, v_meta_tokens, v_norm_mix_w, v_w_in, v_conv_w, v_conv_b, v_dt_bias, v_a_log, v_d_skip, v_ssm_norm_w, v_q_norm_w, v_kv_norm_w, v_w_uq, v_w_ukv, v_w_branch_ssm, v_w_branch_mla, v_w_out, v_norm_mlp_w, v_w_mlp_up, v_w_mlp_down, v_final_norm_w):
    cfg = CFG
    names = ["meta_tokens", "norm_mix_w", "w_in", "conv_w", "conv_b", "dt_bias", "a_log", "d_skip", "ssm_norm_w",
             "q_norm_w", "kv_norm_w", "w_uq", "w_ukv", "w_branch_ssm", "w_branch_mla", "w_out", "norm_mlp_w",
             "w_mlp_up", "w_mlp_down", "final_norm_w"]
    wts = dict(zip(names, [meta_tokens, norm_mix_w, w_in, conv_w, conv_b, dt_bias, a_log, d_skip, ssm_norm_w,
                           q_norm_w, kv_norm_w, w_uq, w_ukv, w_branch_ssm, w_branch_mla, w_out, norm_mlp_w,
                           w_mlp_up, w_mlp_down, final_norm_w]))
    ms = dict(zip(names, [m_meta_tokens, m_norm_mix_w, m_w_in, m_conv_w, m_conv_b, m_dt_bias, m_a_log, m_d_skip,
                          m_ssm_norm_w, m_q_norm_w, m_kv_norm_w, m_w_uq, m_w_ukv, m_w_branch_ssm, m_w_branch_mla,
                          m_w_out, m_norm_mlp_w, m_w_mlp_up, m_w_mlp_down, m_final_norm_w]))
    vs = dict(zip(names, [v_meta_tokens, v_norm_mix_w, v_w_in, v_conv_w, v_conv_b, v_dt_bias, v_a_log, v_d_skip,
                          v_ssm_norm_w, v_q_norm_w, v_kv_norm_w, v_w_uq, v_w_ukv, v_w_branch_ssm, v_w_branch_mla,
                          v_w_out, v_norm_mlp_w, v_w_mlp_up, v_w_mlp_down, v_final_norm_w]))
    cx, cy, cc = _coords()
    chip = 2 * cx + cy

    half1 = jnp.reshape(cc, (1,)).astype(jnp.int32)
    where2 = jnp.stack([chip, cc]).astype(jnp.int32)
    wb = {k: wts[k].astype(BF16) for k in BIG}
    zero_tok = jnp.zeros((8, LANE), F32)

    def halves(a):
        return a.reshape((2, a.shape[0] // 2) + a.shape[1:])

    def gather_start(li, after):
        srcs = [halves(wb[k][li]) for k in BIG]
        lands = [lax.empty((4,) + s.shape, BF16) for s in srcs]
        return ici_start("gather", srcs, lands, after, name=f"gather{li}_start")

    def gather_finish(li, started, after):
        srcs, lands = ici_wait("gather", started, after, name=f"gather{li}_wait")
        lands = pair_share(lands, name=f"gather{li}_share")
        full = {}
        for k, own, land in zip(BIG, srcs, lands):
            r2, c_ = own.shape[1], own.shape[2]
            g = lax.dynamic_update_index_in_dim(land.reshape(4, 2 * r2, c_), own.reshape(2 * r2, c_), chip, 0)
            full[k] = _unshard_layer(k, g)
        return prep_layer(cfg, full), lands[0]

    def reduce_start(li, gw, after):
        ug = unprep_grads(cfg, gw)
        g4 = []
        for k in BIG:
            s = _to_shards(k, ug[k])
            g4.append(s.reshape(4, 2, s.shape[1] // 2, s.shape[2]))
        theirs = pair_exchange(g4, name=f"grad{li}_pair_exchange")
        parts = [pair_add(a, b, half1, name=f"grad{li}_pair_add_{k}") for k, a, b in zip(BIG, g4, theirs)]
        lands = [lax.empty(q.shape, F32) for q in parts]
        return ici_start("scatter", parts, lands, after, name=f"grad{li}_scatter_start")

    def reduce_finish(li, started, after):
        parts, lands = ici_wait("scatter", started, after, name=f"grad{li}_scatter_wait")
        sums = [chip_sum(rc, pt, where2, name=f"grad{li}_chip_sum_{k}") for k, rc, pt in zip(BIG, lands, parts)]
        sums = pair_fill(sums, name=f"grad{li}_pair_fill")
        return {k: s.reshape(2 * s.shape[1], s.shape[2]) for k, s in zip(BIG, sums)}

    gathered = gather_chips([meta_tokens, conv_w], name="gather_small")
    p = dict(wts)
    p["meta_tokens"] = jnp.transpose(gathered[0], (1, 0, 2)).reshape(cfg.n_meta, cfg.d)
    p["conv_w"] = jnp.transpose(gathered[1], (1, 2, 0, 3)).reshape(2, cfg.convk, cfg.conv_dim)

    st0 = gather_start(0, gathered[0])
    pw0, dep0 = gather_finish(0, st0, st0[4])
    st1 = gather_start(1, dep0)

    bsz, d = cfg.bsz, cfg.d
    lead = jnp.zeros((bsz, cfg.pad, d), F32)
    meta = jnp.broadcast_to(p["meta_tokens"][None], (bsz, cfg.n_meta, d))
    h0 = jnp.concatenate([lead, meta, x], axis=1).reshape(cfg.t, d)
    tabs = rope_tables(cfg)
    sm0 = small_params(cfg, p, 0)
    sm0["norm_mix_w"] = sm0["norm_mix_w"] + st1[4][0, 0]
    h1, sv0 = layer_fwd(cfg, h0, pw0, sm0, tabs, 0)
    pw1, _ = gather_finish(1, st1, h1)
    sm1 = small_params(cfg, p, 1)
    h2, sv1 = layer_fwd(cfg, h1, pw1, sm1, tabs, 1)
    loss, dh, dfw = loss_head(cfg, h2, loss_target.reshape(bsz * cfg.seq, d), final_norm_w, name="loss_head")
    loss = lax.psum(loss, ("x", "y", "c"))

    dh, gw1, gs1 = layer_bwd(cfg, dh, pw1, sm1, tabs, sv1, 1)
    red1 = reduce_start(1, gw1, zero_tok)
    sm0b = dict(sm0)
    sm0b["norm_mlp_w"] = sm0["norm_mlp_w"] + red1[4][0, 0]
    dh, gw0, gs0 = layer_bwd(cfg, dh, pw0, sm0b, tabs, sv0, 0)
    dh3 = dh.reshape(bsz, cfg.lp, d)
    grad_x = dh3[:, cfg.chunk:, :]
    gmeta = jnp.sum(dh3[:, cfg.pad:cfg.chunk, :], axis=0)
    big1 = reduce_finish(1, red1, dh)

    small_names = SMALL_REPL + ["conv_w"]
    parts = [jnp.stack([gs0[k], gs1[k]]) for k in small_names] + [dfw, gmeta]
    shapes = [a.shape for a in parts]
    vec, _ = _pack_small(parts)
    red_vec = allreduce_small(vec, big1[BIG[-1]], name="allreduce_small")
    red = _unpack_small(red_vec, shapes)
    sg = dict(zip(small_names + ["final_norm_w", "meta_tokens"], red))
    sg["conv_w"] = lax.dynamic_slice_in_dim(sg["conv_w"], chip * (cfg.conv_dim // 4), cfg.conv_dim // 4, axis=2)
    sg["meta_tokens"] = lax.dynamic_slice_in_dim(sg["meta_tokens"], chip * (cfg.d // 4), cfg.d // 4, axis=1)

    red0 = reduce_start(0, gw0, red_vec)
    grads, deltas, new_m, new_v = {}, {}, {}, {}
    dep = red0[4]
    for k in names:
        if k in BIG:
            continue
        w2, g2, m2, v2 = _as2d(wts[k]), _as2d(sg[k]), _as2d(ms[k]), _as2d(vs[k])
        dl, mn, vn = adamw_small(w2, g2, m2, v2, dep, name=f"adamw_{k}")
        grads[k] = sg[k].reshape(wts[k].shape)
        deltas[k], new_m[k], new_v[k] = (t.reshape(wts[k].shape) for t in (dl, mn, vn))
    outs = {}
    for k in BIG:
        outs[k] = adamw_layer(wts[k], ms[k], vs[k], big1[k], 1, None, dep, name=f"adamw1_{k}")
        dep = outs[k][1]
    big0 = reduce_finish(0, red0, dep)
    for k in BIG:
        outs[k] = adamw_layer(wts[k], ms[k], vs[k], big0[k], 0, outs[k], dep, name=f"adamw0_{k}")
        grads[k], deltas[k], new_m[k], new_v[k] = outs[k]
    return (loss, grad_x, *[grads[k] for k in names], *[deltas[k] for k in names],
            *[new_m[k] for k in names], *[new_v[k] for k in names])


def adamw_small(w, g, m, v, dep, *, name):
    def body(w_ref, g_ref, m_ref, v_ref, dep_ref, d_ref, mo_ref, vo_ref):
        d_ref[...], mo_ref[...], vo_ref[...] = _adam_update(w_ref[...], g_ref[...], m_ref[...], v_ref[...])

    vm = pl.BlockSpec(memory_space=pltpu.VMEM)
    return pl.pallas_call(body, name=name, in_specs=[vm] * 4 + [pl.BlockSpec(memory_space=pl.ANY)], out_specs=[vm] * 3,
                          out_shape=[_sds(w.shape, F32)] * 3, compiler_params=_cp())(w, g, m, v, dep)
```

```python
import functools
import math
from typing import NamedTuple

import numpy as np
import jax
import jax.numpy as jnp
from jax import lax
from jax.experimental import pallas as pl
from jax.experimental.pallas import tpu as pltpu

F32 = jnp.float32
BF16 = jnp.bfloat16
HI = lax.Precision.HIGHEST
EPS = 1e-6
ROPE_THETA = 10000.0
LANE = 128
VMEM_LIMIT = 56 * 1024 * 1024
MASK_VALUE = -1e30
ADAM_LR, ADAM_B1, ADAM_B2, ADAM_EPS, ADAM_WD, ADAM_STEP = 0.001, 0.9, 0.999, 1e-08, 0.01, 10
MESH = pl.DeviceIdType.MESH


class Cfg(NamedTuple):
    d: int = 1024
    seq: int = 2048
    bsz: int = 2
    n_meta: int = 16
    inner: int = 2048
    hd: int = 64
    groups: int = 4
    state: int = 128
    convk: int = 4
    chunk: int = 128
    mh: int = 8
    ql: int = 512
    kvl: int = 256
    nope: int = 128
    rope: int = 64
    vd: int = 128
    ff: int = 4096

    @property
    def heads(self): return self.inner // self.hd
    @property
    def gw(self): return self.inner // self.groups
    @property
    def conv_dim(self): return self.inner + 2 * self.groups * self.state
    @property
    def pad(self): return self.chunk - self.n_meta
    @property
    def lp(self): return self.chunk + self.seq
    @property
    def t(self): return self.bsz * self.lp
    @property
    def nchunks(self): return self.lp // self.chunk
    @property
    def sw(self): return self.ql + self.kvl + 2 * LANE
    @property
    def kt(self): return (self.ql + self.kvl) // LANE
    @property
    def dtt(self): return self.kt + 1
    @property
    def qw(self): return self.mh * 2 * LANE
    @property
    def in_splits(self):
        return [self.inner, self.conv_dim, self.heads, self.ql, self.kvl, self.rope, self.d, self.d]


CFG = Cfg()


def _pick(dim, pref, mult):
    best = None
    for t in range(mult, min(dim, pref) + 1, mult):
        if dim % t == 0:
            best = t
    return best if best is not None else dim


def _cp(**kw):
    return pltpu.CompilerParams(vmem_limit_bytes=VMEM_LIMIT, **kw)


def _sds(shape, dtype):
    return jax.ShapeDtypeStruct(tuple(shape), dtype)


def _silu(x):
    return x * jax.nn.sigmoid(x)


def _dsilu(x):
    s = jax.nn.sigmoid(x)
    return s * (1.0 + x * (1.0 - s))


def _ep_plain(r):
    return (r,)


def _ep_add(r, res):
    return (r + res.astype(F32),)


def _ep_relu2(r):
    rp = jnp.maximum(r, 0.0)
    return r, rp * rp


def _ep_relu2_grad(r, a):
    return (r * (2.0 * jnp.maximum(a.astype(F32), 0.0)),)


def matmul(a, b, *, ta=False, tb=False, out_dtype=F32, add=None, name, tm=None, tn=None, tk=None,
           epilogue=None, extras=(), out_dtypes=None):
    if add is not None:
        epilogue, extras = _ep_add, (add,)
    if epilogue is None:
        epilogue = _ep_plain
    out_dtypes = tuple(out_dtypes) if out_dtypes is not None else (out_dtype,)
    n_ex, n_out = len(extras), len(out_dtypes)
    if ta:
        k_dim, m_dim = a.shape
    else:
        m_dim, k_dim = a.shape
    if tb:
        n_dim, k2 = b.shape
    else:
        k2, n_dim = b.shape
    assert k_dim == k2, (a.shape, b.shape, ta, tb)
    if ta:
        tm = tm or _pick(m_dim, 1024, LANE)
        tk = tk or _pick(k_dim, 1088, 16)
        tn = tn or _pick(n_dim, 1024, LANE)
    else:
        tm = tm or _pick(m_dim, 1088, 16)
        tk = tk or _pick(k_dim, 1024 if a.dtype == F32 else 2048, LANE)
        tn = tn or _pick(n_dim, 512, LANE)
    nm, nn, nk = m_dim // tm, n_dim // tn, k_dim // tk
    dn = (((0 if ta else 1,), (1 if tb else 0,)), ((), ()))

    def body(*refs):
        a_ref, b_ref = refs[:2]
        ex_refs = refs[2:2 + n_ex]
        o_refs = refs[2 + n_ex:2 + n_ex + n_out]
        scr = refs[2 + n_ex + n_out:]
        p = lax.dot_general(a_ref[...].astype(BF16), b_ref[...].astype(BF16), dn, preferred_element_type=F32)

        def finish(r):
            outs = epilogue(r, *[e[...] for e in ex_refs])
            for o_ref, val, dt in zip(o_refs, outs, out_dtypes):
                o_ref[...] = val.astype(dt)

        if nk == 1:
            finish(p)
        else:
            acc = scr[0]
            k = pl.program_id(2)

            @pl.when(k == 0)
            def _():
                acc[...] = p

            @pl.when(k > 0)
            def _():
                acc[...] += p

            @pl.when(k == nk - 1)
            def _():
                finish(acc[...])

    a_spec = pl.BlockSpec((tk, tm), lambda i, j, k: (k, i)) if ta else pl.BlockSpec((tm, tk), lambda i, j, k: (i, k))
    b_spec = pl.BlockSpec((tn, tk), lambda i, j, k: (j, k)) if tb else pl.BlockSpec((tk, tn), lambda i, j, k: (k, j))
    o_spec = pl.BlockSpec((tm, tn), lambda i, j, k: (i, j))
    outs = pl.pallas_call(
        body, name=name, grid=(nm, nn, nk), in_specs=[a_spec, b_spec] + [o_spec] * n_ex, out_specs=[o_spec] * n_out,
        out_shape=[_sds((m_dim, n_dim), dt) for dt in out_dtypes],
        scratch_shapes=[pltpu.VMEM((tm, tn), F32)] if nk > 1 else [],
        compiler_params=_cp(dimension_semantics=("parallel", "parallel", "arbitrary")),
    )(a, b, *extras)
    return outs[0] if n_out == 1 else tuple(outs)


def rmsnorm_fwd(x, w, *, cw=None, ci=0, name):
    t = x.shape[0]
    cw = cw or x.shape[1]
    tr = _pick(t, 544, 16)

    def body(x_ref, w_ref, o_ref):
        xv = x_ref[...].astype(F32)
        r = lax.rsqrt(jnp.mean(xv * xv, axis=-1, keepdims=True) + EPS)
        o_ref[...] = (xv * r * w_ref[...]).astype(BF16)

    return pl.pallas_call(
        body, name=name, grid=(t // tr,),
        in_specs=[pl.BlockSpec((tr, cw), lambda i: (i, ci)), pl.BlockSpec((1, cw), lambda i: (0, 0))],
        out_specs=pl.BlockSpec((tr, cw), lambda i: (i, 0)),
        out_shape=_sds((t, cw), BF16), compiler_params=_cp(),
    )(x, w.reshape(1, cw))


def rmsnorm_bwd(dy, x, w, *, cw=None, ci=0, res=None, out_dtype=F32, name):
    t = x.shape[0]
    cw = cw or x.shape[1]
    tr = _pick(t, 544, 16)
    has_res = res is not None

    def body(*refs):
        if has_res:
            dy_ref, x_ref, w_ref, res_ref, dx_ref, dw_ref = refs
        else:
            dy_ref, x_ref, w_ref, dx_ref, dw_ref = refs
        xv = x_ref[...].astype(F32)
        dyv = dy_ref[...].astype(F32)
        r = lax.rsqrt(jnp.mean(xv * xv, axis=-1, keepdims=True) + EPS)
        xh = xv * r
        g = dyv * w_ref[...]
        dx = r * (g - xh * jnp.mean(g * xh, axis=-1, keepdims=True))
        if has_res:
            dx = dx + res_ref[...]
        dx_ref[...] = dx.astype(out_dtype)

        @pl.when(pl.program_id(0) == 0)
        def _():
            dw_ref[...] = jnp.zeros_like(dw_ref)

        dw_ref[...] += jnp.sum(dyv * xh, axis=0, keepdims=True)

    row = pl.BlockSpec((tr, cw), lambda i: (i, 0))
    in_specs = [row, pl.BlockSpec((tr, cw), lambda i: (i, ci)), pl.BlockSpec((1, cw), lambda i: (0, 0))]
    args = [dy, x, w.reshape(1, cw)]
    if has_res:
        in_specs.append(row)
        args.append(res)
    dx, dw = pl.pallas_call(
        body, name=name, grid=(t // tr,), in_specs=in_specs,
        out_specs=[row, pl.BlockSpec((1, cw), lambda i: (0, 0))],
        out_shape=[_sds((t, cw), out_dtype), _sds((1, cw), F32)], compiler_params=_cp(),
    )(*args)
    return dx, dw[0]


def _shift_down(x, s, rows):
    if s == 0:
        return x
    return jnp.where(rows >= s, pltpu.roll(x, s, 0), 0.0)


def _shift_up(x, s, rows):
    if s == 0:
        return x
    n = x.shape[0]
    return jnp.where(rows < n - s, pltpu.roll(x, n - s, 0), 0.0)


def _conv_pre(x, w_ref, b_ref, rows, kk):
    pre = b_ref[...] + jnp.zeros_like(x)
    for k in range(kk):
        pre = pre + w_ref[k:k + 1, :] * _shift_down(x, kk - 1 - k, rows)
    return pre


def conv_fwd(cfg, xbc, w, b, *, name):
    lp, cd, kk = cfg.lp, cfg.conv_dim, cfg.convk
    cb = _pick(cd, 512, LANE)

    def body(x_ref, w_ref, b_ref, o_ref):
        x = x_ref[...]
        rows = lax.broadcasted_iota(jnp.int32, x.shape, 0)
        o_ref[...] = _silu(_conv_pre(x, w_ref, b_ref, rows, kk))

    blk = pl.BlockSpec((lp, cb), lambda j, bb: (bb, j))
    return pl.pallas_call(
        body, name=name, grid=(cd // cb, cfg.bsz),
        in_specs=[blk, pl.BlockSpec((kk, cb), lambda j, bb: (0, j)), pl.BlockSpec((1, cb), lambda j, bb: (0, j))],
        out_specs=blk, out_shape=_sds((cfg.t, cd), F32), compiler_params=_cp(),
    )(xbc, w, b.reshape(1, cd))


def conv_bwd(cfg, xbc, w, b, dxc, *, name):
    lp, cd, kk = cfg.lp, cfg.conv_dim, cfg.convk
    cb = _pick(cd, 512, LANE)

    def body(x_ref, w_ref, b_ref, d_ref, dx_ref, dw_ref, db_ref):
        x = x_ref[...]
        rows = lax.broadcasted_iota(jnp.int32, x.shape, 0)
        pre = _conv_pre(x, w_ref, b_ref, rows, kk)
        dpre = d_ref[...] * _dsilu(pre)
        dx = jnp.zeros_like(x)
        dws = []
        for k in range(kk):
            s = kk - 1 - k
            dx = dx + w_ref[k:k + 1, :] * _shift_up(dpre, s, rows)
            dws.append(jnp.sum(dpre * _shift_down(x, s, rows), axis=0, keepdims=True))
        dx_ref[...] = dx.astype(BF16)

        @pl.when(pl.program_id(1) == 0)
        def _():
            dw_ref[...] = jnp.zeros_like(dw_ref)
            db_ref[...] = jnp.zeros_like(db_ref)

        for k in range(kk):
            dw_ref[k:k + 1, :] += dws[k]
        db_ref[...] += jnp.sum(dpre, axis=0, keepdims=True)

    blk = pl.BlockSpec((lp, cb), lambda j, bb: (bb, j))
    wsp = pl.BlockSpec((kk, cb), lambda j, bb: (0, j))
    bsp = pl.BlockSpec((1, cb), lambda j, bb: (0, j))
    dx, dw, db = pl.pallas_call(
        body, name=name, grid=(cd // cb, cfg.bsz),
        in_specs=[blk, wsp, bsp, blk], out_specs=[blk, wsp, bsp],
        out_shape=[_sds((cfg.t, cd), BF16), _sds((kk, cd), F32), _sds((1, cd), F32)], compiler_params=_cp(),
    )(xbc, w, b.reshape(1, cd), dxc)
    return dx, dw, db[0]


def _softplus(x):
    return jnp.maximum(x, 0.0) + jnp.log(1.0 + jnp.exp(-jnp.abs(x)))


def _ssd_consts(cfg):
    q = cfg.chunk
    i0 = np.arange(q)[:, None]
    i1 = np.arange(q)[None, :]
    ltri = (i1 <= i0).astype(np.float32)
    rexp = np.zeros((LANE, cfg.inner), np.float32)
    for h in range(cfg.heads):
        rexp[h, h * cfg.hd:(h + 1) * cfg.hd] = 1.0
    return jnp.asarray(ltri), jnp.asarray(rexp)


def _sel_dot(x, m, *, passes=2, left=False, trans=False):
    mb = m.astype(BF16)
    acc, rem = None, x
    for _ in range(passes):
        piece = rem.astype(BF16)
        if not left:
            part = _nn(piece, mb)
        elif trans:
            part = _tn(mb, piece)
        else:
            part = _nn(mb, piece)
        acc = part if acc is None else acc + part
        rem = rem - piece.astype(F32)
    return acc


def _ssd_chunk_common(cfg, raw, bias, avec, c_idx, ltri, rexp):
    q = cfg.chunk
    rows = lax.broadcasted_iota(jnp.int32, (q, LANE), 0)
    live = jnp.logical_or(c_idx > 0, rows >= cfg.pad)
    pre = raw + bias
    dt = jnp.where(live, _softplus(pre), 0.0)
    adt = dt * avec
    cs = _sel_dot(adt, ltri, passes=3, left=True)
    cs_t = cs.T
    cs_last = cs[q - 1:q, :]
    e_in = jnp.exp(cs)
    w0 = jnp.exp(cs_last - cs)
    decay = jnp.exp(cs_last)
    return dict(live=live, pre=pre, dt=dt, adt=adt, cs=cs, cs_t=cs_t, e_in=e_in, w0=w0, decay=decay,
                DT=_sel_dot(dt, rexp), E=_sel_dot(e_in, rexp), W0=_sel_dot(w0, rexp),
                DEC=_sel_dot(jnp.broadcast_to(decay, (8, LANE)), rexp)[0:1, :])


def _tri_masks(q):
    r = lax.broadcasted_iota(jnp.int32, (q, q), 0)
    c = lax.broadcasted_iota(jnp.int32, (q, q), 1)
    return c <= r, r <= c


def _head_l(cq, h, tri, tri_t):
    col = cq["cs"][:, h:h + 1]
    row = cq["cs_t"][h:h + 1, :]
    lmat = jnp.where(tri, jnp.exp(jnp.minimum(col - row, 0.0)), 0.0)
    lmat_t = jnp.where(tri_t, jnp.exp(jnp.minimum(row - col, 0.0)), 0.0)
    return lmat, lmat_t


def _nt(a, b):
    return lax.dot_general(a, b, (((1,), (1,)), ((), ())), preferred_element_type=F32)


def _tn(a, b):
    return lax.dot_general(a, b, (((0,), (0,)), ((), ())), preferred_element_type=F32)


def _nn(a, b):
    return jnp.dot(a, b, preferred_element_type=F32)


def ssd_fwd(cfg, xc, small, dt_bias, avec, dexp, *, name):
    q, inner, st, gw, g_n = cfg.chunk, cfg.inner, cfg.state, cfg.gw, cfg.groups
    nc = cfg.nchunks
    ltri, rexp = _ssd_consts(cfg)
    hpt = LANE // cfg.hd
    tiles_per_group = gw // LANE

    def body(x_ref, b_ref, c_ref, dt_ref, bias_ref, a_ref, d_ref, ltri_ref, rexp_ref, y_ref, sin_ref, s_scr):
        c_idx = pl.program_id(1)

        @pl.when(c_idx == 0)
        def _():
            s_scr[...] = jnp.zeros_like(s_scr)

        ltri_v = ltri_ref[...]
        tri, tri_t = _tri_masks(q)
        cq = _ssd_chunk_common(cfg, dt_ref[...], bias_ref[...], a_ref[...], c_idx, ltri_v, rexp_ref[...])
        xs = x_ref[...]
        xdt = (xs * cq["DT"]).astype(BF16)
        xw = (xs * cq["DT"] * cq["W0"]).astype(BF16)
        s_in = s_scr[...]
        sin_ref[0] = s_in
        lane = lax.broadcasted_iota(jnp.int32, (q, LANE), 1)
        for g in range(g_n):
            bg = b_ref[:, g * st:(g + 1) * st].astype(BF16)
            cg = c_ref[:, g * st:(g + 1) * st].astype(BF16)
            gmat = _nt(cg, bg)
            gs = slice(g * gw, (g + 1) * gw)
            y0 = _nn(cg, s_in[:, gs].astype(BF16))
            for tt in range(tiles_per_group):
                tile = g * tiles_per_group + tt
                ts = slice(tile * LANE, (tile + 1) * LANE)
                xt = xdt[:, ts]
                yd = None
                for hh in range(hpt):
                    h = tile * hpt + hh
                    lmat, _ = _head_l(cq, h, tri, tri_t)
                    part = _nn((gmat * lmat).astype(BF16), xt)
                    if yd is None:
                        yd = part
                    else:
                        yd = jnp.where(lane < (hh * cfg.hd), yd, part)
                y_ref[:, ts] = yd + y0[:, tt * LANE:(tt + 1) * LANE] * cq["E"][:, ts] + xs[:, ts] * d_ref[:, ts]
            s_scr[:, gs] = s_in[:, gs] * cq["DEC"][:, gs] + _tn(bg, xw[:, gs])

    def rowblk(width, col):
        return pl.BlockSpec((q, width), lambda b, c: (b * nc + c, col))

    def const(shape):
        return pl.BlockSpec(shape, lambda b, c: (0, 0))

    y, sin = pl.pallas_call(
        body, name=name, grid=(cfg.bsz, nc),
        in_specs=[rowblk(inner, 0),
                  pl.BlockSpec((q, g_n * st), lambda b, c: (b * nc + c, inner // (g_n * st))),
                  pl.BlockSpec((q, g_n * st), lambda b, c: (b * nc + c, inner // (g_n * st) + 1)),
                  rowblk(LANE, cfg.dtt), const((1, LANE)), const((1, LANE)), const((1, inner)),
                  const((q, q)), const((LANE, inner))],
        out_specs=[rowblk(inner, 0), pl.BlockSpec((1, st, inner), lambda b, c: (b * nc + c, 0, 0))],
        out_shape=[_sds((cfg.t, inner), F32), _sds((cfg.bsz * nc, st, inner), F32)],
        scratch_shapes=[pltpu.VMEM((st, inner), F32)], compiler_params=_cp(),
    )(xc, xc, xc, small, dt_bias, avec, dexp, ltri, rexp)
    return y, sin


def ssd_bwd(cfg, xc, small, dt_bias, avec, dexp, sin, dy, *, name):
    q, inner, st, gw, g_n = cfg.chunk, cfg.inner, cfg.state, cfg.gw, cfg.groups
    nc = cfg.nchunks
    ltri, rexp = _ssd_consts(cfg)
    rexp_t = rexp.T
    hpt = LANE // cfg.hd
    tiles_per_group = gw // LANE
    bcw = g_n * st

    def body(x_ref, b_ref, c_ref, dt_ref, bias_ref, a_ref, d_ref, ltri_ref, rexp_ref, rexpt_ref, sin_ref, dy_ref,
             dx_ref, ddt_ref, dd_ref, da_ref, dbias_ref, ds_scr):
        step = pl.program_id(1)
        c_idx = nc - 1 - step

        @pl.when(step == 0)
        def _():
            ds_scr[...] = jnp.zeros_like(ds_scr)

        @pl.when(jnp.logical_and(step == 0, pl.program_id(0) == 0))
        def _():
            dd_ref[...] = jnp.zeros_like(dd_ref)
            da_ref[...] = jnp.zeros_like(da_ref)
            dbias_ref[...] = jnp.zeros_like(dbias_ref)

        ltri_v = ltri_ref[...]
        tri, tri_t = _tri_masks(q)
        red = _sel_dot
        rexpt = rexpt_ref[...]
        cq = _ssd_chunk_common(cfg, dt_ref[...], bias_ref[...], a_ref[...], c_idx, ltri_v, rexp_ref[...])
        xs = x_ref[...]
        dyv = dy_ref[...]
        s_in = sin_ref[0]
        d_s = ds_scr[...]
        xdt_f = xs * cq["DT"]
        xdt = xdt_f.astype(BF16)
        xw_f = xdt_f * cq["W0"]
        xw = xw_f.astype(BF16)
        lane = lax.broadcasted_iota(jnp.int32, (q, LANE), 1)
        sub = lax.broadcasted_iota(jnp.int32, (LANE, q), 0)

        dd_ref[...] += jnp.sum(dyv * xs, axis=0, keepdims=True)
        dy0 = dyv * cq["E"]
        dcs = jnp.zeros((q, LANE), F32)
        dcs_t = jnp.zeros((LANE, q), F32)
        for g in range(g_n):
            bg_f = b_ref[:, g * st:(g + 1) * st]
            cg_f = c_ref[:, g * st:(g + 1) * st]
            bg = bg_f.astype(BF16)
            cg = cg_f.astype(BF16)
            gs = slice(g * gw, (g + 1) * gw)
            gmat = _nt(cg, bg)
            gmat_t = _nt(bg, cg)
            sing = s_in[:, gs].astype(BF16)
            dsg = d_s[:, gs].astype(BF16)
            y0 = _nn(cg, sing)
            dxw = _nn(bg, dsg)
            d_bg = _nt(xw[:, gs], dsg)
            d_cg = _nt(dy0[:, gs].astype(BF16), sing)
            ds_in_g = _tn(cg, dy0[:, gs].astype(BF16))
            dg = jnp.zeros((q, q), F32)
            dxdt_g = []
            for tt in range(tiles_per_group):
                tile = g * tiles_per_group + tt
                ts = slice(tile * LANE, (tile + 1) * LANE)
                xt = xdt[:, ts]
                dyt = dyv[:, ts]
                dxdt_t = None
                for hh in range(hpt):
                    h = tile * hpt + hh
                    lmat, lmat_t = _head_l(cq, h, tri, tri_t)
                    inhead = jnp.logical_and(lane >= hh * cfg.hd, lane < (hh + 1) * cfg.hd)
                    dyh = jnp.where(inhead, dyt, 0.0).astype(BF16)
                    dm = _nt(dyh, xt)
                    dg = dg + dm * lmat
                    qm = dm * gmat * lmat
                    rs = jnp.sum(qm, axis=1, keepdims=True)
                    csum = jnp.sum(qm, axis=0, keepdims=True)
                    dcs = dcs + jnp.where(lane == h, rs, 0.0)
                    dcs_t = dcs_t + jnp.where(sub == h, csum, 0.0)
                    part = _nn((gmat_t * lmat_t).astype(BF16), dyh)
                    dxdt_t = part if dxdt_t is None else dxdt_t + part
                dxdt_g.append(dxdt_t)
            dxdt_diag = jnp.concatenate(dxdt_g, axis=1) if len(dxdt_g) > 1 else dxdt_g[0]
            dgb = dg.astype(BF16)
            d_cg = d_cg + _nn(dgb, bg)
            d_bg = d_bg + _tn(dgb, cg)
            dx_ref[:, inner + g * st:inner + (g + 1) * st] = d_bg
            dx_ref[:, inner + bcw + g * st:inner + bcw + (g + 1) * st] = d_cg
            dxdt = dxdt_diag + dxw * cq["W0"][:, gs]
            dx_ref[:, gs] = dyv[:, gs] * d_ref[:, gs] + dxdt * cq["DT"][:, gs]
            rt = rexpt[gs, :]
            dcs = dcs + red(dyv[:, gs] * y0 * cq["E"][:, gs], rt)
            r_w = red(dxw * xw_f[:, gs], rt)
            dcs = dcs - r_w
            dcs_last_g = jnp.sum(r_w, axis=0, keepdims=True)
            ddec = red(jnp.broadcast_to(jnp.sum(d_s[:, gs] * s_in[:, gs], axis=0, keepdims=True), (8, gw)), rt)[0:1, :]
            dcs_last_g = dcs_last_g + ddec * cq["decay"]
            dcs = dcs + jnp.where(lax.broadcasted_iota(jnp.int32, (q, LANE), 0) == q - 1, dcs_last_g, 0.0)
            ddt_part = red(dxdt * xs[:, gs], rt)
            if g == 0:
                ddt = ddt_part
            else:
                ddt = ddt + ddt_part
            ds_scr[:, gs] = d_s[:, gs] * cq["DEC"][:, gs] + ds_in_g
        dcs = dcs - dcs_t.T
        dadt = _sel_dot(dcs, ltri_v, left=True, trans=True)
        ddt = ddt + dadt * a_ref[...]
        da_ref[...] += jnp.sum(dadt * cq["dt"], axis=0, keepdims=True)
        draw = jnp.where(cq["live"], ddt * jax.nn.sigmoid(cq["pre"]), 0.0)
        ddt_ref[...] = draw
        dbias_ref[...] += jnp.sum(draw, axis=0, keepdims=True)

    def rowblk(width, col):
        return pl.BlockSpec((q, width), lambda b, s: (b * nc + nc - 1 - s, col))

    def const(shape):
        return pl.BlockSpec(shape, lambda b, s: (0, 0))

    bcol = inner // bcw
    outs = pl.pallas_call(
        body, name=name, grid=(cfg.bsz, nc),
        in_specs=[rowblk(inner, 0), rowblk(bcw, bcol), rowblk(bcw, bcol + 1), rowblk(LANE, cfg.dtt),
                  const((1, LANE)), const((1, LANE)), const((1, inner)), const((q, q)), const((LANE, inner)),
                  const((inner, LANE)),
                  pl.BlockSpec((1, st, inner), lambda b, s: (b * nc + nc - 1 - s, 0, 0)), rowblk(inner, 0)],
        out_specs=[rowblk(cfg.conv_dim, 0), rowblk(LANE, 0),
                   const((1, inner)), const((1, LANE)), const((1, LANE))],
        out_shape=[_sds((cfg.t, cfg.conv_dim), F32),
                   _sds((cfg.t, LANE), F32), _sds((1, inner), F32), _sds((1, LANE), F32), _sds((1, LANE), F32)],
        scratch_shapes=[pltpu.VMEM((st, inner), F32)], compiler_params=_cp(),
    )(xc, xc, xc, small, dt_bias, avec, dexp, ltri, rexp, rexp_t, sin, dy)
    return outs


def tail_fwd(cfg, y, z, w, *, name):
    t, inner, gw = cfg.t, cfg.inner, cfg.gw
    tr = _pick(t, 272, 16)

    def body(y_ref, z_ref, w_ref, o_ref):
        for g in range(cfg.groups):
            gs = slice(g * gw, (g + 1) * gw)
            yg = y_ref[:, gs] * _silu(z_ref[:, gs])
            r = lax.rsqrt(jnp.mean(yg * yg, axis=-1, keepdims=True) + EPS)
            o_ref[:, gs] = (yg * r * w_ref[:, gs]).astype(BF16)

    row = pl.BlockSpec((tr, inner), lambda i: (i, 0))
    return pl.pallas_call(
        body, name=name, grid=(t // tr,), in_specs=[row, row, pl.BlockSpec((1, inner), lambda i: (0, 0))],
        out_specs=row, out_shape=_sds((t, inner), BF16), compiler_params=_cp(),
    )(y, z, w.reshape(1, inner))


def tail_bwd(cfg, do, y, z, w, *, name):
    t, inner, gw = cfg.t, cfg.inner, cfg.gw
    tr = _pick(t, 272, 16)

    def body(do_ref, y_ref, z_ref, w_ref, dy_ref, dz_ref, dw_ref):
        @pl.when(pl.program_id(0) == 0)
        def _():
            dw_ref[...] = jnp.zeros_like(dw_ref)

        for g in range(cfg.groups):
            gs = slice(g * gw, (g + 1) * gw)
            yv = y_ref[:, gs]
            zv = z_ref[:, gs]
            dov = do_ref[:, gs]
            sz = _silu(zv)
            yg = yv * sz
            r = lax.rsqrt(jnp.mean(yg * yg, axis=-1, keepdims=True) + EPS)
            xh = yg * r
            gg = dov * w_ref[:, gs]
            dyg = r * (gg - xh * jnp.mean(gg * xh, axis=-1, keepdims=True))
            dw_ref[:, gs] += jnp.sum(dov * xh, axis=0, keepdims=True)
            dy_ref[:, gs] = dyg * sz
            dz_ref[:, gs] = (dyg * yv * _dsilu(zv)).astype(BF16)

    row = pl.BlockSpec((tr, inner), lambda i: (i, 0))
    vec = pl.BlockSpec((1, inner), lambda i: (0, 0))
    dy, dz, dw = pl.pallas_call(
        body, name=name, grid=(t // tr,), in_specs=[row, row, row, vec], out_specs=[row, row, vec],
        out_shape=[_sds((t, inner), F32), _sds((t, inner), BF16), _sds((1, inner), F32)], compiler_params=_cp(),
    )(do, y, z, w.reshape(1, inner))
    return dy, dz, dw[0]


def rope_tables(cfg):
    half = cfg.rope // 2
    pos = np.maximum(np.arange(cfg.lp) - cfg.pad, 0).astype(np.float32)
    inv = ROPE_THETA ** (-jnp.arange(0, cfg.rope, 2, dtype=F32) / cfg.rope)
    ang = jnp.asarray(pos)[:, None] * inv[None, :]
    cos, sin = jnp.cos(ang), jnp.sin(ang)
    zero = jnp.zeros((cfg.lp, LANE - 2 * half), F32)
    zh = jnp.zeros((cfg.lp, half), F32)
    ctab = jnp.concatenate([cos, cos, zero], axis=1)
    s1 = jnp.concatenate([-sin, zh, zero], axis=1)
    s2 = jnp.concatenate([zh, sin, zero], axis=1)
    return ctab, s1, s2


def _rope(x, c, s1, s2, half):
    return x * c + pltpu.roll(x, LANE - half, 1) * s1 + pltpu.roll(x, half, 1) * s2


def _rope_t(dy, c, s1, s2, half):
    return dy * c + pltpu.roll(dy * s1, half, 1) + pltpu.roll(dy * s2, LANE - half, 1)


def rope_fwd(cfg, qf, small, tabs, *, name):
    t, qw, lp = cfg.t, cfg.qw, cfg.lp
    tr = _pick(lp, 544, 16)
    nrb = lp // tr
    half = cfg.rope // 2

    def body(q_ref, k_ref, c_ref, s1_ref, s2_ref, qo_ref, ko_ref):
        c, s1, s2 = c_ref[...], s1_ref[...], s2_ref[...]
        for h in range(cfg.mh):
            a = h * 2 * LANE
            qo_ref[:, a:a + LANE] = q_ref[:, a:a + LANE].astype(BF16)
            qo_ref[:, a + LANE:a + 2 * LANE] = _rope(q_ref[:, a + LANE:a + 2 * LANE], c, s1, s2, half).astype(BF16)
        ko_ref[...] = _rope(k_ref[...], c, s1, s2, half).astype(BF16)

    tab = pl.BlockSpec((tr, LANE), lambda i: (i % nrb, 0))
    return pl.pallas_call(
        body, name=name, grid=(t // tr,),
        in_specs=[pl.BlockSpec((tr, qw), lambda i: (i, 0)), pl.BlockSpec((tr, LANE), lambda i: (i, cfg.kt)), tab, tab, tab],
        out_specs=[pl.BlockSpec((tr, qw), lambda i: (i, 0)), pl.BlockSpec((tr, LANE), lambda i: (i, 0))],
        out_shape=[_sds((t, qw), BF16), _sds((t, LANE), BF16)], compiler_params=_cp(),
    )(qf, small, *tabs)


def rope_bwd(cfg, dq, dkpe, tabs, *, name):
    t, qw, lp = cfg.t, cfg.qw, cfg.lp
    tr = _pick(lp, 544, 16)
    nrb = lp // tr
    half = cfg.rope // 2

    def body(dq_ref, dk_ref, c_ref, s1_ref, s2_ref, qo_ref, ko_ref):
        c, s1, s2 = c_ref[...], s1_ref[...], s2_ref[...]
        for h in range(cfg.mh):
            a = h * 2 * LANE
            qo_ref[:, a:a + LANE] = dq_ref[:, a:a + LANE].astype(BF16)
            qo_ref[:, a + LANE:a + 2 * LANE] = _rope_t(dq_ref[:, a + LANE:a + 2 * LANE], c, s1, s2, half).astype(BF16)
        dk = dk_ref[0]
        for h in range(1, cfg.mh):
            dk = dk + dk_ref[h]
        ko_ref[...] = _rope_t(dk, c, s1, s2, half)

    tab = pl.BlockSpec((tr, LANE), lambda i: (i % nrb, 0))
    return pl.pallas_call(
        body, name=name, grid=(t // tr,),
        in_specs=[pl.BlockSpec((tr, qw), lambda i: (i, 0)), pl.BlockSpec((cfg.mh, tr, LANE), lambda i: (0, i, 0)),
                  tab, tab, tab],
        out_specs=[pl.BlockSpec((tr, qw), lambda i: (i, 0)), pl.BlockSpec((tr, LANE), lambda i: (i, 0))],
        out_shape=[_sds((t, qw), BF16), _sds((t, LANE), F32)], compiler_params=_cp(),
    )(dq, dkpe, *tabs)


def _q_blocks(cfg):
    bounds = [0, cfg.chunk] + list(range(cfg.chunk + 256, cfg.lp + 1, 256))
    assert bounds[-1] == cfg.lp, "SEQ must be a multiple of 256"
    return list(zip(bounds[:-1], bounds[1:]))


def _attn_mask(cfg, qs, qe):
    rows = qs + lax.broadcasted_iota(jnp.int32, (qe - qs, qe), 0)
    cols = lax.broadcasted_iota(jnp.int32, (qe - qs, qe), 1)
    return jnp.logical_and(cols <= rows, jnp.logical_or(cols >= cfg.pad, rows < cfg.pad))


def _max_q_block(cfg):
    return max(qe - qs for qs, qe in _q_blocks(cfg))


def _masked_scores(cfg, q, k2, qs, qe, s_scr, scale):
    bq, n = qe - qs, qe
    s_scr[0:bq, 0:n] = _nt(q, k2) * scale
    if qs == 0:
        s_scr[0:bq, 0:n] = jnp.where(_attn_mask(cfg, 0, qe), s_scr[0:bq, 0:n], MASK_VALUE)
    else:
        assert qs >= cfg.chunk and cfg.pad < LANE
        cols = lax.broadcasted_iota(jnp.int32, (bq, LANE), 1)
        s_scr[0:bq, 0:LANE] = jnp.where(cols >= cfg.pad, s_scr[0:bq, 0:LANE], MASK_VALUE)
        r = lax.broadcasted_iota(jnp.int32, (bq, bq), 0)
        c = lax.broadcasted_iota(jnp.int32, (bq, bq), 1)
        s_scr[0:bq, qs:qe] = jnp.where(c <= r, s_scr[0:bq, qs:qe], MASK_VALUE)
    return s_scr[0:bq, 0:n]


def attn_fwd(cfg, qr, kv, kpe, *, name):
    lp, t, mh = cfg.lp, cfg.t, cfg.mh
    scale = (cfg.nope + cfg.rope) ** -0.5
    blocks = _q_blocks(cfg)

    def body(q_ref, kv_ref, kp_ref, o_ref, l_ref, s_scr):
        for qs, qe in blocks:
            n = qe
            q = q_ref[qs:qe, :]
            k2 = jnp.concatenate([kv_ref[0:n, 0:LANE], kp_ref[0:n, :]], axis=1)
            s = _masked_scores(cfg, q, k2, qs, qe, s_scr, scale)
            m = jnp.max(s, axis=-1, keepdims=True)
            p = jnp.exp(s - m)
            l = jnp.sum(p, axis=-1, keepdims=True)
            pn = (p * (1.0 / l)).astype(BF16)
            o_ref[qs:qe, :] = _nn(pn, kv_ref[0:n, LANE:2 * LANE])
            l_ref[qs:qe, :] = jnp.broadcast_to(m + jnp.log(l), (qe - qs, LANE))

    hb = pl.BlockSpec((lp, 2 * LANE), lambda b, h: (b, h))
    ob = pl.BlockSpec((lp, LANE), lambda b, h: (b, h))
    return pl.pallas_call(
        body, name=name, grid=(cfg.bsz, mh),
        in_specs=[hb, hb, pl.BlockSpec((lp, LANE), lambda b, h: (b, 0))], out_specs=[ob, ob],
        out_shape=[_sds((t, mh * LANE), F32), _sds((t, mh * LANE), F32)],
        scratch_shapes=[pltpu.VMEM((_max_q_block(cfg), lp), F32)], compiler_params=_cp(),
    )(qr, kv, kpe)


def attn_bwd(cfg, qr, kv, kpe, o, lse, do, *, name):
    lp, t, mh = cfg.lp, cfg.t, cfg.mh
    scale = (cfg.nope + cfg.rope) ** -0.5
    blocks = _q_blocks(cfg)

    def body(q_ref, kv_ref, kp_ref, o_ref, l_ref, do_ref, dq_ref, dkv_ref, dkp_ref, dk_acc, dv_acc, s_scr):
        dk_acc[...] = jnp.zeros_like(dk_acc)
        dv_acc[...] = jnp.zeros_like(dv_acc)
        for qs, qe in blocks:
            n = qe
            q = q_ref[qs:qe, :]
            k2 = jnp.concatenate([kv_ref[0:n, 0:LANE], kp_ref[0:n, :]], axis=1)
            dov = do_ref[qs:qe, :]
            delta = jnp.sum(dov * o_ref[qs:qe, :], axis=-1, keepdims=True)
            dob = dov.astype(BF16)
            s = _masked_scores(cfg, q, k2, qs, qe, s_scr, scale)
            p = jnp.exp(s - l_ref[qs:qe, 0:1])
            dp = _nt(dob, kv_ref[0:n, LANE:2 * LANE])
            ds = (p * (dp - delta) * scale).astype(BF16)
            dq_ref[qs:qe, :] = _nn(ds, k2)
            dv_acc[0:n, :] += _tn(p.astype(BF16), dob)
            dk_acc[0:n, :] += _tn(ds, q)
        dkv_ref[:, 0:LANE] = dk_acc[:, 0:LANE].astype(BF16)
        dkv_ref[:, LANE:2 * LANE] = dv_acc[...].astype(BF16)
        dkp_ref[0] = dk_acc[:, LANE:2 * LANE]

    hb = pl.BlockSpec((lp, 2 * LANE), lambda b, h: (b, h))
    ob = pl.BlockSpec((lp, LANE), lambda b, h: (b, h))
    return pl.pallas_call(
        body, name=name, grid=(cfg.bsz, mh),
        in_specs=[hb, hb, pl.BlockSpec((lp, LANE), lambda b, h: (b, 0)), ob, ob, ob],
        out_specs=[hb, hb, pl.BlockSpec((1, lp, LANE), lambda b, h: (h, b, 0))],
        out_shape=[_sds((t, cfg.qw), F32), _sds((t, mh * 2 * LANE), BF16), _sds((mh, t, LANE), F32)],
        scratch_shapes=[pltpu.VMEM((lp, 2 * LANE), F32), pltpu.VMEM((lp, LANE), F32),
                        pltpu.VMEM((_max_q_block(cfg), lp), F32)], compiler_params=_cp(),
    )(qr, kv, kpe, o, lse, do)


def _live_rows(cfg, tr, shape):
    rows = pl.program_id(1) * tr + lax.broadcasted_iota(jnp.int32, shape, 0)
    return rows >= cfg.pad


def gate_fwd(cfg, ya, yb, g, *, name):
    d, lp = cfg.d, cfg.lp
    tr = _pick(lp, 544, 16)
    nrb = lp // tr

    def body(ya_ref, yb_ref, ga_ref, gb_ref, o_ref):
        mix = jax.nn.sigmoid(ga_ref[...]) * ya_ref[...] + jax.nn.sigmoid(gb_ref[...]) * yb_ref[...]
        o_ref[...] = jnp.where(_live_rows(cfg, tr, mix.shape), mix, 0.0).astype(BF16)

    row = pl.BlockSpec((tr, d), lambda b, j: (b * nrb + j, 0))
    row1 = pl.BlockSpec((tr, d), lambda b, j: (b * nrb + j, 1))
    return pl.pallas_call(
        body, name=name, grid=(cfg.bsz, nrb), in_specs=[row, row, row, row1], out_specs=row,
        out_shape=_sds((cfg.t, d), BF16), compiler_params=_cp(),
    )(ya, yb, g, g)


def gate_bwd(cfg, dmix, ya, yb, g, *, name):
    d, lp = cfg.d, cfg.lp
    tr = _pick(lp, 544, 16)
    nrb = lp // tr

    def body(dm_ref, ya_ref, yb_ref, ga_ref, gb_ref, dya_ref, dyb_ref, dg_ref):
        dm = dm_ref[...]
        dm = jnp.where(_live_rows(cfg, tr, dm.shape), dm, 0.0)
        sa = jax.nn.sigmoid(ga_ref[...])
        sb = jax.nn.sigmoid(gb_ref[...])
        dya_ref[...] = (dm * sa).astype(BF16)
        dyb_ref[...] = (dm * sb).astype(BF16)
        dg_ref[:, 0:d] = (dm * ya_ref[...] * sa * (1.0 - sa)).astype(BF16)
        dg_ref[:, d:2 * d] = (dm * yb_ref[...] * sb * (1.0 - sb)).astype(BF16)

    row = pl.BlockSpec((tr, d), lambda b, j: (b * nrb + j, 0))
    row1 = pl.BlockSpec((tr, d), lambda b, j: (b * nrb + j, 1))
    row2 = pl.BlockSpec((tr, 2 * d), lambda b, j: (b * nrb + j, 0))
    return pl.pallas_call(
        body, name=name, grid=(cfg.bsz, nrb), in_specs=[row, row, row, row, row1], out_specs=[row, row, row2],
        out_shape=[_sds((cfg.t, d), BF16), _sds((cfg.t, d), BF16), _sds((cfg.t, 2 * d), BF16)], compiler_params=_cp(),
    )(dmix, ya, yb, g, g)


def loss_head(cfg, h, target, w, *, name):
    d, q, nc = cfg.d, cfg.chunk, cfg.nchunks
    tpb = cfg.seq // q

    def body(h_ref, t_ref, w_ref, loss_ref, dh_ref, dw_ref):
        j = pl.program_id(1)

        @pl.when(jnp.logical_and(j == 0, pl.program_id(0) == 0))
        def _():
            loss_ref[...] = jnp.zeros_like(loss_ref)
            dw_ref[...] = jnp.zeros_like(dw_ref)

        @pl.when(j == 0)
        def _():
            dh_ref[...] = jnp.zeros_like(dh_ref)

        @pl.when(j > 0)
        def _():
            xv = h_ref[...]
            r = lax.rsqrt(jnp.mean(xv * xv, axis=-1, keepdims=True) + EPS)
            xh = xv * r
            err = xh * w_ref[...] - t_ref[...]
            loss_ref[...] += 0.5 * jnp.sum(jnp.sum(err * err, axis=-1, keepdims=True) / d, axis=0, keepdims=True)
            dy = err * (1.0 / d)
            g = dy * w_ref[...]
            dh_ref[...] = r * (g - xh * jnp.mean(g * xh, axis=-1, keepdims=True))
            dw_ref[...] += jnp.sum(dy * xh, axis=0, keepdims=True)

    row = pl.BlockSpec((q, d), lambda b, j: (b * nc + j, 0))
    loss, dh, dw = pl.pallas_call(
        body, name=name, grid=(cfg.bsz, nc),
        in_specs=[row, pl.BlockSpec((q, d), lambda b, j: (b * tpb + jnp.maximum(j - 1, 0), 0)),
                  pl.BlockSpec((1, d), lambda b, j: (0, 0))],
        out_specs=[pl.BlockSpec((8, LANE), lambda b, j: (0, 0)), row, pl.BlockSpec((1, d), lambda b, j: (0, 0))],
        out_shape=[_sds((8, LANE), F32), _sds((cfg.t, d), F32), _sds((1, d), F32)], compiler_params=_cp(),
    )(h, target, w.reshape(1, d))
    return loss[0, 0], dh, dw[0]


def _rows_tile(r, c):
    return _pick(r, max(8, (1 << 18) // max(c, 1) // 8 * 8), 8)


def _adam_update(w, g, m, v):
    c1 = 1.0 - ADAM_B1 ** ADAM_STEP
    c2 = 1.0 - ADAM_B2 ** ADAM_STEP
    mn = ADAM_B1 * m + (1.0 - ADAM_B1) * g
    vn = ADAM_B2 * v + (1.0 - ADAM_B2) * (g * g)
    delta = -ADAM_LR * ((mn / c1) / (jnp.sqrt(vn / c2) + ADAM_EPS) + ADAM_WD * w)
    return delta, mn, vn


def adamw_layer(w, m, v, g, li, prev, dep, *, name):
    _, r, c = w.shape
    tr = _rows_tile(r, c)

    def body(*refs):
        w_ref, m_ref, v_ref, g_ref = refs[:4]
        go_ref, d_ref, mo_ref, vo_ref = refs[-4:]
        gv = g_ref[...]
        delta, mn, vn = _adam_update(w_ref[0], gv, m_ref[0], v_ref[0])
        go_ref[0] = gv
        d_ref[0] = delta
        mo_ref[0] = mn
        vo_ref[0] = vn

    blk3 = pl.BlockSpec((1, tr, c), lambda i: (li, i, 0))
    anyspec = pl.BlockSpec(memory_space=pl.ANY)
    in_specs = [blk3, blk3, blk3, pl.BlockSpec((tr, c), lambda i: (i, 0)), anyspec]
    args = [w, m, v, g, dep]
    aliases = {}
    if prev is not None:
        in_specs += [anyspec] * 4
        args += list(prev)
        aliases = {5 + i: i for i in range(4)}
    return pl.pallas_call(
        body, name=name, grid=(r // tr,), in_specs=in_specs, out_specs=[blk3] * 4,
        out_shape=[_sds(w.shape, F32)] * 4, input_output_aliases=aliases, compiler_params=_cp(),
    )(*args)


def pair_add(g4, other, half, *, name):
    n, _, r, c = g4.shape
    tr = _rows_tile(r, c)

    def body(h_ref, a_ref, b_ref, o_ref):
        o_ref[0] = (a_ref[0, 0] + b_ref[0]).astype(BF16)

    blk = pl.BlockSpec((1, tr, c), lambda j, i, h: (j, i, 0))
    grid_spec = pltpu.PrefetchScalarGridSpec(
        num_scalar_prefetch=1, grid=(n, r // tr),
        in_specs=[pl.BlockSpec((1, 1, tr, c), lambda j, i, h: (j, h[0], i, 0)), blk], out_specs=blk)
    return pl.pallas_call(body, name=name, grid_spec=grid_spec, out_shape=_sds((n, r, c), BF16),
                          compiler_params=_cp())(half, g4, other)


def chip_sum(recv, part, where, *, name):
    n, r, c = recv.shape
    tr = _rows_tile(r, c)

    def body(s_ref, *refs):
        own_ref, o_ref = refs[n], refs[n + 1]
        acc = None
        for j in range(n):
            term = jnp.where(s_ref[0] == j, own_ref[0], refs[j][0]).astype(F32)
            acc = term if acc is None else acc + term
        o_ref[0] = acc

    def slot(j):
        return pl.BlockSpec((1, tr, c), lambda i, s: (jnp.where(s[0] == j, (j + 1) % n, j), i, 0))

    grid_spec = pltpu.PrefetchScalarGridSpec(
        num_scalar_prefetch=1, grid=(r // tr,),
        in_specs=[slot(j) for j in range(n)] + [pl.BlockSpec((1, tr, c), lambda i, s: (s[0], i, 0))],
        out_specs=pl.BlockSpec((1, tr, c), lambda i, s: (s[1], i, 0)))
    return pl.pallas_call(body, name=name, grid_spec=grid_spec, out_shape=_sds((2, r, c), F32),
                          compiler_params=_cp())(where, *([recv] * n), part)


def _coords():
    return lax.axis_index("x"), lax.axis_index("y"), lax.axis_index("c")


def _other_chips(x, y):
    return [(1 - x, y), (x, 1 - y), (1 - x, 1 - y)]


def gather_chips(arrs, *, name):
    n = len(arrs)
    anyspec = pl.BlockSpec(memory_space=pl.ANY)

    def body(*refs):
        ins, outs = refs[:n], refs[n:2 * n]
        send_sems, recv_sems, local_sems = refs[2 * n:]
        x, y, c = _coords()
        me = 2 * x + y
        chips = _other_chips(x, y)
        copies = []
        for k in range(n):
            loc = pltpu.make_async_copy(ins[k], outs[k].at[me], local_sems.at[k])
            loc.start()
            copies.append(loc)
        sends = []
        for k in range(n):
            for j, (px, py) in enumerate(chips):
                cp = pltpu.make_async_remote_copy(
                    src_ref=ins[k], dst_ref=outs[k].at[me], send_sem=send_sems.at[k, j], recv_sem=recv_sems.at[k, j],
                    device_id=(px, py, c), device_id_type=MESH)
                cp.start()
                sends.append(cp)
        for k in range(n):
            for j, (px, py) in enumerate(chips):
                pltpu.make_async_remote_copy(
                    src_ref=ins[k], dst_ref=outs[k].at[2 * px + py], send_sem=send_sems.at[k, j],
                    recv_sem=recv_sems.at[k, j], device_id=(px, py, c), device_id_type=MESH).wait_recv()
        for cp in sends:
            cp.wait_send()
        for cp in copies:
            cp.wait()

    return pl.pallas_call(
        body, name=name, in_specs=[anyspec] * n, out_specs=[anyspec] * n,
        out_shape=[_sds((4,) + a.shape, a.dtype) for a in arrs],
        scratch_shapes=[pltpu.SemaphoreType.DMA((n, 3)), pltpu.SemaphoreType.DMA((n, 3)), pltpu.SemaphoreType.DMA((n,))],
        compiler_params=_cp(has_side_effects=True),
    )(*arrs)


def allreduce_small(vec, after, *, name):
    r, c = vec.shape

    def body(v_ref, after_ref, o_ref, buf, send_sems, recv_sems):
        x, y, cc = _coords()
        me = 4 * x + 2 * y + cc
        buf[me] = v_ref[...]
        sends = []
        flips = [(fx, fy, fc) for fx in (0, 1) for fy in (0, 1) for fc in (0, 1)][1:]
        for j, (fx, fy, fc) in enumerate(flips):
            peer = ((1 - x) if fx else x, (1 - y) if fy else y, (1 - cc) if fc else cc)
            cp = pltpu.make_async_remote_copy(
                src_ref=v_ref, dst_ref=buf.at[me], send_sem=send_sems.at[j], recv_sem=recv_sems.at[j],
                device_id=peer, device_id_type=MESH)
            cp.start()
            sends.append(cp)
        for j, (fx, fy, fc) in enumerate(flips):
            px, py, pc = ((1 - x) if fx else x, (1 - y) if fy else y, (1 - cc) if fc else cc)
            pltpu.make_async_remote_copy(
                src_ref=v_ref, dst_ref=buf.at[4 * px + 2 * py + pc], send_sem=send_sems.at[j],
                recv_sem=recv_sems.at[j], device_id=(px, py, pc), device_id_type=MESH).wait_recv()
        for cp in sends:
            cp.wait_send()
        acc = buf[0]
        for k in range(1, 8):
            acc = acc + buf[k]
        o_ref[...] = acc

    vm = pl.BlockSpec(memory_space=pltpu.VMEM)
    return pl.pallas_call(
        body, name=name, in_specs=[vm, pl.BlockSpec(memory_space=pl.ANY)], out_specs=vm, out_shape=_sds((r, c), F32),
        scratch_shapes=[pltpu.VMEM((8, r, c), F32), pltpu.SemaphoreType.DMA((7,)), pltpu.SemaphoreType.DMA((7,))],
        compiler_params=_cp(has_side_effects=True),
    )(vec, after)


def pair_exchange(arrs, *, name):
    n = len(arrs)
    anyspec = pl.BlockSpec(memory_space=pl.ANY)

    def body(*refs):
        ins, outs = refs[:n], refs[n:2 * n]
        send_sems, recv_sems = refs[2 * n:]
        x, y, c = _coords()
        sends = []
        for k in range(n):
            for j in range(4):
                cp = pltpu.make_async_remote_copy(
                    src_ref=ins[k].at[j, 1 - c], dst_ref=outs[k].at[j], send_sem=send_sems.at[k, j],
                    recv_sem=recv_sems.at[k, j], device_id=(x, y, 1 - c), device_id_type=MESH)
                cp.start()
                sends.append(cp)
        for cp in sends:
            cp.wait()

    return pl.pallas_call(
        body, name=name, in_specs=[anyspec] * n, out_specs=[anyspec] * n,
        out_shape=[_sds((a.shape[0],) + a.shape[2:], a.dtype) for a in arrs],
        scratch_shapes=[pltpu.SemaphoreType.DMA((n, 4)), pltpu.SemaphoreType.DMA((n, 4))],
        compiler_params=_cp(has_side_effects=True),
    )(*arrs)


def pair_share(lands, *, name):
    n = len(lands)
    anyspec = pl.BlockSpec(memory_space=pl.ANY)

    def body(*refs):
        ins, outs = refs[:n], refs[n:2 * n]
        send_sems, recv_sems = refs[2 * n:]
        x, y, c = _coords()
        sends = []
        for k in range(n):
            for j, (px, py) in enumerate(_other_chips(x, y)):
                cp = pltpu.make_async_remote_copy(
                    src_ref=ins[k].at[2 * px + py, c], dst_ref=outs[k].at[2 * px + py, c], send_sem=send_sems.at[k, j],
                    recv_sem=recv_sems.at[k, j], device_id=(x, y, 1 - c), device_id_type=MESH)
                cp.start()
                sends.append(cp)
        for k in range(n):
            for j, (px, py) in enumerate(_other_chips(x, y)):
                pltpu.make_async_remote_copy(
                    src_ref=ins[k].at[2 * px + py, c], dst_ref=outs[k].at[2 * px + py, 1 - c],
                    send_sem=send_sems.at[k, j], recv_sem=recv_sems.at[k, j], device_id=(x, y, 1 - c),
                    device_id_type=MESH).wait_recv()
        for cp in sends:
            cp.wait_send()

    return pl.pallas_call(
        body, name=name, in_specs=[anyspec] * n, out_specs=[anyspec] * n,
        out_shape=[_sds(a.shape, a.dtype) for a in lands], input_output_aliases={k: k for k in range(n)},
        scratch_shapes=[pltpu.SemaphoreType.DMA((n, 3)), pltpu.SemaphoreType.DMA((n, 3))],
        compiler_params=_cp(has_side_effects=True),
    )(*lands)


def pair_fill(arrs, *, name):
    n = len(arrs)
    anyspec = pl.BlockSpec(memory_space=pl.ANY)

    def body(*refs):
        ins, outs = refs[:n], refs[n:2 * n]
        send_sems, recv_sems = refs[2 * n:]
        x, y, c = _coords()
        sends = []
        for k in range(n):
            cp = pltpu.make_async_remote_copy(
                src_ref=ins[k].at[c], dst_ref=outs[k].at[c], send_sem=send_sems.at[k], recv_sem=recv_sems.at[k],
                device_id=(x, y, 1 - c), device_id_type=MESH)
            cp.start()
            sends.append(cp)
        for k in range(n):
            pltpu.make_async_remote_copy(
                src_ref=ins[k].at[c], dst_ref=outs[k].at[1 - c], send_sem=send_sems.at[k], recv_sem=recv_sems.at[k],
                device_id=(x, y, 1 - c), device_id_type=MESH).wait_recv()
        for cp in sends:
            cp.wait_send()

    return pl.pallas_call(
        body, name=name, in_specs=[anyspec] * n, out_specs=[anyspec] * n,
        out_shape=[_sds(a.shape, a.dtype) for a in arrs], input_output_aliases={k: k for k in range(n)},
        scratch_shapes=[pltpu.SemaphoreType.DMA((n,)), pltpu.SemaphoreType.DMA((n,))],
        compiler_params=_cp(has_side_effects=True),
    )(*arrs)


_HBM = pl.BlockSpec(memory_space=pltpu.HBM)
_SEM = pl.BlockSpec(memory_space=pltpu.SEMAPHORE)


def _ici_copies(kind, srcs, lands, send_sems, recv_sems):
    x, y, c = _coords()
    me = 2 * x + y
    sends, recvs = [], []
    for k in range(len(srcs)):
        for j, (px, py) in enumerate(_other_chips(x, y)):
            peer = 2 * px + py
            if kind == "gather":
                src, there, here = srcs[k].at[c], lands[k].at[me, c], lands[k].at[peer, c]
            else:
                src, there, here = srcs[k].at[peer], lands[k].at[me], lands[k].at[peer]
            sem = 3 * k + j
            mk = functools.partial(pltpu.make_async_remote_copy, src_ref=src, send_sem=send_sems.at[sem],
                                   recv_sem=recv_sems.at[sem], device_id=(px, py, c), device_id_type=MESH)
            sends.append(mk(dst_ref=there))
            recvs.append(mk(dst_ref=here))
    return sends, recvs


def ici_start(kind, srcs, lands, after, *, name):
    n = len(srcs)

    def body(*refs):
        src_refs, land_refs = refs[:n], refs[n:2 * n]
        send_sems, recv_sems = refs[2 * n + 1], refs[2 * n + 2]
        token = refs[-1]
        sends, _ = _ici_copies(kind, src_refs, land_refs, send_sems, recv_sems)
        for cp in sends:
            cp.start()
        token[...] = jnp.zeros_like(token)

    both = list(srcs) + list(lands)
    out = pl.pallas_call(
        body, name=name,
        in_specs=[_HBM] * (2 * n) + [pl.BlockSpec(memory_space=pl.ANY)],
        out_shape=(pltpu.SemaphoreType.DMA((3 * n,)), pltpu.SemaphoreType.DMA((3 * n,)),
                   *[pltpu.HBM(a.shape, a.dtype) for a in both], _sds((8, LANE), F32)),
        out_specs=(_SEM, _SEM, *([_HBM] * (2 * n)), pl.BlockSpec(memory_space=pltpu.VMEM)),
        input_output_aliases={i: 2 + i for i in range(2 * n)},
        compiler_params=_cp(has_side_effects=pltpu.SideEffectType.DATAFLOW_SIDE_EFFECTING),
    )(*[pltpu.with_memory_space_constraint(a, pltpu.HBM) for a in both], after)
    return out[0], out[1], list(out[2:2 + n]), list(out[2 + n:2 + 2 * n]), out[-1]


def ici_wait(kind, started, after, *, name):
    send_sems, recv_sems, srcs, lands, _ = started
    n = len(srcs)

    def body(*refs):
        src_refs, land_refs = refs[:n], refs[n:2 * n]
        sends, recvs = _ici_copies(kind, src_refs, land_refs, refs[2 * n], refs[2 * n + 1])
        for cp in sends:
            cp.wait_send()
        for cp in recvs:
            cp.wait_recv()

    both = list(srcs) + list(lands)
    out = pl.pallas_call(
        body, name=name,
        in_specs=[_HBM] * (2 * n) + [_SEM, _SEM, pl.BlockSpec(memory_space=pl.ANY)],
        out_shape=tuple(pltpu.HBM(a.shape, a.dtype) for a in both), out_specs=tuple([_HBM] * (2 * n)),
        input_output_aliases={i: i for i in range(2 * n)},
        compiler_params=_cp(has_side_effects=pltpu.SideEffectType.DATAFLOW_SIDE_EFFECTING),
    )(*both, send_sems, recv_sems, after)
    return list(out[:n]), list(out[n:])


BIG = ["w_in", "w_uq", "w_ukv", "w_branch_ssm", "w_branch_mla", "w_out", "w_mlp_up", "w_mlp_down"]
COL_SHARDED = {"w_in", "w_uq", "w_ukv", "w_mlp_up"}
SMALL_REPL = ["norm_mix_w", "conv_b", "dt_bias", "a_log", "d_skip", "ssm_norm_w", "q_norm_w", "kv_norm_w", "norm_mlp_w"]


def _unshard_layer(name, g):
    _, r, c = g.shape
    if name in COL_SHARDED:
        return jnp.transpose(g, (1, 0, 2)).reshape(r, 4 * c)
    return g.reshape(4 * r, c)


def _to_shards(name, full):
    r, c = full.shape
    if name in COL_SHARDED:
        return jnp.transpose(full.reshape(r, 4, c // 4), (1, 0, 2))
    return full.reshape(4, r // 4, c)


def prep_layer(cfg, w):
    sp = np.cumsum(cfg.in_splits)[:-1].tolist()
    z, xbc, dt, cq, ckv, kr, gs, gm = jnp.split(w["w_in"], sp, axis=1)
    zpad = lambda n: jnp.zeros((cfg.d, n), z.dtype)
    out = dict(
        w_z=z, w_xbc=xbc, w_g=jnp.concatenate([gs, gm], axis=1),
        w_s=jnp.concatenate([cq, ckv, kr, zpad(LANE - cfg.rope), dt, zpad(LANE - cfg.heads)], axis=1),
        w_uq=jnp.pad(w["w_uq"].reshape(cfg.ql, cfg.mh, cfg.nope + cfg.rope),
                     ((0, 0), (0, 0), (0, 2 * LANE - cfg.nope - cfg.rope))).reshape(cfg.ql, cfg.qw),
        w_ukv=w["w_ukv"], w_bs=w["w_branch_ssm"], w_bm=w["w_branch_mla"], w_out=w["w_out"],
        w_up=w["w_mlp_up"], w_down=w["w_mlp_down"])
    return {k: v.astype(BF16) for k, v in out.items()}


def unprep_grads(cfg, g):
    ql, kvl = cfg.ql, cfg.kvl
    ds_ = g["w_s"]
    cq, ckv = ds_[:, :ql], ds_[:, ql:ql + kvl]
    kr = ds_[:, ql + kvl:ql + kvl + cfg.rope]
    dt = ds_[:, ql + kvl + LANE:ql + kvl + LANE + cfg.heads]
    w_in = jnp.concatenate([g["w_z"], g["w_xbc"], dt, cq, ckv, kr, g["w_g"]], axis=1)
    w_uq = g["w_uq"].reshape(cfg.ql, cfg.mh, 2 * LANE)[:, :, :cfg.nope + cfg.rope].reshape(cfg.ql, -1)
    return dict(w_in=w_in, w_uq=w_uq, w_ukv=g["w_ukv"], w_branch_ssm=g["w_bs"], w_branch_mla=g["w_bm"],
                w_out=g["w_out"], w_mlp_up=g["w_up"], w_mlp_down=g["w_down"])


def layer_fwd(cfg, h, pw, sm, tabs, li):
    n = lambda s: f"l{li}_{s}"
    u = rmsnorm_fwd(h, sm["norm_mix_w"], name=n("norm_mix"))
    z = matmul(u, pw["w_z"], name=n("in_z"))
    xbc = matmul(u, pw["w_xbc"], name=n("in_xbc"))
    g = matmul(u, pw["w_g"], name=n("in_g"))
    small = matmul(u, pw["w_s"], name=n("in_s"), tn=cfg.sw)
    xc = conv_fwd(cfg, xbc, sm["conv_w"], sm["conv_b"], name=n("conv"))
    y, sin = ssd_fwd(cfg, xc, small, sm["dt_bias_p"], sm["avec"], sm["dexp"], name=n("ssd"))
    y_ssm = tail_fwd(cfg, y, z, sm["ssm_norm_w"], name=n("tail"))
    cqn = rmsnorm_fwd(small, sm["q_norm_w"], cw=cfg.ql, ci=0, name=n("q_norm"))
    ckvn = rmsnorm_fwd(small, sm["kv_norm_w"], cw=cfg.kvl, ci=cfg.ql // cfg.kvl, name=n("kv_norm"))
    qf = matmul(cqn, pw["w_uq"], name=n("uq"))
    kv = matmul(ckvn, pw["w_ukv"], out_dtype=BF16, name=n("ukv"))
    qr, kpe = rope_fwd(cfg, qf, small, tabs, name=n("rope"))
    o, lse = attn_fwd(cfg, qr, kv, kpe, name=n("attn"))
    ya = matmul(y_ssm, pw["w_bs"], name=n("branch_ssm"))
    yb = matmul(o, pw["w_bm"], name=n("branch_mla"))
    mixed = gate_fwd(cfg, ya, yb, g, name=n("gate"))
    h1 = matmul(mixed, pw["w_out"], add=h, name=n("out"))
    v = rmsnorm_fwd(h1, sm["norm_mlp_w"], name=n("norm_mlp"))
    a, act = matmul(v, pw["w_up"], name=n("up"), epilogue=_ep_relu2, out_dtypes=(BF16, BF16))
    h2 = matmul(act, pw["w_down"], add=h1, name=n("down"))
    saved = dict(h=h, u=u, z=z, xbc=xbc, g=g, small=small, xc=xc, y=y, sin=sin, y_ssm=y_ssm, cqn=cqn, ckvn=ckvn,
                 qr=qr, kv=kv, kpe=kpe, o=o, lse=lse, ya=ya, yb=yb, mixed=mixed, h1=h1, v=v, a=a, act=act)
    return h2, saved


def layer_bwd(cfg, dh2, pw, sm, tabs, s, li):
    n = lambda t: f"l{li}_b_{t}"
    gw, gs = {}, {}
    gw["w_down"] = matmul(s["act"], dh2, ta=True, name=n("dw_down"))
    da = matmul(dh2, pw["w_down"], tb=True, name=n("dact"), epilogue=_ep_relu2_grad, extras=(s["a"],),
                out_dtypes=(BF16,))
    gw["w_up"] = matmul(s["v"], da, ta=True, name=n("dw_up"))
    dv = matmul(da, pw["w_up"], tb=True, name=n("dv"))
    dh1, gs["norm_mlp_w"] = rmsnorm_bwd(dv, s["h1"], sm["norm_mlp_w"], res=dh2, name=n("norm_mlp"))
    gw["w_out"] = matmul(s["mixed"], dh1, ta=True, name=n("dw_out"))
    dmix = matmul(dh1, pw["w_out"], tb=True, name=n("dmix"))
    dya, dyb, dg = gate_bwd(cfg, dmix, s["ya"], s["yb"], s["g"], name=n("gate"))
    gw["w_bs"] = matmul(s["y_ssm"], dya, ta=True, name=n("dw_bs"))
    gw["w_bm"] = matmul(s["o"], dyb, ta=True, name=n("dw_bm"))
    dy_ssm = matmul(dya, pw["w_bs"], tb=True, name=n("dy_ssm"))
    do = matmul(dyb, pw["w_bm"], tb=True, name=n("do"))
    dq, dkv, dkpe = attn_bwd(cfg, s["qr"], s["kv"], s["kpe"], s["o"], s["lse"], do, name=n("attn"))
    dqf, dkr = rope_bwd(cfg, dq, dkpe, tabs, name=n("rope"))
    gw["w_uq"] = matmul(s["cqn"], dqf, ta=True, name=n("dw_uq"))
    gw["w_ukv"] = matmul(s["ckvn"], dkv, ta=True, name=n("dw_ukv"))
    dcqn = matmul(dqf, pw["w_uq"], tb=True, name=n("dcqn"))
    dckvn = matmul(dkv, pw["w_ukv"], tb=True, name=n("dckvn"))
    dcq, gs["q_norm_w"] = rmsnorm_bwd(dcqn, s["small"], sm["q_norm_w"], cw=cfg.ql, ci=0, out_dtype=BF16, name=n("q_norm"))
    dckv, gs["kv_norm_w"] = rmsnorm_bwd(dckvn, s["small"], sm["kv_norm_w"], cw=cfg.kvl, ci=cfg.ql // cfg.kvl,
                                        out_dtype=BF16, name=n("kv_norm"))
    dy, dz, gs["ssm_norm_w"] = tail_bwd(cfg, dy_ssm, s["y"], s["z"], sm["ssm_norm_w"], name=n("tail"))
    dxc, ddt, ddexp, dav, dbias = ssd_bwd(cfg, s["xc"], s["small"], sm["dt_bias_p"], sm["avec"], sm["dexp"],
                                          s["sin"], dy, name=n("ssd"))
    dxbc, gs["conv_w"], gs["conv_b"] = conv_bwd(cfg, s["xbc"], sm["conv_w"], sm["conv_b"], dxc, name=n("conv"))
    gs["d_skip"] = ddexp.reshape(cfg.heads, cfg.hd).sum(axis=1)
    gs["a_log"] = (dav[0] * sm["avec"][0])[:cfg.heads]
    gs["dt_bias"] = dbias[0, :cfg.heads]
    dsmall = jnp.concatenate([dcq, dckv, dkr.astype(BF16), ddt.astype(BF16)], axis=1)
    gw["w_z"] = matmul(s["u"], dz, ta=True, name=n("dw_z"))
    gw["w_xbc"] = matmul(s["u"], dxbc, ta=True, name=n("dw_xbc"))
    gw["w_g"] = matmul(s["u"], dg, ta=True, name=n("dw_g"))
    gw["w_s"] = matmul(s["u"], dsmall, ta=True, name=n("dw_s"))
    du = matmul(dz, pw["w_z"], tb=True, name=n("du_z"))
    du = matmul(dxbc, pw["w_xbc"], tb=True, add=du, name=n("du_xbc"))
    du = matmul(dg, pw["w_g"], tb=True, add=du, name=n("du_g"))
    du = matmul(dsmall, pw["w_s"], tb=True, add=du, name=n("du_s"))
    dh, gs["norm_mix_w"] = rmsnorm_bwd(du, s["h"], sm["norm_mix_w"], res=dh1, name=n("norm_mix"))
    return dh, gw, gs


def small_params(cfg, p, li):
    pad_l = lambda v: jnp.pad(v, (0, LANE - v.shape[0])).reshape(1, LANE)
    return dict(
        norm_mix_w=p["norm_mix_w"][li], conv_w=p["conv_w"][li], conv_b=p["conv_b"][li],
        dt_bias_p=pad_l(p["dt_bias"][li]), avec=pad_l(-jnp.exp(p["a_log"][li])),
        dexp=jnp.repeat(p["d_skip"][li], cfg.hd).reshape(1, cfg.inner),
        ssm_norm_w=p["ssm_norm_w"][li], q_norm_w=p["q_norm_w"][li], kv_norm_w=p["kv_norm_w"][li],
        norm_mlp_w=p["norm_mlp_w"][li])


def local_step(cfg, x, target, p, depth=2):
    bsz, d = cfg.bsz, cfg.d
    lead = jnp.zeros((bsz, cfg.pad, d), F32)
    meta = jnp.broadcast_to(p["meta_tokens"][None], (bsz, cfg.n_meta, d))
    h = jnp.concatenate([lead, meta, x], axis=1).reshape(cfg.t, d)
    tabs = rope_tables(cfg)
    saved, sms = [], []
    for li in range(depth):
        sm = small_params(cfg, p, li)
        h, s = layer_fwd(cfg, h, p["pw"][li], sm, tabs, li)
        saved.append(s)
        sms.append(sm)
    loss, dh, dfw = loss_head(cfg, h, target.reshape(bsz * cfg.seq, d), p["final_norm_w"], name="loss_head")
    gws, gss = [None] * depth, [None] * depth
    for li in reversed(range(depth)):
        dh, gws[li], gss[li] = layer_bwd(cfg, dh, p["pw"][li], sms[li], tabs, saved[li], li)
    dh = dh.reshape(bsz, cfg.lp, d)
    grad_x = dh[:, cfg.chunk:, :]
    gmeta = jnp.sum(dh[:, cfg.pad:cfg.chunk, :], axis=0)
    return loss, grad_x, gmeta, gws, gss, dfw


def _pack_small(parts):
    flat = jnp.concatenate([a.reshape(-1) for a in parts])
    n = flat.shape[0]
    npad = -n % (8 * LANE)
    return jnp.pad(flat, (0, npad)).reshape(-1, LANE), n


def _unpack_small(vec, shapes):
    flat = vec.reshape(-1)
    out, off = [], 0
    for sh in shapes:
        sz = int(np.prod(sh))
        out.append(flat[off:off + sz].reshape(sh))
        off += sz
    return out


def _as2d(a):
    return a.reshape(-1, a.shape[-1])


def kernel(x, meta_tokens, norm_mix_w, w_in, conv_w, conv_b, dt_bias, a_log, d_skip, ssm_norm_w, q_norm_w, kv_norm_w, w_uq, w_ukv, w_branch_ssm, w_branch_mla, w_out, norm_mlp_w, w_mlp_up, w_mlp_down, final_norm_w, loss_target, m_meta_tokens, m_norm_mix_w, m_w_in, m_conv_w, m_conv_b, m_dt_bias, m_a_log, m_d_skip, m_ssm_norm_w, m_q_norm_w, m_kv_norm_w, m_w_uq, m_w_ukv, m_w_branch_ssm, m_w_branch_mla, m_w_out, m_norm_mlp_w, m_w_mlp_up, m_w_mlp_down, m_final_norm_w, v_meta_tokens, v_norm_mix_w, v_w_in, v_conv_w, v_conv_b, v_dt_bias, v_a_log, v_d_skip, v_ssm_norm_w, v_q_norm_w, v_kv_norm_w, v_w_uq, v_w_ukv, v_w_branch_ssm, v_w_branch_mla, v_w_out, v_norm_mlp_w, v_w_mlp_up, v_w_mlp_down, v_final_norm_w):
    cfg = CFG
    names = ["meta_tokens", "norm_mix_w", "w_in", "conv_w", "conv_b", "dt_bias", "a_log", "d_skip", "ssm_norm_w",
             "q_norm_w", "kv_norm_w", "w_uq", "w_ukv", "w_branch_ssm", "w_branch_mla", "w_out", "norm_mlp_w",
             "w_mlp_up", "w_mlp_down", "final_norm_w"]
    wts = dict(zip(names, [meta_tokens, norm_mix_w, w_in, conv_w, conv_b, dt_bias, a_log, d_skip, ssm_norm_w,
                           q_norm_w, kv_norm_w, w_uq, w_ukv, w_branch_ssm, w_branch_mla, w_out, norm_mlp_w,
                           w_mlp_up, w_mlp_down, final_norm_w]))
    ms = dict(zip(names, [m_meta_tokens, m_norm_mix_w, m_w_in, m_conv_w, m_conv_b, m_dt_bias, m_a_log, m_d_skip,
                          m_ssm_norm_w, m_q_norm_w, m_kv_norm_w, m_w_uq, m_w_ukv, m_w_branch_ssm, m_w_branch_mla,
                          m_w_out, m_norm_mlp_w, m_w_mlp_up, m_w_mlp_down, m_final_norm_w]))
    vs = dict(zip(names, [v_meta_tokens, v_norm_mix_w, v_w_in, v_conv_w, v_conv_b, v_dt_bias, v_a_log, v_d_skip,
                          v_ssm_norm_w, v_q_norm_w, v_kv_norm_w, v_w_uq, v_w_ukv, v_w_branch_ssm, v_w_branch_mla,
                          v_w_out, v_norm_mlp_w, v_w_mlp_up, v_w_mlp_down, v_final_norm_w]))
    cx, cy, cc = _coords()
    chip = 2 * cx + cy

    half1 = jnp.reshape(cc, (1,)).astype(jnp.int32)
    where2 = jnp.stack([chip, cc]).astype(jnp.int32)
    wb = {k: wts[k].astype(BF16) for k in BIG}
    zero_tok = jnp.zeros((8, LANE), F32)

    def halves(a):
        return a.reshape((2, a.shape[0] // 2) + a.shape[1:])

    def gather_start(li, after):
        srcs = [halves(wb[k][li]) for k in BIG]
        lands = [lax.empty((4,) + s.shape, BF16) for s in srcs]
        return ici_start("gather", srcs, lands, after, name=f"gather{li}_start")

    def gather_finish(li, started, after):
        srcs, lands = ici_wait("gather", started, after, name=f"gather{li}_wait")
        lands = pair_share(lands, name=f"gather{li}_share")
        full = {}
        for k, own, land in zip(BIG, srcs, lands):
            slots = [jnp.where(chip == j, own, land[j]) for j in range(4)]
            full[k] = _unshard_layer(k, jnp.stack(slots).reshape((4, 2 * own.shape[1], own.shape[2])))
        return prep_layer(cfg, full), lands[0]

    def reduce_start(li, gw, after):
        ug = unprep_grads(cfg, gw)
        g4 = []
        for k in BIG:
            s = _to_shards(k, ug[k])
            g4.append(s.reshape(4, 2, s.shape[1] // 2, s.shape[2]))
        theirs = pair_exchange(g4, name=f"grad{li}_pair_exchange")
        parts = [pair_add(a, b, half1, name=f"grad{li}_pair_add_{k}") for k, a, b in zip(BIG, g4, theirs)]
        lands = [lax.empty(q.shape, q.dtype) for q in parts]
        return ici_start("scatter", parts, lands, after, name=f"grad{li}_scatter_start")

    def reduce_finish(li, started, after):
        parts, lands = ici_wait("scatter", started, after, name=f"grad{li}_scatter_wait")
        sums = [chip_sum(rc, pt, where2, name=f"grad{li}_chip_sum_{k}") for k, rc, pt in zip(BIG, lands, parts)]
        sums = pair_fill(sums, name=f"grad{li}_pair_fill")
        return {k: s.reshape(2 * s.shape[1], s.shape[2]) for k, s in zip(BIG, sums)}

    gathered = gather_chips([meta_tokens, conv_w], name="gather_small")
    p = dict(wts)
    p["meta_tokens"] = jnp.transpose(gathered[0], (1, 0, 2)).reshape(cfg.n_meta, cfg.d)
    p["conv_w"] = jnp.transpose(gathered[1], (1, 2, 0, 3)).reshape(2, cfg.convk, cfg.conv_dim)

    st0 = gather_start(0, gathered[0])
    pw0, dep0 = gather_finish(0, st0, st0[4])
    st1 = gather_start(1, dep0)

    bsz, d = cfg.bsz, cfg.d
    lead = jnp.zeros((bsz, cfg.pad, d), F32)
    meta = jnp.broadcast_to(p["meta_tokens"][None], (bsz, cfg.n_meta, d))
    h0 = jnp.concatenate([lead, meta, x], axis=1).reshape(cfg.t, d)
    tabs = rope_tables(cfg)
    sm0 = small_params(cfg, p, 0)
    sm0["norm_mix_w"] = sm0["norm_mix_w"] + st1[4][0, 0]
    h1, sv0 = layer_fwd(cfg, h0, pw0, sm0, tabs, 0)
    pw1, _ = gather_finish(1, st1, h1)
    sm1 = small_params(cfg, p, 1)
    h2, sv1 = layer_fwd(cfg, h1, pw1, sm1, tabs, 1)
    loss, dh, dfw = loss_head(cfg, h2, loss_target.reshape(bsz * cfg.seq, d), final_norm_w, name="loss_head")
    loss = lax.psum(loss, ("x", "y", "c"))

    dh, gw1, gs1 = layer_bwd(cfg, dh, pw1, sm1, tabs, sv1, 1)
    red1 = reduce_start(1, gw1, zero_tok)
    sm0b = dict(sm0)
    sm0b["norm_mlp_w"] = sm0["norm_mlp_w"] + red1[4][0, 0]
    dh, gw0, gs0 = layer_bwd(cfg, dh, pw0, sm0b, tabs, sv0, 0)
    dh3 = dh.reshape(bsz, cfg.lp, d)
    grad_x = dh3[:, cfg.chunk:, :]
    gmeta = jnp.sum(dh3[:, cfg.pad:cfg.chunk, :], axis=0)
    big1 = reduce_finish(1, red1, dh)

    small_names = SMALL_REPL + ["conv_w"]
    parts = [jnp.stack([gs0[k], gs1[k]]) for k in small_names] + [dfw, gmeta]
    shapes = [a.shape for a in parts]
    vec, _ = _pack_small(parts)
    red_vec = allreduce_small(vec, big1[BIG[-1]], name="allreduce_small")
    red = _unpack_small(red_vec, shapes)
    sg = dict(zip(small_names + ["final_norm_w", "meta_tokens"], red))
    sg["conv_w"] = lax.dynamic_slice_in_dim(sg["conv_w"], chip * (cfg.conv_dim // 4), cfg.conv_dim // 4, axis=2)
    sg["meta_tokens"] = lax.dynamic_slice_in_dim(sg["meta_tokens"], chip * (cfg.d // 4), cfg.d // 4, axis=1)

    red0 = reduce_start(0, gw0, red_vec)
    grads, deltas, new_m, new_v = {}, {}, {}, {}
    dep = red0[4]
    for k in names:
        if k in BIG:
            continue
        w2, g2, m2, v2 = _as2d(wts[k]), _as2d(sg[k]), _as2d(ms[k]), _as2d(vs[k])
        dl, mn, vn = adamw_small(w2, g2, m2, v2, dep, name=f"adamw_{k}")
        grads[k] = sg[k].reshape(wts[k].shape)
        deltas[k], new_m[k], new_v[k] = (t.reshape(wts[k].shape) for t in (dl, mn, vn))
    outs = {}
    for k in BIG:
        outs[k] = adamw_layer(wts[k], ms[k], vs[k], big1[k], 1, None, dep, name=f"adamw1_{k}")
        dep = outs[k][1]
    big0 = reduce_finish(0, red0, dep)
    for k in BIG:
        outs[k] = adamw_layer(wts[k], ms[k], vs[k], big0[k], 0, outs[k], dep, name=f"adamw0_{k}")
        grads[k], deltas[k], new_m[k], new_v[k] = outs[k]
    return (loss, grad_x, *[grads[k] for k in names], *[deltas[k] for k in names],
            *[new_m[k] for k in names], *[new_v[k] for k in names])


def adamw_small(w, g, m, v, dep, *, name):
    def body(w_ref, g_ref, m_ref, v_ref, dep_ref, d_ref, mo_ref, vo_ref):
        d_ref[...], mo_ref[...], vo_ref[...] = _adam_update(w_ref[...], g_ref[...], m_ref[...], v_ref[...])

    vm = pl.BlockSpec(memory_space=pltpu.VMEM)
    return pl.pallas_call(body, name=name, in_specs=[vm] * 4 + [pl.BlockSpec(memory_space=pl.ANY)], out_specs=[vm] * 3,
                          out_shape=[_sds(w.shape, F32)] * 3, compiler_params=_cp())(w, g, m, v, dep)
```

```python
import functools
import math
from typing import NamedTuple

import numpy as np
import jax
import jax.numpy as jnp
from jax import lax
from jax.experimental import pallas as pl
from jax.experimental.pallas import tpu as pltpu

F32 = jnp.float32
BF16 = jnp.bfloat16
HI = lax.Precision.HIGHEST
EPS = 1e-6
ROPE_THETA = 10000.0
LANE = 128
VMEM_LIMIT = 56 * 1024 * 1024
MASK_VALUE = -1e30
ADAM_LR, ADAM_B1, ADAM_B2, ADAM_EPS, ADAM_WD, ADAM_STEP = 0.001, 0.9, 0.999, 1e-08, 0.01, 10
MESH = pl.DeviceIdType.MESH


class Cfg(NamedTuple):
    d: int = 1024
    seq: int = 2048
    bsz: int = 2
    n_meta: int = 16
    inner: int = 2048
    hd: int = 64
    groups: int = 4
    state: int = 128
    convk: int = 4
    chunk: int = 128
    mh: int = 8
    ql: int = 512
    kvl: int = 256
    nope: int = 128
    rope: int = 64
    vd: int = 128
    ff: int = 4096

    @property
    def heads(self): return self.inner // self.hd
    @property
    def gw(self): return self.inner // self.groups
    @property
    def conv_dim(self): return self.inner + 2 * self.groups * self.state
    @property
    def pad(self): return self.chunk - self.n_meta
    @property
    def lp(self): return self.chunk + self.seq
    @property
    def t(self): return self.bsz * self.lp
    @property
    def nchunks(self): return self.lp // self.chunk
    @property
    def sw(self): return self.ql + self.kvl + 2 * LANE
    @property
    def kt(self): return (self.ql + self.kvl) // LANE
    @property
    def dtt(self): return self.kt + 1
    @property
    def qw(self): return self.mh * 2 * LANE
    @property
    def in_splits(self):
        return [self.inner, self.conv_dim, self.heads, self.ql, self.kvl, self.rope, self.d, self.d]


CFG = Cfg()


def _pick(dim, pref, mult):
    best = None
    for t in range(mult, min(dim, pref) + 1, mult):
        if dim % t == 0:
            best = t
    return best if best is not None else dim


def _cp(**kw):
    return pltpu.CompilerParams(vmem_limit_bytes=VMEM_LIMIT, **kw)


def _sds(shape, dtype):
    return jax.ShapeDtypeStruct(tuple(shape), dtype)


def _silu(x):
    return x * jax.nn.sigmoid(x)


def _dsilu(x):
    s = jax.nn.sigmoid(x)
    return s * (1.0 + x * (1.0 - s))


def _ep_plain(r):
    return (r,)


def _ep_add(r, res):
    return (r + res.astype(F32),)


def _ep_relu2(r):
    rp = jnp.maximum(r, 0.0)
    return r, rp * rp


def _ep_relu2_grad(r, a):
    return (r * (2.0 * jnp.maximum(a.astype(F32), 0.0)),)


def matmul(a, b, *, ta=False, tb=False, out_dtype=F32, add=None, name, tm=None, tn=None, tk=None,
           epilogue=None, extras=(), out_dtypes=None):
    if add is not None:
        epilogue, extras = _ep_add, (add,)
    if epilogue is None:
        epilogue = _ep_plain
    out_dtypes = tuple(out_dtypes) if out_dtypes is not None else (out_dtype,)
    n_ex, n_out = len(extras), len(out_dtypes)
    if ta:
        k_dim, m_dim = a.shape
    else:
        m_dim, k_dim = a.shape
    if tb:
        n_dim, k2 = b.shape
    else:
        k2, n_dim = b.shape
    assert k_dim == k2, (a.shape, b.shape, ta, tb)
    if ta:
        tm = tm or _pick(m_dim, 1024, LANE)
        tk = tk or _pick(k_dim, 1088, 16)
        tn = tn or _pick(n_dim, 1024, LANE)
    else:
        tm = tm or _pick(m_dim, 1088, 16)
        tk = tk or _pick(k_dim, 1024 if a.dtype == F32 else 2048, LANE)
        tn = tn or _pick(n_dim, 512, LANE)
    nm, nn, nk = m_dim // tm, n_dim // tn, k_dim // tk
    dn = (((0 if ta else 1,), (1 if tb else 0,)), ((), ()))

    def body(*refs):
        a_ref, b_ref = refs[:2]
        ex_refs = refs[2:2 + n_ex]
        o_refs = refs[2 + n_ex:2 + n_ex + n_out]
        scr = refs[2 + n_ex + n_out:]
        p = lax.dot_general(a_ref[...].astype(BF16), b_ref[...].astype(BF16), dn, preferred_element_type=F32)

        def finish(r):
            outs = epilogue(r, *[e[...] for e in ex_refs])
            for o_ref, val, dt in zip(o_refs, outs, out_dtypes):
                o_ref[...] = val.astype(dt)

        if nk == 1:
            finish(p)
        else:
            acc = scr[0]
            k = pl.program_id(2)

            @pl.when(k == 0)
            def _():
                acc[...] = p

            @pl.when(k > 0)
            def _():
                acc[...] += p

            @pl.when(k == nk - 1)
            def _():
                finish(acc[...])

    a_spec = pl.BlockSpec((tk, tm), lambda i, j, k: (k, i)) if ta else pl.BlockSpec((tm, tk), lambda i, j, k: (i, k))
    b_spec = pl.BlockSpec((tn, tk), lambda i, j, k: (j, k)) if tb else pl.BlockSpec((tk, tn), lambda i, j, k: (k, j))
    o_spec = pl.BlockSpec((tm, tn), lambda i, j, k: (i, j))
    outs = pl.pallas_call(
        body, name=name, grid=(nm, nn, nk), in_specs=[a_spec, b_spec] + [o_spec] * n_ex, out_specs=[o_spec] * n_out,
        out_shape=[_sds((m_dim, n_dim), dt) for dt in out_dtypes],
        scratch_shapes=[pltpu.VMEM((tm, tn), F32)] if nk > 1 else [],
        compiler_params=_cp(dimension_semantics=("parallel", "parallel", "arbitrary")),
    )(a, b, *extras)
    return outs[0] if n_out == 1 else tuple(outs)


def rmsnorm_fwd(x, w, *, cw=None, ci=0, name):
    t = x.shape[0]
    cw = cw or x.shape[1]
    tr = _pick(t, 544, 16)

    def body(x_ref, w_ref, o_ref):
        xv = x_ref[...].astype(F32)
        r = lax.rsqrt(jnp.mean(xv * xv, axis=-1, keepdims=True) + EPS)
        o_ref[...] = (xv * r * w_ref[...]).astype(BF16)

    return pl.pallas_call(
        body, name=name, grid=(t // tr,),
        in_specs=[pl.BlockSpec((tr, cw), lambda i: (i, ci)), pl.BlockSpec((1, cw), lambda i: (0, 0))],
        out_specs=pl.BlockSpec((tr, cw), lambda i: (i, 0)),
        out_shape=_sds((t, cw), BF16), compiler_params=_cp(),
    )(x, w.reshape(1, cw))


def rmsnorm_bwd(dy, x, w, *, cw=None, ci=0, res=None, out_dtype=F32, name):
    t = x.shape[0]
    cw = cw or x.shape[1]
    tr = _pick(t, 544, 16)
    has_res = res is not None

    def body(*refs):
        if has_res:
            dy_ref, x_ref, w_ref, res_ref, dx_ref, dw_ref = refs
        else:
            dy_ref, x_ref, w_ref, dx_ref, dw_ref = refs
        xv = x_ref[...].astype(F32)
        dyv = dy_ref[...].astype(F32)
        r = lax.rsqrt(jnp.mean(xv * xv, axis=-1, keepdims=True) + EPS)
        xh = xv * r
        g = dyv * w_ref[...]
        dx = r * (g - xh * jnp.mean(g * xh, axis=-1, keepdims=True))
        if has_res:
            dx = dx + res_ref[...]
        dx_ref[...] = dx.astype(out_dtype)

        @pl.when(pl.program_id(0) == 0)
        def _():
            dw_ref[...] = jnp.zeros_like(dw_ref)

        dw_ref[...] += jnp.sum(dyv * xh, axis=0, keepdims=True)

    row = pl.BlockSpec((tr, cw), lambda i: (i, 0))
    in_specs = [row, pl.BlockSpec((tr, cw), lambda i: (i, ci)), pl.BlockSpec((1, cw), lambda i: (0, 0))]
    args = [dy, x, w.reshape(1, cw)]
    if has_res:
        in_specs.append(row)
        args.append(res)
    dx, dw = pl.pallas_call(
        body, name=name, grid=(t // tr,), in_specs=in_specs,
        out_specs=[row, pl.BlockSpec((1, cw), lambda i: (0, 0))],
        out_shape=[_sds((t, cw), out_dtype), _sds((1, cw), F32)], compiler_params=_cp(),
    )(*args)
    return dx, dw[0]


def _shift_down(x, s, rows):
    if s == 0:
        return x
    return jnp.where(rows >= s, pltpu.roll(x, s, 0), 0.0)


def _shift_up(x, s, rows):
    if s == 0:
        return x
    n = x.shape[0]
    return jnp.where(rows < n - s, pltpu.roll(x, n - s, 0), 0.0)


def _conv_pre(x, w_ref, b_ref, rows, kk):
    pre = b_ref[...] + jnp.zeros_like(x)
    for k in range(kk):
        pre = pre + w_ref[k:k + 1, :] * _shift_down(x, kk - 1 - k, rows)
    return pre


def conv_fwd(cfg, xbc, w, b, *, name):
    lp, cd, kk = cfg.lp, cfg.conv_dim, cfg.convk
    cb = _pick(cd, 512, LANE)

    def body(x_ref, w_ref, b_ref, o_ref):
        x = x_ref[...]
        rows = lax.broadcasted_iota(jnp.int32, x.shape, 0)
        o_ref[...] = _silu(_conv_pre(x, w_ref, b_ref, rows, kk))

    blk = pl.BlockSpec((lp, cb), lambda j, bb: (bb, j))
    return pl.pallas_call(
        body, name=name, grid=(cd // cb, cfg.bsz),
        in_specs=[blk, pl.BlockSpec((kk, cb), lambda j, bb: (0, j)), pl.BlockSpec((1, cb), lambda j, bb: (0, j))],
        out_specs=blk, out_shape=_sds((cfg.t, cd), F32), compiler_params=_cp(),
    )(xbc, w, b.reshape(1, cd))


def conv_bwd(cfg, xbc, w, b, dxc, *, name):
    lp, cd, kk = cfg.lp, cfg.conv_dim, cfg.convk
    cb = _pick(cd, 512, LANE)

    def body(x_ref, w_ref, b_ref, d_ref, dx_ref, dw_ref, db_ref):
        x = x_ref[...]
        rows = lax.broadcasted_iota(jnp.int32, x.shape, 0)
        pre = _conv_pre(x, w_ref, b_ref, rows, kk)
        dpre = d_ref[...] * _dsilu(pre)
        dx = jnp.zeros_like(x)
        dws = []
        for k in range(kk):
            s = kk - 1 - k
            dx = dx + w_ref[k:k + 1, :] * _shift_up(dpre, s, rows)
            dws.append(jnp.sum(dpre * _shift_down(x, s, rows), axis=0, keepdims=True))
        dx_ref[...] = dx.astype(BF16)

        @pl.when(pl.program_id(1) == 0)
        def _():
            dw_ref[...] = jnp.zeros_like(dw_ref)
            db_ref[...] = jnp.zeros_like(db_ref)

        for k in range(kk):
            dw_ref[k:k + 1, :] += dws[k]
        db_ref[...] += jnp.sum(dpre, axis=0, keepdims=True)

    blk = pl.BlockSpec((lp, cb), lambda j, bb: (bb, j))
    wsp = pl.BlockSpec((kk, cb), lambda j, bb: (0, j))
    bsp = pl.BlockSpec((1, cb), lambda j, bb: (0, j))
    dx, dw, db = pl.pallas_call(
        body, name=name, grid=(cd // cb, cfg.bsz),
        in_specs=[blk, wsp, bsp, blk], out_specs=[blk, wsp, bsp],
        out_shape=[_sds((cfg.t, cd), BF16), _sds((kk, cd), F32), _sds((1, cd), F32)], compiler_params=_cp(),
    )(xbc, w, b.reshape(1, cd), dxc)
    return dx, dw, db[0]


def _softplus(x):
    return jnp.maximum(x, 0.0) + jnp.log(1.0 + jnp.exp(-jnp.abs(x)))


def _ssd_consts(cfg):
    q = cfg.chunk
    i0 = np.arange(q)[:, None]
    i1 = np.arange(q)[None, :]
    ltri = (i1 <= i0).astype(np.float32)
    rexp = np.zeros((LANE, cfg.inner), np.float32)
    for h in range(cfg.heads):
        rexp[h, h * cfg.hd:(h + 1) * cfg.hd] = 1.0
    return jnp.asarray(ltri), jnp.asarray(rexp)


def _sel_dot(x, m, *, passes=2, left=False, trans=False):
    mb = m.astype(BF16)
    acc, rem = None, x
    for _ in range(passes):
        piece = rem.astype(BF16)
        if not left:
            part = _nn(piece, mb)
        elif trans:
            part = _tn(mb, piece)
        else:
            part = _nn(mb, piece)
        acc = part if acc is None else acc + part
        rem = rem - piece.astype(F32)
    return acc


def _ssd_chunk_common(cfg, raw, bias, avec, c_idx, ltri, rexp):
    q = cfg.chunk
    rows = lax.broadcasted_iota(jnp.int32, (q, LANE), 0)
    live = jnp.logical_or(c_idx > 0, rows >= cfg.pad)
    pre = raw + bias
    dt = jnp.where(live, _softplus(pre), 0.0)
    adt = dt * avec
    cs = _sel_dot(adt, ltri, passes=3, left=True)
    cs_t = cs.T
    cs_last = cs[q - 1:q, :]
    e_in = jnp.exp(cs)
    w0 = jnp.exp(cs_last - cs)
    decay = jnp.exp(cs_last)
    return dict(live=live, pre=pre, dt=dt, adt=adt, cs=cs, cs_t=cs_t, e_in=e_in, w0=w0, decay=decay,
                DT=_sel_dot(dt, rexp), E=_sel_dot(e_in, rexp), W0=_sel_dot(w0, rexp),
                DEC=_sel_dot(jnp.broadcast_to(decay, (8, LANE)), rexp)[0:1, :])


def _tri_masks(q):
    r = lax.broadcasted_iota(jnp.int32, (q, q), 0)
    c = lax.broadcasted_iota(jnp.int32, (q, q), 1)
    return c <= r, r <= c


def _head_l(cq, h, tri, tri_t):
    col = cq["cs"][:, h:h + 1]
    row = cq["cs_t"][h:h + 1, :]
    lmat = jnp.where(tri, jnp.exp(jnp.minimum(col - row, 0.0)), 0.0)
    lmat_t = jnp.where(tri_t, jnp.exp(jnp.minimum(row - col, 0.0)), 0.0)
    return lmat, lmat_t


def _nt(a, b):
    return lax.dot_general(a, b, (((1,), (1,)), ((), ())), preferred_element_type=F32)


def _tn(a, b):
    return lax.dot_general(a, b, (((0,), (0,)), ((), ())), preferred_element_type=F32)


def _nn(a, b):
    return jnp.dot(a, b, preferred_element_type=F32)


def ssd_fwd(cfg, xc, small, dt_bias, avec, dexp, *, name):
    q, inner, st, gw, g_n = cfg.chunk, cfg.inner, cfg.state, cfg.gw, cfg.groups
    nc = cfg.nchunks
    ltri, rexp = _ssd_consts(cfg)
    hpt = LANE // cfg.hd
    tiles_per_group = gw // LANE

    def body(x_ref, b_ref, c_ref, dt_ref, bias_ref, a_ref, d_ref, ltri_ref, rexp_ref, y_ref, sin_ref, s_scr):
        c_idx = pl.program_id(1)

        @pl.when(c_idx == 0)
        def _():
            s_scr[...] = jnp.zeros_like(s_scr)

        ltri_v = ltri_ref[...]
        tri, tri_t = _tri_masks(q)
        cq = _ssd_chunk_common(cfg, dt_ref[...], bias_ref[...], a_ref[...], c_idx, ltri_v, rexp_ref[...])
        xs = x_ref[...]
        xdt = (xs * cq["DT"]).astype(BF16)
        xw = (xs * cq["DT"] * cq["W0"]).astype(BF16)
        s_in = s_scr[...]
        sin_ref[0] = s_in
        lane = lax.broadcasted_iota(jnp.int32, (q, LANE), 1)
        for g in range(g_n):
            bg = b_ref[:, g * st:(g + 1) * st].astype(BF16)
            cg = c_ref[:, g * st:(g + 1) * st].astype(BF16)
            gmat = _nt(cg, bg)
            gs = slice(g * gw, (g + 1) * gw)
            y0 = _nn(cg, s_in[:, gs].astype(BF16))
            for tt in range(tiles_per_group):
                tile = g * tiles_per_group + tt
                ts = slice(tile * LANE, (tile + 1) * LANE)
                xt = xdt[:, ts]
                yd = None
                for hh in range(hpt):
                    h = tile * hpt + hh
                    lmat, _ = _head_l(cq, h, tri, tri_t)
                    part = _nn((gmat * lmat).astype(BF16), xt)
                    if yd is None:
                        yd = part
                    else:
                        yd = jnp.where(lane < (hh * cfg.hd), yd, part)
                y_ref[:, ts] = yd + y0[:, tt * LANE:(tt + 1) * LANE] * cq["E"][:, ts] + xs[:, ts] * d_ref[:, ts]
            s_scr[:, gs] = s_in[:, gs] * cq["DEC"][:, gs] + _tn(bg, xw[:, gs])

    def rowblk(width, col):
        return pl.BlockSpec((q, width), lambda b, c: (b * nc + c, col))

    def const(shape):
        return pl.BlockSpec(shape, lambda b, c: (0, 0))

    y, sin = pl.pallas_call(
        body, name=name, grid=(cfg.bsz, nc),
        in_specs=[rowblk(inner, 0),
                  pl.BlockSpec((q, g_n * st), lambda b, c: (b * nc + c, inner // (g_n * st))),
                  pl.BlockSpec((q, g_n * st), lambda b, c: (b * nc + c, inner // (g_n * st) + 1)),
                  rowblk(LANE, cfg.dtt), const((1, LANE)), const((1, LANE)), const((1, inner)),
                  const((q, q)), const((LANE, inner))],
        out_specs=[rowblk(inner, 0), pl.BlockSpec((1, st, inner), lambda b, c: (b * nc + c, 0, 0))],
        out_shape=[_sds((cfg.t, inner), F32), _sds((cfg.bsz * nc, st, inner), F32)],
        scratch_shapes=[pltpu.VMEM((st, inner), F32)], compiler_params=_cp(),
    )(xc, xc, xc, small, dt_bias, avec, dexp, ltri, rexp)
    return y, sin


def ssd_bwd(cfg, xc, small, dt_bias, avec, dexp, sin, dy, *, name):
    q, inner, st, gw, g_n = cfg.chunk, cfg.inner, cfg.state, cfg.gw, cfg.groups
    nc = cfg.nchunks
    ltri, rexp = _ssd_consts(cfg)
    rexp_t = rexp.T
    hpt = LANE // cfg.hd
    tiles_per_group = gw // LANE
    bcw = g_n * st

    def body(x_ref, b_ref, c_ref, dt_ref, bias_ref, a_ref, d_ref, ltri_ref, rexp_ref, rexpt_ref, sin_ref, dy_ref,
             dx_ref, ddt_ref, dd_ref, da_ref, dbias_ref, ds_scr):
        step = pl.program_id(1)
        c_idx = nc - 1 - step

        @pl.when(step == 0)
        def _():
            ds_scr[...] = jnp.zeros_like(ds_scr)

        @pl.when(jnp.logical_and(step == 0, pl.program_id(0) == 0))
        def _():
            dd_ref[...] = jnp.zeros_like(dd_ref)
            da_ref[...] = jnp.zeros_like(da_ref)
            dbias_ref[...] = jnp.zeros_like(dbias_ref)

        ltri_v = ltri_ref[...]
        tri, tri_t = _tri_masks(q)
        red = _sel_dot
        rexpt = rexpt_ref[...]
        cq = _ssd_chunk_common(cfg, dt_ref[...], bias_ref[...], a_ref[...], c_idx, ltri_v, rexp_ref[...])
        xs = x_ref[...]
        dyv = dy_ref[...]
        s_in = sin_ref[0]
        d_s = ds_scr[...]
        xdt_f = xs * cq["DT"]
        xdt = xdt_f.astype(BF16)
        xw_f = xdt_f * cq["W0"]
        xw = xw_f.astype(BF16)
        lane = lax.broadcasted_iota(jnp.int32, (q, LANE), 1)
        sub = lax.broadcasted_iota(jnp.int32, (LANE, q), 0)

        dd_ref[...] += jnp.sum(dyv * xs, axis=0, keepdims=True)
        dy0 = dyv * cq["E"]
        dcs = jnp.zeros((q, LANE), F32)
        dcs_t = jnp.zeros((LANE, q), F32)
        for g in range(g_n):
            bg_f = b_ref[:, g * st:(g + 1) * st]
            cg_f = c_ref[:, g * st:(g + 1) * st]
            bg = bg_f.astype(BF16)
            cg = cg_f.astype(BF16)
            gs = slice(g * gw, (g + 1) * gw)
            gmat = _nt(cg, bg)
            gmat_t = _nt(bg, cg)
            sing = s_in[:, gs].astype(BF16)
            dsg = d_s[:, gs].astype(BF16)
            y0 = _nn(cg, sing)
            dxw = _nn(bg, dsg)
            d_bg = _nt(xw[:, gs], dsg)
            d_cg = _nt(dy0[:, gs].astype(BF16), sing)
            ds_in_g = _tn(cg, dy0[:, gs].astype(BF16))
            dg = jnp.zeros((q, q), F32)
            dxdt_g = []
            for tt in range(tiles_per_group):
                tile = g * tiles_per_group + tt
                ts = slice(tile * LANE, (tile + 1) * LANE)
                xt = xdt[:, ts]
                dyt = dyv[:, ts]
                dxdt_t = None
                for hh in range(hpt):
                    h = tile * hpt + hh
                    lmat, lmat_t = _head_l(cq, h, tri, tri_t)
                    inhead = jnp.logical_and(lane >= hh * cfg.hd, lane < (hh + 1) * cfg.hd)
                    dyh = jnp.where(inhead, dyt, 0.0).astype(BF16)
                    dm = _nt(dyh, xt)
                    dg = dg + dm * lmat
                    qm = dm * gmat * lmat
                    rs = jnp.sum(qm, axis=1, keepdims=True)
                    csum = jnp.sum(qm, axis=0, keepdims=True)
                    dcs = dcs + jnp.where(lane == h, rs, 0.0)
                    dcs_t = dcs_t + jnp.where(sub == h, csum, 0.0)
                    part = _nn((gmat_t * lmat_t).astype(BF16), dyh)
                    dxdt_t = part if dxdt_t is None else dxdt_t + part
                dxdt_g.append(dxdt_t)
            dxdt_diag = jnp.concatenate(dxdt_g, axis=1) if len(dxdt_g) > 1 else dxdt_g[0]
            dgb = dg.astype(BF16)
            d_cg = d_cg + _nn(dgb, bg)
            d_bg = d_bg + _tn(dgb, cg)
            dx_ref[:, inner + g * st:inner + (g + 1) * st] = d_bg
            dx_ref[:, inner + bcw + g * st:inner + bcw + (g + 1) * st] = d_cg
            dxdt = dxdt_diag + dxw * cq["W0"][:, gs]
            dx_ref[:, gs] = dyv[:, gs] * d_ref[:, gs] + dxdt * cq["DT"][:, gs]
            rt = rexpt[gs, :]
            dcs = dcs + red(dyv[:, gs] * y0 * cq["E"][:, gs], rt)
            r_w = red(dxw * xw_f[:, gs], rt)
            dcs = dcs - r_w
            dcs_last_g = jnp.sum(r_w, axis=0, keepdims=True)
            ddec = red(jnp.broadcast_to(jnp.sum(d_s[:, gs] * s_in[:, gs], axis=0, keepdims=True), (8, gw)), rt)[0:1, :]
            dcs_last_g = dcs_last_g + ddec * cq["decay"]
            dcs = dcs + jnp.where(lax.broadcasted_iota(jnp.int32, (q, LANE), 0) == q - 1, dcs_last_g, 0.0)
            ddt_part = red(dxdt * xs[:, gs], rt)
            if g == 0:
                ddt = ddt_part
            else:
                ddt = ddt + ddt_part
            ds_scr[:, gs] = d_s[:, gs] * cq["DEC"][:, gs] + ds_in_g
        dcs = dcs - dcs_t.T
        dadt = _sel_dot(dcs, ltri_v, left=True, trans=True)
        ddt = ddt + dadt * a_ref[...]
        da_ref[...] += jnp.sum(dadt * cq["dt"], axis=0, keepdims=True)
        draw = jnp.where(cq["live"], ddt * jax.nn.sigmoid(cq["pre"]), 0.0)
        ddt_ref[...] = draw
        dbias_ref[...] += jnp.sum(draw, axis=0, keepdims=True)

    def rowblk(width, col):
        return pl.BlockSpec((q, width), lambda b, s: (b * nc + nc - 1 - s, col))

    def const(shape):
        return pl.BlockSpec(shape, lambda b, s: (0, 0))

    bcol = inner // bcw
    outs = pl.pallas_call(
        body, name=name, grid=(cfg.bsz, nc),
        in_specs=[rowblk(inner, 0), rowblk(bcw, bcol), rowblk(bcw, bcol + 1), rowblk(LANE, cfg.dtt),
                  const((1, LANE)), const((1, LANE)), const((1, inner)), const((q, q)), const((LANE, inner)),
                  const((inner, LANE)),
                  pl.BlockSpec((1, st, inner), lambda b, s: (b * nc + nc - 1 - s, 0, 0)), rowblk(inner, 0)],
        out_specs=[rowblk(cfg.conv_dim, 0), rowblk(LANE, 0),
                   const((1, inner)), const((1, LANE)), const((1, LANE))],
        out_shape=[_sds((cfg.t, cfg.conv_dim), F32),
                   _sds((cfg.t, LANE), F32), _sds((1, inner), F32), _sds((1, LANE), F32), _sds((1, LANE), F32)],
        scratch_shapes=[pltpu.VMEM((st, inner), F32)], compiler_params=_cp(),
    )(xc, xc, xc, small, dt_bias, avec, dexp, ltri, rexp, rexp_t, sin, dy)
    return outs


def tail_fwd(cfg, y, z, w, *, name):
    t, inner, gw = cfg.t, cfg.inner, cfg.gw
    tr = _pick(t, 272, 16)

    def body(y_ref, z_ref, w_ref, o_ref):
        for g in range(cfg.groups):
            gs = slice(g * gw, (g + 1) * gw)
            yg = y_ref[:, gs] * _silu(z_ref[:, gs])
            r = lax.rsqrt(jnp.mean(yg * yg, axis=-1, keepdims=True) + EPS)
            o_ref[:, gs] = (yg * r * w_ref[:, gs]).astype(BF16)

    row = pl.BlockSpec((tr, inner), lambda i: (i, 0))
    return pl.pallas_call(
        body, name=name, grid=(t // tr,), in_specs=[row, row, pl.BlockSpec((1, inner), lambda i: (0, 0))],
        out_specs=row, out_shape=_sds((t, inner), BF16), compiler_params=_cp(),
    )(y, z, w.reshape(1, inner))


def tail_bwd(cfg, do, y, z, w, *, name):
    t, inner, gw = cfg.t, cfg.inner, cfg.gw
    tr = _pick(t, 272, 16)

    def body(do_ref, y_ref, z_ref, w_ref, dy_ref, dz_ref, dw_ref):
        @pl.when(pl.program_id(0) == 0)
        def _():
            dw_ref[...] = jnp.zeros_like(dw_ref)

        for g in range(cfg.groups):
            gs = slice(g * gw, (g + 1) * gw)
            yv = y_ref[:, gs]
            zv = z_ref[:, gs]
            dov = do_ref[:, gs]
            sz = _silu(zv)
            yg = yv * sz
            r = lax.rsqrt(jnp.mean(yg * yg, axis=-1, keepdims=True) + EPS)
            xh = yg * r
            gg = dov * w_ref[:, gs]
            dyg = r * (gg - xh * jnp.mean(gg * xh, axis=-1, keepdims=True))
            dw_ref[:, gs] += jnp.sum(dov * xh, axis=0, keepdims=True)
            dy_ref[:, gs] = dyg * sz
            dz_ref[:, gs] = (dyg * yv * _dsilu(zv)).astype(BF16)

    row = pl.BlockSpec((tr, inner), lambda i: (i, 0))
    vec = pl.BlockSpec((1, inner), lambda i: (0, 0))
    dy, dz, dw = pl.pallas_call(
        body, name=name, grid=(t // tr,), in_specs=[row, row, row, vec], out_specs=[row, row, vec],
        out_shape=[_sds((t, inner), F32), _sds((t, inner), BF16), _sds((1, inner), F32)], compiler_params=_cp(),
    )(do, y, z, w.reshape(1, inner))
    return dy, dz, dw[0]


def rope_tables(cfg):
    half = cfg.rope // 2
    pos = np.maximum(np.arange(cfg.lp) - cfg.pad, 0).astype(np.float32)
    inv = ROPE_THETA ** (-jnp.arange(0, cfg.rope, 2, dtype=F32) / cfg.rope)
    ang = jnp.asarray(pos)[:, None] * inv[None, :]
    cos, sin = jnp.cos(ang), jnp.sin(ang)
    zero = jnp.zeros((cfg.lp, LANE - 2 * half), F32)
    zh = jnp.zeros((cfg.lp, half), F32)
    ctab = jnp.concatenate([cos, cos, zero], axis=1)
    s1 = jnp.concatenate([-sin, zh, zero], axis=1)
    s2 = jnp.concatenate([zh, sin, zero], axis=1)
    return ctab, s1, s2


def _rope(x, c, s1, s2, half):
    return x * c + pltpu.roll(x, LANE - half, 1) * s1 + pltpu.roll(x, half, 1) * s2


def _rope_t(dy, c, s1, s2, half):
    return dy * c + pltpu.roll(dy * s1, half, 1) + pltpu.roll(dy * s2, LANE - half, 1)


def rope_fwd(cfg, qf, small, tabs, *, name):
    t, qw, lp = cfg.t, cfg.qw, cfg.lp
    tr = _pick(lp, 544, 16)
    nrb = lp // tr
    half = cfg.rope // 2

    def body(q_ref, k_ref, c_ref, s1_ref, s2_ref, qo_ref, ko_ref):
        c, s1, s2 = c_ref[...], s1_ref[...], s2_ref[...]
        for h in range(cfg.mh):
            a = h * 2 * LANE
            qo_ref[:, a:a + LANE] = q_ref[:, a:a + LANE].astype(BF16)
            qo_ref[:, a + LANE:a + 2 * LANE] = _rope(q_ref[:, a + LANE:a + 2 * LANE], c, s1, s2, half).astype(BF16)
        ko_ref[...] = _rope(k_ref[...], c, s1, s2, half).astype(BF16)

    tab = pl.BlockSpec((tr, LANE), lambda i: (i % nrb, 0))
    return pl.pallas_call(
        body, name=name, grid=(t // tr,),
        in_specs=[pl.BlockSpec((tr, qw), lambda i: (i, 0)), pl.BlockSpec((tr, LANE), lambda i: (i, cfg.kt)), tab, tab, tab],
        out_specs=[pl.BlockSpec((tr, qw), lambda i: (i, 0)), pl.BlockSpec((tr, LANE), lambda i: (i, 0))],
        out_shape=[_sds((t, qw), BF16), _sds((t, LANE), BF16)], compiler_params=_cp(),
    )(qf, small, *tabs)


def rope_bwd(cfg, dq, dkpe, tabs, *, name):
    t, qw, lp = cfg.t, cfg.qw, cfg.lp
    tr = _pick(lp, 544, 16)
    nrb = lp // tr
    half = cfg.rope // 2

    def body(dq_ref, dk_ref, c_ref, s1_ref, s2_ref, qo_ref, ko_ref):
        c, s1, s2 = c_ref[...], s1_ref[...], s2_ref[...]
        for h in range(cfg.mh):
            a = h * 2 * LANE
            qo_ref[:, a:a + LANE] = dq_ref[:, a:a + LANE].astype(BF16)
            qo_ref[:, a + LANE:a + 2 * LANE] = _rope_t(dq_ref[:, a + LANE:a + 2 * LANE], c, s1, s2, half).astype(BF16)
        dk = dk_ref[0]
        for h in range(1, cfg.mh):
            dk = dk + dk_ref[h]
        ko_ref[...] = _rope_t(dk, c, s1, s2, half)

    tab = pl.BlockSpec((tr, LANE), lambda i: (i % nrb, 0))
    return pl.pallas_call(
        body, name=name, grid=(t // tr,),
        in_specs=[pl.BlockSpec((tr, qw), lambda i: (i, 0)), pl.BlockSpec((cfg.mh, tr, LANE), lambda i: (0, i, 0)),
                  tab, tab, tab],
        out_specs=[pl.BlockSpec((tr, qw), lambda i: (i, 0)), pl.BlockSpec((tr, LANE), lambda i: (i, 0))],
        out_shape=[_sds((t, qw), BF16), _sds((t, LANE), F32)], compiler_params=_cp(),
    )(dq, dkpe, *tabs)


def _q_blocks(cfg):
    bounds = [0, cfg.chunk] + list(range(cfg.chunk + 256, cfg.lp + 1, 256))
    assert bounds[-1] == cfg.lp, "SEQ must be a multiple of 256"
    return list(zip(bounds[:-1], bounds[1:]))


def _attn_mask(cfg, qs, qe):
    rows = qs + lax.broadcasted_iota(jnp.int32, (qe - qs, qe), 0)
    cols = lax.broadcasted_iota(jnp.int32, (qe - qs, qe), 1)
    return jnp.logical_and(cols <= rows, jnp.logical_or(cols >= cfg.pad, rows < cfg.pad))


def _max_q_block(cfg):
    return max(qe - qs for qs, qe in _q_blocks(cfg))


def _masked_scores(cfg, q, k2, qs, qe, s_scr, scale):
    bq, n = qe - qs, qe
    s_scr[0:bq, 0:n] = _nt(q, k2) * scale
    if qs == 0:
        s_scr[0:bq, 0:n] = jnp.where(_attn_mask(cfg, 0, qe), s_scr[0:bq, 0:n], MASK_VALUE)
    else:
        assert qs >= cfg.chunk and cfg.pad < LANE
        cols = lax.broadcasted_iota(jnp.int32, (bq, LANE), 1)
        s_scr[0:bq, 0:LANE] = jnp.where(cols >= cfg.pad, s_scr[0:bq, 0:LANE], MASK_VALUE)
        r = lax.broadcasted_iota(jnp.int32, (bq, bq), 0)
        c = lax.broadcasted_iota(jnp.int32, (bq, bq), 1)
        s_scr[0:bq, qs:qe] = jnp.where(c <= r, s_scr[0:bq, qs:qe], MASK_VALUE)
    return s_scr[0:bq, 0:n]


def attn_fwd(cfg, qr, kv, kpe, *, name):
    lp, t, mh = cfg.lp, cfg.t, cfg.mh
    scale = (cfg.nope + cfg.rope) ** -0.5
    blocks = _q_blocks(cfg)

    def body(q_ref, kv_ref, kp_ref, o_ref, l_ref, s_scr):
        for qs, qe in blocks:
            n = qe
            q = q_ref[qs:qe, :]
            k2 = jnp.concatenate([kv_ref[0:n, 0:LANE], kp_ref[0:n, :]], axis=1)
            s = _masked_scores(cfg, q, k2, qs, qe, s_scr, scale)
            m = jnp.max(s, axis=-1, keepdims=True)
            p = jnp.exp(s - m)
            l = jnp.sum(p, axis=-1, keepdims=True)
            pn = (p * (1.0 / l)).astype(BF16)
            o_ref[qs:qe, :] = _nn(pn, kv_ref[0:n, LANE:2 * LANE])
            l_ref[qs:qe, :] = jnp.broadcast_to(m + jnp.log(l), (qe - qs, LANE))

    hb = pl.BlockSpec((lp, 2 * LANE), lambda b, h: (b, h))
    ob = pl.BlockSpec((lp, LANE), lambda b, h: (b, h))
    return pl.pallas_call(
        body, name=name, grid=(cfg.bsz, mh),
        in_specs=[hb, hb, pl.BlockSpec((lp, LANE), lambda b, h: (b, 0))], out_specs=[ob, ob],
        out_shape=[_sds((t, mh * LANE), F32), _sds((t, mh * LANE), F32)],
        scratch_shapes=[pltpu.VMEM((_max_q_block(cfg), lp), F32)], compiler_params=_cp(),
    )(qr, kv, kpe)


def attn_bwd(cfg, qr, kv, kpe, o, lse, do, *, name):
    lp, t, mh = cfg.lp, cfg.t, cfg.mh
    scale = (cfg.nope + cfg.rope) ** -0.5
    blocks = _q_blocks(cfg)

    def body(q_ref, kv_ref, kp_ref, o_ref, l_ref, do_ref, dq_ref, dkv_ref, dkp_ref, dk_acc, dv_acc, s_scr):
        dk_acc[...] = jnp.zeros_like(dk_acc)
        dv_acc[...] = jnp.zeros_like(dv_acc)
        for qs, qe in blocks:
            n = qe
            q = q_ref[qs:qe, :]
            k2 = jnp.concatenate([kv_ref[0:n, 0:LANE], kp_ref[0:n, :]], axis=1)
            dov = do_ref[qs:qe, :]
            delta = jnp.sum(dov * o_ref[qs:qe, :], axis=-1, keepdims=True)
            dob = dov.astype(BF16)
            s = _masked_scores(cfg, q, k2, qs, qe, s_scr, scale)
            p = jnp.exp(s - l_ref[qs:qe, 0:1])
            dp = _nt(dob, kv_ref[0:n, LANE:2 * LANE])
            ds = (p * (dp - delta) * scale).astype(BF16)
            dq_ref[qs:qe, :] = _nn(ds, k2)
            dv_acc[0:n, :] += _tn(p.astype(BF16), dob)
            dk_acc[0:n, :] += _tn(ds, q)
        dkv_ref[:, 0:LANE] = dk_acc[:, 0:LANE].astype(BF16)
        dkv_ref[:, LANE:2 * LANE] = dv_acc[...].astype(BF16)
        dkp_ref[0] = dk_acc[:, LANE:2 * LANE]

    hb = pl.BlockSpec((lp, 2 * LANE), lambda b, h: (b, h))
    ob = pl.BlockSpec((lp, LANE), lambda b, h: (b, h))
    return pl.pallas_call(
        body, name=name, grid=(cfg.bsz, mh),
        in_specs=[hb, hb, pl.BlockSpec((lp, LANE), lambda b, h: (b, 0)), ob, ob, ob],
        out_specs=[hb, hb, pl.BlockSpec((1, lp, LANE), lambda b, h: (h, b, 0))],
        out_shape=[_sds((t, cfg.qw), F32), _sds((t, mh * 2 * LANE), BF16), _sds((mh, t, LANE), F32)],
        scratch_shapes=[pltpu.VMEM((lp, 2 * LANE), F32), pltpu.VMEM((lp, LANE), F32),
                        pltpu.VMEM((_max_q_block(cfg), lp), F32)], compiler_params=_cp(),
    )(qr, kv, kpe, o, lse, do)


def _live_rows(cfg, tr, shape):
    rows = pl.program_id(1) * tr + lax.broadcasted_iota(jnp.int32, shape, 0)
    return rows >= cfg.pad


def gate_fwd(cfg, ya, yb, g, *, name):
    d, lp = cfg.d, cfg.lp
    tr = _pick(lp, 544, 16)
    nrb = lp // tr

    def body(ya_ref, yb_ref, ga_ref, gb_ref, o_ref):
        mix = jax.nn.sigmoid(ga_ref[...]) * ya_ref[...] + jax.nn.sigmoid(gb_ref[...]) * yb_ref[...]
        o_ref[...] = jnp.where(_live_rows(cfg, tr, mix.shape), mix, 0.0).astype(BF16)

    row = pl.BlockSpec((tr, d), lambda b, j: (b * nrb + j, 0))
    row1 = pl.BlockSpec((tr, d), lambda b, j: (b * nrb + j, 1))
    return pl.pallas_call(
        body, name=name, grid=(cfg.bsz, nrb), in_specs=[row, row, row, row1], out_specs=row,
        out_shape=_sds((cfg.t, d), BF16), compiler_params=_cp(),
    )(ya, yb, g, g)


def gate_bwd(cfg, dmix, ya, yb, g, *, name):
    d, lp = cfg.d, cfg.lp
    tr = _pick(lp, 544, 16)
    nrb = lp // tr

    def body(dm_ref, ya_ref, yb_ref, ga_ref, gb_ref, dya_ref, dyb_ref, dg_ref):
        dm = dm_ref[...]
        dm = jnp.where(_live_rows(cfg, tr, dm.shape), dm, 0.0)
        sa = jax.nn.sigmoid(ga_ref[...])
        sb = jax.nn.sigmoid(gb_ref[...])
        dya_ref[...] = (dm * sa).astype(BF16)
        dyb_ref[...] = (dm * sb).astype(BF16)
        dg_ref[:, 0:d] = (dm * ya_ref[...] * sa * (1.0 - sa)).astype(BF16)
        dg_ref[:, d:2 * d] = (dm * yb_ref[...] * sb * (1.0 - sb)).astype(BF16)

    row = pl.BlockSpec((tr, d), lambda b, j: (b * nrb + j, 0))
    row1 = pl.BlockSpec((tr, d), lambda b, j: (b * nrb + j, 1))
    row2 = pl.BlockSpec((tr, 2 * d), lambda b, j: (b * nrb + j, 0))
    return pl.pallas_call(
        body, name=name, grid=(cfg.bsz, nrb), in_specs=[row, row, row, row, row1], out_specs=[row, row, row2],
        out_shape=[_sds((cfg.t, d), BF16), _sds((cfg.t, d), BF16), _sds((cfg.t, 2 * d), BF16)], compiler_params=_cp(),
    )(dmix, ya, yb, g, g)


def loss_head(cfg, h, target, w, *, name):
    d, q, nc = cfg.d, cfg.chunk, cfg.nchunks
    tpb = cfg.seq // q

    def body(h_ref, t_ref, w_ref, loss_ref, dh_ref, dw_ref):
        j = pl.program_id(1)

        @pl.when(jnp.logical_and(j == 0, pl.program_id(0) == 0))
        def _():
            loss_ref[...] = jnp.zeros_like(loss_ref)
            dw_ref[...] = jnp.zeros_like(dw_ref)

        @pl.when(j == 0)
        def _():
            dh_ref[...] = jnp.zeros_like(dh_ref)

        @pl.when(j > 0)
        def _():
            xv = h_ref[...]
            r = lax.rsqrt(jnp.mean(xv * xv, axis=-1, keepdims=True) + EPS)
            xh = xv * r
            err = xh * w_ref[...] - t_ref[...]
            loss_ref[...] += 0.5 * jnp.sum(jnp.sum(err * err, axis=-1, keepdims=True) / d, axis=0, keepdims=True)
            dy = err * (1.0 / d)
            g = dy * w_ref[...]
            dh_ref[...] = r * (g - xh * jnp.mean(g * xh, axis=-1, keepdims=True))
            dw_ref[...] += jnp.sum(dy * xh, axis=0, keepdims=True)

    row = pl.BlockSpec((q, d), lambda b, j: (b * nc + j, 0))
    loss, dh, dw = pl.pallas_call(
        body, name=name, grid=(cfg.bsz, nc),
        in_specs=[row, pl.BlockSpec((q, d), lambda b, j: (b * tpb + jnp.maximum(j - 1, 0), 0)),
                  pl.BlockSpec((1, d), lambda b, j: (0, 0))],
        out_specs=[pl.BlockSpec((8, LANE), lambda b, j: (0, 0)), row, pl.BlockSpec((1, d), lambda b, j: (0, 0))],
        out_shape=[_sds((8, LANE), F32), _sds((cfg.t, d), F32), _sds((1, d), F32)], compiler_params=_cp(),
    )(h, target, w.reshape(1, d))
    return loss[0, 0], dh, dw[0]


def _rows_tile(r, c):
    return _pick(r, max(8, (1 << 18) // max(c, 1) // 8 * 8), 8)


def _adam_update(w, g, m, v):
    c1 = 1.0 - ADAM_B1 ** ADAM_STEP
    c2 = 1.0 - ADAM_B2 ** ADAM_STEP
    mn = ADAM_B1 * m + (1.0 - ADAM_B1) * g
    vn = ADAM_B2 * v + (1.0 - ADAM_B2) * (g * g)
    delta = -ADAM_LR * ((mn / c1) / (jnp.sqrt(vn / c2) + ADAM_EPS) + ADAM_WD * w)
    return delta, mn, vn


def adamw_layer(w, m, v, g, li, prev, dep, *, name):
    _, r, c = w.shape
    tr = _rows_tile(r, c)

    def body(*refs):
        w_ref, m_ref, v_ref, g_ref = refs[:4]
        go_ref, d_ref, mo_ref, vo_ref = refs[-4:]
        gv = g_ref[...]
        delta, mn, vn = _adam_update(w_ref[0], gv, m_ref[0], v_ref[0])
        go_ref[0] = gv
        d_ref[0] = delta
        mo_ref[0] = mn
        vo_ref[0] = vn

    if tr * c * 4 >= (1 << 16):
        steps = r // tr
        blk3 = pl.BlockSpec((1, tr, c), lambda i: (li, i, 0))
        blk2 = pl.BlockSpec((tr, c), lambda i: (i, 0))
    else:
        tc = _pick(c, max(LANE, (1 << 18) // r // LANE * LANE), LANE)
        steps = c // tc
        blk3 = pl.BlockSpec((1, r, tc), lambda i: (li, 0, i))
        blk2 = pl.BlockSpec((r, tc), lambda i: (0, i))
    anyspec = pl.BlockSpec(memory_space=pl.ANY)
    in_specs = [blk3, blk3, blk3, blk2, anyspec]
    args = [w, m, v, g, dep]
    aliases = {}
    if prev is not None:
        in_specs += [anyspec] * 4
        args += list(prev)
        aliases = {5 + i: i for i in range(4)}
    return pl.pallas_call(
        body, name=name, grid=(steps,), in_specs=in_specs, out_specs=[blk3] * 4,
        out_shape=[_sds(w.shape, F32)] * 4, input_output_aliases=aliases, compiler_params=_cp(),
    )(*args)


def pair_add(g4, other, half, *, name):
    n, _, r, c = g4.shape
    tr = _rows_tile(r, c)

    def body(h_ref, a_ref, b_ref, o_ref):
        o_ref[0] = (a_ref[0, 0] + b_ref[0]).astype(BF16)

    blk = pl.BlockSpec((1, tr, c), lambda j, i, h: (j, i, 0))
    grid_spec = pltpu.PrefetchScalarGridSpec(
        num_scalar_prefetch=1, grid=(n, r // tr),
        in_specs=[pl.BlockSpec((1, 1, tr, c), lambda j, i, h: (j, h[0], i, 0)), blk], out_specs=blk)
    return pl.pallas_call(body, name=name, grid_spec=grid_spec, out_shape=_sds((n, r, c), BF16),
                          compiler_params=_cp())(half, g4, other)


def chip_sum(recv, part, where, *, name):
    n, r, c = recv.shape
    tr = _rows_tile(r, c)

    def body(s_ref, *refs):
        own_ref, o_ref = refs[n], refs[n + 1]
        acc = None
        for j in range(n):
            term = jnp.where(s_ref[0] == j, own_ref[0], refs[j][0]).astype(F32)
            acc = term if acc is None else acc + term
        o_ref[0] = acc

    def slot(j):
        return pl.BlockSpec((1, tr, c), lambda i, s: (jnp.where(s[0] == j, (j + 1) % n, j), i, 0))

    grid_spec = pltpu.PrefetchScalarGridSpec(
        num_scalar_prefetch=1, grid=(r // tr,),
        in_specs=[slot(j) for j in range(n)] + [pl.BlockSpec((1, tr, c), lambda i, s: (s[0], i, 0))],
        out_specs=pl.BlockSpec((1, tr, c), lambda i, s: (s[1], i, 0)))
    return pl.pallas_call(body, name=name, grid_spec=grid_spec, out_shape=_sds((2, r, c), F32),
                          compiler_params=_cp())(where, *([recv] * n), part)


def _coords():
    return lax.axis_index("x"), lax.axis_index("y"), lax.axis_index("c")


def _other_chips(x, y):
    return [(1 - x, y), (x, 1 - y), (1 - x, 1 - y)]


def gather_chips(arrs, *, name):
    n = len(arrs)
    anyspec = pl.BlockSpec(memory_space=pl.ANY)

    def body(*refs):
        ins, outs = refs[:n], refs[n:2 * n]
        send_sems, recv_sems, local_sems = refs[2 * n:]
        x, y, c = _coords()
        me = 2 * x + y
        chips = _other_chips(x, y)
        copies = []
        for k in range(n):
            loc = pltpu.make_async_copy(ins[k], outs[k].at[me], local_sems.at[k])
            loc.start()
            copies.append(loc)
        sends = []
        for k in range(n):
            for j, (px, py) in enumerate(chips):
                cp = pltpu.make_async_remote_copy(
                    src_ref=ins[k], dst_ref=outs[k].at[me], send_sem=send_sems.at[k, j], recv_sem=recv_sems.at[k, j],
                    device_id=(px, py, c), device_id_type=MESH)
                cp.start()
                sends.append(cp)
        for k in range(n):
            for j, (px, py) in enumerate(chips):
                pltpu.make_async_remote_copy(
                    src_ref=ins[k], dst_ref=outs[k].at[2 * px + py], send_sem=send_sems.at[k, j],
                    recv_sem=recv_sems.at[k, j], device_id=(px, py, c), device_id_type=MESH).wait_recv()
        for cp in sends:
            cp.wait_send()
        for cp in copies:
            cp.wait()

    return pl.pallas_call(
        body, name=name, in_specs=[anyspec] * n, out_specs=[anyspec] * n,
        out_shape=[_sds((4,) + a.shape, a.dtype) for a in arrs],
        scratch_shapes=[pltpu.SemaphoreType.DMA((n, 3)), pltpu.SemaphoreType.DMA((n, 3)), pltpu.SemaphoreType.DMA((n,))],
        compiler_params=_cp(has_side_effects=True),
    )(*arrs)


def allreduce_small(vec, after, *, name):
    r, c = vec.shape

    def body(v_ref, after_ref, o_ref, buf, send_sems, recv_sems):
        x, y, cc = _coords()
        me = 4 * x + 2 * y + cc
        buf[me] = v_ref[...]
        sends = []
        flips = [(fx, fy, fc) for fx in (0, 1) for fy in (0, 1) for fc in (0, 1)][1:]
        for j, (fx, fy, fc) in enumerate(flips):
            peer = ((1 - x) if fx else x, (1 - y) if fy else y, (1 - cc) if fc else cc)
            cp = pltpu.make_async_remote_copy(
                src_ref=v_ref, dst_ref=buf.at[me], send_sem=send_sems.at[j], recv_sem=recv_sems.at[j],
                device_id=peer, device_id_type=MESH)
            cp.start()
            sends.append(cp)
        for j, (fx, fy, fc) in enumerate(flips):
            px, py, pc = ((1 - x) if fx else x, (1 - y) if fy else y, (1 - cc) if fc else cc)
            pltpu.make_async_remote_copy(
                src_ref=v_ref, dst_ref=buf.at[4 * px + 2 * py + pc], send_sem=send_sems.at[j],
                recv_sem=recv_sems.at[j], device_id=(px, py, pc), device_id_type=MESH).wait_recv()
        for cp in sends:
            cp.wait_send()
        acc = buf[0]
        for k in range(1, 8):
            acc = acc + buf[k]
        o_ref[...] = acc

    vm = pl.BlockSpec(memory_space=pltpu.VMEM)
    return pl.pallas_call(
        body, name=name, in_specs=[vm, pl.BlockSpec(memory_space=pl.ANY)], out_specs=vm, out_shape=_sds((r, c), F32),
        scratch_shapes=[pltpu.VMEM((8, r, c), F32), pltpu.SemaphoreType.DMA((7,)), pltpu.SemaphoreType.DMA((7,))],
        compiler_params=_cp(has_side_effects=True),
    )(vec, after)


def pair_exchange(arrs, *, name):
    n = len(arrs)
    anyspec = pl.BlockSpec(memory_space=pl.ANY)

    def body(*refs):
        ins, outs = refs[:n], refs[n:2 * n]
        send_sems, recv_sems = refs[2 * n:]
        x, y, c = _coords()
        sends = []
        for k in range(n):
            for j in range(4):
                cp = pltpu.make_async_remote_copy(
                    src_ref=ins[k].at[j, 1 - c], dst_ref=outs[k].at[j], send_sem=send_sems.at[k, j],
                    recv_sem=recv_sems.at[k, j], device_id=(x, y, 1 - c), device_id_type=MESH)
                cp.start()
                sends.append(cp)
        for cp in sends:
            cp.wait()

    return pl.pallas_call(
        body, name=name, in_specs=[anyspec] * n, out_specs=[anyspec] * n,
        out_shape=[_sds((a.shape[0],) + a.shape[2:], a.dtype) for a in arrs],
        scratch_shapes=[pltpu.SemaphoreType.DMA((n, 4)), pltpu.SemaphoreType.DMA((n, 4))],
        compiler_params=_cp(has_side_effects=True),
    )(*arrs)


def pair_share(lands, *, name):
    n = len(lands)
    anyspec = pl.BlockSpec(memory_space=pl.ANY)

    def body(*refs):
        ins, outs = refs[:n], refs[n:2 * n]
        send_sems, recv_sems = refs[2 * n:]
        x, y, c = _coords()
        sends = []
        for k in range(n):
            for j, (px, py) in enumerate(_other_chips(x, y)):
                cp = pltpu.make_async_remote_copy(
                    src_ref=ins[k].at[2 * px + py, c], dst_ref=outs[k].at[2 * px + py, c], send_sem=send_sems.at[k, j],
                    recv_sem=recv_sems.at[k, j], device_id=(x, y, 1 - c), device_id_type=MESH)
                cp.start()
                sends.append(cp)
        for k in range(n):
            for j, (px, py) in enumerate(_other_chips(x, y)):
                pltpu.make_async_remote_copy(
                    src_ref=ins[k].at[2 * px + py, c], dst_ref=outs[k].at[2 * px + py, 1 - c],
                    send_sem=send_sems.at[k, j], recv_sem=recv_sems.at[k, j], device_id=(x, y, 1 - c),
                    device_id_type=MESH).wait_recv()
        for cp in sends:
            cp.wait_send()

    return pl.pallas_call(
        body, name=name, in_specs=[anyspec] * n, out_specs=[anyspec] * n,
        out_shape=[_sds(a.shape, a.dtype) for a in lands], input_output_aliases={k: k for k in range(n)},
        scratch_shapes=[pltpu.SemaphoreType.DMA((n, 3)), pltpu.SemaphoreType.DMA((n, 3))],
        compiler_params=_cp(has_side_effects=True),
    )(*lands)


def pair_fill(arrs, *, name):
    n = len(arrs)
    anyspec = pl.BlockSpec(memory_space=pl.ANY)

    def body(*refs):
        ins, outs = refs[:n], refs[n:2 * n]
        send_sems, recv_sems = refs[2 * n:]
        x, y, c = _coords()
        sends = []
        for k in range(n):
            cp = pltpu.make_async_remote_copy(
                src_ref=ins[k].at[c], dst_ref=outs[k].at[c], send_sem=send_sems.at[k], recv_sem=recv_sems.at[k],
                device_id=(x, y, 1 - c), device_id_type=MESH)
            cp.start()
            sends.append(cp)
        for k in range(n):
            pltpu.make_async_remote_copy(
                src_ref=ins[k].at[c], dst_ref=outs[k].at[1 - c], send_sem=send_sems.at[k], recv_sem=recv_sems.at[k],
                device_id=(x, y, 1 - c), device_id_type=MESH).wait_recv()
        for cp in sends:
            cp.wait_send()

    return pl.pallas_call(
        body, name=name, in_specs=[anyspec] * n, out_specs=[anyspec] * n,
        out_shape=[_sds(a.shape, a.dtype) for a in arrs], input_output_aliases={k: k for k in range(n)},
        scratch_shapes=[pltpu.SemaphoreType.DMA((n,)), pltpu.SemaphoreType.DMA((n,))],
        compiler_params=_cp(has_side_effects=True),
    )(*arrs)


_HBM = pl.BlockSpec(memory_space=pltpu.HBM)
_SEM = pl.BlockSpec(memory_space=pltpu.SEMAPHORE)


def _ici_copies(kind, srcs, lands, send_sems, recv_sems):
    x, y, c = _coords()
    me = 2 * x + y
    sends, recvs = [], []
    for k in range(len(srcs)):
        for j, (px, py) in enumerate(_other_chips(x, y)):
            peer = 2 * px + py
            if kind == "gather":
                src, there, here = srcs[k].at[c], lands[k].at[me, c], lands[k].at[peer, c]
            else:
                src, there, here = srcs[k].at[peer], lands[k].at[me], lands[k].at[peer]
            sem = 3 * k + j
            mk = functools.partial(pltpu.make_async_remote_copy, src_ref=src, send_sem=send_sems.at[sem],
                                   recv_sem=recv_sems.at[sem], device_id=(px, py, c), device_id_type=MESH)
            sends.append(mk(dst_ref=there))
            recvs.append(mk(dst_ref=here))
    return sends, recvs


def ici_start(kind, srcs, lands, after, *, name):
    n = len(srcs)

    def body(*refs):
        src_refs, land_refs = refs[:n], refs[n:2 * n]
        send_sems, recv_sems = refs[2 * n + 1], refs[2 * n + 2]
        token = refs[-1]
        sends, _ = _ici_copies(kind, src_refs, land_refs, send_sems, recv_sems)
        for cp in sends:
            cp.start()
        token[...] = jnp.zeros_like(token)

    both = list(srcs) + list(lands)
    out = pl.pallas_call(
        body, name=name,
        in_specs=[_HBM] * (2 * n) + [pl.BlockSpec(memory_space=pl.ANY)],
        out_shape=(pltpu.SemaphoreType.DMA((3 * n,)), pltpu.SemaphoreType.DMA((3 * n,)),
                   *[pltpu.HBM(a.shape, a.dtype) for a in both], _sds((8, LANE), F32)),
        out_specs=(_SEM, _SEM, *([_HBM] * (2 * n)), pl.BlockSpec(memory_space=pltpu.VMEM)),
        input_output_aliases={i: 2 + i for i in range(2 * n)},
        compiler_params=_cp(has_side_effects=pltpu.SideEffectType.DATAFLOW_SIDE_EFFECTING),
    )(*[pltpu.with_memory_space_constraint(a, pltpu.HBM) for a in both], after)
    return out[0], out[1], list(out[2:2 + n]), list(out[2 + n:2 + 2 * n]), out[-1]


def ici_wait(kind, started, after, *, name):
    send_sems, recv_sems, srcs, lands, _ = started
    n = len(srcs)

    def body(*refs):
        src_refs, land_refs = refs[:n], refs[n:2 * n]
        sends, recvs = _ici_copies(kind, src_refs, land_refs, refs[2 * n], refs[2 * n + 1])
        for cp in sends:
            cp.wait_send()
        for cp in recvs:
            cp.wait_recv()

    both = list(srcs) + list(lands)
    out = pl.pallas_call(
        body, name=name,
        in_specs=[_HBM] * (2 * n) + [_SEM, _SEM, pl.BlockSpec(memory_space=pl.ANY)],
        out_shape=tuple(pltpu.HBM(a.shape, a.dtype) for a in both), out_specs=tuple([_HBM] * (2 * n)),
        input_output_aliases={i: i for i in range(2 * n)},
        compiler_params=_cp(has_side_effects=pltpu.SideEffectType.DATAFLOW_SIDE_EFFECTING),
    )(*both, send_sems, recv_sems, after)
    return list(out[:n]), list(out[n:])


BIG = ["w_in", "w_uq", "w_ukv", "w_branch_ssm", "w_branch_mla", "w_out", "w_mlp_up", "w_mlp_down"]
COL_SHARDED = {"w_in", "w_uq", "w_ukv", "w_mlp_up"}
SMALL_REPL = ["norm_mix_w", "conv_b", "dt_bias", "a_log", "d_skip", "ssm_norm_w", "q_norm_w", "kv_norm_w", "norm_mlp_w"]


def _unshard_layer(name, g):
    _, r, c = g.shape
    if name in COL_SHARDED:
        return jnp.transpose(g, (1, 0, 2)).reshape(r, 4 * c)
    return g.reshape(4 * r, c)


def _to_shards(name, full):
    r, c = full.shape
    if name in COL_SHARDED:
        return jnp.transpose(full.reshape(r, 4, c // 4), (1, 0, 2))
    return full.reshape(4, r // 4, c)


REST = [k for k in BIG if k != "w_in"]


def prep_layer(cfg, w):
    out = {}
    if "w_in" in w:
        sp = np.cumsum(cfg.in_splits)[:-1].tolist()
        z, xbc, dt, cq, ckv, kr, gs, gm = jnp.split(w["w_in"], sp, axis=1)
        zpad = lambda n: jnp.zeros((cfg.d, n), z.dtype)
        out.update(w_z=z, w_xbc=xbc, w_g=jnp.concatenate([gs, gm], axis=1),
                   w_s=jnp.concatenate([cq, ckv, kr, zpad(LANE - cfg.rope), dt, zpad(LANE - cfg.heads)], axis=1))
    if "w_uq" in w:
        out.update(
            w_uq=jnp.pad(w["w_uq"].reshape(cfg.ql, cfg.mh, cfg.nope + cfg.rope),
                         ((0, 0), (0, 0), (0, 2 * LANE - cfg.nope - cfg.rope))).reshape(cfg.ql, cfg.qw),
            w_ukv=w["w_ukv"], w_bs=w["w_branch_ssm"], w_bm=w["w_branch_mla"], w_out=w["w_out"],
            w_up=w["w_mlp_up"], w_down=w["w_mlp_down"])
    return {k: v.astype(BF16) for k, v in out.items()}


def unprep_grads(cfg, g):
    out = {}
    if "w_s" in g:
        ql, kvl = cfg.ql, cfg.kvl
        ds_ = g["w_s"]
        cq, ckv = ds_[:, :ql], ds_[:, ql:ql + kvl]
        kr = ds_[:, ql + kvl:ql + kvl + cfg.rope]
        dt = ds_[:, ql + kvl + LANE:ql + kvl + LANE + cfg.heads]
        out["w_in"] = jnp.concatenate([g["w_z"], g["w_xbc"], dt, cq, ckv, kr, g["w_g"]], axis=1)
    if "w_uq" in g:
        out.update(
            w_uq=g["w_uq"].reshape(cfg.ql, cfg.mh, 2 * LANE)[:, :, :cfg.nope + cfg.rope].reshape(cfg.ql, -1),
            w_ukv=g["w_ukv"], w_branch_ssm=g["w_bs"], w_branch_mla=g["w_bm"],
            w_out=g["w_out"], w_mlp_up=g["w_up"], w_mlp_down=g["w_down"])
    return out


def layer_fwd(cfg, h, pw, sm, tabs, li, rest=None):
    n = lambda s: f"l{li}_{s}"
    u = rmsnorm_fwd(h, sm["norm_mix_w"], name=n("norm_mix"))
    z = matmul(u, pw["w_z"], name=n("in_z"))
    xbc = matmul(u, pw["w_xbc"], name=n("in_xbc"))
    g = matmul(u, pw["w_g"], name=n("in_g"))
    small = matmul(u, pw["w_s"], name=n("in_s"), tn=cfg.sw)
    xc = conv_fwd(cfg, xbc, sm["conv_w"], sm["conv_b"], name=n("conv"))
    y, sin = ssd_fwd(cfg, xc, small, sm["dt_bias_p"], sm["avec"], sm["dexp"], name=n("ssd"))
    y_ssm = tail_fwd(cfg, y, z, sm["ssm_norm_w"], name=n("tail"))
    if rest is not None:
        pw = dict(pw, **rest(y_ssm))
    cqn = rmsnorm_fwd(small, sm["q_norm_w"], cw=cfg.ql, ci=0, name=n("q_norm"))
    ckvn = rmsnorm_fwd(small, sm["kv_norm_w"], cw=cfg.kvl, ci=cfg.ql // cfg.kvl, name=n("kv_norm"))
    qf = matmul(cqn, pw["w_uq"], name=n("uq"))
    kv = matmul(ckvn, pw["w_ukv"], out_dtype=BF16, name=n("ukv"))
    qr, kpe = rope_fwd(cfg, qf, small, tabs, name=n("rope"))
    o, lse = attn_fwd(cfg, qr, kv, kpe, name=n("attn"))
    ya = matmul(y_ssm, pw["w_bs"], name=n("branch_ssm"))
    yb = matmul(o, pw["w_bm"], name=n("branch_mla"))
    mixed = gate_fwd(cfg, ya, yb, g, name=n("gate"))
    h1 = matmul(mixed, pw["w_out"], add=h, name=n("out"))
    v = rmsnorm_fwd(h1, sm["norm_mlp_w"], name=n("norm_mlp"))
    a, act = matmul(v, pw["w_up"], name=n("up"), epilogue=_ep_relu2, out_dtypes=(BF16, BF16))
    h2 = matmul(act, pw["w_down"], add=h1, name=n("down"))
    saved = dict(h=h, u=u, z=z, xbc=xbc, g=g, small=small, xc=xc, y=y, sin=sin, y_ssm=y_ssm, cqn=cqn, ckvn=ckvn,
                 qr=qr, kv=kv, kpe=kpe, o=o, lse=lse, ya=ya, yb=yb, mixed=mixed, h1=h1, v=v, a=a, act=act)
    return h2, saved, pw


def layer_bwd(cfg, dh2, pw, sm, tabs, s, li, early=None):
    n = lambda t: f"l{li}_b_{t}"
    gw, gs = {}, {}
    gw["w_down"] = matmul(s["act"], dh2, ta=True, name=n("dw_down"))
    da = matmul(dh2, pw["w_down"], tb=True, name=n("dact"), epilogue=_ep_relu2_grad, extras=(s["a"],),
                out_dtypes=(BF16,))
    gw["w_up"] = matmul(s["v"], da, ta=True, name=n("dw_up"))
    dv = matmul(da, pw["w_up"], tb=True, name=n("dv"))
    dh1, gs["norm_mlp_w"] = rmsnorm_bwd(dv, s["h1"], sm["norm_mlp_w"], res=dh2, name=n("norm_mlp"))
    gw["w_out"] = matmul(s["mixed"], dh1, ta=True, name=n("dw_out"))
    dmix = matmul(dh1, pw["w_out"], tb=True, name=n("dmix"))
    dya, dyb, dg = gate_bwd(cfg, dmix, s["ya"], s["yb"], s["g"], name=n("gate"))
    gw["w_bs"] = matmul(s["y_ssm"], dya, ta=True, name=n("dw_bs"))
    gw["w_bm"] = matmul(s["o"], dyb, ta=True, name=n("dw_bm"))
    dy_ssm = matmul(dya, pw["w_bs"], tb=True, name=n("dy_ssm"))
    do = matmul(dyb, pw["w_bm"], tb=True, name=n("do"))
    dq, dkv, dkpe = attn_bwd(cfg, s["qr"], s["kv"], s["kpe"], s["o"], s["lse"], do, name=n("attn"))
    dqf, dkr = rope_bwd(cfg, dq, dkpe, tabs, name=n("rope"))
    gw["w_uq"] = matmul(s["cqn"], dqf, ta=True, name=n("dw_uq"))
    gw["w_ukv"] = matmul(s["ckvn"], dkv, ta=True, name=n("dw_ukv"))
    dcqn = matmul(dqf, pw["w_uq"], tb=True, name=n("dcqn"))
    dckvn = matmul(dkv, pw["w_ukv"], tb=True, name=n("dckvn"))
    dcq, gs["q_norm_w"] = rmsnorm_bwd(dcqn, s["small"], sm["q_norm_w"], cw=cfg.ql, ci=0, out_dtype=BF16, name=n("q_norm"))
    dckv, gs["kv_norm_w"] = rmsnorm_bwd(dckvn, s["small"], sm["kv_norm_w"], cw=cfg.kvl, ci=cfg.ql // cfg.kvl,
                                        out_dtype=BF16, name=n("kv_norm"))
    ssm_norm_w = sm["ssm_norm_w"]
    if early is not None:
        ssm_norm_w = ssm_norm_w + early(dict(gw))[0, 0]
    dy, dz, gs["ssm_norm_w"] = tail_bwd(cfg, dy_ssm, s["y"], s["z"], ssm_norm_w, name=n("tail"))
    dxc, ddt, ddexp, dav, dbias = ssd_bwd(cfg, s["xc"], s["small"], sm["dt_bias_p"], sm["avec"], sm["dexp"],
                                          s["sin"], dy, name=n("ssd"))
    dxbc, gs["conv_w"], gs["conv_b"] = conv_bwd(cfg, s["xbc"], sm["conv_w"], sm["conv_b"], dxc, name=n("conv"))
    gs["d_skip"] = ddexp.reshape(cfg.heads, cfg.hd).sum(axis=1)
    gs["a_log"] = (dav[0] * sm["avec"][0])[:cfg.heads]
    gs["dt_bias"] = dbias[0, :cfg.heads]
    dsmall = jnp.concatenate([dcq, dckv, dkr.astype(BF16), ddt.astype(BF16)], axis=1)
    gw["w_z"] = matmul(s["u"], dz, ta=True, name=n("dw_z"))
    gw["w_xbc"] = matmul(s["u"], dxbc, ta=True, name=n("dw_xbc"))
    gw["w_g"] = matmul(s["u"], dg, ta=True, name=n("dw_g"))
    gw["w_s"] = matmul(s["u"], dsmall, ta=True, name=n("dw_s"))
    du = matmul(dz, pw["w_z"], tb=True, name=n("du_z"))
    du = matmul(dxbc, pw["w_xbc"], tb=True, add=du, name=n("du_xbc"))
    du = matmul(dg, pw["w_g"], tb=True, add=du, name=n("du_g"))
    du = matmul(dsmall, pw["w_s"], tb=True, add=du, name=n("du_s"))
    dh, gs["norm_mix_w"] = rmsnorm_bwd(du, s["h"], sm["norm_mix_w"], res=dh1, name=n("norm_mix"))
    return dh, gw, gs


def small_params(cfg, p, li):
    pad_l = lambda v: jnp.pad(v, (0, LANE - v.shape[0])).reshape(1, LANE)
    return dict(
        norm_mix_w=p["norm_mix_w"][li], conv_w=p["conv_w"][li], conv_b=p["conv_b"][li],
        dt_bias_p=pad_l(p["dt_bias"][li]), avec=pad_l(-jnp.exp(p["a_log"][li])),
        dexp=jnp.repeat(p["d_skip"][li], cfg.hd).reshape(1, cfg.inner),
        ssm_norm_w=p["ssm_norm_w"][li], q_norm_w=p["q_norm_w"][li], kv_norm_w=p["kv_norm_w"][li],
        norm_mlp_w=p["norm_mlp_w"][li])


def local_step(cfg, x, target, p, depth=2):
    bsz, d = cfg.bsz, cfg.d
    lead = jnp.zeros((bsz, cfg.pad, d), F32)
    meta = jnp.broadcast_to(p["meta_tokens"][None], (bsz, cfg.n_meta, d))
    h = jnp.concatenate([lead, meta, x], axis=1).reshape(cfg.t, d)
    tabs = rope_tables(cfg)
    saved, sms = [], []
    for li in range(depth):
        sm = small_params(cfg, p, li)
        h, s, _ = layer_fwd(cfg, h, p["pw"][li], sm, tabs, li)
        saved.append(s)
        sms.append(sm)
    loss, dh, dfw = loss_head(cfg, h, target.reshape(bsz * cfg.seq, d), p["final_norm_w"], name="loss_head")
    gws, gss = [None] * depth, [None] * depth
    for li in reversed(range(depth)):
        dh, gws[li], gss[li] = layer_bwd(cfg, dh, p["pw"][li], sms[li], tabs, saved[li], li)
    dh = dh.reshape(bsz, cfg.lp, d)
    grad_x = dh[:, cfg.chunk:, :]
    gmeta = jnp.sum(dh[:, cfg.pad:cfg.chunk, :], axis=0)
    return loss, grad_x, gmeta, gws, gss, dfw


def _pack_small(parts):
    flat = jnp.concatenate([a.reshape(-1) for a in parts])
    n = flat.shape[0]
    npad = -n % (8 * LANE)
    return jnp.pad(flat, (0, npad)).reshape(-1, LANE), n


def _unpack_small(vec, shapes):
    flat = vec.reshape(-1)
    out, off = [], 0
    for sh in shapes:
        sz = int(np.prod(sh))
        out.append(flat[off:off + sz].reshape(sh))
        off += sz
    return out


def _as2d(a):
    return a.reshape(-1, a.shape[-1])


def kernel(x, meta_tokens, norm_mix_w, w_in, conv_w, conv_b, dt_bias, a_log, d_skip, ssm_norm_w, q_norm_w, kv_norm_w, w_uq, w_ukv, w_branch_ssm, w_branch_mla, w_out, norm_mlp_w, w_mlp_up, w_mlp_down, final_norm_w, loss_target, m_meta_tokens, m_norm_mix_w, m_w_in, m_conv_w, m_conv_b, m_dt_bias, m_a_log, m_d_skip, m_ssm_norm_w, m_q_norm_w, m_kv_norm_w, m_w_uq, m_w_ukv, m_w_branch_ssm, m_w_branch_mla, m_w_out, m_norm_mlp_w, m_w_mlp_up, m_w_mlp_down, m_final_norm_w, v_meta_tokens, v_norm_mix_w, v_w_in, v_conv_w, v_conv_b, v_dt_bias, v_a_log, v_d_skip, v_ssm_norm_w, v_q_norm_w, v_kv_norm_w, v_w_uq, v_w_ukv, v_w_branch_ssm, v_w_branch_mla, v_w_out, v_norm_mlp_w, v_w_mlp_up, v_w_mlp_down, v_final_norm_w):
    cfg = CFG
    names = ["meta_tokens", "norm_mix_w", "w_in", "conv_w", "conv_b", "dt_bias", "a_log", "d_skip", "ssm_norm_w",
             "q_norm_w", "kv_norm_w", "w_uq", "w_ukv", "w_branch_ssm", "w_branch_mla", "w_out", "norm_mlp_w",
             "w_mlp_up", "w_mlp_down", "final_norm_w"]
    wts = dict(zip(names, [meta_tokens, norm_mix_w, w_in, conv_w, conv_b, dt_bias, a_log, d_skip, ssm_norm_w,
                           q_norm_w, kv_norm_w, w_uq, w_ukv, w_branch_ssm, w_branch_mla, w_out, norm_mlp_w,
                           w_mlp_up, w_mlp_down, final_norm_w]))
    ms = dict(zip(names, [m_meta_tokens, m_norm_mix_w, m_w_in, m_conv_w, m_conv_b, m_dt_bias, m_a_log, m_d_skip,
                          m_ssm_norm_w, m_q_norm_w, m_kv_norm_w, m_w_uq, m_w_ukv, m_w_branch_ssm, m_w_branch_mla,
                          m_w_out, m_norm_mlp_w, m_w_mlp_up, m_w_mlp_down, m_final_norm_w]))
    vs = dict(zip(names, [v_meta_tokens, v_norm_mix_w, v_w_in, v_conv_w, v_conv_b, v_dt_bias, v_a_log, v_d_skip,
                          v_ssm_norm_w, v_q_norm_w, v_kv_norm_w, v_w_uq, v_w_ukv, v_w_branch_ssm, v_w_branch_mla,
                          v_w_out, v_norm_mlp_w, v_w_mlp_up, v_w_mlp_down, v_final_norm_w]))
    cx, cy, cc = _coords()
    chip = 2 * cx + cy

    half1 = jnp.reshape(cc, (1,)).astype(jnp.int32)
    where2 = jnp.stack([chip, cc]).astype(jnp.int32)
    wb = {k: wts[k].astype(BF16) for k in BIG}
    zero_tok = jnp.zeros((8, LANE), F32)

    def halves(a):
        return a.reshape((2, a.shape[0] // 2) + a.shape[1:])

    def gather_start(li, keys, tag, after):
        srcs = [halves(wb[k][li]) for k in keys]
        lands = [lax.empty((4,) + s.shape, BF16) for s in srcs]
        return ici_start("gather", srcs, lands, after, name=f"gather{li}{tag}_start")

    def gather_finish(li, keys, tag, started, after):
        srcs, lands = ici_wait("gather", started, after, name=f"gather{li}{tag}_wait")
        lands = pair_share(lands, name=f"gather{li}{tag}_share")
        full = {}
        for k, own, land in zip(keys, srcs, lands):
            slots = [jnp.where(chip == j, own, land[j]) for j in range(4)]
            full[k] = _unshard_layer(k, jnp.stack(slots).reshape((4, 2 * own.shape[1], own.shape[2])))
        return prep_layer(cfg, full)

    def reduce_start(li, keys, tag, gw, after):
        ug = unprep_grads(cfg, gw)
        g4 = []
        for k in keys:
            s = _to_shards(k, ug[k])
            g4.append(s.reshape(4, 2, s.shape[1] // 2, s.shape[2]))
        theirs = pair_exchange(g4, name=f"grad{li}{tag}_pair_exchange")
        parts = [pair_add(a, b, half1, name=f"grad{li}_pair_add_{k}") for k, a, b in zip(keys, g4, theirs)]
        lands = [lax.empty(q.shape, q.dtype) for q in parts]
        return ici_start("scatter", parts, lands, after, name=f"grad{li}{tag}_scatter_start")

    def reduce_finish(li, keys, tag, started, after):
        parts, lands = ici_wait("scatter", started, after, name=f"grad{li}{tag}_scatter_wait")
        sums = [chip_sum(rc, pt, where2, name=f"grad{li}_chip_sum_{k}") for k, rc, pt in zip(keys, lands, parts)]
        sums = pair_fill(sums, name=f"grad{li}{tag}_pair_fill")
        return {k: s.reshape(2 * s.shape[1], s.shape[2]) for k, s in zip(keys, sums)}

    gathered = gather_chips([meta_tokens, conv_w], name="gather_small")
    p = dict(wts)
    p["meta_tokens"] = jnp.transpose(gathered[0], (1, 0, 2)).reshape(cfg.n_meta, cfg.d)
    p["conv_w"] = jnp.transpose(gathered[1], (1, 2, 0, 3)).reshape(2, cfg.convk, cfg.conv_dim)

    st0a = gather_start(0, ["w_in"], "a", gathered[0])
    st0b = gather_start(0, REST, "b", st0a[4])
    st1 = gather_start(1, BIG, "", st0b[4])
    pw0 = gather_finish(0, ["w_in"], "a", st0a, st1[4])

    bsz, d = cfg.bsz, cfg.d
    lead = jnp.zeros((bsz, cfg.pad, d), F32)
    meta = jnp.broadcast_to(p["meta_tokens"][None], (bsz, cfg.n_meta, d))
    h0 = jnp.concatenate([lead, meta, x], axis=1).reshape(cfg.t, d)
    tabs = rope_tables(cfg)
    sm0 = small_params(cfg, p, 0)
    h1, sv0, pw0 = layer_fwd(cfg, h0, pw0, sm0, tabs, 0,
                             rest=lambda after: gather_finish(0, REST, "b", st0b, after))
    pw1 = gather_finish(1, BIG, "", st1, h1)
    sm1 = small_params(cfg, p, 1)
    h2, sv1, _ = layer_fwd(cfg, h1, pw1, sm1, tabs, 1)
    loss, dh, dfw = loss_head(cfg, h2, loss_target.reshape(bsz * cfg.seq, d), final_norm_w, name="loss_head")
    loss = lax.psum(loss, ("x", "y", "c"))

    dh, gw1, gs1 = layer_bwd(cfg, dh, pw1, sm1, tabs, sv1, 1)
    red1 = reduce_start(1, BIG, "", gw1, zero_tok)
    sm0b = dict(sm0)
    sm0b["norm_mlp_w"] = sm0["norm_mlp_w"] + red1[4][0, 0]
    early = {}

    def start_early(gw):
        early["st"] = reduce_start(0, REST, "e", gw, zero_tok)
        return early["st"][4]

    dh, gw0, gs0 = layer_bwd(cfg, dh, pw0, sm0b, tabs, sv0, 0, early=start_early)
    dh3 = dh.reshape(bsz, cfg.lp, d)
    grad_x = dh3[:, cfg.chunk:, :]
    gmeta = jnp.sum(dh3[:, cfg.pad:cfg.chunk, :], axis=0)
    big1 = reduce_finish(1, BIG, "", red1, dh)

    small_names = SMALL_REPL + ["conv_w"]
    parts = [jnp.stack([gs0[k], gs1[k]]) for k in small_names] + [dfw, gmeta]
    shapes = [a.shape for a in parts]
    vec, _ = _pack_small(parts)
    red_vec = allreduce_small(vec, big1[BIG[-1]], name="allreduce_small")
    red = _unpack_small(red_vec, shapes)
    sg = dict(zip(small_names + ["final_norm_w", "meta_tokens"], red))
    sg["conv_w"] = lax.dynamic_slice_in_dim(sg["conv_w"], chip * (cfg.conv_dim // 4), cfg.conv_dim // 4, axis=2)
    sg["meta_tokens"] = lax.dynamic_slice_in_dim(sg["meta_tokens"], chip * (cfg.d // 4), cfg.d // 4, axis=1)

    red0 = reduce_start(0, ["w_in"], "l", gw0, red_vec)
    grads, deltas, new_m, new_v = {}, {}, {}, {}
    dep = red0[4]
    for k in names:
        if k in BIG:
            continue
        w2, g2, m2, v2 = _as2d(wts[k]), _as2d(sg[k]), _as2d(ms[k]), _as2d(vs[k])
        dl, mn, vn = adamw_small(w2, g2, m2, v2, dep, name=f"adamw_{k}")
        grads[k] = sg[k].reshape(wts[k].shape)
        deltas[k], new_m[k], new_v[k] = (t.reshape(wts[k].shape) for t in (dl, mn, vn))

    def view(k, a):
        return jnp.swapaxes(a, 1, 2) if k == "w_in" else a

    def gview(k, g):
        return g.T if k == "w_in" else g

    wv, mv, vv = ({k: view(k, t[k]) for k in BIG} for t in (wts, ms, vs))
    outs = {}
    for k in BIG:
        outs[k] = adamw_layer(wv[k], mv[k], vv[k], gview(k, big1[k]), 1, None, dep, name=f"adamw1_{k}")
        dep = outs[k][1]
    big0 = reduce_finish(0, REST, "e", early["st"], dep)
    for k in REST:
        outs[k] = adamw_layer(wv[k], mv[k], vv[k], big0[k], 0, outs[k], dep, name=f"adamw0_{k}")
        dep = outs[k][1]
    big0.update(reduce_finish(0, ["w_in"], "l", red0, dep))
    outs["w_in"] = adamw_layer(wv["w_in"], mv["w_in"], vv["w_in"], gview("w_in", big0["w_in"]), 0, outs["w_in"], dep,
                               name="adamw0_w_in")
    for k in BIG:
        grads[k], deltas[k], new_m[k], new_v[k] = (view(k, t) for t in outs[k])
    return (loss, grad_x, *[grads[k] for k in names], *[deltas[k] for k in names],
            *[new_m[k] for k in names], *[new_v[k] for k in names])


def adamw_small(w, g, m, v, dep, *, name):
    def body(w_ref, g_ref, m_ref, v_ref, dep_ref, d_ref, mo_ref, vo_ref):
        d_ref[...], mo_ref[...], vo_ref[...] = _adam_update(w_ref[...], g_ref[...], m_ref[...], v_ref[...])

    vm = pl.BlockSpec(memory_space=pltpu.VMEM)
    return pl.pallas_call(body, name=name, in_specs=[vm] * 4 + [pl.BlockSpec(memory_space=pl.ANY)], out_specs=[vm] * 3,
                          out_shape=[_sds(w.shape, F32)] * 3, compiler_params=_cp())(w, g, m, v, dep)
```

```python
import functools
import math
from typing import NamedTuple

import numpy as np
import jax
import jax.numpy as jnp
from jax import lax
from jax.experimental import pallas as pl
from jax.experimental.pallas import tpu as pltpu

F32 = jnp.float32
BF16 = jnp.bfloat16
HI = lax.Precision.HIGHEST
EPS = 1e-6
ROPE_THETA = 10000.0
LANE = 128
VMEM_LIMIT = 56 * 1024 * 1024
MASK_VALUE = -1e30
ADAM_LR, ADAM_B1, ADAM_B2, ADAM_EPS, ADAM_WD, ADAM_STEP = 0.001, 0.9, 0.999, 1e-08, 0.01, 10
MESH = pl.DeviceIdType.MESH


class Cfg(NamedTuple):
    d: int = 1024
    seq: int = 2048
    bsz: int = 2
    n_meta: int = 16
    inner: int = 2048
    hd: int = 64
    groups: int = 4
    state: int = 128
    convk: int = 4
    chunk: int = 128
    mh: int = 8
    ql: int = 512
    kvl: int = 256
    nope: int = 128
    rope: int = 64
    vd: int = 128
    ff: int = 4096

    @property
    def heads(self): return self.inner // self.hd
    @property
    def gw(self): return self.inner // self.groups
    @property
    def conv_dim(self): return self.inner + 2 * self.groups * self.state
    @property
    def pad(self): return self.chunk - self.n_meta
    @property
    def lp(self): return self.chunk + self.seq
    @property
    def t(self): return self.bsz * self.lp
    @property
    def nchunks(self): return self.lp // self.chunk
    @property
    def sw(self): return self.ql + self.kvl + 2 * LANE
    @property
    def kt(self): return (self.ql + self.kvl) // LANE
    @property
    def dtt(self): return self.kt + 1
    @property
    def qw(self): return self.mh * 2 * LANE
    @property
    def in_splits(self):
        return [self.inner, self.conv_dim, self.heads, self.ql, self.kvl, self.rope, self.d, self.d]


CFG = Cfg()


def _pick(dim, pref, mult):
    best = None
    for t in range(mult, min(dim, pref) + 1, mult):
        if dim % t == 0:
            best = t
    return best if best is not None else dim


def _cp(**kw):
    return pltpu.CompilerParams(vmem_limit_bytes=VMEM_LIMIT, **kw)


def _sds(shape, dtype):
    return jax.ShapeDtypeStruct(tuple(shape), dtype)


def _silu(x):
    return x * jax.nn.sigmoid(x)


def _dsilu(x):
    s = jax.nn.sigmoid(x)
    return s * (1.0 + x * (1.0 - s))


def _ep_plain(r):
    return (r,)


def _ep_add(r, res):
    return (r + res.astype(F32),)


def _ep_relu2(r):
    rp = jnp.maximum(r, 0.0)
    return r, rp * rp


def _ep_relu2_grad(r, a):
    return (r * (2.0 * jnp.maximum(a.astype(F32), 0.0)),)


def matmul(a, b, *, ta=False, tb=False, out_dtype=F32, add=None, name, tm=None, tn=None, tk=None,
           epilogue=None, extras=(), out_dtypes=None):
    if add is not None:
        epilogue, extras = _ep_add, (add,)
    if epilogue is None:
        epilogue = _ep_plain
    out_dtypes = tuple(out_dtypes) if out_dtypes is not None else (out_dtype,)
    n_ex, n_out = len(extras), len(out_dtypes)
    if ta:
        k_dim, m_dim = a.shape
    else:
        m_dim, k_dim = a.shape
    if tb:
        n_dim, k2 = b.shape
    else:
        k2, n_dim = b.shape
    assert k_dim == k2, (a.shape, b.shape, ta, tb)
    if ta:
        tm = tm or _pick(m_dim, 1024, LANE)
        tk = tk or _pick(k_dim, 1088, 16)
        tn = tn or _pick(n_dim, 1024, LANE)
    else:
        tm = tm or _pick(m_dim, 1088, 16)
        tk = tk or _pick(k_dim, 1024 if a.dtype == F32 else 2048, LANE)
        tn = tn or _pick(n_dim, 1024, LANE)
    nm, nn, nk = m_dim // tm, n_dim // tn, k_dim // tk
    dn = (((0 if ta else 1,), (1 if tb else 0,)), ((), ()))

    def body(*refs):
        a_ref, b_ref = refs[:2]
        ex_refs = refs[2:2 + n_ex]
        o_refs = refs[2 + n_ex:2 + n_ex + n_out]
        scr = refs[2 + n_ex + n_out:]
        p = lax.dot_general(a_ref[...].astype(BF16), b_ref[...].astype(BF16), dn, preferred_element_type=F32)

        def finish(r):
            outs = epilogue(r, *[e[...] for e in ex_refs])
            for o_ref, val, dt in zip(o_refs, outs, out_dtypes):
                o_ref[...] = val.astype(dt)

        if nk == 1:
            finish(p)
        else:
            acc = scr[0]
            k = pl.program_id(2)

            @pl.when(k == 0)
            def _():
                acc[...] = p

            @pl.when(k > 0)
            def _():
                acc[...] += p

            @pl.when(k == nk - 1)
            def _():
                finish(acc[...])

    a_spec = pl.BlockSpec((tk, tm), lambda i, j, k: (k, i)) if ta else pl.BlockSpec((tm, tk), lambda i, j, k: (i, k))
    b_spec = pl.BlockSpec((tn, tk), lambda i, j, k: (j, k)) if tb else pl.BlockSpec((tk, tn), lambda i, j, k: (k, j))
    o_spec = pl.BlockSpec((tm, tn), lambda i, j, k: (i, j))
    outs = pl.pallas_call(
        body, name=name, grid=(nm, nn, nk), in_specs=[a_spec, b_spec] + [o_spec] * n_ex, out_specs=[o_spec] * n_out,
        out_shape=[_sds((m_dim, n_dim), dt) for dt in out_dtypes],
        scratch_shapes=[pltpu.VMEM((tm, tn), F32)] if nk > 1 else [],
        compiler_params=_cp(dimension_semantics=("parallel", "parallel", "arbitrary")),
    )(a, b, *extras)
    return outs[0] if n_out == 1 else tuple(outs)


def rmsnorm_fwd(x, w, *, cw=None, ci=0, name):
    t = x.shape[0]
    cw = cw or x.shape[1]
    tr = _pick(t, 544, 16)

    def body(x_ref, w_ref, o_ref):
        xv = x_ref[...].astype(F32)
        r = lax.rsqrt(jnp.mean(xv * xv, axis=-1, keepdims=True) + EPS)
        o_ref[...] = (xv * r * w_ref[...]).astype(BF16)

    return pl.pallas_call(
        body, name=name, grid=(t // tr,),
        in_specs=[pl.BlockSpec((tr, cw), lambda i: (i, ci)), pl.BlockSpec((1, cw), lambda i: (0, 0))],
        out_specs=pl.BlockSpec((tr, cw), lambda i: (i, 0)),
        out_shape=_sds((t, cw), BF16), compiler_params=_cp(),
    )(x, w.reshape(1, cw))


def rmsnorm_bwd(dy, x, w, *, cw=None, ci=0, res=None, out_dtype=F32, name):
    t = x.shape[0]
    cw = cw or x.shape[1]
    tr = _pick(t, 544, 16)
    has_res = res is not None

    def body(*refs):
        if has_res:
            dy_ref, x_ref, w_ref, res_ref, dx_ref, dw_ref = refs
        else:
            dy_ref, x_ref, w_ref, dx_ref, dw_ref = refs
        xv = x_ref[...].astype(F32)
        dyv = dy_ref[...].astype(F32)
        r = lax.rsqrt(jnp.mean(xv * xv, axis=-1, keepdims=True) + EPS)
        xh = xv * r
        g = dyv * w_ref[...]
        dx = r * (g - xh * jnp.mean(g * xh, axis=-1, keepdims=True))
        if has_res:
            dx = dx + res_ref[...]
        dx_ref[...] = dx.astype(out_dtype)

        @pl.when(pl.program_id(0) == 0)
        def _():
            dw_ref[...] = jnp.zeros_like(dw_ref)

        dw_ref[...] += jnp.sum(dyv * xh, axis=0, keepdims=True)

    row = pl.BlockSpec((tr, cw), lambda i: (i, 0))
    in_specs = [row, pl.BlockSpec((tr, cw), lambda i: (i, ci)), pl.BlockSpec((1, cw), lambda i: (0, 0))]
    args = [dy, x, w.reshape(1, cw)]
    if has_res:
        in_specs.append(row)
        args.append(res)
    dx, dw = pl.pallas_call(
        body, name=name, grid=(t // tr,), in_specs=in_specs,
        out_specs=[row, pl.BlockSpec((1, cw), lambda i: (0, 0))],
        out_shape=[_sds((t, cw), out_dtype), _sds((1, cw), F32)], compiler_params=_cp(),
    )(*args)
    return dx, dw[0]


def _shift_down(x, s):
    return x if s == 0 else pltpu.roll(x, s, 0)


def _shift_up(x, s):
    return x if s == 0 else pltpu.roll(x, x.shape[0] - s, 0)


def _conv_pre(x, w_ref, b_ref, kk):
    pre = b_ref[...] + jnp.zeros_like(x)
    for k in range(kk):
        pre = pre + w_ref[k:k + 1, :] * _shift_down(x, kk - 1 - k)
    return pre


def conv_fwd(cfg, xbc, w, b, *, name):
    lp, cd, kk = cfg.lp, cfg.conv_dim, cfg.convk
    assert cfg.pad >= kk - 1
    cb = _pick(cd, 512, LANE)

    def body(x_ref, w_ref, b_ref, o_ref):
        o_ref[...] = _silu(_conv_pre(x_ref[...], w_ref, b_ref, kk))

    blk = pl.BlockSpec((lp, cb), lambda j, bb: (bb, j))
    return pl.pallas_call(
        body, name=name, grid=(cd // cb, cfg.bsz),
        in_specs=[blk, pl.BlockSpec((kk, cb), lambda j, bb: (0, j)), pl.BlockSpec((1, cb), lambda j, bb: (0, j))],
        out_specs=blk, out_shape=_sds((cfg.t, cd), F32), compiler_params=_cp(),
    )(xbc, w, b.reshape(1, cd))


def conv_bwd(cfg, xbc, w, b, dxc, *, name):
    lp, cd, kk = cfg.lp, cfg.conv_dim, cfg.convk
    cb = _pick(cd, 512, LANE)

    def body(x_ref, w_ref, b_ref, d_ref, dx_ref, dw_ref, db_ref):
        x = x_ref[...]
        pre = _conv_pre(x, w_ref, b_ref, kk)
        dpre = d_ref[...] * _dsilu(pre)
        dx = jnp.zeros_like(x)
        dws = []
        for k in range(kk):
            s = kk - 1 - k
            dx = dx + w_ref[k:k + 1, :] * _shift_up(dpre, s)
            dws.append(jnp.sum(dpre * _shift_down(x, s), axis=0, keepdims=True))
        dx_ref[...] = dx.astype(BF16)

        @pl.when(pl.program_id(1) == 0)
        def _():
            dw_ref[...] = jnp.zeros_like(dw_ref)
            db_ref[...] = jnp.zeros_like(db_ref)

        for k in range(kk):
            dw_ref[k:k + 1, :] += dws[k]
        db_ref[...] += jnp.sum(dpre, axis=0, keepdims=True)

    blk = pl.BlockSpec((lp, cb), lambda j, bb: (bb, j))
    wsp = pl.BlockSpec((kk, cb), lambda j, bb: (0, j))
    bsp = pl.BlockSpec((1, cb), lambda j, bb: (0, j))
    dx, dw, db = pl.pallas_call(
        body, name=name, grid=(cd // cb, cfg.bsz),
        in_specs=[blk, wsp, bsp, blk], out_specs=[blk, wsp, bsp],
        out_shape=[_sds((cfg.t, cd), BF16), _sds((kk, cd), F32), _sds((1, cd), F32)], compiler_params=_cp(),
    )(xbc, w, b.reshape(1, cd), dxc)
    return dx, dw, db[0]


def _softplus(x):
    return jnp.maximum(x, 0.0) + jnp.log(1.0 + jnp.exp(-jnp.abs(x)))


def _ssd_consts(cfg):
    q = cfg.chunk
    i0 = np.arange(q)[:, None]
    i1 = np.arange(q)[None, :]
    ltri = (i1 <= i0).astype(np.float32)
    rexp = np.zeros((LANE, cfg.inner), np.float32)
    for h in range(cfg.heads):
        rexp[h, h * cfg.hd:(h + 1) * cfg.hd] = 1.0
    return jnp.asarray(ltri), jnp.asarray(rexp)


def _sel_dot(x, m, *, passes=2, left=False, trans=False):
    mb = m.astype(BF16)
    acc, rem = None, x
    for _ in range(passes):
        piece = rem.astype(BF16)
        if not left:
            part = _nn(piece, mb)
        elif trans:
            part = _tn(mb, piece)
        else:
            part = _nn(mb, piece)
        acc = part if acc is None else acc + part
        rem = rem - piece.astype(F32)
    return acc


def _ssd_chunk_common(cfg, raw, bias, avec, c_idx, ltri, rexp):
    q = cfg.chunk
    rows = lax.broadcasted_iota(jnp.int32, (q, LANE), 0)
    live = jnp.logical_or(c_idx > 0, rows >= cfg.pad)
    pre = raw + bias
    dt = jnp.where(live, _softplus(pre), 0.0)
    adt = dt * avec
    cs = _sel_dot(adt, ltri, passes=3, left=True)
    cs_t = cs.T
    cs_last = cs[q - 1:q, :]
    e_in = jnp.exp(cs)
    w0 = jnp.exp(cs_last - cs)
    decay = jnp.exp(cs_last)
    return dict(live=live, pre=pre, dt=dt, adt=adt, cs=cs, cs_t=cs_t, e_in=e_in, w0=w0, decay=decay,
                DT=_sel_dot(dt, rexp), E=_sel_dot(e_in, rexp), W0=_sel_dot(w0, rexp),
                DEC=_sel_dot(jnp.broadcast_to(decay, (8, LANE)), rexp)[0:1, :])


def _tri_masks(q):
    r = lax.broadcasted_iota(jnp.int32, (q, q), 0)
    c = lax.broadcasted_iota(jnp.int32, (q, q), 1)
    return c <= r, r <= c


def _head_l(cq, h, tri, tri_t):
    col = cq["cs"][:, h:h + 1]
    row = cq["cs_t"][h:h + 1, :]
    lmat = jnp.where(tri, jnp.exp(jnp.minimum(col - row, 0.0)), 0.0)
    lmat_t = jnp.where(tri_t, jnp.exp(jnp.minimum(row - col, 0.0)), 0.0)
    return lmat, lmat_t


def _nt(a, b):
    return lax.dot_general(a, b, (((1,), (1,)), ((), ())), preferred_element_type=F32)


def _tn(a, b):
    return lax.dot_general(a, b, (((0,), (0,)), ((), ())), preferred_element_type=F32)


def _nn(a, b):
    return jnp.dot(a, b, preferred_element_type=F32)


def ssd_fwd(cfg, xc, small, dt_bias, avec, dexp, *, name):
    q, inner, st, gw, g_n = cfg.chunk, cfg.inner, cfg.state, cfg.gw, cfg.groups
    nc = cfg.nchunks
    ltri, rexp = _ssd_consts(cfg)
    hpt = LANE // cfg.hd
    tiles_per_group = gw // LANE

    def body(x_ref, b_ref, c_ref, dt_ref, bias_ref, a_ref, d_ref, ltri_ref, rexp_ref, y_ref, sin_ref, s_scr):
        c_idx = pl.program_id(1)

        @pl.when(c_idx == 0)
        def _():
            s_scr[...] = jnp.zeros_like(s_scr)

        ltri_v = ltri_ref[...]
        tri, tri_t = _tri_masks(q)
        cq = _ssd_chunk_common(cfg, dt_ref[...], bias_ref[...], a_ref[...], c_idx, ltri_v, rexp_ref[...])
        xs = x_ref[...]
        xdt = (xs * cq["DT"]).astype(BF16)
        xw = (xs * cq["DT"] * cq["W0"]).astype(BF16)
        s_in = s_scr[...]
        sin_ref[0] = s_in
        lane = lax.broadcasted_iota(jnp.int32, (q, LANE), 1)
        for g in range(g_n):
            bg = b_ref[:, g * st:(g + 1) * st].astype(BF16)
            cg = c_ref[:, g * st:(g + 1) * st].astype(BF16)
            gmat = _nt(cg, bg)
            gs = slice(g * gw, (g + 1) * gw)
            y0 = _nn(cg, s_in[:, gs].astype(BF16))
            for tt in range(tiles_per_group):
                tile = g * tiles_per_group + tt
                ts = slice(tile * LANE, (tile + 1) * LANE)
                xt = xdt[:, ts]
                yd = None
                for hh in range(hpt):
                    h = tile * hpt + hh
                    lmat, _ = _head_l(cq, h, tri, tri_t)
                    part = _nn((gmat * lmat).astype(BF16), xt)
                    if yd is None:
                        yd = part
                    else:
                        yd = jnp.where(lane < (hh * cfg.hd), yd, part)
                y_ref[:, ts] = yd + y0[:, tt * LANE:(tt + 1) * LANE] * cq["E"][:, ts] + xs[:, ts] * d_ref[:, ts]
            s_scr[:, gs] = s_in[:, gs] * cq["DEC"][:, gs] + _tn(bg, xw[:, gs])

    def rowblk(width, col):
        return pl.BlockSpec((q, width), lambda b, c: (b * nc + c, col))

    def const(shape):
        return pl.BlockSpec(shape, lambda b, c: (0, 0))

    y, sin = pl.pallas_call(
        body, name=name, grid=(cfg.bsz, nc),
        in_specs=[rowblk(inner, 0),
                  pl.BlockSpec((q, g_n * st), lambda b, c: (b * nc + c, inner // (g_n * st))),
                  pl.BlockSpec((q, g_n * st), lambda b, c: (b * nc + c, inner // (g_n * st) + 1)),
                  rowblk(LANE, cfg.dtt), const((1, LANE)), const((1, LANE)), const((1, inner)),
                  const((q, q)), const((LANE, inner))],
        out_specs=[rowblk(inner, 0), pl.BlockSpec((1, st, inner), lambda b, c: (b * nc + c, 0, 0))],
        out_shape=[_sds((cfg.t, inner), F32), _sds((cfg.bsz * nc, st, inner), F32)],
        scratch_shapes=[pltpu.VMEM((st, inner), F32)], compiler_params=_cp(),
    )(xc, xc, xc, small, dt_bias, avec, dexp, ltri, rexp)
    return y, sin


def ssd_bwd(cfg, xc, small, dt_bias, avec, dexp, sin, dy, *, name):
    q, inner, st, gw, g_n = cfg.chunk, cfg.inner, cfg.state, cfg.gw, cfg.groups
    nc = cfg.nchunks
    ltri, rexp = _ssd_consts(cfg)
    rexp_t = rexp.T
    hpt = LANE // cfg.hd
    tiles_per_group = gw // LANE
    bcw = g_n * st

    def body(x_ref, b_ref, c_ref, dt_ref, bias_ref, a_ref, d_ref, ltri_ref, rexp_ref, rexpt_ref, sin_ref, dy_ref,
             dx_ref, ddt_ref, dd_ref, da_ref, dbias_ref, ds_scr):
        step = pl.program_id(1)
        c_idx = nc - 1 - step

        @pl.when(step == 0)
        def _():
            ds_scr[...] = jnp.zeros_like(ds_scr)

        @pl.when(jnp.logical_and(step == 0, pl.program_id(0) == 0))
        def _():
            dd_ref[...] = jnp.zeros_like(dd_ref)
            da_ref[...] = jnp.zeros_like(da_ref)
            dbias_ref[...] = jnp.zeros_like(dbias_ref)

        ltri_v = ltri_ref[...]
        tri, tri_t = _tri_masks(q)
        red = _sel_dot
        rexpt = rexpt_ref[...]
        cq = _ssd_chunk_common(cfg, dt_ref[...], bias_ref[...], a_ref[...], c_idx, ltri_v, rexp_ref[...])
        xs = x_ref[...]
        dyv = dy_ref[...]
        s_in = sin_ref[0]
        d_s = ds_scr[...]
        xdt_f = xs * cq["DT"]
        xdt = xdt_f.astype(BF16)
        xw_f = xdt_f * cq["W0"]
        xw = xw_f.astype(BF16)
        lane = lax.broadcasted_iota(jnp.int32, (q, LANE), 1)
        sub = lax.broadcasted_iota(jnp.int32, (LANE, q), 0)

        dd_ref[...] += jnp.sum(dyv * xs, axis=0, keepdims=True)
        dy0 = dyv * cq["E"]
        dcs = jnp.zeros((q, LANE), F32)
        dcs_t = jnp.zeros((LANE, q), F32)
        for g in range(g_n):
            bg_f = b_ref[:, g * st:(g + 1) * st]
            cg_f = c_ref[:, g * st:(g + 1) * st]
            bg = bg_f.astype(BF16)
            cg = cg_f.astype(BF16)
            gs = slice(g * gw, (g + 1) * gw)
            gmat = _nt(cg, bg)
            gmat_t = _nt(bg, cg)
            sing = s_in[:, gs].astype(BF16)
            dsg = d_s[:, gs].astype(BF16)
            y0 = _nn(cg, sing)
            dxw = _nn(bg, dsg)
            d_bg = _nt(xw[:, gs], dsg)
            d_cg = _nt(dy0[:, gs].astype(BF16), sing)
            ds_in_g = _tn(cg, dy0[:, gs].astype(BF16))
            dg = jnp.zeros((q, q), F32)
            dxdt_g = []
            for tt in range(tiles_per_group):
                tile = g * tiles_per_group + tt
                ts = slice(tile * LANE, (tile + 1) * LANE)
                xt = xdt[:, ts]
                dyt = dyv[:, ts]
                dxdt_t = None
                for hh in range(hpt):
                    h = tile * hpt + hh
                    lmat, lmat_t = _head_l(cq, h, tri, tri_t)
                    inhead = jnp.logical_and(lane >= hh * cfg.hd, lane < (hh + 1) * cfg.hd)
                    dyh = jnp.where(inhead, dyt, 0.0).astype(BF16)
                    dm = _nt(dyh, xt)
                    dg = dg + dm * lmat
                    qm = dm * gmat * lmat
                    rs = jnp.sum(qm, axis=1, keepdims=True)
                    csum = jnp.sum(qm, axis=0, keepdims=True)
                    dcs = dcs + jnp.where(lane == h, rs, 0.0)
                    dcs_t = dcs_t + jnp.where(sub == h, csum, 0.0)
                    part = _nn((gmat_t * lmat_t).astype(BF16), dyh)
                    dxdt_t = part if dxdt_t is None else dxdt_t + part
                dxdt_g.append(dxdt_t)
            dxdt_diag = jnp.concatenate(dxdt_g, axis=1) if len(dxdt_g) > 1 else dxdt_g[0]
            dgb = dg.astype(BF16)
            d_cg = d_cg + _nn(dgb, bg)
            d_bg = d_bg + _tn(dgb, cg)
            dx_ref[:, inner + g * st:inner + (g + 1) * st] = d_bg
            dx_ref[:, inner + bcw + g * st:inner + bcw + (g + 1) * st] = d_cg
            dxdt = dxdt_diag + dxw * cq["W0"][:, gs]
            dx_ref[:, gs] = dyv[:, gs] * d_ref[:, gs] + dxdt * cq["DT"][:, gs]
            rt = rexpt[gs, :]
            dcs = dcs + red(dyv[:, gs] * y0 * cq["E"][:, gs], rt)
            r_w = red(dxw * xw_f[:, gs], rt)
            dcs = dcs - r_w
            dcs_last_g = jnp.sum(r_w, axis=0, keepdims=True)
            ddec = red(jnp.broadcast_to(jnp.sum(d_s[:, gs] * s_in[:, gs], axis=0, keepdims=True), (8, gw)), rt)[0:1, :]
            dcs_last_g = dcs_last_g + ddec * cq["decay"]
            dcs = dcs + jnp.where(lax.broadcasted_iota(jnp.int32, (q, LANE), 0) == q - 1, dcs_last_g, 0.0)
            ddt_part = red(dxdt * xs[:, gs], rt)
            if g == 0:
                ddt = ddt_part
            else:
                ddt = ddt + ddt_part
            ds_scr[:, gs] = d_s[:, gs] * cq["DEC"][:, gs] + ds_in_g
        dcs = dcs - dcs_t.T
        dadt = _sel_dot(dcs, ltri_v, left=True, trans=True)
        ddt = ddt + dadt * a_ref[...]
        da_ref[...] += jnp.sum(dadt * cq["dt"], axis=0, keepdims=True)
        draw = jnp.where(cq["live"], ddt * jax.nn.sigmoid(cq["pre"]), 0.0)
        ddt_ref[...] = draw
        dbias_ref[...] += jnp.sum(draw, axis=0, keepdims=True)

    def rowblk(width, col):
        return pl.BlockSpec((q, width), lambda b, s: (b * nc + nc - 1 - s, col))

    def const(shape):
        return pl.BlockSpec(shape, lambda b, s: (0, 0))

    bcol = inner // bcw
    outs = pl.pallas_call(
        body, name=name, grid=(cfg.bsz, nc),
        in_specs=[rowblk(inner, 0), rowblk(bcw, bcol), rowblk(bcw, bcol + 1), rowblk(LANE, cfg.dtt),
                  const((1, LANE)), const((1, LANE)), const((1, inner)), const((q, q)), const((LANE, inner)),
                  const((inner, LANE)),
                  pl.BlockSpec((1, st, inner), lambda b, s: (b * nc + nc - 1 - s, 0, 0)), rowblk(inner, 0)],
        out_specs=[rowblk(cfg.conv_dim, 0), rowblk(LANE, 0),
                   const((1, inner)), const((1, LANE)), const((1, LANE))],
        out_shape=[_sds((cfg.t, cfg.conv_dim), F32),
                   _sds((cfg.t, LANE), F32), _sds((1, inner), F32), _sds((1, LANE), F32), _sds((1, LANE), F32)],
        scratch_shapes=[pltpu.VMEM((st, inner), F32)], compiler_params=_cp(),
    )(xc, xc, xc, small, dt_bias, avec, dexp, ltri, rexp, rexp_t, sin, dy)
    return outs


def tail_fwd(cfg, y, z, w, *, name):
    t, inner, gw = cfg.t, cfg.inner, cfg.gw
    tr = _pick(t, 272, 16)

    def body(y_ref, z_ref, w_ref, o_ref):
        for g in range(cfg.groups):
            gs = slice(g * gw, (g + 1) * gw)
            yg = y_ref[:, gs] * _silu(z_ref[:, gs].astype(F32))
            r = lax.rsqrt(jnp.mean(yg * yg, axis=-1, keepdims=True) + EPS)
            o_ref[:, gs] = (yg * r * w_ref[:, gs]).astype(BF16)

    row = pl.BlockSpec((tr, inner), lambda i: (i, 0))
    return pl.pallas_call(
        body, name=name, grid=(t // tr,), in_specs=[row, row, pl.BlockSpec((1, inner), lambda i: (0, 0))],
        out_specs=row, out_shape=_sds((t, inner), BF16), compiler_params=_cp(),
    )(y, z, w.reshape(1, inner))


def tail_bwd(cfg, do, y, z, w, *, name):
    t, inner, gw = cfg.t, cfg.inner, cfg.gw
    tr = _pick(t, 272, 16)

    def body(do_ref, y_ref, z_ref, w_ref, dy_ref, dz_ref, dw_ref):
        @pl.when(pl.program_id(0) == 0)
        def _():
            dw_ref[...] = jnp.zeros_like(dw_ref)

        for g in range(cfg.groups):
            gs = slice(g * gw, (g + 1) * gw)
            yv = y_ref[:, gs]
            zv = z_ref[:, gs].astype(F32)
            dov = do_ref[:, gs]
            sz = _silu(zv)
            yg = yv * sz
            r = lax.rsqrt(jnp.mean(yg * yg, axis=-1, keepdims=True) + EPS)
            xh = yg * r
            gg = dov * w_ref[:, gs]
            dyg = r * (gg - xh * jnp.mean(gg * xh, axis=-1, keepdims=True))
            dw_ref[:, gs] += jnp.sum(dov * xh, axis=0, keepdims=True)
            dy_ref[:, gs] = dyg * sz
            dz_ref[:, gs] = (dyg * yv * _dsilu(zv)).astype(BF16)

    row = pl.BlockSpec((tr, inner), lambda i: (i, 0))
    vec = pl.BlockSpec((1, inner), lambda i: (0, 0))
    dy, dz, dw = pl.pallas_call(
        body, name=name, grid=(t // tr,), in_specs=[row, row, row, vec], out_specs=[row, row, vec],
        out_shape=[_sds((t, inner), F32), _sds((t, inner), BF16), _sds((1, inner), F32)], compiler_params=_cp(),
    )(do, y, z, w.reshape(1, inner))
    return dy, dz, dw[0]


def rope_tables(cfg):
    half = cfg.rope // 2
    pos = np.maximum(np.arange(cfg.lp) - cfg.pad, 0).astype(np.float32)
    inv = ROPE_THETA ** (-jnp.arange(0, cfg.rope, 2, dtype=F32) / cfg.rope)
    ang = jnp.asarray(pos)[:, None] * inv[None, :]
    cos, sin = jnp.cos(ang), jnp.sin(ang)
    zero = jnp.zeros((cfg.lp, LANE - 2 * half), F32)
    zh = jnp.zeros((cfg.lp, half), F32)
    ctab = jnp.concatenate([cos, cos, zero], axis=1)
    s1 = jnp.concatenate([-sin, zh, zero], axis=1)
    s2 = jnp.concatenate([zh, sin, zero], axis=1)
    return ctab, s1, s2


def _rope(x, c, s1, s2, half):
    return x * c + pltpu.roll(x, LANE - half, 1) * s1 + pltpu.roll(x, half, 1) * s2


def _rope_t(dy, c, s1, s2, half):
    return dy * c + pltpu.roll(dy * s1, half, 1) + pltpu.roll(dy * s2, LANE - half, 1)


def _attn_scale(cfg):
    return (cfg.nope + cfg.rope) ** -0.5


def rope_fwd(cfg, qf, small, tabs, *, name):
    t, qw, lp = cfg.t, cfg.qw, cfg.lp
    tr = _pick(lp, 544, 16)
    nrb = lp // tr
    half = cfg.rope // 2
    scale = _attn_scale(cfg)

    def body(q_ref, k_ref, c_ref, s1_ref, s2_ref, qo_ref, ko_ref):
        c, s1, s2 = c_ref[...], s1_ref[...], s2_ref[...]
        for h in range(cfg.mh):
            a = h * 2 * LANE
            qo_ref[:, a:a + LANE] = (q_ref[:, a:a + LANE] * scale).astype(BF16)
            qo_ref[:, a + LANE:a + 2 * LANE] = (_rope(q_ref[:, a + LANE:a + 2 * LANE], c, s1, s2, half) * scale).astype(BF16)
        ko_ref[...] = _rope(k_ref[...], c, s1, s2, half).astype(BF16)

    tab = pl.BlockSpec((tr, LANE), lambda i: (i % nrb, 0))
    return pl.pallas_call(
        body, name=name, grid=(t // tr,),
        in_specs=[pl.BlockSpec((tr, qw), lambda i: (i, 0)), pl.BlockSpec((tr, LANE), lambda i: (i, cfg.kt)), tab, tab, tab],
        out_specs=[pl.BlockSpec((tr, qw), lambda i: (i, 0)), pl.BlockSpec((tr, LANE), lambda i: (i, 0))],
        out_shape=[_sds((t, qw), BF16), _sds((t, LANE), BF16)], compiler_params=_cp(),
    )(qf, small, *tabs)


def rope_bwd(cfg, dq, dkpe, tabs, *, name):
    t, qw, lp = cfg.t, cfg.qw, cfg.lp
    tr = _pick(lp, 544, 16)
    nrb = lp // tr
    half = cfg.rope // 2
    scale = _attn_scale(cfg)

    def body(dq_ref, dk_ref, c_ref, s1_ref, s2_ref, qo_ref, ko_ref):
        c, s1, s2 = c_ref[...], s1_ref[...], s2_ref[...]
        for h in range(cfg.mh):
            a = h * 2 * LANE
            qo_ref[:, a:a + LANE] = (dq_ref[:, a:a + LANE] * scale).astype(BF16)
            qo_ref[:, a + LANE:a + 2 * LANE] = _rope_t(dq_ref[:, a + LANE:a + 2 * LANE] * scale, c, s1, s2, half).astype(BF16)
        dk = dk_ref[0]
        for h in range(1, cfg.mh):
            dk = dk + dk_ref[h]
        ko_ref[...] = _rope_t(dk, c, s1, s2, half)

    tab = pl.BlockSpec((tr, LANE), lambda i: (i % nrb, 0))
    return pl.pallas_call(
        body, name=name, grid=(t // tr,),
        in_specs=[pl.BlockSpec((tr, qw), lambda i: (i, 0)), pl.BlockSpec((cfg.mh, tr, LANE), lambda i: (0, i, 0)),
                  tab, tab, tab],
        out_specs=[pl.BlockSpec((tr, qw), lambda i: (i, 0)), pl.BlockSpec((tr, LANE), lambda i: (i, 0))],
        out_shape=[_sds((t, qw), BF16), _sds((t, LANE), F32)], compiler_params=_cp(),
    )(dq, dkpe, *tabs)


def _q_blocks(cfg):
    bounds = [0, cfg.chunk] + list(range(cfg.chunk + 256, cfg.lp + 1, 256))
    assert bounds[-1] == cfg.lp, "SEQ must be a multiple of 256"
    return list(zip(bounds[:-1], bounds[1:]))


def _attn_mask(cfg, qs, qe):
    rows = qs + lax.broadcasted_iota(jnp.int32, (qe - qs, qe), 0)
    cols = lax.broadcasted_iota(jnp.int32, (qe - qs, qe), 1)
    return jnp.logical_and(cols <= rows, jnp.logical_or(cols >= cfg.pad, rows < cfg.pad))


def _max_q_block(cfg):
    return max(qe - qs for qs, qe in _q_blocks(cfg))


def _masked_scores(cfg, q, k2, qs, qe, s_scr):
    bq, n = qe - qs, qe
    s_scr[0:bq, 0:n] = _nt(q, k2)
    if qs == 0:
        s_scr[0:bq, 0:n] = jnp.where(_attn_mask(cfg, 0, qe), s_scr[0:bq, 0:n], MASK_VALUE)
    else:
        assert qs >= cfg.chunk and cfg.pad < LANE
        cols = lax.broadcasted_iota(jnp.int32, (bq, LANE), 1)
        s_scr[0:bq, 0:LANE] = jnp.where(cols >= cfg.pad, s_scr[0:bq, 0:LANE], MASK_VALUE)
        r = lax.broadcasted_iota(jnp.int32, (bq, bq), 0)
        c = lax.broadcasted_iota(jnp.int32, (bq, bq), 1)
        s_scr[0:bq, qs:qe] = jnp.where(c <= r, s_scr[0:bq, qs:qe], MASK_VALUE)
    return s_scr[0:bq, 0:n]


def attn_fwd(cfg, qr, kv, kpe, *, name):
    lp, t, mh = cfg.lp, cfg.t, cfg.mh
    blocks = _q_blocks(cfg)

    def body(q_ref, kv_ref, kp_ref, o_ref, l_ref, s_scr):
        for qs, qe in blocks:
            n = qe
            q = q_ref[qs:qe, :]
            k2 = jnp.concatenate([kv_ref[0:n, 0:LANE], kp_ref[0:n, :]], axis=1)
            s = _masked_scores(cfg, q, k2, qs, qe, s_scr)
            m = jnp.max(s, axis=-1, keepdims=True)
            p = jnp.exp(s - m)
            l = jnp.sum(p, axis=-1, keepdims=True)
            o_ref[qs:qe, :] = _nn(p.astype(BF16), kv_ref[0:n, LANE:2 * LANE]) * (1.0 / l)
            l_ref[qs:qe, :] = jnp.broadcast_to(m + jnp.log(l), (qe - qs, LANE))

    hb = pl.BlockSpec((lp, 2 * LANE), lambda b, h: (b, h))
    ob = pl.BlockSpec((lp, LANE), lambda b, h: (b, h))
    return pl.pallas_call(
        body, name=name, grid=(cfg.bsz, mh),
        in_specs=[hb, hb, pl.BlockSpec((lp, LANE), lambda b, h: (b, 0))], out_specs=[ob, ob],
        out_shape=[_sds((t, mh * LANE), F32), _sds((t, mh * LANE), F32)],
        scratch_shapes=[pltpu.VMEM((_max_q_block(cfg), lp), F32)], compiler_params=_cp(),
    )(qr, kv, kpe)


def attn_bwd(cfg, qr, kv, kpe, o, lse, do, *, name):
    lp, t, mh = cfg.lp, cfg.t, cfg.mh
    blocks = _q_blocks(cfg)

    def body(q_ref, kv_ref, kp_ref, o_ref, l_ref, do_ref, dq_ref, dkv_ref, dkp_ref, dk_acc, dv_acc, s_scr):
        dk_acc[...] = jnp.zeros_like(dk_acc)
        dv_acc[...] = jnp.zeros_like(dv_acc)
        for qs, qe in blocks:
            n = qe
            q = q_ref[qs:qe, :]
            k2 = jnp.concatenate([kv_ref[0:n, 0:LANE], kp_ref[0:n, :]], axis=1)
            dov = do_ref[qs:qe, :]
            delta = jnp.sum(dov * o_ref[qs:qe, :], axis=-1, keepdims=True)
            dob = dov.astype(BF16)
            s = _masked_scores(cfg, q, k2, qs, qe, s_scr)
            p = jnp.exp(s - l_ref[qs:qe, 0:1])
            dp = _nt(dob, kv_ref[0:n, LANE:2 * LANE])
            ds = (p * (dp - delta)).astype(BF16)
            dq_ref[qs:qe, :] = _nn(ds, k2)
            dv_acc[0:n, :] += _tn(p.astype(BF16), dob)
            dk_acc[0:n, :] += _tn(ds, q)
        dkv_ref[:, 0:LANE] = dk_acc[:, 0:LANE].astype(BF16)
        dkv_ref[:, LANE:2 * LANE] = dv_acc[...].astype(BF16)
        dkp_ref[0] = dk_acc[:, LANE:2 * LANE]

    hb = pl.BlockSpec((lp, 2 * LANE), lambda b, h: (b, h))
    ob = pl.BlockSpec((lp, LANE), lambda b, h: (b, h))
    return pl.pallas_call(
        body, name=name, grid=(cfg.bsz, mh),
        in_specs=[hb, hb, pl.BlockSpec((lp, LANE), lambda b, h: (b, 0)), ob, ob, ob],
        out_specs=[hb, hb, pl.BlockSpec((1, lp, LANE), lambda b, h: (h, b, 0))],
        out_shape=[_sds((t, cfg.qw), F32), _sds((t, mh * 2 * LANE), BF16), _sds((mh, t, LANE), F32)],
        scratch_shapes=[pltpu.VMEM((lp, 2 * LANE), F32), pltpu.VMEM((lp, LANE), F32),
                        pltpu.VMEM((_max_q_block(cfg), lp), F32)], compiler_params=_cp(),
    )(qr, kv, kpe, o, lse, do)


def _live_rows(cfg, tr, shape):
    rows = pl.program_id(1) * tr + lax.broadcasted_iota(jnp.int32, shape, 0)
    return rows >= cfg.pad


def gate_fwd(cfg, ya, yb, g, *, name):
    d, lp = cfg.d, cfg.lp
    tr = _pick(lp, 544, 16)
    nrb = lp // tr

    def body(ya_ref, yb_ref, ga_ref, gb_ref, o_ref):
        f = lambda ref: ref[...].astype(F32)
        mix = jax.nn.sigmoid(f(ga_ref)) * f(ya_ref) + jax.nn.sigmoid(f(gb_ref)) * f(yb_ref)
        o_ref[...] = jnp.where(_live_rows(cfg, tr, mix.shape), mix, 0.0).astype(BF16)

    row = pl.BlockSpec((tr, d), lambda b, j: (b * nrb + j, 0))
    row1 = pl.BlockSpec((tr, d), lambda b, j: (b * nrb + j, 1))
    return pl.pallas_call(
        body, name=name, grid=(cfg.bsz, nrb), in_specs=[row, row, row, row1], out_specs=row,
        out_shape=_sds((cfg.t, d), BF16), compiler_params=_cp(),
    )(ya, yb, g, g)


def gate_bwd(cfg, dmix, ya, yb, g, *, name):
    d, lp = cfg.d, cfg.lp
    tr = _pick(lp, 544, 16)
    nrb = lp // tr

    def body(dm_ref, ya_ref, yb_ref, ga_ref, gb_ref, dya_ref, dyb_ref, dg_ref):
        dm = dm_ref[...]
        dm = jnp.where(_live_rows(cfg, tr, dm.shape), dm, 0.0)
        sa = jax.nn.sigmoid(ga_ref[...].astype(F32))
        sb = jax.nn.sigmoid(gb_ref[...].astype(F32))
        dya_ref[...] = (dm * sa).astype(BF16)
        dyb_ref[...] = (dm * sb).astype(BF16)
        dg_ref[:, 0:d] = (dm * ya_ref[...].astype(F32) * sa * (1.0 - sa)).astype(BF16)
        dg_ref[:, d:2 * d] = (dm * yb_ref[...].astype(F32) * sb * (1.0 - sb)).astype(BF16)

    row = pl.BlockSpec((tr, d), lambda b, j: (b * nrb + j, 0))
    row1 = pl.BlockSpec((tr, d), lambda b, j: (b * nrb + j, 1))
    row2 = pl.BlockSpec((tr, 2 * d), lambda b, j: (b * nrb + j, 0))
    return pl.pallas_call(
        body, name=name, grid=(cfg.bsz, nrb), in_specs=[row, row, row, row, row1], out_specs=[row, row, row2],
        out_shape=[_sds((cfg.t, d), BF16), _sds((cfg.t, d), BF16), _sds((cfg.t, 2 * d), BF16)], compiler_params=_cp(),
    )(dmix, ya, yb, g, g)


def loss_head(cfg, h, target, w, *, name):
    d, q, nc = cfg.d, cfg.chunk, cfg.nchunks
    tpb = cfg.seq // q

    def body(h_ref, t_ref, w_ref, loss_ref, dh_ref, dw_ref):
        j = pl.program_id(1)

        @pl.when(jnp.logical_and(j == 0, pl.program_id(0) == 0))
        def _():
            loss_ref[...] = jnp.zeros_like(loss_ref)
            dw_ref[...] = jnp.zeros_like(dw_ref)

        @pl.when(j == 0)
        def _():
            dh_ref[...] = jnp.zeros_like(dh_ref)

        @pl.when(j > 0)
        def _():
            xv = h_ref[...]
            r = lax.rsqrt(jnp.mean(xv * xv, axis=-1, keepdims=True) + EPS)
            xh = xv * r
            err = xh * w_ref[...] - t_ref[...]
            loss_ref[...] += 0.5 * jnp.sum(jnp.sum(err * err, axis=-1, keepdims=True) / d, axis=0, keepdims=True)
            dy = err * (1.0 / d)
            g = dy * w_ref[...]
            dh_ref[...] = r * (g - xh * jnp.mean(g * xh, axis=-1, keepdims=True))
            dw_ref[...] += jnp.sum(dy * xh, axis=0, keepdims=True)

    row = pl.BlockSpec((q, d), lambda b, j: (b * nc + j, 0))
    loss, dh, dw = pl.pallas_call(
        body, name=name, grid=(cfg.bsz, nc),
        in_specs=[row, pl.BlockSpec((q, d), lambda b, j: (b * tpb + jnp.maximum(j - 1, 0), 0)),
                  pl.BlockSpec((1, d), lambda b, j: (0, 0))],
        out_specs=[pl.BlockSpec((8, LANE), lambda b, j: (0, 0)), row, pl.BlockSpec((1, d), lambda b, j: (0, 0))],
        out_shape=[_sds((8, LANE), F32), _sds((cfg.t, d), F32), _sds((1, d), F32)], compiler_params=_cp(),
    )(h, target, w.reshape(1, d))
    return loss[0, 0], dh, dw[0]


def _rows_tile(r, c):
    return _pick(r, max(8, (1 << 18) // max(c, 1) // 8 * 8), 8)


def _adam_update(w, g, m, v):
    c1 = 1.0 - ADAM_B1 ** ADAM_STEP
    c2 = 1.0 - ADAM_B2 ** ADAM_STEP
    mn = ADAM_B1 * m + (1.0 - ADAM_B1) * g
    vn = ADAM_B2 * v + (1.0 - ADAM_B2) * (g * g)
    delta = -ADAM_LR * ((mn / c1) / (jnp.sqrt(vn / c2) + ADAM_EPS) + ADAM_WD * w)
    return delta, mn, vn


def adamw_layer(w, m, v, g, li, prev, dep, *, name):
    _, r, c = w.shape
    tr = _rows_tile(r, c)

    def body(*refs):
        w_ref, m_ref, v_ref, g_ref = refs[:4]
        go_ref, d_ref, mo_ref, vo_ref = refs[-4:]
        gv = g_ref[...]
        delta, mn, vn = _adam_update(w_ref[0], gv, m_ref[0], v_ref[0])
        go_ref[0] = gv
        d_ref[0] = delta
        mo_ref[0] = mn
        vo_ref[0] = vn

    if tr * c * 4 >= (1 << 16):
        steps = r // tr
        blk3 = pl.BlockSpec((1, tr, c), lambda i: (li, i, 0))
        blk2 = pl.BlockSpec((tr, c), lambda i: (i, 0))
    else:
        tc = _pick(c, max(LANE, (1 << 18) // r // LANE * LANE), LANE)
        steps = c // tc
        blk3 = pl.BlockSpec((1, r, tc), lambda i: (li, 0, i))
        blk2 = pl.BlockSpec((r, tc), lambda i: (0, i))
    anyspec = pl.BlockSpec(memory_space=pl.ANY)
    in_specs = [blk3, blk3, blk3, blk2, anyspec]
    args = [w, m, v, g, dep]
    aliases = {}
    if prev is not None:
        in_specs += [anyspec] * 4
        args += list(prev)
        aliases = {5 + i: i for i in range(4)}
    return pl.pallas_call(
        body, name=name, grid=(steps,), in_specs=in_specs, out_specs=[blk3] * 4,
        out_shape=[_sds(w.shape, F32)] * 4, input_output_aliases=aliases, compiler_params=_cp(),
    )(*args)


def pair_add(g4, other, half, *, name):
    n, _, r, c = g4.shape
    tr = _rows_tile(r, c)

    def body(h_ref, a_ref, b_ref, o_ref):
        o_ref[0] = (a_ref[0, 0] + b_ref[0]).astype(BF16)

    blk = pl.BlockSpec((1, tr, c), lambda j, i, h: (j, i, 0))
    grid_spec = pltpu.PrefetchScalarGridSpec(
        num_scalar_prefetch=1, grid=(n, r // tr),
        in_specs=[pl.BlockSpec((1, 1, tr, c), lambda j, i, h: (j, h[0], i, 0)), blk], out_specs=blk)
    return pl.pallas_call(body, name=name, grid_spec=grid_spec, out_shape=_sds((n, r, c), BF16),
                          compiler_params=_cp())(half, g4, other)


def chip_sum(recv, part, where, *, name):
    n, r, c = recv.shape
    tr = _rows_tile(r, c)

    def body(s_ref, *refs):
        own_ref, o_ref = refs[n], refs[n + 1]
        acc = None
        for j in range(n):
            term = jnp.where(s_ref[0] == j, own_ref[0], refs[j][0]).astype(F32)
            acc = term if acc is None else acc + term
        o_ref[0] = acc

    def slot(j):
        return pl.BlockSpec((1, tr, c), lambda i, s: (jnp.where(s[0] == j, (j + 1) % n, j), i, 0))

    grid_spec = pltpu.PrefetchScalarGridSpec(
        num_scalar_prefetch=1, grid=(r // tr,),
        in_specs=[slot(j) for j in range(n)] + [pl.BlockSpec((1, tr, c), lambda i, s: (s[0], i, 0))],
        out_specs=pl.BlockSpec((1, tr, c), lambda i, s: (s[1], i, 0)))
    return pl.pallas_call(body, name=name, grid_spec=grid_spec, out_shape=_sds((2, r, c), F32),
                          compiler_params=_cp())(where, *([recv] * n), part)


def _coords():
    return lax.axis_index("x"), lax.axis_index("y"), lax.axis_index("c")


def _other_chips(x, y):
    return [(1 - x, y), (x, 1 - y), (1 - x, 1 - y)]


def gather_chips(arrs, *, name):
    n = len(arrs)
    anyspec = pl.BlockSpec(memory_space=pl.ANY)

    def body(*refs):
        ins, outs = refs[:n], refs[n:2 * n]
        send_sems, recv_sems, local_sems = refs[2 * n:]
        x, y, c = _coords()
        me = 2 * x + y
        chips = _other_chips(x, y)
        copies = []
        for k in range(n):
            loc = pltpu.make_async_copy(ins[k], outs[k].at[me], local_sems.at[k])
            loc.start()
            copies.append(loc)
        sends = []
        for k in range(n):
            for j, (px, py) in enumerate(chips):
                cp = pltpu.make_async_remote_copy(
                    src_ref=ins[k], dst_ref=outs[k].at[me], send_sem=send_sems.at[k, j], recv_sem=recv_sems.at[k, j],
                    device_id=(px, py, c), device_id_type=MESH)
                cp.start()
                sends.append(cp)
        for k in range(n):
            for j, (px, py) in enumerate(chips):
                pltpu.make_async_remote_copy(
                    src_ref=ins[k], dst_ref=outs[k].at[2 * px + py], send_sem=send_sems.at[k, j],
                    recv_sem=recv_sems.at[k, j], device_id=(px, py, c), device_id_type=MESH).wait_recv()
        for cp in sends:
            cp.wait_send()
        for cp in copies:
            cp.wait()

    return pl.pallas_call(
        body, name=name, in_specs=[anyspec] * n, out_specs=[anyspec] * n,
        out_shape=[_sds((4,) + a.shape, a.dtype) for a in arrs],
        scratch_shapes=[pltpu.SemaphoreType.DMA((n, 3)), pltpu.SemaphoreType.DMA((n, 3)), pltpu.SemaphoreType.DMA((n,))],
        compiler_params=_cp(has_side_effects=True),
    )(*arrs)


def allreduce_small(vec, after, *, name):
    r, c = vec.shape

    def body(v_ref, after_ref, o_ref, buf, send_sems, recv_sems):
        x, y, cc = _coords()
        me = 4 * x + 2 * y + cc
        buf[me] = v_ref[...]
        sends = []
        flips = [(fx, fy, fc) for fx in (0, 1) for fy in (0, 1) for fc in (0, 1)][1:]
        for j, (fx, fy, fc) in enumerate(flips):
            peer = ((1 - x) if fx else x, (1 - y) if fy else y, (1 - cc) if fc else cc)
            cp = pltpu.make_async_remote_copy(
                src_ref=v_ref, dst_ref=buf.at[me], send_sem=send_sems.at[j], recv_sem=recv_sems.at[j],
                device_id=peer, device_id_type=MESH)
            cp.start()
            sends.append(cp)
        for j, (fx, fy, fc) in enumerate(flips):
            px, py, pc = ((1 - x) if fx else x, (1 - y) if fy else y, (1 - cc) if fc else cc)
            pltpu.make_async_remote_copy(
                src_ref=v_ref, dst_ref=buf.at[4 * px + 2 * py + pc], send_sem=send_sems.at[j],
                recv_sem=recv_sems.at[j], device_id=(px, py, pc), device_id_type=MESH).wait_recv()
        for cp in sends:
            cp.wait_send()
        acc = buf[0]
        for k in range(1, 8):
            acc = acc + buf[k]
        o_ref[...] = acc

    vm = pl.BlockSpec(memory_space=pltpu.VMEM)
    return pl.pallas_call(
        body, name=name, in_specs=[vm, pl.BlockSpec(memory_space=pl.ANY)], out_specs=vm, out_shape=_sds((r, c), F32),
        scratch_shapes=[pltpu.VMEM((8, r, c), F32), pltpu.SemaphoreType.DMA((7,)), pltpu.SemaphoreType.DMA((7,))],
        compiler_params=_cp(has_side_effects=True),
    )(vec, after)


def pair_exchange(arrs, *, name):
    n = len(arrs)
    anyspec = pl.BlockSpec(memory_space=pl.ANY)

    def body(*refs):
        ins, outs = refs[:n], refs[n:2 * n]
        send_sems, recv_sems = refs[2 * n:]
        x, y, c = _coords()
        sends = []
        for k in range(n):
            for j in range(4):
                cp = pltpu.make_async_remote_copy(
                    src_ref=ins[k].at[j, 1 - c], dst_ref=outs[k].at[j], send_sem=send_sems.at[k, j],
                    recv_sem=recv_sems.at[k, j], device_id=(x, y, 1 - c), device_id_type=MESH)
                cp.start()
                sends.append(cp)
        for cp in sends:
            cp.wait()

    return pl.pallas_call(
        body, name=name, in_specs=[anyspec] * n, out_specs=[anyspec] * n,
        out_shape=[_sds((a.shape[0],) + a.shape[2:], a.dtype) for a in arrs],
        scratch_shapes=[pltpu.SemaphoreType.DMA((n, 4)), pltpu.SemaphoreType.DMA((n, 4))],
        compiler_params=_cp(has_side_effects=True),
    )(*arrs)


def pair_share(lands, *, name):
    n = len(lands)
    anyspec = pl.BlockSpec(memory_space=pl.ANY)

    def body(*refs):
        ins, outs = refs[:n], refs[n:2 * n]
        send_sems, recv_sems = refs[2 * n:]
        x, y, c = _coords()
        sends = []
        for k in range(n):
            for j, (px, py) in enumerate(_other_chips(x, y)):
                cp = pltpu.make_async_remote_copy(
                    src_ref=ins[k].at[2 * px + py, c], dst_ref=outs[k].at[2 * px + py, c], send_sem=send_sems.at[k, j],
                    recv_sem=recv_sems.at[k, j], device_id=(x, y, 1 - c), device_id_type=MESH)
                cp.start()
                sends.append(cp)
        for k in range(n):
            for j, (px, py) in enumerate(_other_chips(x, y)):
                pltpu.make_async_remote_copy(
                    src_ref=ins[k].at[2 * px + py, c], dst_ref=outs[k].at[2 * px + py, 1 - c],
                    send_sem=send_sems.at[k, j], recv_sem=recv_sems.at[k, j], device_id=(x, y, 1 - c),
                    device_id_type=MESH).wait_recv()
        for cp in sends:
            cp.wait_send()

    return pl.pallas_call(
        body, name=name, in_specs=[anyspec] * n, out_specs=[anyspec] * n,
        out_shape=[_sds(a.shape, a.dtype) for a in lands], input_output_aliases={k: k for k in range(n)},
        scratch_shapes=[pltpu.SemaphoreType.DMA((n, 3)), pltpu.SemaphoreType.DMA((n, 3))],
        compiler_params=_cp(has_side_effects=True),
    )(*lands)


def pair_fill(arrs, *, name):
    n = len(arrs)
    anyspec = pl.BlockSpec(memory_space=pl.ANY)

    def body(*refs):
        ins, outs = refs[:n], refs[n:2 * n]
        send_sems, recv_sems = refs[2 * n:]
        x, y, c = _coords()
        sends = []
        for k in range(n):
            cp = pltpu.make_async_remote_copy(
                src_ref=ins[k].at[c], dst_ref=outs[k].at[c], send_sem=send_sems.at[k], recv_sem=recv_sems.at[k],
                device_id=(x, y, 1 - c), device_id_type=MESH)
            cp.start()
            sends.append(cp)
        for k in range(n):
            pltpu.make_async_remote_copy(
                src_ref=ins[k].at[c], dst_ref=outs[k].at[1 - c], send_sem=send_sems.at[k], recv_sem=recv_sems.at[k],
                device_id=(x, y, 1 - c), device_id_type=MESH).wait_recv()
        for cp in sends:
            cp.wait_send()

    return pl.pallas_call(
        body, name=name, in_specs=[anyspec] * n, out_specs=[anyspec] * n,
        out_shape=[_sds(a.shape, a.dtype) for a in arrs], input_output_aliases={k: k for k in range(n)},
        scratch_shapes=[pltpu.SemaphoreType.DMA((n,)), pltpu.SemaphoreType.DMA((n,))],
        compiler_params=_cp(has_side_effects=True),
    )(*arrs)


_HBM = pl.BlockSpec(memory_space=pltpu.HBM)
_SEM = pl.BlockSpec(memory_space=pltpu.SEMAPHORE)


def _ici_copies(kind, srcs, lands, send_sems, recv_sems):
    x, y, c = _coords()
    me = 2 * x + y
    sends, recvs = [], []
    for k in range(len(srcs)):
        for j, (px, py) in enumerate(_other_chips(x, y)):
            peer = 2 * px + py
            if kind == "gather":
                src, there, here = srcs[k].at[c], lands[k].at[me, c], lands[k].at[peer, c]
            else:
                src, there, here = srcs[k].at[peer], lands[k].at[me], lands[k].at[peer]
            sem = 3 * k + j
            mk = functools.partial(pltpu.make_async_remote_copy, src_ref=src, send_sem=send_sems.at[sem],
                                   recv_sem=recv_sems.at[sem], device_id=(px, py, c), device_id_type=MESH)
            sends.append(mk(dst_ref=there))
            recvs.append(mk(dst_ref=here))
    return sends, recvs


def ici_start(kind, srcs, lands, after, *, name):
    n = len(srcs)

    def body(*refs):
        src_refs, land_refs = refs[:n], refs[n:2 * n]
        send_sems, recv_sems = refs[2 * n + 1], refs[2 * n + 2]
        token = refs[-1]
        sends, _ = _ici_copies(kind, src_refs, land_refs, send_sems, recv_sems)
        for cp in sends:
            cp.start()
        token[...] = jnp.zeros_like(token)

    both = list(srcs) + list(lands)
    out = pl.pallas_call(
        body, name=name,
        in_specs=[_HBM] * (2 * n) + [pl.BlockSpec(memory_space=pl.ANY)],
        out_shape=(pltpu.SemaphoreType.DMA((3 * n,)), pltpu.SemaphoreType.DMA((3 * n,)),
                   *[pltpu.HBM(a.shape, a.dtype) for a in both], _sds((8, LANE), F32)),
        out_specs=(_SEM, _SEM, *([_HBM] * (2 * n)), pl.BlockSpec(memory_space=pltpu.VMEM)),
        input_output_aliases={i: 2 + i for i in range(2 * n)},
        compiler_params=_cp(has_side_effects=pltpu.SideEffectType.DATAFLOW_SIDE_EFFECTING),
    )(*[pltpu.with_memory_space_constraint(a, pltpu.HBM) for a in both], after)
    return out[0], out[1], list(out[2:2 + n]), list(out[2 + n:2 + 2 * n]), out[-1]


def ici_wait(kind, started, after, *, name):
    send_sems, recv_sems, srcs, lands, _ = started
    n = len(srcs)

    def body(*refs):
        src_refs, land_refs = refs[:n], refs[n:2 * n]
        sends, recvs = _ici_copies(kind, src_refs, land_refs, refs[2 * n], refs[2 * n + 1])
        for cp in sends:
            cp.wait_send()
        for cp in recvs:
            cp.wait_recv()

    both = list(srcs) + list(lands)
    out = pl.pallas_call(
        body, name=name,
        in_specs=[_HBM] * (2 * n) + [_SEM, _SEM, pl.BlockSpec(memory_space=pl.ANY)],
        out_shape=tuple(pltpu.HBM(a.shape, a.dtype) for a in both), out_specs=tuple([_HBM] * (2 * n)),
        input_output_aliases={i: i for i in range(2 * n)},
        compiler_params=_cp(has_side_effects=pltpu.SideEffectType.DATAFLOW_SIDE_EFFECTING),
    )(*both, send_sems, recv_sems, after)
    return list(out[:n]), list(out[n:])


BIG = ["w_in", "w_uq", "w_ukv", "w_branch_ssm", "w_branch_mla", "w_out", "w_mlp_up", "w_mlp_down"]
COL_SHARDED = {"w_in", "w_uq", "w_ukv", "w_mlp_up"}
SMALL_REPL = ["norm_mix_w", "conv_b", "dt_bias", "a_log", "d_skip", "ssm_norm_w", "q_norm_w", "kv_norm_w", "norm_mlp_w"]


def _unshard_layer(name, g):
    _, r, c = g.shape
    if name in COL_SHARDED:
        return jnp.transpose(g, (1, 0, 2)).reshape(r, 4 * c)
    return g.reshape(4 * r, c)


def _to_shards(name, full):
    r, c = full.shape
    if name in COL_SHARDED:
        return jnp.transpose(full.reshape(r, 4, c // 4), (1, 0, 2))
    return full.reshape(4, r // 4, c)


REST = [k for k in BIG if k != "w_in"]


def prep_layer(cfg, w):
    out = {}
    if "w_in" in w:
        sp = np.cumsum(cfg.in_splits)[:-1].tolist()
        z, xbc, dt, cq, ckv, kr, gs, gm = jnp.split(w["w_in"], sp, axis=1)
        zpad = lambda n: jnp.zeros((cfg.d, n), z.dtype)
        out.update(w_z=z, w_xbc=xbc, w_g=jnp.concatenate([gs, gm], axis=1),
                   w_s=jnp.concatenate([cq, ckv, kr, zpad(LANE - cfg.rope), dt, zpad(LANE - cfg.heads)], axis=1))
    if "w_uq" in w:
        out.update(
            w_uq=jnp.pad(w["w_uq"].reshape(cfg.ql, cfg.mh, cfg.nope + cfg.rope),
                         ((0, 0), (0, 0), (0, 2 * LANE - cfg.nope - cfg.rope))).reshape(cfg.ql, cfg.qw),
            w_ukv=w["w_ukv"], w_bs=w["w_branch_ssm"], w_bm=w["w_branch_mla"], w_out=w["w_out"],
            w_up=w["w_mlp_up"], w_down=w["w_mlp_down"])
    return {k: v.astype(BF16) for k, v in out.items()}


def unprep_grads(cfg, g):
    out = {}
    if "w_s" in g:
        ql, kvl = cfg.ql, cfg.kvl
        ds_ = g["w_s"]
        cq, ckv = ds_[:, :ql], ds_[:, ql:ql + kvl]
        kr = ds_[:, ql + kvl:ql + kvl + cfg.rope]
        dt = ds_[:, ql + kvl + LANE:ql + kvl + LANE + cfg.heads]
        out["w_in"] = jnp.concatenate([g["w_z"], g["w_xbc"], dt, cq, ckv, kr, g["w_g"]], axis=1)
    if "w_uq" in g:
        out.update(
            w_uq=g["w_uq"].reshape(cfg.ql, cfg.mh, 2 * LANE)[:, :, :cfg.nope + cfg.rope].reshape(cfg.ql, -1),
            w_ukv=g["w_ukv"], w_branch_ssm=g["w_bs"], w_branch_mla=g["w_bm"],
            w_out=g["w_out"], w_mlp_up=g["w_up"], w_mlp_down=g["w_down"])
    return out


def layer_fwd(cfg, h, pw, sm, tabs, li, rest=None):
    n = lambda s: f"l{li}_{s}"
    u = rmsnorm_fwd(h, sm["norm_mix_w"], name=n("norm_mix"))
    z = matmul(u, pw["w_z"], out_dtype=BF16, name=n("in_z"))
    xbc = matmul(u, pw["w_xbc"], name=n("in_xbc"))
    g = matmul(u, pw["w_g"], out_dtype=BF16, name=n("in_g"))
    small = matmul(u, pw["w_s"], name=n("in_s"), tn=cfg.sw)
    xc = conv_fwd(cfg, xbc, sm["conv_w"], sm["conv_b"], name=n("conv"))
    y, sin = ssd_fwd(cfg, xc, small, sm["dt_bias_p"], sm["avec"], sm["dexp"], name=n("ssd"))
    y_ssm = tail_fwd(cfg, y, z, sm["ssm_norm_w"], name=n("tail"))
    if rest is not None:
        pw = dict(pw, **rest(y_ssm))
    cqn = rmsnorm_fwd(small, sm["q_norm_w"], cw=cfg.ql, ci=0, name=n("q_norm"))
    ckvn = rmsnorm_fwd(small, sm["kv_norm_w"], cw=cfg.kvl, ci=cfg.ql // cfg.kvl, name=n("kv_norm"))
    qf = matmul(cqn, pw["w_uq"], name=n("uq"))
    kv = matmul(ckvn, pw["w_ukv"], out_dtype=BF16, name=n("ukv"))
    qr, kpe = rope_fwd(cfg, qf, small, tabs, name=n("rope"))
    o, lse = attn_fwd(cfg, qr, kv, kpe, name=n("attn"))
    ya = matmul(y_ssm, pw["w_bs"], out_dtype=BF16, name=n("branch_ssm"))
    yb = matmul(o, pw["w_bm"], out_dtype=BF16, name=n("branch_mla"))
    mixed = gate_fwd(cfg, ya, yb, g, name=n("gate"))
    h1 = matmul(mixed, pw["w_out"], add=h, name=n("out"))
    v = rmsnorm_fwd(h1, sm["norm_mlp_w"], name=n("norm_mlp"))
    a, act = matmul(v, pw["w_up"], name=n("up"), epilogue=_ep_relu2, out_dtypes=(BF16, BF16))
    h2 = matmul(act, pw["w_down"], add=h1, name=n("down"))
    saved = dict(h=h, u=u, z=z, xbc=xbc, g=g, small=small, xc=xc, y=y, sin=sin, y_ssm=y_ssm, cqn=cqn, ckvn=ckvn,
                 qr=qr, kv=kv, kpe=kpe, o=o, lse=lse, ya=ya, yb=yb, mixed=mixed, h1=h1, v=v, a=a, act=act)
    return h2, saved, pw


def layer_bwd(cfg, dh2, pw, sm, tabs, s, li, early=None):
    n = lambda t: f"l{li}_b_{t}"
    gw, gs = {}, {}
    gw["w_down"] = matmul(s["act"], dh2, ta=True, name=n("dw_down"))
    da = matmul(dh2, pw["w_down"], tb=True, name=n("dact"), epilogue=_ep_relu2_grad, extras=(s["a"],),
                out_dtypes=(BF16,))
    gw["w_up"] = matmul(s["v"], da, ta=True, name=n("dw_up"))
    dv = matmul(da, pw["w_up"], tb=True, name=n("dv"))
    dh1, gs["norm_mlp_w"] = rmsnorm_bwd(dv, s["h1"], sm["norm_mlp_w"], res=dh2, name=n("norm_mlp"))
    gw["w_out"] = matmul(s["mixed"], dh1, ta=True, name=n("dw_out"))
    dmix = matmul(dh1, pw["w_out"], tb=True, name=n("dmix"))
    dya, dyb, dg = gate_bwd(cfg, dmix, s["ya"], s["yb"], s["g"], name=n("gate"))
    gw["w_bs"] = matmul(s["y_ssm"], dya, ta=True, name=n("dw_bs"))
    gw["w_bm"] = matmul(s["o"], dyb, ta=True, name=n("dw_bm"))
    dy_ssm = matmul(dya, pw["w_bs"], tb=True, name=n("dy_ssm"))
    do = matmul(dyb, pw["w_bm"], tb=True, name=n("do"))
    dq, dkv, dkpe = attn_bwd(cfg, s["qr"], s["kv"], s["kpe"], s["o"], s["lse"], do, name=n("attn"))
    dqf, dkr = rope_bwd(cfg, dq, dkpe, tabs, name=n("rope"))
    gw["w_uq"] = matmul(s["cqn"], dqf, ta=True, name=n("dw_uq"))
    gw["w_ukv"] = matmul(s["ckvn"], dkv, ta=True, name=n("dw_ukv"))
    dcqn = matmul(dqf, pw["w_uq"], tb=True, name=n("dcqn"))
    dckvn = matmul(dkv, pw["w_ukv"], tb=True, name=n("dckvn"))
    dcq, gs["q_norm_w"] = rmsnorm_bwd(dcqn, s["small"], sm["q_norm_w"], cw=cfg.ql, ci=0, out_dtype=BF16, name=n("q_norm"))
    dckv, gs["kv_norm_w"] = rmsnorm_bwd(dckvn, s["small"], sm["kv_norm_w"], cw=cfg.kvl, ci=cfg.ql // cfg.kvl,
                                        out_dtype=BF16, name=n("kv_norm"))
    ssm_norm_w = sm["ssm_norm_w"]
    if early is not None:
        ssm_norm_w = ssm_norm_w + early(dict(gw))[0, 0]
    dy, dz, gs["ssm_norm_w"] = tail_bwd(cfg, dy_ssm, s["y"], s["z"], ssm_norm_w, name=n("tail"))
    dxc, ddt, ddexp, dav, dbias = ssd_bwd(cfg, s["xc"], s["small"], sm["dt_bias_p"], sm["avec"], sm["dexp"],
                                          s["sin"], dy, name=n("ssd"))
    dxbc, gs["conv_w"], gs["conv_b"] = conv_bwd(cfg, s["xbc"], sm["conv_w"], sm["conv_b"], dxc, name=n("conv"))
    gs["d_skip"] = ddexp.reshape(cfg.heads, cfg.hd).sum(axis=1)
    gs["a_log"] = (dav[0] * sm["avec"][0])[:cfg.heads]
    gs["dt_bias"] = dbias[0, :cfg.heads]
    dsmall = jnp.concatenate([dcq, dckv, dkr.astype(BF16), ddt.astype(BF16)], axis=1)
    gw["w_z"] = matmul(s["u"], dz, ta=True, name=n("dw_z"))
    gw["w_xbc"] = matmul(s["u"], dxbc, ta=True, name=n("dw_xbc"))
    gw["w_g"] = matmul(s["u"], dg, ta=True, name=n("dw_g"))
    gw["w_s"] = matmul(s["u"], dsmall, ta=True, name=n("dw_s"))
    du = matmul(dz, pw["w_z"], tb=True, name=n("du_z"))
    du = matmul(dxbc, pw["w_xbc"], tb=True, add=du, name=n("du_xbc"))
    du = matmul(dg, pw["w_g"], tb=True, add=du, name=n("du_g"))
    du = matmul(dsmall, pw["w_s"], tb=True, add=du, name=n("du_s"))
    dh, gs["norm_mix_w"] = rmsnorm_bwd(du, s["h"], sm["norm_mix_w"], res=dh1, name=n("norm_mix"))
    return dh, gw, gs


def small_params(cfg, p, li):
    pad_l = lambda v: jnp.pad(v, (0, LANE - v.shape[0])).reshape(1, LANE)
    return dict(
        norm_mix_w=p["norm_mix_w"][li], conv_w=p["conv_w"][li], conv_b=p["conv_b"][li],
        dt_bias_p=pad_l(p["dt_bias"][li]), avec=pad_l(-jnp.exp(p["a_log"][li])),
        dexp=jnp.repeat(p["d_skip"][li], cfg.hd).reshape(1, cfg.inner),
        ssm_norm_w=p["ssm_norm_w"][li], q_norm_w=p["q_norm_w"][li], kv_norm_w=p["kv_norm_w"][li],
        norm_mlp_w=p["norm_mlp_w"][li])


def local_step(cfg, x, target, p, depth=2):
    bsz, d = cfg.bsz, cfg.d
    lead = jnp.zeros((bsz, cfg.pad, d), F32)
    meta = jnp.broadcast_to(p["meta_tokens"][None], (bsz, cfg.n_meta, d))
    h = jnp.concatenate([lead, meta, x], axis=1).reshape(cfg.t, d)
    tabs = rope_tables(cfg)
    saved, sms = [], []
    for li in range(depth):
        sm = small_params(cfg, p, li)
        h, s, _ = layer_fwd(cfg, h, p["pw"][li], sm, tabs, li)
        saved.append(s)
        sms.append(sm)
    loss, dh, dfw = loss_head(cfg, h, target.reshape(bsz * cfg.seq, d), p["final_norm_w"], name="loss_head")
    gws, gss = [None] * depth, [None] * depth
    for li in reversed(range(depth)):
        dh, gws[li], gss[li] = layer_bwd(cfg, dh, p["pw"][li], sms[li], tabs, saved[li], li)
    dh = dh.reshape(bsz, cfg.lp, d)
    grad_x = dh[:, cfg.chunk:, :]
    gmeta = jnp.sum(dh[:, cfg.pad:cfg.chunk, :], axis=0)
    return loss, grad_x, gmeta, gws, gss, dfw


def _pack_small(parts):
    flat = jnp.concatenate([a.reshape(-1) for a in parts])
    n = flat.shape[0]
    npad = -n % (8 * LANE)
    return jnp.pad(flat, (0, npad)).reshape(-1, LANE), n


def _unpack_small(vec, shapes):
    flat = vec.reshape(-1)
    out, off = [], 0
    for sh in shapes:
        sz = int(np.prod(sh))
        out.append(flat[off:off + sz].reshape(sh))
        off += sz
    return out


def _as2d(a):
    return a.reshape(-1, a.shape[-1])


def kernel(x, meta_tokens, norm_mix_w, w_in, conv_w, conv_b, dt_bias, a_log, d_skip, ssm_norm_w, q_norm_w, kv_norm_w, w_uq, w_ukv, w_branch_ssm, w_branch_mla, w_out, norm_mlp_w, w_mlp_up, w_mlp_down, final_norm_w, loss_target, m_meta_tokens, m_norm_mix_w, m_w_in, m_conv_w, m_conv_b, m_dt_bias, m_a_log, m_d_skip, m_ssm_norm_w, m_q_norm_w, m_kv_norm_w, m_w_uq, m_w_ukv, m_w_branch_ssm, m_w_branch_mla, m_w_out, m_norm_mlp_w, m_w_mlp_up, m_w_mlp_down, m_final_norm_w, v_meta_tokens, v_norm_mix_w, v_w_in, v_conv_w, v_conv_b, v_dt_bias, v_a_log, v_d_skip, v_ssm_norm_w, v_q_norm_w, v_kv_norm_w, v_w_uq, v_w_ukv, v_w_branch_ssm, v_w_branch_mla, v_w_out, v_norm_mlp_w, v_w_mlp_up, v_w_mlp_down, v_final_norm_w):
    cfg = CFG
    names = ["meta_tokens", "norm_mix_w", "w_in", "conv_w", "conv_b", "dt_bias", "a_log", "d_skip", "ssm_norm_w",
             "q_norm_w", "kv_norm_w", "w_uq", "w_ukv", "w_branch_ssm", "w_branch_mla", "w_out", "norm_mlp_w",
             "w_mlp_up", "w_mlp_down", "final_norm_w"]
    wts = dict(zip(names, [meta_tokens, norm_mix_w, w_in, conv_w, conv_b, dt_bias, a_log, d_skip, ssm_norm_w,
                           q_norm_w, kv_norm_w, w_uq, w_ukv, w_branch_ssm, w_branch_mla, w_out, norm_mlp_w,
                           w_mlp_up, w_mlp_down, final_norm_w]))
    ms = dict(zip(names, [m_meta_tokens, m_norm_mix_w, m_w_in, m_conv_w, m_conv_b, m_dt_bias, m_a_log, m_d_skip,
                          m_ssm_norm_w, m_q_norm_w, m_kv_norm_w, m_w_uq, m_w_ukv, m_w_branch_ssm, m_w_branch_mla,
                          m_w_out, m_norm_mlp_w, m_w_mlp_up, m_w_mlp_down, m_final_norm_w]))
    vs = dict(zip(names, [v_meta_tokens, v_norm_mix_w, v_w_in, v_conv_w, v_conv_b, v_dt_bias, v_a_log, v_d_skip,
                          v_ssm_norm_w, v_q_norm_w, v_kv_norm_w, v_w_uq, v_w_ukv, v_w_branch_ssm, v_w_branch_mla,
                          v_w_out, v_norm_mlp_w, v_w_mlp_up, v_w_mlp_down, v_final_norm_w]))
    cx, cy, cc = _coords()
    chip = 2 * cx + cy

    half1 = jnp.reshape(cc, (1,)).astype(jnp.int32)
    where2 = jnp.stack([chip, cc]).astype(jnp.int32)
    wb = {k: wts[k].astype(BF16) for k in BIG}
    zero_tok = jnp.zeros((8, LANE), F32)

    def halves(a):
        return a.reshape((2, a.shape[0] // 2) + a.shape[1:])

    def gather_start(li, keys, tag, after):
        srcs = [halves(wb[k][li]) for k in keys]
        lands = [lax.empty((4,) + s.shape, BF16) for s in srcs]
        return ici_start("gather", srcs, lands, after, name=f"gather{li}{tag}_start")

    def gather_finish(li, keys, tag, started, after):
        srcs, lands = ici_wait("gather", started, after, name=f"gather{li}{tag}_wait")
        lands = pair_share(lands, name=f"gather{li}{tag}_share")
        full = {}
        for k, own, land in zip(keys, srcs, lands):
            slots = [jnp.where(chip == j, own, land[j]) for j in range(4)]
            full[k] = _unshard_layer(k, jnp.stack(slots).reshape((4, 2 * own.shape[1], own.shape[2])))
        return prep_layer(cfg, full)

    def reduce_start(li, keys, tag, gw, after):
        ug = unprep_grads(cfg, gw)
        g4 = []
        for k in keys:
            s = _to_shards(k, ug[k])
            g4.append(s.reshape(4, 2, s.shape[1] // 2, s.shape[2]))
        theirs = pair_exchange(g4, name=f"grad{li}{tag}_pair_exchange")
        parts = [pair_add(a, b, half1, name=f"grad{li}_pair_add_{k}") for k, a, b in zip(keys, g4, theirs)]
        lands = [lax.empty(q.shape, q.dtype) for q in parts]
        return ici_start("scatter", parts, lands, after, name=f"grad{li}{tag}_scatter_start")

    def reduce_finish(li, keys, tag, started, after):
        parts, lands = ici_wait("scatter", started, after, name=f"grad{li}{tag}_scatter_wait")
        sums = [chip_sum(rc, pt, where2, name=f"grad{li}_chip_sum_{k}") for k, rc, pt in zip(keys, lands, parts)]
        sums = pair_fill(sums, name=f"grad{li}{tag}_pair_fill")
        return {k: s.reshape(2 * s.shape[1], s.shape[2]) for k, s in zip(keys, sums)}

    gathered = gather_chips([meta_tokens, conv_w], name="gather_small")
    p = dict(wts)
    p["meta_tokens"] = jnp.transpose(gathered[0], (1, 0, 2)).reshape(cfg.n_meta, cfg.d)
    p["conv_w"] = jnp.transpose(gathered[1], (1, 2, 0, 3)).reshape(2, cfg.convk, cfg.conv_dim)

    st0a = gather_start(0, ["w_in"], "a", gathered[0])
    st0b = gather_start(0, REST, "b", st0a[4])
    st1 = gather_start(1, BIG, "", st0b[4])
    pw0 = gather_finish(0, ["w_in"], "a", st0a, st1[4])

    bsz, d = cfg.bsz, cfg.d
    lead = jnp.zeros((bsz, cfg.pad, d), F32)
    meta = jnp.broadcast_to(p["meta_tokens"][None], (bsz, cfg.n_meta, d))
    h0 = jnp.concatenate([lead, meta, x], axis=1).reshape(cfg.t, d)
    tabs = rope_tables(cfg)
    sm0 = small_params(cfg, p, 0)
    h1, sv0, pw0 = layer_fwd(cfg, h0, pw0, sm0, tabs, 0,
                             rest=lambda after: gather_finish(0, REST, "b", st0b, after))
    pw1 = gather_finish(1, BIG, "", st1, h1)
    sm1 = small_params(cfg, p, 1)
    h2, sv1, _ = layer_fwd(cfg, h1, pw1, sm1, tabs, 1)
    loss, dh, dfw = loss_head(cfg, h2, loss_target.reshape(bsz * cfg.seq, d), final_norm_w, name="loss_head")
    loss = lax.psum(loss, ("x", "y", "c"))

    dh, gw1, gs1 = layer_bwd(cfg, dh, pw1, sm1, tabs, sv1, 1)
    red1 = reduce_start(1, BIG, "", gw1, zero_tok)
    sm0b = dict(sm0)
    sm0b["norm_mlp_w"] = sm0["norm_mlp_w"] + red1[4][0, 0]
    early = {}

    def start_early(gw):
        early["st"] = reduce_start(0, REST, "e", gw, zero_tok)
        return early["st"][4]

    dh, gw0, gs0 = layer_bwd(cfg, dh, pw0, sm0b, tabs, sv0, 0, early=start_early)
    dh3 = dh.reshape(bsz, cfg.lp, d)
    grad_x = dh3[:, cfg.chunk:, :]
    gmeta = jnp.sum(dh3[:, cfg.pad:cfg.chunk, :], axis=0)
    big1 = reduce_finish(1, BIG, "", red1, dh)

    small_names = SMALL_REPL + ["conv_w"]
    parts = [jnp.stack([gs0[k], gs1[k]]) for k in small_names] + [dfw, gmeta]
    shapes = [a.shape for a in parts]
    vec, _ = _pack_small(parts)
    red_vec = allreduce_small(vec, big1[BIG[-1]], name="allreduce_small")
    red = _unpack_small(red_vec, shapes)
    sg = dict(zip(small_names + ["final_norm_w", "meta_tokens"], red))
    sg["conv_w"] = lax.dynamic_slice_in_dim(sg["conv_w"], chip * (cfg.conv_dim // 4), cfg.conv_dim // 4, axis=2)
    sg["meta_tokens"] = lax.dynamic_slice_in_dim(sg["meta_tokens"], chip * (cfg.d // 4), cfg.d // 4, axis=1)

    red0 = reduce_start(0, ["w_in"], "l", gw0, red_vec)
    grads, deltas, new_m, new_v = {}, {}, {}, {}
    dep = red0[4]
    for k in names:
        if k in BIG:
            continue
        w2, g2, m2, v2 = _as2d(wts[k]), _as2d(sg[k]), _as2d(ms[k]), _as2d(vs[k])
        dl, mn, vn = adamw_small(w2, g2, m2, v2, dep, name=f"adamw_{k}")
        grads[k] = sg[k].reshape(wts[k].shape)
        deltas[k], new_m[k], new_v[k] = (t.reshape(wts[k].shape) for t in (dl, mn, vn))

    def view(k, a):
        return jnp.swapaxes(a, 1, 2) if k == "w_in" else a

    def gview(k, g):
        return g.T if k == "w_in" else g

    wv, mv, vv = ({k: view(k, t[k]) for k in BIG} for t in (wts, ms, vs))
    outs = {}
    for k in BIG:
        outs[k] = adamw_layer(wv[k], mv[k], vv[k], gview(k, big1[k]), 1, None, dep, name=f"adamw1_{k}")
        dep = outs[k][1]
    big0 = reduce_finish(0, REST, "e", early["st"], dep)
    for k in REST:
        outs[k] = adamw_layer(wv[k], mv[k], vv[k], big0[k], 0, outs[k], dep, name=f"adamw0_{k}")
        dep = outs[k][1]
    big0.update(reduce_finish(0, ["w_in"], "l", red0, dep))
    outs["w_in"] = adamw_layer(wv["w_in"], mv["w_in"], vv["w_in"], gview("w_in", big0["w_in"]), 0, outs["w_in"], dep,
                               name="adamw0_w_in")
    for k in BIG:
        grads[k], deltas[k], new_m[k], new_v[k] = (view(k, t) for t in outs[k])
    return (loss, grad_x, *[grads[k] for k in names], *[deltas[k] for k in names],
            *[new_m[k] for k in names], *[new_v[k] for k in names])


def adamw_small(w, g, m, v, dep, *, name):
    def body(w_ref, g_ref, m_ref, v_ref, dep_ref, d_ref, mo_ref, vo_ref):
        d_ref[...], mo_ref[...], vo_ref[...] = _adam_update(w_ref[...], g_ref[...], m_ref[...], v_ref[...])

    vm = pl.BlockSpec(memory_space=pltpu.VMEM)
    return pl.pallas_call(body, name=name, in_specs=[vm] * 4 + [pl.BlockSpec(memory_space=pl.ANY)], out_specs=[vm] * 3,
                          out_shape=[_sds(w.shape, F32)] * 3, compiler_params=_cp())(w, g, m, v, dep)
```

```python
import functools
import math
from typing import NamedTuple

import numpy as np
import jax
import jax.numpy as jnp
from jax import lax
from jax.experimental import pallas as pl
from jax.experimental.pallas import tpu as pltpu

F32 = jnp.float32
BF16 = jnp.bfloat16
HI = lax.Precision.HIGHEST
EPS = 1e-6
ROPE_THETA = 10000.0
LANE = 128
VMEM_LIMIT = 56 * 1024 * 1024
MASK_VALUE = -1e30
ADAM_LR, ADAM_B1, ADAM_B2, ADAM_EPS, ADAM_WD, ADAM_STEP = 0.001, 0.9, 0.999, 1e-08, 0.01, 10
MESH = pl.DeviceIdType.MESH


class Cfg(NamedTuple):
    d: int = 1024
    seq: int = 2048
    bsz: int = 2
    n_meta: int = 16
    inner: int = 2048
    hd: int = 64
    groups: int = 4
    state: int = 128
    convk: int = 4
    chunk: int = 128
    mh: int = 8
    ql: int = 512
    kvl: int = 256
    nope: int = 128
    rope: int = 64
    vd: int = 128
    ff: int = 4096

    @property
    def heads(self): return self.inner // self.hd
    @property
    def gw(self): return self.inner // self.groups
    @property
    def conv_dim(self): return self.inner + 2 * self.groups * self.state
    @property
    def pad(self): return self.chunk - self.n_meta
    @property
    def lp(self): return self.chunk + self.seq
    @property
    def t(self): return self.bsz * self.lp
    @property
    def nchunks(self): return self.lp // self.chunk
    @property
    def sw(self): return self.ql + self.kvl + 2 * LANE
    @property
    def kt(self): return (self.ql + self.kvl) // LANE
    @property
    def dtt(self): return self.kt + 1
    @property
    def qw(self): return self.mh * 2 * LANE
    @property
    def in_splits(self):
        return [self.inner, self.conv_dim, self.heads, self.ql, self.kvl, self.rope, self.d, self.d]


CFG = Cfg()


def _pick(dim, pref, mult):
    best = None
    for t in range(mult, min(dim, pref) + 1, mult):
        if dim % t == 0:
            best = t
    return best if best is not None else dim


def _cp(**kw):
    return pltpu.CompilerParams(vmem_limit_bytes=VMEM_LIMIT, **kw)


def _sds(shape, dtype):
    return jax.ShapeDtypeStruct(tuple(shape), dtype)


def _silu(x):
    return x * jax.nn.sigmoid(x)


def _dsilu(x):
    s = jax.nn.sigmoid(x)
    return s * (1.0 + x * (1.0 - s))


def _ep_plain(r):
    return (r,)


def _ep_add(r, res):
    return (r + res.astype(F32),)


def _ep_relu2(r):
    rp = jnp.maximum(r, 0.0)
    return r, rp * rp


def _ep_relu2_grad(r, a):
    return (r * (2.0 * jnp.maximum(a.astype(F32), 0.0)),)


MM_VMEM_BUDGET = 44 * 1024 * 1024


def _mm_tiles(m, n, k, a_bytes, b_bytes, io_bytes, ta):
    m_mult, m_cap = (LANE, 1024) if ta else (16, 1088)
    tms = [t for t in range(m_cap, 0, -m_mult) if m % t == 0] or [m]
    tns = [t for t in (1024, 512, 256, 128) if n % t == 0] or [n]
    best = None
    for tm in tms:
        for tn in tns:
            need = 2 * (tm * k * a_bytes + k * tn * b_bytes + tm * tn * io_bytes)
            if need <= MM_VMEM_BUDGET and (best is None or tm * tn > best[0] * best[1]):
                best = (tm, tn)
    if best is None:
        return (_pick(m, 512, m_mult), _pick(n, 512, LANE), _pick(k, 1088 if ta else 1024, 16 if ta else LANE))
    return best[0], best[1], k


def matmul(a, b, *, ta=False, tb=False, out_dtype=F32, add=None, name, tm=None, tn=None, tk=None,
           epilogue=None, extras=(), out_dtypes=None):
    if add is not None:
        epilogue, extras = _ep_add, (add,)
    if epilogue is None:
        epilogue = _ep_plain
    out_dtypes = tuple(out_dtypes) if out_dtypes is not None else (out_dtype,)
    n_ex, n_out = len(extras), len(out_dtypes)
    if ta:
        k_dim, m_dim = a.shape
    else:
        m_dim, k_dim = a.shape
    if tb:
        n_dim, k2 = b.shape
    else:
        k2, n_dim = b.shape
    assert k_dim == k2, (a.shape, b.shape, ta, tb)
    if tm is None and tn is None and tk is None:
        io_bytes = sum(jnp.dtype(e.dtype).itemsize for e in extras) + sum(jnp.dtype(d).itemsize for d in out_dtypes)
        tm, tn, tk = _mm_tiles(m_dim, n_dim, k_dim, jnp.dtype(a.dtype).itemsize, jnp.dtype(b.dtype).itemsize,
                               io_bytes, ta)
    elif ta:
        tm = tm or _pick(m_dim, 1024, LANE)
        tk = tk or _pick(k_dim, 1088, 16)
        tn = tn or _pick(n_dim, 1024, LANE)
    else:
        tm = tm or _pick(m_dim, 1088, 16)
        tk = tk or _pick(k_dim, 1024 if a.dtype == F32 else 2048, LANE)
        tn = tn or _pick(n_dim, 1024, LANE)
    nm, nn, nk = m_dim // tm, n_dim // tn, k_dim // tk
    dn = (((0 if ta else 1,), (1 if tb else 0,)), ((), ()))

    def body(*refs):
        a_ref, b_ref = refs[:2]
        ex_refs = refs[2:2 + n_ex]
        o_refs = refs[2 + n_ex:2 + n_ex + n_out]
        scr = refs[2 + n_ex + n_out:]
        p = lax.dot_general(a_ref[...].astype(BF16), b_ref[...].astype(BF16), dn, preferred_element_type=F32)

        def finish(r):
            outs = epilogue(r, *[e[...] for e in ex_refs])
            for o_ref, val, dt in zip(o_refs, outs, out_dtypes):
                o_ref[...] = val.astype(dt)

        if nk == 1:
            finish(p)
        else:
            acc = scr[0]
            k = pl.program_id(2)

            @pl.when(k == 0)
            def _():
                acc[...] = p

            @pl.when(k > 0)
            def _():
                acc[...] += p

            @pl.when(k == nk - 1)
            def _():
                finish(acc[...])

    a_spec = pl.BlockSpec((tk, tm), lambda i, j, k: (k, i)) if ta else pl.BlockSpec((tm, tk), lambda i, j, k: (i, k))
    b_spec = pl.BlockSpec((tn, tk), lambda i, j, k: (j, k)) if tb else pl.BlockSpec((tk, tn), lambda i, j, k: (k, j))
    o_spec = pl.BlockSpec((tm, tn), lambda i, j, k: (i, j))
    outs = pl.pallas_call(
        body, name=name, grid=(nm, nn, nk), in_specs=[a_spec, b_spec] + [o_spec] * n_ex, out_specs=[o_spec] * n_out,
        out_shape=[_sds((m_dim, n_dim), dt) for dt in out_dtypes],
        scratch_shapes=[pltpu.VMEM((tm, tn), F32)] if nk > 1 else [],
        compiler_params=_cp(dimension_semantics=("parallel", "parallel", "arbitrary")),
    )(a, b, *extras)
    return outs[0] if n_out == 1 else tuple(outs)


def rmsnorm_fwd(x, w, *, cw=None, ci=0, name):
    t = x.shape[0]
    cw = cw or x.shape[1]
    tr = _pick(t, 544, 16)

    def body(x_ref, w_ref, o_ref):
        xv = x_ref[...].astype(F32)
        r = lax.rsqrt(jnp.mean(xv * xv, axis=-1, keepdims=True) + EPS)
        o_ref[...] = (xv * r * w_ref[...]).astype(BF16)

    return pl.pallas_call(
        body, name=name, grid=(t // tr,),
        in_specs=[pl.BlockSpec((tr, cw), lambda i: (i, ci)), pl.BlockSpec((1, cw), lambda i: (0, 0))],
        out_specs=pl.BlockSpec((tr, cw), lambda i: (i, 0)),
        out_shape=_sds((t, cw), BF16), compiler_params=_cp(),
    )(x, w.reshape(1, cw))


def rmsnorm_bwd(dy, x, w, *, cw=None, ci=0, res=None, out_dtype=F32, name):
    t = x.shape[0]
    cw = cw or x.shape[1]
    tr = _pick(t, 544, 16)
    has_res = res is not None

    def body(*refs):
        if has_res:
            dy_ref, x_ref, w_ref, res_ref, dx_ref, dw_ref = refs
        else:
            dy_ref, x_ref, w_ref, dx_ref, dw_ref = refs
        xv = x_ref[...].astype(F32)
        dyv = dy_ref[...].astype(F32)
        r = lax.rsqrt(jnp.mean(xv * xv, axis=-1, keepdims=True) + EPS)
        xh = xv * r
        g = dyv * w_ref[...]
        dx = r * (g - xh * jnp.mean(g * xh, axis=-1, keepdims=True))
        if has_res:
            dx = dx + res_ref[...]
        dx_ref[...] = dx.astype(out_dtype)

        @pl.when(pl.program_id(0) == 0)
        def _():
            dw_ref[...] = jnp.zeros_like(dw_ref)

        dw_ref[...] += jnp.sum(dyv * xh, axis=0, keepdims=True)

    row = pl.BlockSpec((tr, cw), lambda i: (i, 0))
    in_specs = [row, pl.BlockSpec((tr, cw), lambda i: (i, ci)), pl.BlockSpec((1, cw), lambda i: (0, 0))]
    args = [dy, x, w.reshape(1, cw)]
    if has_res:
        in_specs.append(row)
        args.append(res)
    dx, dw = pl.pallas_call(
        body, name=name, grid=(t // tr,), in_specs=in_specs,
        out_specs=[row, pl.BlockSpec((1, cw), lambda i: (0, 0))],
        out_shape=[_sds((t, cw), out_dtype), _sds((1, cw), F32)], compiler_params=_cp(),
    )(*args)
    return dx, dw[0]


def _shift_down(x, s):
    return x if s == 0 else pltpu.roll(x, s, 0)


def _shift_up(x, s):
    return x if s == 0 else pltpu.roll(x, x.shape[0] - s, 0)


def _conv_pre(x, w_ref, b_ref, kk):
    pre = b_ref[...] + jnp.zeros_like(x)
    for k in range(kk):
        pre = pre + w_ref[k:k + 1, :] * _shift_down(x, kk - 1 - k)
    return pre


def conv_fwd(cfg, xbc, w, b, *, name):
    lp, cd, kk = cfg.lp, cfg.conv_dim, cfg.convk
    assert cfg.pad >= kk - 1
    cb = _pick(cd, 512, LANE)

    def body(x_ref, w_ref, b_ref, o_ref):
        o_ref[...] = _silu(_conv_pre(x_ref[...], w_ref, b_ref, kk))

    blk = pl.BlockSpec((lp, cb), lambda j, bb: (bb, j))
    return pl.pallas_call(
        body, name=name, grid=(cd // cb, cfg.bsz),
        in_specs=[blk, pl.BlockSpec((kk, cb), lambda j, bb: (0, j)), pl.BlockSpec((1, cb), lambda j, bb: (0, j))],
        out_specs=blk, out_shape=_sds((cfg.t, cd), F32), compiler_params=_cp(),
    )(xbc, w, b.reshape(1, cd))


def conv_bwd(cfg, xbc, w, b, dxc, *, name):
    lp, cd, kk = cfg.lp, cfg.conv_dim, cfg.convk
    cb = _pick(cd, 512, LANE)

    def body(x_ref, w_ref, b_ref, d_ref, dx_ref, dw_ref, db_ref):
        x = x_ref[...]
        pre = _conv_pre(x, w_ref, b_ref, kk)
        dpre = d_ref[...] * _dsilu(pre)
        dx = jnp.zeros_like(x)
        dws = []
        for k in range(kk):
            s = kk - 1 - k
            dx = dx + w_ref[k:k + 1, :] * _shift_up(dpre, s)
            dws.append(jnp.sum(dpre * _shift_down(x, s), axis=0, keepdims=True))
        dx_ref[...] = dx.astype(BF16)

        @pl.when(pl.program_id(1) == 0)
        def _():
            dw_ref[...] = jnp.zeros_like(dw_ref)
            db_ref[...] = jnp.zeros_like(db_ref)

        for k in range(kk):
            dw_ref[k:k + 1, :] += dws[k]
        db_ref[...] += jnp.sum(dpre, axis=0, keepdims=True)

    blk = pl.BlockSpec((lp, cb), lambda j, bb: (bb, j))
    wsp = pl.BlockSpec((kk, cb), lambda j, bb: (0, j))
    bsp = pl.BlockSpec((1, cb), lambda j, bb: (0, j))
    dx, dw, db = pl.pallas_call(
        body, name=name, grid=(cd // cb, cfg.bsz),
        in_specs=[blk, wsp, bsp, blk], out_specs=[blk, wsp, bsp],
        out_shape=[_sds((cfg.t, cd), BF16), _sds((kk, cd), F32), _sds((1, cd), F32)], compiler_params=_cp(),
    )(xbc, w, b.reshape(1, cd), dxc)
    return dx, dw, db[0]


def _softplus(x):
    return jnp.maximum(x, 0.0) + jnp.log(1.0 + jnp.exp(-jnp.abs(x)))


def _ssd_consts(cfg):
    q = cfg.chunk
    i0 = np.arange(q)[:, None]
    i1 = np.arange(q)[None, :]
    ltri = (i1 <= i0).astype(np.float32)
    rexp = np.zeros((LANE, cfg.inner), np.float32)
    for h in range(cfg.heads):
        rexp[h, h * cfg.hd:(h + 1) * cfg.hd] = 1.0
    return jnp.asarray(ltri), jnp.asarray(rexp)


def _sel_dot(x, m, *, passes=2, left=False, trans=False):
    mb = m.astype(BF16)
    acc, rem = None, x
    for _ in range(passes):
        piece = rem.astype(BF16)
        if not left:
            part = _nn(piece, mb)
        elif trans:
            part = _tn(mb, piece)
        else:
            part = _nn(mb, piece)
        acc = part if acc is None else acc + part
        rem = rem - piece.astype(F32)
    return acc


def _ssd_chunk_common(cfg, raw, bias, avec, c_idx, ltri, rexp):
    q = cfg.chunk
    rows = lax.broadcasted_iota(jnp.int32, (q, LANE), 0)
    live = jnp.logical_or(c_idx > 0, rows >= cfg.pad)
    pre = raw + bias
    dt = jnp.where(live, _softplus(pre), 0.0)
    adt = dt * avec
    cs = _sel_dot(adt, ltri, passes=3, left=True)
    cs_t = cs.T
    cs_last = cs[q - 1:q, :]
    e_in = jnp.exp(cs)
    w0 = jnp.exp(cs_last - cs)
    decay = jnp.exp(cs_last)
    return dict(live=live, pre=pre, dt=dt, adt=adt, cs=cs, cs_t=cs_t, e_in=e_in, w0=w0, decay=decay,
                DT=_sel_dot(dt, rexp), E=_sel_dot(e_in, rexp), W0=_sel_dot(w0, rexp),
                DEC=_sel_dot(jnp.broadcast_to(decay, (8, LANE)), rexp)[0:1, :])


def _tri_masks(q):
    r = lax.broadcasted_iota(jnp.int32, (q, q), 0)
    c = lax.broadcasted_iota(jnp.int32, (q, q), 1)
    return c <= r, r <= c


def _head_l(cq, h, tri, tri_t):
    col = cq["cs"][:, h:h + 1]
    row = cq["cs_t"][h:h + 1, :]
    lmat = jnp.where(tri, jnp.exp(jnp.minimum(col - row, 0.0)), 0.0)
    lmat_t = jnp.where(tri_t, jnp.exp(jnp.minimum(row - col, 0.0)), 0.0)
    return lmat, lmat_t


def _nt(a, b):
    return lax.dot_general(a, b, (((1,), (1,)), ((), ())), preferred_element_type=F32)


def _tn(a, b):
    return lax.dot_general(a, b, (((0,), (0,)), ((), ())), preferred_element_type=F32)


def _nn(a, b):
    return jnp.dot(a, b, preferred_element_type=F32)


def ssd_fwd(cfg, xc, small, dt_bias, avec, dexp, *, name):
    q, inner, st, gw, g_n = cfg.chunk, cfg.inner, cfg.state, cfg.gw, cfg.groups
    nc = cfg.nchunks
    ltri, rexp = _ssd_consts(cfg)
    hpt = LANE // cfg.hd
    tiles_per_group = gw // LANE

    bsz, lp = cfg.bsz, cfg.lp
    bcw = g_n * st

    def body(x_ref, b_ref, c_ref, dt_ref, bias_ref, a_ref, d_ref, ltri_ref, rexp_ref, y_ref, sin_ref, s_scr):
        c_idx = pl.program_id(0)

        @pl.when(c_idx == 0)
        def _():
            s_scr[...] = jnp.zeros_like(s_scr)

        ltri_v = ltri_ref[...]
        tri, tri_t = _tri_masks(q)
        lane = lax.broadcasted_iota(jnp.int32, (q, LANE), 1)
        for bi in range(bsz):
            cq = _ssd_chunk_common(cfg, dt_ref[bi], bias_ref[...], a_ref[...], c_idx, ltri_v, rexp_ref[...])
            xs = x_ref[bi]
            xdt = (xs * cq["DT"]).astype(BF16)
            xw = (xs * cq["DT"] * cq["W0"]).astype(BF16)
            s_in = s_scr[bi]
            sin_ref[bi, 0] = s_in
            for g in range(g_n):
                bg = b_ref[bi, :, g * st:(g + 1) * st].astype(BF16)
                cg = c_ref[bi, :, g * st:(g + 1) * st].astype(BF16)
                gmat = _nt(cg, bg)
                gs = slice(g * gw, (g + 1) * gw)
                y0 = _nn(cg, s_in[:, gs].astype(BF16))
                for tt in range(tiles_per_group):
                    tile = g * tiles_per_group + tt
                    ts = slice(tile * LANE, (tile + 1) * LANE)
                    xt = xdt[:, ts]
                    ms, xh = [], []
                    for hh in range(hpt):
                        lmat, _ = _head_l(cq, tile * hpt + hh, tri, tri_t)
                        ms.append((gmat * lmat).astype(BF16))
                        inhead = jnp.logical_and(lane >= hh * cfg.hd, lane < (hh + 1) * cfg.hd)
                        xh.append(jnp.where(inhead, xt, jnp.zeros_like(xt)))
                    yd = _nn(jnp.concatenate(ms, axis=1), jnp.concatenate(xh, axis=0))
                    y_ref[bi, :, ts] = (yd + y0[:, tt * LANE:(tt + 1) * LANE] * cq["E"][:, ts]
                                        + xs[:, ts] * d_ref[:, ts])
                s_scr[bi, :, gs] = s_in[:, gs] * cq["DEC"][:, gs] + _tn(bg, xw[:, gs])

    def rowblk(width, col):
        return pl.BlockSpec((bsz, q, width), lambda c: (0, c, col))

    def const(shape):
        return pl.BlockSpec(shape, lambda c: (0, 0))

    xc3 = xc.reshape(bsz, lp, cfg.conv_dim)
    y, sin = pl.pallas_call(
        body, name=name, grid=(nc,),
        in_specs=[rowblk(inner, 0), rowblk(bcw, inner // bcw), rowblk(bcw, inner // bcw + 1),
                  rowblk(LANE, cfg.dtt), const((1, LANE)), const((1, LANE)), const((1, inner)),
                  const((q, q)), const((LANE, inner))],
        out_specs=[rowblk(inner, 0), pl.BlockSpec((bsz, 1, st, inner), lambda c: (0, c, 0, 0))],
        out_shape=[_sds((bsz, lp, inner), F32), _sds((bsz, nc, st, inner), F32)],
        scratch_shapes=[pltpu.VMEM((bsz, st, inner), F32)], compiler_params=_cp(),
    )(xc3, xc3, xc3, small.reshape(bsz, lp, cfg.sw), dt_bias, avec, dexp, ltri, rexp)
    return y.reshape(cfg.t, inner), sin.reshape(bsz * nc, st, inner)


def ssd_bwd(cfg, xc, small, dt_bias, avec, dexp, sin, dy, *, name):
    q, inner, st, gw, g_n = cfg.chunk, cfg.inner, cfg.state, cfg.gw, cfg.groups
    nc = cfg.nchunks
    ltri, rexp = _ssd_consts(cfg)
    rexp_t = rexp.T
    hpt = LANE // cfg.hd
    tiles_per_group = gw // LANE
    bcw = g_n * st

    def body(x_ref, b_ref, c_ref, dt_ref, bias_ref, a_ref, d_ref, ltri_ref, rexp_ref, rexpt_ref, sin_ref, dy_ref,
             dx_ref, ddt_ref, dd_ref, da_ref, dbias_ref, ds_scr):
        step = pl.program_id(1)
        c_idx = nc - 1 - step

        @pl.when(step == 0)
        def _():
            ds_scr[...] = jnp.zeros_like(ds_scr)

        @pl.when(jnp.logical_and(step == 0, pl.program_id(0) == 0))
        def _():
            dd_ref[...] = jnp.zeros_like(dd_ref)
            da_ref[...] = jnp.zeros_like(da_ref)
            dbias_ref[...] = jnp.zeros_like(dbias_ref)

        ltri_v = ltri_ref[...]
        tri, tri_t = _tri_masks(q)
        red = _sel_dot
        rexpt = rexpt_ref[...]
        cq = _ssd_chunk_common(cfg, dt_ref[...], bias_ref[...], a_ref[...], c_idx, ltri_v, rexp_ref[...])
        xs = x_ref[...]
        dyv = dy_ref[...]
        s_in = sin_ref[0]
        d_s = ds_scr[...]
        xdt_f = xs * cq["DT"]
        xdt = xdt_f.astype(BF16)
        xw_f = xdt_f * cq["W0"]
        xw = xw_f.astype(BF16)
        lane = lax.broadcasted_iota(jnp.int32, (q, LANE), 1)
        sub = lax.broadcasted_iota(jnp.int32, (LANE, q), 0)

        dd_ref[...] += jnp.sum(dyv * xs, axis=0, keepdims=True)
        dy0 = dyv * cq["E"]
        dcs = jnp.zeros((q, LANE), F32)
        dcs_t = jnp.zeros((LANE, q), F32)
        for g in range(g_n):
            bg_f = b_ref[:, g * st:(g + 1) * st]
            cg_f = c_ref[:, g * st:(g + 1) * st]
            bg = bg_f.astype(BF16)
            cg = cg_f.astype(BF16)
            gs = slice(g * gw, (g + 1) * gw)
            gmat = _nt(cg, bg)
            gmat_t = _nt(bg, cg)
            sing = s_in[:, gs].astype(BF16)
            dsg = d_s[:, gs].astype(BF16)
            y0 = _nn(cg, sing)
            dxw = _nn(bg, dsg)
            d_bg = _nt(xw[:, gs], dsg)
            d_cg = _nt(dy0[:, gs].astype(BF16), sing)
            ds_in_g = _tn(cg, dy0[:, gs].astype(BF16))
            dg = jnp.zeros((q, q), F32)
            dxdt_g = []
            for tt in range(tiles_per_group):
                tile = g * tiles_per_group + tt
                ts = slice(tile * LANE, (tile + 1) * LANE)
                xt = xdt[:, ts]
                dyt = dyv[:, ts]
                dyhs, lmats, mts = [], [], []
                for hh in range(hpt):
                    lmat, lmat_t = _head_l(cq, tile * hpt + hh, tri, tri_t)
                    inhead = jnp.logical_and(lane >= hh * cfg.hd, lane < (hh + 1) * cfg.hd)
                    dyhs.append(jnp.where(inhead, dyt, 0.0).astype(BF16))
                    lmats.append(lmat)
                    mts.append((gmat_t * lmat_t).astype(BF16))
                dy_stack = jnp.concatenate(dyhs, axis=0)
                dm_all = _nt(dy_stack, xt)
                for hh in range(hpt):
                    h = tile * hpt + hh
                    dm = dm_all[hh * q:(hh + 1) * q, :]
                    dg = dg + dm * lmats[hh]
                    qm = dm * gmat * lmats[hh]
                    rs = jnp.sum(qm, axis=1, keepdims=True)
                    csum = jnp.sum(qm, axis=0, keepdims=True)
                    dcs = dcs + jnp.where(lane == h, rs, 0.0)
                    dcs_t = dcs_t + jnp.where(sub == h, csum, 0.0)
                dxdt_g.append(_nn(jnp.concatenate(mts, axis=1), dy_stack))
            dxdt_diag = jnp.concatenate(dxdt_g, axis=1) if len(dxdt_g) > 1 else dxdt_g[0]
            dgb = dg.astype(BF16)
            d_cg = d_cg + _nn(dgb, bg)
            d_bg = d_bg + _tn(dgb, cg)
            dx_ref[:, inner + g * st:inner + (g + 1) * st] = d_bg
            dx_ref[:, inner + bcw + g * st:inner + bcw + (g + 1) * st] = d_cg
            dxdt = dxdt_diag + dxw * cq["W0"][:, gs]
            dx_ref[:, gs] = dyv[:, gs] * d_ref[:, gs] + dxdt * cq["DT"][:, gs]
            rt = rexpt[gs, :]
            dcs = dcs + red(dyv[:, gs] * y0 * cq["E"][:, gs], rt)
            r_w = red(dxw * xw_f[:, gs], rt)
            dcs = dcs - r_w
            dcs_last_g = jnp.sum(r_w, axis=0, keepdims=True)
            ddec = red(jnp.broadcast_to(jnp.sum(d_s[:, gs] * s_in[:, gs], axis=0, keepdims=True), (8, gw)), rt)[0:1, :]
            dcs_last_g = dcs_last_g + ddec * cq["decay"]
            dcs = dcs + jnp.where(lax.broadcasted_iota(jnp.int32, (q, LANE), 0) == q - 1, dcs_last_g, 0.0)
            ddt_part = red(dxdt * xs[:, gs], rt)
            if g == 0:
                ddt = ddt_part
            else:
                ddt = ddt + ddt_part
            ds_scr[:, gs] = d_s[:, gs] * cq["DEC"][:, gs] + ds_in_g
        dcs = dcs - dcs_t.T
        dadt = _sel_dot(dcs, ltri_v, left=True, trans=True)
        ddt = ddt + dadt * a_ref[...]
        da_ref[...] += jnp.sum(dadt * cq["dt"], axis=0, keepdims=True)
        draw = jnp.where(cq["live"], ddt * jax.nn.sigmoid(cq["pre"]), 0.0)
        ddt_ref[...] = draw
        dbias_ref[...] += jnp.sum(draw, axis=0, keepdims=True)

    def rowblk(width, col):
        return pl.BlockSpec((q, width), lambda b, s: (b * nc + nc - 1 - s, col))

    def const(shape):
        return pl.BlockSpec(shape, lambda b, s: (0, 0))

    bcol = inner // bcw
    outs = pl.pallas_call(
        body, name=name, grid=(cfg.bsz, nc),
        in_specs=[rowblk(inner, 0), rowblk(bcw, bcol), rowblk(bcw, bcol + 1), rowblk(LANE, cfg.dtt),
                  const((1, LANE)), const((1, LANE)), const((1, inner)), const((q, q)), const((LANE, inner)),
                  const((inner, LANE)),
                  pl.BlockSpec((1, st, inner), lambda b, s: (b * nc + nc - 1 - s, 0, 0)), rowblk(inner, 0)],
        out_specs=[rowblk(cfg.conv_dim, 0), rowblk(LANE, 0),
                   const((1, inner)), const((1, LANE)), const((1, LANE))],
        out_shape=[_sds((cfg.t, cfg.conv_dim), F32),
                   _sds((cfg.t, LANE), F32), _sds((1, inner), F32), _sds((1, LANE), F32), _sds((1, LANE), F32)],
        scratch_shapes=[pltpu.VMEM((st, inner), F32)], compiler_params=_cp(),
    )(xc, xc, xc, small, dt_bias, avec, dexp, ltri, rexp, rexp_t, sin, dy)
    return outs


def tail_fwd(cfg, y, z, w, *, name):
    t, inner, gw = cfg.t, cfg.inner, cfg.gw
    tr = _pick(t, 272, 16)

    def body(y_ref, z_ref, w_ref, o_ref):
        for g in range(cfg.groups):
            gs = slice(g * gw, (g + 1) * gw)
            yg = y_ref[:, gs] * _silu(z_ref[:, gs].astype(F32))
            r = lax.rsqrt(jnp.mean(yg * yg, axis=-1, keepdims=True) + EPS)
            o_ref[:, gs] = (yg * r * w_ref[:, gs]).astype(BF16)

    row = pl.BlockSpec((tr, inner), lambda i: (i, 0))
    return pl.pallas_call(
        body, name=name, grid=(t // tr,), in_specs=[row, row, pl.BlockSpec((1, inner), lambda i: (0, 0))],
        out_specs=row, out_shape=_sds((t, inner), BF16), compiler_params=_cp(),
    )(y, z, w.reshape(1, inner))


def tail_bwd(cfg, do, y, z, w, *, name):
    t, inner, gw = cfg.t, cfg.inner, cfg.gw
    tr = _pick(t, 272, 16)

    def body(do_ref, y_ref, z_ref, w_ref, dy_ref, dz_ref, dw_ref):
        @pl.when(pl.program_id(0) == 0)
        def _():
            dw_ref[...] = jnp.zeros_like(dw_ref)

        for g in range(cfg.groups):
            gs = slice(g * gw, (g + 1) * gw)
            yv = y_ref[:, gs]
            zv = z_ref[:, gs].astype(F32)
            dov = do_ref[:, gs]
            sz = _silu(zv)
            yg = yv * sz
            r = lax.rsqrt(jnp.mean(yg * yg, axis=-1, keepdims=True) + EPS)
            xh = yg * r
            gg = dov * w_ref[:, gs]
            dyg = r * (gg - xh * jnp.mean(gg * xh, axis=-1, keepdims=True))
            dw_ref[:, gs] += jnp.sum(dov * xh, axis=0, keepdims=True)
            dy_ref[:, gs] = dyg * sz
            dz_ref[:, gs] = (dyg * yv * _dsilu(zv)).astype(BF16)

    row = pl.BlockSpec((tr, inner), lambda i: (i, 0))
    vec = pl.BlockSpec((1, inner), lambda i: (0, 0))
    dy, dz, dw = pl.pallas_call(
        body, name=name, grid=(t // tr,), in_specs=[row, row, row, vec], out_specs=[row, row, vec],
        out_shape=[_sds((t, inner), F32), _sds((t, inner), BF16), _sds((1, inner), F32)], compiler_params=_cp(),
    )(do, y, z, w.reshape(1, inner))
    return dy, dz, dw[0]


def rope_tables(cfg):
    half = cfg.rope // 2
    pos = np.maximum(np.arange(cfg.lp) - cfg.pad, 0).astype(np.float32)
    inv = ROPE_THETA ** (-jnp.arange(0, cfg.rope, 2, dtype=F32) / cfg.rope)
    ang = jnp.asarray(pos)[:, None] * inv[None, :]
    cos, sin = jnp.cos(ang), jnp.sin(ang)
    zero = jnp.zeros((cfg.lp, LANE - 2 * half), F32)
    zh = jnp.zeros((cfg.lp, half), F32)
    ctab = jnp.concatenate([cos, cos, zero], axis=1)
    s1 = jnp.concatenate([-sin, zh, zero], axis=1)
    s2 = jnp.concatenate([zh, sin, zero], axis=1)
    return ctab, s1, s2


def _rope(x, c, s1, s2, half):
    return x * c + pltpu.roll(x, LANE - half, 1) * s1 + pltpu.roll(x, half, 1) * s2


def _rope_t(dy, c, s1, s2, half):
    return dy * c + pltpu.roll(dy * s1, half, 1) + pltpu.roll(dy * s2, LANE - half, 1)


def _attn_scale(cfg):
    return (cfg.nope + cfg.rope) ** -0.5


def rope_fwd(cfg, qf, small, tabs, *, name):
    t, qw, lp = cfg.t, cfg.qw, cfg.lp
    tr = _pick(lp, 544, 16)
    nrb = lp // tr
    half = cfg.rope // 2
    scale = _attn_scale(cfg)

    def body(q_ref, k_ref, c_ref, s1_ref, s2_ref, qo_ref, ko_ref):
        c, s1, s2 = c_ref[...], s1_ref[...], s2_ref[...]
        for h in range(cfg.mh):
            a = h * 2 * LANE
            qo_ref[:, a:a + LANE] = (q_ref[:, a:a + LANE] * scale).astype(BF16)
            qo_ref[:, a + LANE:a + 2 * LANE] = (_rope(q_ref[:, a + LANE:a + 2 * LANE], c, s1, s2, half) * scale).astype(BF16)
        ko_ref[...] = _rope(k_ref[...], c, s1, s2, half).astype(BF16)

    tab = pl.BlockSpec((tr, LANE), lambda i: (i % nrb, 0))
    return pl.pallas_call(
        body, name=name, grid=(t // tr,),
        in_specs=[pl.BlockSpec((tr, qw), lambda i: (i, 0)), pl.BlockSpec((tr, LANE), lambda i: (i, cfg.kt)), tab, tab, tab],
        out_specs=[pl.BlockSpec((tr, qw), lambda i: (i, 0)), pl.BlockSpec((tr, LANE), lambda i: (i, 0))],
        out_shape=[_sds((t, qw), BF16), _sds((t, LANE), BF16)], compiler_params=_cp(),
    )(qf, small, *tabs)


def rope_bwd(cfg, dq, dkpe, tabs, *, name):
    t, qw, lp = cfg.t, cfg.qw, cfg.lp
    tr = _pick(lp, 544, 16)
    nrb = lp // tr
    half = cfg.rope // 2
    scale = _attn_scale(cfg)

    def body(dq_ref, dk_ref, c_ref, s1_ref, s2_ref, qo_ref, ko_ref):
        c, s1, s2 = c_ref[...], s1_ref[...], s2_ref[...]
        for h in range(cfg.mh):
            a = h * 2 * LANE
            qo_ref[:, a:a + LANE] = (dq_ref[:, a:a + LANE] * scale).astype(BF16)
            qo_ref[:, a + LANE:a + 2 * LANE] = _rope_t(dq_ref[:, a + LANE:a + 2 * LANE] * scale, c, s1, s2, half).astype(BF16)
        dk = dk_ref[0]
        for h in range(1, cfg.mh):
            dk = dk + dk_ref[h]
        ko_ref[...] = _rope_t(dk, c, s1, s2, half)

    tab = pl.BlockSpec((tr, LANE), lambda i: (i % nrb, 0))
    return pl.pallas_call(
        body, name=name, grid=(t // tr,),
        in_specs=[pl.BlockSpec((tr, qw), lambda i: (i, 0)), pl.BlockSpec((cfg.mh, tr, LANE), lambda i: (0, i, 0)),
                  tab, tab, tab],
        out_specs=[pl.BlockSpec((tr, qw), lambda i: (i, 0)), pl.BlockSpec((tr, LANE), lambda i: (i, 0))],
        out_shape=[_sds((t, qw), BF16), _sds((t, LANE), F32)], compiler_params=_cp(),
    )(dq, dkpe, *tabs)


def _q_blocks(cfg):
    bounds = [0, cfg.chunk] + list(range(cfg.chunk + 256, cfg.lp + 1, 256))
    assert bounds[-1] == cfg.lp, "SEQ must be a multiple of 256"
    return list(zip(bounds[:-1], bounds[1:]))


def _attn_mask(cfg, qs, qe):
    rows = qs + lax.broadcasted_iota(jnp.int32, (qe - qs, qe), 0)
    cols = lax.broadcasted_iota(jnp.int32, (qe - qs, qe), 1)
    return jnp.logical_and(cols <= rows, jnp.logical_or(cols >= cfg.pad, rows < cfg.pad))


def _max_q_block(cfg):
    return max(qe - qs for qs, qe in _q_blocks(cfg))


def _masked_scores(cfg, q, k2, qs, qe, s_scr):
    bq, n = qe - qs, qe
    s_scr[0:bq, 0:n] = _nt(q, k2)
    if qs == 0:
        s_scr[0:bq, 0:n] = jnp.where(_attn_mask(cfg, 0, qe), s_scr[0:bq, 0:n], MASK_VALUE)
    else:
        assert qs >= cfg.chunk and cfg.pad < LANE
        cols = lax.broadcasted_iota(jnp.int32, (bq, LANE), 1)
        s_scr[0:bq, 0:LANE] = jnp.where(cols >= cfg.pad, s_scr[0:bq, 0:LANE], MASK_VALUE)
        r = lax.broadcasted_iota(jnp.int32, (bq, bq), 0)
        c = lax.broadcasted_iota(jnp.int32, (bq, bq), 1)
        s_scr[0:bq, qs:qe] = jnp.where(c <= r, s_scr[0:bq, qs:qe], MASK_VALUE)
    return s_scr[0:bq, 0:n]


def attn_fwd(cfg, qr, kv, kpe, *, name):
    lp, t, mh = cfg.lp, cfg.t, cfg.mh
    blocks = _q_blocks(cfg)

    def body(q_ref, kv_ref, kp_ref, o_ref, l_ref, s_scr):
        for qs, qe in blocks:
            n = qe
            q = q_ref[qs:qe, :]
            k2 = jnp.concatenate([kv_ref[0:n, 0:LANE], kp_ref[0:n, :]], axis=1)
            s = _masked_scores(cfg, q, k2, qs, qe, s_scr)
            m = jnp.max(s, axis=-1, keepdims=True)
            p = jnp.exp(s - m)
            l = jnp.sum(p, axis=-1, keepdims=True)
            o_ref[qs:qe, :] = _nn(p.astype(BF16), kv_ref[0:n, LANE:2 * LANE]) * (1.0 / l)
            l_ref[qs:qe, :] = jnp.broadcast_to(m + jnp.log(l), (qe - qs, LANE))

    hb = pl.BlockSpec((lp, 2 * LANE), lambda b, h: (b, h))
    ob = pl.BlockSpec((lp, LANE), lambda b, h: (b, h))
    return pl.pallas_call(
        body, name=name, grid=(cfg.bsz, mh),
        in_specs=[hb, hb, pl.BlockSpec((lp, LANE), lambda b, h: (b, 0))], out_specs=[ob, ob],
        out_shape=[_sds((t, mh * LANE), F32), _sds((t, mh * LANE), F32)],
        scratch_shapes=[pltpu.VMEM((_max_q_block(cfg), lp), F32)], compiler_params=_cp(),
    )(qr, kv, kpe)


def attn_bwd(cfg, qr, kv, kpe, o, lse, do, *, name):
    lp, t, mh = cfg.lp, cfg.t, cfg.mh
    blocks = _q_blocks(cfg)

    def body(q_ref, kv_ref, kp_ref, o_ref, l_ref, do_ref, dq_ref, dkv_ref, dkp_ref, dk_acc, dv_acc, s_scr):
        dk_acc[...] = jnp.zeros_like(dk_acc)
        dv_acc[...] = jnp.zeros_like(dv_acc)
        for qs, qe in blocks:
            n = qe
            q = q_ref[qs:qe, :]
            k2 = jnp.concatenate([kv_ref[0:n, 0:LANE], kp_ref[0:n, :]], axis=1)
            dov = do_ref[qs:qe, :]
            delta = jnp.sum(dov * o_ref[qs:qe, :], axis=-1, keepdims=True)
            dob = dov.astype(BF16)
            s = _masked_scores(cfg, q, k2, qs, qe, s_scr)
            p = jnp.exp(s - l_ref[qs:qe, 0:1])
            dp = _nt(dob, kv_ref[0:n, LANE:2 * LANE])
            ds = (p * (dp - delta)).astype(BF16)
            dq_ref[qs:qe, :] = _nn(ds, k2)
            dv_acc[0:n, :] += _tn(p.astype(BF16), dob)
            dk_acc[0:n, :] += _tn(ds, q)
        dkv_ref[:, 0:LANE] = dk_acc[:, 0:LANE].astype(BF16)
        dkv_ref[:, LANE:2 * LANE] = dv_acc[...].astype(BF16)
        dkp_ref[0] = dk_acc[:, LANE:2 * LANE]

    hb = pl.BlockSpec((lp, 2 * LANE), lambda b, h: (b, h))
    ob = pl.BlockSpec((lp, LANE), lambda b, h: (b, h))
    return pl.pallas_call(
        body, name=name, grid=(cfg.bsz, mh),
        in_specs=[hb, hb, pl.BlockSpec((lp, LANE), lambda b, h: (b, 0)), ob, ob, ob],
        out_specs=[hb, hb, pl.BlockSpec((1, lp, LANE), lambda b, h: (h, b, 0))],
        out_shape=[_sds((t, cfg.qw), F32), _sds((t, mh * 2 * LANE), BF16), _sds((mh, t, LANE), F32)],
        scratch_shapes=[pltpu.VMEM((lp, 2 * LANE), F32), pltpu.VMEM((lp, LANE), F32),
                        pltpu.VMEM((_max_q_block(cfg), lp), F32)], compiler_params=_cp(),
    )(qr, kv, kpe, o, lse, do)


def _live_rows(cfg, tr, shape):
    rows = pl.program_id(1) * tr + lax.broadcasted_iota(jnp.int32, shape, 0)
    return rows >= cfg.pad


def gate_fwd(cfg, ya, yb, g, *, name):
    d, lp = cfg.d, cfg.lp
    tr = _pick(lp, 544, 16)
    nrb = lp // tr

    def body(ya_ref, yb_ref, ga_ref, gb_ref, o_ref):
        f = lambda ref: ref[...].astype(F32)
        mix = jax.nn.sigmoid(f(ga_ref)) * f(ya_ref) + jax.nn.sigmoid(f(gb_ref)) * f(yb_ref)
        o_ref[...] = jnp.where(_live_rows(cfg, tr, mix.shape), mix, 0.0).astype(BF16)

    row = pl.BlockSpec((tr, d), lambda b, j: (b * nrb + j, 0))
    row1 = pl.BlockSpec((tr, d), lambda b, j: (b * nrb + j, 1))
    return pl.pallas_call(
        body, name=name, grid=(cfg.bsz, nrb), in_specs=[row, row, row, row1], out_specs=row,
        out_shape=_sds((cfg.t, d), BF16), compiler_params=_cp(),
    )(ya, yb, g, g)


def gate_bwd(cfg, dmix, ya, yb, g, *, name):
    d, lp = cfg.d, cfg.lp
    tr = _pick(lp, 544, 16)
    nrb = lp // tr

    def body(dm_ref, ya_ref, yb_ref, ga_ref, gb_ref, dya_ref, dyb_ref, dg_ref):
        dm = dm_ref[...]
        dm = jnp.where(_live_rows(cfg, tr, dm.shape), dm, 0.0)
        sa = jax.nn.sigmoid(ga_ref[...].astype(F32))
        sb = jax.nn.sigmoid(gb_ref[...].astype(F32))
        dya_ref[...] = (dm * sa).astype(BF16)
        dyb_ref[...] = (dm * sb).astype(BF16)
        dg_ref[:, 0:d] = (dm * ya_ref[...].astype(F32) * sa * (1.0 - sa)).astype(BF16)
        dg_ref[:, d:2 * d] = (dm * yb_ref[...].astype(F32) * sb * (1.0 - sb)).astype(BF16)

    row = pl.BlockSpec((tr, d), lambda b, j: (b * nrb + j, 0))
    row1 = pl.BlockSpec((tr, d), lambda b, j: (b * nrb + j, 1))
    row2 = pl.BlockSpec((tr, 2 * d), lambda b, j: (b * nrb + j, 0))
    return pl.pallas_call(
        body, name=name, grid=(cfg.bsz, nrb), in_specs=[row, row, row, row, row1], out_specs=[row, row, row2],
        out_shape=[_sds((cfg.t, d), BF16), _sds((cfg.t, d), BF16), _sds((cfg.t, 2 * d), BF16)], compiler_params=_cp(),
    )(dmix, ya, yb, g, g)


def loss_head(cfg, h, target, w, *, name):
    d, q, nc = cfg.d, cfg.chunk, cfg.nchunks
    tpb = cfg.seq // q

    def body(h_ref, t_ref, w_ref, loss_ref, dh_ref, dw_ref):
        j = pl.program_id(1)

        @pl.when(jnp.logical_and(j == 0, pl.program_id(0) == 0))
        def _():
            loss_ref[...] = jnp.zeros_like(loss_ref)
            dw_ref[...] = jnp.zeros_like(dw_ref)

        @pl.when(j == 0)
        def _():
            dh_ref[...] = jnp.zeros_like(dh_ref)

        @pl.when(j > 0)
        def _():
            xv = h_ref[...]
            r = lax.rsqrt(jnp.mean(xv * xv, axis=-1, keepdims=True) + EPS)
            xh = xv * r
            err = xh * w_ref[...] - t_ref[...]
            loss_ref[...] += 0.5 * jnp.sum(jnp.sum(err * err, axis=-1, keepdims=True) / d, axis=0, keepdims=True)
            dy = err * (1.0 / d)
            g = dy * w_ref[...]
            dh_ref[...] = r * (g - xh * jnp.mean(g * xh, axis=-1, keepdims=True))
            dw_ref[...] += jnp.sum(dy * xh, axis=0, keepdims=True)

    row = pl.BlockSpec((q, d), lambda b, j: (b * nc + j, 0))
    loss, dh, dw = pl.pallas_call(
        body, name=name, grid=(cfg.bsz, nc),
        in_specs=[row, pl.BlockSpec((q, d), lambda b, j: (b * tpb + jnp.maximum(j - 1, 0), 0)),
                  pl.BlockSpec((1, d), lambda b, j: (0, 0))],
        out_specs=[pl.BlockSpec((8, LANE), lambda b, j: (0, 0)), row, pl.BlockSpec((1, d), lambda b, j: (0, 0))],
        out_shape=[_sds((8, LANE), F32), _sds((cfg.t, d), F32), _sds((1, d), F32)], compiler_params=_cp(),
    )(h, target, w.reshape(1, d))
    return loss[0, 0], dh, dw[0]


def _rows_tile(r, c):
    return _pick(r, max(8, (1 << 18) // max(c, 1) // 8 * 8), 8)


def _adam_update(w, g, m, v):
    c1 = 1.0 - ADAM_B1 ** ADAM_STEP
    c2 = 1.0 - ADAM_B2 ** ADAM_STEP
    mn = ADAM_B1 * m + (1.0 - ADAM_B1) * g
    vn = ADAM_B2 * v + (1.0 - ADAM_B2) * (g * g)
    delta = -ADAM_LR * ((mn / c1) / (jnp.sqrt(vn / c2) + ADAM_EPS) + ADAM_WD * w)
    return delta, mn, vn


def adamw_layer(w, m, v, g, li, prev, dep, *, name):
    _, r, c = w.shape
    tr = _rows_tile(r, c)

    def body(*refs):
        w_ref, m_ref, v_ref, g_ref = refs[:4]
        go_ref, d_ref, mo_ref, vo_ref = refs[-4:]
        gv = g_ref[...]
        delta, mn, vn = _adam_update(w_ref[0], gv, m_ref[0], v_ref[0])
        go_ref[0] = gv
        d_ref[0] = delta
        mo_ref[0] = mn
        vo_ref[0] = vn

    if tr * c * 4 >= (1 << 16):
        steps = r // tr
        blk3 = pl.BlockSpec((1, tr, c), lambda i: (li, i, 0))
        blk2 = pl.BlockSpec((tr, c), lambda i: (i, 0))
    else:
        tc = _pick(c, max(LANE, (1 << 18) // r // LANE * LANE), LANE)
        steps = c // tc
        blk3 = pl.BlockSpec((1, r, tc), lambda i: (li, 0, i))
        blk2 = pl.BlockSpec((r, tc), lambda i: (0, i))
    anyspec = pl.BlockSpec(memory_space=pl.ANY)
    in_specs = [blk3, blk3, blk3, blk2, anyspec]
    args = [w, m, v, g, dep]
    aliases = {}
    if prev is not None:
        in_specs += [anyspec] * 4
        args += list(prev)
        aliases = {5 + i: i for i in range(4)}
    return pl.pallas_call(
        body, name=name, grid=(steps,), in_specs=in_specs, out_specs=[blk3] * 4,
        out_shape=[_sds(w.shape, F32)] * 4, input_output_aliases=aliases, compiler_params=_cp(),
    )(*args)


def pair_add(g4, other, half, *, name):
    n, _, r, c = g4.shape
    tr = _rows_tile(r, c)

    def body(h_ref, a_ref, b_ref, o_ref):
        o_ref[0] = (a_ref[0, 0] + b_ref[0]).astype(BF16)

    blk = pl.BlockSpec((1, tr, c), lambda j, i, h: (j, i, 0))
    grid_spec = pltpu.PrefetchScalarGridSpec(
        num_scalar_prefetch=1, grid=(n, r // tr),
        in_specs=[pl.BlockSpec((1, 1, tr, c), lambda j, i, h: (j, h[0], i, 0)), blk], out_specs=blk)
    return pl.pallas_call(body, name=name, grid_spec=grid_spec, out_shape=_sds((n, r, c), BF16),
                          compiler_params=_cp())(half, g4, other)


def chip_sum(recv, part, where, *, name):
    n, r, c = recv.shape
    tr = _rows_tile(r, c)

    def body(s_ref, *refs):
        own_ref, o_ref = refs[n], refs[n + 1]
        acc = None
        for j in range(n):
            term = jnp.where(s_ref[0] == j, own_ref[0], refs[j][0]).astype(F32)
            acc = term if acc is None else acc + term
        o_ref[0] = acc

    def slot(j):
        return pl.BlockSpec((1, tr, c), lambda i, s: (jnp.where(s[0] == j, (j + 1) % n, j), i, 0))

    grid_spec = pltpu.PrefetchScalarGridSpec(
        num_scalar_prefetch=1, grid=(r // tr,),
        in_specs=[slot(j) for j in range(n)] + [pl.BlockSpec((1, tr, c), lambda i, s: (s[0], i, 0))],
        out_specs=pl.BlockSpec((1, tr, c), lambda i, s: (s[1], i, 0)))
    return pl.pallas_call(body, name=name, grid_spec=grid_spec, out_shape=_sds((2, r, c), F32),
                          compiler_params=_cp())(where, *([recv] * n), part)


def _coords():
    return lax.axis_index("x"), lax.axis_index("y"), lax.axis_index("c")


def _other_chips(x, y):
    return [(1 - x, y), (x, 1 - y), (1 - x, 1 - y)]


def gather_chips(arrs, *, name):
    n = len(arrs)
    anyspec = pl.BlockSpec(memory_space=pl.ANY)

    def body(*refs):
        ins, outs = refs[:n], refs[n:2 * n]
        send_sems, recv_sems, local_sems = refs[2 * n:]
        x, y, c = _coords()
        me = 2 * x + y
        chips = _other_chips(x, y)
        copies = []
        for k in range(n):
            loc = pltpu.make_async_copy(ins[k], outs[k].at[me], local_sems.at[k])
            loc.start()
            copies.append(loc)
        sends = []
        for k in range(n):
            for j, (px, py) in enumerate(chips):
                cp = pltpu.make_async_remote_copy(
                    src_ref=ins[k], dst_ref=outs[k].at[me], send_sem=send_sems.at[k, j], recv_sem=recv_sems.at[k, j],
                    device_id=(px, py, c), device_id_type=MESH)
                cp.start()
                sends.append(cp)
        for k in range(n):
            for j, (px, py) in enumerate(chips):
                pltpu.make_async_remote_copy(
                    src_ref=ins[k], dst_ref=outs[k].at[2 * px + py], send_sem=send_sems.at[k, j],
                    recv_sem=recv_sems.at[k, j], device_id=(px, py, c), device_id_type=MESH).wait_recv()
        for cp in sends:
            cp.wait_send()
        for cp in copies:
            cp.wait()

    return pl.pallas_call(
        body, name=name, in_specs=[anyspec] * n, out_specs=[anyspec] * n,
        out_shape=[_sds((4,) + a.shape, a.dtype) for a in arrs],
        scratch_shapes=[pltpu.SemaphoreType.DMA((n, 3)), pltpu.SemaphoreType.DMA((n, 3)), pltpu.SemaphoreType.DMA((n,))],
        compiler_params=_cp(has_side_effects=True),
    )(*arrs)


def allreduce_small(vec, after, *, name):
    r, c = vec.shape

    def body(v_ref, after_ref, o_ref, buf, send_sems, recv_sems):
        x, y, cc = _coords()
        me = 4 * x + 2 * y + cc
        buf[me] = v_ref[...]
        sends = []
        flips = [(fx, fy, fc) for fx in (0, 1) for fy in (0, 1) for fc in (0, 1)][1:]
        for j, (fx, fy, fc) in enumerate(flips):
            peer = ((1 - x) if fx else x, (1 - y) if fy else y, (1 - cc) if fc else cc)
            cp = pltpu.make_async_remote_copy(
                src_ref=v_ref, dst_ref=buf.at[me], send_sem=send_sems.at[j], recv_sem=recv_sems.at[j],
                device_id=peer, device_id_type=MESH)
            cp.start()
            sends.append(cp)
        for j, (fx, fy, fc) in enumerate(flips):
            px, py, pc = ((1 - x) if fx else x, (1 - y) if fy else y, (1 - cc) if fc else cc)
            pltpu.make_async_remote_copy(
                src_ref=v_ref, dst_ref=buf.at[4 * px + 2 * py + pc], send_sem=send_sems.at[j],
                recv_sem=recv_sems.at[j], device_id=(px, py, pc), device_id_type=MESH).wait_recv()
        for cp in sends:
            cp.wait_send()
        acc = buf[0]
        for k in range(1, 8):
            acc = acc + buf[k]
        o_ref[...] = acc

    vm = pl.BlockSpec(memory_space=pltpu.VMEM)
    return pl.pallas_call(
        body, name=name, in_specs=[vm, pl.BlockSpec(memory_space=pl.ANY)], out_specs=vm, out_shape=_sds((r, c), F32),
        scratch_shapes=[pltpu.VMEM((8, r, c), F32), pltpu.SemaphoreType.DMA((7,)), pltpu.SemaphoreType.DMA((7,))],
        compiler_params=_cp(has_side_effects=True),
    )(vec, after)


def pair_exchange(arrs, *, name):
    n = len(arrs)
    anyspec = pl.BlockSpec(memory_space=pl.ANY)

    def body(*refs):
        ins, outs = refs[:n], refs[n:2 * n]
        send_sems, recv_sems = refs[2 * n:]
        x, y, c = _coords()
        sends = []
        for k in range(n):
            for j in range(4):
                cp = pltpu.make_async_remote_copy(
                    src_ref=ins[k].at[j, 1 - c], dst_ref=outs[k].at[j], send_sem=send_sems.at[k, j],
                    recv_sem=recv_sems.at[k, j], device_id=(x, y, 1 - c), device_id_type=MESH)
                cp.start()
                sends.append(cp)
        for cp in sends:
            cp.wait()

    return pl.pallas_call(
        body, name=name, in_specs=[anyspec] * n, out_specs=[anyspec] * n,
        out_shape=[_sds((a.shape[0],) + a.shape[2:], a.dtype) for a in arrs],
        scratch_shapes=[pltpu.SemaphoreType.DMA((n, 4)), pltpu.SemaphoreType.DMA((n, 4))],
        compiler_params=_cp(has_side_effects=True),
    )(*arrs)


def pair_share(lands, *, name):
    n = len(lands)
    anyspec = pl.BlockSpec(memory_space=pl.ANY)

    def body(*refs):
        ins, outs = refs[:n], refs[n:2 * n]
        send_sems, recv_sems = refs[2 * n:]
        x, y, c = _coords()
        sends = []
        for k in range(n):
            for j, (px, py) in enumerate(_other_chips(x, y)):
                cp = pltpu.make_async_remote_copy(
                    src_ref=ins[k].at[2 * px + py, c], dst_ref=outs[k].at[2 * px + py, c], send_sem=send_sems.at[k, j],
                    recv_sem=recv_sems.at[k, j], device_id=(x, y, 1 - c), device_id_type=MESH)
                cp.start()
                sends.append(cp)
        for k in range(n):
            for j, (px, py) in enumerate(_other_chips(x, y)):
                pltpu.make_async_remote_copy(
                    src_ref=ins[k].at[2 * px + py, c], dst_ref=outs[k].at[2 * px + py, 1 - c],
                    send_sem=send_sems.at[k, j], recv_sem=recv_sems.at[k, j], device_id=(x, y, 1 - c),
                    device_id_type=MESH).wait_recv()
        for cp in sends:
            cp.wait_send()

    return pl.pallas_call(
        body, name=name, in_specs=[anyspec] * n, out_specs=[anyspec] * n,
        out_shape=[_sds(a.shape, a.dtype) for a in lands], input_output_aliases={k: k for k in range(n)},
        scratch_shapes=[pltpu.SemaphoreType.DMA((n, 3)), pltpu.SemaphoreType.DMA((n, 3))],
        compiler_params=_cp(has_side_effects=True),
    )(*lands)


def pair_fill(arrs, *, name):
    n = len(arrs)
    anyspec = pl.BlockSpec(memory_space=pl.ANY)

    def body(*refs):
        ins, outs = refs[:n], refs[n:2 * n]
        send_sems, recv_sems = refs[2 * n:]
        x, y, c = _coords()
        sends = []
        for k in range(n):
            cp = pltpu.make_async_remote_copy(
                src_ref=ins[k].at[c], dst_ref=outs[k].at[c], send_sem=send_sems.at[k], recv_sem=recv_sems.at[k],
                device_id=(x, y, 1 - c), device_id_type=MESH)
            cp.start()
            sends.append(cp)
        for k in range(n):
            pltpu.make_async_remote_copy(
                src_ref=ins[k].at[c], dst_ref=outs[k].at[1 - c], send_sem=send_sems.at[k], recv_sem=recv_sems.at[k],
                device_id=(x, y, 1 - c), device_id_type=MESH).wait_recv()
        for cp in sends:
            cp.wait_send()

    return pl.pallas_call(
        body, name=name, in_specs=[anyspec] * n, out_specs=[anyspec] * n,
        out_shape=[_sds(a.shape, a.dtype) for a in arrs], input_output_aliases={k: k for k in range(n)},
        scratch_shapes=[pltpu.SemaphoreType.DMA((n,)), pltpu.SemaphoreType.DMA((n,))],
        compiler_params=_cp(has_side_effects=True),
    )(*arrs)


_HBM = pl.BlockSpec(memory_space=pltpu.HBM)
_SEM = pl.BlockSpec(memory_space=pltpu.SEMAPHORE)


def _ici_copies(kind, srcs, lands, send_sems, recv_sems):
    x, y, c = _coords()
    me = 2 * x + y
    sends, recvs = [], []
    for k in range(len(srcs)):
        for j, (px, py) in enumerate(_other_chips(x, y)):
            peer = 2 * px + py
            if kind == "gather":
                src, there, here = srcs[k].at[c], lands[k].at[me, c], lands[k].at[peer, c]
            else:
                src, there, here = srcs[k].at[peer], lands[k].at[me], lands[k].at[peer]
            sem = 3 * k + j
            mk = functools.partial(pltpu.make_async_remote_copy, src_ref=src, send_sem=send_sems.at[sem],
                                   recv_sem=recv_sems.at[sem], device_id=(px, py, c), device_id_type=MESH)
            sends.append(mk(dst_ref=there))
            recvs.append(mk(dst_ref=here))
    return sends, recvs


def ici_start(kind, srcs, lands, after, *, name):
    n = len(srcs)

    def body(*refs):
        src_refs, land_refs = refs[:n], refs[n:2 * n]
        send_sems, recv_sems = refs[2 * n + 1], refs[2 * n + 2]
        token = refs[-1]
        sends, _ = _ici_copies(kind, src_refs, land_refs, send_sems, recv_sems)
        for cp in sends:
            cp.start()
        token[...] = jnp.zeros_like(token)

    both = list(srcs) + list(lands)
    out = pl.pallas_call(
        body, name=name,
        in_specs=[_HBM] * (2 * n) + [pl.BlockSpec(memory_space=pl.ANY)],
        out_shape=(pltpu.SemaphoreType.DMA((3 * n,)), pltpu.SemaphoreType.DMA((3 * n,)),
                   *[pltpu.HBM(a.shape, a.dtype) for a in both], _sds((8, LANE), F32)),
        out_specs=(_SEM, _SEM, *([_HBM] * (2 * n)), pl.BlockSpec(memory_space=pltpu.VMEM)),
        input_output_aliases={i: 2 + i for i in range(2 * n)},
        compiler_params=_cp(has_side_effects=pltpu.SideEffectType.DATAFLOW_SIDE_EFFECTING),
    )(*[pltpu.with_memory_space_constraint(a, pltpu.HBM) for a in both], after)
    return out[0], out[1], list(out[2:2 + n]), list(out[2 + n:2 + 2 * n]), out[-1]


def ici_wait(kind, started, after, *, name):
    send_sems, recv_sems, srcs, lands, _ = started
    n = len(srcs)

    def body(*refs):
        src_refs, land_refs = refs[:n], refs[n:2 * n]
        sends, recvs = _ici_copies(kind, src_refs, land_refs, refs[2 * n], refs[2 * n + 1])
        for cp in sends:
            cp.wait_send()
        for cp in recvs:
            cp.wait_recv()

    both = list(srcs) + list(lands)
    out = pl.pallas_call(
        body, name=name,
        in_specs=[_HBM] * (2 * n) + [_SEM, _SEM, pl.BlockSpec(memory_space=pl.ANY)],
        out_shape=tuple(pltpu.HBM(a.shape, a.dtype) for a in both), out_specs=tuple([_HBM] * (2 * n)),
        input_output_aliases={i: i for i in range(2 * n)},
        compiler_params=_cp(has_side_effects=pltpu.SideEffectType.DATAFLOW_SIDE_EFFECTING),
    )(*both, send_sems, recv_sems, after)
    return list(out[:n]), list(out[n:])


BIG = ["w_in", "w_uq", "w_ukv", "w_branch_ssm", "w_branch_mla", "w_out", "w_mlp_up", "w_mlp_down"]
COL_SHARDED = {"w_in", "w_uq", "w_ukv", "w_mlp_up"}
SMALL_REPL = ["norm_mix_w", "conv_b", "dt_bias", "a_log", "d_skip", "ssm_norm_w", "q_norm_w", "kv_norm_w", "norm_mlp_w"]


def _unshard_layer(name, g):
    _, r, c = g.shape
    if name in COL_SHARDED:
        return jnp.transpose(g, (1, 0, 2)).reshape(r, 4 * c)
    return g.reshape(4 * r, c)


def _to_shards(name, full):
    r, c = full.shape
    if name in COL_SHARDED:
        return jnp.transpose(full.reshape(r, 4, c // 4), (1, 0, 2))
    return full.reshape(4, r // 4, c)


REST = [k for k in BIG if k != "w_in"]


def prep_layer(cfg, w):
    out = {}
    if "w_in" in w:
        sp = np.cumsum(cfg.in_splits)[:-1].tolist()
        z, xbc, dt, cq, ckv, kr, gs, gm = jnp.split(w["w_in"], sp, axis=1)
        zpad = lambda n: jnp.zeros((cfg.d, n), z.dtype)
        out.update(w_z=z, w_xbc=xbc, w_g=jnp.concatenate([gs, gm], axis=1),
                   w_s=jnp.concatenate([cq, ckv, kr, zpad(LANE - cfg.rope), dt, zpad(LANE - cfg.heads)], axis=1))
    if "w_uq" in w:
        out.update(
            w_uq=jnp.pad(w["w_uq"].reshape(cfg.ql, cfg.mh, cfg.nope + cfg.rope),
                         ((0, 0), (0, 0), (0, 2 * LANE - cfg.nope - cfg.rope))).reshape(cfg.ql, cfg.qw),
            w_ukv=w["w_ukv"], w_bs=w["w_branch_ssm"], w_bm=w["w_branch_mla"], w_out=w["w_out"],
            w_up=w["w_mlp_up"], w_down=w["w_mlp_down"])
    return {k: v.astype(BF16) for k, v in out.items()}


def unprep_grads(cfg, g):
    out = {}
    if "w_s" in g:
        ql, kvl = cfg.ql, cfg.kvl
        ds_ = g["w_s"]
        cq, ckv = ds_[:, :ql], ds_[:, ql:ql + kvl]
        kr = ds_[:, ql + kvl:ql + kvl + cfg.rope]
        dt = ds_[:, ql + kvl + LANE:ql + kvl + LANE + cfg.heads]
        out["w_in"] = jnp.concatenate([g["w_z"], g["w_xbc"], dt, cq, ckv, kr, g["w_g"]], axis=1)
    if "w_uq" in g:
        out.update(
            w_uq=g["w_uq"].reshape(cfg.ql, cfg.mh, 2 * LANE)[:, :, :cfg.nope + cfg.rope].reshape(cfg.ql, -1),
            w_ukv=g["w_ukv"], w_branch_ssm=g["w_bs"], w_branch_mla=g["w_bm"],
            w_out=g["w_out"], w_mlp_up=g["w_up"], w_mlp_down=g["w_down"])
    return out


def layer_fwd(cfg, h, pw, sm, tabs, li, rest=None):
    n = lambda s: f"l{li}_{s}"
    u = rmsnorm_fwd(h, sm["norm_mix_w"], name=n("norm_mix"))
    z = matmul(u, pw["w_z"], out_dtype=BF16, name=n("in_z"))
    xbc = matmul(u, pw["w_xbc"], name=n("in_xbc"))
    g = matmul(u, pw["w_g"], out_dtype=BF16, name=n("in_g"))
    small = matmul(u, pw["w_s"], name=n("in_s"))
    xc = conv_fwd(cfg, xbc, sm["conv_w"], sm["conv_b"], name=n("conv"))
    y, sin = ssd_fwd(cfg, xc, small, sm["dt_bias_p"], sm["avec"], sm["dexp"], name=n("ssd"))
    y_ssm = tail_fwd(cfg, y, z, sm["ssm_norm_w"], name=n("tail"))
    if rest is not None:
        pw = dict(pw, **rest(y_ssm))
    cqn = rmsnorm_fwd(small, sm["q_norm_w"], cw=cfg.ql, ci=0, name=n("q_norm"))
    ckvn = rmsnorm_fwd(small, sm["kv_norm_w"], cw=cfg.kvl, ci=cfg.ql // cfg.kvl, name=n("kv_norm"))
    qf = matmul(cqn, pw["w_uq"], name=n("uq"))
    kv = matmul(ckvn, pw["w_ukv"], out_dtype=BF16, name=n("ukv"))
    qr, kpe = rope_fwd(cfg, qf, small, tabs, name=n("rope"))
    o, lse = attn_fwd(cfg, qr, kv, kpe, name=n("attn"))
    ya = matmul(y_ssm, pw["w_bs"], out_dtype=BF16, name=n("branch_ssm"))
    yb = matmul(o, pw["w_bm"], out_dtype=BF16, name=n("branch_mla"))
    mixed = gate_fwd(cfg, ya, yb, g, name=n("gate"))
    h1 = matmul(mixed, pw["w_out"], add=h, name=n("out"))
    v = rmsnorm_fwd(h1, sm["norm_mlp_w"], name=n("norm_mlp"))
    a, act = matmul(v, pw["w_up"], name=n("up"), epilogue=_ep_relu2, out_dtypes=(BF16, BF16))
    h2 = matmul(act, pw["w_down"], add=h1, name=n("down"))
    saved = dict(h=h, u=u, z=z, xbc=xbc, g=g, small=small, xc=xc, y=y, sin=sin, y_ssm=y_ssm, cqn=cqn, ckvn=ckvn,
                 qr=qr, kv=kv, kpe=kpe, o=o, lse=lse, ya=ya, yb=yb, mixed=mixed, h1=h1, v=v, a=a, act=act)
    return h2, saved, pw


def layer_bwd(cfg, dh2, pw, sm, tabs, s, li, early=None):
    n = lambda t: f"l{li}_b_{t}"
    gw, gs = {}, {}
    gw["w_down"] = matmul(s["act"], dh2, ta=True, name=n("dw_down"))
    da = matmul(dh2, pw["w_down"], tb=True, name=n("dact"), epilogue=_ep_relu2_grad, extras=(s["a"],),
                out_dtypes=(BF16,))
    gw["w_up"] = matmul(s["v"], da, ta=True, name=n("dw_up"))
    dv = matmul(da, pw["w_up"], tb=True, name=n("dv"))
    dh1, gs["norm_mlp_w"] = rmsnorm_bwd(dv, s["h1"], sm["norm_mlp_w"], res=dh2, name=n("norm_mlp"))
    gw["w_out"] = matmul(s["mixed"], dh1, ta=True, name=n("dw_out"))
    dmix = matmul(dh1, pw["w_out"], tb=True, name=n("dmix"))
    dya, dyb, dg = gate_bwd(cfg, dmix, s["ya"], s["yb"], s["g"], name=n("gate"))
    gw["w_bs"] = matmul(s["y_ssm"], dya, ta=True, name=n("dw_bs"))
    gw["w_bm"] = matmul(s["o"], dyb, ta=True, name=n("dw_bm"))
    dy_ssm = matmul(dya, pw["w_bs"], tb=True, name=n("dy_ssm"))
    do = matmul(dyb, pw["w_bm"], tb=True, name=n("do"))
    dq, dkv, dkpe = attn_bwd(cfg, s["qr"], s["kv"], s["kpe"], s["o"], s["lse"], do, name=n("attn"))
    dqf, dkr = rope_bwd(cfg, dq, dkpe, tabs, name=n("rope"))
    gw["w_uq"] = matmul(s["cqn"], dqf, ta=True, name=n("dw_uq"))
    gw["w_ukv"] = matmul(s["ckvn"], dkv, ta=True, name=n("dw_ukv"))
    dcqn = matmul(dqf, pw["w_uq"], tb=True, name=n("dcqn"))
    dckvn = matmul(dkv, pw["w_ukv"], tb=True, name=n("dckvn"))
    dcq, gs["q_norm_w"] = rmsnorm_bwd(dcqn, s["small"], sm["q_norm_w"], cw=cfg.ql, ci=0, out_dtype=BF16, name=n("q_norm"))
    dckv, gs["kv_norm_w"] = rmsnorm_bwd(dckvn, s["small"], sm["kv_norm_w"], cw=cfg.kvl, ci=cfg.ql // cfg.kvl,
                                        out_dtype=BF16, name=n("kv_norm"))
    ssm_norm_w = sm["ssm_norm_w"]
    if early is not None:
        ssm_norm_w = ssm_norm_w + early(dict(gw))[0, 0]
    dy, dz, gs["ssm_norm_w"] = tail_bwd(cfg, dy_ssm, s["y"], s["z"], ssm_norm_w, name=n("tail"))
    dxc, ddt, ddexp, dav, dbias = ssd_bwd(cfg, s["xc"], s["small"], sm["dt_bias_p"], sm["avec"], sm["dexp"],
                                          s["sin"], dy, name=n("ssd"))
    dxbc, gs["conv_w"], gs["conv_b"] = conv_bwd(cfg, s["xbc"], sm["conv_w"], sm["conv_b"], dxc, name=n("conv"))
    gs["d_skip"] = ddexp.reshape(cfg.heads, cfg.hd).sum(axis=1)
    gs["a_log"] = (dav[0] * sm["avec"][0])[:cfg.heads]
    gs["dt_bias"] = dbias[0, :cfg.heads]
    dsmall = jnp.concatenate([dcq, dckv, dkr.astype(BF16), ddt.astype(BF16)], axis=1)
    gw["w_z"] = matmul(s["u"], dz, ta=True, name=n("dw_z"))
    gw["w_xbc"] = matmul(s["u"], dxbc, ta=True, name=n("dw_xbc"))
    gw["w_g"] = matmul(s["u"], dg, ta=True, name=n("dw_g"))
    gw["w_s"] = matmul(s["u"], dsmall, ta=True, name=n("dw_s"))
    du = matmul(dz, pw["w_z"], tb=True, name=n("du_z"))
    du = matmul(dxbc, pw["w_xbc"], tb=True, add=du, name=n("du_xbc"))
    du = matmul(dg, pw["w_g"], tb=True, add=du, name=n("du_g"))
    du = matmul(dsmall, pw["w_s"], tb=True, add=du, name=n("du_s"))
    dh, gs["norm_mix_w"] = rmsnorm_bwd(du, s["h"], sm["norm_mix_w"], res=dh1, name=n("norm_mix"))
    return dh, gw, gs


def small_params(cfg, p, li):
    pad_l = lambda v: jnp.pad(v, (0, LANE - v.shape[0])).reshape(1, LANE)
    return dict(
        norm_mix_w=p["norm_mix_w"][li], conv_w=p["conv_w"][li], conv_b=p["conv_b"][li],
        dt_bias_p=pad_l(p["dt_bias"][li]), avec=pad_l(-jnp.exp(p["a_log"][li])),
        dexp=jnp.repeat(p["d_skip"][li], cfg.hd).reshape(1, cfg.inner),
        ssm_norm_w=p["ssm_norm_w"][li], q_norm_w=p["q_norm_w"][li], kv_norm_w=p["kv_norm_w"][li],
        norm_mlp_w=p["norm_mlp_w"][li])


def local_step(cfg, x, target, p, depth=2):
    bsz, d = cfg.bsz, cfg.d
    lead = jnp.zeros((bsz, cfg.pad, d), F32)
    meta = jnp.broadcast_to(p["meta_tokens"][None], (bsz, cfg.n_meta, d))
    h = jnp.concatenate([lead, meta, x], axis=1).reshape(cfg.t, d)
    tabs = rope_tables(cfg)
    saved, sms = [], []
    for li in range(depth):
        sm = small_params(cfg, p, li)
        h, s, _ = layer_fwd(cfg, h, p["pw"][li], sm, tabs, li)
        saved.append(s)
        sms.append(sm)
    loss, dh, dfw = loss_head(cfg, h, target.reshape(bsz * cfg.seq, d), p["final_norm_w"], name="loss_head")
    gws, gss = [None] * depth, [None] * depth
    for li in reversed(range(depth)):
        dh, gws[li], gss[li] = layer_bwd(cfg, dh, p["pw"][li], sms[li], tabs, saved[li], li)
    dh = dh.reshape(bsz, cfg.lp, d)
    grad_x = dh[:, cfg.chunk:, :]
    gmeta = jnp.sum(dh[:, cfg.pad:cfg.chunk, :], axis=0)
    return loss, grad_x, gmeta, gws, gss, dfw


def _pack_small(parts):
    flat = jnp.concatenate([a.reshape(-1) for a in parts])
    n = flat.shape[0]
    npad = -n % (8 * LANE)
    return jnp.pad(flat, (0, npad)).reshape(-1, LANE), n


def _unpack_small(vec, shapes):
    flat = vec.reshape(-1)
    out, off = [], 0
    for sh in shapes:
        sz = int(np.prod(sh))
        out.append(flat[off:off + sz].reshape(sh))
        off += sz
    return out


def _as2d(a):
    return a.reshape(-1, a.shape[-1])


def kernel(x, meta_tokens, norm_mix_w, w_in, conv_w, conv_b, dt_bias, a_log, d_skip, ssm_norm_w, q_norm_w, kv_norm_w, w_uq, w_ukv, w_branch_ssm, w_branch_mla, w_out, norm_mlp_w, w_mlp_up, w_mlp_down, final_norm_w, loss_target, m_meta_tokens, m_norm_mix_w, m_w_in, m_conv_w, m_conv_b, m_dt_bias, m_a_log, m_d_skip, m_ssm_norm_w, m_q_norm_w, m_kv_norm_w, m_w_uq, m_w_ukv, m_w_branch_ssm, m_w_branch_mla, m_w_out, m_norm_mlp_w, m_w_mlp_up, m_w_mlp_down, m_final_norm_w, v_meta_tokens, v_norm_mix_w, v_w_in, v_conv_w, v_conv_b, v_dt_bias, v_a_log, v_d_skip, v_ssm_norm_w, v_q_norm_w, v_kv_norm_w, v_w_uq, v_w_ukv, v_w_branch_ssm, v_w_branch_mla, v_w_out, v_norm_mlp_w, v_w_mlp_up, v_w_mlp_down, v_final_norm_w):
    cfg = CFG
    names = ["meta_tokens", "norm_mix_w", "w_in", "conv_w", "conv_b", "dt_bias", "a_log", "d_skip", "ssm_norm_w",
             "q_norm_w", "kv_norm_w", "w_uq", "w_ukv", "w_branch_ssm", "w_branch_mla", "w_out", "norm_mlp_w",
             "w_mlp_up", "w_mlp_down", "final_norm_w"]
    wts = dict(zip(names, [meta_tokens, norm_mix_w, w_in, conv_w, conv_b, dt_bias, a_log, d_skip, ssm_norm_w,
                           q_norm_w, kv_norm_w, w_uq, w_ukv, w_branch_ssm, w_branch_mla, w_out, norm_mlp_w,
                           w_mlp_up, w_mlp_down, final_norm_w]))
    ms = dict(zip(names, [m_meta_tokens, m_norm_mix_w, m_w_in, m_conv_w, m_conv_b, m_dt_bias, m_a_log, m_d_skip,
                          m_ssm_norm_w, m_q_norm_w, m_kv_norm_w, m_w_uq, m_w_ukv, m_w_branch_ssm, m_w_branch_mla,
                          m_w_out, m_norm_mlp_w, m_w_mlp_up, m_w_mlp_down, m_final_norm_w]))
    vs = dict(zip(names, [v_meta_tokens, v_norm_mix_w, v_w_in, v_conv_w, v_conv_b, v_dt_bias, v_a_log, v_d_skip,
                          v_ssm_norm_w, v_q_norm_w, v_kv_norm_w, v_w_uq, v_w_ukv, v_w_branch_ssm, v_w_branch_mla,
                          v_w_out, v_norm_mlp_w, v_w_mlp_up, v_w_mlp_down, v_final_norm_w]))
    cx, cy, cc = _coords()
    chip = 2 * cx + cy

    half1 = jnp.reshape(cc, (1,)).astype(jnp.int32)
    where2 = jnp.stack([chip, cc]).astype(jnp.int32)
    wb = {k: wts[k].astype(BF16) for k in BIG}
    zero_tok = jnp.zeros((8, LANE), F32)

    def halves(a):
        return a.reshape((2, a.shape[0] // 2) + a.shape[1:])

    def gather_start(li, keys, tag, after):
        srcs = [halves(wb[k][li]) for k in keys]
        lands = [lax.empty((4,) + s.shape, BF16) for s in srcs]
        return ici_start("gather", srcs, lands, after, name=f"gather{li}{tag}_start")

    def gather_finish(li, keys, tag, started, after):
        srcs, lands = ici_wait("gather", started, after, name=f"gather{li}{tag}_wait")
        lands = pair_share(lands, name=f"gather{li}{tag}_share")
        full = {}
        for k, own, land in zip(keys, srcs, lands):
            slots = [jnp.where(chip == j, own, land[j]) for j in range(4)]
            full[k] = _unshard_layer(k, jnp.stack(slots).reshape((4, 2 * own.shape[1], own.shape[2])))
        return prep_layer(cfg, full)

    def reduce_start(li, keys, tag, gw, after):
        ug = unprep_grads(cfg, gw)
        g4 = []
        for k in keys:
            s = _to_shards(k, ug[k])
            g4.append(s.reshape(4, 2, s.shape[1] // 2, s.shape[2]))
        theirs = pair_exchange(g4, name=f"grad{li}{tag}_pair_exchange")
        parts = [pair_add(a, b, half1, name=f"grad{li}_pair_add_{k}") for k, a, b in zip(keys, g4, theirs)]
        lands = [lax.empty(q.shape, q.dtype) for q in parts]
        return ici_start("scatter", parts, lands, after, name=f"grad{li}{tag}_scatter_start")

    def reduce_finish(li, keys, tag, started, after):
        parts, lands = ici_wait("scatter", started, after, name=f"grad{li}{tag}_scatter_wait")
        sums = [chip_sum(rc, pt, where2, name=f"grad{li}_chip_sum_{k}") for k, rc, pt in zip(keys, lands, parts)]
        sums = pair_fill(sums, name=f"grad{li}{tag}_pair_fill")
        return {k: s.reshape(2 * s.shape[1], s.shape[2]) for k, s in zip(keys, sums)}

    gathered = gather_chips([meta_tokens, conv_w], name="gather_small")
    p = dict(wts)
    p["meta_tokens"] = jnp.transpose(gathered[0], (1, 0, 2)).reshape(cfg.n_meta, cfg.d)
    p["conv_w"] = jnp.transpose(gathered[1], (1, 2, 0, 3)).reshape(2, cfg.convk, cfg.conv_dim)

    st0a = gather_start(0, ["w_in"], "a", gathered[0])
    st0b = gather_start(0, REST, "b", st0a[4])
    st1 = gather_start(1, BIG, "", st0b[4])
    pw0 = gather_finish(0, ["w_in"], "a", st0a, st1[4])

    bsz, d = cfg.bsz, cfg.d
    lead = jnp.zeros((bsz, cfg.pad, d), F32)
    meta = jnp.broadcast_to(p["meta_tokens"][None], (bsz, cfg.n_meta, d))
    h0 = jnp.concatenate([lead, meta, x], axis=1).reshape(cfg.t, d)
    tabs = rope_tables(cfg)
    sm0 = small_params(cfg, p, 0)
    h1, sv0, pw0 = layer_fwd(cfg, h0, pw0, sm0, tabs, 0,
                             rest=lambda after: gather_finish(0, REST, "b", st0b, after))
    pw1 = gather_finish(1, BIG, "", st1, h1)
    sm1 = small_params(cfg, p, 1)
    h2, sv1, _ = layer_fwd(cfg, h1, pw1, sm1, tabs, 1)
    loss, dh, dfw = loss_head(cfg, h2, loss_target.reshape(bsz * cfg.seq, d), final_norm_w, name="loss_head")
    loss = lax.psum(loss, ("x", "y", "c"))

    dh, gw1, gs1 = layer_bwd(cfg, dh, pw1, sm1, tabs, sv1, 1)
    red1 = reduce_start(1, BIG, "", gw1, zero_tok)
    sm0b = dict(sm0)
    sm0b["norm_mlp_w"] = sm0["norm_mlp_w"] + red1[4][0, 0]
    early = {}

    def start_early(gw):
        early["st"] = reduce_start(0, REST, "e", gw, zero_tok)
        return early["st"][4]

    dh, gw0, gs0 = layer_bwd(cfg, dh, pw0, sm0b, tabs, sv0, 0, early=start_early)
    dh3 = dh.reshape(bsz, cfg.lp, d)
    grad_x = dh3[:, cfg.chunk:, :]
    gmeta = jnp.sum(dh3[:, cfg.pad:cfg.chunk, :], axis=0)
    big1 = reduce_finish(1, BIG, "", red1, dh)

    small_names = SMALL_REPL + ["conv_w"]
    parts = [jnp.stack([gs0[k], gs1[k]]) for k in small_names] + [dfw, gmeta]
    shapes = [a.shape for a in parts]
    vec, _ = _pack_small(parts)
    red_vec = allreduce_small(vec, big1[BIG[-1]], name="allreduce_small")
    red = _unpack_small(red_vec, shapes)
    sg = dict(zip(small_names + ["final_norm_w", "meta_tokens"], red))
    sg["conv_w"] = lax.dynamic_slice_in_dim(sg["conv_w"], chip * (cfg.conv_dim // 4), cfg.conv_dim // 4, axis=2)
    sg["meta_tokens"] = lax.dynamic_slice_in_dim(sg["meta_tokens"], chip * (cfg.d // 4), cfg.d // 4, axis=1)

    red0 = reduce_start(0, ["w_in"], "l", gw0, red_vec)
    grads, deltas, new_m, new_v = {}, {}, {}, {}
    dep = red0[4]
    for k in names:
        if k in BIG:
            continue
        w2, g2, m2, v2 = _as2d(wts[k]), _as2d(sg[k]), _as2d(ms[k]), _as2d(vs[k])
        dl, mn, vn = adamw_small(w2, g2, m2, v2, dep, name=f"adamw_{k}")
        grads[k] = sg[k].reshape(wts[k].shape)
        deltas[k], new_m[k], new_v[k] = (t.reshape(wts[k].shape) for t in (dl, mn, vn))

    def view(k, a):
        return jnp.swapaxes(a, 1, 2) if k == "w_in" else a

    def gview(k, g):
        return g.T if k == "w_in" else g

    wv, mv, vv = ({k: view(k, t[k]) for k in BIG} for t in (wts, ms, vs))
    outs = {}
    for k in BIG:
        outs[k] = adamw_layer(wv[k], mv[k], vv[k], gview(k, big1[k]), 1, None, dep, name=f"adamw1_{k}")
        dep = outs[k][1]
    big0 = reduce_finish(0, REST, "e", early["st"], dep)
    for k in REST:
        outs[k] = adamw_layer(wv[k], mv[k], vv[k], big0[k], 0, outs[k], dep, name=f"adamw0_{k}")
        dep = outs[k][1]
    big0.update(reduce_finish(0, ["w_in"], "l", red0, dep))
    outs["w_in"] = adamw_layer(wv["w_in"], mv["w_in"], vv["w_in"], gview("w_in", big0["w_in"]), 0, outs["w_in"], dep,
                               name="adamw0_w_in")
    for k in BIG:
        grads[k], deltas[k], new_m[k], new_v[k] = (view(k, t) for t in outs[k])
    return (loss, grad_x, *[grads[k] for k in names], *[deltas[k] for k in names],
            *[new_m[k] for k in names], *[new_v[k] for k in names])


def adamw_small(w, g, m, v, dep, *, name):
    def body(w_ref, g_ref, m_ref, v_ref, dep_ref, d_ref, mo_ref, vo_ref):
        d_ref[...], mo_ref[...], vo_ref[...] = _adam_update(w_ref[...], g_ref[...], m_ref[...], v_ref[...])

    vm = pl.BlockSpec(memory_space=pltpu.VMEM)
    return pl.pallas_call(body, name=name, in_specs=[vm] * 4 + [pl.BlockSpec(memory_space=pl.ANY)], out_specs=[vm] * 3,
                          out_shape=[_sds(w.shape, F32)] * 3, compiler_params=_cp())(w, g, m, v, dep)
```

```python
import functools
import math
from typing import NamedTuple

import numpy as np
import jax
import jax.numpy as jnp
from jax import lax
from jax.experimental import pallas as pl
from jax.experimental.pallas import tpu as pltpu

F32 = jnp.float32
BF16 = jnp.bfloat16
HI = lax.Precision.HIGHEST
EPS = 1e-6
ROPE_THETA = 10000.0
LANE = 128
VMEM_LIMIT = 56 * 1024 * 1024
MASK_VALUE = -1e30
ADAM_LR, ADAM_B1, ADAM_B2, ADAM_EPS, ADAM_WD, ADAM_STEP = 0.001, 0.9, 0.999, 1e-08, 0.01, 10
MESH = pl.DeviceIdType.MESH


class Cfg(NamedTuple):
    d: int = 1024
    seq: int = 2048
    bsz: int = 2
    n_meta: int = 16
    inner: int = 2048
    hd: int = 64
    groups: int = 4
    state: int = 128
    convk: int = 4
    chunk: int = 128
    mh: int = 8
    ql: int = 512
    kvl: int = 256
    nope: int = 128
    rope: int = 64
    vd: int = 128
    ff: int = 4096

    @property
    def heads(self): return self.inner // self.hd
    @property
    def gw(self): return self.inner // self.groups
    @property
    def conv_dim(self): return self.inner + 2 * self.groups * self.state
    @property
    def pad(self): return self.chunk - self.n_meta
    @property
    def lp(self): return self.chunk + self.seq
    @property
    def t(self): return self.bsz * self.lp
    @property
    def nchunks(self): return self.lp // self.chunk
    @property
    def sw(self): return self.ql + self.kvl + 2 * LANE
    @property
    def kt(self): return (self.ql + self.kvl) // LANE
    @property
    def dtt(self): return self.kt + 1
    @property
    def qw(self): return self.mh * 2 * LANE
    @property
    def in_splits(self):
        return [self.inner, self.conv_dim, self.heads, self.ql, self.kvl, self.rope, self.d, self.d]


CFG = Cfg()


def _pick(dim, pref, mult):
    best = None
    for t in range(mult, min(dim, pref) + 1, mult):
        if dim % t == 0:
            best = t
    return best if best is not None else dim


def _cp(**kw):
    return pltpu.CompilerParams(vmem_limit_bytes=VMEM_LIMIT, **kw)


def _sds(shape, dtype):
    return jax.ShapeDtypeStruct(tuple(shape), dtype)


def _silu(x):
    return x * jax.nn.sigmoid(x)


def _dsilu(x):
    s = jax.nn.sigmoid(x)
    return s * (1.0 + x * (1.0 - s))


def _ep_plain(r):
    return (r,)


def _ep_add(r, res):
    return (r + res.astype(F32),)


def _ep_relu2(r):
    rp = jnp.maximum(r, 0.0)
    return r, rp * rp


def _ep_relu2_grad(r, a):
    return (r * (2.0 * jnp.maximum(a.astype(F32), 0.0)),)


MM_VMEM_BUDGET = 44 * 1024 * 1024


def _mm_tiles(m, n, k, a_bytes, b_bytes, io_bytes, ta):
    m_mult, m_cap = (LANE, 1024) if ta else (16, 1088)
    tms = [t for t in range(m_cap, 0, -m_mult) if m % t == 0] or [m]
    tns = [t for t in (1024, 512, 256, 128) if n % t == 0] or [n]
    best = None
    for tm in tms:
        for tn in tns:
            need = 2 * (tm * k * a_bytes + k * tn * b_bytes + tm * tn * io_bytes)
            if need <= MM_VMEM_BUDGET and (best is None or tm * tn > best[0] * best[1]):
                best = (tm, tn)
    if best is None:
        return (_pick(m, 512, m_mult), _pick(n, 512, LANE), _pick(k, 1088 if ta else 1024, 16 if ta else LANE))
    return best[0], best[1], k


def matmul(a, b, *, ta=False, tb=False, out_dtype=F32, add=None, name, tm=None, tn=None, tk=None,
           epilogue=None, extras=(), out_dtypes=None):
    if add is not None:
        epilogue, extras = _ep_add, (add,)
    if epilogue is None:
        epilogue = _ep_plain
    out_dtypes = tuple(out_dtypes) if out_dtypes is not None else (out_dtype,)
    n_ex, n_out = len(extras), len(out_dtypes)
    if ta:
        k_dim, m_dim = a.shape
    else:
        m_dim, k_dim = a.shape
    if tb:
        n_dim, k2 = b.shape
    else:
        k2, n_dim = b.shape
    assert k_dim == k2, (a.shape, b.shape, ta, tb)
    if tm is None and tn is None and tk is None:
        io_bytes = sum(jnp.dtype(e.dtype).itemsize for e in extras) + sum(jnp.dtype(d).itemsize for d in out_dtypes)
        tm, tn, tk = _mm_tiles(m_dim, n_dim, k_dim, jnp.dtype(a.dtype).itemsize, jnp.dtype(b.dtype).itemsize,
                               io_bytes, ta)
    elif ta:
        tm = tm or _pick(m_dim, 1024, LANE)
        tk = tk or _pick(k_dim, 1088, 16)
        tn = tn or _pick(n_dim, 1024, LANE)
    else:
        tm = tm or _pick(m_dim, 1088, 16)
        tk = tk or _pick(k_dim, 1024 if a.dtype == F32 else 2048, LANE)
        tn = tn or _pick(n_dim, 1024, LANE)
    nm, nn, nk = m_dim // tm, n_dim // tn, k_dim // tk
    dn = (((0 if ta else 1,), (1 if tb else 0,)), ((), ()))

    def body(*refs):
        a_ref, b_ref = refs[:2]
        ex_refs = refs[2:2 + n_ex]
        o_refs = refs[2 + n_ex:2 + n_ex + n_out]
        scr = refs[2 + n_ex + n_out:]
        p = lax.dot_general(a_ref[...].astype(BF16), b_ref[...].astype(BF16), dn, preferred_element_type=F32)

        def finish(r):
            outs = epilogue(r, *[e[...] for e in ex_refs])
            for o_ref, val, dt in zip(o_refs, outs, out_dtypes):
                o_ref[...] = val.astype(dt)

        if nk == 1:
            finish(p)
        else:
            acc = scr[0]
            k = pl.program_id(2)

            @pl.when(k == 0)
            def _():
                acc[...] = p

            @pl.when(k > 0)
            def _():
                acc[...] += p

            @pl.when(k == nk - 1)
            def _():
                finish(acc[...])

    a_spec = pl.BlockSpec((tk, tm), lambda i, j, k: (k, i)) if ta else pl.BlockSpec((tm, tk), lambda i, j, k: (i, k))
    b_spec = pl.BlockSpec((tn, tk), lambda i, j, k: (j, k)) if tb else pl.BlockSpec((tk, tn), lambda i, j, k: (k, j))
    o_spec = pl.BlockSpec((tm, tn), lambda i, j, k: (i, j))
    outs = pl.pallas_call(
        body, name=name, grid=(nm, nn, nk), in_specs=[a_spec, b_spec] + [o_spec] * n_ex, out_specs=[o_spec] * n_out,
        out_shape=[_sds((m_dim, n_dim), dt) for dt in out_dtypes],
        scratch_shapes=[pltpu.VMEM((tm, tn), F32)] if nk > 1 else [],
        compiler_params=_cp(dimension_semantics=("parallel", "parallel", "arbitrary")),
    )(a, b, *extras)
    return outs[0] if n_out == 1 else tuple(outs)


def rmsnorm_fwd(x, w, *, cw=None, ci=0, name):
    t = x.shape[0]
    cw = cw or x.shape[1]
    tr = _pick(t, 544, 16)

    def body(x_ref, w_ref, o_ref):
        xv = x_ref[...].astype(F32)
        r = lax.rsqrt(jnp.mean(xv * xv, axis=-1, keepdims=True) + EPS)
        o_ref[...] = (xv * r * w_ref[...]).astype(BF16)

    return pl.pallas_call(
        body, name=name, grid=(t // tr,),
        in_specs=[pl.BlockSpec((tr, cw), lambda i: (i, ci)), pl.BlockSpec((1, cw), lambda i: (0, 0))],
        out_specs=pl.BlockSpec((tr, cw), lambda i: (i, 0)),
        out_shape=_sds((t, cw), BF16), compiler_params=_cp(),
    )(x, w.reshape(1, cw))


def rmsnorm_bwd(dy, x, w, *, cw=None, ci=0, res=None, out_dtype=F32, name):
    t = x.shape[0]
    cw = cw or x.shape[1]
    tr = _pick(t, 544, 16)
    has_res = res is not None

    def body(*refs):
        if has_res:
            dy_ref, x_ref, w_ref, res_ref, dx_ref, dw_ref = refs
        else:
            dy_ref, x_ref, w_ref, dx_ref, dw_ref = refs
        xv = x_ref[...].astype(F32)
        dyv = dy_ref[...].astype(F32)
        r = lax.rsqrt(jnp.mean(xv * xv, axis=-1, keepdims=True) + EPS)
        xh = xv * r
        g = dyv * w_ref[...]
        dx = r * (g - xh * jnp.mean(g * xh, axis=-1, keepdims=True))
        if has_res:
            dx = dx + res_ref[...]
        dx_ref[...] = dx.astype(out_dtype)

        @pl.when(pl.program_id(0) == 0)
        def _():
            dw_ref[...] = jnp.zeros_like(dw_ref)

        dw_ref[...] += jnp.sum(dyv * xh, axis=0, keepdims=True)

    row = pl.BlockSpec((tr, cw), lambda i: (i, 0))
    in_specs = [row, pl.BlockSpec((tr, cw), lambda i: (i, ci)), pl.BlockSpec((1, cw), lambda i: (0, 0))]
    args = [dy, x, w.reshape(1, cw)]
    if has_res:
        in_specs.append(row)
        args.append(res)
    dx, dw = pl.pallas_call(
        body, name=name, grid=(t // tr,), in_specs=in_specs,
        out_specs=[row, pl.BlockSpec((1, cw), lambda i: (0, 0))],
        out_shape=[_sds((t, cw), out_dtype), _sds((1, cw), F32)], compiler_params=_cp(),
    )(*args)
    return dx, dw[0]


def _shift_down(x, s):
    return x if s == 0 else pltpu.roll(x, s, 0)


def _shift_up(x, s):
    return x if s == 0 else pltpu.roll(x, x.shape[0] - s, 0)


def _conv_pre(x, w_ref, b_ref, kk):
    pre = b_ref[...] + jnp.zeros_like(x)
    for k in range(kk):
        pre = pre + w_ref[k:k + 1, :] * _shift_down(x, kk - 1 - k)
    return pre


def conv_fwd(cfg, xbc, w, b, *, name):
    lp, cd, kk = cfg.lp, cfg.conv_dim, cfg.convk
    assert cfg.pad >= kk - 1
    cb = _pick(cd, 512, LANE)

    def body(x_ref, w_ref, b_ref, o_ref):
        o_ref[...] = _silu(_conv_pre(x_ref[...], w_ref, b_ref, kk))

    blk = pl.BlockSpec((lp, cb), lambda j, bb: (bb, j))
    return pl.pallas_call(
        body, name=name, grid=(cd // cb, cfg.bsz),
        in_specs=[blk, pl.BlockSpec((kk, cb), lambda j, bb: (0, j)), pl.BlockSpec((1, cb), lambda j, bb: (0, j))],
        out_specs=blk, out_shape=_sds((cfg.t, cd), F32), compiler_params=_cp(),
    )(xbc, w, b.reshape(1, cd))


def conv_bwd(cfg, xbc, w, b, dxc, *, name):
    lp, cd, kk = cfg.lp, cfg.conv_dim, cfg.convk
    cb = _pick(cd, 512, LANE)

    def body(x_ref, w_ref, b_ref, d_ref, dx_ref, dw_ref, db_ref):
        x = x_ref[...]
        pre = _conv_pre(x, w_ref, b_ref, kk)
        dpre = d_ref[...] * _dsilu(pre)
        dx = jnp.zeros_like(x)
        dws = []
        for k in range(kk):
            s = kk - 1 - k
            dx = dx + w_ref[k:k + 1, :] * _shift_up(dpre, s)
            dws.append(jnp.sum(dpre * _shift_down(x, s), axis=0, keepdims=True))
        dx_ref[...] = dx.astype(BF16)

        @pl.when(pl.program_id(1) == 0)
        def _():
            dw_ref[...] = jnp.zeros_like(dw_ref)
            db_ref[...] = jnp.zeros_like(db_ref)

        for k in range(kk):
            dw_ref[k:k + 1, :] += dws[k]
        db_ref[...] += jnp.sum(dpre, axis=0, keepdims=True)

    blk = pl.BlockSpec((lp, cb), lambda j, bb: (bb, j))
    wsp = pl.BlockSpec((kk, cb), lambda j, bb: (0, j))
    bsp = pl.BlockSpec((1, cb), lambda j, bb: (0, j))
    dx, dw, db = pl.pallas_call(
        body, name=name, grid=(cd // cb, cfg.bsz),
        in_specs=[blk, wsp, bsp, blk], out_specs=[blk, wsp, bsp],
        out_shape=[_sds((cfg.t, cd), BF16), _sds((kk, cd), F32), _sds((1, cd), F32)], compiler_params=_cp(),
    )(xbc, w, b.reshape(1, cd), dxc)
    return dx, dw, db[0]


def _softplus(x):
    return jnp.maximum(x, 0.0) + jnp.log(1.0 + jnp.exp(-jnp.abs(x)))


def _ssd_consts(cfg):
    q = cfg.chunk
    i0 = np.arange(q)[:, None]
    i1 = np.arange(q)[None, :]
    ltri = (i1 <= i0).astype(np.float32)
    rexp = np.zeros((LANE, cfg.inner), np.float32)
    for h in range(cfg.heads):
        rexp[h, h * cfg.hd:(h + 1) * cfg.hd] = 1.0
    return jnp.asarray(ltri), jnp.asarray(rexp)


def _sel_dot(x, m, *, passes=2, left=False, trans=False):
    mb = m.astype(BF16)
    acc, rem = None, x
    for _ in range(passes):
        piece = rem.astype(BF16)
        if not left:
            part = _nn(piece, mb)
        elif trans:
            part = _tn(mb, piece)
        else:
            part = _nn(mb, piece)
        acc = part if acc is None else acc + part
        rem = rem - piece.astype(F32)
    return acc


def _ssd_chunk_common(cfg, raw, bias, avec, c_idx, ltri, rexp):
    q = cfg.chunk
    rows = lax.broadcasted_iota(jnp.int32, (q, LANE), 0)
    live = jnp.logical_or(c_idx > 0, rows >= cfg.pad)
    pre = raw + bias
    dt = jnp.where(live, _softplus(pre), 0.0)
    adt = dt * avec
    cs = _sel_dot(adt, ltri, passes=3, left=True)
    cs_t = cs.T
    cs_last = cs[q - 1:q, :]
    e_in = jnp.exp(cs)
    w0 = jnp.exp(cs_last - cs)
    decay = jnp.exp(cs_last)
    return dict(live=live, pre=pre, dt=dt, adt=adt, cs=cs, cs_t=cs_t, e_in=e_in, w0=w0, decay=decay,
                DT=_sel_dot(dt, rexp), E=_sel_dot(e_in, rexp), W0=_sel_dot(w0, rexp),
                DEC=_sel_dot(jnp.broadcast_to(decay, (8, LANE)), rexp)[0:1, :])


def _tri_masks(q):
    r = lax.broadcasted_iota(jnp.int32, (q, q), 0)
    c = lax.broadcasted_iota(jnp.int32, (q, q), 1)
    return c <= r, r <= c


def _head_l(cq, h, tri, tri_t):
    col = cq["cs"][:, h:h + 1]
    row = cq["cs_t"][h:h + 1, :]
    lmat = jnp.where(tri, jnp.exp(jnp.minimum(col - row, 0.0)), 0.0)
    lmat_t = jnp.where(tri_t, jnp.exp(jnp.minimum(row - col, 0.0)), 0.0)
    return lmat, lmat_t


def _nt(a, b):
    return lax.dot_general(a, b, (((1,), (1,)), ((), ())), preferred_element_type=F32)


def _tn(a, b):
    return lax.dot_general(a, b, (((0,), (0,)), ((), ())), preferred_element_type=F32)


def _nn(a, b):
    return jnp.dot(a, b, preferred_element_type=F32)


def ssd_fwd(cfg, xc, small, dt_bias, avec, dexp, *, name):
    q, inner, st, gw, g_n = cfg.chunk, cfg.inner, cfg.state, cfg.gw, cfg.groups
    nc = cfg.nchunks
    ltri, rexp = _ssd_consts(cfg)
    hpt = LANE // cfg.hd
    tiles_per_group = gw // LANE

    bsz, lp = cfg.bsz, cfg.lp
    bcw = g_n * st

    def body(x_ref, b_ref, c_ref, dt_ref, bias_ref, a_ref, d_ref, ltri_ref, rexp_ref, y_ref, sin_ref, s_scr):
        c_idx = pl.program_id(0)

        @pl.when(c_idx == 0)
        def _():
            s_scr[...] = jnp.zeros_like(s_scr)

        ltri_v = ltri_ref[...]
        tri, tri_t = _tri_masks(q)
        lane = lax.broadcasted_iota(jnp.int32, (q, LANE), 1)
        for bi in range(bsz):
            cq = _ssd_chunk_common(cfg, dt_ref[bi], bias_ref[...], a_ref[...], c_idx, ltri_v, rexp_ref[...])
            xs = x_ref[bi]
            xdt = (xs * cq["DT"]).astype(BF16)
            xw = (xs * cq["DT"] * cq["W0"]).astype(BF16)
            s_in = s_scr[bi]
            sin_ref[bi, 0] = s_in
            for g in range(g_n):
                bg = b_ref[bi, :, g * st:(g + 1) * st].astype(BF16)
                cg = c_ref[bi, :, g * st:(g + 1) * st].astype(BF16)
                gmat = _nt(cg, bg)
                gs = slice(g * gw, (g + 1) * gw)
                y0 = _nn(cg, s_in[:, gs].astype(BF16))
                for tt in range(tiles_per_group):
                    tile = g * tiles_per_group + tt
                    ts = slice(tile * LANE, (tile + 1) * LANE)
                    xt = xdt[:, ts]
                    ms, xh = [], []
                    for hh in range(hpt):
                        lmat, _ = _head_l(cq, tile * hpt + hh, tri, tri_t)
                        ms.append((gmat * lmat).astype(BF16))
                        inhead = jnp.logical_and(lane >= hh * cfg.hd, lane < (hh + 1) * cfg.hd)
                        xh.append(jnp.where(inhead, xt, jnp.zeros_like(xt)))
                    yd = _nn(jnp.concatenate(ms, axis=1), jnp.concatenate(xh, axis=0))
                    y_ref[bi, :, ts] = (yd + y0[:, tt * LANE:(tt + 1) * LANE] * cq["E"][:, ts]
                                        + xs[:, ts] * d_ref[:, ts])
                s_scr[bi, :, gs] = s_in[:, gs] * cq["DEC"][:, gs] + _tn(bg, xw[:, gs])

    def rowblk(width, col):
        return pl.BlockSpec((bsz, q, width), lambda c: (0, c, col))

    def const(shape):
        return pl.BlockSpec(shape, lambda c: (0, 0))

    xc3 = xc.reshape(bsz, lp, cfg.conv_dim)
    y, sin = pl.pallas_call(
        body, name=name, grid=(nc,),
        in_specs=[rowblk(inner, 0), rowblk(bcw, inner // bcw), rowblk(bcw, inner // bcw + 1),
                  rowblk(LANE, cfg.dtt), const((1, LANE)), const((1, LANE)), const((1, inner)),
                  const((q, q)), const((LANE, inner))],
        out_specs=[rowblk(inner, 0), pl.BlockSpec((bsz, 1, st, inner), lambda c: (0, c, 0, 0))],
        out_shape=[_sds((bsz, lp, inner), F32), _sds((bsz, nc, st, inner), F32)],
        scratch_shapes=[pltpu.VMEM((bsz, st, inner), F32)], compiler_params=_cp(),
    )(xc3, xc3, xc3, small.reshape(bsz, lp, cfg.sw), dt_bias, avec, dexp, ltri, rexp)
    return y.reshape(cfg.t, inner), sin.reshape(bsz * nc, st, inner)


def ssd_bwd(cfg, xc, small, dt_bias, avec, dexp, sin, dy, *, name):
    q, inner, st, gw, g_n = cfg.chunk, cfg.inner, cfg.state, cfg.gw, cfg.groups
    nc = cfg.nchunks
    ltri, rexp = _ssd_consts(cfg)
    rexp_t = rexp.T
    hpt = LANE // cfg.hd
    tiles_per_group = gw // LANE
    bcw = g_n * st

    def body(x_ref, b_ref, c_ref, dt_ref, bias_ref, a_ref, d_ref, ltri_ref, rexp_ref, rexpt_ref, sin_ref, dy_ref,
             dx_ref, ddt_ref, dd_ref, da_ref, dbias_ref, ds_scr):
        step = pl.program_id(1)
        c_idx = nc - 1 - step

        @pl.when(step == 0)
        def _():
            ds_scr[...] = jnp.zeros_like(ds_scr)

        @pl.when(jnp.logical_and(step == 0, pl.program_id(0) == 0))
        def _():
            dd_ref[...] = jnp.zeros_like(dd_ref)
            da_ref[...] = jnp.zeros_like(da_ref)
            dbias_ref[...] = jnp.zeros_like(dbias_ref)

        ltri_v = ltri_ref[...]
        tri, tri_t = _tri_masks(q)
        red = _sel_dot
        rexpt = rexpt_ref[...]
        cq = _ssd_chunk_common(cfg, dt_ref[...], bias_ref[...], a_ref[...], c_idx, ltri_v, rexp_ref[...])
        xs = x_ref[...]
        dyv = dy_ref[...]
        s_in = sin_ref[0]
        d_s = ds_scr[...]
        xdt_f = xs * cq["DT"]
        xdt = xdt_f.astype(BF16)
        xw_f = xdt_f * cq["W0"]
        xw = xw_f.astype(BF16)
        lane = lax.broadcasted_iota(jnp.int32, (q, LANE), 1)
        sub = lax.broadcasted_iota(jnp.int32, (LANE, q), 0)

        dd_ref[...] += jnp.sum(dyv * xs, axis=0, keepdims=True)
        dy0 = dyv * cq["E"]
        dcs = jnp.zeros((q, LANE), F32)
        dcs_t = jnp.zeros((LANE, q), F32)
        for g in range(g_n):
            bg_f = b_ref[:, g * st:(g + 1) * st]
            cg_f = c_ref[:, g * st:(g + 1) * st]
            bg = bg_f.astype(BF16)
            cg = cg_f.astype(BF16)
            gs = slice(g * gw, (g + 1) * gw)
            gmat = _nt(cg, bg)
            gmat_t = _nt(bg, cg)
            sing = s_in[:, gs].astype(BF16)
            dsg = d_s[:, gs].astype(BF16)
            y0 = _nn(cg, sing)
            dxw = _nn(bg, dsg)
            d_bg = _nt(xw[:, gs], dsg)
            d_cg = _nt(dy0[:, gs].astype(BF16), sing)
            ds_in_g = _tn(cg, dy0[:, gs].astype(BF16))
            dg = jnp.zeros((q, q), F32)
            dxdt_g = []
            for tt in range(tiles_per_group):
                tile = g * tiles_per_group + tt
                ts = slice(tile * LANE, (tile + 1) * LANE)
                xt = xdt[:, ts]
                dyt = dyv[:, ts]
                dyhs, lmats, mts = [], [], []
                for hh in range(hpt):
                    lmat, lmat_t = _head_l(cq, tile * hpt + hh, tri, tri_t)
                    inhead = jnp.logical_and(lane >= hh * cfg.hd, lane < (hh + 1) * cfg.hd)
                    dyhs.append(jnp.where(inhead, dyt, 0.0).astype(BF16))
                    lmats.append(lmat)
                    mts.append((gmat_t * lmat_t).astype(BF16))
                dy_stack = jnp.concatenate(dyhs, axis=0)
                dm_all = _nt(dy_stack, xt)
                for hh in range(hpt):
                    h = tile * hpt + hh
                    dm = dm_all[hh * q:(hh + 1) * q, :]
                    dg = dg + dm * lmats[hh]
                    qm = dm * gmat * lmats[hh]
                    rs = jnp.sum(qm, axis=1, keepdims=True)
                    csum = jnp.sum(qm, axis=0, keepdims=True)
                    dcs = dcs + jnp.where(lane == h, rs, 0.0)
                    dcs_t = dcs_t + jnp.where(sub == h, csum, 0.0)
                dxdt_g.append(_nn(jnp.concatenate(mts, axis=1), dy_stack))
            dxdt_diag = jnp.concatenate(dxdt_g, axis=1) if len(dxdt_g) > 1 else dxdt_g[0]
            dgb = dg.astype(BF16)
            d_cg = d_cg + _nn(dgb, bg)
            d_bg = d_bg + _tn(dgb, cg)
            dx_ref[:, inner + g * st:inner + (g + 1) * st] = d_bg
            dx_ref[:, inner + bcw + g * st:inner + bcw + (g + 1) * st] = d_cg
            dxdt = dxdt_diag + dxw * cq["W0"][:, gs]
            dx_ref[:, gs] = dyv[:, gs] * d_ref[:, gs] + dxdt * cq["DT"][:, gs]
            rt = rexpt[gs, :]
            dcs = dcs + red(dyv[:, gs] * y0 * cq["E"][:, gs], rt)
            r_w = red(dxw * xw_f[:, gs], rt)
            dcs = dcs - r_w
            dcs_last_g = jnp.sum(r_w, axis=0, keepdims=True)
            ddec = red(jnp.broadcast_to(jnp.sum(d_s[:, gs] * s_in[:, gs], axis=0, keepdims=True), (8, gw)), rt)[0:1, :]
            dcs_last_g = dcs_last_g + ddec * cq["decay"]
            dcs = dcs + jnp.where(lax.broadcasted_iota(jnp.int32, (q, LANE), 0) == q - 1, dcs_last_g, 0.0)
            ddt_part = red(dxdt * xs[:, gs], rt)
            if g == 0:
                ddt = ddt_part
            else:
                ddt = ddt + ddt_part
            ds_scr[:, gs] = d_s[:, gs] * cq["DEC"][:, gs] + ds_in_g
        dcs = dcs - dcs_t.T
        dadt = _sel_dot(dcs, ltri_v, left=True, trans=True)
        ddt = ddt + dadt * a_ref[...]
        da_ref[...] += jnp.sum(dadt * cq["dt"], axis=0, keepdims=True)
        draw = jnp.where(cq["live"], ddt * jax.nn.sigmoid(cq["pre"]), 0.0)
        ddt_ref[...] = draw
        dbias_ref[...] += jnp.sum(draw, axis=0, keepdims=True)

    def rowblk(width, col):
        return pl.BlockSpec((q, width), lambda b, s: (b * nc + nc - 1 - s, col))

    def const(shape):
        return pl.BlockSpec(shape, lambda b, s: (0, 0))

    bcol = inner // bcw
    outs = pl.pallas_call(
        body, name=name, grid=(cfg.bsz, nc),
        in_specs=[rowblk(inner, 0), rowblk(bcw, bcol), rowblk(bcw, bcol + 1), rowblk(LANE, cfg.dtt),
                  const((1, LANE)), const((1, LANE)), const((1, inner)), const((q, q)), const((LANE, inner)),
                  const((inner, LANE)),
                  pl.BlockSpec((1, st, inner), lambda b, s: (b * nc + nc - 1 - s, 0, 0)), rowblk(inner, 0)],
        out_specs=[rowblk(cfg.conv_dim, 0), rowblk(LANE, 0),
                   const((1, inner)), const((1, LANE)), const((1, LANE))],
        out_shape=[_sds((cfg.t, cfg.conv_dim), F32),
                   _sds((cfg.t, LANE), F32), _sds((1, inner), F32), _sds((1, LANE), F32), _sds((1, LANE), F32)],
        scratch_shapes=[pltpu.VMEM((st, inner), F32)], compiler_params=_cp(),
    )(xc, xc, xc, small, dt_bias, avec, dexp, ltri, rexp, rexp_t, sin, dy)
    return outs


def tail_fwd(cfg, y, z, w, *, name):
    t, inner, gw = cfg.t, cfg.inner, cfg.gw
    tr = _pick(t, 272, 16)

    def body(y_ref, z_ref, w_ref, o_ref):
        for g in range(cfg.groups):
            gs = slice(g * gw, (g + 1) * gw)
            yg = y_ref[:, gs] * _silu(z_ref[:, gs].astype(F32))
            r = lax.rsqrt(jnp.mean(yg * yg, axis=-1, keepdims=True) + EPS)
            o_ref[:, gs] = (yg * r * w_ref[:, gs]).astype(BF16)

    row = pl.BlockSpec((tr, inner), lambda i: (i, 0))
    return pl.pallas_call(
        body, name=name, grid=(t // tr,), in_specs=[row, row, pl.BlockSpec((1, inner), lambda i: (0, 0))],
        out_specs=row, out_shape=_sds((t, inner), BF16), compiler_params=_cp(),
    )(y, z, w.reshape(1, inner))


def tail_bwd(cfg, do, y, z, w, *, name):
    t, inner, gw = cfg.t, cfg.inner, cfg.gw
    tr = _pick(t, 272, 16)

    def body(do_ref, y_ref, z_ref, w_ref, dy_ref, dz_ref, dw_ref):
        @pl.when(pl.program_id(0) == 0)
        def _():
            dw_ref[...] = jnp.zeros_like(dw_ref)

        for g in range(cfg.groups):
            gs = slice(g * gw, (g + 1) * gw)
            yv = y_ref[:, gs]
            zv = z_ref[:, gs].astype(F32)
            dov = do_ref[:, gs]
            sz = _silu(zv)
            yg = yv * sz
            r = lax.rsqrt(jnp.mean(yg * yg, axis=-1, keepdims=True) + EPS)
            xh = yg * r
            gg = dov * w_ref[:, gs]
            dyg = r * (gg - xh * jnp.mean(gg * xh, axis=-1, keepdims=True))
            dw_ref[:, gs] += jnp.sum(dov * xh, axis=0, keepdims=True)
            dy_ref[:, gs] = dyg * sz
            dz_ref[:, gs] = (dyg * yv * _dsilu(zv)).astype(BF16)

    row = pl.BlockSpec((tr, inner), lambda i: (i, 0))
    vec = pl.BlockSpec((1, inner), lambda i: (0, 0))
    dy, dz, dw = pl.pallas_call(
        body, name=name, grid=(t // tr,), in_specs=[row, row, row, vec], out_specs=[row, row, vec],
        out_shape=[_sds((t, inner), F32), _sds((t, inner), BF16), _sds((1, inner), F32)], compiler_params=_cp(),
    )(do, y, z, w.reshape(1, inner))
    return dy, dz, dw[0]


def rope_tables(cfg):
    half = cfg.rope // 2
    pos = np.maximum(np.arange(cfg.lp) - cfg.pad, 0).astype(np.float32)
    inv = ROPE_THETA ** (-jnp.arange(0, cfg.rope, 2, dtype=F32) / cfg.rope)
    ang = jnp.asarray(pos)[:, None] * inv[None, :]
    cos, sin = jnp.cos(ang), jnp.sin(ang)
    zero = jnp.zeros((cfg.lp, LANE - 2 * half), F32)
    zh = jnp.zeros((cfg.lp, half), F32)
    ctab = jnp.concatenate([cos, cos, zero], axis=1)
    s1 = jnp.concatenate([-sin, zh, zero], axis=1)
    s2 = jnp.concatenate([zh, sin, zero], axis=1)
    return ctab, s1, s2


def _rope(x, c, s1, s2, half):
    return x * c + pltpu.roll(x, LANE - half, 1) * s1 + pltpu.roll(x, half, 1) * s2


def _rope_t(dy, c, s1, s2, half):
    return dy * c + pltpu.roll(dy * s1, half, 1) + pltpu.roll(dy * s2, LANE - half, 1)


def _attn_scale(cfg):
    return (cfg.nope + cfg.rope) ** -0.5


def rope_fwd(cfg, qf, small, tabs, *, name):
    t, qw, lp = cfg.t, cfg.qw, cfg.lp
    tr = _pick(lp, 544, 16)
    nrb = lp // tr
    half = cfg.rope // 2
    scale = _attn_scale(cfg)

    def body(q_ref, k_ref, c_ref, s1_ref, s2_ref, qo_ref, ko_ref):
        c, s1, s2 = c_ref[...], s1_ref[...], s2_ref[...]
        for h in range(cfg.mh):
            a = h * 2 * LANE
            qo_ref[:, a:a + LANE] = (q_ref[:, a:a + LANE] * scale).astype(BF16)
            qo_ref[:, a + LANE:a + 2 * LANE] = (_rope(q_ref[:, a + LANE:a + 2 * LANE], c, s1, s2, half) * scale).astype(BF16)
        ko_ref[...] = _rope(k_ref[...], c, s1, s2, half).astype(BF16)

    tab = pl.BlockSpec((tr, LANE), lambda i: (i % nrb, 0))
    return pl.pallas_call(
        body, name=name, grid=(t // tr,),
        in_specs=[pl.BlockSpec((tr, qw), lambda i: (i, 0)), pl.BlockSpec((tr, LANE), lambda i: (i, cfg.kt)), tab, tab, tab],
        out_specs=[pl.BlockSpec((tr, qw), lambda i: (i, 0)), pl.BlockSpec((tr, LANE), lambda i: (i, 0))],
        out_shape=[_sds((t, qw), BF16), _sds((t, LANE), BF16)], compiler_params=_cp(),
    )(qf, small, *tabs)


def rope_bwd(cfg, dq, dkpe, tabs, *, name):
    t, qw, lp = cfg.t, cfg.qw, cfg.lp
    tr = _pick(lp, 544, 16)
    nrb = lp // tr
    half = cfg.rope // 2
    scale = _attn_scale(cfg)

    def body(dq_ref, dk_ref, c_ref, s1_ref, s2_ref, qo_ref, ko_ref):
        c, s1, s2 = c_ref[...], s1_ref[...], s2_ref[...]
        for h in range(cfg.mh):
            a = h * 2 * LANE
            qo_ref[:, a:a + LANE] = (dq_ref[:, a:a + LANE] * scale).astype(BF16)
            qo_ref[:, a + LANE:a + 2 * LANE] = _rope_t(dq_ref[:, a + LANE:a + 2 * LANE] * scale, c, s1, s2, half).astype(BF16)
        dk = dk_ref[0]
        for h in range(1, cfg.mh):
            dk = dk + dk_ref[h]
        ko_ref[...] = _rope_t(dk, c, s1, s2, half)

    tab = pl.BlockSpec((tr, LANE), lambda i: (i % nrb, 0))
    return pl.pallas_call(
        body, name=name, grid=(t // tr,),
        in_specs=[pl.BlockSpec((tr, qw), lambda i: (i, 0)), pl.BlockSpec((cfg.mh, tr, LANE), lambda i: (0, i, 0)),
                  tab, tab, tab],
        out_specs=[pl.BlockSpec((tr, qw), lambda i: (i, 0)), pl.BlockSpec((tr, LANE), lambda i: (i, 0))],
        out_shape=[_sds((t, qw), BF16), _sds((t, LANE), F32)], compiler_params=_cp(),
    )(dq, dkpe, *tabs)


def _q_blocks(cfg):
    bounds = [0, cfg.chunk] + list(range(cfg.chunk + 256, cfg.lp + 1, 256))
    assert bounds[-1] == cfg.lp, "SEQ must be a multiple of 256"
    return list(zip(bounds[:-1], bounds[1:]))


def _attn_mask(cfg, qs, qe):
    rows = qs + lax.broadcasted_iota(jnp.int32, (qe - qs, qe), 0)
    cols = lax.broadcasted_iota(jnp.int32, (qe - qs, qe), 1)
    return jnp.logical_and(cols <= rows, jnp.logical_or(cols >= cfg.pad, rows < cfg.pad))


def _max_q_block(cfg):
    return max(qe - qs for qs, qe in _q_blocks(cfg))


def _masked_scores(cfg, q, k2, qs, qe, s_scr):
    bq, n = qe - qs, qe
    s_scr[0:bq, 0:n] = _nt(q, k2)
    if qs == 0:
        s_scr[0:bq, 0:n] = jnp.where(_attn_mask(cfg, 0, qe), s_scr[0:bq, 0:n], MASK_VALUE)
    else:
        assert qs >= cfg.chunk and cfg.pad < LANE
        cols = lax.broadcasted_iota(jnp.int32, (bq, LANE), 1)
        s_scr[0:bq, 0:LANE] = jnp.where(cols >= cfg.pad, s_scr[0:bq, 0:LANE], MASK_VALUE)
        r = lax.broadcasted_iota(jnp.int32, (bq, bq), 0)
        c = lax.broadcasted_iota(jnp.int32, (bq, bq), 1)
        s_scr[0:bq, qs:qe] = jnp.where(c <= r, s_scr[0:bq, qs:qe], MASK_VALUE)
    return s_scr[0:bq, 0:n]


def attn_fwd(cfg, qr, kv, kpe, *, name):
    lp, t, mh = cfg.lp, cfg.t, cfg.mh
    blocks = _q_blocks(cfg)

    def body(q_ref, kv_ref, kp_ref, o_ref, l_ref, s_scr):
        for qs, qe in blocks:
            n = qe
            q = q_ref[qs:qe, :]
            k2 = jnp.concatenate([kv_ref[0:n, 0:LANE], kp_ref[0:n, :]], axis=1)
            s = _masked_scores(cfg, q, k2, qs, qe, s_scr)
            m = jnp.max(s, axis=-1, keepdims=True)
            p = jnp.exp(s - m)
            l = jnp.sum(p, axis=-1, keepdims=True)
            o_ref[qs:qe, :] = _nn(p.astype(BF16), kv_ref[0:n, LANE:2 * LANE]) * (1.0 / l)
            l_ref[qs:qe, :] = jnp.broadcast_to(m + jnp.log(l), (qe - qs, LANE))

    hb = pl.BlockSpec((lp, 2 * LANE), lambda b, h: (b, h))
    ob = pl.BlockSpec((lp, LANE), lambda b, h: (b, h))
    return pl.pallas_call(
        body, name=name, grid=(cfg.bsz, mh),
        in_specs=[hb, hb, pl.BlockSpec((lp, LANE), lambda b, h: (b, 0))], out_specs=[ob, ob],
        out_shape=[_sds((t, mh * LANE), F32), _sds((t, mh * LANE), F32)],
        scratch_shapes=[pltpu.VMEM((_max_q_block(cfg), lp), F32)], compiler_params=_cp(),
    )(qr, kv, kpe)


def attn_bwd(cfg, qr, kv, kpe, o, lse, do, *, name):
    lp, t, mh = cfg.lp, cfg.t, cfg.mh
    blocks = _q_blocks(cfg)

    def body(q_ref, kv_ref, kp_ref, o_ref, l_ref, do_ref, dq_ref, dkv_ref, dkp_ref, dk_acc, dv_acc, s_scr):
        dk_acc[...] = jnp.zeros_like(dk_acc)
        dv_acc[...] = jnp.zeros_like(dv_acc)
        for qs, qe in blocks:
            n = qe
            q = q_ref[qs:qe, :]
            k2 = jnp.concatenate([kv_ref[0:n, 0:LANE], kp_ref[0:n, :]], axis=1)
            dov = do_ref[qs:qe, :]
            delta = jnp.sum(dov * o_ref[qs:qe, :], axis=-1, keepdims=True)
            dob = dov.astype(BF16)
            s = _masked_scores(cfg, q, k2, qs, qe, s_scr)
            p = jnp.exp(s - l_ref[qs:qe, 0:1])
            dp = _nt(dob, kv_ref[0:n, LANE:2 * LANE])
            ds = (p * (dp - delta)).astype(BF16)
            dq_ref[qs:qe, :] = _nn(ds, k2)
            dv_acc[0:n, :] += _tn(p.astype(BF16), dob)
            dk_acc[0:n, :] += _tn(ds, q)
        dkv_ref[:, 0:LANE] = dk_acc[:, 0:LANE].astype(BF16)
        dkv_ref[:, LANE:2 * LANE] = dv_acc[...].astype(BF16)
        dkp_ref[0] = dk_acc[:, LANE:2 * LANE]

    hb = pl.BlockSpec((lp, 2 * LANE), lambda b, h: (b, h))
    ob = pl.BlockSpec((lp, LANE), lambda b, h: (b, h))
    return pl.pallas_call(
        body, name=name, grid=(cfg.bsz, mh),
        in_specs=[hb, hb, pl.BlockSpec((lp, LANE), lambda b, h: (b, 0)), ob, ob, ob],
        out_specs=[hb, hb, pl.BlockSpec((1, lp, LANE), lambda b, h: (h, b, 0))],
        out_shape=[_sds((t, cfg.qw), F32), _sds((t, mh * 2 * LANE), BF16), _sds((mh, t, LANE), F32)],
        scratch_shapes=[pltpu.VMEM((lp, 2 * LANE), F32), pltpu.VMEM((lp, LANE), F32),
                        pltpu.VMEM((_max_q_block(cfg), lp), F32)], compiler_params=_cp(),
    )(qr, kv, kpe, o, lse, do)


def _live_rows(cfg, tr, shape):
    rows = pl.program_id(1) * tr + lax.broadcasted_iota(jnp.int32, shape, 0)
    return rows >= cfg.pad


def gate_fwd(cfg, ya, yb, g, *, name):
    d, lp = cfg.d, cfg.lp
    tr = _pick(lp, 544, 16)
    nrb = lp // tr

    def body(ya_ref, yb_ref, ga_ref, gb_ref, o_ref):
        f = lambda ref: ref[...].astype(F32)
        mix = jax.nn.sigmoid(f(ga_ref)) * f(ya_ref) + jax.nn.sigmoid(f(gb_ref)) * f(yb_ref)
        o_ref[...] = jnp.where(_live_rows(cfg, tr, mix.shape), mix, 0.0).astype(BF16)

    row = pl.BlockSpec((tr, d), lambda b, j: (b * nrb + j, 0))
    row1 = pl.BlockSpec((tr, d), lambda b, j: (b * nrb + j, 1))
    return pl.pallas_call(
        body, name=name, grid=(cfg.bsz, nrb), in_specs=[row, row, row, row1], out_specs=row,
        out_shape=_sds((cfg.t, d), BF16), compiler_params=_cp(),
    )(ya, yb, g, g)


def gate_bwd(cfg, dmix, ya, yb, g, *, name):
    d, lp = cfg.d, cfg.lp
    tr = _pick(lp, 544, 16)
    nrb = lp // tr

    def body(dm_ref, ya_ref, yb_ref, ga_ref, gb_ref, dya_ref, dyb_ref, dg_ref):
        dm = dm_ref[...]
        dm = jnp.where(_live_rows(cfg, tr, dm.shape), dm, 0.0)
        sa = jax.nn.sigmoid(ga_ref[...].astype(F32))
        sb = jax.nn.sigmoid(gb_ref[...].astype(F32))
        dya_ref[...] = (dm * sa).astype(BF16)
        dyb_ref[...] = (dm * sb).astype(BF16)
        dg_ref[:, 0:d] = (dm * ya_ref[...].astype(F32) * sa * (1.0 - sa)).astype(BF16)
        dg_ref[:, d:2 * d] = (dm * yb_ref[...].astype(F32) * sb * (1.0 - sb)).astype(BF16)

    row = pl.BlockSpec((tr, d), lambda b, j: (b * nrb + j, 0))
    row1 = pl.BlockSpec((tr, d), lambda b, j: (b * nrb + j, 1))
    row2 = pl.BlockSpec((tr, 2 * d), lambda b, j: (b * nrb + j, 0))
    return pl.pallas_call(
        body, name=name, grid=(cfg.bsz, nrb), in_specs=[row, row, row, row, row1], out_specs=[row, row, row2],
        out_shape=[_sds((cfg.t, d), BF16), _sds((cfg.t, d), BF16), _sds((cfg.t, 2 * d), BF16)], compiler_params=_cp(),
    )(dmix, ya, yb, g, g)


def loss_head(cfg, h, target, w, *, name):
    d, q, nc = cfg.d, cfg.chunk, cfg.nchunks
    tpb = cfg.seq // q

    def body(h_ref, t_ref, w_ref, loss_ref, dh_ref, dw_ref):
        j = pl.program_id(1)

        @pl.when(jnp.logical_and(j == 0, pl.program_id(0) == 0))
        def _():
            loss_ref[...] = jnp.zeros_like(loss_ref)
            dw_ref[...] = jnp.zeros_like(dw_ref)

        @pl.when(j == 0)
        def _():
            dh_ref[...] = jnp.zeros_like(dh_ref)

        @pl.when(j > 0)
        def _():
            xv = h_ref[...]
            r = lax.rsqrt(jnp.mean(xv * xv, axis=-1, keepdims=True) + EPS)
            xh = xv * r
            err = xh * w_ref[...] - t_ref[...]
            loss_ref[...] += 0.5 * jnp.sum(jnp.sum(err * err, axis=-1, keepdims=True) / d, axis=0, keepdims=True)
            dy = err * (1.0 / d)
            g = dy * w_ref[...]
            dh_ref[...] = r * (g - xh * jnp.mean(g * xh, axis=-1, keepdims=True))
            dw_ref[...] += jnp.sum(dy * xh, axis=0, keepdims=True)

    row = pl.BlockSpec((q, d), lambda b, j: (b * nc + j, 0))
    loss, dh, dw = pl.pallas_call(
        body, name=name, grid=(cfg.bsz, nc),
        in_specs=[row, pl.BlockSpec((q, d), lambda b, j: (b * tpb + jnp.maximum(j - 1, 0), 0)),
                  pl.BlockSpec((1, d), lambda b, j: (0, 0))],
        out_specs=[pl.BlockSpec((8, LANE), lambda b, j: (0, 0)), row, pl.BlockSpec((1, d), lambda b, j: (0, 0))],
        out_shape=[_sds((8, LANE), F32), _sds((cfg.t, d), F32), _sds((1, d), F32)], compiler_params=_cp(),
    )(h, target, w.reshape(1, d))
    return loss[0, 0], dh, dw[0]


def _rows_tile(r, c):
    return _pick(r, max(8, (1 << 18) // max(c, 1) // 8 * 8), 8)


def _adam_update(w, g, m, v):
    c1 = 1.0 - ADAM_B1 ** ADAM_STEP
    c2 = 1.0 - ADAM_B2 ** ADAM_STEP
    mn = ADAM_B1 * m + (1.0 - ADAM_B1) * g
    vn = ADAM_B2 * v + (1.0 - ADAM_B2) * (g * g)
    delta = -ADAM_LR * ((mn / c1) / (jnp.sqrt(vn / c2) + ADAM_EPS) + ADAM_WD * w)
    return delta, mn, vn


def adamw_layer(w, m, v, g, li, prev, dep, *, name):
    _, r, c = w.shape
    tr = _rows_tile(r, c)

    def body(*refs):
        w_ref, m_ref, v_ref, g_ref = refs[:4]
        go_ref, d_ref, mo_ref, vo_ref = refs[-4:]
        gv = g_ref[...]
        delta, mn, vn = _adam_update(w_ref[0], gv, m_ref[0], v_ref[0])
        go_ref[0] = gv
        d_ref[0] = delta
        mo_ref[0] = mn
        vo_ref[0] = vn

    if tr * c * 4 >= (1 << 16):
        steps = r // tr
        blk3 = pl.BlockSpec((1, tr, c), lambda i: (li, i, 0))
        blk2 = pl.BlockSpec((tr, c), lambda i: (i, 0))
    else:
        tc = _pick(c, max(LANE, (1 << 18) // r // LANE * LANE), LANE)
        steps = c // tc
        blk3 = pl.BlockSpec((1, r, tc), lambda i: (li, 0, i))
        blk2 = pl.BlockSpec((r, tc), lambda i: (0, i))
    anyspec = pl.BlockSpec(memory_space=pl.ANY)
    in_specs = [blk3, blk3, blk3, blk2, anyspec]
    args = [w, m, v, g, dep]
    aliases = {}
    if prev is not None:
        in_specs += [anyspec] * 4
        args += list(prev)
        aliases = {5 + i: i for i in range(4)}
    return pl.pallas_call(
        body, name=name, grid=(steps,), in_specs=in_specs, out_specs=[blk3] * 4,
        out_shape=[_sds(w.shape, F32)] * 4, input_output_aliases=aliases, compiler_params=_cp(),
    )(*args)


def pair_add(g4, other, half, *, name):
    n, _, r, c = g4.shape
    tr = _rows_tile(r, c)

    def body(h_ref, a_ref, b_ref, o_ref):
        o_ref[0] = (a_ref[0, 0].astype(F32) + b_ref[0].astype(F32)).astype(BF16)

    blk = pl.BlockSpec((1, tr, c), lambda j, i, h: (j, i, 0))
    grid_spec = pltpu.PrefetchScalarGridSpec(
        num_scalar_prefetch=1, grid=(n, r // tr),
        in_specs=[pl.BlockSpec((1, 1, tr, c), lambda j, i, h: (j, h[0], i, 0)), blk], out_specs=blk)
    return pl.pallas_call(body, name=name, grid_spec=grid_spec, out_shape=_sds((n, r, c), BF16),
                          compiler_params=_cp())(half, g4, other)


def chip_sum(recv, part, where, *, name):
    n, r, c = recv.shape
    tr = _rows_tile(r, c)

    def body(s_ref, *refs):
        own_ref, o_ref = refs[n], refs[n + 1]
        acc = None
        for j in range(n):
            term = jnp.where(s_ref[0] == j, own_ref[0], refs[j][0]).astype(F32)
            acc = term if acc is None else acc + term
        o_ref[0] = acc

    def slot(j):
        return pl.BlockSpec((1, tr, c), lambda i, s: (jnp.where(s[0] == j, (j + 1) % n, j), i, 0))

    grid_spec = pltpu.PrefetchScalarGridSpec(
        num_scalar_prefetch=1, grid=(r // tr,),
        in_specs=[slot(j) for j in range(n)] + [pl.BlockSpec((1, tr, c), lambda i, s: (s[0], i, 0))],
        out_specs=pl.BlockSpec((1, tr, c), lambda i, s: (s[1], i, 0)))
    return pl.pallas_call(body, name=name, grid_spec=grid_spec, out_shape=_sds((2, r, c), F32),
                          compiler_params=_cp())(where, *([recv] * n), part)


def _coords():
    return lax.axis_index("x"), lax.axis_index("y"), lax.axis_index("c")


def _other_chips(x, y):
    return [(1 - x, y), (x, 1 - y), (1 - x, 1 - y)]


def gather_chips(arrs, *, name):
    n = len(arrs)
    anyspec = pl.BlockSpec(memory_space=pl.ANY)

    def body(*refs):
        ins, outs = refs[:n], refs[n:2 * n]
        send_sems, recv_sems, local_sems = refs[2 * n:]
        x, y, c = _coords()
        me = 2 * x + y
        chips = _other_chips(x, y)
        copies = []
        for k in range(n):
            loc = pltpu.make_async_copy(ins[k], outs[k].at[me], local_sems.at[k])
            loc.start()
            copies.append(loc)
        sends = []
        for k in range(n):
            for j, (px, py) in enumerate(chips):
                cp = pltpu.make_async_remote_copy(
                    src_ref=ins[k], dst_ref=outs[k].at[me], send_sem=send_sems.at[k, j], recv_sem=recv_sems.at[k, j],
                    device_id=(px, py, c), device_id_type=MESH)
                cp.start()
                sends.append(cp)
        for k in range(n):
            for j, (px, py) in enumerate(chips):
                pltpu.make_async_remote_copy(
                    src_ref=ins[k], dst_ref=outs[k].at[2 * px + py], send_sem=send_sems.at[k, j],
                    recv_sem=recv_sems.at[k, j], device_id=(px, py, c), device_id_type=MESH).wait_recv()
        for cp in sends:
            cp.wait_send()
        for cp in copies:
            cp.wait()

    return pl.pallas_call(
        body, name=name, in_specs=[anyspec] * n, out_specs=[anyspec] * n,
        out_shape=[_sds((4,) + a.shape, a.dtype) for a in arrs],
        scratch_shapes=[pltpu.SemaphoreType.DMA((n, 3)), pltpu.SemaphoreType.DMA((n, 3)), pltpu.SemaphoreType.DMA((n,))],
        compiler_params=_cp(has_side_effects=True),
    )(*arrs)


def allreduce_small(vec, after, *, name):
    r, c = vec.shape

    def body(v_ref, after_ref, o_ref, buf, send_sems, recv_sems):
        x, y, cc = _coords()
        me = 4 * x + 2 * y + cc
        buf[me] = v_ref[...]
        sends = []
        flips = [(fx, fy, fc) for fx in (0, 1) for fy in (0, 1) for fc in (0, 1)][1:]
        for j, (fx, fy, fc) in enumerate(flips):
            peer = ((1 - x) if fx else x, (1 - y) if fy else y, (1 - cc) if fc else cc)
            cp = pltpu.make_async_remote_copy(
                src_ref=v_ref, dst_ref=buf.at[me], send_sem=send_sems.at[j], recv_sem=recv_sems.at[j],
                device_id=peer, device_id_type=MESH)
            cp.start()
            sends.append(cp)
        for j, (fx, fy, fc) in enumerate(flips):
            px, py, pc = ((1 - x) if fx else x, (1 - y) if fy else y, (1 - cc) if fc else cc)
            pltpu.make_async_remote_copy(
                src_ref=v_ref, dst_ref=buf.at[4 * px + 2 * py + pc], send_sem=send_sems.at[j],
                recv_sem=recv_sems.at[j], device_id=(px, py, pc), device_id_type=MESH).wait_recv()
        for cp in sends:
            cp.wait_send()
        acc = buf[0]
        for k in range(1, 8):
            acc = acc + buf[k]
        o_ref[...] = acc

    vm = pl.BlockSpec(memory_space=pltpu.VMEM)
    return pl.pallas_call(
        body, name=name, in_specs=[vm, pl.BlockSpec(memory_space=pl.ANY)], out_specs=vm, out_shape=_sds((r, c), F32),
        scratch_shapes=[pltpu.VMEM((8, r, c), F32), pltpu.SemaphoreType.DMA((7,)), pltpu.SemaphoreType.DMA((7,))],
        compiler_params=_cp(has_side_effects=True),
    )(vec, after)


def pair_exchange(arrs, *, name):
    n = len(arrs)
    anyspec = pl.BlockSpec(memory_space=pl.ANY)

    def body(*refs):
        ins, outs = refs[:n], refs[n:2 * n]
        send_sems, recv_sems = refs[2 * n:]
        x, y, c = _coords()
        sends = []
        for k in range(n):
            for j in range(4):
                cp = pltpu.make_async_remote_copy(
                    src_ref=ins[k].at[j, 1 - c], dst_ref=outs[k].at[j], send_sem=send_sems.at[k, j],
                    recv_sem=recv_sems.at[k, j], device_id=(x, y, 1 - c), device_id_type=MESH)
                cp.start()
                sends.append(cp)
        for cp in sends:
            cp.wait()

    return pl.pallas_call(
        body, name=name, in_specs=[anyspec] * n, out_specs=[anyspec] * n,
        out_shape=[_sds((a.shape[0],) + a.shape[2:], a.dtype) for a in arrs],
        scratch_shapes=[pltpu.SemaphoreType.DMA((n, 4)), pltpu.SemaphoreType.DMA((n, 4))],
        compiler_params=_cp(has_side_effects=True),
    )(*arrs)


def pair_share(lands, owns, *, name):
    n = len(lands)
    anyspec = pl.BlockSpec(memory_space=pl.ANY)

    def body(*refs):
        ins, own_refs, outs = refs[:n], refs[n:2 * n], refs[2 * n:3 * n]
        send_sems, recv_sems = refs[3 * n:]
        x, y, c = _coords()
        me = 2 * x + y
        sib = (x, y, 1 - c)
        sends = []
        for k in range(n):
            for j, (px, py) in enumerate(_other_chips(x, y)):
                cp = pltpu.make_async_remote_copy(
                    src_ref=ins[k].at[2 * px + py, c], dst_ref=outs[k].at[2 * px + py, c], send_sem=send_sems.at[k, j],
                    recv_sem=recv_sems.at[k, j], device_id=sib, device_id_type=MESH)
                cp.start()
                sends.append(cp)
            cp = pltpu.make_async_remote_copy(
                src_ref=own_refs[k], dst_ref=outs[k].at[me], send_sem=send_sems.at[k, 3], recv_sem=recv_sems.at[k, 3],
                device_id=sib, device_id_type=MESH)
            cp.start()
            sends.append(cp)
        for k in range(n):
            for j, (px, py) in enumerate(_other_chips(x, y)):
                pltpu.make_async_remote_copy(
                    src_ref=ins[k].at[2 * px + py, c], dst_ref=outs[k].at[2 * px + py, 1 - c],
                    send_sem=send_sems.at[k, j], recv_sem=recv_sems.at[k, j], device_id=sib,
                    device_id_type=MESH).wait_recv()
            pltpu.make_async_remote_copy(
                src_ref=own_refs[k], dst_ref=outs[k].at[me], send_sem=send_sems.at[k, 3], recv_sem=recv_sems.at[k, 3],
                device_id=sib, device_id_type=MESH).wait_recv()
        for cp in sends:
            cp.wait_send()

    return pl.pallas_call(
        body, name=name, in_specs=[anyspec] * (2 * n), out_specs=[anyspec] * n,
        out_shape=[_sds(a.shape, a.dtype) for a in lands], input_output_aliases={k: k for k in range(n)},
        scratch_shapes=[pltpu.SemaphoreType.DMA((n, 4)), pltpu.SemaphoreType.DMA((n, 4))],
        compiler_params=_cp(has_side_effects=True),
    )(*lands, *owns)


def pair_fill(arrs, *, name):
    n = len(arrs)
    anyspec = pl.BlockSpec(memory_space=pl.ANY)

    def body(*refs):
        ins, outs = refs[:n], refs[n:2 * n]
        send_sems, recv_sems = refs[2 * n:]
        x, y, c = _coords()
        sends = []
        for k in range(n):
            cp = pltpu.make_async_remote_copy(
                src_ref=ins[k].at[c], dst_ref=outs[k].at[c], send_sem=send_sems.at[k], recv_sem=recv_sems.at[k],
                device_id=(x, y, 1 - c), device_id_type=MESH)
            cp.start()
            sends.append(cp)
        for k in range(n):
            pltpu.make_async_remote_copy(
                src_ref=ins[k].at[c], dst_ref=outs[k].at[1 - c], send_sem=send_sems.at[k], recv_sem=recv_sems.at[k],
                device_id=(x, y, 1 - c), device_id_type=MESH).wait_recv()
        for cp in sends:
            cp.wait_send()

    return pl.pallas_call(
        body, name=name, in_specs=[anyspec] * n, out_specs=[anyspec] * n,
        out_shape=[_sds(a.shape, a.dtype) for a in arrs], input_output_aliases={k: k for k in range(n)},
        scratch_shapes=[pltpu.SemaphoreType.DMA((n,)), pltpu.SemaphoreType.DMA((n,))],
        compiler_params=_cp(has_side_effects=True),
    )(*arrs)


_HBM = pl.BlockSpec(memory_space=pltpu.HBM)
_SEM = pl.BlockSpec(memory_space=pltpu.SEMAPHORE)


def _ici_copies(kind, srcs, lands, send_sems, recv_sems):
    x, y, c = _coords()
    me = 2 * x + y
    sends, recvs = [], []
    for k in range(len(srcs)):
        for j, (px, py) in enumerate(_other_chips(x, y)):
            peer = 2 * px + py
            if kind == "gather":
                src, there, here = srcs[k].at[c], lands[k].at[me, c], lands[k].at[peer, c]
            else:
                src, there, here = srcs[k].at[peer], lands[k].at[me], lands[k].at[peer]
            sem = 3 * k + j
            mk = functools.partial(pltpu.make_async_remote_copy, src_ref=src, send_sem=send_sems.at[sem],
                                   recv_sem=recv_sems.at[sem], device_id=(px, py, c), device_id_type=MESH)
            sends.append(mk(dst_ref=there))
            recvs.append(mk(dst_ref=here))
    return sends, recvs


def ici_start(kind, srcs, lands, after, *, name):
    n = len(srcs)

    def body(*refs):
        src_refs, land_refs = refs[:n], refs[n:2 * n]
        send_sems, recv_sems = refs[2 * n + 1], refs[2 * n + 2]
        token = refs[-1]
        sends, _ = _ici_copies(kind, src_refs, land_refs, send_sems, recv_sems)
        for cp in sends:
            cp.start()
        token[...] = jnp.zeros_like(token)

    both = list(srcs) + list(lands)
    out = pl.pallas_call(
        body, name=name,
        in_specs=[_HBM] * (2 * n) + [pl.BlockSpec(memory_space=pl.ANY)],
        out_shape=(pltpu.SemaphoreType.DMA((3 * n,)), pltpu.SemaphoreType.DMA((3 * n,)),
                   *[pltpu.HBM(a.shape, a.dtype) for a in both], _sds((8, LANE), F32)),
        out_specs=(_SEM, _SEM, *([_HBM] * (2 * n)), pl.BlockSpec(memory_space=pltpu.VMEM)),
        input_output_aliases={i: 2 + i for i in range(2 * n)},
        compiler_params=_cp(has_side_effects=pltpu.SideEffectType.DATAFLOW_SIDE_EFFECTING),
    )(*[pltpu.with_memory_space_constraint(a, pltpu.HBM) for a in both], after)
    return out[0], out[1], list(out[2:2 + n]), list(out[2 + n:2 + 2 * n]), out[-1]


def ici_wait(kind, started, after, *, name):
    send_sems, recv_sems, srcs, lands, _ = started
    n = len(srcs)

    def body(*refs):
        src_refs, land_refs = refs[:n], refs[n:2 * n]
        sends, recvs = _ici_copies(kind, src_refs, land_refs, refs[2 * n], refs[2 * n + 1])
        for cp in sends:
            cp.wait_send()
        for cp in recvs:
            cp.wait_recv()

    both = list(srcs) + list(lands)
    out = pl.pallas_call(
        body, name=name,
        in_specs=[_HBM] * (2 * n) + [_SEM, _SEM, pl.BlockSpec(memory_space=pl.ANY)],
        out_shape=tuple(pltpu.HBM(a.shape, a.dtype) for a in both), out_specs=tuple([_HBM] * (2 * n)),
        input_output_aliases={i: i for i in range(2 * n)},
        compiler_params=_cp(has_side_effects=pltpu.SideEffectType.DATAFLOW_SIDE_EFFECTING),
    )(*both, send_sems, recv_sems, after)
    return list(out[:n]), list(out[n:])


BIG = ["w_in", "w_uq", "w_ukv", "w_branch_ssm", "w_branch_mla", "w_out", "w_mlp_up", "w_mlp_down"]
COL_SHARDED = {"w_in", "w_uq", "w_ukv", "w_mlp_up"}
SMALL_REPL = ["norm_mix_w", "conv_b", "dt_bias", "a_log", "d_skip", "ssm_norm_w", "q_norm_w", "kv_norm_w", "norm_mlp_w"]


def _unshard_layer(name, g):
    _, r, c = g.shape
    if name in COL_SHARDED:
        return jnp.transpose(g, (1, 0, 2)).reshape(r, 4 * c)
    return g.reshape(4 * r, c)


def _to_shards(name, full):
    r, c = full.shape
    if name in COL_SHARDED:
        return jnp.transpose(full.reshape(r, 4, c // 4), (1, 0, 2))
    return full.reshape(4, r // 4, c)


REST = [k for k in BIG if k != "w_in"]


def prep_layer(cfg, w):
    out = {}
    if "w_in" in w:
        sp = np.cumsum(cfg.in_splits)[:-1].tolist()
        z, xbc, dt, cq, ckv, kr, gs, gm = jnp.split(w["w_in"], sp, axis=1)
        zpad = lambda n: jnp.zeros((cfg.d, n), z.dtype)
        out.update(w_z=z, w_xbc=xbc, w_g=jnp.concatenate([gs, gm], axis=1),
                   w_s=jnp.concatenate([cq, ckv, kr, zpad(LANE - cfg.rope), dt, zpad(LANE - cfg.heads)], axis=1))
    if "w_uq" in w:
        out.update(
            w_uq=jnp.pad(w["w_uq"].reshape(cfg.ql, cfg.mh, cfg.nope + cfg.rope),
                         ((0, 0), (0, 0), (0, 2 * LANE - cfg.nope - cfg.rope))).reshape(cfg.ql, cfg.qw),
            w_ukv=w["w_ukv"], w_bs=w["w_branch_ssm"], w_bm=w["w_branch_mla"], w_out=w["w_out"],
            w_up=w["w_mlp_up"], w_down=w["w_mlp_down"])
    return {k: v.astype(BF16) for k, v in out.items()}


def unprep_grads(cfg, g):
    out = {}
    if "w_s" in g:
        ql, kvl = cfg.ql, cfg.kvl
        ds_ = g["w_s"]
        cq, ckv = ds_[:, :ql], ds_[:, ql:ql + kvl]
        kr = ds_[:, ql + kvl:ql + kvl + cfg.rope]
        dt = ds_[:, ql + kvl + LANE:ql + kvl + LANE + cfg.heads]
        out["w_in"] = jnp.concatenate([g["w_z"], g["w_xbc"], dt, cq, ckv, kr, g["w_g"]], axis=1)
    if "w_uq" in g:
        out.update(
            w_uq=g["w_uq"].reshape(cfg.ql, cfg.mh, 2 * LANE)[:, :, :cfg.nope + cfg.rope].reshape(cfg.ql, -1),
            w_ukv=g["w_ukv"], w_branch_ssm=g["w_bs"], w_branch_mla=g["w_bm"],
            w_out=g["w_out"], w_mlp_up=g["w_up"], w_mlp_down=g["w_down"])
    return out


def layer_fwd(cfg, h, pw, sm, tabs, li, rest=None):
    n = lambda s: f"l{li}_{s}"
    u = rmsnorm_fwd(h, sm["norm_mix_w"], name=n("norm_mix"))
    z = matmul(u, pw["w_z"], out_dtype=BF16, name=n("in_z"))
    xbc = matmul(u, pw["w_xbc"], name=n("in_xbc"))
    g = matmul(u, pw["w_g"], out_dtype=BF16, name=n("in_g"))
    small = matmul(u, pw["w_s"], name=n("in_s"))
    xc = conv_fwd(cfg, xbc, sm["conv_w"], sm["conv_b"], name=n("conv"))
    y, sin = ssd_fwd(cfg, xc, small, sm["dt_bias_p"], sm["avec"], sm["dexp"], name=n("ssd"))
    y_ssm = tail_fwd(cfg, y, z, sm["ssm_norm_w"], name=n("tail"))
    if rest is not None:
        pw = dict(pw, **rest(y_ssm))
    cqn = rmsnorm_fwd(small, sm["q_norm_w"], cw=cfg.ql, ci=0, name=n("q_norm"))
    ckvn = rmsnorm_fwd(small, sm["kv_norm_w"], cw=cfg.kvl, ci=cfg.ql // cfg.kvl, name=n("kv_norm"))
    qf = matmul(cqn, pw["w_uq"], name=n("uq"))
    kv = matmul(ckvn, pw["w_ukv"], out_dtype=BF16, name=n("ukv"))
    qr, kpe = rope_fwd(cfg, qf, small, tabs, name=n("rope"))
    o, lse = attn_fwd(cfg, qr, kv, kpe, name=n("attn"))
    ya = matmul(y_ssm, pw["w_bs"], out_dtype=BF16, name=n("branch_ssm"))
    yb = matmul(o, pw["w_bm"], out_dtype=BF16, name=n("branch_mla"))
    mixed = gate_fwd(cfg, ya, yb, g, name=n("gate"))
    h1 = matmul(mixed, pw["w_out"], add=h, name=n("out"))
    v = rmsnorm_fwd(h1, sm["norm_mlp_w"], name=n("norm_mlp"))
    a, act = matmul(v, pw["w_up"], name=n("up"), epilogue=_ep_relu2, out_dtypes=(BF16, BF16))
    h2 = matmul(act, pw["w_down"], add=h1, name=n("down"))
    saved = dict(h=h, u=u, z=z, xbc=xbc, g=g, small=small, xc=xc, y=y, sin=sin, y_ssm=y_ssm, cqn=cqn, ckvn=ckvn,
                 qr=qr, kv=kv, kpe=kpe, o=o, lse=lse, ya=ya, yb=yb, mixed=mixed, h1=h1, v=v, a=a, act=act)
    return h2, saved, pw


def layer_bwd(cfg, dh2, pw, sm, tabs, s, li, early=None):
    n = lambda t: f"l{li}_b_{t}"
    gw, gs = {}, {}
    wgrad = functools.partial(matmul, ta=True, out_dtype=BF16)
    gw["w_down"] = wgrad(s["act"], dh2, name=n("dw_down"))
    da = matmul(dh2, pw["w_down"], tb=True, name=n("dact"), epilogue=_ep_relu2_grad, extras=(s["a"],),
                out_dtypes=(BF16,))
    gw["w_up"] = wgrad(s["v"], da, name=n("dw_up"))
    dv = matmul(da, pw["w_up"], tb=True, name=n("dv"))
    dh1, gs["norm_mlp_w"] = rmsnorm_bwd(dv, s["h1"], sm["norm_mlp_w"], res=dh2, name=n("norm_mlp"))
    gw["w_out"] = wgrad(s["mixed"], dh1, name=n("dw_out"))
    dmix = matmul(dh1, pw["w_out"], tb=True, name=n("dmix"))
    dya, dyb, dg = gate_bwd(cfg, dmix, s["ya"], s["yb"], s["g"], name=n("gate"))
    gw["w_bs"] = wgrad(s["y_ssm"], dya, name=n("dw_bs"))
    gw["w_bm"] = wgrad(s["o"], dyb, name=n("dw_bm"))
    dy_ssm = matmul(dya, pw["w_bs"], tb=True, name=n("dy_ssm"))
    do = matmul(dyb, pw["w_bm"], tb=True, name=n("do"))
    dq, dkv, dkpe = attn_bwd(cfg, s["qr"], s["kv"], s["kpe"], s["o"], s["lse"], do, name=n("attn"))
    dqf, dkr = rope_bwd(cfg, dq, dkpe, tabs, name=n("rope"))
    gw["w_uq"] = wgrad(s["cqn"], dqf, name=n("dw_uq"))
    gw["w_ukv"] = wgrad(s["ckvn"], dkv, name=n("dw_ukv"))
    dcqn = matmul(dqf, pw["w_uq"], tb=True, name=n("dcqn"))
    dckvn = matmul(dkv, pw["w_ukv"], tb=True, name=n("dckvn"))
    dcq, gs["q_norm_w"] = rmsnorm_bwd(dcqn, s["small"], sm["q_norm_w"], cw=cfg.ql, ci=0, out_dtype=BF16, name=n("q_norm"))
    dckv, gs["kv_norm_w"] = rmsnorm_bwd(dckvn, s["small"], sm["kv_norm_w"], cw=cfg.kvl, ci=cfg.ql // cfg.kvl,
                                        out_dtype=BF16, name=n("kv_norm"))
    ssm_norm_w = sm["ssm_norm_w"]
    if early is not None:
        ssm_norm_w = ssm_norm_w + early(dict(gw))[0, 0]
    dy, dz, gs["ssm_norm_w"] = tail_bwd(cfg, dy_ssm, s["y"], s["z"], ssm_norm_w, name=n("tail"))
    dxc, ddt, ddexp, dav, dbias = ssd_bwd(cfg, s["xc"], s["small"], sm["dt_bias_p"], sm["avec"], sm["dexp"],
                                          s["sin"], dy, name=n("ssd"))
    dxbc, gs["conv_w"], gs["conv_b"] = conv_bwd(cfg, s["xbc"], sm["conv_w"], sm["conv_b"], dxc, name=n("conv"))
    gs["d_skip"] = ddexp.reshape(cfg.heads, cfg.hd).sum(axis=1)
    gs["a_log"] = (dav[0] * sm["avec"][0])[:cfg.heads]
    gs["dt_bias"] = dbias[0, :cfg.heads]
    dsmall = jnp.concatenate([dcq, dckv, dkr.astype(BF16), ddt.astype(BF16)], axis=1)
    gw["w_z"] = wgrad(s["u"], dz, name=n("dw_z"))
    gw["w_xbc"] = wgrad(s["u"], dxbc, name=n("dw_xbc"))
    gw["w_g"] = wgrad(s["u"], dg, name=n("dw_g"))
    gw["w_s"] = wgrad(s["u"], dsmall, name=n("dw_s"))
    du = matmul(dz, pw["w_z"], tb=True, name=n("du_z"))
    du = matmul(dxbc, pw["w_xbc"], tb=True, add=du, name=n("du_xbc"))
    du = matmul(dg, pw["w_g"], tb=True, add=du, name=n("du_g"))
    du = matmul(dsmall, pw["w_s"], tb=True, add=du, name=n("du_s"))
    dh, gs["norm_mix_w"] = rmsnorm_bwd(du, s["h"], sm["norm_mix_w"], res=dh1, name=n("norm_mix"))
    return dh, gw, gs


def small_params(cfg, p, li):
    pad_l = lambda v: jnp.pad(v, (0, LANE - v.shape[0])).reshape(1, LANE)
    return dict(
        norm_mix_w=p["norm_mix_w"][li], conv_w=p["conv_w"][li], conv_b=p["conv_b"][li],
        dt_bias_p=pad_l(p["dt_bias"][li]), avec=pad_l(-jnp.exp(p["a_log"][li])),
        dexp=jnp.repeat(p["d_skip"][li], cfg.hd).reshape(1, cfg.inner),
        ssm_norm_w=p["ssm_norm_w"][li], q_norm_w=p["q_norm_w"][li], kv_norm_w=p["kv_norm_w"][li],
        norm_mlp_w=p["norm_mlp_w"][li])


def local_step(cfg, x, target, p, depth=2):
    bsz, d = cfg.bsz, cfg.d
    lead = jnp.zeros((bsz, cfg.pad, d), F32)
    meta = jnp.broadcast_to(p["meta_tokens"][None], (bsz, cfg.n_meta, d))
    h = jnp.concatenate([lead, meta, x], axis=1).reshape(cfg.t, d)
    tabs = rope_tables(cfg)
    saved, sms = [], []
    for li in range(depth):
        sm = small_params(cfg, p, li)
        h, s, _ = layer_fwd(cfg, h, p["pw"][li], sm, tabs, li)
        saved.append(s)
        sms.append(sm)
    loss, dh, dfw = loss_head(cfg, h, target.reshape(bsz * cfg.seq, d), p["final_norm_w"], name="loss_head")
    gws, gss = [None] * depth, [None] * depth
    for li in reversed(range(depth)):
        dh, gws[li], gss[li] = layer_bwd(cfg, dh, p["pw"][li], sms[li], tabs, saved[li], li)
    dh = dh.reshape(bsz, cfg.lp, d)
    grad_x = dh[:, cfg.chunk:, :]
    gmeta = jnp.sum(dh[:, cfg.pad:cfg.chunk, :], axis=0)
    return loss, grad_x, gmeta, gws, gss, dfw


def _pack_small(parts):
    flat = jnp.concatenate([a.reshape(-1) for a in parts])
    n = flat.shape[0]
    npad = -n % (8 * LANE)
    return jnp.pad(flat, (0, npad)).reshape(-1, LANE), n


def _unpack_small(vec, shapes):
    flat = vec.reshape(-1)
    out, off = [], 0
    for sh in shapes:
        sz = int(np.prod(sh))
        out.append(flat[off:off + sz].reshape(sh))
        off += sz
    return out


def _as2d(a):
    return a.reshape(-1, a.shape[-1])


def kernel(x, meta_tokens, norm_mix_w, w_in, conv_w, conv_b, dt_bias, a_log, d_skip, ssm_norm_w, q_norm_w, kv_norm_w, w_uq, w_ukv, w_branch_ssm, w_branch_mla, w_out, norm_mlp_w, w_mlp_up, w_mlp_down, final_norm_w, loss_target, m_meta_tokens, m_norm_mix_w, m_w_in, m_conv_w, m_conv_b, m_dt_bias, m_a_log, m_d_skip, m_ssm_norm_w, m_q_norm_w, m_kv_norm_w, m_w_uq, m_w_ukv, m_w_branch_ssm, m_w_branch_mla, m_w_out, m_norm_mlp_w, m_w_mlp_up, m_w_mlp_down, m_final_norm_w, v_meta_tokens, v_norm_mix_w, v_w_in, v_conv_w, v_conv_b, v_dt_bias, v_a_log, v_d_skip, v_ssm_norm_w, v_q_norm_w, v_kv_norm_w, v_w_uq, v_w_ukv, v_w_branch_ssm, v_w_branch_mla, v_w_out, v_norm_mlp_w, v_w_mlp_up, v_w_mlp_down, v_final_norm_w):
    cfg = CFG
    names = ["meta_tokens", "norm_mix_w", "w_in", "conv_w", "conv_b", "dt_bias", "a_log", "d_skip", "ssm_norm_w",
             "q_norm_w", "kv_norm_w", "w_uq", "w_ukv", "w_branch_ssm", "w_branch_mla", "w_out", "norm_mlp_w",
             "w_mlp_up", "w_mlp_down", "final_norm_w"]
    wts = dict(zip(names, [meta_tokens, norm_mix_w, w_in, conv_w, conv_b, dt_bias, a_log, d_skip, ssm_norm_w,
                           q_norm_w, kv_norm_w, w_uq, w_ukv, w_branch_ssm, w_branch_mla, w_out, norm_mlp_w,
                           w_mlp_up, w_mlp_down, final_norm_w]))
    ms = dict(zip(names, [m_meta_tokens, m_norm_mix_w, m_w_in, m_conv_w, m_conv_b, m_dt_bias, m_a_log, m_d_skip,
                          m_ssm_norm_w, m_q_norm_w, m_kv_norm_w, m_w_uq, m_w_ukv, m_w_branch_ssm, m_w_branch_mla,
                          m_w_out, m_norm_mlp_w, m_w_mlp_up, m_w_mlp_down, m_final_norm_w]))
    vs = dict(zip(names, [v_meta_tokens, v_norm_mix_w, v_w_in, v_conv_w, v_conv_b, v_dt_bias, v_a_log, v_d_skip,
                          v_ssm_norm_w, v_q_norm_w, v_kv_norm_w, v_w_uq, v_w_ukv, v_w_branch_ssm, v_w_branch_mla,
                          v_w_out, v_norm_mlp_w, v_w_mlp_up, v_w_mlp_down, v_final_norm_w]))
    cx, cy, cc = _coords()
    chip = 2 * cx + cy

    half1 = jnp.reshape(cc, (1,)).astype(jnp.int32)
    where2 = jnp.stack([chip, cc]).astype(jnp.int32)
    wb = {k: wts[k].astype(BF16) for k in BIG}
    zero_tok = jnp.zeros((8, LANE), F32)

    def halves(a):
        return a.reshape((2, a.shape[0] // 2) + a.shape[1:])

    def gather_start(li, keys, tag, after):
        srcs = [halves(wb[k][li]) for k in keys]
        lands = [lax.empty((4,) + s.shape, BF16) for s in srcs]
        return ici_start("gather", srcs, lands, after, name=f"gather{li}{tag}_start")

    def gather_finish(li, keys, tag, started, after):
        srcs, lands = ici_wait("gather", started, after, name=f"gather{li}{tag}_wait")
        lands = pair_share(lands, srcs, name=f"gather{li}{tag}_share")
        full = {k: _unshard_layer(k, land.reshape((4, 2 * land.shape[2], land.shape[3])))
                for k, land in zip(keys, lands)}
        return prep_layer(cfg, full)

    def reduce_start(li, keys, tag, gw, after):
        ug = unprep_grads(cfg, gw)
        g4 = []
        for k in keys:
            s = _to_shards(k, ug[k])
            g4.append(s.reshape(4, 2, s.shape[1] // 2, s.shape[2]))
        theirs = pair_exchange(g4, name=f"grad{li}{tag}_pair_exchange")
        parts = [pair_add(a, b, half1, name=f"grad{li}_pair_add_{k}") for k, a, b in zip(keys, g4, theirs)]
        lands = [lax.empty(q.shape, q.dtype) for q in parts]
        return ici_start("scatter", parts, lands, after, name=f"grad{li}{tag}_scatter_start")

    def reduce_finish(li, keys, tag, started, after):
        parts, lands = ici_wait("scatter", started, after, name=f"grad{li}{tag}_scatter_wait")
        sums = [chip_sum(rc, pt, where2, name=f"grad{li}_chip_sum_{k}") for k, rc, pt in zip(keys, lands, parts)]
        sums = pair_fill(sums, name=f"grad{li}{tag}_pair_fill")
        return {k: s.reshape(2 * s.shape[1], s.shape[2]) for k, s in zip(keys, sums)}

    gathered = gather_chips([meta_tokens, conv_w], name="gather_small")
    p = dict(wts)
    p["meta_tokens"] = jnp.transpose(gathered[0], (1, 0, 2)).reshape(cfg.n_meta, cfg.d)
    p["conv_w"] = jnp.transpose(gathered[1], (1, 2, 0, 3)).reshape(2, cfg.convk, cfg.conv_dim)

    st0a = gather_start(0, ["w_in"], "a", gathered[0])
    st0b = gather_start(0, REST, "b", st0a[4])
    st1 = gather_start(1, BIG, "", st0b[4])
    pw0 = gather_finish(0, ["w_in"], "a", st0a, st1[4])

    bsz, d = cfg.bsz, cfg.d
    lead = jnp.zeros((bsz, cfg.pad, d), F32)
    meta = jnp.broadcast_to(p["meta_tokens"][None], (bsz, cfg.n_meta, d))
    h0 = jnp.concatenate([lead, meta, x], axis=1).reshape(cfg.t, d)
    tabs = rope_tables(cfg)
    sm0 = small_params(cfg, p, 0)
    h1, sv0, pw0 = layer_fwd(cfg, h0, pw0, sm0, tabs, 0,
                             rest=lambda after: gather_finish(0, REST, "b", st0b, after))
    pw1 = gather_finish(1, BIG, "", st1, h1)
    sm1 = small_params(cfg, p, 1)
    h2, sv1, _ = layer_fwd(cfg, h1, pw1, sm1, tabs, 1)
    loss, dh, dfw = loss_head(cfg, h2, loss_target.reshape(bsz * cfg.seq, d), final_norm_w, name="loss_head")
    loss = lax.psum(loss, ("x", "y", "c"))

    dh, gw1, gs1 = layer_bwd(cfg, dh, pw1, sm1, tabs, sv1, 1)
    red1 = reduce_start(1, BIG, "", gw1, zero_tok)
    sm0b = dict(sm0)
    sm0b["norm_mlp_w"] = sm0["norm_mlp_w"] + red1[4][0, 0]
    early = {}

    def start_early(gw):
        early["st"] = reduce_start(0, REST, "e", gw, zero_tok)
        return early["st"][4]

    dh, gw0, gs0 = layer_bwd(cfg, dh, pw0, sm0b, tabs, sv0, 0, early=start_early)
    dh3 = dh.reshape(bsz, cfg.lp, d)
    grad_x = dh3[:, cfg.chunk:, :]
    gmeta = jnp.sum(dh3[:, cfg.pad:cfg.chunk, :], axis=0)
    big1 = reduce_finish(1, BIG, "", red1, dh)

    small_names = SMALL_REPL + ["conv_w"]
    parts = [jnp.stack([gs0[k], gs1[k]]) for k in small_names] + [dfw, gmeta]
    shapes = [a.shape for a in parts]
    vec, _ = _pack_small(parts)
    red_vec = allreduce_small(vec, big1[BIG[-1]], name="allreduce_small")
    red = _unpack_small(red_vec, shapes)
    sg = dict(zip(small_names + ["final_norm_w", "meta_tokens"], red))
    sg["conv_w"] = lax.dynamic_slice_in_dim(sg["conv_w"], chip * (cfg.conv_dim // 4), cfg.conv_dim // 4, axis=2)
    sg["meta_tokens"] = lax.dynamic_slice_in_dim(sg["meta_tokens"], chip * (cfg.d // 4), cfg.d // 4, axis=1)

    red0 = reduce_start(0, ["w_in"], "l", gw0, red_vec)
    grads, deltas, new_m, new_v = {}, {}, {}, {}
    dep = red0[4]
    for k in names:
        if k in BIG:
            continue
        w2, g2, m2, v2 = _as2d(wts[k]), _as2d(sg[k]), _as2d(ms[k]), _as2d(vs[k])
        dl, mn, vn = adamw_small(w2, g2, m2, v2, dep, name=f"adamw_{k}")
        grads[k] = sg[k].reshape(wts[k].shape)
        deltas[k], new_m[k], new_v[k] = (t.reshape(wts[k].shape) for t in (dl, mn, vn))

    def view(k, a):
        return jnp.swapaxes(a, 1, 2) if k == "w_in" else a

    def gview(k, g):
        return g.T if k == "w_in" else g

    wv, mv, vv = ({k: view(k, t[k]) for k in BIG} for t in (wts, ms, vs))
    outs = {}
    for k in BIG:
        outs[k] = adamw_layer(wv[k], mv[k], vv[k], gview(k, big1[k]), 1, None, dep, name=f"adamw1_{k}")
        dep = outs[k][1]
    big0 = reduce_finish(0, REST, "e", early["st"], dep)
    for k in REST:
        outs[k] = adamw_layer(wv[k], mv[k], vv[k], big0[k], 0, outs[k], dep, name=f"adamw0_{k}")
        dep = outs[k][1]
    big0.update(reduce_finish(0, ["w_in"], "l", red0, dep))
    outs["w_in"] = adamw_layer(wv["w_in"], mv["w_in"], vv["w_in"], gview("w_in", big0["w_in"]), 0, outs["w_in"], dep,
                               name="adamw0_w_in")
    for k in BIG:
        grads[k], deltas[k], new_m[k], new_v[k] = (view(k, t) for t in outs[k])
    return (loss, grad_x, *[grads[k] for k in names], *[deltas[k] for k in names],
            *[new_m[k] for k in names], *[new_v[k] for k in names])


def adamw_small(w, g, m, v, dep, *, name):
    def body(w_ref, g_ref, m_ref, v_ref, dep_ref, d_ref, mo_ref, vo_ref):
        d_ref[...], mo_ref[...], vo_ref[...] = _adam_update(w_ref[...], g_ref[...], m_ref[...], v_ref[...])

    vm = pl.BlockSpec(memory_space=pltpu.VMEM)
    return pl.pallas_call(body, name=name, in_specs=[vm] * 4 + [pl.BlockSpec(memory_space=pl.ANY)], out_specs=[vm] * 3,
                          out_shape=[_sds(w.shape, F32)] * 3, compiler_params=_cp())(w, g, m, v, dep)
```

```python
import functools
import math
from typing import NamedTuple

import numpy as np
import jax
import jax.numpy as jnp
from jax import lax
from jax.experimental import pallas as pl
from jax.experimental.pallas import tpu as pltpu

F32 = jnp.float32
BF16 = jnp.bfloat16
HI = lax.Precision.HIGHEST
EPS = 1e-6
ROPE_THETA = 10000.0
LANE = 128
VMEM_LIMIT = 56 * 1024 * 1024
MASK_VALUE = -1e30
ADAM_LR, ADAM_B1, ADAM_B2, ADAM_EPS, ADAM_WD, ADAM_STEP = 0.001, 0.9, 0.999, 1e-08, 0.01, 10
MESH = pl.DeviceIdType.MESH


class Cfg(NamedTuple):
    d: int = 1024
    seq: int = 2048
    bsz: int = 2
    n_meta: int = 16
    inner: int = 2048
    hd: int = 64
    groups: int = 4
    state: int = 128
    convk: int = 4
    chunk: int = 128
    mh: int = 8
    ql: int = 512
    kvl: int = 256
    nope: int = 128
    rope: int = 64
    vd: int = 128
    ff: int = 4096

    @property
    def heads(self): return self.inner // self.hd
    @property
    def gw(self): return self.inner // self.groups
    @property
    def conv_dim(self): return self.inner + 2 * self.groups * self.state
    @property
    def pad(self): return self.chunk - self.n_meta
    @property
    def lp(self): return self.chunk + self.seq
    @property
    def t(self): return self.bsz * self.lp
    @property
    def nchunks(self): return self.lp // self.chunk
    @property
    def sw(self): return self.ql + self.kvl + 2 * LANE
    @property
    def kt(self): return (self.ql + self.kvl) // LANE
    @property
    def dtt(self): return self.kt + 1
    @property
    def qw(self): return self.mh * 2 * LANE
    @property
    def in_splits(self):
        return [self.inner, self.conv_dim, self.heads, self.ql, self.kvl, self.rope, self.d, self.d]


CFG = Cfg()


def _pick(dim, pref, mult):
    best = None
    for t in range(mult, min(dim, pref) + 1, mult):
        if dim % t == 0:
            best = t
    return best if best is not None else dim


def _cp(**kw):
    return pltpu.CompilerParams(vmem_limit_bytes=VMEM_LIMIT, **kw)


def _sds(shape, dtype):
    return jax.ShapeDtypeStruct(tuple(shape), dtype)


def _silu(x):
    return x * jax.nn.sigmoid(x)


def _dsilu(x):
    s = jax.nn.sigmoid(x)
    return s * (1.0 + x * (1.0 - s))


def _ep_plain(r):
    return (r,)


def _ep_add(r, res):
    return (r + res.astype(F32),)


def _ep_relu2(r):
    rp = jnp.maximum(r, 0.0)
    return r, rp * rp


def _ep_relu2_grad(r, a):
    return (r * (2.0 * jnp.maximum(a.astype(F32), 0.0)),)


MM_VMEM_BUDGET = 44 * 1024 * 1024


def _mm_tiles(m, n, k, a_bytes, b_bytes, io_bytes, ta):
    m_mult, m_cap = (LANE, 1024) if ta else (16, 1088)
    tms = [t for t in range(m_cap, 0, -m_mult) if m % t == 0] or [m]
    tns = [t for t in (1024, 512, 256, 128) if n % t == 0] or [n]
    best = None
    for tm in tms:
        for tn in tns:
            need = 2 * (tm * k * a_bytes + k * tn * b_bytes + tm * tn * io_bytes)
            if need <= MM_VMEM_BUDGET and (best is None or tm * tn > best[0] * best[1]):
                best = (tm, tn)
    if best is None:
        return (_pick(m, 512, m_mult), _pick(n, 512, LANE), _pick(k, 1088 if ta else 1024, 16 if ta else LANE))
    return best[0], best[1], k


def matmul(a, b, *, ta=False, tb=False, out_dtype=F32, add=None, name, tm=None, tn=None, tk=None,
           epilogue=None, extras=(), out_dtypes=None):
    if add is not None:
        epilogue, extras = _ep_add, (add,)
    if epilogue is None:
        epilogue = _ep_plain
    out_dtypes = tuple(out_dtypes) if out_dtypes is not None else (out_dtype,)
    n_ex, n_out = len(extras), len(out_dtypes)
    if ta:
        k_dim, m_dim = a.shape
    else:
        m_dim, k_dim = a.shape
    if tb:
        n_dim, k2 = b.shape
    else:
        k2, n_dim = b.shape
    assert k_dim == k2, (a.shape, b.shape, ta, tb)
    if tm is None and tn is None and tk is None:
        io_bytes = sum(jnp.dtype(e.dtype).itemsize for e in extras) + sum(jnp.dtype(d).itemsize for d in out_dtypes)
        tm, tn, tk = _mm_tiles(m_dim, n_dim, k_dim, jnp.dtype(a.dtype).itemsize, jnp.dtype(b.dtype).itemsize,
                               io_bytes, ta)
    elif ta:
        tm = tm or _pick(m_dim, 1024, LANE)
        tk = tk or _pick(k_dim, 1088, 16)
        tn = tn or _pick(n_dim, 1024, LANE)
    else:
        tm = tm or _pick(m_dim, 1088, 16)
        tk = tk or _pick(k_dim, 1024 if a.dtype == F32 else 2048, LANE)
        tn = tn or _pick(n_dim, 1024, LANE)
    nm, nn, nk = m_dim // tm, n_dim // tn, k_dim // tk
    dn = (((0 if ta else 1,), (1 if tb else 0,)), ((), ()))

    def body(*refs):
        a_ref, b_ref = refs[:2]
        ex_refs = refs[2:2 + n_ex]
        o_refs = refs[2 + n_ex:2 + n_ex + n_out]
        scr = refs[2 + n_ex + n_out:]
        p = lax.dot_general(a_ref[...].astype(BF16), b_ref[...].astype(BF16), dn, preferred_element_type=F32)

        def finish(r):
            outs = epilogue(r, *[e[...] for e in ex_refs])
            for o_ref, val, dt in zip(o_refs, outs, out_dtypes):
                o_ref[...] = val.astype(dt)

        if nk == 1:
            finish(p)
        else:
            acc = scr[0]
            k = pl.program_id(2)

            @pl.when(k == 0)
            def _():
                acc[...] = p

            @pl.when(k > 0)
            def _():
                acc[...] += p

            @pl.when(k == nk - 1)
            def _():
                finish(acc[...])

    a_spec = pl.BlockSpec((tk, tm), lambda i, j, k: (k, i)) if ta else pl.BlockSpec((tm, tk), lambda i, j, k: (i, k))
    b_spec = pl.BlockSpec((tn, tk), lambda i, j, k: (j, k)) if tb else pl.BlockSpec((tk, tn), lambda i, j, k: (k, j))
    o_spec = pl.BlockSpec((tm, tn), lambda i, j, k: (i, j))
    outs = pl.pallas_call(
        body, name=name, grid=(nm, nn, nk), in_specs=[a_spec, b_spec] + [o_spec] * n_ex, out_specs=[o_spec] * n_out,
        out_shape=[_sds((m_dim, n_dim), dt) for dt in out_dtypes],
        scratch_shapes=[pltpu.VMEM((tm, tn), F32)] if nk > 1 else [],
        compiler_params=_cp(dimension_semantics=("parallel", "parallel", "arbitrary")),
    )(a, b, *extras)
    return outs[0] if n_out == 1 else tuple(outs)


def rmsnorm_fwd(x, w, *, cw=None, ci=0, name):
    t = x.shape[0]
    cw = cw or x.shape[1]
    tr = _pick(t, 544, 16)

    def body(x_ref, w_ref, o_ref):
        xv = x_ref[...].astype(F32)
        r = lax.rsqrt(jnp.mean(xv * xv, axis=-1, keepdims=True) + EPS)
        o_ref[...] = (xv * r * w_ref[...]).astype(BF16)

    return pl.pallas_call(
        body, name=name, grid=(t // tr,),
        in_specs=[pl.BlockSpec((tr, cw), lambda i: (i, ci)), pl.BlockSpec((1, cw), lambda i: (0, 0))],
        out_specs=pl.BlockSpec((tr, cw), lambda i: (i, 0)),
        out_shape=_sds((t, cw), BF16), compiler_params=_cp(),
    )(x, w.reshape(1, cw))


def rmsnorm_bwd(dy, x, w, *, cw=None, ci=0, res=None, out_dtype=F32, name):
    t = x.shape[0]
    cw = cw or x.shape[1]
    tr = _pick(t, 544, 16)
    has_res = res is not None

    def body(*refs):
        if has_res:
            dy_ref, x_ref, w_ref, res_ref, dx_ref, dw_ref = refs
        else:
            dy_ref, x_ref, w_ref, dx_ref, dw_ref = refs
        xv = x_ref[...].astype(F32)
        dyv = dy_ref[...].astype(F32)
        r = lax.rsqrt(jnp.mean(xv * xv, axis=-1, keepdims=True) + EPS)
        xh = xv * r
        g = dyv * w_ref[...]
        dx = r * (g - xh * jnp.mean(g * xh, axis=-1, keepdims=True))
        if has_res:
            dx = dx + res_ref[...]
        dx_ref[...] = dx.astype(out_dtype)

        @pl.when(pl.program_id(0) == 0)
        def _():
            dw_ref[...] = jnp.zeros_like(dw_ref)

        dw_ref[...] += jnp.sum(dyv * xh, axis=0, keepdims=True)

    row = pl.BlockSpec((tr, cw), lambda i: (i, 0))
    in_specs = [row, pl.BlockSpec((tr, cw), lambda i: (i, ci)), pl.BlockSpec((1, cw), lambda i: (0, 0))]
    args = [dy, x, w.reshape(1, cw)]
    if has_res:
        in_specs.append(row)
        args.append(res)
    dx, dw = pl.pallas_call(
        body, name=name, grid=(t // tr,), in_specs=in_specs,
        out_specs=[row, pl.BlockSpec((1, cw), lambda i: (0, 0))],
        out_shape=[_sds((t, cw), out_dtype), _sds((1, cw), F32)], compiler_params=_cp(),
    )(*args)
    return dx, dw[0]


def _shift_down(x, s):
    return x if s == 0 else pltpu.roll(x, s, 0)


def _shift_up(x, s):
    return x if s == 0 else pltpu.roll(x, x.shape[0] - s, 0)


def _conv_pre(x, w_ref, b_ref, kk):
    pre = b_ref[...] + jnp.zeros_like(x)
    for k in range(kk):
        pre = pre + w_ref[k:k + 1, :] * _shift_down(x, kk - 1 - k)
    return pre


def conv_fwd(cfg, xbc, w, b, *, name):
    lp, cd, kk = cfg.lp, cfg.conv_dim, cfg.convk
    assert cfg.pad >= kk - 1
    cb = _pick(cd, 512, LANE)

    def body(x_ref, w_ref, b_ref, o_ref):
        o_ref[...] = _silu(_conv_pre(x_ref[...], w_ref, b_ref, kk))

    blk = pl.BlockSpec((lp, cb), lambda j, bb: (bb, j))
    return pl.pallas_call(
        body, name=name, grid=(cd // cb, cfg.bsz),
        in_specs=[blk, pl.BlockSpec((kk, cb), lambda j, bb: (0, j)), pl.BlockSpec((1, cb), lambda j, bb: (0, j))],
        out_specs=blk, out_shape=_sds((cfg.t, cd), F32), compiler_params=_cp(),
    )(xbc, w, b.reshape(1, cd))


def conv_bwd(cfg, xbc, w, b, dxc, *, name):
    lp, cd, kk = cfg.lp, cfg.conv_dim, cfg.convk
    cb = _pick(cd, 512, LANE)

    def body(x_ref, w_ref, b_ref, d_ref, dx_ref, dw_ref, db_ref):
        x = x_ref[...]
        pre = _conv_pre(x, w_ref, b_ref, kk)
        dpre = d_ref[...] * _dsilu(pre)
        dx = jnp.zeros_like(x)
        dws = []
        for k in range(kk):
            s = kk - 1 - k
            dx = dx + w_ref[k:k + 1, :] * _shift_up(dpre, s)
            dws.append(jnp.sum(dpre * _shift_down(x, s), axis=0, keepdims=True))
        dx_ref[...] = dx.astype(BF16)

        @pl.when(pl.program_id(1) == 0)
        def _():
            dw_ref[...] = jnp.zeros_like(dw_ref)
            db_ref[...] = jnp.zeros_like(db_ref)

        for k in range(kk):
            dw_ref[k:k + 1, :] += dws[k]
        db_ref[...] += jnp.sum(dpre, axis=0, keepdims=True)

    blk = pl.BlockSpec((lp, cb), lambda j, bb: (bb, j))
    wsp = pl.BlockSpec((kk, cb), lambda j, bb: (0, j))
    bsp = pl.BlockSpec((1, cb), lambda j, bb: (0, j))
    dx, dw, db = pl.pallas_call(
        body, name=name, grid=(cd // cb, cfg.bsz),
        in_specs=[blk, wsp, bsp, blk], out_specs=[blk, wsp, bsp],
        out_shape=[_sds((cfg.t, cd), BF16), _sds((kk, cd), F32), _sds((1, cd), F32)], compiler_params=_cp(),
    )(xbc, w, b.reshape(1, cd), dxc)
    return dx, dw, db[0]


def _softplus(x):
    return jnp.maximum(x, 0.0) + jnp.log(1.0 + jnp.exp(-jnp.abs(x)))


def _ssd_consts(cfg):
    q = cfg.chunk
    i0 = np.arange(q)[:, None]
    i1 = np.arange(q)[None, :]
    ltri = (i1 <= i0).astype(np.float32)
    rexp = np.zeros((LANE, cfg.inner), np.float32)
    for h in range(cfg.heads):
        rexp[h, h * cfg.hd:(h + 1) * cfg.hd] = 1.0
    return jnp.asarray(ltri), jnp.asarray(rexp)


def _sel_dot(x, m, *, passes=2, left=False, trans=False):
    mb = m.astype(BF16)
    acc, rem = None, x
    for _ in range(passes):
        piece = rem.astype(BF16)
        if not left:
            part = _nn(piece, mb)
        elif trans:
            part = _tn(mb, piece)
        else:
            part = _nn(mb, piece)
        acc = part if acc is None else acc + part
        rem = rem - piece.astype(F32)
    return acc


def _ssd_chunk_common(cfg, raw, bias, avec, c_idx, ltri, rexp):
    q = cfg.chunk
    rows = lax.broadcasted_iota(jnp.int32, (q, LANE), 0)
    live = jnp.logical_or(c_idx > 0, rows >= cfg.pad)
    pre = raw + bias
    dt = jnp.where(live, _softplus(pre), 0.0)
    adt = dt * avec
    cs = _sel_dot(adt, ltri, passes=3, left=True)
    cs_t = cs.T
    cs_last = cs[q - 1:q, :]
    e_in = jnp.exp(cs)
    w0 = jnp.exp(cs_last - cs)
    decay = jnp.exp(cs_last)
    return dict(live=live, pre=pre, dt=dt, adt=adt, cs=cs, cs_t=cs_t, e_in=e_in, w0=w0, decay=decay,
                DT=_sel_dot(dt, rexp), E=_sel_dot(e_in, rexp), W0=_sel_dot(w0, rexp),
                DEC=_sel_dot(jnp.broadcast_to(decay, (8, LANE)), rexp)[0:1, :])


def _tri_masks(q):
    r = lax.broadcasted_iota(jnp.int32, (q, q), 0)
    c = lax.broadcasted_iota(jnp.int32, (q, q), 1)
    return c <= r, r <= c


def _head_l(cq, h, tri, tri_t):
    col = cq["cs"][:, h:h + 1]
    row = cq["cs_t"][h:h + 1, :]
    lmat = jnp.where(tri, jnp.exp(jnp.minimum(col - row, 0.0)), 0.0)
    lmat_t = jnp.where(tri_t, jnp.exp(jnp.minimum(row - col, 0.0)), 0.0)
    return lmat, lmat_t


def _nt(a, b):
    return lax.dot_general(a, b, (((1,), (1,)), ((), ())), preferred_element_type=F32)


def _tn(a, b):
    return lax.dot_general(a, b, (((0,), (0,)), ((), ())), preferred_element_type=F32)


def _nn(a, b):
    return jnp.dot(a, b, preferred_element_type=F32)


def ssd_fwd(cfg, xc, small, dt_bias, avec, dexp, *, name):
    q, inner, st, gw, g_n = cfg.chunk, cfg.inner, cfg.state, cfg.gw, cfg.groups
    nc = cfg.nchunks
    ltri, rexp = _ssd_consts(cfg)
    hpt = LANE // cfg.hd
    tiles_per_group = gw // LANE

    bsz, lp = cfg.bsz, cfg.lp
    bcw = g_n * st

    def body(x_ref, b_ref, c_ref, dt_ref, bias_ref, a_ref, d_ref, ltri_ref, rexp_ref, y_ref, sin_ref, s_scr):
        c_idx = pl.program_id(0)

        @pl.when(c_idx == 0)
        def _():
            s_scr[...] = jnp.zeros_like(s_scr)

        ltri_v = ltri_ref[...]
        tri, tri_t = _tri_masks(q)
        lane = lax.broadcasted_iota(jnp.int32, (q, LANE), 1)
        for bi in range(bsz):
            cq = _ssd_chunk_common(cfg, dt_ref[bi], bias_ref[...], a_ref[...], c_idx, ltri_v, rexp_ref[...])
            xs = x_ref[bi]
            xdt = (xs * cq["DT"]).astype(BF16)
            xw = (xs * cq["DT"] * cq["W0"]).astype(BF16)
            s_in = s_scr[bi]
            sin_ref[bi, 0] = s_in
            for g in range(g_n):
                bg = b_ref[bi, :, g * st:(g + 1) * st].astype(BF16)
                cg = c_ref[bi, :, g * st:(g + 1) * st].astype(BF16)
                gmat = _nt(cg, bg)
                gs = slice(g * gw, (g + 1) * gw)
                y0 = _nn(cg, s_in[:, gs].astype(BF16))
                for tt in range(tiles_per_group):
                    tile = g * tiles_per_group + tt
                    ts = slice(tile * LANE, (tile + 1) * LANE)
                    xt = xdt[:, ts]
                    ms, xh = [], []
                    for hh in range(hpt):
                        lmat, _ = _head_l(cq, tile * hpt + hh, tri, tri_t)
                        ms.append((gmat * lmat).astype(BF16))
                        inhead = jnp.logical_and(lane >= hh * cfg.hd, lane < (hh + 1) * cfg.hd)
                        xh.append(jnp.where(inhead, xt, jnp.zeros_like(xt)))
                    yd = _nn(jnp.concatenate(ms, axis=1), jnp.concatenate(xh, axis=0))
                    y_ref[bi, :, ts] = (yd + y0[:, tt * LANE:(tt + 1) * LANE] * cq["E"][:, ts]
                                        + xs[:, ts] * d_ref[:, ts])
                s_scr[bi, :, gs] = s_in[:, gs] * cq["DEC"][:, gs] + _tn(bg, xw[:, gs])

    def rowblk(width, col):
        return pl.BlockSpec((bsz, q, width), lambda c: (0, c, col))

    def const(shape):
        return pl.BlockSpec(shape, lambda c: (0, 0))

    xc3 = xc.reshape(bsz, lp, cfg.conv_dim)
    y, sin = pl.pallas_call(
        body, name=name, grid=(nc,),
        in_specs=[rowblk(inner, 0), rowblk(bcw, inner // bcw), rowblk(bcw, inner // bcw + 1),
                  rowblk(LANE, cfg.dtt), const((1, LANE)), const((1, LANE)), const((1, inner)),
                  const((q, q)), const((LANE, inner))],
        out_specs=[rowblk(inner, 0), pl.BlockSpec((bsz, 1, st, inner), lambda c: (0, c, 0, 0))],
        out_shape=[_sds((bsz, lp, inner), F32), _sds((bsz, nc, st, inner), F32)],
        scratch_shapes=[pltpu.VMEM((bsz, st, inner), F32)], compiler_params=_cp(),
    )(xc3, xc3, xc3, small.reshape(bsz, lp, cfg.sw), dt_bias, avec, dexp, ltri, rexp)
    return y.reshape(cfg.t, inner), sin.reshape(bsz * nc, st, inner)


def ssd_bwd(cfg, xc, small, dt_bias, avec, dexp, sin, dy, *, name):
    q, inner, st, gw, g_n = cfg.chunk, cfg.inner, cfg.state, cfg.gw, cfg.groups
    nc = cfg.nchunks
    ltri, rexp = _ssd_consts(cfg)
    rexp_t = rexp.T
    hpt = LANE // cfg.hd
    tiles_per_group = gw // LANE
    bcw = g_n * st

    def body(x_ref, b_ref, c_ref, dt_ref, bias_ref, a_ref, d_ref, ltri_ref, rexp_ref, rexpt_ref, sin_ref, dy_ref,
             dx_ref, ddt_ref, dd_ref, da_ref, dbias_ref, ds_scr):
        step = pl.program_id(1)
        c_idx = nc - 1 - step

        @pl.when(step == 0)
        def _():
            ds_scr[...] = jnp.zeros_like(ds_scr)

        @pl.when(jnp.logical_and(step == 0, pl.program_id(0) == 0))
        def _():
            dd_ref[...] = jnp.zeros_like(dd_ref)
            da_ref[...] = jnp.zeros_like(da_ref)
            dbias_ref[...] = jnp.zeros_like(dbias_ref)

        ltri_v = ltri_ref[...]
        tri, tri_t = _tri_masks(q)
        red = _sel_dot
        rexpt = rexpt_ref[...]
        cq = _ssd_chunk_common(cfg, dt_ref[...], bias_ref[...], a_ref[...], c_idx, ltri_v, rexp_ref[...])
        xs = x_ref[...]
        dyv = dy_ref[...]
        s_in = sin_ref[0]
        d_s = ds_scr[...]
        xdt_f = xs * cq["DT"]
        xdt = xdt_f.astype(BF16)
        xw_f = xdt_f * cq["W0"]
        xw = xw_f.astype(BF16)
        lane = lax.broadcasted_iota(jnp.int32, (q, LANE), 1)
        sub = lax.broadcasted_iota(jnp.int32, (LANE, q), 0)

        dd_ref[...] += jnp.sum(dyv * xs, axis=0, keepdims=True)
        dy0 = dyv * cq["E"]
        dcs = jnp.zeros((q, LANE), F32)
        dcs_t = jnp.zeros((LANE, q), F32)
        for g in range(g_n):
            bg_f = b_ref[:, g * st:(g + 1) * st]
            cg_f = c_ref[:, g * st:(g + 1) * st]
            bg = bg_f.astype(BF16)
            cg = cg_f.astype(BF16)
            gs = slice(g * gw, (g + 1) * gw)
            gmat = _nt(cg, bg)
            gmat_t = _nt(bg, cg)
            sing = s_in[:, gs].astype(BF16)
            dsg = d_s[:, gs].astype(BF16)
            y0 = _nn(cg, sing)
            dxw = _nn(bg, dsg)
            d_bg = _nt(xw[:, gs], dsg)
            d_cg = _nt(dy0[:, gs].astype(BF16), sing)
            ds_in_g = _tn(cg, dy0[:, gs].astype(BF16))
            dg = jnp.zeros((q, q), F32)
            dxdt_g = []
            for tt in range(tiles_per_group):
                tile = g * tiles_per_group + tt
                ts = slice(tile * LANE, (tile + 1) * LANE)
                xt = xdt[:, ts]
                dyt = dyv[:, ts]
                dyhs, lmats, mts = [], [], []
                for hh in range(hpt):
                    lmat, lmat_t = _head_l(cq, tile * hpt + hh, tri, tri_t)
                    inhead = jnp.logical_and(lane >= hh * cfg.hd, lane < (hh + 1) * cfg.hd)
                    dyhs.append(jnp.where(inhead, dyt, 0.0).astype(BF16))
                    lmats.append(lmat)
                    mts.append((gmat_t * lmat_t).astype(BF16))
                dy_stack = jnp.concatenate(dyhs, axis=0)
                dm_all = _nt(dy_stack, xt)
                for hh in range(hpt):
                    h = tile * hpt + hh
                    dm = dm_all[hh * q:(hh + 1) * q, :]
                    dg = dg + dm * lmats[hh]
                    qm = dm * gmat * lmats[hh]
                    rs = jnp.sum(qm, axis=1, keepdims=True)
                    csum = jnp.sum(qm, axis=0, keepdims=True)
                    dcs = dcs + jnp.where(lane == h, rs, 0.0)
                    dcs_t = dcs_t + jnp.where(sub == h, csum, 0.0)
                dxdt_g.append(_nn(jnp.concatenate(mts, axis=1), dy_stack))
            dxdt_diag = jnp.concatenate(dxdt_g, axis=1) if len(dxdt_g) > 1 else dxdt_g[0]
            dgb = dg.astype(BF16)
            d_cg = d_cg + _nn(dgb, bg)
            d_bg = d_bg + _tn(dgb, cg)
            dx_ref[:, inner + g * st:inner + (g + 1) * st] = d_bg
            dx_ref[:, inner + bcw + g * st:inner + bcw + (g + 1) * st] = d_cg
            dxdt = dxdt_diag + dxw * cq["W0"][:, gs]
            dx_ref[:, gs] = dyv[:, gs] * d_ref[:, gs] + dxdt * cq["DT"][:, gs]
            rt = rexpt[gs, :]
            dcs = dcs + red(dyv[:, gs] * y0 * cq["E"][:, gs], rt)
            r_w = red(dxw * xw_f[:, gs], rt)
            dcs = dcs - r_w
            dcs_last_g = jnp.sum(r_w, axis=0, keepdims=True)
            ddec = red(jnp.broadcast_to(jnp.sum(d_s[:, gs] * s_in[:, gs], axis=0, keepdims=True), (8, gw)), rt)[0:1, :]
            dcs_last_g = dcs_last_g + ddec * cq["decay"]
            dcs = dcs + jnp.where(lax.broadcasted_iota(jnp.int32, (q, LANE), 0) == q - 1, dcs_last_g, 0.0)
            ddt_part = red(dxdt * xs[:, gs], rt)
            if g == 0:
                ddt = ddt_part
            else:
                ddt = ddt + ddt_part
            ds_scr[:, gs] = d_s[:, gs] * cq["DEC"][:, gs] + ds_in_g
        dcs = dcs - dcs_t.T
        dadt = _sel_dot(dcs, ltri_v, left=True, trans=True)
        ddt = ddt + dadt * a_ref[...]
        da_ref[...] += jnp.sum(dadt * cq["dt"], axis=0, keepdims=True)
        draw = jnp.where(cq["live"], ddt * jax.nn.sigmoid(cq["pre"]), 0.0)
        ddt_ref[...] = draw
        dbias_ref[...] += jnp.sum(draw, axis=0, keepdims=True)

    def rowblk(width, col):
        return pl.BlockSpec((q, width), lambda b, s: (b * nc + nc - 1 - s, col))

    def const(shape):
        return pl.BlockSpec(shape, lambda b, s: (0, 0))

    bcol = inner // bcw
    outs = pl.pallas_call(
        body, name=name, grid=(cfg.bsz, nc),
        in_specs=[rowblk(inner, 0), rowblk(bcw, bcol), rowblk(bcw, bcol + 1), rowblk(LANE, cfg.dtt),
                  const((1, LANE)), const((1, LANE)), const((1, inner)), const((q, q)), const((LANE, inner)),
                  const((inner, LANE)),
                  pl.BlockSpec((1, st, inner), lambda b, s: (b * nc + nc - 1 - s, 0, 0)), rowblk(inner, 0)],
        out_specs=[rowblk(cfg.conv_dim, 0), rowblk(LANE, 0),
                   const((1, inner)), const((1, LANE)), const((1, LANE))],
        out_shape=[_sds((cfg.t, cfg.conv_dim), F32),
                   _sds((cfg.t, LANE), F32), _sds((1, inner), F32), _sds((1, LANE), F32), _sds((1, LANE), F32)],
        scratch_shapes=[pltpu.VMEM((st, inner), F32)], compiler_params=_cp(),
    )(xc, xc, xc, small, dt_bias, avec, dexp, ltri, rexp, rexp_t, sin, dy)
    return outs


def tail_fwd(cfg, y, z, w, *, name):
    t, inner, gw = cfg.t, cfg.inner, cfg.gw
    tr = _pick(t, 272, 16)

    def body(y_ref, z_ref, w_ref, o_ref):
        for g in range(cfg.groups):
            gs = slice(g * gw, (g + 1) * gw)
            yg = y_ref[:, gs] * _silu(z_ref[:, gs].astype(F32))
            r = lax.rsqrt(jnp.mean(yg * yg, axis=-1, keepdims=True) + EPS)
            o_ref[:, gs] = (yg * r * w_ref[:, gs]).astype(BF16)

    row = pl.BlockSpec((tr, inner), lambda i: (i, 0))
    return pl.pallas_call(
        body, name=name, grid=(t // tr,), in_specs=[row, row, pl.BlockSpec((1, inner), lambda i: (0, 0))],
        out_specs=row, out_shape=_sds((t, inner), BF16), compiler_params=_cp(),
    )(y, z, w.reshape(1, inner))


def tail_bwd(cfg, do, y, z, w, *, name):
    t, inner, gw = cfg.t, cfg.inner, cfg.gw
    tr = _pick(t, 272, 16)

    def body(do_ref, y_ref, z_ref, w_ref, dy_ref, dz_ref, dw_ref):
        @pl.when(pl.program_id(0) == 0)
        def _():
            dw_ref[...] = jnp.zeros_like(dw_ref)

        for g in range(cfg.groups):
            gs = slice(g * gw, (g + 1) * gw)
            yv = y_ref[:, gs]
            zv = z_ref[:, gs].astype(F32)
            dov = do_ref[:, gs]
            sz = _silu(zv)
            yg = yv * sz
            r = lax.rsqrt(jnp.mean(yg * yg, axis=-1, keepdims=True) + EPS)
            xh = yg * r
            gg = dov * w_ref[:, gs]
            dyg = r * (gg - xh * jnp.mean(gg * xh, axis=-1, keepdims=True))
            dw_ref[:, gs] += jnp.sum(dov * xh, axis=0, keepdims=True)
            dy_ref[:, gs] = dyg * sz
            dz_ref[:, gs] = (dyg * yv * _dsilu(zv)).astype(BF16)

    row = pl.BlockSpec((tr, inner), lambda i: (i, 0))
    vec = pl.BlockSpec((1, inner), lambda i: (0, 0))
    dy, dz, dw = pl.pallas_call(
        body, name=name, grid=(t // tr,), in_specs=[row, row, row, vec], out_specs=[row, row, vec],
        out_shape=[_sds((t, inner), F32), _sds((t, inner), BF16), _sds((1, inner), F32)], compiler_params=_cp(),
    )(do, y, z, w.reshape(1, inner))
    return dy, dz, dw[0]


def rope_tables(cfg):
    half = cfg.rope // 2
    pos = np.maximum(np.arange(cfg.lp) - cfg.pad, 0).astype(np.float32)
    inv = ROPE_THETA ** (-jnp.arange(0, cfg.rope, 2, dtype=F32) / cfg.rope)
    ang = jnp.asarray(pos)[:, None] * inv[None, :]
    cos, sin = jnp.cos(ang), jnp.sin(ang)
    zero = jnp.zeros((cfg.lp, LANE - 2 * half), F32)
    zh = jnp.zeros((cfg.lp, half), F32)
    ctab = jnp.concatenate([cos, cos, zero], axis=1)
    s1 = jnp.concatenate([-sin, zh, zero], axis=1)
    s2 = jnp.concatenate([zh, sin, zero], axis=1)
    return ctab, s1, s2


def _rope(x, c, s1, s2, half):
    return x * c + pltpu.roll(x, LANE - half, 1) * s1 + pltpu.roll(x, half, 1) * s2


def _rope_t(dy, c, s1, s2, half):
    return dy * c + pltpu.roll(dy * s1, half, 1) + pltpu.roll(dy * s2, LANE - half, 1)


def _attn_scale(cfg):
    return (cfg.nope + cfg.rope) ** -0.5


def rope_fwd(cfg, qf, small, tabs, *, name):
    t, qw, lp = cfg.t, cfg.qw, cfg.lp
    tr = _pick(lp, 544, 16)
    nrb = lp // tr
    half = cfg.rope // 2
    scale = _attn_scale(cfg)

    def body(q_ref, k_ref, c_ref, s1_ref, s2_ref, qo_ref, ko_ref):
        c, s1, s2 = c_ref[...], s1_ref[...], s2_ref[...]
        for h in range(cfg.mh):
            a = h * 2 * LANE
            qo_ref[:, a:a + LANE] = (q_ref[:, a:a + LANE] * scale).astype(BF16)
            qo_ref[:, a + LANE:a + 2 * LANE] = (_rope(q_ref[:, a + LANE:a + 2 * LANE], c, s1, s2, half) * scale).astype(BF16)
        ko_ref[...] = _rope(k_ref[...], c, s1, s2, half).astype(BF16)

    tab = pl.BlockSpec((tr, LANE), lambda i: (i % nrb, 0))
    return pl.pallas_call(
        body, name=name, grid=(t // tr,),
        in_specs=[pl.BlockSpec((tr, qw), lambda i: (i, 0)), pl.BlockSpec((tr, LANE), lambda i: (i, cfg.kt)), tab, tab, tab],
        out_specs=[pl.BlockSpec((tr, qw), lambda i: (i, 0)), pl.BlockSpec((tr, LANE), lambda i: (i, 0))],
        out_shape=[_sds((t, qw), BF16), _sds((t, LANE), BF16)], compiler_params=_cp(),
    )(qf, small, *tabs)


def rope_bwd(cfg, dq, dkpe, tabs, *, name):
    t, qw, lp = cfg.t, cfg.qw, cfg.lp
    tr = _pick(lp, 544, 16)
    nrb = lp // tr
    half = cfg.rope // 2
    scale = _attn_scale(cfg)

    def body(dq_ref, dk_ref, c_ref, s1_ref, s2_ref, qo_ref, ko_ref):
        c, s1, s2 = c_ref[...], s1_ref[...], s2_ref[...]
        for h in range(cfg.mh):
            a = h * 2 * LANE
            qo_ref[:, a:a + LANE] = (dq_ref[:, a:a + LANE] * scale).astype(BF16)
            qo_ref[:, a + LANE:a + 2 * LANE] = _rope_t(dq_ref[:, a + LANE:a + 2 * LANE] * scale, c, s1, s2, half).astype(BF16)
        dk = dk_ref[0]
        for h in range(1, cfg.mh):
            dk = dk + dk_ref[h]
        ko_ref[...] = _rope_t(dk, c, s1, s2, half)

    tab = pl.BlockSpec((tr, LANE), lambda i: (i % nrb, 0))
    return pl.pallas_call(
        body, name=name, grid=(t // tr,),
        in_specs=[pl.BlockSpec((tr, qw), lambda i: (i, 0)), pl.BlockSpec((cfg.mh, tr, LANE), lambda i: (0, i, 0)),
                  tab, tab, tab],
        out_specs=[pl.BlockSpec((tr, qw), lambda i: (i, 0)), pl.BlockSpec((tr, LANE), lambda i: (i, 0))],
        out_shape=[_sds((t, qw), BF16), _sds((t, LANE), F32)], compiler_params=_cp(),
    )(dq, dkpe, *tabs)


def _q_blocks(cfg):
    bounds = [0, cfg.chunk] + list(range(cfg.chunk + 256, cfg.lp + 1, 256))
    assert bounds[-1] == cfg.lp, "SEQ must be a multiple of 256"
    return list(zip(bounds[:-1], bounds[1:]))


def _attn_mask(cfg, qs, qe):
    rows = qs + lax.broadcasted_iota(jnp.int32, (qe - qs, qe), 0)
    cols = lax.broadcasted_iota(jnp.int32, (qe - qs, qe), 1)
    return jnp.logical_and(cols <= rows, jnp.logical_or(cols >= cfg.pad, rows < cfg.pad))


def _max_q_block(cfg):
    return max(qe - qs for qs, qe in _q_blocks(cfg))


def _masked_scores(cfg, q, k2, qs, qe, s_scr):
    bq, n = qe - qs, qe
    s_scr[0:bq, 0:n] = _nt(q, k2)
    if qs == 0:
        s_scr[0:bq, 0:n] = jnp.where(_attn_mask(cfg, 0, qe), s_scr[0:bq, 0:n], MASK_VALUE)
    else:
        assert qs >= cfg.chunk and cfg.pad < LANE
        cols = lax.broadcasted_iota(jnp.int32, (bq, LANE), 1)
        s_scr[0:bq, 0:LANE] = jnp.where(cols >= cfg.pad, s_scr[0:bq, 0:LANE], MASK_VALUE)
        r = lax.broadcasted_iota(jnp.int32, (bq, bq), 0)
        c = lax.broadcasted_iota(jnp.int32, (bq, bq), 1)
        s_scr[0:bq, qs:qe] = jnp.where(c <= r, s_scr[0:bq, qs:qe], MASK_VALUE)
    return s_scr[0:bq, 0:n]


def attn_fwd(cfg, qr, kv, kpe, *, name):
    lp, t, mh = cfg.lp, cfg.t, cfg.mh
    blocks = _q_blocks(cfg)

    def body(q_ref, kv_ref, kp_ref, o_ref, l_ref, s_scr):
        for qs, qe in blocks:
            n = qe
            q = q_ref[qs:qe, :]
            k2 = jnp.concatenate([kv_ref[0:n, 0:LANE], kp_ref[0:n, :]], axis=1)
            s = _masked_scores(cfg, q, k2, qs, qe, s_scr)
            m = jnp.max(s, axis=-1, keepdims=True)
            p = jnp.exp(s - m)
            l = jnp.sum(p, axis=-1, keepdims=True)
            o_ref[qs:qe, :] = _nn(p.astype(BF16), kv_ref[0:n, LANE:2 * LANE]) * (1.0 / l)
            l_ref[qs:qe, :] = jnp.broadcast_to(m + jnp.log(l), (qe - qs, LANE))

    hb = pl.BlockSpec((lp, 2 * LANE), lambda b, h: (b, h))
    ob = pl.BlockSpec((lp, LANE), lambda b, h: (b, h))
    return pl.pallas_call(
        body, name=name, grid=(cfg.bsz, mh),
        in_specs=[hb, hb, pl.BlockSpec((lp, LANE), lambda b, h: (b, 0))], out_specs=[ob, ob],
        out_shape=[_sds((t, mh * LANE), F32), _sds((t, mh * LANE), F32)],
        scratch_shapes=[pltpu.VMEM((_max_q_block(cfg), lp), F32)], compiler_params=_cp(),
    )(qr, kv, kpe)


def attn_bwd(cfg, qr, kv, kpe, o, lse, do, *, name):
    lp, t, mh = cfg.lp, cfg.t, cfg.mh
    blocks = _q_blocks(cfg)

    def body(q_ref, kv_ref, kp_ref, o_ref, l_ref, do_ref, dq_ref, dkv_ref, dkp_ref, dk_acc, dv_acc, s_scr):
        dk_acc[...] = jnp.zeros_like(dk_acc)
        dv_acc[...] = jnp.zeros_like(dv_acc)
        for qs, qe in blocks:
            n = qe
            q = q_ref[qs:qe, :]
            k2 = jnp.concatenate([kv_ref[0:n, 0:LANE], kp_ref[0:n, :]], axis=1)
            dov = do_ref[qs:qe, :]
            delta = jnp.sum(dov * o_ref[qs:qe, :], axis=-1, keepdims=True)
            dob = dov.astype(BF16)
            s = _masked_scores(cfg, q, k2, qs, qe, s_scr)
            p = jnp.exp(s - l_ref[qs:qe, 0:1])
            dp = _nt(dob, kv_ref[0:n, LANE:2 * LANE])
            ds = (p * (dp - delta)).astype(BF16)
            dq_ref[qs:qe, :] = _nn(ds, k2)
            dv_acc[0:n, :] += _tn(p.astype(BF16), dob)
            dk_acc[0:n, :] += _tn(ds, q)
        dkv_ref[:, 0:LANE] = dk_acc[:, 0:LANE].astype(BF16)
        dkv_ref[:, LANE:2 * LANE] = dv_acc[...].astype(BF16)
        dkp_ref[0] = dk_acc[:, LANE:2 * LANE]

    hb = pl.BlockSpec((lp, 2 * LANE), lambda b, h: (b, h))
    ob = pl.BlockSpec((lp, LANE), lambda b, h: (b, h))
    return pl.pallas_call(
        body, name=name, grid=(cfg.bsz, mh),
        in_specs=[hb, hb, pl.BlockSpec((lp, LANE), lambda b, h: (b, 0)), ob, ob, ob],
        out_specs=[hb, hb, pl.BlockSpec((1, lp, LANE), lambda b, h: (h, b, 0))],
        out_shape=[_sds((t, cfg.qw), F32), _sds((t, mh * 2 * LANE), BF16), _sds((mh, t, LANE), F32)],
        scratch_shapes=[pltpu.VMEM((lp, 2 * LANE), F32), pltpu.VMEM((lp, LANE), F32),
                        pltpu.VMEM((_max_q_block(cfg), lp), F32)], compiler_params=_cp(),
    )(qr, kv, kpe, o, lse, do)


def _live_rows(cfg, tr, shape):
    rows = pl.program_id(1) * tr + lax.broadcasted_iota(jnp.int32, shape, 0)
    return rows >= cfg.pad


def gate_fwd(cfg, ya, yb, g, *, name):
    d, lp = cfg.d, cfg.lp
    tr = _pick(lp, 544, 16)
    nrb = lp // tr

    def body(ya_ref, yb_ref, ga_ref, gb_ref, o_ref):
        f = lambda ref: ref[...].astype(F32)
        mix = jax.nn.sigmoid(f(ga_ref)) * f(ya_ref) + jax.nn.sigmoid(f(gb_ref)) * f(yb_ref)
        o_ref[...] = jnp.where(_live_rows(cfg, tr, mix.shape), mix, 0.0).astype(BF16)

    row = pl.BlockSpec((tr, d), lambda b, j: (b * nrb + j, 0))
    row1 = pl.BlockSpec((tr, d), lambda b, j: (b * nrb + j, 1))
    return pl.pallas_call(
        body, name=name, grid=(cfg.bsz, nrb), in_specs=[row, row, row, row1], out_specs=row,
        out_shape=_sds((cfg.t, d), BF16), compiler_params=_cp(),
    )(ya, yb, g, g)


def gate_bwd(cfg, dmix, ya, yb, g, *, name):
    d, lp = cfg.d, cfg.lp
    tr = _pick(lp, 544, 16)
    nrb = lp // tr

    def body(dm_ref, ya_ref, yb_ref, ga_ref, gb_ref, dya_ref, dyb_ref, dg_ref):
        dm = dm_ref[...]
        dm = jnp.where(_live_rows(cfg, tr, dm.shape), dm, 0.0)
        sa = jax.nn.sigmoid(ga_ref[...].astype(F32))
        sb = jax.nn.sigmoid(gb_ref[...].astype(F32))
        dya_ref[...] = (dm * sa).astype(BF16)
        dyb_ref[...] = (dm * sb).astype(BF16)
        dg_ref[:, 0:d] = (dm * ya_ref[...].astype(F32) * sa * (1.0 - sa)).astype(BF16)
        dg_ref[:, d:2 * d] = (dm * yb_ref[...].astype(F32) * sb * (1.0 - sb)).astype(BF16)

    row = pl.BlockSpec((tr, d), lambda b, j: (b * nrb + j, 0))
    row1 = pl.BlockSpec((tr, d), lambda b, j: (b * nrb + j, 1))
    row2 = pl.BlockSpec((tr, 2 * d), lambda b, j: (b * nrb + j, 0))
    return pl.pallas_call(
        body, name=name, grid=(cfg.bsz, nrb), in_specs=[row, row, row, row, row1], out_specs=[row, row, row2],
        out_shape=[_sds((cfg.t, d), BF16), _sds((cfg.t, d), BF16), _sds((cfg.t, 2 * d), BF16)], compiler_params=_cp(),
    )(dmix, ya, yb, g, g)


def loss_head(cfg, h, target, w, *, name):
    d, q, nc = cfg.d, cfg.chunk, cfg.nchunks
    tpb = cfg.seq // q

    def body(h_ref, t_ref, w_ref, loss_ref, dh_ref, dw_ref):
        j = pl.program_id(1)

        @pl.when(jnp.logical_and(j == 0, pl.program_id(0) == 0))
        def _():
            loss_ref[...] = jnp.zeros_like(loss_ref)
            dw_ref[...] = jnp.zeros_like(dw_ref)

        @pl.when(j == 0)
        def _():
            dh_ref[...] = jnp.zeros_like(dh_ref)

        @pl.when(j > 0)
        def _():
            xv = h_ref[...]
            r = lax.rsqrt(jnp.mean(xv * xv, axis=-1, keepdims=True) + EPS)
            xh = xv * r
            err = xh * w_ref[...] - t_ref[...]
            loss_ref[...] += 0.5 * jnp.sum(jnp.sum(err * err, axis=-1, keepdims=True) / d, axis=0, keepdims=True)
            dy = err * (1.0 / d)
            g = dy * w_ref[...]
            dh_ref[...] = r * (g - xh * jnp.mean(g * xh, axis=-1, keepdims=True))
            dw_ref[...] += jnp.sum(dy * xh, axis=0, keepdims=True)

    row = pl.BlockSpec((q, d), lambda b, j: (b * nc + j, 0))
    loss, dh, dw = pl.pallas_call(
        body, name=name, grid=(cfg.bsz, nc),
        in_specs=[row, pl.BlockSpec((q, d), lambda b, j: (b * tpb + jnp.maximum(j - 1, 0), 0)),
                  pl.BlockSpec((1, d), lambda b, j: (0, 0))],
        out_specs=[pl.BlockSpec((8, LANE), lambda b, j: (0, 0)), row, pl.BlockSpec((1, d), lambda b, j: (0, 0))],
        out_shape=[_sds((8, LANE), F32), _sds((cfg.t, d), F32), _sds((1, d), F32)], compiler_params=_cp(),
    )(h, target, w.reshape(1, d))
    return loss[0, 0], dh, dw[0]


def _rows_tile(r, c):
    return _pick(r, max(8, (1 << 18) // max(c, 1) // 8 * 8), 8)


def _adam_update(w, g, m, v):
    c1 = 1.0 - ADAM_B1 ** ADAM_STEP
    c2 = 1.0 - ADAM_B2 ** ADAM_STEP
    mn = ADAM_B1 * m + (1.0 - ADAM_B1) * g
    vn = ADAM_B2 * v + (1.0 - ADAM_B2) * (g * g)
    delta = -ADAM_LR * ((mn / c1) / (jnp.sqrt(vn / c2) + ADAM_EPS) + ADAM_WD * w)
    return delta, mn, vn


def adamw_layer(w, m, v, g, li, prev, dep, *, name):
    _, r, c = w.shape
    tr = _rows_tile(r, c)

    def body(*refs):
        w_ref, m_ref, v_ref, g_ref = refs[:4]
        go_ref, d_ref, mo_ref, vo_ref = refs[-4:]
        gv = g_ref[...]
        delta, mn, vn = _adam_update(w_ref[0], gv, m_ref[0], v_ref[0])
        go_ref[0] = gv
        d_ref[0] = delta
        mo_ref[0] = mn
        vo_ref[0] = vn

    if tr * c * 4 >= (1 << 16):
        steps = r // tr
        blk3 = pl.BlockSpec((1, tr, c), lambda i: (li, i, 0))
        blk2 = pl.BlockSpec((tr, c), lambda i: (i, 0))
    else:
        tc = _pick(c, max(LANE, (1 << 18) // r // LANE * LANE), LANE)
        steps = c // tc
        blk3 = pl.BlockSpec((1, r, tc), lambda i: (li, 0, i))
        blk2 = pl.BlockSpec((r, tc), lambda i: (0, i))
    anyspec = pl.BlockSpec(memory_space=pl.ANY)
    in_specs = [blk3, blk3, blk3, blk2, anyspec]
    args = [w, m, v, g, dep]
    aliases = {}
    if prev is not None:
        in_specs += [anyspec] * 4
        args += list(prev)
        aliases = {5 + i: i for i in range(4)}
    return pl.pallas_call(
        body, name=name, grid=(steps,), in_specs=in_specs, out_specs=[blk3] * 4,
        out_shape=[_sds(w.shape, F32)] * 4, input_output_aliases=aliases, compiler_params=_cp(),
    )(*args)


def pair_add(g4, other, half, *, name):
    n, _, r, c = g4.shape
    tr = _rows_tile(r, c)

    def body(h_ref, a_ref, b_ref, o_ref):
        o_ref[0] = (a_ref[0, 0].astype(F32) + b_ref[0].astype(F32)).astype(BF16)

    blk = pl.BlockSpec((1, tr, c), lambda j, i, h: (j, i, 0))
    grid_spec = pltpu.PrefetchScalarGridSpec(
        num_scalar_prefetch=1, grid=(n, r // tr),
        in_specs=[pl.BlockSpec((1, 1, tr, c), lambda j, i, h: (j, h[0], i, 0)), blk], out_specs=blk)
    return pl.pallas_call(body, name=name, grid_spec=grid_spec, out_shape=_sds((n, r, c), BF16),
                          compiler_params=_cp())(half, g4, other)


def chip_sum(recv, part, where, *, name):
    n, r, c = recv.shape
    tr = _rows_tile(r, c)

    def body(s_ref, *refs):
        own_ref, o_ref = refs[n], refs[n + 1]
        acc = None
        for j in range(n):
            term = jnp.where(s_ref[0] == j, own_ref[0], refs[j][0]).astype(F32)
            acc = term if acc is None else acc + term
        o_ref[0] = acc

    def slot(j):
        return pl.BlockSpec((1, tr, c), lambda i, s: (jnp.where(s[0] == j, (j + 1) % n, j), i, 0))

    grid_spec = pltpu.PrefetchScalarGridSpec(
        num_scalar_prefetch=1, grid=(r // tr,),
        in_specs=[slot(j) for j in range(n)] + [pl.BlockSpec((1, tr, c), lambda i, s: (s[0], i, 0))],
        out_specs=pl.BlockSpec((1, tr, c), lambda i, s: (s[1], i, 0)))
    return pl.pallas_call(body, name=name, grid_spec=grid_spec, out_shape=_sds((2, r, c), F32),
                          compiler_params=_cp())(where, *([recv] * n), part)


def _coords():
    return lax.axis_index("x"), lax.axis_index("y"), lax.axis_index("c")


def _other_chips(x, y):
    return [(1 - x, y), (x, 1 - y), (1 - x, 1 - y)]


def gather_chips(arrs, *, name):
    n = len(arrs)
    anyspec = pl.BlockSpec(memory_space=pl.ANY)

    def body(*refs):
        ins, outs = refs[:n], refs[n:2 * n]
        send_sems, recv_sems, local_sems = refs[2 * n:]
        x, y, c = _coords()
        me = 2 * x + y
        chips = _other_chips(x, y)
        copies = []
        for k in range(n):
            loc = pltpu.make_async_copy(ins[k], outs[k].at[me], local_sems.at[k])
            loc.start()
            copies.append(loc)
        sends = []
        for k in range(n):
            for j, (px, py) in enumerate(chips):
                cp = pltpu.make_async_remote_copy(
                    src_ref=ins[k], dst_ref=outs[k].at[me], send_sem=send_sems.at[k, j], recv_sem=recv_sems.at[k, j],
                    device_id=(px, py, c), device_id_type=MESH)
                cp.start()
                sends.append(cp)
        for k in range(n):
            for j, (px, py) in enumerate(chips):
                pltpu.make_async_remote_copy(
                    src_ref=ins[k], dst_ref=outs[k].at[2 * px + py], send_sem=send_sems.at[k, j],
                    recv_sem=recv_sems.at[k, j], device_id=(px, py, c), device_id_type=MESH).wait_recv()
        for cp in sends:
            cp.wait_send()
        for cp in copies:
            cp.wait()

    return pl.pallas_call(
        body, name=name, in_specs=[anyspec] * n, out_specs=[anyspec] * n,
        out_shape=[_sds((4,) + a.shape, a.dtype) for a in arrs],
        scratch_shapes=[pltpu.SemaphoreType.DMA((n, 3)), pltpu.SemaphoreType.DMA((n, 3)), pltpu.SemaphoreType.DMA((n,))],
        compiler_params=_cp(has_side_effects=True),
    )(*arrs)


def allreduce_small(vec, after, *, name):
    r, c = vec.shape

    def body(v_ref, after_ref, o_ref, buf, send_sems, recv_sems):
        x, y, cc = _coords()
        me = 4 * x + 2 * y + cc
        buf[me] = v_ref[...]
        sends = []
        flips = [(fx, fy, fc) for fx in (0, 1) for fy in (0, 1) for fc in (0, 1)][1:]
        for j, (fx, fy, fc) in enumerate(flips):
            peer = ((1 - x) if fx else x, (1 - y) if fy else y, (1 - cc) if fc else cc)
            cp = pltpu.make_async_remote_copy(
                src_ref=v_ref, dst_ref=buf.at[me], send_sem=send_sems.at[j], recv_sem=recv_sems.at[j],
                device_id=peer, device_id_type=MESH)
            cp.start()
            sends.append(cp)
        for j, (fx, fy, fc) in enumerate(flips):
            px, py, pc = ((1 - x) if fx else x, (1 - y) if fy else y, (1 - cc) if fc else cc)
            pltpu.make_async_remote_copy(
                src_ref=v_ref, dst_ref=buf.at[4 * px + 2 * py + pc], send_sem=send_sems.at[j],
                recv_sem=recv_sems.at[j], device_id=(px, py, pc), device_id_type=MESH).wait_recv()
        for cp in sends:
            cp.wait_send()
        acc = buf[0]
        for k in range(1, 8):
            acc = acc + buf[k]
        o_ref[...] = acc

    vm = pl.BlockSpec(memory_space=pltpu.VMEM)
    return pl.pallas_call(
        body, name=name, in_specs=[vm, pl.BlockSpec(memory_space=pl.ANY)], out_specs=vm, out_shape=_sds((r, c), F32),
        scratch_shapes=[pltpu.VMEM((8, r, c), F32), pltpu.SemaphoreType.DMA((7,)), pltpu.SemaphoreType.DMA((7,))],
        compiler_params=_cp(has_side_effects=True),
    )(vec, after)


def pair_share(lands, owns, *, name):
    n = len(lands)
    anyspec = pl.BlockSpec(memory_space=pl.ANY)

    def body(*refs):
        ins, own_refs, outs = refs[:n], refs[n:2 * n], refs[2 * n:3 * n]
        send_sems, recv_sems = refs[3 * n:]
        x, y, c = _coords()
        me = 2 * x + y
        sib = (x, y, 1 - c)
        sends = []
        for k in range(n):
            for j, (px, py) in enumerate(_other_chips(x, y)):
                cp = pltpu.make_async_remote_copy(
                    src_ref=ins[k].at[2 * px + py, c], dst_ref=outs[k].at[2 * px + py, c], send_sem=send_sems.at[k, j],
                    recv_sem=recv_sems.at[k, j], device_id=sib, device_id_type=MESH)
                cp.start()
                sends.append(cp)
            cp = pltpu.make_async_remote_copy(
                src_ref=own_refs[k], dst_ref=outs[k].at[me], send_sem=send_sems.at[k, 3], recv_sem=recv_sems.at[k, 3],
                device_id=sib, device_id_type=MESH)
            cp.start()
            sends.append(cp)
        for k in range(n):
            for j, (px, py) in enumerate(_other_chips(x, y)):
                pltpu.make_async_remote_copy(
                    src_ref=ins[k].at[2 * px + py, c], dst_ref=outs[k].at[2 * px + py, 1 - c],
                    send_sem=send_sems.at[k, j], recv_sem=recv_sems.at[k, j], device_id=sib,
                    device_id_type=MESH).wait_recv()
            pltpu.make_async_remote_copy(
                src_ref=own_refs[k], dst_ref=outs[k].at[me], send_sem=send_sems.at[k, 3], recv_sem=recv_sems.at[k, 3],
                device_id=sib, device_id_type=MESH).wait_recv()
        for cp in sends:
            cp.wait_send()

    return pl.pallas_call(
        body, name=name, in_specs=[anyspec] * (2 * n), out_specs=[anyspec] * n,
        out_shape=[_sds(a.shape, a.dtype) for a in lands], input_output_aliases={k: k for k in range(n)},
        scratch_shapes=[pltpu.SemaphoreType.DMA((n, 4)), pltpu.SemaphoreType.DMA((n, 4))],
        compiler_params=_cp(has_side_effects=True),
    )(*lands, *owns)


def pair_fill(arrs, *, name):
    n = len(arrs)
    anyspec = pl.BlockSpec(memory_space=pl.ANY)

    def body(*refs):
        ins, outs = refs[:n], refs[n:2 * n]
        send_sems, recv_sems = refs[2 * n:]
        x, y, c = _coords()
        sends = []
        for k in range(n):
            cp = pltpu.make_async_remote_copy(
                src_ref=ins[k].at[c], dst_ref=outs[k].at[c], send_sem=send_sems.at[k], recv_sem=recv_sems.at[k],
                device_id=(x, y, 1 - c), device_id_type=MESH)
            cp.start()
            sends.append(cp)
        for k in range(n):
            pltpu.make_async_remote_copy(
                src_ref=ins[k].at[c], dst_ref=outs[k].at[1 - c], send_sem=send_sems.at[k], recv_sem=recv_sems.at[k],
                device_id=(x, y, 1 - c), device_id_type=MESH).wait_recv()
        for cp in sends:
            cp.wait_send()

    return pl.pallas_call(
        body, name=name, in_specs=[anyspec] * n, out_specs=[anyspec] * n,
        out_shape=[_sds(a.shape, a.dtype) for a in arrs], input_output_aliases={k: k for k in range(n)},
        scratch_shapes=[pltpu.SemaphoreType.DMA((n,)), pltpu.SemaphoreType.DMA((n,))],
        compiler_params=_cp(has_side_effects=True),
    )(*arrs)


_HBM = pl.BlockSpec(memory_space=pltpu.HBM)
_SEM = pl.BlockSpec(memory_space=pltpu.SEMAPHORE)


_COPIES_PER_ARRAY = {"gather": 3, "scatter": 3, "share": 4, "exchange": 4}


def _ici_copies(kind, srcs, lands, send_sems, recv_sems):
    x, y, c = _coords()
    me = 2 * x + y
    per = _COPIES_PER_ARRAY[kind]
    sends, recvs = [], []
    for k in range(len(srcs)):
        triples = []
        for j, (px, py) in enumerate(_other_chips(x, y)):
            peer = 2 * px + py
            if kind == "gather":
                triples.append((srcs[k].at[c], lands[k].at[me, c], lands[k].at[peer, c], (px, py, c)))
            elif kind == "scatter":
                triples.append((srcs[k].at[peer], lands[k].at[me], lands[k].at[peer], (px, py, c)))
            elif kind == "share":
                triples.append((lands[k].at[peer, c], lands[k].at[peer, c], lands[k].at[peer, 1 - c], (x, y, 1 - c)))
        if kind == "share":
            triples.append((srcs[k], lands[k].at[me], lands[k].at[me], (x, y, 1 - c)))
        if kind == "exchange":
            triples = [(srcs[k].at[j, 1 - c], lands[k].at[j], lands[k].at[j], (x, y, 1 - c)) for j in range(4)]
        for j, (src, there, here, dev) in enumerate(triples):
            sem = per * k + j
            mk = functools.partial(pltpu.make_async_remote_copy, src_ref=src, send_sem=send_sems.at[sem],
                                   recv_sem=recv_sems.at[sem], device_id=dev, device_id_type=MESH)
            sends.append(mk(dst_ref=there))
            recvs.append(mk(dst_ref=here))
    return sends, recvs


def ici_start(kind, srcs, lands, after, *, name):
    n = len(srcs)

    def body(*refs):
        src_refs, land_refs = refs[:n], refs[n:2 * n]
        send_sems, recv_sems = refs[2 * n + 1], refs[2 * n + 2]
        token = refs[-1]
        sends, _ = _ici_copies(kind, src_refs, land_refs, send_sems, recv_sems)
        for cp in sends:
            cp.start()
        token[...] = jnp.zeros_like(token)

    both = list(srcs) + list(lands)
    out = pl.pallas_call(
        body, name=name,
        in_specs=[_HBM] * (2 * n) + [pl.BlockSpec(memory_space=pl.ANY)],
        out_shape=(pltpu.SemaphoreType.DMA((_COPIES_PER_ARRAY[kind] * n,)),
                   pltpu.SemaphoreType.DMA((_COPIES_PER_ARRAY[kind] * n,)),
                   *[pltpu.HBM(a.shape, a.dtype) for a in both], _sds((8, LANE), F32)),
        out_specs=(_SEM, _SEM, *([_HBM] * (2 * n)), pl.BlockSpec(memory_space=pltpu.VMEM)),
        input_output_aliases={i: 2 + i for i in range(2 * n)},
        compiler_params=_cp(has_side_effects=pltpu.SideEffectType.DATAFLOW_SIDE_EFFECTING),
    )(*[pltpu.with_memory_space_constraint(a, pltpu.HBM) for a in both], after)
    return out[0], out[1], list(out[2:2 + n]), list(out[2 + n:2 + 2 * n]), out[-1]


def ici_wait(kind, started, after, *, name):
    send_sems, recv_sems, srcs, lands, _ = started
    n = len(srcs)

    def body(*refs):
        src_refs, land_refs = refs[:n], refs[n:2 * n]
        sends, recvs = _ici_copies(kind, src_refs, land_refs, refs[2 * n], refs[2 * n + 1])
        for cp in sends:
            cp.wait_send()
        for cp in recvs:
            cp.wait_recv()

    both = list(srcs) + list(lands)
    out = pl.pallas_call(
        body, name=name,
        in_specs=[_HBM] * (2 * n) + [_SEM, _SEM, pl.BlockSpec(memory_space=pl.ANY)],
        out_shape=tuple(pltpu.HBM(a.shape, a.dtype) for a in both), out_specs=tuple([_HBM] * (2 * n)),
        input_output_aliases={i: i for i in range(2 * n)},
        compiler_params=_cp(has_side_effects=pltpu.SideEffectType.DATAFLOW_SIDE_EFFECTING),
    )(*both, send_sems, recv_sems, after)
    return list(out[:n]), list(out[n:])


BIG = ["w_in", "w_uq", "w_ukv", "w_branch_ssm", "w_branch_mla", "w_out", "w_mlp_up", "w_mlp_down"]
COL_SHARDED = {"w_in", "w_uq", "w_ukv", "w_mlp_up"}
SMALL_REPL = ["norm_mix_w", "conv_b", "dt_bias", "a_log", "d_skip", "ssm_norm_w", "q_norm_w", "kv_norm_w", "norm_mlp_w"]


def _unshard_layer(name, g):
    _, r, c = g.shape
    if name in COL_SHARDED:
        return jnp.transpose(g, (1, 0, 2)).reshape(r, 4 * c)
    return g.reshape(4 * r, c)


def _to_shards(name, full):
    r, c = full.shape
    if name in COL_SHARDED:
        return jnp.transpose(full.reshape(r, 4, c // 4), (1, 0, 2))
    return full.reshape(4, r // 4, c)


REST = [k for k in BIG if k != "w_in"]


def prep_layer(cfg, w):
    out = {}
    if "w_in" in w:
        sp = np.cumsum(cfg.in_splits)[:-1].tolist()
        z, xbc, dt, cq, ckv, kr, gs, gm = jnp.split(w["w_in"], sp, axis=1)
        zpad = lambda n: jnp.zeros((cfg.d, n), z.dtype)
        out.update(w_z=z, w_xbc=xbc, w_g=jnp.concatenate([gs, gm], axis=1),
                   w_s=jnp.concatenate([cq, ckv, kr, zpad(LANE - cfg.rope), dt, zpad(LANE - cfg.heads)], axis=1))
    if "w_uq" in w:
        out.update(
            w_uq=jnp.pad(w["w_uq"].reshape(cfg.ql, cfg.mh, cfg.nope + cfg.rope),
                         ((0, 0), (0, 0), (0, 2 * LANE - cfg.nope - cfg.rope))).reshape(cfg.ql, cfg.qw),
            w_ukv=w["w_ukv"], w_bs=w["w_branch_ssm"], w_bm=w["w_branch_mla"], w_out=w["w_out"],
            w_up=w["w_mlp_up"], w_down=w["w_mlp_down"])
    return {k: v.astype(BF16) for k, v in out.items()}


def unprep_grads(cfg, g):
    out = {}
    if "w_s" in g:
        ql, kvl = cfg.ql, cfg.kvl
        ds_ = g["w_s"]
        cq, ckv = ds_[:, :ql], ds_[:, ql:ql + kvl]
        kr = ds_[:, ql + kvl:ql + kvl + cfg.rope]
        dt = ds_[:, ql + kvl + LANE:ql + kvl + LANE + cfg.heads]
        out["w_in"] = jnp.concatenate([g["w_z"], g["w_xbc"], dt, cq, ckv, kr, g["w_g"]], axis=1)
    if "w_uq" in g:
        out.update(
            w_uq=g["w_uq"].reshape(cfg.ql, cfg.mh, 2 * LANE)[:, :, :cfg.nope + cfg.rope].reshape(cfg.ql, -1),
            w_ukv=g["w_ukv"], w_branch_ssm=g["w_bs"], w_branch_mla=g["w_bm"],
            w_out=g["w_out"], w_mlp_up=g["w_up"], w_mlp_down=g["w_down"])
    return out


def _hook(hooks, name, arg):
    if hooks and name in hooks:
        return hooks[name](arg)[0, 0]
    return 0.0


def layer_fwd(cfg, h, pw, sm, tabs, li, hooks=None):
    n = lambda s: f"l{li}_{s}"
    u = rmsnorm_fwd(h, sm["norm_mix_w"], name=n("norm_mix"))
    z = matmul(u, pw["w_z"], out_dtype=BF16, name=n("in_z"))
    xbc = matmul(u, pw["w_xbc"], name=n("in_xbc"))
    g = matmul(u, pw["w_g"], out_dtype=BF16, name=n("in_g"))
    small = matmul(u, pw["w_s"], name=n("in_s"))
    xc = conv_fwd(cfg, xbc, sm["conv_w"], sm["conv_b"], name=n("conv"))
    dt_bias = sm["dt_bias_p"] + _hook(hooks, "after_conv", xc)
    y, sin = ssd_fwd(cfg, xc, small, dt_bias, sm["avec"], sm["dexp"], name=n("ssd"))
    y_ssm = tail_fwd(cfg, y, z, sm["ssm_norm_w"], name=n("tail"))
    if hooks and "weights" in hooks:
        pw = dict(pw, **hooks["weights"](y_ssm))
    cqn = rmsnorm_fwd(small, sm["q_norm_w"], cw=cfg.ql, ci=0, name=n("q_norm"))
    ckvn = rmsnorm_fwd(small, sm["kv_norm_w"], cw=cfg.kvl, ci=cfg.ql // cfg.kvl, name=n("kv_norm"))
    qf = matmul(cqn, pw["w_uq"], name=n("uq"))
    kv = matmul(ckvn, pw["w_ukv"], out_dtype=BF16, name=n("ukv"))
    qr, kpe = rope_fwd(cfg, qf, small, tabs, name=n("rope"))
    o, lse = attn_fwd(cfg, qr, kv, kpe, name=n("attn"))
    ya = matmul(y_ssm, pw["w_bs"], out_dtype=BF16, name=n("branch_ssm"))
    yb = matmul(o, pw["w_bm"], out_dtype=BF16, name=n("branch_mla"))
    mixed = gate_fwd(cfg, ya, yb, g, name=n("gate"))
    h1 = matmul(mixed, pw["w_out"], add=h, name=n("out"))
    v = rmsnorm_fwd(h1, sm["norm_mlp_w"] + _hook(hooks, "after_attn", o), name=n("norm_mlp"))
    a, act = matmul(v, pw["w_up"], name=n("up"), epilogue=_ep_relu2, out_dtypes=(BF16, BF16))
    h2 = matmul(act, pw["w_down"], add=h1, name=n("down"))
    saved = dict(h=h, u=u, z=z, xbc=xbc, g=g, small=small, xc=xc, y=y, sin=sin, y_ssm=y_ssm, cqn=cqn, ckvn=ckvn,
                 qr=qr, kv=kv, kpe=kpe, o=o, lse=lse, ya=ya, yb=yb, mixed=mixed, h1=h1, v=v, a=a, act=act)
    return h2, saved, pw


def layer_bwd(cfg, dh2, pw, sm, tabs, s, li, hooks=None):
    n = lambda t: f"l{li}_b_{t}"
    gw, gs = {}, {}
    wgrad = functools.partial(matmul, ta=True, out_dtype=BF16)
    gw["w_down"] = wgrad(s["act"], dh2, name=n("dw_down"))
    da = matmul(dh2, pw["w_down"], tb=True, name=n("dact"), epilogue=_ep_relu2_grad, extras=(s["a"],),
                out_dtypes=(BF16,))
    gw["w_up"] = wgrad(s["v"], da, name=n("dw_up"))
    dv = matmul(da, pw["w_up"], tb=True, name=n("dv"))
    dh1, gs["norm_mlp_w"] = rmsnorm_bwd(dv, s["h1"], sm["norm_mlp_w"], res=dh2, name=n("norm_mlp"))
    gw["w_out"] = wgrad(s["mixed"], dh1, name=n("dw_out"))
    dmix = matmul(dh1, pw["w_out"], tb=True, name=n("dmix"))
    dya, dyb, dg = gate_bwd(cfg, dmix, s["ya"], s["yb"], s["g"], name=n("gate"))
    gw["w_bs"] = wgrad(s["y_ssm"], dya, name=n("dw_bs"))
    gw["w_bm"] = wgrad(s["o"], dyb, name=n("dw_bm"))
    dy_ssm = matmul(dya, pw["w_bs"], tb=True, name=n("dy_ssm"))
    do = matmul(dyb, pw["w_bm"], tb=True, name=n("do"))
    dq, dkv, dkpe = attn_bwd(cfg, s["qr"], s["kv"], s["kpe"], s["o"], s["lse"], do, name=n("attn"))
    dqf, dkr = rope_bwd(cfg, dq, dkpe, tabs, name=n("rope"))
    gw["w_uq"] = wgrad(s["cqn"], dqf, name=n("dw_uq"))
    gw["w_ukv"] = wgrad(s["ckvn"], dkv, name=n("dw_ukv"))
    dcqn = matmul(dqf, pw["w_uq"], tb=True, name=n("dcqn"))
    dckvn = matmul(dkv, pw["w_ukv"], tb=True, name=n("dckvn"))
    q_norm_w = sm["q_norm_w"] + _hook(hooks, "after_attn", dqf)
    dcq, gs["q_norm_w"] = rmsnorm_bwd(dcqn, s["small"], q_norm_w, cw=cfg.ql, ci=0, out_dtype=BF16, name=n("q_norm"))
    dckv, gs["kv_norm_w"] = rmsnorm_bwd(dckvn, s["small"], sm["kv_norm_w"], cw=cfg.kvl, ci=cfg.ql // cfg.kvl,
                                        out_dtype=BF16, name=n("kv_norm"))
    ssm_norm_w = sm["ssm_norm_w"] + _hook(hooks, "early", dict(gw))
    dy, dz, gs["ssm_norm_w"] = tail_bwd(cfg, dy_ssm, s["y"], s["z"], ssm_norm_w, name=n("tail"))
    dxc, ddt, ddexp, dav, dbias = ssd_bwd(cfg, s["xc"], s["small"], sm["dt_bias_p"], sm["avec"], sm["dexp"],
                                          s["sin"], dy, name=n("ssd"))
    conv_b = sm["conv_b"] + _hook(hooks, "after_ssd", dxc)
    dxbc, gs["conv_w"], gs["conv_b"] = conv_bwd(cfg, s["xbc"], sm["conv_w"], conv_b, dxc, name=n("conv"))
    gs["d_skip"] = ddexp.reshape(cfg.heads, cfg.hd).sum(axis=1)
    gs["a_log"] = (dav[0] * sm["avec"][0])[:cfg.heads]
    gs["dt_bias"] = dbias[0, :cfg.heads]
    dsmall = jnp.concatenate([dcq, dckv, dkr.astype(BF16), ddt.astype(BF16)], axis=1)
    gw["w_z"] = wgrad(s["u"], dz, name=n("dw_z"))
    gw["w_xbc"] = wgrad(s["u"], dxbc, name=n("dw_xbc"))
    gw["w_g"] = wgrad(s["u"], dg, name=n("dw_g"))
    gw["w_s"] = wgrad(s["u"], dsmall, name=n("dw_s"))
    du = matmul(dz, pw["w_z"], tb=True, name=n("du_z"))
    du = matmul(dxbc, pw["w_xbc"], tb=True, add=du, name=n("du_xbc"))
    du = matmul(dg, pw["w_g"], tb=True, add=du, name=n("du_g"))
    du = matmul(dsmall, pw["w_s"], tb=True, add=du, name=n("du_s"))
    dh, gs["norm_mix_w"] = rmsnorm_bwd(du, s["h"], sm["norm_mix_w"], res=dh1, name=n("norm_mix"))
    return dh, gw, gs


def small_params(cfg, p, li):
    pad_l = lambda v: jnp.pad(v, (0, LANE - v.shape[0])).reshape(1, LANE)
    return dict(
        norm_mix_w=p["norm_mix_w"][li], conv_w=p["conv_w"][li], conv_b=p["conv_b"][li],
        dt_bias_p=pad_l(p["dt_bias"][li]), avec=pad_l(-jnp.exp(p["a_log"][li])),
        dexp=jnp.repeat(p["d_skip"][li], cfg.hd).reshape(1, cfg.inner),
        ssm_norm_w=p["ssm_norm_w"][li], q_norm_w=p["q_norm_w"][li], kv_norm_w=p["kv_norm_w"][li],
        norm_mlp_w=p["norm_mlp_w"][li])


def local_step(cfg, x, target, p, depth=2):
    bsz, d = cfg.bsz, cfg.d
    lead = jnp.zeros((bsz, cfg.pad, d), F32)
    meta = jnp.broadcast_to(p["meta_tokens"][None], (bsz, cfg.n_meta, d))
    h = jnp.concatenate([lead, meta, x], axis=1).reshape(cfg.t, d)
    tabs = rope_tables(cfg)
    saved, sms = [], []
    for li in range(depth):
        sm = small_params(cfg, p, li)
        h, s, _ = layer_fwd(cfg, h, p["pw"][li], sm, tabs, li)
        saved.append(s)
        sms.append(sm)
    loss, dh, dfw = loss_head(cfg, h, target.reshape(bsz * cfg.seq, d), p["final_norm_w"], name="loss_head")
    gws, gss = [None] * depth, [None] * depth
    for li in reversed(range(depth)):
        dh, gws[li], gss[li] = layer_bwd(cfg, dh, p["pw"][li], sms[li], tabs, saved[li], li)
    dh = dh.reshape(bsz, cfg.lp, d)
    grad_x = dh[:, cfg.chunk:, :]
    gmeta = jnp.sum(dh[:, cfg.pad:cfg.chunk, :], axis=0)
    return loss, grad_x, gmeta, gws, gss, dfw


def _pack_small(parts):
    flat = jnp.concatenate([a.reshape(-1) for a in parts])
    n = flat.shape[0]
    npad = -n % (8 * LANE)
    return jnp.pad(flat, (0, npad)).reshape(-1, LANE), n


def _unpack_small(vec, shapes):
    flat = vec.reshape(-1)
    out, off = [], 0
    for sh in shapes:
        sz = int(np.prod(sh))
        out.append(flat[off:off + sz].reshape(sh))
        off += sz
    return out


def _as2d(a):
    return a.reshape(-1, a.shape[-1])


def kernel(x, meta_tokens, norm_mix_w, w_in, conv_w, conv_b, dt_bias, a_log, d_skip, ssm_norm_w, q_norm_w, kv_norm_w, w_uq, w_ukv, w_branch_ssm, w_branch_mla, w_out, norm_mlp_w, w_mlp_up, w_mlp_down, final_norm_w, loss_target, m_meta_tokens, m_norm_mix_w, m_w_in, m_conv_w, m_conv_b, m_dt_bias, m_a_log, m_d_skip, m_ssm_norm_w, m_q_norm_w, m_kv_norm_w, m_w_uq, m_w_ukv, m_w_branch_ssm, m_w_branch_mla, m_w_out, m_norm_mlp_w, m_w_mlp_up, m_w_mlp_down, m_final_norm_w, v_meta_tokens, v_norm_mix_w, v_w_in, v_conv_w, v_conv_b, v_dt_bias, v_a_log, v_d_skip, v_ssm_norm_w, v_q_norm_w, v_kv_norm_w, v_w_uq, v_w_ukv, v_w_branch_ssm, v_w_branch_mla, v_w_out, v_norm_mlp_w, v_w_mlp_up, v_w_mlp_down, v_final_norm_w):
    cfg = CFG
    names = ["meta_tokens", "norm_mix_w", "w_in", "conv_w", "conv_b", "dt_bias", "a_log", "d_skip", "ssm_norm_w",
             "q_norm_w", "kv_norm_w", "w_uq", "w_ukv", "w_branch_ssm", "w_branch_mla", "w_out", "norm_mlp_w",
             "w_mlp_up", "w_mlp_down", "final_norm_w"]
    wts = dict(zip(names, [meta_tokens, norm_mix_w, w_in, conv_w, conv_b, dt_bias, a_log, d_skip, ssm_norm_w,
                           q_norm_w, kv_norm_w, w_uq, w_ukv, w_branch_ssm, w_branch_mla, w_out, norm_mlp_w,
                           w_mlp_up, w_mlp_down, final_norm_w]))
    ms = dict(zip(names, [m_meta_tokens, m_norm_mix_w, m_w_in, m_conv_w, m_conv_b, m_dt_bias, m_a_log, m_d_skip,
                          m_ssm_norm_w, m_q_norm_w, m_kv_norm_w, m_w_uq, m_w_ukv, m_w_branch_ssm, m_w_branch_mla,
                          m_w_out, m_norm_mlp_w, m_w_mlp_up, m_w_mlp_down, m_final_norm_w]))
    vs = dict(zip(names, [v_meta_tokens, v_norm_mix_w, v_w_in, v_conv_w, v_conv_b, v_dt_bias, v_a_log, v_d_skip,
                          v_ssm_norm_w, v_q_norm_w, v_kv_norm_w, v_w_uq, v_w_ukv, v_w_branch_ssm, v_w_branch_mla,
                          v_w_out, v_norm_mlp_w, v_w_mlp_up, v_w_mlp_down, v_final_norm_w]))
    cx, cy, cc = _coords()
    chip = 2 * cx + cy

    half1 = jnp.reshape(cc, (1,)).astype(jnp.int32)
    where2 = jnp.stack([chip, cc]).astype(jnp.int32)
    wb = {k: wts[k].astype(BF16) for k in BIG}
    zero_tok = jnp.zeros((8, LANE), F32)

    def halves(a):
        return a.reshape((2, a.shape[0] // 2) + a.shape[1:])

    def gather_start(li, keys, tag, after):
        srcs = [halves(wb[k][li]) for k in keys]
        lands = [lax.empty((4,) + s.shape, BF16) for s in srcs]
        return ici_start("gather", srcs, lands, after, name=f"gather{li}{tag}_start")

    def gather_finish(li, keys, tag, started, after):
        srcs, lands = ici_wait("gather", started, after, name=f"gather{li}{tag}_wait")
        lands = pair_share(lands, srcs, name=f"gather{li}{tag}_share")
        full = {k: _unshard_layer(k, land.reshape((4, 2 * land.shape[2], land.shape[3])))
                for k, land in zip(keys, lands)}
        return prep_layer(cfg, full)

    def gather_mid(li, keys, tag, started, after):
        srcs, lands = ici_wait("gather", started, after, name=f"gather{li}{tag}_wait")
        return ici_start("share", srcs, lands, zero_tok, name=f"gather{li}{tag}_share_start")

    def gather_end(li, keys, tag, shared, after):
        _, lands = ici_wait("share", shared, after, name=f"gather{li}{tag}_share_wait")
        full = {k: _unshard_layer(k, land.reshape((4, 2 * land.shape[2], land.shape[3])))
                for k, land in zip(keys, lands)}
        return prep_layer(cfg, full)

    def exchange_start(li, keys, tag, gw, after):
        ug = unprep_grads(cfg, gw)
        g4 = []
        for k in keys:
            s = _to_shards(k, ug[k])
            g4.append(s.reshape(4, 2, s.shape[1] // 2, s.shape[2]))
        lands = [lax.empty((4,) + a.shape[2:], a.dtype) for a in g4]
        return ici_start("exchange", g4, lands, after, name=f"grad{li}{tag}_exchange_start")

    def reduce_start(li, keys, tag, exchanged, after):
        g4, theirs = ici_wait("exchange", exchanged, after, name=f"grad{li}{tag}_exchange_wait")
        parts = [pair_add(a, b, half1, name=f"grad{li}_pair_add_{k}") for k, a, b in zip(keys, g4, theirs)]
        lands = [lax.empty(q.shape, q.dtype) for q in parts]
        return ici_start("scatter", parts, lands, zero_tok, name=f"grad{li}{tag}_scatter_start")

    def reduce_finish(li, keys, tag, started, after):
        parts, lands = ici_wait("scatter", started, after, name=f"grad{li}{tag}_scatter_wait")
        sums = [chip_sum(rc, pt, where2, name=f"grad{li}_chip_sum_{k}") for k, rc, pt in zip(keys, lands, parts)]
        sums = pair_fill(sums, name=f"grad{li}{tag}_pair_fill")
        return {k: s.reshape(2 * s.shape[1], s.shape[2]) for k, s in zip(keys, sums)}

    gathered = gather_chips([meta_tokens, conv_w], name="gather_small")
    p = dict(wts)
    p["meta_tokens"] = jnp.transpose(gathered[0], (1, 0, 2)).reshape(cfg.n_meta, cfg.d)
    p["conv_w"] = jnp.transpose(gathered[1], (1, 2, 0, 3)).reshape(2, cfg.convk, cfg.conv_dim)

    st0a = gather_start(0, ["w_in"], "a", gathered[0])
    st0b = gather_start(0, REST, "b", st0a[4])
    st1 = gather_start(1, BIG, "", st0b[4])
    pw0 = gather_finish(0, ["w_in"], "a", st0a, st1[4])

    bsz, d = cfg.bsz, cfg.d
    lead = jnp.zeros((bsz, cfg.pad, d), F32)
    meta = jnp.broadcast_to(p["meta_tokens"][None], (bsz, cfg.n_meta, d))
    h0 = jnp.concatenate([lead, meta, x], axis=1).reshape(cfg.t, d)
    tabs = rope_tables(cfg)
    sm0 = small_params(cfg, p, 0)
    st = {}

    def step(key, fn):
        def run(arg):
            st[key] = fn(arg)
            return st[key][4]
        return run

    h1, sv0, pw0 = layer_fwd(cfg, h0, pw0, sm0, tabs, 0, hooks={
        "after_conv": step("share0b", lambda after: gather_mid(0, REST, "b", st0b, after)),
        "weights": lambda after: gather_end(0, REST, "b", st["share0b"], after),
        "after_attn": step("share1", lambda after: gather_mid(1, BIG, "", st1, after))})
    pw1 = gather_end(1, BIG, "", st["share1"], h1)
    sm1 = small_params(cfg, p, 1)
    h2, sv1, _ = layer_fwd(cfg, h1, pw1, sm1, tabs, 1)
    loss, dh, dfw = loss_head(cfg, h2, loss_target.reshape(bsz * cfg.seq, d), final_norm_w, name="loss_head")
    loss = lax.psum(loss, ("x", "y", "c"))

    dh, gw1, gs1 = layer_bwd(cfg, dh, pw1, sm1, tabs, sv1, 1)
    ex1 = exchange_start(1, BIG, "", gw1, zero_tok)
    sm0b = dict(sm0)
    sm0b["norm_mlp_w"] = sm0["norm_mlp_w"] + ex1[4][0, 0]
    dh, gw0, gs0 = layer_bwd(cfg, dh, pw0, sm0b, tabs, sv0, 0, hooks={
        "after_attn": step("red1", lambda after: reduce_start(1, BIG, "", ex1, after)),
        "early": step("ex0e", lambda gw: exchange_start(0, REST, "e", gw, zero_tok)),
        "after_ssd": step("red0e", lambda after: reduce_start(0, REST, "e", st["ex0e"], after))})
    dh3 = dh.reshape(bsz, cfg.lp, d)
    grad_x = dh3[:, cfg.chunk:, :]
    gmeta = jnp.sum(dh3[:, cfg.pad:cfg.chunk, :], axis=0)
    big1 = reduce_finish(1, BIG, "", st["red1"], dh)
    ex0l = exchange_start(0, ["w_in"], "l", gw0, big1[BIG[-1]])

    small_names = SMALL_REPL + ["conv_w"]
    parts = [jnp.stack([gs0[k], gs1[k]]) for k in small_names] + [dfw, gmeta]
    shapes = [a.shape for a in parts]
    vec, _ = _pack_small(parts)
    red_vec = allreduce_small(vec, ex0l[4], name="allreduce_small")
    red = _unpack_small(red_vec, shapes)
    sg = dict(zip(small_names + ["final_norm_w", "meta_tokens"], red))
    sg["conv_w"] = lax.dynamic_slice_in_dim(sg["conv_w"], chip * (cfg.conv_dim // 4), cfg.conv_dim // 4, axis=2)
    sg["meta_tokens"] = lax.dynamic_slice_in_dim(sg["meta_tokens"], chip * (cfg.d // 4), cfg.d // 4, axis=1)

    red0 = reduce_start(0, ["w_in"], "l", ex0l, red_vec)
    grads, deltas, new_m, new_v = {}, {}, {}, {}
    dep = red0[4]
    for k in names:
        if k in BIG:
            continue
        w2, g2, m2, v2 = _as2d(wts[k]), _as2d(sg[k]), _as2d(ms[k]), _as2d(vs[k])
        dl, mn, vn = adamw_small(w2, g2, m2, v2, dep, name=f"adamw_{k}")
        grads[k] = sg[k].reshape(wts[k].shape)
        deltas[k], new_m[k], new_v[k] = (t.reshape(wts[k].shape) for t in (dl, mn, vn))

    def view(k, a):
        return jnp.swapaxes(a, 1, 2) if k == "w_in" else a

    def gview(k, g):
        return g.T if k == "w_in" else g

    wv, mv, vv = ({k: view(k, t[k]) for k in BIG} for t in (wts, ms, vs))
    outs = {}
    for k in BIG:
        outs[k] = adamw_layer(wv[k], mv[k], vv[k], gview(k, big1[k]), 1, None, dep, name=f"adamw1_{k}")
        dep = outs[k][1]
    big0 = reduce_finish(0, REST, "e", st["red0e"], dep)
    for k in REST:
        outs[k] = adamw_layer(wv[k], mv[k], vv[k], big0[k], 0, outs[k], dep, name=f"adamw0_{k}")
        dep = outs[k][1]
    big0.update(reduce_finish(0, ["w_in"], "l", red0, dep))
    outs["w_in"] = adamw_layer(wv["w_in"], mv["w_in"], vv["w_in"], gview("w_in", big0["w_in"]), 0, outs["w_in"], dep,
                               name="adamw0_w_in")
    for k in BIG:
        grads[k], deltas[k], new_m[k], new_v[k] = (view(k, t) for t in outs[k])
    return (loss, grad_x, *[grads[k] for k in names], *[deltas[k] for k in names],
            *[new_m[k] for k in names], *[new_v[k] for k in names])


def adamw_small(w, g, m, v, dep, *, name):
    def body(w_ref, g_ref, m_ref, v_ref, dep_ref, d_ref, mo_ref, vo_ref):
        d_ref[...], mo_ref[...], vo_ref[...] = _adam_update(w_ref[...], g_ref[...], m_ref[...], v_ref[...])

    vm = pl.BlockSpec(memory_space=pltpu.VMEM)
    return pl.pallas_call(body, name=name, in_specs=[vm] * 4 + [pl.BlockSpec(memory_space=pl.ANY)], out_specs=[vm] * 3,
                          out_shape=[_sds(w.shape, F32)] * 3, compiler_params=_cp())(w, g, m, v, dep)
```

```python
import functools
import math
from typing import NamedTuple

import numpy as np
import jax
import jax.numpy as jnp
from jax import lax
from jax.experimental import pallas as pl
from jax.experimental.pallas import tpu as pltpu

F32 = jnp.float32
BF16 = jnp.bfloat16
HI = lax.Precision.HIGHEST
EPS = 1e-6
ROPE_THETA = 10000.0
LANE = 128
VMEM_LIMIT = 56 * 1024 * 1024
MASK_VALUE = -1e30
ADAM_LR, ADAM_B1, ADAM_B2, ADAM_EPS, ADAM_WD, ADAM_STEP = 0.001, 0.9, 0.999, 1e-08, 0.01, 10
MESH = pl.DeviceIdType.MESH


class Cfg(NamedTuple):
    d: int = 1024
    seq: int = 2048
    bsz: int = 2
    n_meta: int = 16
    inner: int = 2048
    hd: int = 64
    groups: int = 4
    state: int = 128
    convk: int = 4
    chunk: int = 128
    mh: int = 8
    ql: int = 512
    kvl: int = 256
    nope: int = 128
    rope: int = 64
    vd: int = 128
    ff: int = 4096

    @property
    def heads(self): return self.inner // self.hd
    @property
    def gw(self): return self.inner // self.groups
    @property
    def conv_dim(self): return self.inner + 2 * self.groups * self.state
    @property
    def pad(self): return self.chunk - self.n_meta
    @property
    def lp(self): return self.chunk + self.seq
    @property
    def t(self): return self.bsz * self.lp
    @property
    def nchunks(self): return self.lp // self.chunk
    @property
    def sw(self): return self.ql + self.kvl + 2 * LANE
    @property
    def kt(self): return (self.ql + self.kvl) // LANE
    @property
    def dtt(self): return self.kt + 1
    @property
    def qw(self): return self.mh * 2 * LANE
    @property
    def in_splits(self):
        return [self.inner, self.conv_dim, self.heads, self.ql, self.kvl, self.rope, self.d, self.d]


CFG = Cfg()


def _pick(dim, pref, mult):
    best = None
    for t in range(mult, min(dim, pref) + 1, mult):
        if dim % t == 0:
            best = t
    return best if best is not None else dim


def _cp(**kw):
    return pltpu.CompilerParams(vmem_limit_bytes=VMEM_LIMIT, **kw)


def _sds(shape, dtype):
    return jax.ShapeDtypeStruct(tuple(shape), dtype)


def _silu(x):
    return x * jax.nn.sigmoid(x)


def _dsilu(x):
    s = jax.nn.sigmoid(x)
    return s * (1.0 + x * (1.0 - s))


def _ep_plain(r):
    return (r,)


def _ep_add(r, res):
    return (r + res.astype(F32),)


def _ep_relu2(r):
    rp = jnp.maximum(r, 0.0)
    return r, rp * rp


def _ep_relu2_grad(r, a):
    return (r * (2.0 * jnp.maximum(a.astype(F32), 0.0)),)


MM_VMEM_BUDGET = 44 * 1024 * 1024


def _mm_tiles(m, n, k, a_bytes, b_bytes, io_bytes, ta):
    m_mult, m_cap = (LANE, 1024) if ta else (16, 1088)
    tms = [t for t in range(m_cap, 0, -m_mult) if m % t == 0] or [m]
    tns = [t for t in (1024, 512, 256, 128) if n % t == 0] or [n]
    best = None
    for tm in tms:
        for tn in tns:
            need = 2 * (tm * k * a_bytes + k * tn * b_bytes + tm * tn * io_bytes)
            if need <= MM_VMEM_BUDGET and (best is None or tm * tn > best[0] * best[1]):
                best = (tm, tn)
    if best is None:
        return (_pick(m, 512, m_mult), _pick(n, 512, LANE), _pick(k, 1088 if ta else 1024, 16 if ta else LANE))
    return best[0], best[1], k


def matmul(a, b, *, ta=False, tb=False, out_dtype=F32, add=None, name, tm=None, tn=None, tk=None,
           epilogue=None, extras=(), out_dtypes=None):
    if add is not None:
        epilogue, extras = _ep_add, (add,)
    if epilogue is None:
        epilogue = _ep_plain
    out_dtypes = tuple(out_dtypes) if out_dtypes is not None else (out_dtype,)
    n_ex, n_out = len(extras), len(out_dtypes)
    if ta:
        k_dim, m_dim = a.shape
    else:
        m_dim, k_dim = a.shape
    if tb:
        n_dim, k2 = b.shape
    else:
        k2, n_dim = b.shape
    assert k_dim == k2, (a.shape, b.shape, ta, tb)
    if tm is None and tn is None and tk is None:
        io_bytes = sum(jnp.dtype(e.dtype).itemsize for e in extras) + sum(jnp.dtype(d).itemsize for d in out_dtypes)
        tm, tn, tk = _mm_tiles(m_dim, n_dim, k_dim, jnp.dtype(a.dtype).itemsize, jnp.dtype(b.dtype).itemsize,
                               io_bytes, ta)
    elif ta:
        tm = tm or _pick(m_dim, 1024, LANE)
        tk = tk or _pick(k_dim, 1088, 16)
        tn = tn or _pick(n_dim, 1024, LANE)
    else:
        tm = tm or _pick(m_dim, 1088, 16)
        tk = tk or _pick(k_dim, 1024 if a.dtype == F32 else 2048, LANE)
        tn = tn or _pick(n_dim, 1024, LANE)
    nm, nn, nk = m_dim // tm, n_dim // tn, k_dim // tk
    dn = (((0 if ta else 1,), (1 if tb else 0,)), ((), ()))

    def body(*refs):
        a_ref, b_ref = refs[:2]
        ex_refs = refs[2:2 + n_ex]
        o_refs = refs[2 + n_ex:2 + n_ex + n_out]
        scr = refs[2 + n_ex + n_out:]
        p = lax.dot_general(a_ref[...].astype(BF16), b_ref[...].astype(BF16), dn, preferred_element_type=F32)

        def finish(r):
            outs = epilogue(r, *[e[...] for e in ex_refs])
            for o_ref, val, dt in zip(o_refs, outs, out_dtypes):
                o_ref[...] = val.astype(dt)

        if nk == 1:
            finish(p)
        else:
            acc = scr[0]
            k = pl.program_id(2)

            @pl.when(k == 0)
            def _():
                acc[...] = p

            @pl.when(k > 0)
            def _():
                acc[...] += p

            @pl.when(k == nk - 1)
            def _():
                finish(acc[...])

    a_spec = pl.BlockSpec((tk, tm), lambda i, j, k: (k, i)) if ta else pl.BlockSpec((tm, tk), lambda i, j, k: (i, k))
    b_spec = pl.BlockSpec((tn, tk), lambda i, j, k: (j, k)) if tb else pl.BlockSpec((tk, tn), lambda i, j, k: (k, j))
    o_spec = pl.BlockSpec((tm, tn), lambda i, j, k: (i, j))
    outs = pl.pallas_call(
        body, name=name, grid=(nm, nn, nk), in_specs=[a_spec, b_spec] + [o_spec] * n_ex, out_specs=[o_spec] * n_out,
        out_shape=[_sds((m_dim, n_dim), dt) for dt in out_dtypes],
        scratch_shapes=[pltpu.VMEM((tm, tn), F32)] if nk > 1 else [],
        compiler_params=_cp(dimension_semantics=("parallel", "parallel", "arbitrary")),
    )(a, b, *extras)
    return outs[0] if n_out == 1 else tuple(outs)


def rmsnorm_fwd(x, w, *, cw=None, ci=0, name):
    t = x.shape[0]
    cw = cw or x.shape[1]
    tr = _pick(t, 544, 16)

    def body(x_ref, w_ref, o_ref):
        xv = x_ref[...].astype(F32)
        r = lax.rsqrt(jnp.mean(xv * xv, axis=-1, keepdims=True) + EPS)
        o_ref[...] = (xv * r * w_ref[...]).astype(BF16)

    return pl.pallas_call(
        body, name=name, grid=(t // tr,),
        in_specs=[pl.BlockSpec((tr, cw), lambda i: (i, ci)), pl.BlockSpec((1, cw), lambda i: (0, 0))],
        out_specs=pl.BlockSpec((tr, cw), lambda i: (i, 0)),
        out_shape=_sds((t, cw), BF16), compiler_params=_cp(),
    )(x, w.reshape(1, cw))


def rmsnorm_bwd(dy, x, w, *, cw=None, ci=0, res=None, out_dtype=F32, name):
    t = x.shape[0]
    cw = cw or x.shape[1]
    tr = _pick(t, 544, 16)
    has_res = res is not None

    def body(*refs):
        if has_res:
            dy_ref, x_ref, w_ref, res_ref, dx_ref, dw_ref = refs
        else:
            dy_ref, x_ref, w_ref, dx_ref, dw_ref = refs
        xv = x_ref[...].astype(F32)
        dyv = dy_ref[...].astype(F32)
        r = lax.rsqrt(jnp.mean(xv * xv, axis=-1, keepdims=True) + EPS)
        xh = xv * r
        g = dyv * w_ref[...]
        dx = r * (g - xh * jnp.mean(g * xh, axis=-1, keepdims=True))
        if has_res:
            dx = dx + res_ref[...]
        dx_ref[...] = dx.astype(out_dtype)

        @pl.when(pl.program_id(0) == 0)
        def _():
            dw_ref[...] = jnp.zeros_like(dw_ref)

        dw_ref[...] += jnp.sum(dyv * xh, axis=0, keepdims=True)

    row = pl.BlockSpec((tr, cw), lambda i: (i, 0))
    in_specs = [row, pl.BlockSpec((tr, cw), lambda i: (i, ci)), pl.BlockSpec((1, cw), lambda i: (0, 0))]
    args = [dy, x, w.reshape(1, cw)]
    if has_res:
        in_specs.append(row)
        args.append(res)
    dx, dw = pl.pallas_call(
        body, name=name, grid=(t // tr,), in_specs=in_specs,
        out_specs=[row, pl.BlockSpec((1, cw), lambda i: (0, 0))],
        out_shape=[_sds((t, cw), out_dtype), _sds((1, cw), F32)], compiler_params=_cp(),
    )(*args)
    return dx, dw[0]


def _shift_down(x, s):
    return x if s == 0 else pltpu.roll(x, s, 0)


def _shift_up(x, s):
    return x if s == 0 else pltpu.roll(x, x.shape[0] - s, 0)


def _conv_pre(x, w_ref, b_ref, kk):
    pre = b_ref[...] + jnp.zeros_like(x)
    for k in range(kk):
        pre = pre + w_ref[k:k + 1, :] * _shift_down(x, kk - 1 - k)
    return pre


def conv_fwd(cfg, xbc, w, b, *, name):
    lp, cd, kk = cfg.lp, cfg.conv_dim, cfg.convk
    assert cfg.pad >= kk - 1
    cb = _pick(cd, 512, LANE)

    def body(x_ref, w_ref, b_ref, o_ref):
        o_ref[...] = _silu(_conv_pre(x_ref[...], w_ref, b_ref, kk))

    blk = pl.BlockSpec((lp, cb), lambda j, bb: (bb, j))
    return pl.pallas_call(
        body, name=name, grid=(cd // cb, cfg.bsz),
        in_specs=[blk, pl.BlockSpec((kk, cb), lambda j, bb: (0, j)), pl.BlockSpec((1, cb), lambda j, bb: (0, j))],
        out_specs=blk, out_shape=_sds((cfg.t, cd), F32), compiler_params=_cp(),
    )(xbc, w, b.reshape(1, cd))


def conv_bwd(cfg, xbc, w, b, dxc, *, name):
    lp, cd, kk = cfg.lp, cfg.conv_dim, cfg.convk
    cb = _pick(cd, 512, LANE)

    def body(x_ref, w_ref, b_ref, d_ref, dx_ref, dw_ref, db_ref):
        x = x_ref[...]
        pre = _conv_pre(x, w_ref, b_ref, kk)
        dpre = d_ref[...] * _dsilu(pre)
        dx = jnp.zeros_like(x)
        dws = []
        for k in range(kk):
            s = kk - 1 - k
            dx = dx + w_ref[k:k + 1, :] * _shift_up(dpre, s)
            dws.append(jnp.sum(dpre * _shift_down(x, s), axis=0, keepdims=True))
        dx_ref[...] = dx.astype(BF16)

        @pl.when(pl.program_id(1) == 0)
        def _():
            dw_ref[...] = jnp.zeros_like(dw_ref)
            db_ref[...] = jnp.zeros_like(db_ref)

        for k in range(kk):
            dw_ref[k:k + 1, :] += dws[k]
        db_ref[...] += jnp.sum(dpre, axis=0, keepdims=True)

    blk = pl.BlockSpec((lp, cb), lambda j, bb: (bb, j))
    wsp = pl.BlockSpec((kk, cb), lambda j, bb: (0, j))
    bsp = pl.BlockSpec((1, cb), lambda j, bb: (0, j))
    dx, dw, db = pl.pallas_call(
        body, name=name, grid=(cd // cb, cfg.bsz),
        in_specs=[blk, wsp, bsp, blk], out_specs=[blk, wsp, bsp],
        out_shape=[_sds((cfg.t, cd), BF16), _sds((kk, cd), F32), _sds((1, cd), F32)], compiler_params=_cp(),
    )(xbc, w, b.reshape(1, cd), dxc)
    return dx, dw, db[0]


def _softplus(x):
    return jnp.maximum(x, 0.0) + jnp.log(1.0 + jnp.exp(-jnp.abs(x)))


def _ssd_consts(cfg):
    q = cfg.chunk
    i0 = np.arange(q)[:, None]
    i1 = np.arange(q)[None, :]
    ltri = (i1 <= i0).astype(np.float32)
    rexp = np.zeros((LANE, cfg.inner), np.float32)
    for h in range(cfg.heads):
        rexp[h, h * cfg.hd:(h + 1) * cfg.hd] = 1.0
    return jnp.asarray(ltri), jnp.asarray(rexp)


def _sel_dot(x, m, *, passes=2, left=False, trans=False):
    mb = m.astype(BF16)
    acc, rem = None, x
    for _ in range(passes):
        piece = rem.astype(BF16)
        if not left:
            part = _nn(piece, mb)
        elif trans:
            part = _tn(mb, piece)
        else:
            part = _nn(mb, piece)
        acc = part if acc is None else acc + part
        rem = rem - piece.astype(F32)
    return acc


def _ssd_chunk_common(cfg, raw, bias, avec, c_idx, ltri, rexp):
    q = cfg.chunk
    rows = lax.broadcasted_iota(jnp.int32, (q, LANE), 0)
    live = jnp.logical_or(c_idx > 0, rows >= cfg.pad)
    pre = raw + bias
    dt = jnp.where(live, _softplus(pre), 0.0)
    adt = dt * avec
    cs = _sel_dot(adt, ltri, passes=3, left=True)
    cs_t = cs.T
    cs_last = cs[q - 1:q, :]
    e_in = jnp.exp(cs)
    w0 = jnp.exp(cs_last - cs)
    decay = jnp.exp(cs_last)
    return dict(live=live, pre=pre, dt=dt, adt=adt, cs=cs, cs_t=cs_t, e_in=e_in, w0=w0, decay=decay,
                DT=_sel_dot(dt, rexp), E=_sel_dot(e_in, rexp), W0=_sel_dot(w0, rexp),
                DEC=_sel_dot(jnp.broadcast_to(decay, (8, LANE)), rexp)[0:1, :])


def _tri_masks(q):
    r = lax.broadcasted_iota(jnp.int32, (q, q), 0)
    c = lax.broadcasted_iota(jnp.int32, (q, q), 1)
    return c <= r, r <= c


def _head_l(cq, h, tri, tri_t):
    col = cq["cs"][:, h:h + 1]
    row = cq["cs_t"][h:h + 1, :]
    lmat = jnp.where(tri, jnp.exp(jnp.minimum(col - row, 0.0)), 0.0)
    lmat_t = jnp.where(tri_t, jnp.exp(jnp.minimum(row - col, 0.0)), 0.0)
    return lmat, lmat_t


def _nt(a, b):
    return lax.dot_general(a, b, (((1,), (1,)), ((), ())), preferred_element_type=F32)


def _tn(a, b):
    return lax.dot_general(a, b, (((0,), (0,)), ((), ())), preferred_element_type=F32)


def _nn(a, b):
    return jnp.dot(a, b, preferred_element_type=F32)


def ssd_fwd(cfg, xc, small, dt_bias, avec, dexp, *, name):
    q, inner, st, gw, g_n = cfg.chunk, cfg.inner, cfg.state, cfg.gw, cfg.groups
    nc = cfg.nchunks
    ltri, rexp = _ssd_consts(cfg)
    hpt = LANE // cfg.hd
    tiles_per_group = gw // LANE

    bsz, lp = cfg.bsz, cfg.lp
    bcw = g_n * st

    def body(x_ref, b_ref, c_ref, dt_ref, bias_ref, a_ref, d_ref, ltri_ref, rexp_ref, y_ref, sin_ref, s_scr):
        c_idx = pl.program_id(0)

        @pl.when(c_idx == 0)
        def _():
            s_scr[...] = jnp.zeros_like(s_scr)

        ltri_v = ltri_ref[...]
        tri, tri_t = _tri_masks(q)
        lane = lax.broadcasted_iota(jnp.int32, (q, LANE), 1)
        for bi in range(bsz):
            cq = _ssd_chunk_common(cfg, dt_ref[bi], bias_ref[...], a_ref[...], c_idx, ltri_v, rexp_ref[...])
            xs = x_ref[bi]
            xdt = (xs * cq["DT"]).astype(BF16)
            xw = (xs * cq["DT"] * cq["W0"]).astype(BF16)
            s_in = s_scr[bi]
            sin_ref[bi, 0] = s_in
            for g in range(g_n):
                bg = b_ref[bi, :, g * st:(g + 1) * st].astype(BF16)
                cg = c_ref[bi, :, g * st:(g + 1) * st].astype(BF16)
                gmat = _nt(cg, bg)
                gs = slice(g * gw, (g + 1) * gw)
                y0 = _nn(cg, s_in[:, gs].astype(BF16))
                for tt in range(tiles_per_group):
                    tile = g * tiles_per_group + tt
                    ts = slice(tile * LANE, (tile + 1) * LANE)
                    xt = xdt[:, ts]
                    ms, xh = [], []
                    for hh in range(hpt):
                        lmat, _ = _head_l(cq, tile * hpt + hh, tri, tri_t)
                        ms.append((gmat * lmat).astype(BF16))
                        inhead = jnp.logical_and(lane >= hh * cfg.hd, lane < (hh + 1) * cfg.hd)
                        xh.append(jnp.where(inhead, xt, jnp.zeros_like(xt)))
                    yd = _nn(jnp.concatenate(ms, axis=1), jnp.concatenate(xh, axis=0))
                    y_ref[bi, :, ts] = (yd + y0[:, tt * LANE:(tt + 1) * LANE] * cq["E"][:, ts]
                                        + xs[:, ts] * d_ref[:, ts]).astype(BF16)
                s_scr[bi, :, gs] = s_in[:, gs] * cq["DEC"][:, gs] + _tn(bg, xw[:, gs])

    def rowblk(width, col):
        return pl.BlockSpec((bsz, q, width), lambda c: (0, c, col))

    def const(shape):
        return pl.BlockSpec(shape, lambda c: (0, 0))

    xc3 = xc.reshape(bsz, lp, cfg.conv_dim)
    y, sin = pl.pallas_call(
        body, name=name, grid=(nc,),
        in_specs=[rowblk(inner, 0), rowblk(bcw, inner // bcw), rowblk(bcw, inner // bcw + 1),
                  rowblk(LANE, cfg.dtt), const((1, LANE)), const((1, LANE)), const((1, inner)),
                  const((q, q)), const((LANE, inner))],
        out_specs=[rowblk(inner, 0), pl.BlockSpec((bsz, 1, st, inner), lambda c: (0, c, 0, 0))],
        out_shape=[_sds((bsz, lp, inner), BF16), _sds((bsz, nc, st, inner), F32)],
        scratch_shapes=[pltpu.VMEM((bsz, st, inner), F32)], compiler_params=_cp(),
    )(xc3, xc3, xc3, small.reshape(bsz, lp, cfg.sw), dt_bias, avec, dexp, ltri, rexp)
    return y.reshape(cfg.t, inner), sin.reshape(bsz * nc, st, inner)


def ssd_bwd(cfg, xc, small, dt_bias, avec, dexp, sin, dy, *, name):
    q, inner, st, gw, g_n = cfg.chunk, cfg.inner, cfg.state, cfg.gw, cfg.groups
    nc = cfg.nchunks
    ltri, rexp = _ssd_consts(cfg)
    rexp_t = rexp.T
    hpt = LANE // cfg.hd
    tiles_per_group = gw // LANE
    bcw = g_n * st

    def body(x_ref, b_ref, c_ref, dt_ref, bias_ref, a_ref, d_ref, ltri_ref, rexp_ref, rexpt_ref, sin_ref, dy_ref,
             dx_ref, ddt_ref, dd_ref, da_ref, dbias_ref, ds_scr):
        step = pl.program_id(1)
        c_idx = nc - 1 - step

        @pl.when(step == 0)
        def _():
            ds_scr[...] = jnp.zeros_like(ds_scr)

        @pl.when(jnp.logical_and(step == 0, pl.program_id(0) == 0))
        def _():
            dd_ref[...] = jnp.zeros_like(dd_ref)
            da_ref[...] = jnp.zeros_like(da_ref)
            dbias_ref[...] = jnp.zeros_like(dbias_ref)

        ltri_v = ltri_ref[...]
        tri, tri_t = _tri_masks(q)
        red = _sel_dot
        rexpt = rexpt_ref[...]
        cq = _ssd_chunk_common(cfg, dt_ref[...], bias_ref[...], a_ref[...], c_idx, ltri_v, rexp_ref[...])
        xs = x_ref[...]
        dyv = dy_ref[...]
        s_in = sin_ref[0]
        d_s = ds_scr[...]
        xdt_f = xs * cq["DT"]
        xdt = xdt_f.astype(BF16)
        xw_f = xdt_f * cq["W0"]
        xw = xw_f.astype(BF16)
        lane = lax.broadcasted_iota(jnp.int32, (q, LANE), 1)
        sub = lax.broadcasted_iota(jnp.int32, (LANE, q), 0)

        dd_ref[...] += jnp.sum(dyv * xs, axis=0, keepdims=True)
        dy0 = dyv * cq["E"]
        dcs = jnp.zeros((q, LANE), F32)
        dcs_t = jnp.zeros((LANE, q), F32)
        for g in range(g_n):
            bg_f = b_ref[:, g * st:(g + 1) * st]
            cg_f = c_ref[:, g * st:(g + 1) * st]
            bg = bg_f.astype(BF16)
            cg = cg_f.astype(BF16)
            gs = slice(g * gw, (g + 1) * gw)
            gmat = _nt(cg, bg)
            gmat_t = _nt(bg, cg)
            sing = s_in[:, gs].astype(BF16)
            dsg = d_s[:, gs].astype(BF16)
            y0 = _nn(cg, sing)
            dxw = _nn(bg, dsg)
            d_bg = _nt(xw[:, gs], dsg)
            d_cg = _nt(dy0[:, gs].astype(BF16), sing)
            ds_in_g = _tn(cg, dy0[:, gs].astype(BF16))
            dg = jnp.zeros((q, q), F32)
            dxdt_g = []
            for tt in range(tiles_per_group):
                tile = g * tiles_per_group + tt
                ts = slice(tile * LANE, (tile + 1) * LANE)
                xt = xdt[:, ts]
                dyt = dyv[:, ts]
                dyhs, lmats, mts = [], [], []
                for hh in range(hpt):
                    lmat, lmat_t = _head_l(cq, tile * hpt + hh, tri, tri_t)
                    inhead = jnp.logical_and(lane >= hh * cfg.hd, lane < (hh + 1) * cfg.hd)
                    dyhs.append(jnp.where(inhead, dyt, 0.0).astype(BF16))
                    lmats.append(lmat)
                    mts.append((gmat_t * lmat_t).astype(BF16))
                dy_stack = jnp.concatenate(dyhs, axis=0)
                dm_all = _nt(dy_stack, xt)
                for hh in range(hpt):
                    h = tile * hpt + hh
                    dm = dm_all[hh * q:(hh + 1) * q, :]
                    dg = dg + dm * lmats[hh]
                    qm = dm * gmat * lmats[hh]
                    rs = jnp.sum(qm, axis=1, keepdims=True)
                    csum = jnp.sum(qm, axis=0, keepdims=True)
                    dcs = dcs + jnp.where(lane == h, rs, 0.0)
                    dcs_t = dcs_t + jnp.where(sub == h, csum, 0.0)
                dxdt_g.append(_nn(jnp.concatenate(mts, axis=1), dy_stack))
            dxdt_diag = jnp.concatenate(dxdt_g, axis=1) if len(dxdt_g) > 1 else dxdt_g[0]
            dgb = dg.astype(BF16)
            d_cg = d_cg + _nn(dgb, bg)
            d_bg = d_bg + _tn(dgb, cg)
            dx_ref[:, inner + g * st:inner + (g + 1) * st] = d_bg
            dx_ref[:, inner + bcw + g * st:inner + bcw + (g + 1) * st] = d_cg
            dxdt = dxdt_diag + dxw * cq["W0"][:, gs]
            dx_ref[:, gs] = dyv[:, gs] * d_ref[:, gs] + dxdt * cq["DT"][:, gs]
            rt = rexpt[gs, :]
            dcs = dcs + red(dyv[:, gs] * y0 * cq["E"][:, gs], rt)
            r_w = red(dxw * xw_f[:, gs], rt)
            dcs = dcs - r_w
            dcs_last_g = jnp.sum(r_w, axis=0, keepdims=True)
            ddec = red(jnp.broadcast_to(jnp.sum(d_s[:, gs] * s_in[:, gs], axis=0, keepdims=True), (8, gw)), rt)[0:1, :]
            dcs_last_g = dcs_last_g + ddec * cq["decay"]
            dcs = dcs + jnp.where(lax.broadcasted_iota(jnp.int32, (q, LANE), 0) == q - 1, dcs_last_g, 0.0)
            ddt_part = red(dxdt * xs[:, gs], rt)
            if g == 0:
                ddt = ddt_part
            else:
                ddt = ddt + ddt_part
            ds_scr[:, gs] = d_s[:, gs] * cq["DEC"][:, gs] + ds_in_g
        dcs = dcs - dcs_t.T
        dadt = _sel_dot(dcs, ltri_v, left=True, trans=True)
        ddt = ddt + dadt * a_ref[...]
        da_ref[...] += jnp.sum(dadt * cq["dt"], axis=0, keepdims=True)
        draw = jnp.where(cq["live"], ddt * jax.nn.sigmoid(cq["pre"]), 0.0)
        ddt_ref[...] = draw
        dbias_ref[...] += jnp.sum(draw, axis=0, keepdims=True)

    def rowblk(width, col):
        return pl.BlockSpec((q, width), lambda b, s: (b * nc + nc - 1 - s, col))

    def const(shape):
        return pl.BlockSpec(shape, lambda b, s: (0, 0))

    bcol = inner // bcw
    outs = pl.pallas_call(
        body, name=name, grid=(cfg.bsz, nc),
        in_specs=[rowblk(inner, 0), rowblk(bcw, bcol), rowblk(bcw, bcol + 1), rowblk(LANE, cfg.dtt),
                  const((1, LANE)), const((1, LANE)), const((1, inner)), const((q, q)), const((LANE, inner)),
                  const((inner, LANE)),
                  pl.BlockSpec((1, st, inner), lambda b, s: (b * nc + nc - 1 - s, 0, 0)), rowblk(inner, 0)],
        out_specs=[rowblk(cfg.conv_dim, 0), rowblk(LANE, 0),
                   const((1, inner)), const((1, LANE)), const((1, LANE))],
        out_shape=[_sds((cfg.t, cfg.conv_dim), F32),
                   _sds((cfg.t, LANE), F32), _sds((1, inner), F32), _sds((1, LANE), F32), _sds((1, LANE), F32)],
        scratch_shapes=[pltpu.VMEM((st, inner), F32)], compiler_params=_cp(),
    )(xc, xc, xc, small, dt_bias, avec, dexp, ltri, rexp, rexp_t, sin, dy)
    return outs


def tail_fwd(cfg, y, z, w, *, name):
    t, inner, gw = cfg.t, cfg.inner, cfg.gw
    tr = _pick(t, 272, 16)

    def body(y_ref, z_ref, w_ref, o_ref):
        for g in range(cfg.groups):
            gs = slice(g * gw, (g + 1) * gw)
            yg = y_ref[:, gs].astype(F32) * _silu(z_ref[:, gs].astype(F32))
            r = lax.rsqrt(jnp.mean(yg * yg, axis=-1, keepdims=True) + EPS)
            o_ref[:, gs] = (yg * r * w_ref[:, gs]).astype(BF16)

    row = pl.BlockSpec((tr, inner), lambda i: (i, 0))
    return pl.pallas_call(
        body, name=name, grid=(t // tr,), in_specs=[row, row, pl.BlockSpec((1, inner), lambda i: (0, 0))],
        out_specs=row, out_shape=_sds((t, inner), BF16), compiler_params=_cp(),
    )(y, z, w.reshape(1, inner))


def tail_bwd(cfg, do, y, z, w, *, name):
    t, inner, gw = cfg.t, cfg.inner, cfg.gw
    tr = _pick(t, 272, 16)

    def body(do_ref, y_ref, z_ref, w_ref, dy_ref, dz_ref, dw_ref):
        @pl.when(pl.program_id(0) == 0)
        def _():
            dw_ref[...] = jnp.zeros_like(dw_ref)

        for g in range(cfg.groups):
            gs = slice(g * gw, (g + 1) * gw)
            yv = y_ref[:, gs].astype(F32)
            zv = z_ref[:, gs].astype(F32)
            dov = do_ref[:, gs].astype(F32)
            sz = _silu(zv)
            yg = yv * sz
            r = lax.rsqrt(jnp.mean(yg * yg, axis=-1, keepdims=True) + EPS)
            xh = yg * r
            gg = dov * w_ref[:, gs]
            dyg = r * (gg - xh * jnp.mean(gg * xh, axis=-1, keepdims=True))
            dw_ref[:, gs] += jnp.sum(dov * xh, axis=0, keepdims=True)
            dy_ref[:, gs] = dyg * sz
            dz_ref[:, gs] = (dyg * yv * _dsilu(zv)).astype(BF16)

    row = pl.BlockSpec((tr, inner), lambda i: (i, 0))
    vec = pl.BlockSpec((1, inner), lambda i: (0, 0))
    dy, dz, dw = pl.pallas_call(
        body, name=name, grid=(t // tr,), in_specs=[row, row, row, vec], out_specs=[row, row, vec],
        out_shape=[_sds((t, inner), F32), _sds((t, inner), BF16), _sds((1, inner), F32)], compiler_params=_cp(),
    )(do, y, z, w.reshape(1, inner))
    return dy, dz, dw[0]


def rope_tables(cfg):
    half = cfg.rope // 2
    pos = np.maximum(np.arange(cfg.lp) - cfg.pad, 0).astype(np.float32)
    inv = ROPE_THETA ** (-jnp.arange(0, cfg.rope, 2, dtype=F32) / cfg.rope)
    ang = jnp.asarray(pos)[:, None] * inv[None, :]
    cos, sin = jnp.cos(ang), jnp.sin(ang)
    zero = jnp.zeros((cfg.lp, LANE - 2 * half), F32)
    zh = jnp.zeros((cfg.lp, half), F32)
    ctab = jnp.concatenate([cos, cos, zero], axis=1)
    s1 = jnp.concatenate([-sin, zh, zero], axis=1)
    s2 = jnp.concatenate([zh, sin, zero], axis=1)
    return ctab, s1, s2


def _rope(x, c, s1, s2, half):
    return x * c + pltpu.roll(x, LANE - half, 1) * s1 + pltpu.roll(x, half, 1) * s2


def _rope_t(dy, c, s1, s2, half):
    return dy * c + pltpu.roll(dy * s1, half, 1) + pltpu.roll(dy * s2, LANE - half, 1)


def _attn_scale(cfg):
    return (cfg.nope + cfg.rope) ** -0.5


def rope_fwd(cfg, qf, small, tabs, *, name):
    t, qw, lp = cfg.t, cfg.qw, cfg.lp
    tr = _pick(lp, 544, 16)
    nrb = lp // tr
    half = cfg.rope // 2
    scale = _attn_scale(cfg)

    def body(q_ref, k_ref, c_ref, s1_ref, s2_ref, qo_ref, ko_ref):
        c, s1, s2 = c_ref[...], s1_ref[...], s2_ref[...]
        for h in range(cfg.mh):
            a = h * 2 * LANE
            qo_ref[:, a:a + LANE] = (q_ref[:, a:a + LANE].astype(F32) * scale).astype(BF16)
            qo_ref[:, a + LANE:a + 2 * LANE] = (
                _rope(q_ref[:, a + LANE:a + 2 * LANE].astype(F32), c, s1, s2, half) * scale).astype(BF16)
        ko_ref[...] = _rope(k_ref[...], c, s1, s2, half).astype(BF16)

    tab = pl.BlockSpec((tr, LANE), lambda i: (i % nrb, 0))
    return pl.pallas_call(
        body, name=name, grid=(t // tr,),
        in_specs=[pl.BlockSpec((tr, qw), lambda i: (i, 0)), pl.BlockSpec((tr, LANE), lambda i: (i, cfg.kt)), tab, tab, tab],
        out_specs=[pl.BlockSpec((tr, qw), lambda i: (i, 0)), pl.BlockSpec((tr, LANE), lambda i: (i, 0))],
        out_shape=[_sds((t, qw), BF16), _sds((t, LANE), BF16)], compiler_params=_cp(),
    )(qf, small, *tabs)


def rope_bwd(cfg, dq, dkpe, tabs, *, name):
    t, qw, lp = cfg.t, cfg.qw, cfg.lp
    tr = _pick(lp, 544, 16)
    nrb = lp // tr
    half = cfg.rope // 2
    scale = _attn_scale(cfg)

    def body(dq_ref, dk_ref, c_ref, s1_ref, s2_ref, qo_ref, ko_ref):
        c, s1, s2 = c_ref[...], s1_ref[...], s2_ref[...]
        for h in range(cfg.mh):
            a = h * 2 * LANE
            qo_ref[:, a:a + LANE] = (dq_ref[:, a:a + LANE].astype(F32) * scale).astype(BF16)
            qo_ref[:, a + LANE:a + 2 * LANE] = _rope_t(
                dq_ref[:, a + LANE:a + 2 * LANE].astype(F32) * scale, c, s1, s2, half).astype(BF16)
        dk = dk_ref[0]
        for h in range(1, cfg.mh):
            dk = dk + dk_ref[h]
        ko_ref[...] = _rope_t(dk, c, s1, s2, half)

    tab = pl.BlockSpec((tr, LANE), lambda i: (i % nrb, 0))
    return pl.pallas_call(
        body, name=name, grid=(t // tr,),
        in_specs=[pl.BlockSpec((tr, qw), lambda i: (i, 0)), pl.BlockSpec((cfg.mh, tr, LANE), lambda i: (0, i, 0)),
                  tab, tab, tab],
        out_specs=[pl.BlockSpec((tr, qw), lambda i: (i, 0)), pl.BlockSpec((tr, LANE), lambda i: (i, 0))],
        out_shape=[_sds((t, qw), BF16), _sds((t, LANE), F32)], compiler_params=_cp(),
    )(dq, dkpe, *tabs)


def _q_blocks(cfg):
    bounds = [0, cfg.chunk] + list(range(cfg.chunk + 256, cfg.lp + 1, 256))
    assert bounds[-1] == cfg.lp, "SEQ must be a multiple of 256"
    return list(zip(bounds[:-1], bounds[1:]))


def _attn_mask(cfg, qs, qe):
    rows = qs + lax.broadcasted_iota(jnp.int32, (qe - qs, qe), 0)
    cols = lax.broadcasted_iota(jnp.int32, (qe - qs, qe), 1)
    return jnp.logical_and(cols <= rows, jnp.logical_or(cols >= cfg.pad, rows < cfg.pad))


def _max_q_block(cfg):
    return max(qe - qs for qs, qe in _q_blocks(cfg))


def _masked_scores(cfg, q, k2, qs, qe, s_scr):
    bq, n = qe - qs, qe
    s_scr[0:bq, 0:n] = _nt(q, k2)
    if qs == 0:
        s_scr[0:bq, 0:n] = jnp.where(_attn_mask(cfg, 0, qe), s_scr[0:bq, 0:n], MASK_VALUE)
    else:
        assert qs >= cfg.chunk and cfg.pad < LANE
        cols = lax.broadcasted_iota(jnp.int32, (bq, LANE), 1)
        s_scr[0:bq, 0:LANE] = jnp.where(cols >= cfg.pad, s_scr[0:bq, 0:LANE], MASK_VALUE)
        r = lax.broadcasted_iota(jnp.int32, (bq, bq), 0)
        c = lax.broadcasted_iota(jnp.int32, (bq, bq), 1)
        s_scr[0:bq, qs:qe] = jnp.where(c <= r, s_scr[0:bq, qs:qe], MASK_VALUE)
    return s_scr[0:bq, 0:n]


def attn_fwd(cfg, qr, kv, kpe, *, name):
    lp, t, mh = cfg.lp, cfg.t, cfg.mh
    blocks = _q_blocks(cfg)

    def body(q_ref, kv_ref, kp_ref, o_ref, l_ref, s_scr):
        for qs, qe in blocks:
            n = qe
            q = q_ref[qs:qe, :]
            k2 = jnp.concatenate([kv_ref[0:n, 0:LANE], kp_ref[0:n, :]], axis=1)
            s = _masked_scores(cfg, q, k2, qs, qe, s_scr)
            m = jnp.max(s, axis=-1, keepdims=True)
            p = jnp.exp(s - m)
            l = jnp.sum(p, axis=-1, keepdims=True)
            o_ref[qs:qe, :] = (_nn(p.astype(BF16), kv_ref[0:n, LANE:2 * LANE]) * (1.0 / l)).astype(BF16)
            l_ref[qs:qe, :] = jnp.broadcast_to(m + jnp.log(l), (qe - qs, LANE))

    hb = pl.BlockSpec((lp, 2 * LANE), lambda b, h: (b, h))
    ob = pl.BlockSpec((lp, LANE), lambda b, h: (b, h))
    return pl.pallas_call(
        body, name=name, grid=(cfg.bsz, mh),
        in_specs=[hb, hb, pl.BlockSpec((lp, LANE), lambda b, h: (b, 0))], out_specs=[ob, ob],
        out_shape=[_sds((t, mh * LANE), BF16), _sds((t, mh * LANE), F32)],
        scratch_shapes=[pltpu.VMEM((_max_q_block(cfg), lp), F32)], compiler_params=_cp(),
    )(qr, kv, kpe)


def attn_bwd(cfg, qr, kv, kpe, o, lse, do, *, name):
    lp, t, mh = cfg.lp, cfg.t, cfg.mh
    blocks = _q_blocks(cfg)

    def body(q_ref, kv_ref, kp_ref, o_ref, l_ref, do_ref, dq_ref, dkv_ref, dkp_ref, dk_acc, dv_acc, s_scr):
        dk_acc[...] = jnp.zeros_like(dk_acc)
        dv_acc[...] = jnp.zeros_like(dv_acc)
        for qs, qe in blocks:
            n = qe
            q = q_ref[qs:qe, :]
            k2 = jnp.concatenate([kv_ref[0:n, 0:LANE], kp_ref[0:n, :]], axis=1)
            dob = do_ref[qs:qe, :].astype(BF16)
            delta = jnp.sum(dob.astype(F32) * o_ref[qs:qe, :].astype(F32), axis=-1, keepdims=True)
            s = _masked_scores(cfg, q, k2, qs, qe, s_scr)
            p = jnp.exp(s - l_ref[qs:qe, 0:1])
            dp = _nt(dob, kv_ref[0:n, LANE:2 * LANE])
            ds = (p * (dp - delta)).astype(BF16)
            dq_ref[qs:qe, :] = _nn(ds, k2).astype(BF16)
            dv_acc[0:n, :] += _tn(p.astype(BF16), dob)
            dk_acc[0:n, :] += _tn(ds, q)
        dkv_ref[:, 0:LANE] = dk_acc[:, 0:LANE].astype(BF16)
        dkv_ref[:, LANE:2 * LANE] = dv_acc[...].astype(BF16)
        dkp_ref[0] = dk_acc[:, LANE:2 * LANE]

    hb = pl.BlockSpec((lp, 2 * LANE), lambda b, h: (b, h))
    ob = pl.BlockSpec((lp, LANE), lambda b, h: (b, h))
    return pl.pallas_call(
        body, name=name, grid=(cfg.bsz, mh),
        in_specs=[hb, hb, pl.BlockSpec((lp, LANE), lambda b, h: (b, 0)), ob, ob, ob],
        out_specs=[hb, hb, pl.BlockSpec((1, lp, LANE), lambda b, h: (h, b, 0))],
        out_shape=[_sds((t, cfg.qw), BF16), _sds((t, mh * 2 * LANE), BF16), _sds((mh, t, LANE), F32)],
        scratch_shapes=[pltpu.VMEM((lp, 2 * LANE), F32), pltpu.VMEM((lp, LANE), F32),
                        pltpu.VMEM((_max_q_block(cfg), lp), F32)], compiler_params=_cp(),
    )(qr, kv, kpe, o, lse, do)


def _live_rows(cfg, tr, shape):
    rows = pl.program_id(1) * tr + lax.broadcasted_iota(jnp.int32, shape, 0)
    return rows >= cfg.pad


def gate_fwd(cfg, ya, yb, g, *, name):
    d, lp = cfg.d, cfg.lp
    tr = _pick(lp, 544, 16)
    nrb = lp // tr

    def body(ya_ref, yb_ref, ga_ref, gb_ref, o_ref):
        f = lambda ref: ref[...].astype(F32)
        mix = jax.nn.sigmoid(f(ga_ref)) * f(ya_ref) + jax.nn.sigmoid(f(gb_ref)) * f(yb_ref)
        o_ref[...] = jnp.where(_live_rows(cfg, tr, mix.shape), mix, 0.0).astype(BF16)

    row = pl.BlockSpec((tr, d), lambda b, j: (b * nrb + j, 0))
    row1 = pl.BlockSpec((tr, d), lambda b, j: (b * nrb + j, 1))
    return pl.pallas_call(
        body, name=name, grid=(cfg.bsz, nrb), in_specs=[row, row, row, row1], out_specs=row,
        out_shape=_sds((cfg.t, d), BF16), compiler_params=_cp(),
    )(ya, yb, g, g)


def gate_bwd(cfg, dmix, ya, yb, g, *, name):
    d, lp = cfg.d, cfg.lp
    tr = _pick(lp, 544, 16)
    nrb = lp // tr

    def body(dm_ref, ya_ref, yb_ref, ga_ref, gb_ref, dya_ref, dyb_ref, dg_ref):
        dm = dm_ref[...].astype(F32)
        dm = jnp.where(_live_rows(cfg, tr, dm.shape), dm, 0.0)
        sa = jax.nn.sigmoid(ga_ref[...].astype(F32))
        sb = jax.nn.sigmoid(gb_ref[...].astype(F32))
        dya_ref[...] = (dm * sa).astype(BF16)
        dyb_ref[...] = (dm * sb).astype(BF16)
        dg_ref[:, 0:d] = (dm * ya_ref[...].astype(F32) * sa * (1.0 - sa)).astype(BF16)
        dg_ref[:, d:2 * d] = (dm * yb_ref[...].astype(F32) * sb * (1.0 - sb)).astype(BF16)

    row = pl.BlockSpec((tr, d), lambda b, j: (b * nrb + j, 0))
    row1 = pl.BlockSpec((tr, d), lambda b, j: (b * nrb + j, 1))
    row2 = pl.BlockSpec((tr, 2 * d), lambda b, j: (b * nrb + j, 0))
    return pl.pallas_call(
        body, name=name, grid=(cfg.bsz, nrb), in_specs=[row, row, row, row, row1], out_specs=[row, row, row2],
        out_shape=[_sds((cfg.t, d), BF16), _sds((cfg.t, d), BF16), _sds((cfg.t, 2 * d), BF16)], compiler_params=_cp(),
    )(dmix, ya, yb, g, g)


def loss_head(cfg, h, target, w, *, name):
    d, q, nc = cfg.d, cfg.chunk, cfg.nchunks
    tpb = cfg.seq // q

    def body(h_ref, t_ref, w_ref, loss_ref, dh_ref, dw_ref):
        j = pl.program_id(1)

        @pl.when(jnp.logical_and(j == 0, pl.program_id(0) == 0))
        def _():
            loss_ref[...] = jnp.zeros_like(loss_ref)
            dw_ref[...] = jnp.zeros_like(dw_ref)

        @pl.when(j == 0)
        def _():
            dh_ref[...] = jnp.zeros_like(dh_ref)

        @pl.when(j > 0)
        def _():
            xv = h_ref[...]
            r = lax.rsqrt(jnp.mean(xv * xv, axis=-1, keepdims=True) + EPS)
            xh = xv * r
            err = xh * w_ref[...] - t_ref[...]
            loss_ref[...] += 0.5 * jnp.sum(jnp.sum(err * err, axis=-1, keepdims=True) / d, axis=0, keepdims=True)
            dy = err * (1.0 / d)
            g = dy * w_ref[...]
            dh_ref[...] = r * (g - xh * jnp.mean(g * xh, axis=-1, keepdims=True))
            dw_ref[...] += jnp.sum(dy * xh, axis=0, keepdims=True)

    row = pl.BlockSpec((q, d), lambda b, j: (b * nc + j, 0))
    loss, dh, dw = pl.pallas_call(
        body, name=name, grid=(cfg.bsz, nc),
        in_specs=[row, pl.BlockSpec((q, d), lambda b, j: (b * tpb + jnp.maximum(j - 1, 0), 0)),
                  pl.BlockSpec((1, d), lambda b, j: (0, 0))],
        out_specs=[pl.BlockSpec((8, LANE), lambda b, j: (0, 0)), row, pl.BlockSpec((1, d), lambda b, j: (0, 0))],
        out_shape=[_sds((8, LANE), F32), _sds((cfg.t, d), F32), _sds((1, d), F32)], compiler_params=_cp(),
    )(h, target, w.reshape(1, d))
    return loss[0, 0], dh, dw[0]


def _rows_tile(r, c):
    return _pick(r, max(8, (1 << 18) // max(c, 1) // 8 * 8), 8)


def _adam_update(w, g, m, v):
    c1 = 1.0 - ADAM_B1 ** ADAM_STEP
    c2 = 1.0 - ADAM_B2 ** ADAM_STEP
    mn = ADAM_B1 * m + (1.0 - ADAM_B1) * g
    vn = ADAM_B2 * v + (1.0 - ADAM_B2) * (g * g)
    delta = -ADAM_LR * ((mn / c1) / (jnp.sqrt(vn / c2) + ADAM_EPS) + ADAM_WD * w)
    return delta, mn, vn


def adamw_layer(w, m, v, g, li, prev, dep, *, name):
    _, r, c = w.shape
    tr = _rows_tile(r, c)

    def body(*refs):
        w_ref, m_ref, v_ref, g_ref = refs[:4]
        go_ref, d_ref, mo_ref, vo_ref = refs[-4:]
        gv = g_ref[...]
        delta, mn, vn = _adam_update(w_ref[0], gv, m_ref[0], v_ref[0])
        go_ref[0] = gv
        d_ref[0] = delta
        mo_ref[0] = mn
        vo_ref[0] = vn

    if tr * c * 4 >= (1 << 16):
        steps = r // tr
        blk3 = pl.BlockSpec((1, tr, c), lambda i: (li, i, 0))
        blk2 = pl.BlockSpec((tr, c), lambda i: (i, 0))
    else:
        tc = _pick(c, max(LANE, (1 << 18) // r // LANE * LANE), LANE)
        steps = c // tc
        blk3 = pl.BlockSpec((1, r, tc), lambda i: (li, 0, i))
        blk2 = pl.BlockSpec((r, tc), lambda i: (0, i))
    anyspec = pl.BlockSpec(memory_space=pl.ANY)
    in_specs = [blk3, blk3, blk3, blk2, anyspec]
    args = [w, m, v, g, dep]
    aliases = {}
    if prev is not None:
        in_specs += [anyspec] * 4
        args += list(prev)
        aliases = {5 + i: i for i in range(4)}
    return pl.pallas_call(
        body, name=name, grid=(steps,), in_specs=in_specs, out_specs=[blk3] * 4,
        out_shape=[_sds(w.shape, F32)] * 4, input_output_aliases=aliases, compiler_params=_cp(),
    )(*args)


def pair_add(g4, other, half, *, name):
    n, _, r, c = g4.shape
    tr = _rows_tile(r, c)

    def body(h_ref, a_ref, b_ref, o_ref):
        o_ref[0] = (a_ref[0, 0].astype(F32) + b_ref[0].astype(F32)).astype(BF16)

    blk = pl.BlockSpec((1, tr, c), lambda j, i, h: (j, i, 0))
    grid_spec = pltpu.PrefetchScalarGridSpec(
        num_scalar_prefetch=1, grid=(n, r // tr),
        in_specs=[pl.BlockSpec((1, 1, tr, c), lambda j, i, h: (j, h[0], i, 0)), blk], out_specs=blk)
    return pl.pallas_call(body, name=name, grid_spec=grid_spec, out_shape=_sds((n, r, c), BF16),
                          compiler_params=_cp())(half, g4, other)


def chip_sum(recv, part, where, *, name):
    n, r, c = recv.shape
    tr = _rows_tile(r, c)

    def body(s_ref, *refs):
        own_ref, o_ref = refs[n], refs[n + 1]
        acc = None
        for j in range(n):
            term = jnp.where(s_ref[0] == j, own_ref[0], refs[j][0]).astype(F32)
            acc = term if acc is None else acc + term
        o_ref[0] = acc

    def slot(j):
        return pl.BlockSpec((1, tr, c), lambda i, s: (jnp.where(s[0] == j, (j + 1) % n, j), i, 0))

    grid_spec = pltpu.PrefetchScalarGridSpec(
        num_scalar_prefetch=1, grid=(r // tr,),
        in_specs=[slot(j) for j in range(n)] + [pl.BlockSpec((1, tr, c), lambda i, s: (s[0], i, 0))],
        out_specs=pl.BlockSpec((1, tr, c), lambda i, s: (s[1], i, 0)))
    return pl.pallas_call(body, name=name, grid_spec=grid_spec, out_shape=_sds((2, r, c), F32),
                          compiler_params=_cp())(where, *([recv] * n), part)


def _coords():
    return lax.axis_index("x"), lax.axis_index("y"), lax.axis_index("c")


def _other_chips(x, y):
    return [(1 - x, y), (x, 1 - y), (1 - x, 1 - y)]


def gather_chips(arrs, *, name):
    n = len(arrs)
    anyspec = pl.BlockSpec(memory_space=pl.ANY)

    def body(*refs):
        ins, outs = refs[:n], refs[n:2 * n]
        send_sems, recv_sems, local_sems = refs[2 * n:]
        x, y, c = _coords()
        me = 2 * x + y
        chips = _other_chips(x, y)
        copies = []
        for k in range(n):
            loc = pltpu.make_async_copy(ins[k], outs[k].at[me], local_sems.at[k])
            loc.start()
            copies.append(loc)
        sends = []
        for k in range(n):
            for j, (px, py) in enumerate(chips):
                cp = pltpu.make_async_remote_copy(
                    src_ref=ins[k], dst_ref=outs[k].at[me], send_sem=send_sems.at[k, j], recv_sem=recv_sems.at[k, j],
                    device_id=(px, py, c), device_id_type=MESH)
                cp.start()
                sends.append(cp)
        for k in range(n):
            for j, (px, py) in enumerate(chips):
                pltpu.make_async_remote_copy(
                    src_ref=ins[k], dst_ref=outs[k].at[2 * px + py], send_sem=send_sems.at[k, j],
                    recv_sem=recv_sems.at[k, j], device_id=(px, py, c), device_id_type=MESH).wait_recv()
        for cp in sends:
            cp.wait_send()
        for cp in copies:
            cp.wait()

    return pl.pallas_call(
        body, name=name, in_specs=[anyspec] * n, out_specs=[anyspec] * n,
        out_shape=[_sds((4,) + a.shape, a.dtype) for a in arrs],
        scratch_shapes=[pltpu.SemaphoreType.DMA((n, 3)), pltpu.SemaphoreType.DMA((n, 3)), pltpu.SemaphoreType.DMA((n,))],
        compiler_params=_cp(has_side_effects=True),
    )(*arrs)


def allreduce_small(vec, after, *, name):
    r, c = vec.shape

    def body(v_ref, after_ref, o_ref, buf, send_sems, recv_sems):
        x, y, cc = _coords()
        me = 4 * x + 2 * y + cc
        buf[me] = v_ref[...]
        sends = []
        flips = [(fx, fy, fc) for fx in (0, 1) for fy in (0, 1) for fc in (0, 1)][1:]
        for j, (fx, fy, fc) in enumerate(flips):
            peer = ((1 - x) if fx else x, (1 - y) if fy else y, (1 - cc) if fc else cc)
            cp = pltpu.make_async_remote_copy(
                src_ref=v_ref, dst_ref=buf.at[me], send_sem=send_sems.at[j], recv_sem=recv_sems.at[j],
                device_id=peer, device_id_type=MESH)
            cp.start()
            sends.append(cp)
        for j, (fx, fy, fc) in enumerate(flips):
            px, py, pc = ((1 - x) if fx else x, (1 - y) if fy else y, (1 - cc) if fc else cc)
            pltpu.make_async_remote_copy(
                src_ref=v_ref, dst_ref=buf.at[4 * px + 2 * py + pc], send_sem=send_sems.at[j],
                recv_sem=recv_sems.at[j], device_id=(px, py, pc), device_id_type=MESH).wait_recv()
        for cp in sends:
            cp.wait_send()
        acc = buf[0]
        for k in range(1, 8):
            acc = acc + buf[k]
        o_ref[...] = acc

    vm = pl.BlockSpec(memory_space=pltpu.VMEM)
    return pl.pallas_call(
        body, name=name, in_specs=[vm, pl.BlockSpec(memory_space=pl.ANY)], out_specs=vm, out_shape=_sds((r, c), F32),
        scratch_shapes=[pltpu.VMEM((8, r, c), F32), pltpu.SemaphoreType.DMA((7,)), pltpu.SemaphoreType.DMA((7,))],
        compiler_params=_cp(has_side_effects=True),
    )(vec, after)


def pair_share(lands, owns, *, name):
    n = len(lands)
    anyspec = pl.BlockSpec(memory_space=pl.ANY)

    def body(*refs):
        ins, own_refs, outs = refs[:n], refs[n:2 * n], refs[2 * n:3 * n]
        send_sems, recv_sems = refs[3 * n:]
        x, y, c = _coords()
        me = 2 * x + y
        sib = (x, y, 1 - c)
        sends = []
        for k in range(n):
            for j, (px, py) in enumerate(_other_chips(x, y)):
                cp = pltpu.make_async_remote_copy(
                    src_ref=ins[k].at[2 * px + py, c], dst_ref=outs[k].at[2 * px + py, c], send_sem=send_sems.at[k, j],
                    recv_sem=recv_sems.at[k, j], device_id=sib, device_id_type=MESH)
                cp.start()
                sends.append(cp)
            cp = pltpu.make_async_remote_copy(
                src_ref=own_refs[k], dst_ref=outs[k].at[me], send_sem=send_sems.at[k, 3], recv_sem=recv_sems.at[k, 3],
                device_id=sib, device_id_type=MESH)
            cp.start()
            sends.append(cp)
        for k in range(n):
            for j, (px, py) in enumerate(_other_chips(x, y)):
                pltpu.make_async_remote_copy(
                    src_ref=ins[k].at[2 * px + py, c], dst_ref=outs[k].at[2 * px + py, 1 - c],
                    send_sem=send_sems.at[k, j], recv_sem=recv_sems.at[k, j], device_id=sib,
                    device_id_type=MESH).wait_recv()
            pltpu.make_async_remote_copy(
                src_ref=own_refs[k], dst_ref=outs[k].at[me], send_sem=send_sems.at[k, 3], recv_sem=recv_sems.at[k, 3],
                device_id=sib, device_id_type=MESH).wait_recv()
        for cp in sends:
            cp.wait_send()

    return pl.pallas_call(
        body, name=name, in_specs=[anyspec] * (2 * n), out_specs=[anyspec] * n,
        out_shape=[_sds(a.shape, a.dtype) for a in lands], input_output_aliases={k: k for k in range(n)},
        scratch_shapes=[pltpu.SemaphoreType.DMA((n, 4)), pltpu.SemaphoreType.DMA((n, 4))],
        compiler_params=_cp(has_side_effects=True),
    )(*lands, *owns)


def pair_fill(arrs, *, name):
    n = len(arrs)
    anyspec = pl.BlockSpec(memory_space=pl.ANY)

    def body(*refs):
        ins, outs = refs[:n], refs[n:2 * n]
        send_sems, recv_sems = refs[2 * n:]
        x, y, c = _coords()
        sends = []
        for k in range(n):
            cp = pltpu.make_async_remote_copy(
                src_ref=ins[k].at[c], dst_ref=outs[k].at[c], send_sem=send_sems.at[k], recv_sem=recv_sems.at[k],
                device_id=(x, y, 1 - c), device_id_type=MESH)
            cp.start()
            sends.append(cp)
        for k in range(n):
            pltpu.make_async_remote_copy(
                src_ref=ins[k].at[c], dst_ref=outs[k].at[1 - c], send_sem=send_sems.at[k], recv_sem=recv_sems.at[k],
                device_id=(x, y, 1 - c), device_id_type=MESH).wait_recv()
        for cp in sends:
            cp.wait_send()

    return pl.pallas_call(
        body, name=name, in_specs=[anyspec] * n, out_specs=[anyspec] * n,
        out_shape=[_sds(a.shape, a.dtype) for a in arrs], input_output_aliases={k: k for k in range(n)},
        scratch_shapes=[pltpu.SemaphoreType.DMA((n,)), pltpu.SemaphoreType.DMA((n,))],
        compiler_params=_cp(has_side_effects=True),
    )(*arrs)


_HBM = pl.BlockSpec(memory_space=pltpu.HBM)
_SEM = pl.BlockSpec(memory_space=pltpu.SEMAPHORE)


_COPIES_PER_ARRAY = {"gather": 3, "scatter": 3, "share": 4, "exchange": 4}


def _ici_copies(kind, srcs, lands, send_sems, recv_sems):
    x, y, c = _coords()
    me = 2 * x + y
    per = _COPIES_PER_ARRAY[kind]
    sends, recvs = [], []
    for k in range(len(srcs)):
        triples = []
        for j, (px, py) in enumerate(_other_chips(x, y)):
            peer = 2 * px + py
            if kind == "gather":
                triples.append((srcs[k].at[c], lands[k].at[me, c], lands[k].at[peer, c], (px, py, c)))
            elif kind == "scatter":
                triples.append((srcs[k].at[peer], lands[k].at[me], lands[k].at[peer], (px, py, c)))
            elif kind == "share":
                triples.append((lands[k].at[peer, c], lands[k].at[peer, c], lands[k].at[peer, 1 - c], (x, y, 1 - c)))
        if kind == "share":
            triples.append((srcs[k], lands[k].at[me], lands[k].at[me], (x, y, 1 - c)))
        if kind == "exchange":
            triples = [(srcs[k].at[j, 1 - c], lands[k].at[j], lands[k].at[j], (x, y, 1 - c)) for j in range(4)]
        for j, (src, there, here, dev) in enumerate(triples):
            sem = per * k + j
            mk = functools.partial(pltpu.make_async_remote_copy, src_ref=src, send_sem=send_sems.at[sem],
                                   recv_sem=recv_sems.at[sem], device_id=dev, device_id_type=MESH)
            sends.append(mk(dst_ref=there))
            recvs.append(mk(dst_ref=here))
    return sends, recvs


def ici_start(kind, srcs, lands, after, *, name):
    n = len(srcs)

    def body(*refs):
        src_refs, land_refs = refs[:n], refs[n:2 * n]
        send_sems, recv_sems = refs[2 * n + 1], refs[2 * n + 2]
        token = refs[-1]
        sends, _ = _ici_copies(kind, src_refs, land_refs, send_sems, recv_sems)
        for cp in sends:
            cp.start()
        token[...] = jnp.zeros_like(token)

    both = list(srcs) + list(lands)
    out = pl.pallas_call(
        body, name=name,
        in_specs=[_HBM] * (2 * n) + [pl.BlockSpec(memory_space=pl.ANY)],
        out_shape=(pltpu.SemaphoreType.DMA((_COPIES_PER_ARRAY[kind] * n,)),
                   pltpu.SemaphoreType.DMA((_COPIES_PER_ARRAY[kind] * n,)),
                   *[pltpu.HBM(a.shape, a.dtype) for a in both], _sds((8, LANE), F32)),
        out_specs=(_SEM, _SEM, *([_HBM] * (2 * n)), pl.BlockSpec(memory_space=pltpu.VMEM)),
        input_output_aliases={i: 2 + i for i in range(2 * n)},
        compiler_params=_cp(has_side_effects=pltpu.SideEffectType.DATAFLOW_SIDE_EFFECTING),
    )(*[pltpu.with_memory_space_constraint(a, pltpu.HBM) for a in both], after)
    return out[0], out[1], list(out[2:2 + n]), list(out[2 + n:2 + 2 * n]), out[-1]


def ici_wait(kind, started, after, *, name):
    send_sems, recv_sems, srcs, lands, _ = started
    n = len(srcs)

    def body(*refs):
        src_refs, land_refs = refs[:n], refs[n:2 * n]
        sends, recvs = _ici_copies(kind, src_refs, land_refs, refs[2 * n], refs[2 * n + 1])
        for cp in sends:
            cp.wait_send()
        for cp in recvs:
            cp.wait_recv()

    both = list(srcs) + list(lands)
    out = pl.pallas_call(
        body, name=name,
        in_specs=[_HBM] * (2 * n) + [_SEM, _SEM, pl.BlockSpec(memory_space=pl.ANY)],
        out_shape=tuple(pltpu.HBM(a.shape, a.dtype) for a in both), out_specs=tuple([_HBM] * (2 * n)),
        input_output_aliases={i: i for i in range(2 * n)},
        compiler_params=_cp(has_side_effects=pltpu.SideEffectType.DATAFLOW_SIDE_EFFECTING),
    )(*both, send_sems, recv_sems, after)
    return list(out[:n]), list(out[n:])


BIG = ["w_in", "w_uq", "w_ukv", "w_branch_ssm", "w_branch_mla", "w_out", "w_mlp_up", "w_mlp_down"]
COL_SHARDED = {"w_in", "w_uq", "w_ukv", "w_mlp_up"}
SMALL_REPL = ["norm_mix_w", "conv_b", "dt_bias", "a_log", "d_skip", "ssm_norm_w", "q_norm_w", "kv_norm_w", "norm_mlp_w"]


def _unshard_layer(name, g):
    _, r, c = g.shape
    if name in COL_SHARDED:
        return jnp.transpose(g, (1, 0, 2)).reshape(r, 4 * c)
    return g.reshape(4 * r, c)


def _to_shards(name, full):
    r, c = full.shape
    if name in COL_SHARDED:
        return jnp.transpose(full.reshape(r, 4, c // 4), (1, 0, 2))
    return full.reshape(4, r // 4, c)


REST = [k for k in BIG if k != "w_in"]


def prep_layer(cfg, w):
    out = {}
    if "w_in" in w:
        sp = np.cumsum(cfg.in_splits)[:-1].tolist()
        z, xbc, dt, cq, ckv, kr, gs, gm = jnp.split(w["w_in"], sp, axis=1)
        zpad = lambda n: jnp.zeros((cfg.d, n), z.dtype)
        out.update(w_z=z, w_xbc=xbc, w_g=jnp.concatenate([gs, gm], axis=1),
                   w_s=jnp.concatenate([cq, ckv, kr, zpad(LANE - cfg.rope), dt, zpad(LANE - cfg.heads)], axis=1))
    if "w_uq" in w:
        out.update(
            w_uq=jnp.pad(w["w_uq"].reshape(cfg.ql, cfg.mh, cfg.nope + cfg.rope),
                         ((0, 0), (0, 0), (0, 2 * LANE - cfg.nope - cfg.rope))).reshape(cfg.ql, cfg.qw),
            w_ukv=w["w_ukv"], w_bs=w["w_branch_ssm"], w_bm=w["w_branch_mla"], w_out=w["w_out"],
            w_up=w["w_mlp_up"], w_down=w["w_mlp_down"])
    return {k: v.astype(BF16) for k, v in out.items()}


def unprep_grads(cfg, g):
    out = {}
    if "w_s" in g:
        ql, kvl = cfg.ql, cfg.kvl
        ds_ = g["w_s"]
        cq, ckv = ds_[:, :ql], ds_[:, ql:ql + kvl]
        kr = ds_[:, ql + kvl:ql + kvl + cfg.rope]
        dt = ds_[:, ql + kvl + LANE:ql + kvl + LANE + cfg.heads]
        out["w_in"] = jnp.concatenate([g["w_z"], g["w_xbc"], dt, cq, ckv, kr, g["w_g"]], axis=1)
    if "w_uq" in g:
        out.update(
            w_uq=g["w_uq"].reshape(cfg.ql, cfg.mh, 2 * LANE)[:, :, :cfg.nope + cfg.rope].reshape(cfg.ql, -1),
            w_ukv=g["w_ukv"], w_branch_ssm=g["w_bs"], w_branch_mla=g["w_bm"],
            w_out=g["w_out"], w_mlp_up=g["w_up"], w_mlp_down=g["w_down"])
    return out


def _hook(hooks, name, arg):
    if hooks and name in hooks:
        return hooks[name](arg)[0, 0]
    return 0.0


def layer_fwd(cfg, h, pw, sm, tabs, li, hooks=None):
    n = lambda s: f"l{li}_{s}"
    u = rmsnorm_fwd(h, sm["norm_mix_w"], name=n("norm_mix"))
    z = matmul(u, pw["w_z"], out_dtype=BF16, name=n("in_z"))
    xbc = matmul(u, pw["w_xbc"], name=n("in_xbc"))
    g = matmul(u, pw["w_g"], out_dtype=BF16, name=n("in_g"))
    small = matmul(u, pw["w_s"], name=n("in_s"))
    xc = conv_fwd(cfg, xbc, sm["conv_w"], sm["conv_b"], name=n("conv"))
    dt_bias = sm["dt_bias_p"] + _hook(hooks, "after_conv", xc)
    y, sin = ssd_fwd(cfg, xc, small, dt_bias, sm["avec"], sm["dexp"], name=n("ssd"))
    y_ssm = tail_fwd(cfg, y, z, sm["ssm_norm_w"], name=n("tail"))
    if hooks and "weights" in hooks:
        pw = dict(pw, **hooks["weights"](y_ssm))
    cqn = rmsnorm_fwd(small, sm["q_norm_w"], cw=cfg.ql, ci=0, name=n("q_norm"))
    ckvn = rmsnorm_fwd(small, sm["kv_norm_w"], cw=cfg.kvl, ci=cfg.ql // cfg.kvl, name=n("kv_norm"))
    qf = matmul(cqn, pw["w_uq"], out_dtype=BF16, name=n("uq"))
    kv = matmul(ckvn, pw["w_ukv"], out_dtype=BF16, name=n("ukv"))
    qr, kpe = rope_fwd(cfg, qf, small, tabs, name=n("rope"))
    o, lse = attn_fwd(cfg, qr, kv, kpe, name=n("attn"))
    ya = matmul(y_ssm, pw["w_bs"], out_dtype=BF16, name=n("branch_ssm"))
    yb = matmul(o, pw["w_bm"], out_dtype=BF16, name=n("branch_mla"))
    mixed = gate_fwd(cfg, ya, yb, g, name=n("gate"))
    h1 = matmul(mixed, pw["w_out"], add=h, name=n("out"))
    v = rmsnorm_fwd(h1, sm["norm_mlp_w"] + _hook(hooks, "after_attn", o), name=n("norm_mlp"))
    a, act = matmul(v, pw["w_up"], name=n("up"), epilogue=_ep_relu2, out_dtypes=(BF16, BF16))
    h2 = matmul(act, pw["w_down"], add=h1, name=n("down"))
    saved = dict(h=h, u=u, z=z, xbc=xbc, g=g, small=small, xc=xc, y=y, sin=sin, y_ssm=y_ssm, cqn=cqn, ckvn=ckvn,
                 qr=qr, kv=kv, kpe=kpe, o=o, lse=lse, ya=ya, yb=yb, mixed=mixed, h1=h1, v=v, a=a, act=act)
    return h2, saved, pw


def layer_bwd(cfg, dh2, pw, sm, tabs, s, li, hooks=None):
    n = lambda t: f"l{li}_b_{t}"
    gw, gs = {}, {}
    wgrad = functools.partial(matmul, ta=True, out_dtype=BF16)
    gw["w_down"] = wgrad(s["act"], dh2, name=n("dw_down"))
    da = matmul(dh2, pw["w_down"], tb=True, name=n("dact"), epilogue=_ep_relu2_grad, extras=(s["a"],),
                out_dtypes=(BF16,))
    gw["w_up"] = wgrad(s["v"], da, name=n("dw_up"))
    dv = matmul(da, pw["w_up"], tb=True, out_dtype=BF16, name=n("dv"))
    dh1, gs["norm_mlp_w"] = rmsnorm_bwd(dv, s["h1"], sm["norm_mlp_w"], res=dh2, name=n("norm_mlp"))
    gw["w_out"] = wgrad(s["mixed"], dh1, name=n("dw_out"))
    dmix = matmul(dh1, pw["w_out"], tb=True, out_dtype=BF16, name=n("dmix"))
    dya, dyb, dg = gate_bwd(cfg, dmix, s["ya"], s["yb"], s["g"], name=n("gate"))
    gw["w_bs"] = wgrad(s["y_ssm"], dya, name=n("dw_bs"))
    gw["w_bm"] = wgrad(s["o"], dyb, name=n("dw_bm"))
    dy_ssm = matmul(dya, pw["w_bs"], tb=True, out_dtype=BF16, name=n("dy_ssm"))
    do = matmul(dyb, pw["w_bm"], tb=True, out_dtype=BF16, name=n("do"))
    dq, dkv, dkpe = attn_bwd(cfg, s["qr"], s["kv"], s["kpe"], s["o"], s["lse"], do, name=n("attn"))
    dqf, dkr = rope_bwd(cfg, dq, dkpe, tabs, name=n("rope"))
    gw["w_uq"] = wgrad(s["cqn"], dqf, name=n("dw_uq"))
    gw["w_ukv"] = wgrad(s["ckvn"], dkv, name=n("dw_ukv"))
    dcqn = matmul(dqf, pw["w_uq"], tb=True, name=n("dcqn"))
    dckvn = matmul(dkv, pw["w_ukv"], tb=True, name=n("dckvn"))
    q_norm_w = sm["q_norm_w"] + _hook(hooks, "after_attn", dqf)
    dcq, gs["q_norm_w"] = rmsnorm_bwd(dcqn, s["small"], q_norm_w, cw=cfg.ql, ci=0, out_dtype=BF16, name=n("q_norm"))
    dckv, gs["kv_norm_w"] = rmsnorm_bwd(dckvn, s["small"], sm["kv_norm_w"], cw=cfg.kvl, ci=cfg.ql // cfg.kvl,
                                        out_dtype=BF16, name=n("kv_norm"))
    ssm_norm_w = sm["ssm_norm_w"] + _hook(hooks, "early", dict(gw))
    dy, dz, gs["ssm_norm_w"] = tail_bwd(cfg, dy_ssm, s["y"], s["z"], ssm_norm_w, name=n("tail"))
    dxc, ddt, ddexp, dav, dbias = ssd_bwd(cfg, s["xc"], s["small"], sm["dt_bias_p"], sm["avec"], sm["dexp"],
                                          s["sin"], dy, name=n("ssd"))
    conv_b = sm["conv_b"] + _hook(hooks, "after_ssd", dxc)
    dxbc, gs["conv_w"], gs["conv_b"] = conv_bwd(cfg, s["xbc"], sm["conv_w"], conv_b, dxc, name=n("conv"))
    gs["d_skip"] = ddexp.reshape(cfg.heads, cfg.hd).sum(axis=1)
    gs["a_log"] = (dav[0] * sm["avec"][0])[:cfg.heads]
    gs["dt_bias"] = dbias[0, :cfg.heads]
    dsmall = jnp.concatenate([dcq, dckv, dkr.astype(BF16), ddt.astype(BF16)], axis=1)
    gw["w_z"] = wgrad(s["u"], dz, name=n("dw_z"))
    gw["w_xbc"] = wgrad(s["u"], dxbc, name=n("dw_xbc"))
    gw["w_g"] = wgrad(s["u"], dg, name=n("dw_g"))
    gw["w_s"] = wgrad(s["u"], dsmall, name=n("dw_s"))
    du = matmul(dz, pw["w_z"], tb=True, name=n("du_z"))
    du = matmul(dxbc, pw["w_xbc"], tb=True, add=du, name=n("du_xbc"))
    du = matmul(dg, pw["w_g"], tb=True, add=du, name=n("du_g"))
    du = matmul(dsmall, pw["w_s"], tb=True, add=du, name=n("du_s"))
    dh, gs["norm_mix_w"] = rmsnorm_bwd(du, s["h"], sm["norm_mix_w"], res=dh1, name=n("norm_mix"))
    return dh, gw, gs


def small_params(cfg, p, li):
    pad_l = lambda v: jnp.pad(v, (0, LANE - v.shape[0])).reshape(1, LANE)
    return dict(
        norm_mix_w=p["norm_mix_w"][li], conv_w=p["conv_w"][li], conv_b=p["conv_b"][li],
        dt_bias_p=pad_l(p["dt_bias"][li]), avec=pad_l(-jnp.exp(p["a_log"][li])),
        dexp=jnp.repeat(p["d_skip"][li], cfg.hd).reshape(1, cfg.inner),
        ssm_norm_w=p["ssm_norm_w"][li], q_norm_w=p["q_norm_w"][li], kv_norm_w=p["kv_norm_w"][li],
        norm_mlp_w=p["norm_mlp_w"][li])


def local_step(cfg, x, target, p, depth=2):
    bsz, d = cfg.bsz, cfg.d
    lead = jnp.zeros((bsz, cfg.pad, d), F32)
    meta = jnp.broadcast_to(p["meta_tokens"][None], (bsz, cfg.n_meta, d))
    h = jnp.concatenate([lead, meta, x], axis=1).reshape(cfg.t, d)
    tabs = rope_tables(cfg)
    saved, sms = [], []
    for li in range(depth):
        sm = small_params(cfg, p, li)
        h, s, _ = layer_fwd(cfg, h, p["pw"][li], sm, tabs, li)
        saved.append(s)
        sms.append(sm)
    loss, dh, dfw = loss_head(cfg, h, target.reshape(bsz * cfg.seq, d), p["final_norm_w"], name="loss_head")
    gws, gss = [None] * depth, [None] * depth
    for li in reversed(range(depth)):
        dh, gws[li], gss[li] = layer_bwd(cfg, dh, p["pw"][li], sms[li], tabs, saved[li], li)
    dh = dh.reshape(bsz, cfg.lp, d)
    grad_x = dh[:, cfg.chunk:, :]
    gmeta = jnp.sum(dh[:, cfg.pad:cfg.chunk, :], axis=0)
    return loss, grad_x, gmeta, gws, gss, dfw


def _pack_small(parts):
    flat = jnp.concatenate([a.reshape(-1) for a in parts])
    n = flat.shape[0]
    npad = -n % (8 * LANE)
    return jnp.pad(flat, (0, npad)).reshape(-1, LANE), n


def _unpack_small(vec, shapes):
    flat = vec.reshape(-1)
    out, off = [], 0
    for sh in shapes:
        sz = int(np.prod(sh))
        out.append(flat[off:off + sz].reshape(sh))
        off += sz
    return out


def _as2d(a):
    return a.reshape(-1, a.shape[-1])


def kernel(x, meta_tokens, norm_mix_w, w_in, conv_w, conv_b, dt_bias, a_log, d_skip, ssm_norm_w, q_norm_w, kv_norm_w, w_uq, w_ukv, w_branch_ssm, w_branch_mla, w_out, norm_mlp_w, w_mlp_up, w_mlp_down, final_norm_w, loss_target, m_meta_tokens, m_norm_mix_w, m_w_in, m_conv_w, m_conv_b, m_dt_bias, m_a_log, m_d_skip, m_ssm_norm_w, m_q_norm_w, m_kv_norm_w, m_w_uq, m_w_ukv, m_w_branch_ssm, m_w_branch_mla, m_w_out, m_norm_mlp_w, m_w_mlp_up, m_w_mlp_down, m_final_norm_w, v_meta_tokens, v_norm_mix_w, v_w_in, v_conv_w, v_conv_b, v_dt_bias, v_a_log, v_d_skip, v_ssm_norm_w, v_q_norm_w, v_kv_norm_w, v_w_uq, v_w_ukv, v_w_branch_ssm, v_w_branch_mla, v_w_out, v_norm_mlp_w, v_w_mlp_up, v_w_mlp_down, v_final_norm_w):
    cfg = CFG
    names = ["meta_tokens", "norm_mix_w", "w_in", "conv_w", "conv_b", "dt_bias", "a_log", "d_skip", "ssm_norm_w",
             "q_norm_w", "kv_norm_w", "w_uq", "w_ukv", "w_branch_ssm", "w_branch_mla", "w_out", "norm_mlp_w",
             "w_mlp_up", "w_mlp_down", "final_norm_w"]
    wts = dict(zip(names, [meta_tokens, norm_mix_w, w_in, conv_w, conv_b, dt_bias, a_log, d_skip, ssm_norm_w,
                           q_norm_w, kv_norm_w, w_uq, w_ukv, w_branch_ssm, w_branch_mla, w_out, norm_mlp_w,
                           w_mlp_up, w_mlp_down, final_norm_w]))
    ms = dict(zip(names, [m_meta_tokens, m_norm_mix_w, m_w_in, m_conv_w, m_conv_b, m_dt_bias, m_a_log, m_d_skip,
                          m_ssm_norm_w, m_q_norm_w, m_kv_norm_w, m_w_uq, m_w_ukv, m_w_branch_ssm, m_w_branch_mla,
                          m_w_out, m_norm_mlp_w, m_w_mlp_up, m_w_mlp_down, m_final_norm_w]))
    vs = dict(zip(names, [v_meta_tokens, v_norm_mix_w, v_w_in, v_conv_w, v_conv_b, v_dt_bias, v_a_log, v_d_skip,
                          v_ssm_norm_w, v_q_norm_w, v_kv_norm_w, v_w_uq, v_w_ukv, v_w_branch_ssm, v_w_branch_mla,
                          v_w_out, v_norm_mlp_w, v_w_mlp_up, v_w_mlp_down, v_final_norm_w]))
    cx, cy, cc = _coords()
    chip = 2 * cx + cy

    half1 = jnp.reshape(cc, (1,)).astype(jnp.int32)
    where2 = jnp.stack([chip, cc]).astype(jnp.int32)
    wb = {k: wts[k].astype(BF16) for k in BIG}
    zero_tok = jnp.zeros((8, LANE), F32)

    def halves(a):
        return a.reshape((2, a.shape[0] // 2) + a.shape[1:])

    def gather_start(li, keys, tag, after):
        srcs = [halves(wb[k][li]) for k in keys]
        lands = [lax.empty((4,) + s.shape, BF16) for s in srcs]
        return ici_start("gather", srcs, lands, after, name=f"gather{li}{tag}_start")

    def gather_finish(li, keys, tag, started, after):
        srcs, lands = ici_wait("gather", started, after, name=f"gather{li}{tag}_wait")
        lands = pair_share(lands, srcs, name=f"gather{li}{tag}_share")
        full = {k: _unshard_layer(k, land.reshape((4, 2 * land.shape[2], land.shape[3])))
                for k, land in zip(keys, lands)}
        return prep_layer(cfg, full)

    def gather_mid(li, keys, tag, started, after):
        srcs, lands = ici_wait("gather", started, after, name=f"gather{li}{tag}_wait")
        return ici_start("share", srcs, lands, zero_tok, name=f"gather{li}{tag}_share_start")

    def gather_end(li, keys, tag, shared, after):
        _, lands = ici_wait("share", shared, after, name=f"gather{li}{tag}_share_wait")
        full = {k: _unshard_layer(k, land.reshape((4, 2 * land.shape[2], land.shape[3])))
                for k, land in zip(keys, lands)}
        return prep_layer(cfg, full)

    def exchange_start(li, keys, tag, gw, after):
        ug = unprep_grads(cfg, gw)
        g4 = []
        for k in keys:
            s = _to_shards(k, ug[k])
            g4.append(s.reshape(4, 2, s.shape[1] // 2, s.shape[2]))
        lands = [lax.empty((4,) + a.shape[2:], a.dtype) for a in g4]
        return ici_start("exchange", g4, lands, after, name=f"grad{li}{tag}_exchange_start")

    def reduce_start(li, keys, tag, exchanged, after):
        g4, theirs = ici_wait("exchange", exchanged, after, name=f"grad{li}{tag}_exchange_wait")
        parts = [pair_add(a, b, half1, name=f"grad{li}_pair_add_{k}") for k, a, b in zip(keys, g4, theirs)]
        lands = [lax.empty(q.shape, q.dtype) for q in parts]
        return ici_start("scatter", parts, lands, zero_tok, name=f"grad{li}{tag}_scatter_start")

    def reduce_finish(li, keys, tag, started, after):
        parts, lands = ici_wait("scatter", started, after, name=f"grad{li}{tag}_scatter_wait")
        sums = [chip_sum(rc, pt, where2, name=f"grad{li}_chip_sum_{k}") for k, rc, pt in zip(keys, lands, parts)]
        sums = pair_fill(sums, name=f"grad{li}{tag}_pair_fill")
        return {k: s.reshape(2 * s.shape[1], s.shape[2]) for k, s in zip(keys, sums)}

    gathered = gather_chips([meta_tokens, conv_w], name="gather_small")
    p = dict(wts)
    p["meta_tokens"] = jnp.transpose(gathered[0], (1, 0, 2)).reshape(cfg.n_meta, cfg.d)
    p["conv_w"] = jnp.transpose(gathered[1], (1, 2, 0, 3)).reshape(2, cfg.convk, cfg.conv_dim)

    st0a = gather_start(0, ["w_in"], "a", gathered[0])
    st0b = gather_start(0, REST, "b", st0a[4])
    st1 = gather_start(1, BIG, "", st0b[4])
    pw0 = gather_finish(0, ["w_in"], "a", st0a, st1[4])

    bsz, d = cfg.bsz, cfg.d
    lead = jnp.zeros((bsz, cfg.pad, d), F32)
    meta = jnp.broadcast_to(p["meta_tokens"][None], (bsz, cfg.n_meta, d))
    h0 = jnp.concatenate([lead, meta, x], axis=1).reshape(cfg.t, d)
    tabs = rope_tables(cfg)
    sm0 = small_params(cfg, p, 0)
    st = {}

    def step(key, fn):
        def run(arg):
            st[key] = fn(arg)
            return st[key][4]
        return run

    h1, sv0, pw0 = layer_fwd(cfg, h0, pw0, sm0, tabs, 0, hooks={
        "after_conv": step("share0b", lambda after: gather_mid(0, REST, "b", st0b, after)),
        "weights": lambda after: gather_end(0, REST, "b", st["share0b"], after),
        "after_attn": step("share1", lambda after: gather_mid(1, BIG, "", st1, after))})
    pw1 = gather_end(1, BIG, "", st["share1"], h1)
    sm1 = small_params(cfg, p, 1)
    h2, sv1, _ = layer_fwd(cfg, h1, pw1, sm1, tabs, 1)
    loss, dh, dfw = loss_head(cfg, h2, loss_target.reshape(bsz * cfg.seq, d), final_norm_w, name="loss_head")
    loss = lax.psum(loss, ("x", "y", "c"))

    dh, gw1, gs1 = layer_bwd(cfg, dh, pw1, sm1, tabs, sv1, 1)
    ex1 = exchange_start(1, BIG, "", gw1, zero_tok)
    sm0b = dict(sm0)
    sm0b["norm_mlp_w"] = sm0["norm_mlp_w"] + ex1[4][0, 0]
    dh, gw0, gs0 = layer_bwd(cfg, dh, pw0, sm0b, tabs, sv0, 0, hooks={
        "after_attn": step("red1", lambda after: reduce_start(1, BIG, "", ex1, after)),
        "early": step("ex0e", lambda gw: exchange_start(0, REST, "e", gw, zero_tok)),
        "after_ssd": step("red0e", lambda after: reduce_start(0, REST, "e", st["ex0e"], after))})
    dh3 = dh.reshape(bsz, cfg.lp, d)
    grad_x = dh3[:, cfg.chunk:, :]
    gmeta = jnp.sum(dh3[:, cfg.pad:cfg.chunk, :], axis=0)
    big1 = reduce_finish(1, BIG, "", st["red1"], dh)
    ex0l = exchange_start(0, ["w_in"], "l", gw0, big1[BIG[-1]])

    small_names = SMALL_REPL + ["conv_w"]
    parts = [jnp.stack([gs0[k], gs1[k]]) for k in small_names] + [dfw, gmeta]
    shapes = [a.shape for a in parts]
    vec, _ = _pack_small(parts)
    red_vec = allreduce_small(vec, ex0l[4], name="allreduce_small")
    red = _unpack_small(red_vec, shapes)
    sg = dict(zip(small_names + ["final_norm_w", "meta_tokens"], red))
    sg["conv_w"] = lax.dynamic_slice_in_dim(sg["conv_w"], chip * (cfg.conv_dim // 4), cfg.conv_dim // 4, axis=2)
    sg["meta_tokens"] = lax.dynamic_slice_in_dim(sg["meta_tokens"], chip * (cfg.d // 4), cfg.d // 4, axis=1)

    red0 = reduce_start(0, ["w_in"], "l", ex0l, red_vec)
    grads, deltas, new_m, new_v = {}, {}, {}, {}
    dep = red0[4]
    for k in names:
        if k in BIG:
            continue
        w2, g2, m2, v2 = _as2d(wts[k]), _as2d(sg[k]), _as2d(ms[k]), _as2d(vs[k])
        dl, mn, vn = adamw_small(w2, g2, m2, v2, dep, name=f"adamw_{k}")
        grads[k] = sg[k].reshape(wts[k].shape)
        deltas[k], new_m[k], new_v[k] = (t.reshape(wts[k].shape) for t in (dl, mn, vn))

    def view(k, a):
        return jnp.swapaxes(a, 1, 2) if k == "w_in" else a

    def gview(k, g):
        return g.T if k == "w_in" else g

    wv, mv, vv = ({k: view(k, t[k]) for k in BIG} for t in (wts, ms, vs))
    outs = {}
    for k in BIG:
        outs[k] = adamw_layer(wv[k], mv[k], vv[k], gview(k, big1[k]), 1, None, dep, name=f"adamw1_{k}")
        dep = outs[k][1]
    big0 = reduce_finish(0, REST, "e", st["red0e"], dep)
    for k in REST:
        outs[k] = adamw_layer(wv[k], mv[k], vv[k], big0[k], 0, outs[k], dep, name=f"adamw0_{k}")
        dep = outs[k][1]
    big0.update(reduce_finish(0, ["w_in"], "l", red0, dep))
    outs["w_in"] = adamw_layer(wv["w_in"], mv["w_in"], vv["w_in"], gview("w_in", big0["w_in"]), 0, outs["w_in"], dep,
                               name="adamw0_w_in")
    for k in BIG:
        grads[k], deltas[k], new_m[k], new_v[k] = (view(k, t) for t in outs[k])
    return (loss, grad_x, *[grads[k] for k in names], *[deltas[k] for k in names],
            *[new_m[k] for k in names], *[new_v[k] for k in names])


def adamw_small(w, g, m, v, dep, *, name):
    def body(w_ref, g_ref, m_ref, v_ref, dep_ref, d_ref, mo_ref, vo_ref):
        d_ref[...], mo_ref[...], vo_ref[...] = _adam_update(w_ref[...], g_ref[...], m_ref[...], v_ref[...])

    vm = pl.BlockSpec(memory_space=pltpu.VMEM)
    return pl.pallas_call(body, name=name, in_specs=[vm] * 4 + [pl.BlockSpec(memory_space=pl.ANY)], out_specs=[vm] * 3,
                          out_shape=[_sds(w.shape, F32)] * 3, compiler_params=_cp())(w, g, m, v, dep)
```

```python
import functools
import math
from typing import NamedTuple

import numpy as np
import jax
import jax.numpy as jnp
from jax import lax
from jax.experimental import pallas as pl
from jax.experimental.pallas import tpu as pltpu

F32 = jnp.float32
BF16 = jnp.bfloat16
HI = lax.Precision.HIGHEST
EPS = 1e-6
ROPE_THETA = 10000.0
LANE = 128
VMEM_LIMIT = 56 * 1024 * 1024
MASK_VALUE = -1e30
ADAM_LR, ADAM_B1, ADAM_B2, ADAM_EPS, ADAM_WD, ADAM_STEP = 0.001, 0.9, 0.999, 1e-08, 0.01, 10
MESH = pl.DeviceIdType.MESH


class Cfg(NamedTuple):
    d: int = 1024
    seq: int = 2048
    bsz: int = 2
    n_meta: int = 16
    inner: int = 2048
    hd: int = 64
    groups: int = 4
    state: int = 128
    convk: int = 4
    chunk: int = 128
    mh: int = 8
    ql: int = 512
    kvl: int = 256
    nope: int = 128
    rope: int = 64
    vd: int = 128
    ff: int = 4096

    @property
    def heads(self): return self.inner // self.hd
    @property
    def gw(self): return self.inner // self.groups
    @property
    def conv_dim(self): return self.inner + 2 * self.groups * self.state
    @property
    def pad(self): return self.chunk - self.n_meta
    @property
    def lp(self): return self.chunk + self.seq
    @property
    def t(self): return self.bsz * self.lp
    @property
    def nchunks(self): return self.lp // self.chunk
    @property
    def sw(self): return self.ql + self.kvl + 2 * LANE
    @property
    def kt(self): return (self.ql + self.kvl) // LANE
    @property
    def dtt(self): return self.kt + 1
    @property
    def qw(self): return self.mh * 2 * LANE
    @property
    def in_splits(self):
        return [self.inner, self.conv_dim, self.heads, self.ql, self.kvl, self.rope, self.d, self.d]


CFG = Cfg()


def _pick(dim, pref, mult):
    best = None
    for t in range(mult, min(dim, pref) + 1, mult):
        if dim % t == 0:
            best = t
    return best if best is not None else dim


def _cp(**kw):
    return pltpu.CompilerParams(vmem_limit_bytes=VMEM_LIMIT, **kw)


def _sds(shape, dtype):
    return jax.ShapeDtypeStruct(tuple(shape), dtype)


def _silu(x):
    return x * jax.nn.sigmoid(x)


def _dsilu(x):
    s = jax.nn.sigmoid(x)
    return s * (1.0 + x * (1.0 - s))


def _ep_plain(r):
    return (r,)


def _ep_add(r, res):
    return (r + res.astype(F32),)


def _ep_relu2(r):
    rp = jnp.maximum(r, 0.0)
    return (rp * rp,)


def _ep_relu2_grad(r, act):
    return (r * (2.0 * jnp.sqrt(act.astype(F32))),)


MM_VMEM_BUDGET = 44 * 1024 * 1024


def _mm_tiles(m, n, k, a_bytes, b_bytes, io_bytes, ta):
    m_mult, m_cap = (LANE, 1024) if ta else (16, 1088)
    tms = [t for t in range(m_cap, 0, -m_mult) if m % t == 0] or [m]
    tns = [t for t in (1024, 512, 256, 128) if n % t == 0] or [n]
    best = None
    for tm in tms:
        for tn in tns:
            need = 2 * (tm * k * a_bytes + k * tn * b_bytes + tm * tn * io_bytes)
            if need <= MM_VMEM_BUDGET and (best is None or tm * tn > best[0] * best[1]):
                best = (tm, tn)
    if best is None:
        return (_pick(m, 512, m_mult), _pick(n, 512, LANE), _pick(k, 1088 if ta else 1024, 16 if ta else LANE))
    return best[0], best[1], k


def matmul(a, b, *, ta=False, tb=False, out_dtype=F32, add=None, name, tm=None, tn=None, tk=None,
           epilogue=None, extras=(), out_dtypes=None):
    if add is not None:
        epilogue, extras = _ep_add, (add,)
    if epilogue is None:
        epilogue = _ep_plain
    out_dtypes = tuple(out_dtypes) if out_dtypes is not None else (out_dtype,)
    n_ex, n_out = len(extras), len(out_dtypes)
    if ta:
        k_dim, m_dim = a.shape
    else:
        m_dim, k_dim = a.shape
    if tb:
        n_dim, k2 = b.shape
    else:
        k2, n_dim = b.shape
    assert k_dim == k2, (a.shape, b.shape, ta, tb)
    if tm is None and tn is None and tk is None:
        io_bytes = sum(jnp.dtype(e.dtype).itemsize for e in extras) + sum(jnp.dtype(d).itemsize for d in out_dtypes)
        tm, tn, tk = _mm_tiles(m_dim, n_dim, k_dim, jnp.dtype(a.dtype).itemsize, jnp.dtype(b.dtype).itemsize,
                               io_bytes, ta)
    elif ta:
        tm = tm or _pick(m_dim, 1024, LANE)
        tk = tk or _pick(k_dim, 1088, 16)
        tn = tn or _pick(n_dim, 1024, LANE)
    else:
        tm = tm or _pick(m_dim, 1088, 16)
        tk = tk or _pick(k_dim, 1024 if a.dtype == F32 else 2048, LANE)
        tn = tn or _pick(n_dim, 1024, LANE)
    nm, nn, nk = m_dim // tm, n_dim // tn, k_dim // tk
    dn = (((0 if ta else 1,), (1 if tb else 0,)), ((), ()))

    def body(*refs):
        a_ref, b_ref = refs[:2]
        ex_refs = refs[2:2 + n_ex]
        o_refs = refs[2 + n_ex:2 + n_ex + n_out]
        scr = refs[2 + n_ex + n_out:]
        p = lax.dot_general(a_ref[...].astype(BF16), b_ref[...].astype(BF16), dn, preferred_element_type=F32)

        def finish(r):
            outs = epilogue(r, *[e[...] for e in ex_refs])
            for o_ref, val, dt in zip(o_refs, outs, out_dtypes):
                o_ref[...] = val.astype(dt)

        if nk == 1:
            finish(p)
        else:
            acc = scr[0]
            k = pl.program_id(2)

            @pl.when(k == 0)
            def _():
                acc[...] = p

            @pl.when(k > 0)
            def _():
                acc[...] += p

            @pl.when(k == nk - 1)
            def _():
                finish(acc[...])

    a_spec = pl.BlockSpec((tk, tm), lambda i, j, k: (k, i)) if ta else pl.BlockSpec((tm, tk), lambda i, j, k: (i, k))
    b_spec = pl.BlockSpec((tn, tk), lambda i, j, k: (j, k)) if tb else pl.BlockSpec((tk, tn), lambda i, j, k: (k, j))
    o_spec = pl.BlockSpec((tm, tn), lambda i, j, k: (i, j))
    outs = pl.pallas_call(
        body, name=name, grid=(nm, nn, nk), in_specs=[a_spec, b_spec] + [o_spec] * n_ex, out_specs=[o_spec] * n_out,
        out_shape=[_sds((m_dim, n_dim), dt) for dt in out_dtypes],
        scratch_shapes=[pltpu.VMEM((tm, tn), F32)] if nk > 1 else [],
        compiler_params=_cp(dimension_semantics=("parallel", "parallel", "arbitrary")),
    )(a, b, *extras)
    return outs[0] if n_out == 1 else tuple(outs)


def rmsnorm_fwd(x, w, *, cw=None, ci=0, name):
    t = x.shape[0]
    cw = cw or x.shape[1]
    tr = _pick(t, 544, 16)

    def body(x_ref, w_ref, o_ref):
        xv = x_ref[...].astype(F32)
        r = lax.rsqrt(jnp.mean(xv * xv, axis=-1, keepdims=True) + EPS)
        o_ref[...] = (xv * r * w_ref[...]).astype(BF16)

    return pl.pallas_call(
        body, name=name, grid=(t // tr,),
        in_specs=[pl.BlockSpec((tr, cw), lambda i: (i, ci)), pl.BlockSpec((1, cw), lambda i: (0, 0))],
        out_specs=pl.BlockSpec((tr, cw), lambda i: (i, 0)),
        out_shape=_sds((t, cw), BF16), compiler_params=_cp(),
    )(x, w.reshape(1, cw))


def rmsnorm_bwd(dy, x, w, *, cw=None, ci=0, res=None, out_dtype=F32, name):
    t = x.shape[0]
    cw = cw or x.shape[1]
    tr = _pick(t, 544, 16)
    has_res = res is not None

    def body(*refs):
        if has_res:
            dy_ref, x_ref, w_ref, res_ref, dx_ref, dw_ref = refs
        else:
            dy_ref, x_ref, w_ref, dx_ref, dw_ref = refs
        xv = x_ref[...].astype(F32)
        dyv = dy_ref[...].astype(F32)
        r = lax.rsqrt(jnp.mean(xv * xv, axis=-1, keepdims=True) + EPS)
        xh = xv * r
        g = dyv * w_ref[...]
        dx = r * (g - xh * jnp.mean(g * xh, axis=-1, keepdims=True))
        if has_res:
            dx = dx + res_ref[...]
        dx_ref[...] = dx.astype(out_dtype)

        @pl.when(pl.program_id(0) == 0)
        def _():
            dw_ref[...] = jnp.zeros_like(dw_ref)

        dw_ref[...] += jnp.sum(dyv * xh, axis=0, keepdims=True)

    row = pl.BlockSpec((tr, cw), lambda i: (i, 0))
    in_specs = [row, pl.BlockSpec((tr, cw), lambda i: (i, ci)), pl.BlockSpec((1, cw), lambda i: (0, 0))]
    args = [dy, x, w.reshape(1, cw)]
    if has_res:
        in_specs.append(row)
        args.append(res)
    dx, dw = pl.pallas_call(
        body, name=name, grid=(t // tr,), in_specs=in_specs,
        out_specs=[row, pl.BlockSpec((1, cw), lambda i: (0, 0))],
        out_shape=[_sds((t, cw), out_dtype), _sds((1, cw), F32)], compiler_params=_cp(),
    )(*args)
    return dx, dw[0]


def _shift_down(x, s):
    return x if s == 0 else pltpu.roll(x, s, 0)


def _shift_up(x, s):
    return x if s == 0 else pltpu.roll(x, x.shape[0] - s, 0)


def _conv_pre(x, w_ref, b_ref, kk):
    pre = b_ref[...] + jnp.zeros_like(x)
    for k in range(kk):
        pre = pre + w_ref[k:k + 1, :] * _shift_down(x, kk - 1 - k)
    return pre


def conv_fwd(cfg, xbc, w, b, *, name):
    lp, cd, kk = cfg.lp, cfg.conv_dim, cfg.convk
    assert cfg.pad >= kk - 1
    cb = _pick(cd, 512, LANE)

    def body(x_ref, w_ref, b_ref, o_ref, ds_ref):
        pre = _conv_pre(x_ref[...], w_ref, b_ref, kk)
        sg = jax.nn.sigmoid(pre)
        o_ref[...] = pre * sg
        ds_ref[...] = (sg * (1.0 + pre * (1.0 - sg))).astype(BF16)

    blk = pl.BlockSpec((lp, cb), lambda j, bb: (bb, j))
    return pl.pallas_call(
        body, name=name, grid=(cd // cb, cfg.bsz),
        in_specs=[blk, pl.BlockSpec((kk, cb), lambda j, bb: (0, j)), pl.BlockSpec((1, cb), lambda j, bb: (0, j))],
        out_specs=[blk, blk], out_shape=[_sds((cfg.t, cd), F32), _sds((cfg.t, cd), BF16)], compiler_params=_cp(),
    )(xbc, w, b.reshape(1, cd))


def conv_bwd(cfg, xbc, w, dsilu, dxc, *, name):
    lp, cd, kk = cfg.lp, cfg.conv_dim, cfg.convk
    cb = _pick(cd, 512, LANE)

    def body(x_ref, w_ref, s_ref, d_ref, dx_ref, dw_ref, db_ref):
        x = x_ref[...]
        dpre = d_ref[...] * s_ref[...].astype(F32)
        dx = jnp.zeros_like(x)
        dws = []
        for k in range(kk):
            s = kk - 1 - k
            dx = dx + w_ref[k:k + 1, :] * _shift_up(dpre, s)
            dws.append(jnp.sum(dpre * _shift_down(x, s), axis=0, keepdims=True))
        dx_ref[...] = dx.astype(BF16)

        @pl.when(pl.program_id(1) == 0)
        def _():
            dw_ref[...] = jnp.zeros_like(dw_ref)
            db_ref[...] = jnp.zeros_like(db_ref)

        for k in range(kk):
            dw_ref[k:k + 1, :] += dws[k]
        db_ref[...] += jnp.sum(dpre, axis=0, keepdims=True)

    blk = pl.BlockSpec((lp, cb), lambda j, bb: (bb, j))
    wsp = pl.BlockSpec((kk, cb), lambda j, bb: (0, j))
    bsp = pl.BlockSpec((1, cb), lambda j, bb: (0, j))
    dx, dw, db = pl.pallas_call(
        body, name=name, grid=(cd // cb, cfg.bsz),
        in_specs=[blk, wsp, blk, blk], out_specs=[blk, wsp, bsp],
        out_shape=[_sds((cfg.t, cd), BF16), _sds((kk, cd), F32), _sds((1, cd), F32)], compiler_params=_cp(),
    )(xbc, w, dsilu, dxc)
    return dx, dw, db[0]


def _softplus(x):
    return jnp.maximum(x, 0.0) + jnp.log(1.0 + jnp.exp(-jnp.abs(x)))


def _ssd_consts(cfg):
    q = cfg.chunk
    i0 = np.arange(q)[:, None]
    i1 = np.arange(q)[None, :]
    ltri = (i1 <= i0).astype(np.float32)
    rexp = np.zeros((LANE, cfg.inner), np.float32)
    for h in range(cfg.heads):
        rexp[h, h * cfg.hd:(h + 1) * cfg.hd] = 1.0
    return jnp.asarray(ltri), jnp.asarray(rexp)


def _sel_dot(x, m, *, passes=2, left=False, trans=False):
    mb = m.astype(BF16)
    acc, rem = None, x
    for _ in range(passes):
        piece = rem.astype(BF16)
        if not left:
            part = _nn(piece, mb)
        elif trans:
            part = _tn(mb, piece)
        else:
            part = _nn(mb, piece)
        acc = part if acc is None else acc + part
        rem = rem - piece.astype(F32)
    return acc


def _ssd_chunk_common(cfg, raw, bias, avec, c_idx, ltri, rexp):
    q = cfg.chunk
    rows = lax.broadcasted_iota(jnp.int32, (q, LANE), 0)
    live = jnp.logical_or(c_idx > 0, rows >= cfg.pad)
    pre = raw + bias
    dt = jnp.where(live, _softplus(pre), 0.0)
    adt = dt * avec
    cs = _sel_dot(adt, ltri, passes=3, left=True)
    cs_t = cs.T
    cs_last = cs[q - 1:q, :]
    e_in = jnp.exp(cs)
    w0 = jnp.exp(cs_last - cs)
    decay = jnp.exp(cs_last)
    return dict(live=live, pre=pre, dt=dt, adt=adt, cs=cs, cs_t=cs_t, e_in=e_in, w0=w0, decay=decay,
                DT=_sel_dot(dt, rexp), E=_sel_dot(e_in, rexp), W0=_sel_dot(w0, rexp),
                DEC=_sel_dot(jnp.broadcast_to(decay, (8, LANE)), rexp)[0:1, :])


def _tri_masks(q):
    r = lax.broadcasted_iota(jnp.int32, (q, q), 0)
    c = lax.broadcasted_iota(jnp.int32, (q, q), 1)
    return c <= r, r <= c


def _head_l(cq, h, tri, tri_t):
    col = cq["cs"][:, h:h + 1]
    row = cq["cs_t"][h:h + 1, :]
    lmat = jnp.where(tri, jnp.exp(jnp.minimum(col - row, 0.0)), 0.0)
    lmat_t = jnp.where(tri_t, jnp.exp(jnp.minimum(row - col, 0.0)), 0.0)
    return lmat, lmat_t


def _nt(a, b):
    return lax.dot_general(a, b, (((1,), (1,)), ((), ())), preferred_element_type=F32)


def _tn(a, b):
    return lax.dot_general(a, b, (((0,), (0,)), ((), ())), preferred_element_type=F32)


def _nn(a, b):
    return jnp.dot(a, b, preferred_element_type=F32)


def ssd_fwd(cfg, xc, small, dt_bias, avec, dexp, *, name):
    q, inner, st, gw, g_n = cfg.chunk, cfg.inner, cfg.state, cfg.gw, cfg.groups
    nc = cfg.nchunks
    ltri, rexp = _ssd_consts(cfg)
    hpt = LANE // cfg.hd
    tiles_per_group = gw // LANE

    bsz, lp = cfg.bsz, cfg.lp
    bcw = g_n * st

    def body(x_ref, b_ref, c_ref, dt_ref, bias_ref, a_ref, d_ref, ltri_ref, rexp_ref, y_ref, sin_ref, s_scr):
        c_idx = pl.program_id(0)

        @pl.when(c_idx == 0)
        def _():
            s_scr[...] = jnp.zeros_like(s_scr)

        ltri_v = ltri_ref[...]
        tri, tri_t = _tri_masks(q)
        lane = lax.broadcasted_iota(jnp.int32, (q, LANE), 1)
        for bi in range(bsz):
            cq = _ssd_chunk_common(cfg, dt_ref[bi], bias_ref[...], a_ref[...], c_idx, ltri_v, rexp_ref[...])
            xs = x_ref[bi]
            xdt = (xs * cq["DT"]).astype(BF16)
            xw = (xs * cq["DT"] * cq["W0"]).astype(BF16)
            s_in = s_scr[bi]
            sin_ref[bi, 0] = s_in
            for g in range(g_n):
                bg = b_ref[bi, :, g * st:(g + 1) * st].astype(BF16)
                cg = c_ref[bi, :, g * st:(g + 1) * st].astype(BF16)
                gmat = _nt(cg, bg)
                gs = slice(g * gw, (g + 1) * gw)
                y0 = _nn(cg, s_in[:, gs].astype(BF16))
                for tt in range(tiles_per_group):
                    tile = g * tiles_per_group + tt
                    ts = slice(tile * LANE, (tile + 1) * LANE)
                    xt = xdt[:, ts]
                    ms, xh = [], []
                    for hh in range(hpt):
                        lmat, _ = _head_l(cq, tile * hpt + hh, tri, tri_t)
                        ms.append((gmat * lmat).astype(BF16))
                        inhead = jnp.logical_and(lane >= hh * cfg.hd, lane < (hh + 1) * cfg.hd)
                        xh.append(jnp.where(inhead, xt, jnp.zeros_like(xt)))
                    yd = _nn(jnp.concatenate(ms, axis=1), jnp.concatenate(xh, axis=0))
                    y_ref[bi, :, ts] = (yd + y0[:, tt * LANE:(tt + 1) * LANE] * cq["E"][:, ts]
                                        + xs[:, ts] * d_ref[:, ts]).astype(BF16)
                s_scr[bi, :, gs] = s_in[:, gs] * cq["DEC"][:, gs] + _tn(bg, xw[:, gs])

    def rowblk(width, col):
        return pl.BlockSpec((bsz, q, width), lambda c: (0, c, col))

    def const(shape):
        return pl.BlockSpec(shape, lambda c: (0, 0))

    xc3 = xc.reshape(bsz, lp, cfg.conv_dim)
    y, sin = pl.pallas_call(
        body, name=name, grid=(nc,),
        in_specs=[rowblk(inner, 0), rowblk(bcw, inner // bcw), rowblk(bcw, inner // bcw + 1),
                  rowblk(LANE, cfg.dtt), const((1, LANE)), const((1, LANE)), const((1, inner)),
                  const((q, q)), const((LANE, inner))],
        out_specs=[rowblk(inner, 0), pl.BlockSpec((bsz, 1, st, inner), lambda c: (0, c, 0, 0))],
        out_shape=[_sds((bsz, lp, inner), BF16), _sds((bsz, nc, st, inner), F32)],
        scratch_shapes=[pltpu.VMEM((bsz, st, inner), F32)], compiler_params=_cp(),
    )(xc3, xc3, xc3, small.reshape(bsz, lp, cfg.sw), dt_bias, avec, dexp, ltri, rexp)
    return y.reshape(cfg.t, inner), sin.reshape(bsz * nc, st, inner)


def ssd_bwd(cfg, xc, small, dt_bias, avec, dexp, sin, dy, *, name):
    q, inner, st, gw, g_n = cfg.chunk, cfg.inner, cfg.state, cfg.gw, cfg.groups
    nc = cfg.nchunks
    ltri, rexp = _ssd_consts(cfg)
    rexp_t = rexp.T
    hpt = LANE // cfg.hd
    tiles_per_group = gw // LANE
    bcw = g_n * st

    def body(x_ref, b_ref, c_ref, dt_ref, bias_ref, a_ref, d_ref, ltri_ref, rexp_ref, rexpt_ref, sin_ref, dy_ref,
             dx_ref, ddt_ref, dd_ref, da_ref, dbias_ref, ds_scr):
        step = pl.program_id(1)
        c_idx = nc - 1 - step

        @pl.when(step == 0)
        def _():
            ds_scr[...] = jnp.zeros_like(ds_scr)

        @pl.when(jnp.logical_and(step == 0, pl.program_id(0) == 0))
        def _():
            dd_ref[...] = jnp.zeros_like(dd_ref)
            da_ref[...] = jnp.zeros_like(da_ref)
            dbias_ref[...] = jnp.zeros_like(dbias_ref)

        ltri_v = ltri_ref[...]
        tri, tri_t = _tri_masks(q)
        red = _sel_dot
        rexpt = rexpt_ref[...]
        cq = _ssd_chunk_common(cfg, dt_ref[...], bias_ref[...], a_ref[...], c_idx, ltri_v, rexp_ref[...])
        xs = x_ref[...]
        dyv = dy_ref[...].astype(F32)
        s_in = sin_ref[0]
        d_s = ds_scr[...]
        xdt_f = xs * cq["DT"]
        xdt = xdt_f.astype(BF16)
        xw_f = xdt_f * cq["W0"]
        xw = xw_f.astype(BF16)
        lane = lax.broadcasted_iota(jnp.int32, (q, LANE), 1)
        sub = lax.broadcasted_iota(jnp.int32, (LANE, q), 0)

        dd_ref[...] += jnp.sum(dyv * xs, axis=0, keepdims=True)
        dy0 = dyv * cq["E"]
        dcs = jnp.zeros((q, LANE), F32)
        dcs_t = jnp.zeros((LANE, q), F32)
        for g in range(g_n):
            bg_f = b_ref[:, g * st:(g + 1) * st]
            cg_f = c_ref[:, g * st:(g + 1) * st]
            bg = bg_f.astype(BF16)
            cg = cg_f.astype(BF16)
            gs = slice(g * gw, (g + 1) * gw)
            gmat = _nt(cg, bg)
            gmat_t = _nt(bg, cg)
            sing = s_in[:, gs].astype(BF16)
            dsg = d_s[:, gs].astype(BF16)
            y0 = _nn(cg, sing)
            dxw = _nn(bg, dsg)
            d_bg = _nt(xw[:, gs], dsg)
            d_cg = _nt(dy0[:, gs].astype(BF16), sing)
            ds_in_g = _tn(cg, dy0[:, gs].astype(BF16))
            dg = jnp.zeros((q, q), F32)
            dxdt_g = []
            for tt in range(tiles_per_group):
                tile = g * tiles_per_group + tt
                ts = slice(tile * LANE, (tile + 1) * LANE)
                xt = xdt[:, ts]
                dyt = dyv[:, ts]
                dyhs, lmats, mts = [], [], []
                for hh in range(hpt):
                    lmat, lmat_t = _head_l(cq, tile * hpt + hh, tri, tri_t)
                    inhead = jnp.logical_and(lane >= hh * cfg.hd, lane < (hh + 1) * cfg.hd)
                    dyhs.append(jnp.where(inhead, dyt, 0.0).astype(BF16))
                    lmats.append(lmat)
                    mts.append((gmat_t * lmat_t).astype(BF16))
                dy_stack = jnp.concatenate(dyhs, axis=0)
                dm_all = _nt(dy_stack, xt)
                for hh in range(hpt):
                    h = tile * hpt + hh
                    dm = dm_all[hh * q:(hh + 1) * q, :]
                    dg = dg + dm * lmats[hh]
                    qm = dm * gmat * lmats[hh]
                    rs = jnp.sum(qm, axis=1, keepdims=True)
                    csum = jnp.sum(qm, axis=0, keepdims=True)
                    dcs = dcs + jnp.where(lane == h, rs, 0.0)
                    dcs_t = dcs_t + jnp.where(sub == h, csum, 0.0)
                dxdt_g.append(_nn(jnp.concatenate(mts, axis=1), dy_stack))
            dxdt_diag = jnp.concatenate(dxdt_g, axis=1) if len(dxdt_g) > 1 else dxdt_g[0]
            dgb = dg.astype(BF16)
            d_cg = d_cg + _nn(dgb, bg)
            d_bg = d_bg + _tn(dgb, cg)
            dx_ref[:, inner + g * st:inner + (g + 1) * st] = d_bg
            dx_ref[:, inner + bcw + g * st:inner + bcw + (g + 1) * st] = d_cg
            dxdt = dxdt_diag + dxw * cq["W0"][:, gs]
            dx_ref[:, gs] = dyv[:, gs] * d_ref[:, gs] + dxdt * cq["DT"][:, gs]
            rt = rexpt[gs, :]
            dcs = dcs + red(dyv[:, gs] * y0 * cq["E"][:, gs], rt)
            r_w = red(dxw * xw_f[:, gs], rt)
            dcs = dcs - r_w
            dcs_last_g = jnp.sum(r_w, axis=0, keepdims=True)
            ddec = red(jnp.broadcast_to(jnp.sum(d_s[:, gs] * s_in[:, gs], axis=0, keepdims=True), (8, gw)), rt)[0:1, :]
            dcs_last_g = dcs_last_g + ddec * cq["decay"]
            dcs = dcs + jnp.where(lax.broadcasted_iota(jnp.int32, (q, LANE), 0) == q - 1, dcs_last_g, 0.0)
            ddt_part = red(dxdt * xs[:, gs], rt)
            if g == 0:
                ddt = ddt_part
            else:
                ddt = ddt + ddt_part
            ds_scr[:, gs] = d_s[:, gs] * cq["DEC"][:, gs] + ds_in_g
        dcs = dcs - dcs_t.T
        dadt = _sel_dot(dcs, ltri_v, left=True, trans=True)
        ddt = ddt + dadt * a_ref[...]
        da_ref[...] += jnp.sum(dadt * cq["dt"], axis=0, keepdims=True)
        draw = jnp.where(cq["live"], ddt * jax.nn.sigmoid(cq["pre"]), 0.0)
        ddt_ref[...] = draw
        dbias_ref[...] += jnp.sum(draw, axis=0, keepdims=True)

    def rowblk(width, col):
        return pl.BlockSpec((q, width), lambda b, s: (b * nc + nc - 1 - s, col))

    def const(shape):
        return pl.BlockSpec(shape, lambda b, s: (0, 0))

    bcol = inner // bcw
    outs = pl.pallas_call(
        body, name=name, grid=(cfg.bsz, nc),
        in_specs=[rowblk(inner, 0), rowblk(bcw, bcol), rowblk(bcw, bcol + 1), rowblk(LANE, cfg.dtt),
                  const((1, LANE)), const((1, LANE)), const((1, inner)), const((q, q)), const((LANE, inner)),
                  const((inner, LANE)),
                  pl.BlockSpec((1, st, inner), lambda b, s: (b * nc + nc - 1 - s, 0, 0)), rowblk(inner, 0)],
        out_specs=[rowblk(cfg.conv_dim, 0), rowblk(LANE, 0),
                   const((1, inner)), const((1, LANE)), const((1, LANE))],
        out_shape=[_sds((cfg.t, cfg.conv_dim), F32),
                   _sds((cfg.t, LANE), F32), _sds((1, inner), F32), _sds((1, LANE), F32), _sds((1, LANE), F32)],
        scratch_shapes=[pltpu.VMEM((st, inner), F32)], compiler_params=_cp(),
    )(xc, xc, xc, small, dt_bias, avec, dexp, ltri, rexp, rexp_t, sin, dy)
    return outs


def tail_fwd(cfg, y, z, w, *, name):
    t, inner, gw = cfg.t, cfg.inner, cfg.gw
    tr = _pick(t, 272, 16)

    def body(y_ref, z_ref, w_ref, o_ref):
        for g in range(cfg.groups):
            gs = slice(g * gw, (g + 1) * gw)
            yg = y_ref[:, gs].astype(F32) * _silu(z_ref[:, gs].astype(F32))
            r = lax.rsqrt(jnp.mean(yg * yg, axis=-1, keepdims=True) + EPS)
            o_ref[:, gs] = (yg * r * w_ref[:, gs]).astype(BF16)

    row = pl.BlockSpec((tr, inner), lambda i: (i, 0))
    return pl.pallas_call(
        body, name=name, grid=(t // tr,), in_specs=[row, row, pl.BlockSpec((1, inner), lambda i: (0, 0))],
        out_specs=row, out_shape=_sds((t, inner), BF16), compiler_params=_cp(),
    )(y, z, w.reshape(1, inner))


def tail_bwd(cfg, do, y, z, w, *, name):
    t, inner, gw = cfg.t, cfg.inner, cfg.gw
    tr = _pick(t, 272, 16)

    def body(do_ref, y_ref, z_ref, w_ref, dy_ref, dz_ref, dw_ref):
        @pl.when(pl.program_id(0) == 0)
        def _():
            dw_ref[...] = jnp.zeros_like(dw_ref)

        for g in range(cfg.groups):
            gs = slice(g * gw, (g + 1) * gw)
            yv = y_ref[:, gs].astype(F32)
            zv = z_ref[:, gs].astype(F32)
            dov = do_ref[:, gs].astype(F32)
            sz = _silu(zv)
            yg = yv * sz
            r = lax.rsqrt(jnp.mean(yg * yg, axis=-1, keepdims=True) + EPS)
            xh = yg * r
            gg = dov * w_ref[:, gs]
            dyg = r * (gg - xh * jnp.mean(gg * xh, axis=-1, keepdims=True))
            dw_ref[:, gs] += jnp.sum(dov * xh, axis=0, keepdims=True)
            dy_ref[:, gs] = (dyg * sz).astype(BF16)
            dz_ref[:, gs] = (dyg * yv * _dsilu(zv)).astype(BF16)

    row = pl.BlockSpec((tr, inner), lambda i: (i, 0))
    vec = pl.BlockSpec((1, inner), lambda i: (0, 0))
    dy, dz, dw = pl.pallas_call(
        body, name=name, grid=(t // tr,), in_specs=[row, row, row, vec], out_specs=[row, row, vec],
        out_shape=[_sds((t, inner), BF16), _sds((t, inner), BF16), _sds((1, inner), F32)], compiler_params=_cp(),
    )(do, y, z, w.reshape(1, inner))
    return dy, dz, dw[0]


def rope_tables(cfg):
    half = cfg.rope // 2
    pos = np.maximum(np.arange(cfg.lp) - cfg.pad, 0).astype(np.float32)
    inv = ROPE_THETA ** (-jnp.arange(0, cfg.rope, 2, dtype=F32) / cfg.rope)
    ang = jnp.asarray(pos)[:, None] * inv[None, :]
    cos, sin = jnp.cos(ang), jnp.sin(ang)
    zero = jnp.zeros((cfg.lp, LANE - 2 * half), F32)
    zh = jnp.zeros((cfg.lp, half), F32)
    ctab = jnp.concatenate([cos, cos, zero], axis=1)
    s1 = jnp.concatenate([-sin, zh, zero], axis=1)
    s2 = jnp.concatenate([zh, sin, zero], axis=1)
    return ctab, s1, s2


def _rope(x, c, s1, s2, half):
    return x * c + pltpu.roll(x, LANE - half, 1) * s1 + pltpu.roll(x, half, 1) * s2


def _rope_t(dy, c, s1, s2, half):
    return dy * c + pltpu.roll(dy * s1, half, 1) + pltpu.roll(dy * s2, LANE - half, 1)


def _attn_scale(cfg):
    return (cfg.nope + cfg.rope) ** -0.5


def rope_fwd(cfg, qf, small, tabs, *, name):
    t, qw, lp = cfg.t, cfg.qw, cfg.lp
    tr = _pick(lp, 544, 16)
    nrb = lp // tr
    half = cfg.rope // 2
    scale = _attn_scale(cfg)

    def body(q_ref, k_ref, c_ref, s1_ref, s2_ref, qo_ref, ko_ref):
        c, s1, s2 = c_ref[...], s1_ref[...], s2_ref[...]
        for h in range(cfg.mh):
            a = h * 2 * LANE
            qo_ref[:, a:a + LANE] = (q_ref[:, a:a + LANE].astype(F32) * scale).astype(BF16)
            qo_ref[:, a + LANE:a + 2 * LANE] = (
                _rope(q_ref[:, a + LANE:a + 2 * LANE].astype(F32), c, s1, s2, half) * scale).astype(BF16)
        ko_ref[...] = _rope(k_ref[...], c, s1, s2, half).astype(BF16)

    tab = pl.BlockSpec((tr, LANE), lambda i: (i % nrb, 0))
    return pl.pallas_call(
        body, name=name, grid=(t // tr,),
        in_specs=[pl.BlockSpec((tr, qw), lambda i: (i, 0)), pl.BlockSpec((tr, LANE), lambda i: (i, cfg.kt)), tab, tab, tab],
        out_specs=[pl.BlockSpec((tr, qw), lambda i: (i, 0)), pl.BlockSpec((tr, LANE), lambda i: (i, 0))],
        out_shape=[_sds((t, qw), BF16), _sds((t, LANE), BF16)], compiler_params=_cp(),
    )(qf, small, *tabs)


def rope_bwd(cfg, dq, dkpe, tabs, *, name):
    t, qw, lp = cfg.t, cfg.qw, cfg.lp
    tr = _pick(lp, 544, 16)
    nrb = lp // tr
    half = cfg.rope // 2
    scale = _attn_scale(cfg)

    def body(dq_ref, dk_ref, c_ref, s1_ref, s2_ref, qo_ref, ko_ref):
        c, s1, s2 = c_ref[...], s1_ref[...], s2_ref[...]
        for h in range(cfg.mh):
            a = h * 2 * LANE
            qo_ref[:, a:a + LANE] = (dq_ref[:, a:a + LANE].astype(F32) * scale).astype(BF16)
            qo_ref[:, a + LANE:a + 2 * LANE] = _rope_t(
                dq_ref[:, a + LANE:a + 2 * LANE].astype(F32) * scale, c, s1, s2, half).astype(BF16)
        dk = dk_ref[0]
        for h in range(1, cfg.mh):
            dk = dk + dk_ref[h]
        ko_ref[...] = _rope_t(dk, c, s1, s2, half)

    tab = pl.BlockSpec((tr, LANE), lambda i: (i % nrb, 0))
    return pl.pallas_call(
        body, name=name, grid=(t // tr,),
        in_specs=[pl.BlockSpec((tr, qw), lambda i: (i, 0)), pl.BlockSpec((cfg.mh, tr, LANE), lambda i: (0, i, 0)),
                  tab, tab, tab],
        out_specs=[pl.BlockSpec((tr, qw), lambda i: (i, 0)), pl.BlockSpec((tr, LANE), lambda i: (i, 0))],
        out_shape=[_sds((t, qw), BF16), _sds((t, LANE), F32)], compiler_params=_cp(),
    )(dq, dkpe, *tabs)


def _q_blocks(cfg):
    bounds = [0, cfg.chunk] + list(range(cfg.chunk + 256, cfg.lp + 1, 256))
    assert bounds[-1] == cfg.lp, "SEQ must be a multiple of 256"
    return list(zip(bounds[:-1], bounds[1:]))


def _attn_mask(cfg, qs, qe):
    rows = qs + lax.broadcasted_iota(jnp.int32, (qe - qs, qe), 0)
    cols = lax.broadcasted_iota(jnp.int32, (qe - qs, qe), 1)
    return jnp.logical_and(cols <= rows, jnp.logical_or(cols >= cfg.pad, rows < cfg.pad))


def _max_q_block(cfg):
    return max(qe - qs for qs, qe in _q_blocks(cfg))


def _masked_scores(cfg, q, k2, qs, qe, s_scr):
    bq, n = qe - qs, qe
    s_scr[0:bq, 0:n] = _nt(q, k2)
    if qs == 0:
        s_scr[0:bq, 0:n] = jnp.where(_attn_mask(cfg, 0, qe), s_scr[0:bq, 0:n], MASK_VALUE)
    else:
        assert qs >= cfg.chunk and cfg.pad < LANE
        cols = lax.broadcasted_iota(jnp.int32, (bq, LANE), 1)
        s_scr[0:bq, 0:LANE] = jnp.where(cols >= cfg.pad, s_scr[0:bq, 0:LANE], MASK_VALUE)
        r = lax.broadcasted_iota(jnp.int32, (bq, bq), 0)
        c = lax.broadcasted_iota(jnp.int32, (bq, bq), 1)
        s_scr[0:bq, qs:qe] = jnp.where(c <= r, s_scr[0:bq, qs:qe], MASK_VALUE)
    return s_scr[0:bq, 0:n]


def attn_fwd(cfg, qr, kv, kpe, *, name):
    lp, t, mh = cfg.lp, cfg.t, cfg.mh
    blocks = _q_blocks(cfg)

    def body(q_ref, kv_ref, kp_ref, o_ref, l_ref, s_scr):
        for qs, qe in blocks:
            n = qe
            q = q_ref[qs:qe, :]
            k2 = jnp.concatenate([kv_ref[0:n, 0:LANE], kp_ref[0:n, :]], axis=1)
            s = _masked_scores(cfg, q, k2, qs, qe, s_scr)
            m = jnp.max(s, axis=-1, keepdims=True)
            p = jnp.exp(s - m)
            l = jnp.sum(p, axis=-1, keepdims=True)
            o_ref[qs:qe, :] = (_nn(p.astype(BF16), kv_ref[0:n, LANE:2 * LANE]) * (1.0 / l)).astype(BF16)
            l_ref[qs:qe, :] = jnp.broadcast_to(m + jnp.log(l), (qe - qs, LANE))

    hb = pl.BlockSpec((lp, 2 * LANE), lambda b, h: (b, h))
    ob = pl.BlockSpec((lp, LANE), lambda b, h: (b, h))
    return pl.pallas_call(
        body, name=name, grid=(cfg.bsz, mh),
        in_specs=[hb, hb, pl.BlockSpec((lp, LANE), lambda b, h: (b, 0))], out_specs=[ob, ob],
        out_shape=[_sds((t, mh * LANE), BF16), _sds((t, mh * LANE), F32)],
        scratch_shapes=[pltpu.VMEM((_max_q_block(cfg), lp), F32)], compiler_params=_cp(),
    )(qr, kv, kpe)


def attn_bwd(cfg, qr, kv, kpe, o, lse, do, *, name):
    lp, t, mh = cfg.lp, cfg.t, cfg.mh
    blocks = _q_blocks(cfg)

    def body(q_ref, kv_ref, kp_ref, o_ref, l_ref, do_ref, dq_ref, dkv_ref, dkp_ref, dk_acc, dv_acc, s_scr):
        dk_acc[...] = jnp.zeros_like(dk_acc)
        dv_acc[...] = jnp.zeros_like(dv_acc)
        for qs, qe in blocks:
            n = qe
            q = q_ref[qs:qe, :]
            k2 = jnp.concatenate([kv_ref[0:n, 0:LANE], kp_ref[0:n, :]], axis=1)
            dob = do_ref[qs:qe, :].astype(BF16)
            delta = jnp.sum(dob.astype(F32) * o_ref[qs:qe, :].astype(F32), axis=-1, keepdims=True)
            s = _masked_scores(cfg, q, k2, qs, qe, s_scr)
            p = jnp.exp(s - l_ref[qs:qe, 0:1])
            dp = _nt(dob, kv_ref[0:n, LANE:2 * LANE])
            ds = (p * (dp - delta)).astype(BF16)
            dq_ref[qs:qe, :] = _nn(ds, k2).astype(BF16)
            dv_acc[0:n, :] += _tn(p.astype(BF16), dob)
            dk_acc[0:n, :] += _tn(ds, q)
        dkv_ref[:, 0:LANE] = dk_acc[:, 0:LANE].astype(BF16)
        dkv_ref[:, LANE:2 * LANE] = dv_acc[...].astype(BF16)
        dkp_ref[0] = dk_acc[:, LANE:2 * LANE]

    hb = pl.BlockSpec((lp, 2 * LANE), lambda b, h: (b, h))
    ob = pl.BlockSpec((lp, LANE), lambda b, h: (b, h))
    return pl.pallas_call(
        body, name=name, grid=(cfg.bsz, mh),
        in_specs=[hb, hb, pl.BlockSpec((lp, LANE), lambda b, h: (b, 0)), ob, ob, ob],
        out_specs=[hb, hb, pl.BlockSpec((1, lp, LANE), lambda b, h: (h, b, 0))],
        out_shape=[_sds((t, cfg.qw), BF16), _sds((t, mh * 2 * LANE), BF16), _sds((mh, t, LANE), F32)],
        scratch_shapes=[pltpu.VMEM((lp, 2 * LANE), F32), pltpu.VMEM((lp, LANE), F32),
                        pltpu.VMEM((_max_q_block(cfg), lp), F32)], compiler_params=_cp(),
    )(qr, kv, kpe, o, lse, do)


def _live_rows(cfg, tr, shape):
    rows = pl.program_id(1) * tr + lax.broadcasted_iota(jnp.int32, shape, 0)
    return rows >= cfg.pad


def gate_fwd(cfg, ya, yb, g, *, name):
    d, lp = cfg.d, cfg.lp
    tr = _pick(lp, 544, 16)
    nrb = lp // tr

    def body(ya_ref, yb_ref, ga_ref, gb_ref, o_ref):
        f = lambda ref: ref[...].astype(F32)
        mix = jax.nn.sigmoid(f(ga_ref)) * f(ya_ref) + jax.nn.sigmoid(f(gb_ref)) * f(yb_ref)
        o_ref[...] = jnp.where(_live_rows(cfg, tr, mix.shape), mix, 0.0).astype(BF16)

    row = pl.BlockSpec((tr, d), lambda b, j: (b * nrb + j, 0))
    row1 = pl.BlockSpec((tr, d), lambda b, j: (b * nrb + j, 1))
    return pl.pallas_call(
        body, name=name, grid=(cfg.bsz, nrb), in_specs=[row, row, row, row1], out_specs=row,
        out_shape=_sds((cfg.t, d), BF16), compiler_params=_cp(),
    )(ya, yb, g, g)


def gate_bwd(cfg, dmix, ya, yb, g, *, name):
    d, lp = cfg.d, cfg.lp
    tr = _pick(lp, 544, 16)
    nrb = lp // tr

    def body(dm_ref, ya_ref, yb_ref, ga_ref, gb_ref, dya_ref, dyb_ref, dg_ref):
        dm = dm_ref[...].astype(F32)
        dm = jnp.where(_live_rows(cfg, tr, dm.shape), dm, 0.0)
        sa = jax.nn.sigmoid(ga_ref[...].astype(F32))
        sb = jax.nn.sigmoid(gb_ref[...].astype(F32))
        dya_ref[...] = (dm * sa).astype(BF16)
        dyb_ref[...] = (dm * sb).astype(BF16)
        dg_ref[:, 0:d] = (dm * ya_ref[...].astype(F32) * sa * (1.0 - sa)).astype(BF16)
        dg_ref[:, d:2 * d] = (dm * yb_ref[...].astype(F32) * sb * (1.0 - sb)).astype(BF16)

    row = pl.BlockSpec((tr, d), lambda b, j: (b * nrb + j, 0))
    row1 = pl.BlockSpec((tr, d), lambda b, j: (b * nrb + j, 1))
    row2 = pl.BlockSpec((tr, 2 * d), lambda b, j: (b * nrb + j, 0))
    return pl.pallas_call(
        body, name=name, grid=(cfg.bsz, nrb), in_specs=[row, row, row, row, row1], out_specs=[row, row, row2],
        out_shape=[_sds((cfg.t, d), BF16), _sds((cfg.t, d), BF16), _sds((cfg.t, 2 * d), BF16)], compiler_params=_cp(),
    )(dmix, ya, yb, g, g)


def loss_head(cfg, h, target, w, *, name):
    d, q, nc = cfg.d, cfg.chunk, cfg.nchunks
    tpb = cfg.seq // q

    def body(h_ref, t_ref, w_ref, loss_ref, dh_ref, dw_ref):
        j = pl.program_id(1)

        @pl.when(jnp.logical_and(j == 0, pl.program_id(0) == 0))
        def _():
            loss_ref[...] = jnp.zeros_like(loss_ref)
            dw_ref[...] = jnp.zeros_like(dw_ref)

        @pl.when(j == 0)
        def _():
            dh_ref[...] = jnp.zeros_like(dh_ref)

        @pl.when(j > 0)
        def _():
            xv = h_ref[...]
            r = lax.rsqrt(jnp.mean(xv * xv, axis=-1, keepdims=True) + EPS)
            xh = xv * r
            err = xh * w_ref[...] - t_ref[...]
            loss_ref[...] += 0.5 * jnp.sum(jnp.sum(err * err, axis=-1, keepdims=True) / d, axis=0, keepdims=True)
            dy = err * (1.0 / d)
            g = dy * w_ref[...]
            dh_ref[...] = r * (g - xh * jnp.mean(g * xh, axis=-1, keepdims=True))
            dw_ref[...] += jnp.sum(dy * xh, axis=0, keepdims=True)

    row = pl.BlockSpec((q, d), lambda b, j: (b * nc + j, 0))
    loss, dh, dw = pl.pallas_call(
        body, name=name, grid=(cfg.bsz, nc),
        in_specs=[row, pl.BlockSpec((q, d), lambda b, j: (b * tpb + jnp.maximum(j - 1, 0), 0)),
                  pl.BlockSpec((1, d), lambda b, j: (0, 0))],
        out_specs=[pl.BlockSpec((8, LANE), lambda b, j: (0, 0)), row, pl.BlockSpec((1, d), lambda b, j: (0, 0))],
        out_shape=[_sds((8, LANE), F32), _sds((cfg.t, d), F32), _sds((1, d), F32)], compiler_params=_cp(),
    )(h, target, w.reshape(1, d))
    return loss[0, 0], dh, dw[0]


def _rows_tile(r, c):
    return _pick(r, max(8, (1 << 18) // max(c, 1) // 8 * 8), 8)


def _adam_update(w, g, m, v):
    c1 = 1.0 - ADAM_B1 ** ADAM_STEP
    c2 = 1.0 - ADAM_B2 ** ADAM_STEP
    mn = ADAM_B1 * m + (1.0 - ADAM_B1) * g
    vn = ADAM_B2 * v + (1.0 - ADAM_B2) * (g * g)
    delta = -ADAM_LR * ((mn / c1) / (jnp.sqrt(vn / c2) + ADAM_EPS) + ADAM_WD * w)
    return delta, mn, vn


def adamw_layer(w, m, v, g, li, prev, dep, *, name):
    _, r, c = w.shape
    tr = _rows_tile(r, c)

    def body(*refs):
        w_ref, m_ref, v_ref, g_ref = refs[:4]
        go_ref, d_ref, mo_ref, vo_ref = refs[-4:]
        gv = g_ref[...]
        delta, mn, vn = _adam_update(w_ref[0], gv, m_ref[0], v_ref[0])
        go_ref[0] = gv
        d_ref[0] = delta
        mo_ref[0] = mn
        vo_ref[0] = vn

    if tr * c * 4 >= (1 << 16):
        steps = r // tr
        blk3 = pl.BlockSpec((1, tr, c), lambda i: (li, i, 0))
        blk2 = pl.BlockSpec((tr, c), lambda i: (i, 0))
    else:
        tc = _pick(c, max(LANE, (1 << 18) // r // LANE * LANE), LANE)
        steps = c // tc
        blk3 = pl.BlockSpec((1, r, tc), lambda i: (li, 0, i))
        blk2 = pl.BlockSpec((r, tc), lambda i: (0, i))
    anyspec = pl.BlockSpec(memory_space=pl.ANY)
    in_specs = [blk3, blk3, blk3, blk2, anyspec]
    args = [w, m, v, g, dep]
    aliases = {}
    if prev is not None:
        in_specs += [anyspec] * 4
        args += list(prev)
        aliases = {5 + i: i for i in range(4)}
    return pl.pallas_call(
        body, name=name, grid=(steps,), in_specs=in_specs, out_specs=[blk3] * 4,
        out_shape=[_sds(w.shape, F32)] * 4, input_output_aliases=aliases, compiler_params=_cp(),
    )(*args)


def pair_add(g4, other, half, *, name):
    n, _, r, c = g4.shape
    tr = _rows_tile(r, c)

    def body(h_ref, a_ref, b_ref, o_ref):
        o_ref[0] = (a_ref[0, 0].astype(F32) + b_ref[0].astype(F32)).astype(BF16)

    blk = pl.BlockSpec((1, tr, c), lambda j, i, h: (j, i, 0))
    grid_spec = pltpu.PrefetchScalarGridSpec(
        num_scalar_prefetch=1, grid=(n, r // tr),
        in_specs=[pl.BlockSpec((1, 1, tr, c), lambda j, i, h: (j, h[0], i, 0)), blk], out_specs=blk)
    return pl.pallas_call(body, name=name, grid_spec=grid_spec, out_shape=_sds((n, r, c), BF16),
                          compiler_params=_cp())(half, g4, other)


def chip_sum(recv, part, where, *, name):
    n, r, c = recv.shape
    tr = _rows_tile(r, c)

    def body(s_ref, *refs):
        own_ref, o_ref = refs[n], refs[n + 1]
        acc = None
        for j in range(n):
            term = jnp.where(s_ref[0] == j, own_ref[0], refs[j][0]).astype(F32)
            acc = term if acc is None else acc + term
        o_ref[0] = acc

    def slot(j):
        return pl.BlockSpec((1, tr, c), lambda i, s: (jnp.where(s[0] == j, (j + 1) % n, j), i, 0))

    grid_spec = pltpu.PrefetchScalarGridSpec(
        num_scalar_prefetch=1, grid=(r // tr,),
        in_specs=[slot(j) for j in range(n)] + [pl.BlockSpec((1, tr, c), lambda i, s: (s[0], i, 0))],
        out_specs=pl.BlockSpec((1, tr, c), lambda i, s: (s[1], i, 0)))
    return pl.pallas_call(body, name=name, grid_spec=grid_spec, out_shape=_sds((2, r, c), F32),
                          compiler_params=_cp())(where, *([recv] * n), part)


def _coords():
    return lax.axis_index("x"), lax.axis_index("y"), lax.axis_index("c")


def _other_chips(x, y):
    return [(1 - x, y), (x, 1 - y), (1 - x, 1 - y)]


def gather_chips(arrs, *, name):
    n = len(arrs)
    anyspec = pl.BlockSpec(memory_space=pl.ANY)

    def body(*refs):
        ins, outs = refs[:n], refs[n:2 * n]
        send_sems, recv_sems, local_sems = refs[2 * n:]
        x, y, c = _coords()
        me = 2 * x + y
        chips = _other_chips(x, y)
        copies = []
        for k in range(n):
            loc = pltpu.make_async_copy(ins[k], outs[k].at[me], local_sems.at[k])
            loc.start()
            copies.append(loc)
        sends = []
        for k in range(n):
            for j, (px, py) in enumerate(chips):
                cp = pltpu.make_async_remote_copy(
                    src_ref=ins[k], dst_ref=outs[k].at[me], send_sem=send_sems.at[k, j], recv_sem=recv_sems.at[k, j],
                    device_id=(px, py, c), device_id_type=MESH)
                cp.start()
                sends.append(cp)
        for k in range(n):
            for j, (px, py) in enumerate(chips):
                pltpu.make_async_remote_copy(
                    src_ref=ins[k], dst_ref=outs[k].at[2 * px + py], send_sem=send_sems.at[k, j],
                    recv_sem=recv_sems.at[k, j], device_id=(px, py, c), device_id_type=MESH).wait_recv()
        for cp in sends:
            cp.wait_send()
        for cp in copies:
            cp.wait()

    return pl.pallas_call(
        body, name=name, in_specs=[anyspec] * n, out_specs=[anyspec] * n,
        out_shape=[_sds((4,) + a.shape, a.dtype) for a in arrs],
        scratch_shapes=[pltpu.SemaphoreType.DMA((n, 3)), pltpu.SemaphoreType.DMA((n, 3)), pltpu.SemaphoreType.DMA((n,))],
        compiler_params=_cp(has_side_effects=True),
    )(*arrs)


def allreduce_small(vec, after, *, name):
    r, c = vec.shape

    def body(v_ref, after_ref, o_ref, buf, send_sems, recv_sems):
        x, y, cc = _coords()
        me = 4 * x + 2 * y + cc
        buf[me] = v_ref[...]
        sends = []
        flips = [(fx, fy, fc) for fx in (0, 1) for fy in (0, 1) for fc in (0, 1)][1:]
        for j, (fx, fy, fc) in enumerate(flips):
            peer = ((1 - x) if fx else x, (1 - y) if fy else y, (1 - cc) if fc else cc)
            cp = pltpu.make_async_remote_copy(
                src_ref=v_ref, dst_ref=buf.at[me], send_sem=send_sems.at[j], recv_sem=recv_sems.at[j],
                device_id=peer, device_id_type=MESH)
            cp.start()
            sends.append(cp)
        for j, (fx, fy, fc) in enumerate(flips):
            px, py, pc = ((1 - x) if fx else x, (1 - y) if fy else y, (1 - cc) if fc else cc)
            pltpu.make_async_remote_copy(
                src_ref=v_ref, dst_ref=buf.at[4 * px + 2 * py + pc], send_sem=send_sems.at[j],
                recv_sem=recv_sems.at[j], device_id=(px, py, pc), device_id_type=MESH).wait_recv()
        for cp in sends:
            cp.wait_send()
        acc = buf[0]
        for k in range(1, 8):
            acc = acc + buf[k]
        o_ref[...] = acc

    vm = pl.BlockSpec(memory_space=pltpu.VMEM)
    return pl.pallas_call(
        body, name=name, in_specs=[vm, pl.BlockSpec(memory_space=pl.ANY)], out_specs=vm, out_shape=_sds((r, c), F32),
        scratch_shapes=[pltpu.VMEM((8, r, c), F32), pltpu.SemaphoreType.DMA((7,)), pltpu.SemaphoreType.DMA((7,))],
        compiler_params=_cp(has_side_effects=True),
    )(vec, after)


def pair_share(lands, owns, *, name):
    n = len(lands)
    anyspec = pl.BlockSpec(memory_space=pl.ANY)

    def body(*refs):
        ins, own_refs, outs = refs[:n], refs[n:2 * n], refs[2 * n:3 * n]
        send_sems, recv_sems = refs[3 * n:]
        x, y, c = _coords()
        me = 2 * x + y
        sib = (x, y, 1 - c)
        sends = []
        for k in range(n):
            for j, (px, py) in enumerate(_other_chips(x, y)):
                cp = pltpu.make_async_remote_copy(
                    src_ref=ins[k].at[2 * px + py, c], dst_ref=outs[k].at[2 * px + py, c], send_sem=send_sems.at[k, j],
                    recv_sem=recv_sems.at[k, j], device_id=sib, device_id_type=MESH)
                cp.start()
                sends.append(cp)
            cp = pltpu.make_async_remote_copy(
                src_ref=own_refs[k], dst_ref=outs[k].at[me], send_sem=send_sems.at[k, 3], recv_sem=recv_sems.at[k, 3],
                device_id=sib, device_id_type=MESH)
            cp.start()
            sends.append(cp)
        for k in range(n):
            for j, (px, py) in enumerate(_other_chips(x, y)):
                pltpu.make_async_remote_copy(
                    src_ref=ins[k].at[2 * px + py, c], dst_ref=outs[k].at[2 * px + py, 1 - c],
                    send_sem=send_sems.at[k, j], recv_sem=recv_sems.at[k, j], device_id=sib,
                    device_id_type=MESH).wait_recv()
            pltpu.make_async_remote_copy(
                src_ref=own_refs[k], dst_ref=outs[k].at[me], send_sem=send_sems.at[k, 3], recv_sem=recv_sems.at[k, 3],
                device_id=sib, device_id_type=MESH).wait_recv()
        for cp in sends:
            cp.wait_send()

    return pl.pallas_call(
        body, name=name, in_specs=[anyspec] * (2 * n), out_specs=[anyspec] * n,
        out_shape=[_sds(a.shape, a.dtype) for a in lands], input_output_aliases={k: k for k in range(n)},
        scratch_shapes=[pltpu.SemaphoreType.DMA((n, 4)), pltpu.SemaphoreType.DMA((n, 4))],
        compiler_params=_cp(has_side_effects=True),
    )(*lands, *owns)


def pair_fill(arrs, *, name):
    n = len(arrs)
    anyspec = pl.BlockSpec(memory_space=pl.ANY)

    def body(*refs):
        ins, outs = refs[:n], refs[n:2 * n]
        send_sems, recv_sems = refs[2 * n:]
        x, y, c = _coords()
        sends = []
        for k in range(n):
            cp = pltpu.make_async_remote_copy(
                src_ref=ins[k].at[c], dst_ref=outs[k].at[c], send_sem=send_sems.at[k], recv_sem=recv_sems.at[k],
                device_id=(x, y, 1 - c), device_id_type=MESH)
            cp.start()
            sends.append(cp)
        for k in range(n):
            pltpu.make_async_remote_copy(
                src_ref=ins[k].at[c], dst_ref=outs[k].at[1 - c], send_sem=send_sems.at[k], recv_sem=recv_sems.at[k],
                device_id=(x, y, 1 - c), device_id_type=MESH).wait_recv()
        for cp in sends:
            cp.wait_send()

    return pl.pallas_call(
        body, name=name, in_specs=[anyspec] * n, out_specs=[anyspec] * n,
        out_shape=[_sds(a.shape, a.dtype) for a in arrs], input_output_aliases={k: k for k in range(n)},
        scratch_shapes=[pltpu.SemaphoreType.DMA((n,)), pltpu.SemaphoreType.DMA((n,))],
        compiler_params=_cp(has_side_effects=True),
    )(*arrs)


_HBM = pl.BlockSpec(memory_space=pltpu.HBM)
_SEM = pl.BlockSpec(memory_space=pltpu.SEMAPHORE)


_COPIES_PER_ARRAY = {"gather": 3, "scatter": 3, "share": 4, "exchange": 4}


def _ici_copies(kind, srcs, lands, send_sems, recv_sems):
    x, y, c = _coords()
    me = 2 * x + y
    per = _COPIES_PER_ARRAY[kind]
    sends, recvs = [], []
    for k in range(len(srcs)):
        triples = []
        for j, (px, py) in enumerate(_other_chips(x, y)):
            peer = 2 * px + py
            if kind == "gather":
                triples.append((srcs[k].at[c], lands[k].at[me, c], lands[k].at[peer, c], (px, py, c)))
            elif kind == "scatter":
                triples.append((srcs[k].at[peer], lands[k].at[me], lands[k].at[peer], (px, py, c)))
            elif kind == "share":
                triples.append((lands[k].at[peer, c], lands[k].at[peer, c], lands[k].at[peer, 1 - c], (x, y, 1 - c)))
        if kind == "share":
            triples.append((srcs[k], lands[k].at[me], lands[k].at[me], (x, y, 1 - c)))
        if kind == "exchange":
            triples = [(srcs[k].at[j, 1 - c], lands[k].at[j], lands[k].at[j], (x, y, 1 - c)) for j in range(4)]
        for j, (src, there, here, dev) in enumerate(triples):
            sem = per * k + j
            mk = functools.partial(pltpu.make_async_remote_copy, src_ref=src, send_sem=send_sems.at[sem],
                                   recv_sem=recv_sems.at[sem], device_id=dev, device_id_type=MESH)
            sends.append(mk(dst_ref=there))
            recvs.append(mk(dst_ref=here))
    return sends, recvs


def ici_start(kind, srcs, lands, after, *, name):
    n = len(srcs)

    def body(*refs):
        src_refs, land_refs = refs[:n], refs[n:2 * n]
        send_sems, recv_sems = refs[2 * n + 1], refs[2 * n + 2]
        token = refs[-1]
        sends, _ = _ici_copies(kind, src_refs, land_refs, send_sems, recv_sems)
        for cp in sends:
            cp.start()
        token[...] = jnp.zeros_like(token)

    both = list(srcs) + list(lands)
    out = pl.pallas_call(
        body, name=name,
        in_specs=[_HBM] * (2 * n) + [pl.BlockSpec(memory_space=pl.ANY)],
        out_shape=(pltpu.SemaphoreType.DMA((_COPIES_PER_ARRAY[kind] * n,)),
                   pltpu.SemaphoreType.DMA((_COPIES_PER_ARRAY[kind] * n,)),
                   *[pltpu.HBM(a.shape, a.dtype) for a in both], _sds((8, LANE), F32)),
        out_specs=(_SEM, _SEM, *([_HBM] * (2 * n)), pl.BlockSpec(memory_space=pltpu.VMEM)),
        input_output_aliases={i: 2 + i for i in range(2 * n)},
        compiler_params=_cp(has_side_effects=pltpu.SideEffectType.DATAFLOW_SIDE_EFFECTING),
    )(*[pltpu.with_memory_space_constraint(a, pltpu.HBM) for a in both], after)
    return out[0], out[1], list(out[2:2 + n]), list(out[2 + n:2 + 2 * n]), out[-1]


def ici_wait(kind, started, after, *, name):
    send_sems, recv_sems, srcs, lands, _ = started
    n = len(srcs)

    def body(*refs):
        src_refs, land_refs = refs[:n], refs[n:2 * n]
        sends, recvs = _ici_copies(kind, src_refs, land_refs, refs[2 * n], refs[2 * n + 1])
        for cp in sends:
            cp.wait_send()
        for cp in recvs:
            cp.wait_recv()

    both = list(srcs) + list(lands)
    out = pl.pallas_call(
        body, name=name,
        in_specs=[_HBM] * (2 * n) + [_SEM, _SEM, pl.BlockSpec(memory_space=pl.ANY)],
        out_shape=tuple(pltpu.HBM(a.shape, a.dtype) for a in both), out_specs=tuple([_HBM] * (2 * n)),
        input_output_aliases={i: i for i in range(2 * n)},
        compiler_params=_cp(has_side_effects=pltpu.SideEffectType.DATAFLOW_SIDE_EFFECTING),
    )(*both, send_sems, recv_sems, after)
    return list(out[:n]), list(out[n:])


BIG = ["w_in", "w_uq", "w_ukv", "w_branch_ssm", "w_branch_mla", "w_out", "w_mlp_up", "w_mlp_down"]
COL_SHARDED = {"w_in", "w_uq", "w_ukv", "w_mlp_up"}
SMALL_REPL = ["norm_mix_w", "conv_b", "dt_bias", "a_log", "d_skip", "ssm_norm_w", "q_norm_w", "kv_norm_w", "norm_mlp_w"]


def _unshard_layer(name, g):
    _, r, c = g.shape
    if name in COL_SHARDED:
        return jnp.transpose(g, (1, 0, 2)).reshape(r, 4 * c)
    return g.reshape(4 * r, c)


def _to_shards(name, full):
    r, c = full.shape
    if name in COL_SHARDED:
        return jnp.transpose(full.reshape(r, 4, c // 4), (1, 0, 2))
    return full.reshape(4, r // 4, c)


REST = [k for k in BIG if k != "w_in"]


def prep_layer(cfg, w):
    out = {}
    if "w_in" in w:
        sp = np.cumsum(cfg.in_splits)[:-1].tolist()
        z, xbc, dt, cq, ckv, kr, gs, gm = jnp.split(w["w_in"], sp, axis=1)
        zpad = lambda n: jnp.zeros((cfg.d, n), z.dtype)
        out.update(w_z=z, w_xbc=xbc, w_g=jnp.concatenate([gs, gm], axis=1),
                   w_s=jnp.concatenate([cq, ckv, kr, zpad(LANE - cfg.rope), dt, zpad(LANE - cfg.heads)], axis=1))
    if "w_uq" in w:
        out.update(
            w_uq=jnp.pad(w["w_uq"].reshape(cfg.ql, cfg.mh, cfg.nope + cfg.rope),
                         ((0, 0), (0, 0), (0, 2 * LANE - cfg.nope - cfg.rope))).reshape(cfg.ql, cfg.qw),
            w_ukv=w["w_ukv"], w_bs=w["w_branch_ssm"], w_bm=w["w_branch_mla"], w_out=w["w_out"],
            w_up=w["w_mlp_up"], w_down=w["w_mlp_down"])
    return {k: v.astype(BF16) for k, v in out.items()}


def unprep_grads(cfg, g):
    out = {}
    if "w_s" in g:
        ql, kvl = cfg.ql, cfg.kvl
        ds_ = g["w_s"]
        cq, ckv = ds_[:, :ql], ds_[:, ql:ql + kvl]
        kr = ds_[:, ql + kvl:ql + kvl + cfg.rope]
        dt = ds_[:, ql + kvl + LANE:ql + kvl + LANE + cfg.heads]
        out["w_in"] = jnp.concatenate([g["w_z"], g["w_xbc"], dt, cq, ckv, kr, g["w_g"]], axis=1)
    if "w_uq" in g:
        out.update(
            w_uq=g["w_uq"].reshape(cfg.ql, cfg.mh, 2 * LANE)[:, :, :cfg.nope + cfg.rope].reshape(cfg.ql, -1),
            w_ukv=g["w_ukv"], w_branch_ssm=g["w_bs"], w_branch_mla=g["w_bm"],
            w_out=g["w_out"], w_mlp_up=g["w_up"], w_mlp_down=g["w_down"])
    return out


def _hook(hooks, name, arg):
    if hooks and name in hooks:
        return hooks[name](arg)[0, 0]
    return 0.0


def layer_fwd(cfg, h, pw, sm, tabs, li, hooks=None):
    n = lambda s: f"l{li}_{s}"
    u = rmsnorm_fwd(h, sm["norm_mix_w"], name=n("norm_mix"))
    z = matmul(u, pw["w_z"], out_dtype=BF16, name=n("in_z"))
    xbc = matmul(u, pw["w_xbc"], name=n("in_xbc"))
    g = matmul(u, pw["w_g"], out_dtype=BF16, name=n("in_g"))
    small = matmul(u, pw["w_s"], name=n("in_s"))
    xc, dsilu = conv_fwd(cfg, xbc, sm["conv_w"], sm["conv_b"], name=n("conv"))
    dt_bias = sm["dt_bias_p"] + _hook(hooks, "after_conv", xc)
    y, sin = ssd_fwd(cfg, xc, small, dt_bias, sm["avec"], sm["dexp"], name=n("ssd"))
    y_ssm = tail_fwd(cfg, y, z, sm["ssm_norm_w"], name=n("tail"))
    if hooks and "weights" in hooks:
        pw = dict(pw, **hooks["weights"](y_ssm))
    cqn = rmsnorm_fwd(small, sm["q_norm_w"], cw=cfg.ql, ci=0, name=n("q_norm"))
    ckvn = rmsnorm_fwd(small, sm["kv_norm_w"], cw=cfg.kvl, ci=cfg.ql // cfg.kvl, name=n("kv_norm"))
    qf = matmul(cqn, pw["w_uq"], out_dtype=BF16, name=n("uq"))
    kv = matmul(ckvn, pw["w_ukv"], out_dtype=BF16, name=n("ukv"))
    qr, kpe = rope_fwd(cfg, qf, small, tabs, name=n("rope"))
    o, lse = attn_fwd(cfg, qr, kv, kpe, name=n("attn"))
    ya = matmul(y_ssm, pw["w_bs"], out_dtype=BF16, name=n("branch_ssm"))
    yb = matmul(o, pw["w_bm"], out_dtype=BF16, name=n("branch_mla"))
    mixed = gate_fwd(cfg, ya, yb, g, name=n("gate"))
    h1 = matmul(mixed, pw["w_out"], add=h, name=n("out"))
    v = rmsnorm_fwd(h1, sm["norm_mlp_w"] + _hook(hooks, "after_attn", o), name=n("norm_mlp"))
    act = matmul(v, pw["w_up"], name=n("up"), epilogue=_ep_relu2, out_dtypes=(BF16,))
    h2 = matmul(act, pw["w_down"], add=h1, name=n("down"))
    saved = dict(h=h, u=u, z=z, xbc=xbc, g=g, small=small, xc=xc, dsilu=dsilu, y=y, sin=sin, y_ssm=y_ssm, cqn=cqn, ckvn=ckvn,
                 qr=qr, kv=kv, kpe=kpe, o=o, lse=lse, ya=ya, yb=yb, mixed=mixed, h1=h1, v=v, act=act)
    return h2, saved, pw


def layer_bwd(cfg, dh2, pw, sm, tabs, s, li, hooks=None):
    n = lambda t: f"l{li}_b_{t}"
    gw, gs = {}, {}
    wgrad = functools.partial(matmul, ta=True, out_dtype=BF16)
    gw["w_down"] = wgrad(s["act"], dh2, name=n("dw_down"))
    da = matmul(dh2, pw["w_down"], tb=True, name=n("dact"), epilogue=_ep_relu2_grad, extras=(s["act"],),
                out_dtypes=(BF16,))
    gw["w_up"] = wgrad(s["v"], da, name=n("dw_up"))
    dv = matmul(da, pw["w_up"], tb=True, out_dtype=BF16, name=n("dv"))
    dh1, gs["norm_mlp_w"] = rmsnorm_bwd(dv, s["h1"], sm["norm_mlp_w"], res=dh2, name=n("norm_mlp"))
    gw["w_out"] = wgrad(s["mixed"], dh1, name=n("dw_out"))
    dmix = matmul(dh1, pw["w_out"], tb=True, out_dtype=BF16, name=n("dmix"))
    dya, dyb, dg = gate_bwd(cfg, dmix, s["ya"], s["yb"], s["g"], name=n("gate"))
    gw["w_bs"] = wgrad(s["y_ssm"], dya, name=n("dw_bs"))
    gw["w_bm"] = wgrad(s["o"], dyb, name=n("dw_bm"))
    dy_ssm = matmul(dya, pw["w_bs"], tb=True, out_dtype=BF16, name=n("dy_ssm"))
    do = matmul(dyb, pw["w_bm"], tb=True, out_dtype=BF16, name=n("do"))
    dq, dkv, dkpe = attn_bwd(cfg, s["qr"], s["kv"], s["kpe"], s["o"], s["lse"], do, name=n("attn"))
    dqf, dkr = rope_bwd(cfg, dq, dkpe, tabs, name=n("rope"))
    gw["w_uq"] = wgrad(s["cqn"], dqf, name=n("dw_uq"))
    gw["w_ukv"] = wgrad(s["ckvn"], dkv, name=n("dw_ukv"))
    dcqn = matmul(dqf, pw["w_uq"], tb=True, name=n("dcqn"))
    dckvn = matmul(dkv, pw["w_ukv"], tb=True, name=n("dckvn"))
    q_norm_w = sm["q_norm_w"] + _hook(hooks, "after_attn", dqf)
    dcq, gs["q_norm_w"] = rmsnorm_bwd(dcqn, s["small"], q_norm_w, cw=cfg.ql, ci=0, out_dtype=BF16, name=n("q_norm"))
    dckv, gs["kv_norm_w"] = rmsnorm_bwd(dckvn, s["small"], sm["kv_norm_w"], cw=cfg.kvl, ci=cfg.ql // cfg.kvl,
                                        out_dtype=BF16, name=n("kv_norm"))
    ssm_norm_w = sm["ssm_norm_w"] + _hook(hooks, "early", dict(gw))
    dy, dz, gs["ssm_norm_w"] = tail_bwd(cfg, dy_ssm, s["y"], s["z"], ssm_norm_w, name=n("tail"))
    dxc, ddt, ddexp, dav, dbias = ssd_bwd(cfg, s["xc"], s["small"], sm["dt_bias_p"], sm["avec"], sm["dexp"],
                                          s["sin"], dy, name=n("ssd"))
    conv_w = sm["conv_w"] + _hook(hooks, "after_ssd", dxc)
    dxbc, gs["conv_w"], gs["conv_b"] = conv_bwd(cfg, s["xbc"], conv_w, s["dsilu"], dxc, name=n("conv"))
    gs["d_skip"] = ddexp.reshape(cfg.heads, cfg.hd).sum(axis=1)
    gs["a_log"] = (dav[0] * sm["avec"][0])[:cfg.heads]
    gs["dt_bias"] = dbias[0, :cfg.heads]
    dsmall = jnp.concatenate([dcq, dckv, dkr.astype(BF16), ddt.astype(BF16)], axis=1)
    gw["w_z"] = wgrad(s["u"], dz, name=n("dw_z"))
    gw["w_xbc"] = wgrad(s["u"], dxbc, name=n("dw_xbc"))
    gw["w_g"] = wgrad(s["u"], dg, name=n("dw_g"))
    gw["w_s"] = wgrad(s["u"], dsmall, name=n("dw_s"))
    du = matmul(dz, pw["w_z"], tb=True, name=n("du_z"))
    du = matmul(dxbc, pw["w_xbc"], tb=True, add=du, name=n("du_xbc"))
    du = matmul(dg, pw["w_g"], tb=True, add=du, name=n("du_g"))
    du = matmul(dsmall, pw["w_s"], tb=True, add=du, name=n("du_s"))
    dh, gs["norm_mix_w"] = rmsnorm_bwd(du, s["h"], sm["norm_mix_w"], res=dh1, name=n("norm_mix"))
    return dh, gw, gs


def small_params(cfg, p, li):
    pad_l = lambda v: jnp.pad(v, (0, LANE - v.shape[0])).reshape(1, LANE)
    return dict(
        norm_mix_w=p["norm_mix_w"][li], conv_w=p["conv_w"][li], conv_b=p["conv_b"][li],
        dt_bias_p=pad_l(p["dt_bias"][li]), avec=pad_l(-jnp.exp(p["a_log"][li])),
        dexp=jnp.repeat(p["d_skip"][li], cfg.hd).reshape(1, cfg.inner),
        ssm_norm_w=p["ssm_norm_w"][li], q_norm_w=p["q_norm_w"][li], kv_norm_w=p["kv_norm_w"][li],
        norm_mlp_w=p["norm_mlp_w"][li])


def local_step(cfg, x, target, p, depth=2):
    bsz, d = cfg.bsz, cfg.d
    lead = jnp.zeros((bsz, cfg.pad, d), F32)
    meta = jnp.broadcast_to(p["meta_tokens"][None], (bsz, cfg.n_meta, d))
    h = jnp.concatenate([lead, meta, x], axis=1).reshape(cfg.t, d)
    tabs = rope_tables(cfg)
    saved, sms = [], []
    for li in range(depth):
        sm = small_params(cfg, p, li)
        h, s, _ = layer_fwd(cfg, h, p["pw"][li], sm, tabs, li)
        saved.append(s)
        sms.append(sm)
    loss, dh, dfw = loss_head(cfg, h, target.reshape(bsz * cfg.seq, d), p["final_norm_w"], name="loss_head")
    gws, gss = [None] * depth, [None] * depth
    for li in reversed(range(depth)):
        dh, gws[li], gss[li] = layer_bwd(cfg, dh, p["pw"][li], sms[li], tabs, saved[li], li)
    dh = dh.reshape(bsz, cfg.lp, d)
    grad_x = dh[:, cfg.chunk:, :]
    gmeta = jnp.sum(dh[:, cfg.pad:cfg.chunk, :], axis=0)
    return loss, grad_x, gmeta, gws, gss, dfw


def _pack_small(parts):
    flat = jnp.concatenate([a.reshape(-1) for a in parts])
    n = flat.shape[0]
    npad = -n % (8 * LANE)
    return jnp.pad(flat, (0, npad)).reshape(-1, LANE), n


def _unpack_small(vec, shapes):
    flat = vec.reshape(-1)
    out, off = [], 0
    for sh in shapes:
        sz = int(np.prod(sh))
        out.append(flat[off:off + sz].reshape(sh))
        off += sz
    return out


def _as2d(a):
    return a.reshape(-1, a.shape[-1])


def kernel(x, meta_tokens, norm_mix_w, w_in, conv_w, conv_b, dt_bias, a_log, d_skip, ssm_norm_w, q_norm_w, kv_norm_w, w_uq, w_ukv, w_branch_ssm, w_branch_mla, w_out, norm_mlp_w, w_mlp_up, w_mlp_down, final_norm_w, loss_target, m_meta_tokens, m_norm_mix_w, m_w_in, m_conv_w, m_conv_b, m_dt_bias, m_a_log, m_d_skip, m_ssm_norm_w, m_q_norm_w, m_kv_norm_w, m_w_uq, m_w_ukv, m_w_branch_ssm, m_w_branch_mla, m_w_out, m_norm_mlp_w, m_w_mlp_up, m_w_mlp_down, m_final_norm_w, v_meta_tokens, v_norm_mix_w, v_w_in, v_conv_w, v_conv_b, v_dt_bias, v_a_log, v_d_skip, v_ssm_norm_w, v_q_norm_w, v_kv_norm_w, v_w_uq, v_w_ukv, v_w_branch_ssm, v_w_branch_mla, v_w_out, v_norm_mlp_w, v_w_mlp_up, v_w_mlp_down, v_final_norm_w):
    cfg = CFG
    names = ["meta_tokens", "norm_mix_w", "w_in", "conv_w", "conv_b", "dt_bias", "a_log", "d_skip", "ssm_norm_w",
             "q_norm_w", "kv_norm_w", "w_uq", "w_ukv", "w_branch_ssm", "w_branch_mla", "w_out", "norm_mlp_w",
             "w_mlp_up", "w_mlp_down", "final_norm_w"]
    wts = dict(zip(names, [meta_tokens, norm_mix_w, w_in, conv_w, conv_b, dt_bias, a_log, d_skip, ssm_norm_w,
                           q_norm_w, kv_norm_w, w_uq, w_ukv, w_branch_ssm, w_branch_mla, w_out, norm_mlp_w,
                           w_mlp_up, w_mlp_down, final_norm_w]))
    ms = dict(zip(names, [m_meta_tokens, m_norm_mix_w, m_w_in, m_conv_w, m_conv_b, m_dt_bias, m_a_log, m_d_skip,
                          m_ssm_norm_w, m_q_norm_w, m_kv_norm_w, m_w_uq, m_w_ukv, m_w_branch_ssm, m_w_branch_mla,
                          m_w_out, m_norm_mlp_w, m_w_mlp_up, m_w_mlp_down, m_final_norm_w]))
    vs = dict(zip(names, [v_meta_tokens, v_norm_mix_w, v_w_in, v_conv_w, v_conv_b, v_dt_bias, v_a_log, v_d_skip,
                          v_ssm_norm_w, v_q_norm_w, v_kv_norm_w, v_w_uq, v_w_ukv, v_w_branch_ssm, v_w_branch_mla,
                          v_w_out, v_norm_mlp_w, v_w_mlp_up, v_w_mlp_down, v_final_norm_w]))
    cx, cy, cc = _coords()
    chip = 2 * cx + cy

    half1 = jnp.reshape(cc, (1,)).astype(jnp.int32)
    where2 = jnp.stack([chip, cc]).astype(jnp.int32)
    wb = {k: wts[k].astype(BF16) for k in BIG}
    zero_tok = jnp.zeros((8, LANE), F32)

    def halves(a):
        return a.reshape((2, a.shape[0] // 2) + a.shape[1:])

    def gather_start(li, keys, tag, after):
        srcs = [halves(wb[k][li]) for k in keys]
        lands = [lax.empty((4,) + s.shape, BF16) for s in srcs]
        return ici_start("gather", srcs, lands, after, name=f"gather{li}{tag}_start")

    def gather_finish(li, keys, tag, started, after):
        srcs, lands = ici_wait("gather", started, after, name=f"gather{li}{tag}_wait")
        lands = pair_share(lands, srcs, name=f"gather{li}{tag}_share")
        full = {k: _unshard_layer(k, land.reshape((4, 2 * land.shape[2], land.shape[3])))
                for k, land in zip(keys, lands)}
        return prep_layer(cfg, full)

    def gather_mid(li, keys, tag, started, after):
        srcs, lands = ici_wait("gather", started, after, name=f"gather{li}{tag}_wait")
        return ici_start("share", srcs, lands, zero_tok, name=f"gather{li}{tag}_share_start")

    def gather_end(li, keys, tag, shared, after):
        _, lands = ici_wait("share", shared, after, name=f"gather{li}{tag}_share_wait")
        full = {k: _unshard_layer(k, land.reshape((4, 2 * land.shape[2], land.shape[3])))
                for k, land in zip(keys, lands)}
        return prep_layer(cfg, full)

    def exchange_start(li, keys, tag, gw, after):
        ug = unprep_grads(cfg, gw)
        g4 = []
        for k in keys:
            s = _to_shards(k, ug[k])
            g4.append(s.reshape(4, 2, s.shape[1] // 2, s.shape[2]))
        lands = [lax.empty((4,) + a.shape[2:], a.dtype) for a in g4]
        return ici_start("exchange", g4, lands, after, name=f"grad{li}{tag}_exchange_start")

    def reduce_start(li, keys, tag, exchanged, after):
        g4, theirs = ici_wait("exchange", exchanged, after, name=f"grad{li}{tag}_exchange_wait")
        parts = [pair_add(a, b, half1, name=f"grad{li}_pair_add_{k}") for k, a, b in zip(keys, g4, theirs)]
        lands = [lax.empty(q.shape, q.dtype) for q in parts]
        return ici_start("scatter", parts, lands, zero_tok, name=f"grad{li}{tag}_scatter_start")

    def reduce_finish(li, keys, tag, started, after):
        parts, lands = ici_wait("scatter", started, after, name=f"grad{li}{tag}_scatter_wait")
        sums = [chip_sum(rc, pt, where2, name=f"grad{li}_chip_sum_{k}") for k, rc, pt in zip(keys, lands, parts)]
        sums = pair_fill(sums, name=f"grad{li}{tag}_pair_fill")
        return {k: s.reshape(2 * s.shape[1], s.shape[2]) for k, s in zip(keys, sums)}

    gathered = gather_chips([meta_tokens, conv_w], name="gather_small")
    p = dict(wts)
    p["meta_tokens"] = jnp.transpose(gathered[0], (1, 0, 2)).reshape(cfg.n_meta, cfg.d)
    p["conv_w"] = jnp.transpose(gathered[1], (1, 2, 0, 3)).reshape(2, cfg.convk, cfg.conv_dim)

    st0a = gather_start(0, ["w_in"], "a", gathered[0])
    st0b = gather_start(0, REST, "b", st0a[4])
    st1 = gather_start(1, BIG, "", st0b[4])
    pw0 = gather_finish(0, ["w_in"], "a", st0a, st1[4])

    bsz, d = cfg.bsz, cfg.d
    lead = jnp.zeros((bsz, cfg.pad, d), F32)
    meta = jnp.broadcast_to(p["meta_tokens"][None], (bsz, cfg.n_meta, d))
    h0 = jnp.concatenate([lead, meta, x], axis=1).reshape(cfg.t, d)
    tabs = rope_tables(cfg)
    sm0 = small_params(cfg, p, 0)
    st = {}

    def step(key, fn):
        def run(arg):
            st[key] = fn(arg)
            return st[key][4]
        return run

    h1, sv0, pw0 = layer_fwd(cfg, h0, pw0, sm0, tabs, 0, hooks={
        "after_conv": step("share0b", lambda after: gather_mid(0, REST, "b", st0b, after)),
        "weights": lambda after: gather_end(0, REST, "b", st["share0b"], after),
        "after_attn": step("share1", lambda after: gather_mid(1, BIG, "", st1, after))})
    pw1 = gather_end(1, BIG, "", st["share1"], h1)
    sm1 = small_params(cfg, p, 1)
    h2, sv1, _ = layer_fwd(cfg, h1, pw1, sm1, tabs, 1)
    loss, dh, dfw = loss_head(cfg, h2, loss_target.reshape(bsz * cfg.seq, d), final_norm_w, name="loss_head")

    dh, gw1, gs1 = layer_bwd(cfg, dh, pw1, sm1, tabs, sv1, 1)
    ex1 = exchange_start(1, BIG, "", gw1, zero_tok)
    sm0b = dict(sm0)
    sm0b["norm_mlp_w"] = sm0["norm_mlp_w"] + ex1[4][0, 0]
    dh, gw0, gs0 = layer_bwd(cfg, dh, pw0, sm0b, tabs, sv0, 0, hooks={
        "after_attn": step("red1", lambda after: reduce_start(1, BIG, "", ex1, after)),
        "early": step("ex0e", lambda gw: exchange_start(0, REST, "e", gw, zero_tok)),
        "after_ssd": step("red0e", lambda after: reduce_start(0, REST, "e", st["ex0e"], after))})
    dh3 = dh.reshape(bsz, cfg.lp, d)
    grad_x = dh3[:, cfg.chunk:, :]
    gmeta = jnp.sum(dh3[:, cfg.pad:cfg.chunk, :], axis=0)
    big1 = reduce_finish(1, BIG, "", st["red1"], dh)
    ex0l = exchange_start(0, ["w_in"], "l", gw0, big1[BIG[-1]])

    small_names = SMALL_REPL + ["conv_w"]
    parts = [jnp.stack([gs0[k], gs1[k]]) for k in small_names] + [dfw, gmeta, loss.reshape(1)]
    shapes = [a.shape for a in parts]
    vec, _ = _pack_small(parts)
    red_vec = allreduce_small(vec, ex0l[4], name="allreduce_small")
    red = _unpack_small(red_vec, shapes)
    sg = dict(zip(small_names + ["final_norm_w", "meta_tokens"], red))
    loss = red[-1].reshape(())
    sg["conv_w"] = lax.dynamic_slice_in_dim(sg["conv_w"], chip * (cfg.conv_dim // 4), cfg.conv_dim // 4, axis=2)
    sg["meta_tokens"] = lax.dynamic_slice_in_dim(sg["meta_tokens"], chip * (cfg.d // 4), cfg.d // 4, axis=1)

    red0 = reduce_start(0, ["w_in"], "l", ex0l, red_vec)
    grads, deltas, new_m, new_v = {}, {}, {}, {}
    dep = red0[4]
    for k in names:
        if k in BIG:
            continue
        w2, g2, m2, v2 = _as2d(wts[k]), _as2d(sg[k]), _as2d(ms[k]), _as2d(vs[k])
        dl, mn, vn = adamw_small(w2, g2, m2, v2, dep, name=f"adamw_{k}")
        grads[k] = sg[k].reshape(wts[k].shape)
        deltas[k], new_m[k], new_v[k] = (t.reshape(wts[k].shape) for t in (dl, mn, vn))

    def view(k, a):
        return jnp.swapaxes(a, 1, 2) if k == "w_in" else a

    def gview(k, g):
        return g.T if k == "w_in" else g

    wv, mv, vv = ({k: view(k, t[k]) for k in BIG} for t in (wts, ms, vs))
    outs = {}
    for k in BIG:
        outs[k] = adamw_layer(wv[k], mv[k], vv[k], gview(k, big1[k]), 1, None, dep, name=f"adamw1_{k}")
        dep = outs[k][1]
    big0 = reduce_finish(0, REST, "e", st["red0e"], dep)
    for k in REST:
        outs[k] = adamw_layer(wv[k], mv[k], vv[k], big0[k], 0, outs[k], dep, name=f"adamw0_{k}")
        dep = outs[k][1]
    big0.update(reduce_finish(0, ["w_in"], "l", red0, dep))
    outs["w_in"] = adamw_layer(wv["w_in"], mv["w_in"], vv["w_in"], gview("w_in", big0["w_in"]), 0, outs["w_in"], dep,
                               name="adamw0_w_in")
    for k in BIG:
        grads[k], deltas[k], new_m[k], new_v[k] = (view(k, t) for t in outs[k])
    return (loss, grad_x, *[grads[k] for k in names], *[deltas[k] for k in names],
            *[new_m[k] for k in names], *[new_v[k] for k in names])


def adamw_small(w, g, m, v, dep, *, name):
    def body(w_ref, g_ref, m_ref, v_ref, dep_ref, d_ref, mo_ref, vo_ref):
        d_ref[...], mo_ref[...], vo_ref[...] = _adam_update(w_ref[...], g_ref[...], m_ref[...], v_ref[...])

    vm = pl.BlockSpec(memory_space=pltpu.VMEM)
    return pl.pallas_call(body, name=name, in_specs=[vm] * 4 + [pl.BlockSpec(memory_space=pl.ANY)], out_specs=[vm] * 3,
                          out_shape=[_sds(w.shape, F32)] * 3, compiler_params=_cp())(w, g, m, v, dep)
```

```python
import functools
import math
from typing import NamedTuple

import numpy as np
import jax
import jax.numpy as jnp
from jax import lax
from jax.experimental import pallas as pl
from jax.experimental.pallas import tpu as pltpu

F32 = jnp.float32
BF16 = jnp.bfloat16
HI = lax.Precision.HIGHEST
EPS = 1e-6
ROPE_THETA = 10000.0
LANE = 128
VMEM_LIMIT = 56 * 1024 * 1024
MASK_VALUE = -1e30
ADAM_LR, ADAM_B1, ADAM_B2, ADAM_EPS, ADAM_WD, ADAM_STEP = 0.001, 0.9, 0.999, 1e-08, 0.01, 10
MESH = pl.DeviceIdType.MESH


class Cfg(NamedTuple):
    d: int = 1024
    seq: int = 2048
    bsz: int = 2
    n_meta: int = 16
    inner: int = 2048
    hd: int = 64
    groups: int = 4
    state: int = 128
    convk: int = 4
    chunk: int = 128
    mh: int = 8
    ql: int = 512
    kvl: int = 256
    nope: int = 128
    rope: int = 64
    vd: int = 128
    ff: int = 4096

    @property
    def heads(self): return self.inner // self.hd
    @property
    def gw(self): return self.inner // self.groups
    @property
    def conv_dim(self): return self.inner + 2 * self.groups * self.state
    @property
    def pad(self): return self.chunk - self.n_meta
    @property
    def lp(self): return self.chunk + self.seq
    @property
    def t(self): return self.bsz * self.lp
    @property
    def nchunks(self): return self.lp // self.chunk
    @property
    def sw(self): return self.ql + self.kvl + 2 * LANE
    @property
    def kt(self): return (self.ql + self.kvl) // LANE
    @property
    def dtt(self): return self.kt + 1
    @property
    def qw(self): return self.mh * 2 * LANE
    @property
    def in_splits(self):
        return [self.inner, self.conv_dim, self.heads, self.ql, self.kvl, self.rope, self.d, self.d]


CFG = Cfg()


def _pick(dim, pref, mult):
    best = None
    for t in range(mult, min(dim, pref) + 1, mult):
        if dim % t == 0:
            best = t
    return best if best is not None else dim


def _cp(**kw):
    return pltpu.CompilerParams(vmem_limit_bytes=VMEM_LIMIT, **kw)


def _sds(shape, dtype):
    return jax.ShapeDtypeStruct(tuple(shape), dtype)


def _silu(x):
    return x * jax.nn.sigmoid(x)


def _dsilu(x):
    s = jax.nn.sigmoid(x)
    return s * (1.0 + x * (1.0 - s))


def _ep_plain(r):
    return (r,)


def _ep_add(r, res):
    return (r + res.astype(F32),)


def _ep_relu2(r):
    rp = jnp.maximum(r, 0.0)
    return r, rp * rp


def _ep_relu2_grad(r, a):
    return (r * (2.0 * jnp.maximum(a.astype(F32), 0.0)),)


MM_VMEM_BUDGET = 44 * 1024 * 1024


def _mm_tiles(m, n, k, a_bytes, b_bytes, io_bytes, ta):
    m_mult, m_cap = (LANE, 1024) if ta else (16, 1088)
    tms = [t for t in range(m_cap, 0, -m_mult) if m % t == 0] or [m]
    tns = [t for t in (1024, 512, 256, 128) if n % t == 0] or [n]
    best = None
    for tm in tms:
        for tn in tns:
            need = 2 * (tm * k * a_bytes + k * tn * b_bytes + tm * tn * io_bytes)
            if need <= MM_VMEM_BUDGET and (best is None or tm * tn > best[0] * best[1]):
                best = (tm, tn)
    if best is None:
        return (_pick(m, 512, m_mult), _pick(n, 512, LANE), _pick(k, 1088 if ta else 1024, 16 if ta else LANE))
    return best[0], best[1], k


def matmul(a, b, *, ta=False, tb=False, out_dtype=F32, add=None, name, tm=None, tn=None, tk=None,
           epilogue=None, extras=(), out_dtypes=None):
    if add is not None:
        epilogue, extras = _ep_add, (add,)
    if epilogue is None:
        epilogue = _ep_plain
    out_dtypes = tuple(out_dtypes) if out_dtypes is not None else (out_dtype,)
    n_ex, n_out = len(extras), len(out_dtypes)
    if ta:
        k_dim, m_dim = a.shape
    else:
        m_dim, k_dim = a.shape
    if tb:
        n_dim, k2 = b.shape
    else:
        k2, n_dim = b.shape
    assert k_dim == k2, (a.shape, b.shape, ta, tb)
    if tm is None and tn is None and tk is None:
        io_bytes = sum(jnp.dtype(e.dtype).itemsize for e in extras) + sum(jnp.dtype(d).itemsize for d in out_dtypes)
        tm, tn, tk = _mm_tiles(m_dim, n_dim, k_dim, jnp.dtype(a.dtype).itemsize, jnp.dtype(b.dtype).itemsize,
                               io_bytes, ta)
    elif ta:
        tm = tm or _pick(m_dim, 1024, LANE)
        tk = tk or _pick(k_dim, 1088, 16)
        tn = tn or _pick(n_dim, 1024, LANE)
    else:
        tm = tm or _pick(m_dim, 1088, 16)
        tk = tk or _pick(k_dim, 1024 if a.dtype == F32 else 2048, LANE)
        tn = tn or _pick(n_dim, 1024, LANE)
    nm, nn, nk = m_dim // tm, n_dim // tn, k_dim // tk
    dn = (((0 if ta else 1,), (1 if tb else 0,)), ((), ()))

    def body(*refs):
        a_ref, b_ref = refs[:2]
        ex_refs = refs[2:2 + n_ex]
        o_refs = refs[2 + n_ex:2 + n_ex + n_out]
        scr = refs[2 + n_ex + n_out:]
        p = lax.dot_general(a_ref[...].astype(BF16), b_ref[...].astype(BF16), dn, preferred_element_type=F32)

        def finish(r):
            outs = epilogue(r, *[e[...] for e in ex_refs])
            for o_ref, val, dt in zip(o_refs, outs, out_dtypes):
                o_ref[...] = val.astype(dt)

        if nk == 1:
            finish(p)
        else:
            acc = scr[0]
            k = pl.program_id(2)

            @pl.when(k == 0)
            def _():
                acc[...] = p

            @pl.when(k > 0)
            def _():
                acc[...] += p

            @pl.when(k == nk - 1)
            def _():
                finish(acc[...])

    a_spec = pl.BlockSpec((tk, tm), lambda i, j, k: (k, i)) if ta else pl.BlockSpec((tm, tk), lambda i, j, k: (i, k))
    b_spec = pl.BlockSpec((tn, tk), lambda i, j, k: (j, k)) if tb else pl.BlockSpec((tk, tn), lambda i, j, k: (k, j))
    o_spec = pl.BlockSpec((tm, tn), lambda i, j, k: (i, j))
    outs = pl.pallas_call(
        body, name=name, grid=(nm, nn, nk), in_specs=[a_spec, b_spec] + [o_spec] * n_ex, out_specs=[o_spec] * n_out,
        out_shape=[_sds((m_dim, n_dim), dt) for dt in out_dtypes],
        scratch_shapes=[pltpu.VMEM((tm, tn), F32)] if nk > 1 else [],
        compiler_params=_cp(dimension_semantics=("parallel", "parallel", "arbitrary")),
    )(a, b, *extras)
    return outs[0] if n_out == 1 else tuple(outs)


def matmul_nt_sum(as_, bs_, *, out_dtype=F32, name):
    m, n = as_[0].shape[0], bs_[0].shape[0]
    ks = [a.shape[1] for a in as_]
    assert [b.shape[1] for b in bs_] == ks
    ksum, cnt = sum(ks), len(ks)
    best = None
    for tm in [t for t in range(1088, 0, -16) if m % t == 0]:
        for tn in [t for t in (1024, 512, 256, 128) if n % t == 0]:
            need = 2 * (tm * ksum * 2 + tn * ksum * 2 + tm * tn * jnp.dtype(out_dtype).itemsize)
            if need <= MM_VMEM_BUDGET and (best is None or tm * tn > best[0] * best[1]):
                best = (tm, tn)
    tm, tn = best

    def body(*refs):
        a_refs, b_refs, o_ref = refs[:cnt], refs[cnt:2 * cnt], refs[2 * cnt]
        acc = None
        for a_ref, b_ref in zip(a_refs, b_refs):
            p = _nt(a_ref[...].astype(BF16), b_ref[...].astype(BF16))
            acc = p if acc is None else acc + p
        o_ref[...] = acc.astype(out_dtype)

    return pl.pallas_call(
        body, name=name, grid=(n // tn, m // tm),
        in_specs=[pl.BlockSpec((tm, k), lambda j, i: (i, 0)) for k in ks]
        + [pl.BlockSpec((tn, k), lambda j, i: (j, 0)) for k in ks],
        out_specs=pl.BlockSpec((tm, tn), lambda j, i: (i, j)), out_shape=_sds((m, n), out_dtype),
        compiler_params=_cp(dimension_semantics=("parallel", "parallel")),
    )(*as_, *bs_)


def rmsnorm_fwd(x, w, *, cw=None, ci=0, name):
    t = x.shape[0]
    cw = cw or x.shape[1]
    tr = _pick(t, 544, 16)

    def body(x_ref, w_ref, o_ref):
        xv = x_ref[...].astype(F32)
        r = lax.rsqrt(jnp.mean(xv * xv, axis=-1, keepdims=True) + EPS)
        o_ref[...] = (xv * r * w_ref[...]).astype(BF16)

    return pl.pallas_call(
        body, name=name, grid=(t // tr,),
        in_specs=[pl.BlockSpec((tr, cw), lambda i: (i, ci)), pl.BlockSpec((1, cw), lambda i: (0, 0))],
        out_specs=pl.BlockSpec((tr, cw), lambda i: (i, 0)),
        out_shape=_sds((t, cw), BF16), compiler_params=_cp(),
    )(x, w.reshape(1, cw))


def rmsnorm_bwd(dy, x, w, *, cw=None, ci=0, res=None, out_dtype=F32, name):
    t = x.shape[0]
    cw = cw or x.shape[1]
    tr = _pick(t, 544, 16)
    has_res = res is not None

    def body(*refs):
        if has_res:
            dy_ref, x_ref, w_ref, res_ref, dx_ref, dw_ref = refs
        else:
            dy_ref, x_ref, w_ref, dx_ref, dw_ref = refs
        xv = x_ref[...].astype(F32)
        dyv = dy_ref[...].astype(F32)
        r = lax.rsqrt(jnp.mean(xv * xv, axis=-1, keepdims=True) + EPS)
        xh = xv * r
        g = dyv * w_ref[...]
        dx = r * (g - xh * jnp.mean(g * xh, axis=-1, keepdims=True))
        if has_res:
            dx = dx + res_ref[...]
        dx_ref[...] = dx.astype(out_dtype)

        @pl.when(pl.program_id(0) == 0)
        def _():
            dw_ref[...] = jnp.zeros_like(dw_ref)

        dw_ref[...] += jnp.sum(dyv * xh, axis=0, keepdims=True)

    row = pl.BlockSpec((tr, cw), lambda i: (i, 0))
    in_specs = [row, pl.BlockSpec((tr, cw), lambda i: (i, ci)), pl.BlockSpec((1, cw), lambda i: (0, 0))]
    args = [dy, x, w.reshape(1, cw)]
    if has_res:
        in_specs.append(row)
        args.append(res)
    dx, dw = pl.pallas_call(
        body, name=name, grid=(t // tr,), in_specs=in_specs,
        out_specs=[row, pl.BlockSpec((1, cw), lambda i: (0, 0))],
        out_shape=[_sds((t, cw), out_dtype), _sds((1, cw), F32)], compiler_params=_cp(),
    )(*args)
    return dx, dw[0]


def _shift_down(x, s):
    return x if s == 0 else pltpu.roll(x, s, 0)


def _shift_up(x, s):
    return x if s == 0 else pltpu.roll(x, x.shape[0] - s, 0)


def _conv_pre(x, w_ref, b_ref, kk):
    pre = b_ref[...] + jnp.zeros_like(x)
    for k in range(kk):
        pre = pre + w_ref[k:k + 1, :] * _shift_down(x, kk - 1 - k)
    return pre


def conv_fwd(cfg, xbc, w, b, *, name):
    lp, cd, kk = cfg.lp, cfg.conv_dim, cfg.convk
    assert cfg.pad >= kk - 1
    cb = _pick(cd, 512, LANE)

    def body(x_ref, w_ref, b_ref, o_ref, ds_ref):
        pre = _conv_pre(x_ref[...], w_ref, b_ref, kk)
        sg = jax.nn.sigmoid(pre)
        o_ref[...] = pre * sg
        ds_ref[...] = (sg * (1.0 + pre * (1.0 - sg))).astype(BF16)

    blk = pl.BlockSpec((lp, cb), lambda j, bb: (bb, j))
    return pl.pallas_call(
        body, name=name, grid=(cd // cb, cfg.bsz),
        in_specs=[blk, pl.BlockSpec((kk, cb), lambda j, bb: (0, j)), pl.BlockSpec((1, cb), lambda j, bb: (0, j))],
        out_specs=[blk, blk], out_shape=[_sds((cfg.t, cd), F32), _sds((cfg.t, cd), BF16)], compiler_params=_cp(),
    )(xbc, w, b.reshape(1, cd))


def conv_bwd(cfg, xbc, w, dsilu, dxc, *, name):
    lp, cd, kk = cfg.lp, cfg.conv_dim, cfg.convk
    cb = _pick(cd, 512, LANE)

    def body(x_ref, w_ref, s_ref, d_ref, dx_ref, dw_ref, db_ref):
        x = x_ref[...]
        dpre = d_ref[...] * s_ref[...].astype(F32)
        dx = jnp.zeros_like(x)
        dws = []
        for k in range(kk):
            s = kk - 1 - k
            dx = dx + w_ref[k:k + 1, :] * _shift_up(dpre, s)
            dws.append(jnp.sum(dpre * _shift_down(x, s), axis=0, keepdims=True))
        dx_ref[...] = dx.astype(BF16)

        @pl.when(pl.program_id(1) == 0)
        def _():
            dw_ref[...] = jnp.zeros_like(dw_ref)
            db_ref[...] = jnp.zeros_like(db_ref)

        for k in range(kk):
            dw_ref[k:k + 1, :] += dws[k]
        db_ref[...] += jnp.sum(dpre, axis=0, keepdims=True)

    blk = pl.BlockSpec((lp, cb), lambda j, bb: (bb, j))
    wsp = pl.BlockSpec((kk, cb), lambda j, bb: (0, j))
    bsp = pl.BlockSpec((1, cb), lambda j, bb: (0, j))
    dx, dw, db = pl.pallas_call(
        body, name=name, grid=(cd // cb, cfg.bsz),
        in_specs=[blk, wsp, blk, blk], out_specs=[blk, wsp, bsp],
        out_shape=[_sds((cfg.t, cd), BF16), _sds((kk, cd), F32), _sds((1, cd), F32)], compiler_params=_cp(),
    )(xbc, w, dsilu, dxc)
    return dx, dw, db[0]


def _softplus(x):
    return jnp.maximum(x, 0.0) + jnp.log(1.0 + jnp.exp(-jnp.abs(x)))


def _ssd_consts(cfg):
    q = cfg.chunk
    i0 = np.arange(q)[:, None]
    i1 = np.arange(q)[None, :]
    ltri = (i1 <= i0).astype(np.float32)
    rexp = np.zeros((LANE, cfg.inner), np.float32)
    for h in range(cfg.heads):
        rexp[h, h * cfg.hd:(h + 1) * cfg.hd] = 1.0
    return jnp.asarray(ltri), jnp.asarray(rexp)


def _sel_dot(x, m, *, passes=2, left=False, trans=False):
    mb = m.astype(BF16)
    acc, rem = None, x
    for _ in range(passes):
        piece = rem.astype(BF16)
        if not left:
            part = _nn(piece, mb)
        elif trans:
            part = _tn(mb, piece)
        else:
            part = _nn(mb, piece)
        acc = part if acc is None else acc + part
        rem = rem - piece.astype(F32)
    return acc


def _ssd_chunk_common(cfg, raw, bias, avec, c_idx, ltri, rexp):
    q = cfg.chunk
    rows = lax.broadcasted_iota(jnp.int32, (q, LANE), 0)
    live = jnp.logical_or(c_idx > 0, rows >= cfg.pad)
    pre = raw + bias
    dt = jnp.where(live, _softplus(pre), 0.0)
    adt = dt * avec
    cs = _sel_dot(adt, ltri, passes=3, left=True)
    cs_t = cs.T
    cs_last = cs[q - 1:q, :]
    e_in = jnp.exp(cs)
    w0 = jnp.exp(cs_last - cs)
    decay = jnp.exp(cs_last)
    return dict(live=live, pre=pre, dt=dt, adt=adt, cs=cs, cs_t=cs_t, e_in=e_in, w0=w0, decay=decay,
                DT=_sel_dot(dt, rexp), E=_sel_dot(e_in, rexp), W0=_sel_dot(w0, rexp),
                DEC=_sel_dot(jnp.broadcast_to(decay, (8, LANE)), rexp)[0:1, :])


def _tri_masks(q):
    r = lax.broadcasted_iota(jnp.int32, (q, q), 0)
    c = lax.broadcasted_iota(jnp.int32, (q, q), 1)
    return c <= r, r <= c


def _head_l(cq, h, tri, tri_t):
    col = cq["cs"][:, h:h + 1]
    row = cq["cs_t"][h:h + 1, :]
    lmat = jnp.where(tri, jnp.exp(jnp.minimum(col - row, 0.0)), 0.0)
    lmat_t = jnp.where(tri_t, jnp.exp(jnp.minimum(row - col, 0.0)), 0.0)
    return lmat, lmat_t


def _nt(a, b):
    return lax.dot_general(a, b, (((1,), (1,)), ((), ())), preferred_element_type=F32)


def _tn(a, b):
    return lax.dot_general(a, b, (((0,), (0,)), ((), ())), preferred_element_type=F32)


def _nn(a, b):
    return jnp.dot(a, b, preferred_element_type=F32)


def ssd_fwd(cfg, xc, small, dt_bias, avec, dexp, *, name):
    q, inner, st, gw, g_n = cfg.chunk, cfg.inner, cfg.state, cfg.gw, cfg.groups
    nc = cfg.nchunks
    ltri, rexp = _ssd_consts(cfg)
    hpt = LANE // cfg.hd
    tiles_per_group = gw // LANE

    bsz, lp = cfg.bsz, cfg.lp
    bcw = g_n * st

    def body(x_ref, b_ref, c_ref, dt_ref, bias_ref, a_ref, d_ref, ltri_ref, rexp_ref, y_ref, sin_ref, s_scr):
        c_idx = pl.program_id(0)

        @pl.when(c_idx == 0)
        def _():
            s_scr[...] = jnp.zeros_like(s_scr)

        ltri_v = ltri_ref[...]
        tri, tri_t = _tri_masks(q)
        lane = lax.broadcasted_iota(jnp.int32, (q, LANE), 1)
        for bi in range(bsz):
            cq = _ssd_chunk_common(cfg, dt_ref[bi], bias_ref[...], a_ref[...], c_idx, ltri_v, rexp_ref[...])
            xs = x_ref[bi]
            xdt = (xs * cq["DT"]).astype(BF16)
            xw = (xs * cq["DT"] * cq["W0"]).astype(BF16)
            s_in = s_scr[bi]
            sin_ref[bi, 0] = s_in
            for g in range(g_n):
                bg = b_ref[bi, :, g * st:(g + 1) * st].astype(BF16)
                cg = c_ref[bi, :, g * st:(g + 1) * st].astype(BF16)
                gmat = _nt(cg, bg)
                gs = slice(g * gw, (g + 1) * gw)
                y0 = _nn(cg, s_in[:, gs].astype(BF16))
                for tt in range(tiles_per_group):
                    tile = g * tiles_per_group + tt
                    ts = slice(tile * LANE, (tile + 1) * LANE)
                    xt = xdt[:, ts]
                    ms, xh = [], []
                    for hh in range(hpt):
                        lmat, _ = _head_l(cq, tile * hpt + hh, tri, tri_t)
                        ms.append((gmat * lmat).astype(BF16))
                        inhead = jnp.logical_and(lane >= hh * cfg.hd, lane < (hh + 1) * cfg.hd)
                        xh.append(jnp.where(inhead, xt, jnp.zeros_like(xt)))
                    yd = _nn(jnp.concatenate(ms, axis=1), jnp.concatenate(xh, axis=0))
                    y_ref[bi, :, ts] = (yd + y0[:, tt * LANE:(tt + 1) * LANE] * cq["E"][:, ts]
                                        + xs[:, ts] * d_ref[:, ts]).astype(BF16)
                s_scr[bi, :, gs] = s_in[:, gs] * cq["DEC"][:, gs] + _tn(bg, xw[:, gs])

    def rowblk(width, col):
        return pl.BlockSpec((bsz, q, width), lambda c: (0, c, col))

    def const(shape):
        return pl.BlockSpec(shape, lambda c: (0, 0))

    xc3 = xc.reshape(bsz, lp, cfg.conv_dim)
    y, sin = pl.pallas_call(
        body, name=name, grid=(nc,),
        in_specs=[rowblk(inner, 0), rowblk(bcw, inner // bcw), rowblk(bcw, inner // bcw + 1),
                  rowblk(LANE, cfg.dtt), const((1, LANE)), const((1, LANE)), const((1, inner)),
                  const((q, q)), const((LANE, inner))],
        out_specs=[rowblk(inner, 0), pl.BlockSpec((bsz, 1, st, inner), lambda c: (0, c, 0, 0))],
        out_shape=[_sds((bsz, lp, inner), BF16), _sds((bsz, nc, st, inner), F32)],
        scratch_shapes=[pltpu.VMEM((bsz, st, inner), F32)], compiler_params=_cp(),
    )(xc3, xc3, xc3, small.reshape(bsz, lp, cfg.sw), dt_bias, avec, dexp, ltri, rexp)
    return y.reshape(cfg.t, inner), sin.reshape(bsz * nc, st, inner)


def ssd_bwd(cfg, xc, small, dt_bias, avec, dexp, sin, dy, *, name):
    q, inner, st, gw, g_n = cfg.chunk, cfg.inner, cfg.state, cfg.gw, cfg.groups
    nc = cfg.nchunks
    ltri, rexp = _ssd_consts(cfg)
    rexp_t = rexp.T
    hpt = LANE // cfg.hd
    tiles_per_group = gw // LANE
    bcw = g_n * st

    def body(x_ref, b_ref, c_ref, dt_ref, bias_ref, a_ref, d_ref, ltri_ref, rexp_ref, rexpt_ref, sin_ref, dy_ref,
             dx_ref, ddt_ref, dd_ref, da_ref, dbias_ref, ds_scr):
        step = pl.program_id(1)
        c_idx = nc - 1 - step

        @pl.when(step == 0)
        def _():
            ds_scr[...] = jnp.zeros_like(ds_scr)

        @pl.when(jnp.logical_and(step == 0, pl.program_id(0) == 0))
        def _():
            dd_ref[...] = jnp.zeros_like(dd_ref)
            da_ref[...] = jnp.zeros_like(da_ref)
            dbias_ref[...] = jnp.zeros_like(dbias_ref)

        ltri_v = ltri_ref[...]
        tri, tri_t = _tri_masks(q)
        red = _sel_dot
        rexpt = rexpt_ref[...]
        cq = _ssd_chunk_common(cfg, dt_ref[...], bias_ref[...], a_ref[...], c_idx, ltri_v, rexp_ref[...])
        xs = x_ref[...]
        dyv = dy_ref[...].astype(F32)
        s_in = sin_ref[0]
        d_s = ds_scr[...]
        xdt_f = xs * cq["DT"]
        xdt = xdt_f.astype(BF16)
        xw_f = xdt_f * cq["W0"]
        xw = xw_f.astype(BF16)
        lane = lax.broadcasted_iota(jnp.int32, (q, LANE), 1)
        sub = lax.broadcasted_iota(jnp.int32, (LANE, q), 0)

        dd_ref[...] += jnp.sum(dyv * xs, axis=0, keepdims=True)
        dy0 = dyv * cq["E"]
        dcs = jnp.zeros((q, LANE), F32)
        dcs_t = jnp.zeros((LANE, q), F32)
        for g in range(g_n):
            bg_f = b_ref[:, g * st:(g + 1) * st]
            cg_f = c_ref[:, g * st:(g + 1) * st]
            bg = bg_f.astype(BF16)
            cg = cg_f.astype(BF16)
            gs = slice(g * gw, (g + 1) * gw)
            gmat = _nt(cg, bg)
            gmat_t = _nt(bg, cg)
            sing = s_in[:, gs].astype(BF16)
            dsg = d_s[:, gs].astype(BF16)
            y0 = _nn(cg, sing)
            dxw = _nn(bg, dsg)
            d_bg = _nt(xw[:, gs], dsg)
            d_cg = _nt(dy0[:, gs].astype(BF16), sing)
            ds_in_g = _tn(cg, dy0[:, gs].astype(BF16))
            dg = jnp.zeros((q, q), F32)
            dxdt_g = []
            for tt in range(tiles_per_group):
                tile = g * tiles_per_group + tt
                ts = slice(tile * LANE, (tile + 1) * LANE)
                xt = xdt[:, ts]
                dyt = dyv[:, ts]
                dyhs, lmats, mts = [], [], []
                for hh in range(hpt):
                    lmat, lmat_t = _head_l(cq, tile * hpt + hh, tri, tri_t)
                    inhead = jnp.logical_and(lane >= hh * cfg.hd, lane < (hh + 1) * cfg.hd)
                    dyhs.append(jnp.where(inhead, dyt, 0.0).astype(BF16))
                    lmats.append(lmat)
                    mts.append((gmat_t * lmat_t).astype(BF16))
                dy_stack = jnp.concatenate(dyhs, axis=0)
                dm_all = _nt(dy_stack, xt)
                for hh in range(hpt):
                    h = tile * hpt + hh
                    dm = dm_all[hh * q:(hh + 1) * q, :]
                    dg = dg + dm * lmats[hh]
                    qm = dm * gmat * lmats[hh]
                    rs = jnp.sum(qm, axis=1, keepdims=True)
                    csum = jnp.sum(qm, axis=0, keepdims=True)
                    dcs = dcs + jnp.where(lane == h, rs, 0.0)
                    dcs_t = dcs_t + jnp.where(sub == h, csum, 0.0)
                dxdt_g.append(_nn(jnp.concatenate(mts, axis=1), dy_stack))
            dxdt_diag = jnp.concatenate(dxdt_g, axis=1) if len(dxdt_g) > 1 else dxdt_g[0]
            dgb = dg.astype(BF16)
            d_cg = d_cg + _nn(dgb, bg)
            d_bg = d_bg + _tn(dgb, cg)
            dx_ref[:, inner + g * st:inner + (g + 1) * st] = d_bg
            dx_ref[:, inner + bcw + g * st:inner + bcw + (g + 1) * st] = d_cg
            dxdt = dxdt_diag + dxw * cq["W0"][:, gs]
            dx_ref[:, gs] = dyv[:, gs] * d_ref[:, gs] + dxdt * cq["DT"][:, gs]
            rt = rexpt[gs, :]
            dcs = dcs + red(dyv[:, gs] * y0 * cq["E"][:, gs], rt)
            r_w = red(dxw * xw_f[:, gs], rt)
            dcs = dcs - r_w
            dcs_last_g = jnp.sum(r_w, axis=0, keepdims=True)
            ddec = red(jnp.broadcast_to(jnp.sum(d_s[:, gs] * s_in[:, gs], axis=0, keepdims=True), (8, gw)), rt)[0:1, :]
            dcs_last_g = dcs_last_g + ddec * cq["decay"]
            dcs = dcs + jnp.where(lax.broadcasted_iota(jnp.int32, (q, LANE), 0) == q - 1, dcs_last_g, 0.0)
            ddt_part = red(dxdt * xs[:, gs], rt)
            if g == 0:
                ddt = ddt_part
            else:
                ddt = ddt + ddt_part
            ds_scr[:, gs] = d_s[:, gs] * cq["DEC"][:, gs] + ds_in_g
        dcs = dcs - dcs_t.T
        dadt = _sel_dot(dcs, ltri_v, left=True, trans=True)
        ddt = ddt + dadt * a_ref[...]
        da_ref[...] += jnp.sum(dadt * cq["dt"], axis=0, keepdims=True)
        draw = jnp.where(cq["live"], ddt * jax.nn.sigmoid(cq["pre"]), 0.0)
        ddt_ref[...] = draw
        dbias_ref[...] += jnp.sum(draw, axis=0, keepdims=True)

    def rowblk(width, col):
        return pl.BlockSpec((q, width), lambda b, s: (b * nc + nc - 1 - s, col))

    def const(shape):
        return pl.BlockSpec(shape, lambda b, s: (0, 0))

    bcol = inner // bcw
    outs = pl.pallas_call(
        body, name=name, grid=(cfg.bsz, nc),
        in_specs=[rowblk(inner, 0), rowblk(bcw, bcol), rowblk(bcw, bcol + 1), rowblk(LANE, cfg.dtt),
                  const((1, LANE)), const((1, LANE)), const((1, inner)), const((q, q)), const((LANE, inner)),
                  const((inner, LANE)),
                  pl.BlockSpec((1, st, inner), lambda b, s: (b * nc + nc - 1 - s, 0, 0)), rowblk(inner, 0)],
        out_specs=[rowblk(cfg.conv_dim, 0), rowblk(LANE, 0),
                   const((1, inner)), const((1, LANE)), const((1, LANE))],
        out_shape=[_sds((cfg.t, cfg.conv_dim), F32),
                   _sds((cfg.t, LANE), F32), _sds((1, inner), F32), _sds((1, LANE), F32), _sds((1, LANE), F32)],
        scratch_shapes=[pltpu.VMEM((st, inner), F32)], compiler_params=_cp(),
    )(xc, xc, xc, small, dt_bias, avec, dexp, ltri, rexp, rexp_t, sin, dy)
    return outs


def tail_fwd(cfg, y, z, w, *, name):
    t, inner, gw = cfg.t, cfg.inner, cfg.gw
    tr = _pick(t, 272, 16)

    def body(y_ref, z_ref, w_ref, o_ref):
        for g in range(cfg.groups):
            gs = slice(g * gw, (g + 1) * gw)
            yg = y_ref[:, gs].astype(F32) * _silu(z_ref[:, gs].astype(F32))
            r = lax.rsqrt(jnp.mean(yg * yg, axis=-1, keepdims=True) + EPS)
            o_ref[:, gs] = (yg * r * w_ref[:, gs]).astype(BF16)

    row = pl.BlockSpec((tr, inner), lambda i: (i, 0))
    return pl.pallas_call(
        body, name=name, grid=(t // tr,), in_specs=[row, row, pl.BlockSpec((1, inner), lambda i: (0, 0))],
        out_specs=row, out_shape=_sds((t, inner), BF16), compiler_params=_cp(),
    )(y, z, w.reshape(1, inner))


def tail_bwd(cfg, do, y, z, w, *, name):
    t, inner, gw = cfg.t, cfg.inner, cfg.gw
    tr = _pick(t, 272, 16)

    def body(do_ref, y_ref, z_ref, w_ref, dy_ref, dz_ref, dw_ref):
        @pl.when(pl.program_id(0) == 0)
        def _():
            dw_ref[...] = jnp.zeros_like(dw_ref)

        for g in range(cfg.groups):
            gs = slice(g * gw, (g + 1) * gw)
            yv = y_ref[:, gs].astype(F32)
            zv = z_ref[:, gs].astype(F32)
            dov = do_ref[:, gs].astype(F32)
            sz = _silu(zv)
            yg = yv * sz
            r = lax.rsqrt(jnp.mean(yg * yg, axis=-1, keepdims=True) + EPS)
            xh = yg * r
            gg = dov * w_ref[:, gs]
            dyg = r * (gg - xh * jnp.mean(gg * xh, axis=-1, keepdims=True))
            dw_ref[:, gs] += jnp.sum(dov * xh, axis=0, keepdims=True)
            dy_ref[:, gs] = (dyg * sz).astype(BF16)
            dz_ref[:, gs] = (dyg * yv * _dsilu(zv)).astype(BF16)

    row = pl.BlockSpec((tr, inner), lambda i: (i, 0))
    vec = pl.BlockSpec((1, inner), lambda i: (0, 0))
    dy, dz, dw = pl.pallas_call(
        body, name=name, grid=(t // tr,), in_specs=[row, row, row, vec], out_specs=[row, row, vec],
        out_shape=[_sds((t, inner), BF16), _sds((t, inner), BF16), _sds((1, inner), F32)], compiler_params=_cp(),
    )(do, y, z, w.reshape(1, inner))
    return dy, dz, dw[0]


def rope_tables(cfg):
    half = cfg.rope // 2
    pos = np.maximum(np.arange(cfg.lp) - cfg.pad, 0).astype(np.float32)
    inv = ROPE_THETA ** (-jnp.arange(0, cfg.rope, 2, dtype=F32) / cfg.rope)
    ang = jnp.asarray(pos)[:, None] * inv[None, :]
    cos, sin = jnp.cos(ang), jnp.sin(ang)
    zero = jnp.zeros((cfg.lp, LANE - 2 * half), F32)
    zh = jnp.zeros((cfg.lp, half), F32)
    ctab = jnp.concatenate([cos, cos, zero], axis=1)
    s1 = jnp.concatenate([-sin, zh, zero], axis=1)
    s2 = jnp.concatenate([zh, sin, zero], axis=1)
    return ctab, s1, s2


def _rope(x, c, s1, s2, half):
    return x * c + pltpu.roll(x, LANE - half, 1) * s1 + pltpu.roll(x, half, 1) * s2


def _rope_t(dy, c, s1, s2, half):
    return dy * c + pltpu.roll(dy * s1, half, 1) + pltpu.roll(dy * s2, LANE - half, 1)


def _attn_scale(cfg):
    return (cfg.nope + cfg.rope) ** -0.5


def rope_fwd(cfg, qf, small, tabs, *, name):
    t, qw, lp = cfg.t, cfg.qw, cfg.lp
    tr = _pick(lp, 544, 16)
    nrb = lp // tr
    half = cfg.rope // 2
    scale = _attn_scale(cfg)

    def body(q_ref, k_ref, c_ref, s1_ref, s2_ref, qo_ref, ko_ref):
        c, s1, s2 = c_ref[...], s1_ref[...], s2_ref[...]
        for h in range(cfg.mh):
            a = h * 2 * LANE
            qo_ref[:, a:a + LANE] = (q_ref[:, a:a + LANE].astype(F32) * scale).astype(BF16)
            qo_ref[:, a + LANE:a + 2 * LANE] = (
                _rope(q_ref[:, a + LANE:a + 2 * LANE].astype(F32), c, s1, s2, half) * scale).astype(BF16)
        ko_ref[...] = _rope(k_ref[...], c, s1, s2, half).astype(BF16)

    tab = pl.BlockSpec((tr, LANE), lambda i: (i % nrb, 0))
    return pl.pallas_call(
        body, name=name, grid=(t // tr,),
        in_specs=[pl.BlockSpec((tr, qw), lambda i: (i, 0)), pl.BlockSpec((tr, LANE), lambda i: (i, cfg.kt)), tab, tab, tab],
        out_specs=[pl.BlockSpec((tr, qw), lambda i: (i, 0)), pl.BlockSpec((tr, LANE), lambda i: (i, 0))],
        out_shape=[_sds((t, qw), BF16), _sds((t, LANE), BF16)], compiler_params=_cp(),
    )(qf, small, *tabs)


def rope_bwd(cfg, dq, dkpe, tabs, *, name):
    t, qw, lp = cfg.t, cfg.qw, cfg.lp
    tr = _pick(lp, 544, 16)
    nrb = lp // tr
    half = cfg.rope // 2
    scale = _attn_scale(cfg)

    def body(dq_ref, dk_ref, c_ref, s1_ref, s2_ref, qo_ref, ko_ref):
        c, s1, s2 = c_ref[...], s1_ref[...], s2_ref[...]
        for h in range(cfg.mh):
            a = h * 2 * LANE
            qo_ref[:, a:a + LANE] = (dq_ref[:, a:a + LANE].astype(F32) * scale).astype(BF16)
            qo_ref[:, a + LANE:a + 2 * LANE] = _rope_t(
                dq_ref[:, a + LANE:a + 2 * LANE].astype(F32) * scale, c, s1, s2, half).astype(BF16)
        dk = dk_ref[0]
        for h in range(1, cfg.mh):
            dk = dk + dk_ref[h]
        ko_ref[...] = _rope_t(dk, c, s1, s2, half)

    tab = pl.BlockSpec((tr, LANE), lambda i: (i % nrb, 0))
    return pl.pallas_call(
        body, name=name, grid=(t // tr,),
        in_specs=[pl.BlockSpec((tr, qw), lambda i: (i, 0)), pl.BlockSpec((cfg.mh, tr, LANE), lambda i: (0, i, 0)),
                  tab, tab, tab],
        out_specs=[pl.BlockSpec((tr, qw), lambda i: (i, 0)), pl.BlockSpec((tr, LANE), lambda i: (i, 0))],
        out_shape=[_sds((t, qw), BF16), _sds((t, LANE), F32)], compiler_params=_cp(),
    )(dq, dkpe, *tabs)


def _q_blocks(cfg):
    bounds = [0, cfg.chunk] + list(range(cfg.chunk + 256, cfg.lp + 1, 256))
    assert bounds[-1] == cfg.lp, "SEQ must be a multiple of 256"
    return list(zip(bounds[:-1], bounds[1:]))


def _attn_mask(cfg, qs, qe):
    rows = qs + lax.broadcasted_iota(jnp.int32, (qe - qs, qe), 0)
    cols = lax.broadcasted_iota(jnp.int32, (qe - qs, qe), 1)
    return jnp.logical_and(cols <= rows, jnp.logical_or(cols >= cfg.pad, rows < cfg.pad))


def _max_q_block(cfg):
    return max(qe - qs for qs, qe in _q_blocks(cfg))


def _masked_scores(cfg, q, k2, qs, qe, s_scr):
    bq, n = qe - qs, qe
    s_scr[0:bq, 0:n] = _nt(q, k2)
    if qs == 0:
        s_scr[0:bq, 0:n] = jnp.where(_attn_mask(cfg, 0, qe), s_scr[0:bq, 0:n], MASK_VALUE)
    else:
        assert qs >= cfg.chunk and cfg.pad < LANE
        cols = lax.broadcasted_iota(jnp.int32, (bq, LANE), 1)
        s_scr[0:bq, 0:LANE] = jnp.where(cols >= cfg.pad, s_scr[0:bq, 0:LANE], MASK_VALUE)
        r = lax.broadcasted_iota(jnp.int32, (bq, bq), 0)
        c = lax.broadcasted_iota(jnp.int32, (bq, bq), 1)
        s_scr[0:bq, qs:qe] = jnp.where(c <= r, s_scr[0:bq, qs:qe], MASK_VALUE)
    return s_scr[0:bq, 0:n]


def attn_fwd(cfg, qr, kv, kpe, *, name):
    lp, t, mh = cfg.lp, cfg.t, cfg.mh
    blocks = _q_blocks(cfg)

    def body(q_ref, kv_ref, kp_ref, o_ref, l_ref, s_scr):
        for qs, qe in blocks:
            n = qe
            q = q_ref[qs:qe, :]
            k2 = jnp.concatenate([kv_ref[0:n, 0:LANE], kp_ref[0:n, :]], axis=1)
            s = _masked_scores(cfg, q, k2, qs, qe, s_scr)
            m = jnp.max(s, axis=-1, keepdims=True)
            p = jnp.exp(s - m)
            l = jnp.sum(p, axis=-1, keepdims=True)
            o_ref[qs:qe, :] = (_nn(p.astype(BF16), kv_ref[0:n, LANE:2 * LANE]) * (1.0 / l)).astype(BF16)
            l_ref[qs:qe, :] = jnp.broadcast_to(m + jnp.log(l), (qe - qs, LANE))

    hb = pl.BlockSpec((lp, 2 * LANE), lambda b, h: (b, h))
    ob = pl.BlockSpec((lp, LANE), lambda b, h: (b, h))
    return pl.pallas_call(
        body, name=name, grid=(cfg.bsz, mh),
        in_specs=[hb, hb, pl.BlockSpec((lp, LANE), lambda b, h: (b, 0))], out_specs=[ob, ob],
        out_shape=[_sds((t, mh * LANE), BF16), _sds((t, mh * LANE), F32)],
        scratch_shapes=[pltpu.VMEM((_max_q_block(cfg), lp), F32)], compiler_params=_cp(),
    )(qr, kv, kpe)


def attn_bwd(cfg, qr, kv, kpe, o, lse, do, *, name):
    lp, t, mh = cfg.lp, cfg.t, cfg.mh
    blocks = _q_blocks(cfg)

    def body(q_ref, kv_ref, kp_ref, o_ref, l_ref, do_ref, dq_ref, dkv_ref, dkp_ref, dk_acc, dv_acc, s_scr):
        dk_acc[...] = jnp.zeros_like(dk_acc)
        dv_acc[...] = jnp.zeros_like(dv_acc)
        for qs, qe in blocks:
            n = qe
            q = q_ref[qs:qe, :]
            k2 = jnp.concatenate([kv_ref[0:n, 0:LANE], kp_ref[0:n, :]], axis=1)
            dob = do_ref[qs:qe, :].astype(BF16)
            delta = jnp.sum(dob.astype(F32) * o_ref[qs:qe, :].astype(F32), axis=-1, keepdims=True)
            s = _masked_scores(cfg, q, k2, qs, qe, s_scr)
            p = jnp.exp(s - l_ref[qs:qe, 0:1])
            dp = _nt(dob, kv_ref[0:n, LANE:2 * LANE])
            ds = (p * (dp - delta)).astype(BF16)
            dq_ref[qs:qe, :] = _nn(ds, k2).astype(BF16)
            dv_acc[0:n, :] += _tn(p.astype(BF16), dob)
            dk_acc[0:n, :] += _tn(ds, q)
        dkv_ref[:, 0:LANE] = dk_acc[:, 0:LANE].astype(BF16)
        dkv_ref[:, LANE:2 * LANE] = dv_acc[...].astype(BF16)
        dkp_ref[0] = dk_acc[:, LANE:2 * LANE]

    hb = pl.BlockSpec((lp, 2 * LANE), lambda b, h: (b, h))
    ob = pl.BlockSpec((lp, LANE), lambda b, h: (b, h))
    return pl.pallas_call(
        body, name=name, grid=(cfg.bsz, mh),
        in_specs=[hb, hb, pl.BlockSpec((lp, LANE), lambda b, h: (b, 0)), ob, ob, ob],
        out_specs=[hb, hb, pl.BlockSpec((1, lp, LANE), lambda b, h: (h, b, 0))],
        out_shape=[_sds((t, cfg.qw), BF16), _sds((t, mh * 2 * LANE), BF16), _sds((mh, t, LANE), F32)],
        scratch_shapes=[pltpu.VMEM((lp, 2 * LANE), F32), pltpu.VMEM((lp, LANE), F32),
                        pltpu.VMEM((_max_q_block(cfg), lp), F32)], compiler_params=_cp(),
    )(qr, kv, kpe, o, lse, do)


def _live_rows(cfg, tr, shape):
    rows = pl.program_id(1) * tr + lax.broadcasted_iota(jnp.int32, shape, 0)
    return rows >= cfg.pad


def gate_fwd(cfg, ya, yb, g, *, name):
    d, lp = cfg.d, cfg.lp
    tr = _pick(lp, 544, 16)
    nrb = lp // tr

    def body(ya_ref, yb_ref, ga_ref, gb_ref, o_ref):
        f = lambda ref: ref[...].astype(F32)
        mix = jax.nn.sigmoid(f(ga_ref)) * f(ya_ref) + jax.nn.sigmoid(f(gb_ref)) * f(yb_ref)
        o_ref[...] = jnp.where(_live_rows(cfg, tr, mix.shape), mix, 0.0).astype(BF16)

    row = pl.BlockSpec((tr, d), lambda b, j: (b * nrb + j, 0))
    row1 = pl.BlockSpec((tr, d), lambda b, j: (b * nrb + j, 1))
    return pl.pallas_call(
        body, name=name, grid=(cfg.bsz, nrb), in_specs=[row, row, row, row1], out_specs=row,
        out_shape=_sds((cfg.t, d), BF16), compiler_params=_cp(),
    )(ya, yb, g, g)


def gate_bwd(cfg, dmix, ya, yb, g, *, name):
    d, lp = cfg.d, cfg.lp
    tr = _pick(lp, 544, 16)
    nrb = lp // tr

    def body(dm_ref, ya_ref, yb_ref, ga_ref, gb_ref, dya_ref, dyb_ref, dg_ref):
        dm = dm_ref[...].astype(F32)
        dm = jnp.where(_live_rows(cfg, tr, dm.shape), dm, 0.0)
        sa = jax.nn.sigmoid(ga_ref[...].astype(F32))
        sb = jax.nn.sigmoid(gb_ref[...].astype(F32))
        dya_ref[...] = (dm * sa).astype(BF16)
        dyb_ref[...] = (dm * sb).astype(BF16)
        dg_ref[:, 0:d] = (dm * ya_ref[...].astype(F32) * sa * (1.0 - sa)).astype(BF16)
        dg_ref[:, d:2 * d] = (dm * yb_ref[...].astype(F32) * sb * (1.0 - sb)).astype(BF16)

    row = pl.BlockSpec((tr, d), lambda b, j: (b * nrb + j, 0))
    row1 = pl.BlockSpec((tr, d), lambda b, j: (b * nrb + j, 1))
    row2 = pl.BlockSpec((tr, 2 * d), lambda b, j: (b * nrb + j, 0))
    return pl.pallas_call(
        body, name=name, grid=(cfg.bsz, nrb), in_specs=[row, row, row, row, row1], out_specs=[row, row, row2],
        out_shape=[_sds((cfg.t, d), BF16), _sds((cfg.t, d), BF16), _sds((cfg.t, 2 * d), BF16)], compiler_params=_cp(),
    )(dmix, ya, yb, g, g)


def loss_head(cfg, h, target, w, *, name):
    d, q, nc = cfg.d, cfg.chunk, cfg.nchunks
    tpb = cfg.seq // q

    def body(h_ref, t_ref, w_ref, loss_ref, dh_ref, dw_ref):
        j = pl.program_id(1)

        @pl.when(jnp.logical_and(j == 0, pl.program_id(0) == 0))
        def _():
            loss_ref[...] = jnp.zeros_like(loss_ref)
            dw_ref[...] = jnp.zeros_like(dw_ref)

        @pl.when(j == 0)
        def _():
            dh_ref[...] = jnp.zeros_like(dh_ref)

        @pl.when(j > 0)
        def _():
            xv = h_ref[...]
            r = lax.rsqrt(jnp.mean(xv * xv, axis=-1, keepdims=True) + EPS)
            xh = xv * r
            err = xh * w_ref[...] - t_ref[...]
            loss_ref[...] += 0.5 * jnp.sum(jnp.sum(err * err, axis=-1, keepdims=True) / d, axis=0, keepdims=True)
            dy = err * (1.0 / d)
            g = dy * w_ref[...]
            dh_ref[...] = r * (g - xh * jnp.mean(g * xh, axis=-1, keepdims=True))
            dw_ref[...] += jnp.sum(dy * xh, axis=0, keepdims=True)

    row = pl.BlockSpec((q, d), lambda b, j: (b * nc + j, 0))
    loss, dh, dw = pl.pallas_call(
        body, name=name, grid=(cfg.bsz, nc),
        in_specs=[row, pl.BlockSpec((q, d), lambda b, j: (b * tpb + jnp.maximum(j - 1, 0), 0)),
                  pl.BlockSpec((1, d), lambda b, j: (0, 0))],
        out_specs=[pl.BlockSpec((8, LANE), lambda b, j: (0, 0)), row, pl.BlockSpec((1, d), lambda b, j: (0, 0))],
        out_shape=[_sds((8, LANE), F32), _sds((cfg.t, d), F32), _sds((1, d), F32)], compiler_params=_cp(),
    )(h, target, w.reshape(1, d))
    return loss[0, 0], dh, dw[0]


def _rows_tile(r, c):
    return _pick(r, max(8, (1 << 18) // max(c, 1) // 8 * 8), 8)


def _adam_update(w, g, m, v):
    c1 = 1.0 - ADAM_B1 ** ADAM_STEP
    c2 = 1.0 - ADAM_B2 ** ADAM_STEP
    mn = ADAM_B1 * m + (1.0 - ADAM_B1) * g
    vn = ADAM_B2 * v + (1.0 - ADAM_B2) * (g * g)
    delta = -ADAM_LR * ((mn / c1) / (jnp.sqrt(vn / c2) + ADAM_EPS) + ADAM_WD * w)
    return delta, mn, vn


def adamw_layer(w, m, v, g, li, prev, dep, *, name):
    _, r, c = w.shape
    tr = _rows_tile(r, c)

    def body(*refs):
        w_ref, m_ref, v_ref, g_ref = refs[:4]
        go_ref, d_ref, mo_ref, vo_ref = refs[-4:]
        gv = g_ref[...]
        delta, mn, vn = _adam_update(w_ref[0], gv, m_ref[0], v_ref[0])
        go_ref[0] = gv
        d_ref[0] = delta
        mo_ref[0] = mn
        vo_ref[0] = vn

    if tr * c * 4 >= (1 << 16):
        steps = r // tr
        blk3 = pl.BlockSpec((1, tr, c), lambda i: (li, i, 0))
        blk2 = pl.BlockSpec((tr, c), lambda i: (i, 0))
    else:
        tc = _pick(c, max(LANE, (1 << 18) // r // LANE * LANE), LANE)
        steps = c // tc
        blk3 = pl.BlockSpec((1, r, tc), lambda i: (li, 0, i))
        blk2 = pl.BlockSpec((r, tc), lambda i: (0, i))
    anyspec = pl.BlockSpec(memory_space=pl.ANY)
    in_specs = [blk3, blk3, blk3, blk2, anyspec]
    args = [w, m, v, g, dep]
    aliases = {}
    if prev is not None:
        in_specs += [anyspec] * 4
        args += list(prev)
        aliases = {5 + i: i for i in range(4)}
    return pl.pallas_call(
        body, name=name, grid=(steps,), in_specs=in_specs, out_specs=[blk3] * 4,
        out_shape=[_sds(w.shape, F32)] * 4, input_output_aliases=aliases, compiler_params=_cp(),
    )(*args)


def pair_add(g4, other, half, *, name):
    n, _, r, c = g4.shape
    tr = _rows_tile(r, c)

    def body(h_ref, a_ref, b_ref, o_ref):
        o_ref[0] = (a_ref[0, 0].astype(F32) + b_ref[0].astype(F32)).astype(BF16)

    blk = pl.BlockSpec((1, tr, c), lambda j, i, h: (j, i, 0))
    grid_spec = pltpu.PrefetchScalarGridSpec(
        num_scalar_prefetch=1, grid=(n, r // tr),
        in_specs=[pl.BlockSpec((1, 1, tr, c), lambda j, i, h: (j, h[0], i, 0)), blk], out_specs=blk)
    return pl.pallas_call(body, name=name, grid_spec=grid_spec, out_shape=_sds((n, r, c), BF16),
                          compiler_params=_cp())(half, g4, other)


def chip_sum(recv, part, where, *, name):
    n, r, c = recv.shape
    tr = _rows_tile(r, c)

    def body(s_ref, *refs):
        own_ref, o_ref = refs[n], refs[n + 1]
        acc = None
        for j in range(n):
            term = jnp.where(s_ref[0] == j, own_ref[0], refs[j][0]).astype(F32)
            acc = term if acc is None else acc + term
        o_ref[0] = acc

    def slot(j):
        return pl.BlockSpec((1, tr, c), lambda i, s: (jnp.where(s[0] == j, (j + 1) % n, j), i, 0))

    grid_spec = pltpu.PrefetchScalarGridSpec(
        num_scalar_prefetch=1, grid=(r // tr,),
        in_specs=[slot(j) for j in range(n)] + [pl.BlockSpec((1, tr, c), lambda i, s: (s[0], i, 0))],
        out_specs=pl.BlockSpec((1, tr, c), lambda i, s: (s[1], i, 0)))
    return pl.pallas_call(body, name=name, grid_spec=grid_spec, out_shape=_sds((2, r, c), F32),
                          compiler_params=_cp())(where, *([recv] * n), part)


def _coords():
    return lax.axis_index("x"), lax.axis_index("y"), lax.axis_index("c")


def _other_chips(x, y):
    return [(1 - x, y), (x, 1 - y), (1 - x, 1 - y)]


def gather_chips(arrs, *, name):
    n = len(arrs)
    anyspec = pl.BlockSpec(memory_space=pl.ANY)

    def body(*refs):
        ins, outs = refs[:n], refs[n:2 * n]
        send_sems, recv_sems, local_sems = refs[2 * n:]
        x, y, c = _coords()
        me = 2 * x + y
        chips = _other_chips(x, y)
        copies = []
        for k in range(n):
            loc = pltpu.make_async_copy(ins[k], outs[k].at[me], local_sems.at[k])
            loc.start()
            copies.append(loc)
        sends = []
        for k in range(n):
            for j, (px, py) in enumerate(chips):
                cp = pltpu.make_async_remote_copy(
                    src_ref=ins[k], dst_ref=outs[k].at[me], send_sem=send_sems.at[k, j], recv_sem=recv_sems.at[k, j],
                    device_id=(px, py, c), device_id_type=MESH)
                cp.start()
                sends.append(cp)
        for k in range(n):
            for j, (px, py) in enumerate(chips):
                pltpu.make_async_remote_copy(
                    src_ref=ins[k], dst_ref=outs[k].at[2 * px + py], send_sem=send_sems.at[k, j],
                    recv_sem=recv_sems.at[k, j], device_id=(px, py, c), device_id_type=MESH).wait_recv()
        for cp in sends:
            cp.wait_send()
        for cp in copies:
            cp.wait()

    return pl.pallas_call(
        body, name=name, in_specs=[anyspec] * n, out_specs=[anyspec] * n,
        out_shape=[_sds((4,) + a.shape, a.dtype) for a in arrs],
        scratch_shapes=[pltpu.SemaphoreType.DMA((n, 3)), pltpu.SemaphoreType.DMA((n, 3)), pltpu.SemaphoreType.DMA((n,))],
        compiler_params=_cp(has_side_effects=True),
    )(*arrs)


def allreduce_small(vec, after, *, name):
    r, c = vec.shape

    def body(v_ref, after_ref, o_ref, buf, send_sems, recv_sems):
        x, y, cc = _coords()
        me = 4 * x + 2 * y + cc
        buf[me] = v_ref[...]
        sends = []
        flips = [(fx, fy, fc) for fx in (0, 1) for fy in (0, 1) for fc in (0, 1)][1:]
        for j, (fx, fy, fc) in enumerate(flips):
            peer = ((1 - x) if fx else x, (1 - y) if fy else y, (1 - cc) if fc else cc)
            cp = pltpu.make_async_remote_copy(
                src_ref=v_ref, dst_ref=buf.at[me], send_sem=send_sems.at[j], recv_sem=recv_sems.at[j],
                device_id=peer, device_id_type=MESH)
            cp.start()
            sends.append(cp)
        for j, (fx, fy, fc) in enumerate(flips):
            px, py, pc = ((1 - x) if fx else x, (1 - y) if fy else y, (1 - cc) if fc else cc)
            pltpu.make_async_remote_copy(
                src_ref=v_ref, dst_ref=buf.at[4 * px + 2 * py + pc], send_sem=send_sems.at[j],
                recv_sem=recv_sems.at[j], device_id=(px, py, pc), device_id_type=MESH).wait_recv()
        for cp in sends:
            cp.wait_send()
        acc = buf[0]
        for k in range(1, 8):
            acc = acc + buf[k]
        o_ref[...] = acc

    vm = pl.BlockSpec(memory_space=pltpu.VMEM)
    return pl.pallas_call(
        body, name=name, in_specs=[vm, pl.BlockSpec(memory_space=pl.ANY)], out_specs=vm, out_shape=_sds((r, c), F32),
        scratch_shapes=[pltpu.VMEM((8, r, c), F32), pltpu.SemaphoreType.DMA((7,)), pltpu.SemaphoreType.DMA((7,))],
        compiler_params=_cp(has_side_effects=True),
    )(vec, after)


def pair_share(lands, owns, *, name):
    n = len(lands)
    anyspec = pl.BlockSpec(memory_space=pl.ANY)

    def body(*refs):
        ins, own_refs, outs = refs[:n], refs[n:2 * n], refs[2 * n:3 * n]
        send_sems, recv_sems = refs[3 * n:]
        x, y, c = _coords()
        me = 2 * x + y
        sib = (x, y, 1 - c)
        sends = []
        for k in range(n):
            for j, (px, py) in enumerate(_other_chips(x, y)):
                cp = pltpu.make_async_remote_copy(
                    src_ref=ins[k].at[2 * px + py, c], dst_ref=outs[k].at[2 * px + py, c], send_sem=send_sems.at[k, j],
                    recv_sem=recv_sems.at[k, j], device_id=sib, device_id_type=MESH)
                cp.start()
                sends.append(cp)
            cp = pltpu.make_async_remote_copy(
                src_ref=own_refs[k], dst_ref=outs[k].at[me], send_sem=send_sems.at[k, 3], recv_sem=recv_sems.at[k, 3],
                device_id=sib, device_id_type=MESH)
            cp.start()
            sends.append(cp)
        for k in range(n):
            for j, (px, py) in enumerate(_other_chips(x, y)):
                pltpu.make_async_remote_copy(
                    src_ref=ins[k].at[2 * px + py, c], dst_ref=outs[k].at[2 * px + py, 1 - c],
                    send_sem=send_sems.at[k, j], recv_sem=recv_sems.at[k, j], device_id=sib,
                    device_id_type=MESH).wait_recv()
            pltpu.make_async_remote_copy(
                src_ref=own_refs[k], dst_ref=outs[k].at[me], send_sem=send_sems.at[k, 3], recv_sem=recv_sems.at[k, 3],
                device_id=sib, device_id_type=MESH).wait_recv()
        for cp in sends:
            cp.wait_send()

    return pl.pallas_call(
        body, name=name, in_specs=[anyspec] * (2 * n), out_specs=[anyspec] * n,
        out_shape=[_sds(a.shape, a.dtype) for a in lands], input_output_aliases={k: k for k in range(n)},
        scratch_shapes=[pltpu.SemaphoreType.DMA((n, 4)), pltpu.SemaphoreType.DMA((n, 4))],
        compiler_params=_cp(has_side_effects=True),
    )(*lands, *owns)


def pair_fill(arrs, *, name):
    n = len(arrs)
    anyspec = pl.BlockSpec(memory_space=pl.ANY)

    def body(*refs):
        ins, outs = refs[:n], refs[n:2 * n]
        send_sems, recv_sems = refs[2 * n:]
        x, y, c = _coords()
        sends = []
        for k in range(n):
            cp = pltpu.make_async_remote_copy(
                src_ref=ins[k].at[c], dst_ref=outs[k].at[c], send_sem=send_sems.at[k], recv_sem=recv_sems.at[k],
                device_id=(x, y, 1 - c), device_id_type=MESH)
            cp.start()
            sends.append(cp)
        for k in range(n):
            pltpu.make_async_remote_copy(
                src_ref=ins[k].at[c], dst_ref=outs[k].at[1 - c], send_sem=send_sems.at[k], recv_sem=recv_sems.at[k],
                device_id=(x, y, 1 - c), device_id_type=MESH).wait_recv()
        for cp in sends:
            cp.wait_send()

    return pl.pallas_call(
        body, name=name, in_specs=[anyspec] * n, out_specs=[anyspec] * n,
        out_shape=[_sds(a.shape, a.dtype) for a in arrs], input_output_aliases={k: k for k in range(n)},
        scratch_shapes=[pltpu.SemaphoreType.DMA((n,)), pltpu.SemaphoreType.DMA((n,))],
        compiler_params=_cp(has_side_effects=True),
    )(*arrs)


_HBM = pl.BlockSpec(memory_space=pltpu.HBM)
_SEM = pl.BlockSpec(memory_space=pltpu.SEMAPHORE)


_COPIES_PER_ARRAY = {"gather": 3, "scatter": 3, "share": 4, "exchange": 4}


def _ici_copies(kind, srcs, lands, send_sems, recv_sems):
    x, y, c = _coords()
    me = 2 * x + y
    per = _COPIES_PER_ARRAY[kind]
    sends, recvs = [], []
    for k in range(len(srcs)):
        triples = []
        for j, (px, py) in enumerate(_other_chips(x, y)):
            peer = 2 * px + py
            if kind == "gather":
                triples.append((srcs[k].at[c], lands[k].at[me, c], lands[k].at[peer, c], (px, py, c)))
            elif kind == "scatter":
                triples.append((srcs[k].at[peer], lands[k].at[me], lands[k].at[peer], (px, py, c)))
            elif kind == "share":
                triples.append((lands[k].at[peer, c], lands[k].at[peer, c], lands[k].at[peer, 1 - c], (x, y, 1 - c)))
        if kind == "share":
            triples.append((srcs[k], lands[k].at[me], lands[k].at[me], (x, y, 1 - c)))
        if kind == "exchange":
            triples = [(srcs[k].at[j, 1 - c], lands[k].at[j], lands[k].at[j], (x, y, 1 - c)) for j in range(4)]
        for j, (src, there, here, dev) in enumerate(triples):
            sem = per * k + j
            mk = functools.partial(pltpu.make_async_remote_copy, src_ref=src, send_sem=send_sems.at[sem],
                                   recv_sem=recv_sems.at[sem], device_id=dev, device_id_type=MESH)
            sends.append(mk(dst_ref=there))
            recvs.append(mk(dst_ref=here))
    return sends, recvs


def ici_start(kind, srcs, lands, after, *, name):
    n = len(srcs)

    def body(*refs):
        src_refs, land_refs = refs[:n], refs[n:2 * n]
        send_sems, recv_sems = refs[2 * n + 1], refs[2 * n + 2]
        token = refs[-1]
        sends, _ = _ici_copies(kind, src_refs, land_refs, send_sems, recv_sems)
        for cp in sends:
            cp.start()
        token[...] = jnp.zeros_like(token)

    both = list(srcs) + list(lands)
    out = pl.pallas_call(
        body, name=name,
        in_specs=[_HBM] * (2 * n) + [pl.BlockSpec(memory_space=pl.ANY)],
        out_shape=(pltpu.SemaphoreType.DMA((_COPIES_PER_ARRAY[kind] * n,)),
                   pltpu.SemaphoreType.DMA((_COPIES_PER_ARRAY[kind] * n,)),
                   *[pltpu.HBM(a.shape, a.dtype) for a in both], _sds((8, LANE), F32)),
        out_specs=(_SEM, _SEM, *([_HBM] * (2 * n)), pl.BlockSpec(memory_space=pltpu.VMEM)),
        input_output_aliases={i: 2 + i for i in range(2 * n)},
        compiler_params=_cp(has_side_effects=pltpu.SideEffectType.DATAFLOW_SIDE_EFFECTING),
    )(*[pltpu.with_memory_space_constraint(a, pltpu.HBM) for a in both], after)
    return out[0], out[1], list(out[2:2 + n]), list(out[2 + n:2 + 2 * n]), out[-1]


def ici_wait(kind, started, after, *, name):
    send_sems, recv_sems, srcs, lands, _ = started
    n = len(srcs)

    def body(*refs):
        src_refs, land_refs = refs[:n], refs[n:2 * n]
        sends, recvs = _ici_copies(kind, src_refs, land_refs, refs[2 * n], refs[2 * n + 1])
        for cp in sends:
            cp.wait_send()
        for cp in recvs:
            cp.wait_recv()

    both = list(srcs) + list(lands)
    out = pl.pallas_call(
        body, name=name,
        in_specs=[_HBM] * (2 * n) + [_SEM, _SEM, pl.BlockSpec(memory_space=pl.ANY)],
        out_shape=tuple(pltpu.HBM(a.shape, a.dtype) for a in both), out_specs=tuple([_HBM] * (2 * n)),
        input_output_aliases={i: i for i in range(2 * n)},
        compiler_params=_cp(has_side_effects=pltpu.SideEffectType.DATAFLOW_SIDE_EFFECTING),
    )(*both, send_sems, recv_sems, after)
    return list(out[:n]), list(out[n:])


BIG = ["w_in", "w_uq", "w_ukv", "w_branch_ssm", "w_branch_mla", "w_out", "w_mlp_up", "w_mlp_down"]
COL_SHARDED = {"w_in", "w_uq", "w_ukv", "w_mlp_up"}
SMALL_REPL = ["norm_mix_w", "conv_b", "dt_bias", "a_log", "d_skip", "ssm_norm_w", "q_norm_w", "kv_norm_w", "norm_mlp_w"]


def _unshard_layer(name, g):
    _, r, c = g.shape
    if name in COL_SHARDED:
        return jnp.transpose(g, (1, 0, 2)).reshape(r, 4 * c)
    return g.reshape(4 * r, c)


def _to_shards(name, full):
    r, c = full.shape
    if name in COL_SHARDED:
        return jnp.transpose(full.reshape(r, 4, c // 4), (1, 0, 2))
    return full.reshape(4, r // 4, c)


REST = [k for k in BIG if k != "w_in"]


def prep_layer(cfg, w):
    out = {}
    if "w_in" in w:
        sp = np.cumsum(cfg.in_splits)[:-1].tolist()
        z, xbc, dt, cq, ckv, kr, gs, gm = jnp.split(w["w_in"], sp, axis=1)
        zpad = lambda n: jnp.zeros((cfg.d, n), z.dtype)
        out.update(w_z=z, w_xbc=xbc, w_g=jnp.concatenate([gs, gm], axis=1),
                   w_s=jnp.concatenate([cq, ckv, kr, zpad(LANE - cfg.rope), dt, zpad(LANE - cfg.heads)], axis=1))
    if "w_uq" in w:
        out.update(
            w_uq=jnp.pad(w["w_uq"].reshape(cfg.ql, cfg.mh, cfg.nope + cfg.rope),
                         ((0, 0), (0, 0), (0, 2 * LANE - cfg.nope - cfg.rope))).reshape(cfg.ql, cfg.qw),
            w_ukv=w["w_ukv"], w_bs=w["w_branch_ssm"], w_bm=w["w_branch_mla"], w_out=w["w_out"],
            w_up=w["w_mlp_up"], w_down=w["w_mlp_down"])
    return {k: v.astype(BF16) for k, v in out.items()}


def unprep_grads(cfg, g):
    out = {}
    if "w_s" in g:
        ql, kvl = cfg.ql, cfg.kvl
        ds_ = g["w_s"]
        cq, ckv = ds_[:, :ql], ds_[:, ql:ql + kvl]
        kr = ds_[:, ql + kvl:ql + kvl + cfg.rope]
        dt = ds_[:, ql + kvl + LANE:ql + kvl + LANE + cfg.heads]
        out["w_in"] = jnp.concatenate([g["w_z"], g["w_xbc"], dt, cq, ckv, kr, g["w_g"]], axis=1)
    if "w_uq" in g:
        out.update(
            w_uq=g["w_uq"].reshape(cfg.ql, cfg.mh, 2 * LANE)[:, :, :cfg.nope + cfg.rope].reshape(cfg.ql, -1),
            w_ukv=g["w_ukv"], w_branch_ssm=g["w_bs"], w_branch_mla=g["w_bm"],
            w_out=g["w_out"], w_mlp_up=g["w_up"], w_mlp_down=g["w_down"])
    return out


def _hook(hooks, name, arg):
    if hooks and name in hooks:
        return hooks[name](arg)[0, 0]
    return 0.0


def layer_fwd(cfg, h, pw, sm, tabs, li, hooks=None):
    n = lambda s: f"l{li}_{s}"
    u = rmsnorm_fwd(h, sm["norm_mix_w"], name=n("norm_mix"))
    z = matmul(u, pw["w_z"], out_dtype=BF16, name=n("in_z"))
    xbc = matmul(u, pw["w_xbc"], name=n("in_xbc"))
    g = matmul(u, pw["w_g"], out_dtype=BF16, name=n("in_g"))
    small = matmul(u, pw["w_s"], name=n("in_s"))
    xc, dsilu = conv_fwd(cfg, xbc, sm["conv_w"], sm["conv_b"], name=n("conv"))
    dt_bias = sm["dt_bias_p"] + _hook(hooks, "after_conv", xc)
    y, sin = ssd_fwd(cfg, xc, small, dt_bias, sm["avec"], sm["dexp"], name=n("ssd"))
    y_ssm = tail_fwd(cfg, y, z, sm["ssm_norm_w"], name=n("tail"))
    if hooks and "weights" in hooks:
        pw = dict(pw, **hooks["weights"](y_ssm))
    cqn = rmsnorm_fwd(small, sm["q_norm_w"], cw=cfg.ql, ci=0, name=n("q_norm"))
    ckvn = rmsnorm_fwd(small, sm["kv_norm_w"], cw=cfg.kvl, ci=cfg.ql // cfg.kvl, name=n("kv_norm"))
    qf = matmul(cqn, pw["w_uq"], out_dtype=BF16, name=n("uq"))
    kv = matmul(ckvn, pw["w_ukv"], out_dtype=BF16, name=n("ukv"))
    qr, kpe = rope_fwd(cfg, qf, small, tabs, name=n("rope"))
    o, lse = attn_fwd(cfg, qr, kv, kpe, name=n("attn"))
    ya = matmul(y_ssm, pw["w_bs"], out_dtype=BF16, name=n("branch_ssm"))
    yb = matmul(o, pw["w_bm"], out_dtype=BF16, name=n("branch_mla"))
    mixed = gate_fwd(cfg, ya, yb, g, name=n("gate"))
    h1 = matmul(mixed, pw["w_out"], add=h, name=n("out"))
    v = rmsnorm_fwd(h1, sm["norm_mlp_w"] + _hook(hooks, "after_attn", o), name=n("norm_mlp"))
    a, act = matmul(v, pw["w_up"], name=n("up"), epilogue=_ep_relu2, out_dtypes=(BF16, BF16))
    h2 = matmul(act, pw["w_down"], add=h1, name=n("down"))
    saved = dict(h=h, u=u, z=z, xbc=xbc, g=g, small=small, xc=xc, dsilu=dsilu, y=y, sin=sin, y_ssm=y_ssm, cqn=cqn, ckvn=ckvn,
                 qr=qr, kv=kv, kpe=kpe, o=o, lse=lse, ya=ya, yb=yb, mixed=mixed, h1=h1, v=v, a=a, act=act)
    return h2, saved, pw


def layer_bwd(cfg, dh2, pw, sm, tabs, s, li, hooks=None):
    n = lambda t: f"l{li}_b_{t}"
    gw, gs = {}, {}
    wgrad = functools.partial(matmul, ta=True, out_dtype=BF16)
    gw["w_down"] = wgrad(s["act"], dh2, name=n("dw_down"))
    da = matmul(dh2, pw["w_down"], tb=True, name=n("dact"), epilogue=_ep_relu2_grad, extras=(s["a"],),
                out_dtypes=(BF16,))
    gw["w_up"] = wgrad(s["v"], da, name=n("dw_up"))
    dv = matmul(da, pw["w_up"], tb=True, out_dtype=BF16, name=n("dv"))
    dh1, gs["norm_mlp_w"] = rmsnorm_bwd(dv, s["h1"], sm["norm_mlp_w"], res=dh2, name=n("norm_mlp"))
    gw["w_out"] = wgrad(s["mixed"], dh1, name=n("dw_out"))
    dmix = matmul(dh1, pw["w_out"], tb=True, out_dtype=BF16, name=n("dmix"))
    dya, dyb, dg = gate_bwd(cfg, dmix, s["ya"], s["yb"], s["g"], name=n("gate"))
    gw["w_bs"] = wgrad(s["y_ssm"], dya, name=n("dw_bs"))
    gw["w_bm"] = wgrad(s["o"], dyb, name=n("dw_bm"))
    dy_ssm = matmul(dya, pw["w_bs"], tb=True, out_dtype=BF16, name=n("dy_ssm"))
    do = matmul(dyb, pw["w_bm"], tb=True, out_dtype=BF16, name=n("do"))
    dq, dkv, dkpe = attn_bwd(cfg, s["qr"], s["kv"], s["kpe"], s["o"], s["lse"], do, name=n("attn"))
    dqf, dkr = rope_bwd(cfg, dq, dkpe, tabs, name=n("rope"))
    gw["w_uq"] = wgrad(s["cqn"], dqf, name=n("dw_uq"))
    gw["w_ukv"] = wgrad(s["ckvn"], dkv, name=n("dw_ukv"))
    dcqn = matmul(dqf, pw["w_uq"], tb=True, name=n("dcqn"))
    dckvn = matmul(dkv, pw["w_ukv"], tb=True, name=n("dckvn"))
    q_norm_w = sm["q_norm_w"] + _hook(hooks, "after_attn", dqf)
    dcq, gs["q_norm_w"] = rmsnorm_bwd(dcqn, s["small"], q_norm_w, cw=cfg.ql, ci=0, out_dtype=BF16, name=n("q_norm"))
    dckv, gs["kv_norm_w"] = rmsnorm_bwd(dckvn, s["small"], sm["kv_norm_w"], cw=cfg.kvl, ci=cfg.ql // cfg.kvl,
                                        out_dtype=BF16, name=n("kv_norm"))
    ssm_norm_w = sm["ssm_norm_w"] + _hook(hooks, "early", dict(gw))
    dy, dz, gs["ssm_norm_w"] = tail_bwd(cfg, dy_ssm, s["y"], s["z"], ssm_norm_w, name=n("tail"))
    dxc, ddt, ddexp, dav, dbias = ssd_bwd(cfg, s["xc"], s["small"], sm["dt_bias_p"], sm["avec"], sm["dexp"],
                                          s["sin"], dy, name=n("ssd"))
    conv_w = sm["conv_w"] + _hook(hooks, "after_ssd", dxc)
    dxbc, gs["conv_w"], gs["conv_b"] = conv_bwd(cfg, s["xbc"], conv_w, s["dsilu"], dxc, name=n("conv"))
    gs["d_skip"] = ddexp.reshape(cfg.heads, cfg.hd).sum(axis=1)
    gs["a_log"] = (dav[0] * sm["avec"][0])[:cfg.heads]
    gs["dt_bias"] = dbias[0, :cfg.heads]
    dsmall = jnp.concatenate([dcq, dckv, dkr.astype(BF16), ddt.astype(BF16)], axis=1)
    gw["w_z"] = wgrad(s["u"], dz, name=n("dw_z"))
    gw["w_xbc"] = wgrad(s["u"], dxbc, name=n("dw_xbc"))
    gw["w_g"] = wgrad(s["u"], dg, name=n("dw_g"))
    gw["w_s"] = wgrad(s["u"], dsmall, name=n("dw_s"))
    du = matmul_nt_sum([dz, dxbc, dg, dsmall], [pw["w_z"], pw["w_xbc"], pw["w_g"], pw["w_s"]], out_dtype=BF16,
                       name=n("du"))
    dh, gs["norm_mix_w"] = rmsnorm_bwd(du, s["h"], sm["norm_mix_w"], res=dh1, name=n("norm_mix"))
    return dh, gw, gs


def small_params(cfg, p, li):
    pad_l = lambda v: jnp.pad(v, (0, LANE - v.shape[0])).reshape(1, LANE)
    return dict(
        norm_mix_w=p["norm_mix_w"][li], conv_w=p["conv_w"][li], conv_b=p["conv_b"][li],
        dt_bias_p=pad_l(p["dt_bias"][li]), avec=pad_l(-jnp.exp(p["a_log"][li])),
        dexp=jnp.repeat(p["d_skip"][li], cfg.hd).reshape(1, cfg.inner),
        ssm_norm_w=p["ssm_norm_w"][li], q_norm_w=p["q_norm_w"][li], kv_norm_w=p["kv_norm_w"][li],
        norm_mlp_w=p["norm_mlp_w"][li])


def local_step(cfg, x, target, p, depth=2):
    bsz, d = cfg.bsz, cfg.d
    lead = jnp.zeros((bsz, cfg.pad, d), F32)
    meta = jnp.broadcast_to(p["meta_tokens"][None], (bsz, cfg.n_meta, d))
    h = jnp.concatenate([lead, meta, x], axis=1).reshape(cfg.t, d)
    tabs = rope_tables(cfg)
    saved, sms = [], []
    for li in range(depth):
        sm = small_params(cfg, p, li)
        h, s, _ = layer_fwd(cfg, h, p["pw"][li], sm, tabs, li)
        saved.append(s)
        sms.append(sm)
    loss, dh, dfw = loss_head(cfg, h, target.reshape(bsz * cfg.seq, d), p["final_norm_w"], name="loss_head")
    gws, gss = [None] * depth, [None] * depth
    for li in reversed(range(depth)):
        dh, gws[li], gss[li] = layer_bwd(cfg, dh, p["pw"][li], sms[li], tabs, saved[li], li)
    dh = dh.reshape(bsz, cfg.lp, d)
    grad_x = dh[:, cfg.chunk:, :]
    gmeta = jnp.sum(dh[:, cfg.pad:cfg.chunk, :], axis=0)
    return loss, grad_x, gmeta, gws, gss, dfw


def _pack_small(parts):
    flat = jnp.concatenate([a.reshape(-1) for a in parts])
    n = flat.shape[0]
    npad = -n % (8 * LANE)
    return jnp.pad(flat, (0, npad)).reshape(-1, LANE), n


def _unpack_small(vec, shapes):
    flat = vec.reshape(-1)
    out, off = [], 0
    for sh in shapes:
        sz = int(np.prod(sh))
        out.append(flat[off:off + sz].reshape(sh))
        off += sz
    return out


def _as2d(a):
    return a.reshape(-1, a.shape[-1])


def kernel(x, meta_tokens, norm_mix_w, w_in, conv_w, conv_b, dt_bias, a_log, d_skip, ssm_norm_w, q_norm_w, kv_norm_w, w_uq, w_ukv, w_branch_ssm, w_branch_mla, w_out, norm_mlp_w, w_mlp_up, w_mlp_down, final_norm_w, loss_target, m_meta_tokens, m_norm_mix_w, m_w_in, m_conv_w, m_conv_b, m_dt_bias, m_a_log, m_d_skip, m_ssm_norm_w, m_q_norm_w, m_kv_norm_w, m_w_uq, m_w_ukv, m_w_branch_ssm, m_w_branch_mla, m_w_out, m_norm_mlp_w, m_w_mlp_up, m_w_mlp_down, m_final_norm_w, v_meta_tokens, v_norm_mix_w, v_w_in, v_conv_w, v_conv_b, v_dt_bias, v_a_log, v_d_skip, v_ssm_norm_w, v_q_norm_w, v_kv_norm_w, v_w_uq, v_w_ukv, v_w_branch_ssm, v_w_branch_mla, v_w_out, v_norm_mlp_w, v_w_mlp_up, v_w_mlp_down, v_final_norm_w):
    cfg = CFG
    names = ["meta_tokens", "norm_mix_w", "w_in", "conv_w", "conv_b", "dt_bias", "a_log", "d_skip", "ssm_norm_w",
             "q_norm_w", "kv_norm_w", "w_uq", "w_ukv", "w_branch_ssm", "w_branch_mla", "w_out", "norm_mlp_w",
             "w_mlp_up", "w_mlp_down", "final_norm_w"]
    wts = dict(zip(names, [meta_tokens, norm_mix_w, w_in, conv_w, conv_b, dt_bias, a_log, d_skip, ssm_norm_w,
                           q_norm_w, kv_norm_w, w_uq, w_ukv, w_branch_ssm, w_branch_mla, w_out, norm_mlp_w,
                           w_mlp_up, w_mlp_down, final_norm_w]))
    ms = dict(zip(names, [m_meta_tokens, m_norm_mix_w, m_w_in, m_conv_w, m_conv_b, m_dt_bias, m_a_log, m_d_skip,
                          m_ssm_norm_w, m_q_norm_w, m_kv_norm_w, m_w_uq, m_w_ukv, m_w_branch_ssm, m_w_branch_mla,
                          m_w_out, m_norm_mlp_w, m_w_mlp_up, m_w_mlp_down, m_final_norm_w]))
    vs = dict(zip(names, [v_meta_tokens, v_norm_mix_w, v_w_in, v_conv_w, v_conv_b, v_dt_bias, v_a_log, v_d_skip,
                          v_ssm_norm_w, v_q_norm_w, v_kv_norm_w, v_w_uq, v_w_ukv, v_w_branch_ssm, v_w_branch_mla,
                          v_w_out, v_norm_mlp_w, v_w_mlp_up, v_w_mlp_down, v_final_norm_w]))
    cx, cy, cc = _coords()
    chip = 2 * cx + cy

    half1 = jnp.reshape(cc, (1,)).astype(jnp.int32)
    where2 = jnp.stack([chip, cc]).astype(jnp.int32)
    wb = {k: wts[k].astype(BF16) for k in BIG}
    zero_tok = jnp.zeros((8, LANE), F32)

    def halves(a):
        return a.reshape((2, a.shape[0] // 2) + a.shape[1:])

    def gather_start(li, keys, tag, after):
        srcs = [halves(wb[k][li]) for k in keys]
        lands = [lax.empty((4,) + s.shape, BF16) for s in srcs]
        return ici_start("gather", srcs, lands, after, name=f"gather{li}{tag}_start")

    def gather_finish(li, keys, tag, started, after):
        srcs, lands = ici_wait("gather", started, after, name=f"gather{li}{tag}_wait")
        lands = pair_share(lands, srcs, name=f"gather{li}{tag}_share")
        full = {k: _unshard_layer(k, land.reshape((4, 2 * land.shape[2], land.shape[3])))
                for k, land in zip(keys, lands)}
        return prep_layer(cfg, full)

    def gather_mid(li, keys, tag, started, after):
        srcs, lands = ici_wait("gather", started, after, name=f"gather{li}{tag}_wait")
        return ici_start("share", srcs, lands, zero_tok, name=f"gather{li}{tag}_share_start")

    def gather_end(li, keys, tag, shared, after):
        _, lands = ici_wait("share", shared, after, name=f"gather{li}{tag}_share_wait")
        full = {k: _unshard_layer(k, land.reshape((4, 2 * land.shape[2], land.shape[3])))
                for k, land in zip(keys, lands)}
        return prep_layer(cfg, full)

    def exchange_start(li, keys, tag, gw, after):
        ug = unprep_grads(cfg, gw)
        g4 = []
        for k in keys:
            s = _to_shards(k, ug[k])
            g4.append(s.reshape(4, 2, s.shape[1] // 2, s.shape[2]))
        lands = [lax.empty((4,) + a.shape[2:], a.dtype) for a in g4]
        return ici_start("exchange", g4, lands, after, name=f"grad{li}{tag}_exchange_start")

    def reduce_start(li, keys, tag, exchanged, after):
        g4, theirs = ici_wait("exchange", exchanged, after, name=f"grad{li}{tag}_exchange_wait")
        parts = [pair_add(a, b, half1, name=f"grad{li}_pair_add_{k}") for k, a, b in zip(keys, g4, theirs)]
        lands = [lax.empty(q.shape, q.dtype) for q in parts]
        return ici_start("scatter", parts, lands, zero_tok, name=f"grad{li}{tag}_scatter_start")

    def reduce_finish(li, keys, tag, started, after):
        parts, lands = ici_wait("scatter", started, after, name=f"grad{li}{tag}_scatter_wait")
        sums = [chip_sum(rc, pt, where2, name=f"grad{li}_chip_sum_{k}") for k, rc, pt in zip(keys, lands, parts)]
        sums = pair_fill(sums, name=f"grad{li}{tag}_pair_fill")
        return {k: s.reshape(2 * s.shape[1], s.shape[2]) for k, s in zip(keys, sums)}

    gathered = gather_chips([meta_tokens, conv_w], name="gather_small")
    p = dict(wts)
    p["meta_tokens"] = jnp.transpose(gathered[0], (1, 0, 2)).reshape(cfg.n_meta, cfg.d)
    p["conv_w"] = jnp.transpose(gathered[1], (1, 2, 0, 3)).reshape(2, cfg.convk, cfg.conv_dim)

    st0a = gather_start(0, ["w_in"], "a", gathered[0])
    st0b = gather_start(0, REST, "b", st0a[4])
    st1 = gather_start(1, BIG, "", st0b[4])
    pw0 = gather_finish(0, ["w_in"], "a", st0a, st1[4])

    bsz, d = cfg.bsz, cfg.d
    lead = jnp.zeros((bsz, cfg.pad, d), F32)
    meta = jnp.broadcast_to(p["meta_tokens"][None], (bsz, cfg.n_meta, d))
    h0 = jnp.concatenate([lead, meta, x], axis=1).reshape(cfg.t, d)
    tabs = rope_tables(cfg)
    sm0 = small_params(cfg, p, 0)
    st = {}

    def step(key, fn):
        def run(arg):
            st[key] = fn(arg)
            return st[key][4]
        return run

    h1, sv0, pw0 = layer_fwd(cfg, h0, pw0, sm0, tabs, 0, hooks={
        "after_conv": step("share0b", lambda after: gather_mid(0, REST, "b", st0b, after)),
        "weights": lambda after: gather_end(0, REST, "b", st["share0b"], after),
        "after_attn": step("share1", lambda after: gather_mid(1, BIG, "", st1, after))})
    pw1 = gather_end(1, BIG, "", st["share1"], h1)
    sm1 = small_params(cfg, p, 1)
    h2, sv1, _ = layer_fwd(cfg, h1, pw1, sm1, tabs, 1)
    loss, dh, dfw = loss_head(cfg, h2, loss_target.reshape(bsz * cfg.seq, d), final_norm_w, name="loss_head")

    dh, gw1, gs1 = layer_bwd(cfg, dh, pw1, sm1, tabs, sv1, 1)
    ex1 = exchange_start(1, BIG, "", gw1, zero_tok)
    sm0b = dict(sm0)
    sm0b["norm_mlp_w"] = sm0["norm_mlp_w"] + ex1[4][0, 0]
    dh, gw0, gs0 = layer_bwd(cfg, dh, pw0, sm0b, tabs, sv0, 0, hooks={
        "after_attn": step("red1", lambda after: reduce_start(1, BIG, "", ex1, after)),
        "early": step("ex0e", lambda gw: exchange_start(0, REST, "e", gw, zero_tok)),
        "after_ssd": step("red0e", lambda after: reduce_start(0, REST, "e", st["ex0e"], after))})
    dh3 = dh.reshape(bsz, cfg.lp, d)
    grad_x = dh3[:, cfg.chunk:, :]
    gmeta = jnp.sum(dh3[:, cfg.pad:cfg.chunk, :], axis=0)
    big1 = reduce_finish(1, BIG, "", st["red1"], dh)
    ex0l = exchange_start(0, ["w_in"], "l", gw0, big1[BIG[-1]])

    small_names = SMALL_REPL + ["conv_w"]
    parts = [jnp.stack([gs0[k], gs1[k]]) for k in small_names] + [dfw, gmeta, loss.reshape(1)]
    shapes = [a.shape for a in parts]
    vec, _ = _pack_small(parts)
    red_vec = allreduce_small(vec, ex0l[4], name="allreduce_small")
    red = _unpack_small(red_vec, shapes)
    sg = dict(zip(small_names + ["final_norm_w", "meta_tokens"], red))
    loss = red[-1].reshape(())
    sg["conv_w"] = lax.dynamic_slice_in_dim(sg["conv_w"], chip * (cfg.conv_dim // 4), cfg.conv_dim // 4, axis=2)
    sg["meta_tokens"] = lax.dynamic_slice_in_dim(sg["meta_tokens"], chip * (cfg.d // 4), cfg.d // 4, axis=1)

    red0 = reduce_start(0, ["w_in"], "l", ex0l, red_vec)
    grads, deltas, new_m, new_v = {}, {}, {}, {}
    dep = red0[4]
    for k in names:
        if k in BIG:
            continue
        w2, g2, m2, v2 = _as2d(wts[k]), _as2d(sg[k]), _as2d(ms[k]), _as2d(vs[k])
        dl, mn, vn = adamw_small(w2, g2, m2, v2, dep, name=f"adamw_{k}")
        grads[k] = sg[k].reshape(wts[k].shape)
        deltas[k], new_m[k], new_v[k] = (t.reshape(wts[k].shape) for t in (dl, mn, vn))

    def view(k, a):
        return jnp.swapaxes(a, 1, 2) if k == "w_in" else a

    def gview(k, g):
        return g.T if k == "w_in" else g

    wv, mv, vv = ({k: view(k, t[k]) for k in BIG} for t in (wts, ms, vs))
    outs = {}
    for k in BIG:
        outs[k] = adamw_layer(wv[k], mv[k], vv[k], gview(k, big1[k]), 1, None, dep, name=f"adamw1_{k}")
        dep = outs[k][1]
    big0 = reduce_finish(0, REST, "e", st["red0e"], dep)
    for k in REST:
        outs[k] = adamw_layer(wv[k], mv[k], vv[k], big0[k], 0, outs[k], dep, name=f"adamw0_{k}")
        dep = outs[k][1]
    big0.update(reduce_finish(0, ["w_in"], "l", red0, dep))
    outs["w_in"] = adamw_layer(wv["w_in"], mv["w_in"], vv["w_in"], gview("w_in", big0["w_in"]), 0, outs["w_in"], dep,
                               name="adamw0_w_in")
    for k in BIG:
        grads[k], deltas[k], new_m[k], new_v[k] = (view(k, t) for t in outs[k])
    return (loss, grad_x, *[grads[k] for k in names], *[deltas[k] for k in names],
            *[new_m[k] for k in names], *[new_v[k] for k in names])


def adamw_small(w, g, m, v, dep, *, name):
    def body(w_ref, g_ref, m_ref, v_ref, dep_ref, d_ref, mo_ref, vo_ref):
        d_ref[...], mo_ref[...], vo_ref[...] = _adam_update(w_ref[...], g_ref[...], m_ref[...], v_ref[...])

    vm = pl.BlockSpec(memory_space=pltpu.VMEM)
    return pl.pallas_call(body, name=name, in_specs=[vm] * 4 + [pl.BlockSpec(memory_space=pl.ANY)], out_specs=[vm] * 3,
                          out_shape=[_sds(w.shape, F32)] * 3, compiler_params=_cp())(w, g, m, v, dep)
```

```python
import functools
import math
from typing import NamedTuple

import numpy as np
import jax
import jax.numpy as jnp
from jax import lax
from jax.experimental import pallas as pl
from jax.experimental.pallas import tpu as pltpu

F32 = jnp.float32
BF16 = jnp.bfloat16
HI = lax.Precision.HIGHEST
EPS = 1e-6
ROPE_THETA = 10000.0
LANE = 128
VMEM_LIMIT = 56 * 1024 * 1024
MASK_VALUE = -1e30
ADAM_LR, ADAM_B1, ADAM_B2, ADAM_EPS, ADAM_WD, ADAM_STEP = 0.001, 0.9, 0.999, 1e-08, 0.01, 10
MESH = pl.DeviceIdType.MESH


class Cfg(NamedTuple):
    d: int = 1024
    seq: int = 2048
    bsz: int = 2
    n_meta: int = 16
    inner: int = 2048
    hd: int = 64
    groups: int = 4
    state: int = 128
    convk: int = 4
    chunk: int = 128
    mh: int = 8
    ql: int = 512
    kvl: int = 256
    nope: int = 128
    rope: int = 64
    vd: int = 128
    ff: int = 4096

    @property
    def heads(self): return self.inner // self.hd
    @property
    def gw(self): return self.inner // self.groups
    @property
    def conv_dim(self): return self.inner + 2 * self.groups * self.state
    @property
    def pad(self): return self.chunk - self.n_meta
    @property
    def lp(self): return self.chunk + self.seq
    @property
    def t(self): return self.bsz * self.lp
    @property
    def nchunks(self): return self.lp // self.chunk
    @property
    def sw(self): return self.ql + self.kvl + 2 * LANE
    @property
    def kt(self): return (self.ql + self.kvl) // LANE
    @property
    def dtt(self): return self.kt + 1
    @property
    def qw(self): return self.mh * 2 * LANE
    @property
    def in_splits(self):
        return [self.inner, self.conv_dim, self.heads, self.ql, self.kvl, self.rope, self.d, self.d]


CFG = Cfg()


def _pick(dim, pref, mult):
    best = None
    for t in range(mult, min(dim, pref) + 1, mult):
        if dim % t == 0:
            best = t
    return best if best is not None else dim


def _cp(**kw):
    return pltpu.CompilerParams(vmem_limit_bytes=VMEM_LIMIT, **kw)


def _sds(shape, dtype):
    return jax.ShapeDtypeStruct(tuple(shape), dtype)


def _silu(x):
    return x * jax.nn.sigmoid(x)


def _dsilu(x):
    s = jax.nn.sigmoid(x)
    return s * (1.0 + x * (1.0 - s))


def _ep_plain(r):
    return (r,)


def _ep_add(r, res):
    return (r + res.astype(F32),)


def _ep_relu2(r):
    rp = jnp.maximum(r, 0.0)
    return r, rp * rp


def _ep_relu2_grad(r, a):
    return (r * (2.0 * jnp.maximum(a.astype(F32), 0.0)),)


MM_VMEM_BUDGET = 44 * 1024 * 1024


def _mm_tiles(m, n, k, a_bytes, b_bytes, io_bytes, ta):
    m_mult, m_cap = (LANE, 1024) if ta else (16, 1088)
    tms = [t for t in range(m_cap, 0, -m_mult) if m % t == 0] or [m]
    tns = [t for t in (1024, 512, 256, 128) if n % t == 0] or [n]
    best = None
    for tm in tms:
        for tn in tns:
            need = 2 * (tm * k * a_bytes + k * tn * b_bytes + tm * tn * io_bytes)
            if need <= MM_VMEM_BUDGET and (best is None or tm * tn > best[0] * best[1]):
                best = (tm, tn)
    if best is None:
        return (_pick(m, 512, m_mult), _pick(n, 512, LANE), _pick(k, 1088 if ta else 1024, 16 if ta else LANE))
    return best[0], best[1], k


def matmul(a, b, *, ta=False, tb=False, out_dtype=F32, add=None, name, tm=None, tn=None, tk=None,
           epilogue=None, extras=(), out_dtypes=None):
    if add is not None:
        epilogue, extras = _ep_add, (add,)
    if epilogue is None:
        epilogue = _ep_plain
    out_dtypes = tuple(out_dtypes) if out_dtypes is not None else (out_dtype,)
    n_ex, n_out = len(extras), len(out_dtypes)
    if ta:
        k_dim, m_dim = a.shape
    else:
        m_dim, k_dim = a.shape
    if tb:
        n_dim, k2 = b.shape
    else:
        k2, n_dim = b.shape
    assert k_dim == k2, (a.shape, b.shape, ta, tb)
    if tm is None and tn is None and tk is None:
        io_bytes = sum(jnp.dtype(e.dtype).itemsize for e in extras) + sum(jnp.dtype(d).itemsize for d in out_dtypes)
        tm, tn, tk = _mm_tiles(m_dim, n_dim, k_dim, jnp.dtype(a.dtype).itemsize, jnp.dtype(b.dtype).itemsize,
                               io_bytes, ta)
    elif ta:
        tm = tm or _pick(m_dim, 1024, LANE)
        tk = tk or _pick(k_dim, 1088, 16)
        tn = tn or _pick(n_dim, 1024, LANE)
    else:
        tm = tm or _pick(m_dim, 1088, 16)
        tk = tk or _pick(k_dim, 1024 if a.dtype == F32 else 2048, LANE)
        tn = tn or _pick(n_dim, 1024, LANE)
    nm, nn, nk = m_dim // tm, n_dim // tn, k_dim // tk
    dn = (((0 if ta else 1,), (1 if tb else 0,)), ((), ()))

    def body(*refs):
        a_ref, b_ref = refs[:2]
        ex_refs = refs[2:2 + n_ex]
        o_refs = refs[2 + n_ex:2 + n_ex + n_out]
        scr = refs[2 + n_ex + n_out:]
        p = lax.dot_general(a_ref[...].astype(BF16), b_ref[...].astype(BF16), dn, preferred_element_type=F32)

        def finish(r):
            outs = epilogue(r, *[e[...] for e in ex_refs])
            for o_ref, val, dt in zip(o_refs, outs, out_dtypes):
                o_ref[...] = val.astype(dt)

        if nk == 1:
            finish(p)
        else:
            acc = scr[0]
            k = pl.program_id(2)

            @pl.when(k == 0)
            def _():
                acc[...] = p

            @pl.when(k > 0)
            def _():
                acc[...] += p

            @pl.when(k == nk - 1)
            def _():
                finish(acc[...])

    a_spec = pl.BlockSpec((tk, tm), lambda i, j, k: (k, i)) if ta else pl.BlockSpec((tm, tk), lambda i, j, k: (i, k))
    b_spec = pl.BlockSpec((tn, tk), lambda i, j, k: (j, k)) if tb else pl.BlockSpec((tk, tn), lambda i, j, k: (k, j))
    o_spec = pl.BlockSpec((tm, tn), lambda i, j, k: (i, j))
    outs = pl.pallas_call(
        body, name=name, grid=(nm, nn, nk), in_specs=[a_spec, b_spec] + [o_spec] * n_ex, out_specs=[o_spec] * n_out,
        out_shape=[_sds((m_dim, n_dim), dt) for dt in out_dtypes],
        scratch_shapes=[pltpu.VMEM((tm, tn), F32)] if nk > 1 else [],
        compiler_params=_cp(dimension_semantics=("parallel", "parallel", "arbitrary")),
    )(a, b, *extras)
    return outs[0] if n_out == 1 else tuple(outs)


def matmul_nt_sum(as_, bs_, *, out_dtype=F32, name, tiles=None):
    m, n = as_[0].shape[0], bs_[0].shape[0]
    ks = [a.shape[1] for a in as_]
    assert [b.shape[1] for b in bs_] == ks
    ksum, cnt = sum(ks), len(ks)
    best = tiles
    for tn in [t for t in (1024, 512, 256, 128) if n % t == 0]:
        for tm in [t for t in range(1088, 0, -16) if m % t == 0]:
            need = 2 * (tm * ksum * 2 + tn * ksum * 2 + tm * tn * jnp.dtype(out_dtype).itemsize)
            if best is None and need <= MM_VMEM_BUDGET and tm >= 256:
                best = (tm, tn)
    tm, tn = best

    def body(*refs):
        a_refs, b_refs, o_ref = refs[:cnt], refs[cnt:2 * cnt], refs[2 * cnt]
        acc = None
        for a_ref, b_ref in zip(a_refs, b_refs):
            p = _nt(a_ref[...].astype(BF16), b_ref[...].astype(BF16))
            acc = p if acc is None else acc + p
        o_ref[...] = acc.astype(out_dtype)

    return pl.pallas_call(
        body, name=name, grid=(n // tn, m // tm),
        in_specs=[pl.BlockSpec((tm, k), lambda j, i: (i, 0)) for k in ks]
        + [pl.BlockSpec((tn, k), lambda j, i: (j, 0)) for k in ks],
        out_specs=pl.BlockSpec((tm, tn), lambda j, i: (i, j)), out_shape=_sds((m, n), out_dtype),
        compiler_params=_cp(dimension_semantics=("parallel", "parallel")),
    )(*as_, *bs_)


def rmsnorm_fwd(x, w, *, cw=None, ci=0, name):
    t = x.shape[0]
    cw = cw or x.shape[1]
    tr = _pick(t, 544, 16)

    def body(x_ref, w_ref, o_ref):
        xv = x_ref[...].astype(F32)
        r = lax.rsqrt(jnp.mean(xv * xv, axis=-1, keepdims=True) + EPS)
        o_ref[...] = (xv * r * w_ref[...]).astype(BF16)

    return pl.pallas_call(
        body, name=name, grid=(t // tr,),
        in_specs=[pl.BlockSpec((tr, cw), lambda i: (i, ci)), pl.BlockSpec((1, cw), lambda i: (0, 0))],
        out_specs=pl.BlockSpec((tr, cw), lambda i: (i, 0)),
        out_shape=_sds((t, cw), BF16), compiler_params=_cp(),
    )(x, w.reshape(1, cw))


def rmsnorm_bwd(dy, x, w, *, cw=None, ci=0, res=None, out_dtype=F32, with_bf16=False, name):
    t = x.shape[0]
    cw = cw or x.shape[1]
    tr = _pick(t, 544, 16)
    has_res = res is not None

    def body(*refs):
        dxb_ref = None
        if with_bf16:
            refs, dxb_ref = refs[:-1], refs[-1]
        if has_res:
            dy_ref, x_ref, w_ref, res_ref, dx_ref, dw_ref = refs
        else:
            dy_ref, x_ref, w_ref, dx_ref, dw_ref = refs
        xv = x_ref[...].astype(F32)
        dyv = dy_ref[...].astype(F32)
        r = lax.rsqrt(jnp.mean(xv * xv, axis=-1, keepdims=True) + EPS)
        xh = xv * r
        g = dyv * w_ref[...]
        dx = r * (g - xh * jnp.mean(g * xh, axis=-1, keepdims=True))
        if has_res:
            dx = dx + res_ref[...]
        dx_ref[...] = dx.astype(out_dtype)
        if with_bf16:
            dxb_ref[...] = dx.astype(BF16)

        @pl.when(pl.program_id(0) == 0)
        def _():
            dw_ref[...] = jnp.zeros_like(dw_ref)

        dw_ref[...] += jnp.sum(dyv * xh, axis=0, keepdims=True)

    row = pl.BlockSpec((tr, cw), lambda i: (i, 0))
    in_specs = [row, pl.BlockSpec((tr, cw), lambda i: (i, ci)), pl.BlockSpec((1, cw), lambda i: (0, 0))]
    args = [dy, x, w.reshape(1, cw)]
    if has_res:
        in_specs.append(row)
        args.append(res)
    outs = pl.pallas_call(
        body, name=name, grid=(t // tr,), in_specs=in_specs,
        out_specs=[row, pl.BlockSpec((1, cw), lambda i: (0, 0))] + ([row] if with_bf16 else []),
        out_shape=[_sds((t, cw), out_dtype), _sds((1, cw), F32)] + ([_sds((t, cw), BF16)] if with_bf16 else []),
        compiler_params=_cp(),
    )(*args)
    if with_bf16:
        return outs[0], outs[1][0], outs[2]
    return outs[0], outs[1][0]


def _shift_down(x, s):
    return x if s == 0 else pltpu.roll(x, s, 0)


def _shift_up(x, s):
    return x if s == 0 else pltpu.roll(x, x.shape[0] - s, 0)


def _conv_pre(x, w_ref, b_ref, kk):
    pre = b_ref[...] + jnp.zeros_like(x)
    for k in range(kk):
        pre = pre + w_ref[k:k + 1, :] * _shift_down(x, kk - 1 - k)
    return pre


def conv_fwd(cfg, xbc, w, b, *, name):
    lp, cd, kk = cfg.lp, cfg.conv_dim, cfg.convk
    assert cfg.pad >= kk - 1
    cb = _pick(cd, 512, LANE)

    def body(x_ref, w_ref, b_ref, o_ref, ds_ref):
        pre = _conv_pre(x_ref[...], w_ref, b_ref, kk)
        sg = jax.nn.sigmoid(pre)
        o_ref[...] = pre * sg
        ds_ref[...] = (sg * (1.0 + pre * (1.0 - sg))).astype(BF16)

    blk = pl.BlockSpec((lp, cb), lambda j, bb: (bb, j))
    return pl.pallas_call(
        body, name=name, grid=(cd // cb, cfg.bsz),
        in_specs=[blk, pl.BlockSpec((kk, cb), lambda j, bb: (0, j)), pl.BlockSpec((1, cb), lambda j, bb: (0, j))],
        out_specs=[blk, blk], out_shape=[_sds((cfg.t, cd), F32), _sds((cfg.t, cd), BF16)], compiler_params=_cp(),
    )(xbc, w, b.reshape(1, cd))


def conv_bwd(cfg, xbc, w, dsilu, dxc, *, name):
    lp, cd, kk = cfg.lp, cfg.conv_dim, cfg.convk
    cb = _pick(cd, 512, LANE)

    def body(x_ref, w_ref, s_ref, d_ref, dx_ref, dw_ref, db_ref):
        x = x_ref[...]
        dpre = d_ref[...] * s_ref[...].astype(F32)
        dx = jnp.zeros_like(x)
        dws = []
        for k in range(kk):
            s = kk - 1 - k
            dx = dx + w_ref[k:k + 1, :] * _shift_up(dpre, s)
            dws.append(jnp.sum(dpre * _shift_down(x, s), axis=0, keepdims=True))
        dx_ref[...] = dx.astype(BF16)

        @pl.when(pl.program_id(1) == 0)
        def _():
            dw_ref[...] = jnp.zeros_like(dw_ref)
            db_ref[...] = jnp.zeros_like(db_ref)

        for k in range(kk):
            dw_ref[k:k + 1, :] += dws[k]
        db_ref[...] += jnp.sum(dpre, axis=0, keepdims=True)

    blk = pl.BlockSpec((lp, cb), lambda j, bb: (bb, j))
    wsp = pl.BlockSpec((kk, cb), lambda j, bb: (0, j))
    bsp = pl.BlockSpec((1, cb), lambda j, bb: (0, j))
    dx, dw, db = pl.pallas_call(
        body, name=name, grid=(cd // cb, cfg.bsz),
        in_specs=[blk, wsp, blk, blk], out_specs=[blk, wsp, bsp],
        out_shape=[_sds((cfg.t, cd), BF16), _sds((kk, cd), F32), _sds((1, cd), F32)], compiler_params=_cp(),
    )(xbc, w, dsilu, dxc)
    return dx, dw, db[0]


def _softplus(x):
    return jnp.maximum(x, 0.0) + jnp.log(1.0 + jnp.exp(-jnp.abs(x)))


def _ssd_consts(cfg):
    q = cfg.chunk
    i0 = np.arange(q)[:, None]
    i1 = np.arange(q)[None, :]
    ltri = (i1 <= i0).astype(np.float32)
    rexp = np.zeros((LANE, cfg.inner), np.float32)
    for h in range(cfg.heads):
        rexp[h, h * cfg.hd:(h + 1) * cfg.hd] = 1.0
    return jnp.asarray(ltri), jnp.asarray(rexp)


def _sel_dot(x, m, *, passes=2, left=False, trans=False):
    mb = m.astype(BF16)
    acc, rem = None, x
    for _ in range(passes):
        piece = rem.astype(BF16)
        if not left:
            part = _nn(piece, mb)
        elif trans:
            part = _tn(mb, piece)
        else:
            part = _nn(mb, piece)
        acc = part if acc is None else acc + part
        rem = rem - piece.astype(F32)
    return acc


def _ssd_chunk_common(cfg, raw, bias, avec, c_idx, ltri, rexp):
    q = cfg.chunk
    rows = lax.broadcasted_iota(jnp.int32, (q, LANE), 0)
    live = jnp.logical_or(c_idx > 0, rows >= cfg.pad)
    pre = raw + bias
    dt = jnp.where(live, _softplus(pre), 0.0)
    adt = dt * avec
    cs = _sel_dot(adt, ltri, passes=3, left=True)
    cs_t = cs.T
    cs_last = cs[q - 1:q, :]
    e_in = jnp.exp(cs)
    w0 = jnp.exp(cs_last - cs)
    decay = jnp.exp(cs_last)
    return dict(live=live, pre=pre, dt=dt, adt=adt, cs=cs, cs_t=cs_t, e_in=e_in, w0=w0, decay=decay,
                DT=_sel_dot(dt, rexp), E=_sel_dot(e_in, rexp), W0=_sel_dot(w0, rexp),
                DEC=_sel_dot(jnp.broadcast_to(decay, (8, LANE)), rexp)[0:1, :])


def _tri_masks(q):
    r = lax.broadcasted_iota(jnp.int32, (q, q), 0)
    c = lax.broadcasted_iota(jnp.int32, (q, q), 1)
    return c <= r, r <= c


def _head_l(cq, h, tri, tri_t):
    col = cq["cs"][:, h:h + 1]
    row = cq["cs_t"][h:h + 1, :]
    lmat = jnp.where(tri, jnp.exp(jnp.minimum(col - row, 0.0)), 0.0)
    lmat_t = jnp.where(tri_t, jnp.exp(jnp.minimum(row - col, 0.0)), 0.0)
    return lmat, lmat_t


def _nt(a, b):
    return lax.dot_general(a, b, (((1,), (1,)), ((), ())), preferred_element_type=F32)


def _tn(a, b):
    return lax.dot_general(a, b, (((0,), (0,)), ((), ())), preferred_element_type=F32)


def _nn(a, b):
    return jnp.dot(a, b, preferred_element_type=F32)


def ssd_fwd(cfg, xc, small, dt_bias, avec, dexp, *, name):
    q, inner, st, gw, g_n = cfg.chunk, cfg.inner, cfg.state, cfg.gw, cfg.groups
    nc = cfg.nchunks
    ltri, rexp = _ssd_consts(cfg)
    hpt = LANE // cfg.hd
    tiles_per_group = gw // LANE

    bsz, lp = cfg.bsz, cfg.lp
    bcw = g_n * st

    def body(x_ref, b_ref, c_ref, dt_ref, bias_ref, a_ref, d_ref, ltri_ref, rexp_ref, y_ref, sin_ref, s_scr):
        c_idx = pl.program_id(0)

        @pl.when(c_idx == 0)
        def _():
            s_scr[...] = jnp.zeros_like(s_scr)

        ltri_v = ltri_ref[...]
        tri, tri_t = _tri_masks(q)
        lane = lax.broadcasted_iota(jnp.int32, (q, LANE), 1)
        for bi in range(bsz):
            cq = _ssd_chunk_common(cfg, dt_ref[bi], bias_ref[...], a_ref[...], c_idx, ltri_v, rexp_ref[...])
            xs = x_ref[bi]
            xdt = (xs * cq["DT"]).astype(BF16)
            xw = (xs * cq["DT"] * cq["W0"]).astype(BF16)
            s_in = s_scr[bi]
            sin_ref[bi, 0] = s_in
            for g in range(g_n):
                bg = b_ref[bi, :, g * st:(g + 1) * st].astype(BF16)
                cg = c_ref[bi, :, g * st:(g + 1) * st].astype(BF16)
                gmat = _nt(cg, bg)
                gs = slice(g * gw, (g + 1) * gw)
                y0 = _nn(cg, s_in[:, gs].astype(BF16))
                for tt in range(tiles_per_group):
                    tile = g * tiles_per_group + tt
                    ts = slice(tile * LANE, (tile + 1) * LANE)
                    xt = xdt[:, ts]
                    ms, xh = [], []
                    for hh in range(hpt):
                        lmat, _ = _head_l(cq, tile * hpt + hh, tri, tri_t)
                        ms.append((gmat * lmat).astype(BF16))
                        inhead = jnp.logical_and(lane >= hh * cfg.hd, lane < (hh + 1) * cfg.hd)
                        xh.append(jnp.where(inhead, xt, jnp.zeros_like(xt)))
                    yd = _nn(jnp.concatenate(ms, axis=1), jnp.concatenate(xh, axis=0))
                    y_ref[bi, :, ts] = (yd + y0[:, tt * LANE:(tt + 1) * LANE] * cq["E"][:, ts]
                                        + xs[:, ts] * d_ref[:, ts]).astype(BF16)
                s_scr[bi, :, gs] = s_in[:, gs] * cq["DEC"][:, gs] + _tn(bg, xw[:, gs])

    def rowblk(width, col):
        return pl.BlockSpec((bsz, q, width), lambda c: (0, c, col))

    def const(shape):
        return pl.BlockSpec(shape, lambda c: (0, 0))

    xc3 = xc.reshape(bsz, lp, cfg.conv_dim)
    y, sin = pl.pallas_call(
        body, name=name, grid=(nc,),
        in_specs=[rowblk(inner, 0), rowblk(bcw, inner // bcw), rowblk(bcw, inner // bcw + 1),
                  rowblk(LANE, cfg.dtt), const((1, LANE)), const((1, LANE)), const((1, inner)),
                  const((q, q)), const((LANE, inner))],
        out_specs=[rowblk(inner, 0), pl.BlockSpec((bsz, 1, st, inner), lambda c: (0, c, 0, 0))],
        out_shape=[_sds((bsz, lp, inner), BF16), _sds((bsz, nc, st, inner), F32)],
        scratch_shapes=[pltpu.VMEM((bsz, st, inner), F32)], compiler_params=_cp(),
    )(xc3, xc3, xc3, small.reshape(bsz, lp, cfg.sw), dt_bias, avec, dexp, ltri, rexp)
    return y.reshape(cfg.t, inner), sin.reshape(bsz * nc, st, inner)


def ssd_bwd(cfg, xc, small, dt_bias, avec, dexp, sin, dy, *, name):
    q, inner, st, gw, g_n = cfg.chunk, cfg.inner, cfg.state, cfg.gw, cfg.groups
    nc = cfg.nchunks
    ltri, rexp = _ssd_consts(cfg)
    rexp_t = rexp.T
    hpt = LANE // cfg.hd
    tiles_per_group = gw // LANE
    bcw = g_n * st

    def body(x_ref, b_ref, c_ref, dt_ref, bias_ref, a_ref, d_ref, ltri_ref, rexp_ref, rexpt_ref, sin_ref, dy_ref,
             dx_ref, ddt_ref, dd_ref, da_ref, dbias_ref, ds_scr):
        step = pl.program_id(1)
        c_idx = nc - 1 - step

        @pl.when(step == 0)
        def _():
            ds_scr[...] = jnp.zeros_like(ds_scr)

        @pl.when(jnp.logical_and(step == 0, pl.program_id(0) == 0))
        def _():
            dd_ref[...] = jnp.zeros_like(dd_ref)
            da_ref[...] = jnp.zeros_like(da_ref)
            dbias_ref[...] = jnp.zeros_like(dbias_ref)

        ltri_v = ltri_ref[...]
        tri, tri_t = _tri_masks(q)
        red = _sel_dot
        rexpt = rexpt_ref[...]
        cq = _ssd_chunk_common(cfg, dt_ref[...], bias_ref[...], a_ref[...], c_idx, ltri_v, rexp_ref[...])
        xs = x_ref[...]
        dyv = dy_ref[...].astype(F32)
        s_in = sin_ref[0]
        d_s = ds_scr[...]
        xdt_f = xs * cq["DT"]
        xdt = xdt_f.astype(BF16)
        xw_f = xdt_f * cq["W0"]
        xw = xw_f.astype(BF16)
        lane = lax.broadcasted_iota(jnp.int32, (q, LANE), 1)
        sub = lax.broadcasted_iota(jnp.int32, (LANE, q), 0)

        dd_ref[...] += jnp.sum(dyv * xs, axis=0, keepdims=True)
        dy0 = dyv * cq["E"]
        dcs = jnp.zeros((q, LANE), F32)
        dcs_t = jnp.zeros((LANE, q), F32)
        for g in range(g_n):
            bg_f = b_ref[:, g * st:(g + 1) * st]
            cg_f = c_ref[:, g * st:(g + 1) * st]
            bg = bg_f.astype(BF16)
            cg = cg_f.astype(BF16)
            gs = slice(g * gw, (g + 1) * gw)
            gmat = _nt(cg, bg)
            gmat_t = _nt(bg, cg)
            sing = s_in[:, gs].astype(BF16)
            dsg = d_s[:, gs].astype(BF16)
            y0 = _nn(cg, sing)
            dxw = _nn(bg, dsg)
            d_bg = _nt(xw[:, gs], dsg)
            d_cg = _nt(dy0[:, gs].astype(BF16), sing)
            ds_in_g = _tn(cg, dy0[:, gs].astype(BF16))
            dg = jnp.zeros((q, q), F32)
            dxdt_g = []
            for tt in range(tiles_per_group):
                tile = g * tiles_per_group + tt
                ts = slice(tile * LANE, (tile + 1) * LANE)
                xt = xdt[:, ts]
                dyt = dyv[:, ts]
                dyhs, lmats, mts = [], [], []
                for hh in range(hpt):
                    lmat, lmat_t = _head_l(cq, tile * hpt + hh, tri, tri_t)
                    inhead = jnp.logical_and(lane >= hh * cfg.hd, lane < (hh + 1) * cfg.hd)
                    dyhs.append(jnp.where(inhead, dyt, 0.0).astype(BF16))
                    lmats.append(lmat)
                    mts.append((gmat_t * lmat_t).astype(BF16))
                dy_stack = jnp.concatenate(dyhs, axis=0)
                dm_all = _nt(dy_stack, xt)
                for hh in range(hpt):
                    h = tile * hpt + hh
                    dm = dm_all[hh * q:(hh + 1) * q, :]
                    dg = dg + dm * lmats[hh]
                    qm = dm * gmat * lmats[hh]
                    rs = jnp.sum(qm, axis=1, keepdims=True)
                    csum = jnp.sum(qm, axis=0, keepdims=True)
                    dcs = dcs + jnp.where(lane == h, rs, 0.0)
                    dcs_t = dcs_t + jnp.where(sub == h, csum, 0.0)
                dxdt_g.append(_nn(jnp.concatenate(mts, axis=1), dy_stack))
            dxdt_diag = jnp.concatenate(dxdt_g, axis=1) if len(dxdt_g) > 1 else dxdt_g[0]
            dgb = dg.astype(BF16)
            d_cg = d_cg + _nn(dgb, bg)
            d_bg = d_bg + _tn(dgb, cg)
            dx_ref[:, inner + g * st:inner + (g + 1) * st] = d_bg
            dx_ref[:, inner + bcw + g * st:inner + bcw + (g + 1) * st] = d_cg
            dxdt = dxdt_diag + dxw * cq["W0"][:, gs]
            dx_ref[:, gs] = dyv[:, gs] * d_ref[:, gs] + dxdt * cq["DT"][:, gs]
            rt = rexpt[gs, :]
            dcs = dcs + red(dyv[:, gs] * y0 * cq["E"][:, gs], rt)
            r_w = red(dxw * xw_f[:, gs], rt)
            dcs = dcs - r_w
            dcs_last_g = jnp.sum(r_w, axis=0, keepdims=True)
            ddec = red(jnp.broadcast_to(jnp.sum(d_s[:, gs] * s_in[:, gs], axis=0, keepdims=True), (8, gw)), rt)[0:1, :]
            dcs_last_g = dcs_last_g + ddec * cq["decay"]
            dcs = dcs + jnp.where(lax.broadcasted_iota(jnp.int32, (q, LANE), 0) == q - 1, dcs_last_g, 0.0)
            ddt_part = red(dxdt * xs[:, gs], rt)
            if g == 0:
                ddt = ddt_part
            else:
                ddt = ddt + ddt_part
            ds_scr[:, gs] = d_s[:, gs] * cq["DEC"][:, gs] + ds_in_g
        dcs = dcs - dcs_t.T
        dadt = _sel_dot(dcs, ltri_v, left=True, trans=True)
        ddt = ddt + dadt * a_ref[...]
        da_ref[...] += jnp.sum(dadt * cq["dt"], axis=0, keepdims=True)
        draw = jnp.where(cq["live"], ddt * jax.nn.sigmoid(cq["pre"]), 0.0)
        ddt_ref[...] = draw
        dbias_ref[...] += jnp.sum(draw, axis=0, keepdims=True)

    def rowblk(width, col):
        return pl.BlockSpec((q, width), lambda b, s: (b * nc + nc - 1 - s, col))

    def const(shape):
        return pl.BlockSpec(shape, lambda b, s: (0, 0))

    bcol = inner // bcw
    outs = pl.pallas_call(
        body, name=name, grid=(cfg.bsz, nc),
        in_specs=[rowblk(inner, 0), rowblk(bcw, bcol), rowblk(bcw, bcol + 1), rowblk(LANE, cfg.dtt),
                  const((1, LANE)), const((1, LANE)), const((1, inner)), const((q, q)), const((LANE, inner)),
                  const((inner, LANE)),
                  pl.BlockSpec((1, st, inner), lambda b, s: (b * nc + nc - 1 - s, 0, 0)), rowblk(inner, 0)],
        out_specs=[rowblk(cfg.conv_dim, 0), rowblk(LANE, 0),
                   const((1, inner)), const((1, LANE)), const((1, LANE))],
        out_shape=[_sds((cfg.t, cfg.conv_dim), F32),
                   _sds((cfg.t, LANE), F32), _sds((1, inner), F32), _sds((1, LANE), F32), _sds((1, LANE), F32)],
        scratch_shapes=[pltpu.VMEM((st, inner), F32)], compiler_params=_cp(),
    )(xc, xc, xc, small, dt_bias, avec, dexp, ltri, rexp, rexp_t, sin, dy)
    return outs


def tail_fwd(cfg, y, z, w, *, name):
    t, inner, gw = cfg.t, cfg.inner, cfg.gw
    tr = _pick(t, 272, 16)

    def body(y_ref, z_ref, w_ref, o_ref):
        for g in range(cfg.groups):
            gs = slice(g * gw, (g + 1) * gw)
            yg = y_ref[:, gs].astype(F32) * _silu(z_ref[:, gs].astype(F32))
            r = lax.rsqrt(jnp.mean(yg * yg, axis=-1, keepdims=True) + EPS)
            o_ref[:, gs] = (yg * r * w_ref[:, gs]).astype(BF16)

    row = pl.BlockSpec((tr, inner), lambda i: (i, 0))
    return pl.pallas_call(
        body, name=name, grid=(t // tr,), in_specs=[row, row, pl.BlockSpec((1, inner), lambda i: (0, 0))],
        out_specs=row, out_shape=_sds((t, inner), BF16), compiler_params=_cp(),
    )(y, z, w.reshape(1, inner))


def tail_bwd(cfg, do, y, z, w, *, name):
    t, inner, gw = cfg.t, cfg.inner, cfg.gw
    tr = _pick(t, 272, 16)

    def body(do_ref, y_ref, z_ref, w_ref, dy_ref, dz_ref, dw_ref):
        @pl.when(pl.program_id(0) == 0)
        def _():
            dw_ref[...] = jnp.zeros_like(dw_ref)

        for g in range(cfg.groups):
            gs = slice(g * gw, (g + 1) * gw)
            yv = y_ref[:, gs].astype(F32)
            zv = z_ref[:, gs].astype(F32)
            dov = do_ref[:, gs].astype(F32)
            sz = _silu(zv)
            yg = yv * sz
            r = lax.rsqrt(jnp.mean(yg * yg, axis=-1, keepdims=True) + EPS)
            xh = yg * r
            gg = dov * w_ref[:, gs]
            dyg = r * (gg - xh * jnp.mean(gg * xh, axis=-1, keepdims=True))
            dw_ref[:, gs] += jnp.sum(dov * xh, axis=0, keepdims=True)
            dy_ref[:, gs] = (dyg * sz).astype(BF16)
            dz_ref[:, gs] = (dyg * yv * _dsilu(zv)).astype(BF16)

    row = pl.BlockSpec((tr, inner), lambda i: (i, 0))
    vec = pl.BlockSpec((1, inner), lambda i: (0, 0))
    dy, dz, dw = pl.pallas_call(
        body, name=name, grid=(t // tr,), in_specs=[row, row, row, vec], out_specs=[row, row, vec],
        out_shape=[_sds((t, inner), BF16), _sds((t, inner), BF16), _sds((1, inner), F32)], compiler_params=_cp(),
    )(do, y, z, w.reshape(1, inner))
    return dy, dz, dw[0]


def rope_tables(cfg):
    half = cfg.rope // 2
    pos = np.maximum(np.arange(cfg.lp) - cfg.pad, 0).astype(np.float32)
    inv = ROPE_THETA ** (-jnp.arange(0, cfg.rope, 2, dtype=F32) / cfg.rope)
    ang = jnp.asarray(pos)[:, None] * inv[None, :]
    cos, sin = jnp.cos(ang), jnp.sin(ang)
    zero = jnp.zeros((cfg.lp, LANE - 2 * half), F32)
    zh = jnp.zeros((cfg.lp, half), F32)
    ctab = jnp.concatenate([cos, cos, zero], axis=1)
    s1 = jnp.concatenate([-sin, zh, zero], axis=1)
    s2 = jnp.concatenate([zh, sin, zero], axis=1)
    return ctab, s1, s2


def _rope(x, c, s1, s2, half):
    return x * c + pltpu.roll(x, LANE - half, 1) * s1 + pltpu.roll(x, half, 1) * s2


def _rope_t(dy, c, s1, s2, half):
    return dy * c + pltpu.roll(dy * s1, half, 1) + pltpu.roll(dy * s2, LANE - half, 1)


def _attn_scale(cfg):
    return (cfg.nope + cfg.rope) ** -0.5


def rope_fwd(cfg, qf, small, tabs, *, name):
    t, qw, lp = cfg.t, cfg.qw, cfg.lp
    tr = _pick(lp, 544, 16)
    nrb = lp // tr
    half = cfg.rope // 2
    scale = _attn_scale(cfg)

    def body(q_ref, k_ref, c_ref, s1_ref, s2_ref, qo_ref, ko_ref):
        c, s1, s2 = c_ref[...], s1_ref[...], s2_ref[...]
        for h in range(cfg.mh):
            a = h * 2 * LANE
            qo_ref[:, a:a + LANE] = (q_ref[:, a:a + LANE].astype(F32) * scale).astype(BF16)
            qo_ref[:, a + LANE:a + 2 * LANE] = (
                _rope(q_ref[:, a + LANE:a + 2 * LANE].astype(F32), c, s1, s2, half) * scale).astype(BF16)
        ko_ref[...] = _rope(k_ref[...], c, s1, s2, half).astype(BF16)

    tab = pl.BlockSpec((tr, LANE), lambda i: (i % nrb, 0))
    return pl.pallas_call(
        body, name=name, grid=(t // tr,),
        in_specs=[pl.BlockSpec((tr, qw), lambda i: (i, 0)), pl.BlockSpec((tr, LANE), lambda i: (i, cfg.kt)), tab, tab, tab],
        out_specs=[pl.BlockSpec((tr, qw), lambda i: (i, 0)), pl.BlockSpec((tr, LANE), lambda i: (i, 0))],
        out_shape=[_sds((t, qw), BF16), _sds((t, LANE), BF16)], compiler_params=_cp(),
    )(qf, small, *tabs)


def rope_bwd(cfg, dq, dkpe, tabs, *, name):
    t, qw, lp = cfg.t, cfg.qw, cfg.lp
    tr = _pick(lp, 544, 16)
    nrb = lp // tr
    half = cfg.rope // 2
    scale = _attn_scale(cfg)

    def body(dq_ref, dk_ref, c_ref, s1_ref, s2_ref, qo_ref, ko_ref):
        c, s1, s2 = c_ref[...], s1_ref[...], s2_ref[...]
        for h in range(cfg.mh):
            a = h * 2 * LANE
            qo_ref[:, a:a + LANE] = (dq_ref[:, a:a + LANE].astype(F32) * scale).astype(BF16)
            qo_ref[:, a + LANE:a + 2 * LANE] = _rope_t(
                dq_ref[:, a + LANE:a + 2 * LANE].astype(F32) * scale, c, s1, s2, half).astype(BF16)
        dk = dk_ref[0]
        for h in range(1, cfg.mh):
            dk = dk + dk_ref[h]
        ko_ref[...] = _rope_t(dk, c, s1, s2, half)

    tab = pl.BlockSpec((tr, LANE), lambda i: (i % nrb, 0))
    return pl.pallas_call(
        body, name=name, grid=(t // tr,),
        in_specs=[pl.BlockSpec((tr, qw), lambda i: (i, 0)), pl.BlockSpec((cfg.mh, tr, LANE), lambda i: (0, i, 0)),
                  tab, tab, tab],
        out_specs=[pl.BlockSpec((tr, qw), lambda i: (i, 0)), pl.BlockSpec((tr, LANE), lambda i: (i, 0))],
        out_shape=[_sds((t, qw), BF16), _sds((t, LANE), F32)], compiler_params=_cp(),
    )(dq, dkpe, *tabs)


def _q_blocks(cfg):
    bounds = [0, cfg.chunk] + list(range(cfg.chunk + 256, cfg.lp + 1, 256))
    assert bounds[-1] == cfg.lp, "SEQ must be a multiple of 256"
    return list(zip(bounds[:-1], bounds[1:]))


def _attn_mask(cfg, qs, qe):
    rows = qs + lax.broadcasted_iota(jnp.int32, (qe - qs, qe), 0)
    cols = lax.broadcasted_iota(jnp.int32, (qe - qs, qe), 1)
    return jnp.logical_and(cols <= rows, jnp.logical_or(cols >= cfg.pad, rows < cfg.pad))


def _max_q_block(cfg):
    return max(qe - qs for qs, qe in _q_blocks(cfg))


def _masked_scores(cfg, q, k2, qs, qe, s_scr):
    bq, n = qe - qs, qe
    s_scr[0:bq, 0:n] = _nt(q, k2)
    if qs == 0:
        s_scr[0:bq, 0:n] = jnp.where(_attn_mask(cfg, 0, qe), s_scr[0:bq, 0:n], MASK_VALUE)
    else:
        assert qs >= cfg.chunk and cfg.pad < LANE
        cols = lax.broadcasted_iota(jnp.int32, (bq, LANE), 1)
        s_scr[0:bq, 0:LANE] = jnp.where(cols >= cfg.pad, s_scr[0:bq, 0:LANE], MASK_VALUE)
        r = lax.broadcasted_iota(jnp.int32, (bq, bq), 0)
        c = lax.broadcasted_iota(jnp.int32, (bq, bq), 1)
        s_scr[0:bq, qs:qe] = jnp.where(c <= r, s_scr[0:bq, qs:qe], MASK_VALUE)
    return s_scr[0:bq, 0:n]


def attn_fwd(cfg, qr, kv, kpe, *, name):
    lp, t, mh = cfg.lp, cfg.t, cfg.mh
    blocks = _q_blocks(cfg)

    def body(q_ref, kv_ref, kp_ref, o_ref, l_ref, s_scr):
        for qs, qe in blocks:
            n = qe
            q = q_ref[qs:qe, :]
            k2 = jnp.concatenate([kv_ref[0:n, 0:LANE], kp_ref[0:n, :]], axis=1)
            s = _masked_scores(cfg, q, k2, qs, qe, s_scr)
            m = jnp.max(s, axis=-1, keepdims=True)
            p = jnp.exp(s - m)
            l = jnp.sum(p, axis=-1, keepdims=True)
            o_ref[qs:qe, :] = (_nn(p.astype(BF16), kv_ref[0:n, LANE:2 * LANE]) * (1.0 / l)).astype(BF16)
            l_ref[qs:qe, :] = jnp.broadcast_to(m + jnp.log(l), (qe - qs, LANE))

    hb = pl.BlockSpec((lp, 2 * LANE), lambda b, h: (b, h))
    ob = pl.BlockSpec((lp, LANE), lambda b, h: (b, h))
    return pl.pallas_call(
        body, name=name, grid=(cfg.bsz, mh),
        in_specs=[hb, hb, pl.BlockSpec((lp, LANE), lambda b, h: (b, 0))], out_specs=[ob, ob],
        out_shape=[_sds((t, mh * LANE), BF16), _sds((t, mh * LANE), F32)],
        scratch_shapes=[pltpu.VMEM((_max_q_block(cfg), lp), F32)], compiler_params=_cp(),
    )(qr, kv, kpe)


def attn_bwd(cfg, qr, kv, kpe, o, lse, do, *, name):
    lp, t, mh = cfg.lp, cfg.t, cfg.mh
    blocks = _q_blocks(cfg)

    def body(q_ref, kv_ref, kp_ref, o_ref, l_ref, do_ref, dq_ref, dkv_ref, dkp_ref, dk_acc, dv_acc, s_scr):
        dk_acc[...] = jnp.zeros_like(dk_acc)
        dv_acc[...] = jnp.zeros_like(dv_acc)
        for qs, qe in blocks:
            n = qe
            q = q_ref[qs:qe, :]
            k2 = jnp.concatenate([kv_ref[0:n, 0:LANE], kp_ref[0:n, :]], axis=1)
            dob = do_ref[qs:qe, :].astype(BF16)
            delta = jnp.sum(dob.astype(F32) * o_ref[qs:qe, :].astype(F32), axis=-1, keepdims=True)
            s = _masked_scores(cfg, q, k2, qs, qe, s_scr)
            p = jnp.exp(s - l_ref[qs:qe, 0:1])
            dp = _nt(dob, kv_ref[0:n, LANE:2 * LANE])
            ds = (p * (dp - delta)).astype(BF16)
            dq_ref[qs:qe, :] = _nn(ds, k2).astype(BF16)
            dv_acc[0:n, :] += _tn(p.astype(BF16), dob)
            dk_acc[0:n, :] += _tn(ds, q)
        dkv_ref[:, 0:LANE] = dk_acc[:, 0:LANE].astype(BF16)
        dkv_ref[:, LANE:2 * LANE] = dv_acc[...].astype(BF16)
        dkp_ref[0] = dk_acc[:, LANE:2 * LANE]

    hb = pl.BlockSpec((lp, 2 * LANE), lambda b, h: (b, h))
    ob = pl.BlockSpec((lp, LANE), lambda b, h: (b, h))
    return pl.pallas_call(
        body, name=name, grid=(cfg.bsz, mh),
        in_specs=[hb, hb, pl.BlockSpec((lp, LANE), lambda b, h: (b, 0)), ob, ob, ob],
        out_specs=[hb, hb, pl.BlockSpec((1, lp, LANE), lambda b, h: (h, b, 0))],
        out_shape=[_sds((t, cfg.qw), BF16), _sds((t, mh * 2 * LANE), BF16), _sds((mh, t, LANE), F32)],
        scratch_shapes=[pltpu.VMEM((lp, 2 * LANE), F32), pltpu.VMEM((lp, LANE), F32),
                        pltpu.VMEM((_max_q_block(cfg), lp), F32)], compiler_params=_cp(),
    )(qr, kv, kpe, o, lse, do)


def _live_rows(cfg, tr, shape):
    rows = pl.program_id(1) * tr + lax.broadcasted_iota(jnp.int32, shape, 0)
    return rows >= cfg.pad


def gate_fwd(cfg, ya, yb, g, *, name):
    d, lp = cfg.d, cfg.lp
    tr = _pick(lp, 544, 16)
    nrb = lp // tr

    def body(ya_ref, yb_ref, ga_ref, gb_ref, o_ref):
        f = lambda ref: ref[...].astype(F32)
        mix = jax.nn.sigmoid(f(ga_ref)) * f(ya_ref) + jax.nn.sigmoid(f(gb_ref)) * f(yb_ref)
        o_ref[...] = jnp.where(_live_rows(cfg, tr, mix.shape), mix, 0.0).astype(BF16)

    row = pl.BlockSpec((tr, d), lambda b, j: (b * nrb + j, 0))
    row1 = pl.BlockSpec((tr, d), lambda b, j: (b * nrb + j, 1))
    return pl.pallas_call(
        body, name=name, grid=(cfg.bsz, nrb), in_specs=[row, row, row, row1], out_specs=row,
        out_shape=_sds((cfg.t, d), BF16), compiler_params=_cp(),
    )(ya, yb, g, g)


def gate_bwd(cfg, dmix, ya, yb, g, *, name):
    d, lp = cfg.d, cfg.lp
    tr = _pick(lp, 544, 16)
    nrb = lp // tr

    def body(dm_ref, ya_ref, yb_ref, ga_ref, gb_ref, dya_ref, dyb_ref, dg_ref):
        dm = dm_ref[...].astype(F32)
        dm = jnp.where(_live_rows(cfg, tr, dm.shape), dm, 0.0)
        sa = jax.nn.sigmoid(ga_ref[...].astype(F32))
        sb = jax.nn.sigmoid(gb_ref[...].astype(F32))
        dya_ref[...] = (dm * sa).astype(BF16)
        dyb_ref[...] = (dm * sb).astype(BF16)
        dg_ref[:, 0:d] = (dm * ya_ref[...].astype(F32) * sa * (1.0 - sa)).astype(BF16)
        dg_ref[:, d:2 * d] = (dm * yb_ref[...].astype(F32) * sb * (1.0 - sb)).astype(BF16)

    row = pl.BlockSpec((tr, d), lambda b, j: (b * nrb + j, 0))
    row1 = pl.BlockSpec((tr, d), lambda b, j: (b * nrb + j, 1))
    row2 = pl.BlockSpec((tr, 2 * d), lambda b, j: (b * nrb + j, 0))
    return pl.pallas_call(
        body, name=name, grid=(cfg.bsz, nrb), in_specs=[row, row, row, row, row1], out_specs=[row, row, row2],
        out_shape=[_sds((cfg.t, d), BF16), _sds((cfg.t, d), BF16), _sds((cfg.t, 2 * d), BF16)], compiler_params=_cp(),
    )(dmix, ya, yb, g, g)


def loss_head(cfg, h, target, w, *, name):
    d, q, nc = cfg.d, cfg.chunk, cfg.nchunks
    tpb = cfg.seq // q

    def body(h_ref, t_ref, w_ref, loss_ref, dh_ref, dw_ref, dhb_ref):
        j = pl.program_id(1)

        @pl.when(jnp.logical_and(j == 0, pl.program_id(0) == 0))
        def _():
            loss_ref[...] = jnp.zeros_like(loss_ref)
            dw_ref[...] = jnp.zeros_like(dw_ref)

        @pl.when(j == 0)
        def _():
            dh_ref[...] = jnp.zeros_like(dh_ref)
            dhb_ref[...] = jnp.zeros_like(dhb_ref)

        @pl.when(j > 0)
        def _():
            xv = h_ref[...]
            r = lax.rsqrt(jnp.mean(xv * xv, axis=-1, keepdims=True) + EPS)
            xh = xv * r
            err = xh * w_ref[...] - t_ref[...]
            loss_ref[...] += 0.5 * jnp.sum(jnp.sum(err * err, axis=-1, keepdims=True) / d, axis=0, keepdims=True)
            dy = err * (1.0 / d)
            g = dy * w_ref[...]
            dh = r * (g - xh * jnp.mean(g * xh, axis=-1, keepdims=True))
            dh_ref[...] = dh
            dhb_ref[...] = dh.astype(BF16)
            dw_ref[...] += jnp.sum(dy * xh, axis=0, keepdims=True)

    row = pl.BlockSpec((q, d), lambda b, j: (b * nc + j, 0))
    loss, dh, dw, dhb = pl.pallas_call(
        body, name=name, grid=(cfg.bsz, nc),
        in_specs=[row, pl.BlockSpec((q, d), lambda b, j: (b * tpb + jnp.maximum(j - 1, 0), 0)),
                  pl.BlockSpec((1, d), lambda b, j: (0, 0))],
        out_specs=[pl.BlockSpec((8, LANE), lambda b, j: (0, 0)), row, pl.BlockSpec((1, d), lambda b, j: (0, 0)), row],
        out_shape=[_sds((8, LANE), F32), _sds((cfg.t, d), F32), _sds((1, d), F32), _sds((cfg.t, d), BF16)],
        compiler_params=_cp(),
    )(h, target, w.reshape(1, d))
    return loss[0, 0], (dh, dhb), dw[0]


def _rows_tile(r, c):
    return _pick(r, max(8, (1 << 18) // max(c, 1) // 8 * 8), 8)


def _adam_update(w, g, m, v):
    c1 = 1.0 - ADAM_B1 ** ADAM_STEP
    c2 = 1.0 - ADAM_B2 ** ADAM_STEP
    mn = ADAM_B1 * m + (1.0 - ADAM_B1) * g
    vn = ADAM_B2 * v + (1.0 - ADAM_B2) * (g * g)
    delta = -ADAM_LR * ((mn / c1) / (jnp.sqrt(vn / c2) + ADAM_EPS) + ADAM_WD * w)
    return delta, mn, vn


def adamw_layer(w, m, v, g, li, prev, dep, *, name):
    _, r, c = w.shape
    tr = _rows_tile(r, c)

    def body(*refs):
        w_ref, m_ref, v_ref, g_ref = refs[:4]
        go_ref, d_ref, mo_ref, vo_ref = refs[-4:]
        gv = g_ref[...]
        delta, mn, vn = _adam_update(w_ref[0], gv, m_ref[0], v_ref[0])
        go_ref[0] = gv
        d_ref[0] = delta
        mo_ref[0] = mn
        vo_ref[0] = vn

    if tr * c * 4 >= (1 << 16):
        steps = r // tr
        blk3 = pl.BlockSpec((1, tr, c), lambda i: (li, i, 0))
        blk2 = pl.BlockSpec((tr, c), lambda i: (i, 0))
    else:
        tc = _pick(c, max(LANE, (1 << 18) // r // LANE * LANE), LANE)
        steps = c // tc
        blk3 = pl.BlockSpec((1, r, tc), lambda i: (li, 0, i))
        blk2 = pl.BlockSpec((r, tc), lambda i: (0, i))
    anyspec = pl.BlockSpec(memory_space=pl.ANY)
    in_specs = [blk3, blk3, blk3, blk2, anyspec]
    args = [w, m, v, g, dep]
    aliases = {}
    if prev is not None:
        in_specs += [anyspec] * 4
        args += list(prev)
        aliases = {5 + i: i for i in range(4)}
    return pl.pallas_call(
        body, name=name, grid=(steps,), in_specs=in_specs, out_specs=[blk3] * 4,
        out_shape=[_sds(w.shape, F32)] * 4, input_output_aliases=aliases, compiler_params=_cp(),
    )(*args)


def pair_add(g4, other, half, *, name):
    n, _, r, c = g4.shape
    tr = _rows_tile(r, c)

    def body(h_ref, a_ref, b_ref, o_ref):
        o_ref[0] = (a_ref[0, 0].astype(F32) + b_ref[0].astype(F32)).astype(BF16)

    blk = pl.BlockSpec((1, tr, c), lambda j, i, h: (j, i, 0))
    grid_spec = pltpu.PrefetchScalarGridSpec(
        num_scalar_prefetch=1, grid=(n, r // tr),
        in_specs=[pl.BlockSpec((1, 1, tr, c), lambda j, i, h: (j, h[0], i, 0)), blk], out_specs=blk)
    return pl.pallas_call(body, name=name, grid_spec=grid_spec, out_shape=_sds((n, r, c), BF16),
                          compiler_params=_cp())(half, g4, other)


def chip_sum(recv, part, where, *, name):
    n, r, c = recv.shape
    tr = _rows_tile(r, c)

    def body(s_ref, *refs):
        own_ref, o_ref = refs[n], refs[n + 1]
        acc = None
        for j in range(n):
            term = jnp.where(s_ref[0] == j, own_ref[0], refs[j][0]).astype(F32)
            acc = term if acc is None else acc + term
        o_ref[0] = acc

    def slot(j):
        return pl.BlockSpec((1, tr, c), lambda i, s: (jnp.where(s[0] == j, (j + 1) % n, j), i, 0))

    grid_spec = pltpu.PrefetchScalarGridSpec(
        num_scalar_prefetch=1, grid=(r // tr,),
        in_specs=[slot(j) for j in range(n)] + [pl.BlockSpec((1, tr, c), lambda i, s: (s[0], i, 0))],
        out_specs=pl.BlockSpec((1, tr, c), lambda i, s: (s[1], i, 0)))
    return pl.pallas_call(body, name=name, grid_spec=grid_spec, out_shape=_sds((2, r, c), F32),
                          compiler_params=_cp())(where, *([recv] * n), part)


def _coords():
    return lax.axis_index("x"), lax.axis_index("y"), lax.axis_index("c")


def _other_chips(x, y):
    return [(1 - x, y), (x, 1 - y), (1 - x, 1 - y)]


def gather_chips(arrs, *, name):
    n = len(arrs)
    anyspec = pl.BlockSpec(memory_space=pl.ANY)

    def body(*refs):
        ins, outs = refs[:n], refs[n:2 * n]
        send_sems, recv_sems, local_sems = refs[2 * n:]
        x, y, c = _coords()
        me = 2 * x + y
        chips = _other_chips(x, y)
        copies = []
        for k in range(n):
            loc = pltpu.make_async_copy(ins[k], outs[k].at[me], local_sems.at[k])
            loc.start()
            copies.append(loc)
        sends = []
        for k in range(n):
            for j, (px, py) in enumerate(chips):
                cp = pltpu.make_async_remote_copy(
                    src_ref=ins[k], dst_ref=outs[k].at[me], send_sem=send_sems.at[k, j], recv_sem=recv_sems.at[k, j],
                    device_id=(px, py, c), device_id_type=MESH)
                cp.start()
                sends.append(cp)
        for k in range(n):
            for j, (px, py) in enumerate(chips):
                pltpu.make_async_remote_copy(
                    src_ref=ins[k], dst_ref=outs[k].at[2 * px + py], send_sem=send_sems.at[k, j],
                    recv_sem=recv_sems.at[k, j], device_id=(px, py, c), device_id_type=MESH).wait_recv()
        for cp in sends:
            cp.wait_send()
        for cp in copies:
            cp.wait()

    return pl.pallas_call(
        body, name=name, in_specs=[anyspec] * n, out_specs=[anyspec] * n,
        out_shape=[_sds((4,) + a.shape, a.dtype) for a in arrs],
        scratch_shapes=[pltpu.SemaphoreType.DMA((n, 3)), pltpu.SemaphoreType.DMA((n, 3)), pltpu.SemaphoreType.DMA((n,))],
        compiler_params=_cp(has_side_effects=True),
    )(*arrs)


def allreduce_small(vec, after, *, name):
    r, c = vec.shape

    def body(v_ref, after_ref, o_ref, buf, send_sems, recv_sems):
        x, y, cc = _coords()
        me = 4 * x + 2 * y + cc
        buf[me] = v_ref[...]
        sends = []
        flips = [(fx, fy, fc) for fx in (0, 1) for fy in (0, 1) for fc in (0, 1)][1:]
        for j, (fx, fy, fc) in enumerate(flips):
            peer = ((1 - x) if fx else x, (1 - y) if fy else y, (1 - cc) if fc else cc)
            cp = pltpu.make_async_remote_copy(
                src_ref=v_ref, dst_ref=buf.at[me], send_sem=send_sems.at[j], recv_sem=recv_sems.at[j],
                device_id=peer, device_id_type=MESH)
            cp.start()
            sends.append(cp)
        for j, (fx, fy, fc) in enumerate(flips):
            px, py, pc = ((1 - x) if fx else x, (1 - y) if fy else y, (1 - cc) if fc else cc)
            pltpu.make_async_remote_copy(
                src_ref=v_ref, dst_ref=buf.at[4 * px + 2 * py + pc], send_sem=send_sems.at[j],
                recv_sem=recv_sems.at[j], device_id=(px, py, pc), device_id_type=MESH).wait_recv()
        for cp in sends:
            cp.wait_send()
        acc = buf[0]
        for k in range(1, 8):
            acc = acc + buf[k]
        o_ref[...] = acc

    vm = pl.BlockSpec(memory_space=pltpu.VMEM)
    return pl.pallas_call(
        body, name=name, in_specs=[vm, pl.BlockSpec(memory_space=pl.ANY)], out_specs=vm, out_shape=_sds((r, c), F32),
        scratch_shapes=[pltpu.VMEM((8, r, c), F32), pltpu.SemaphoreType.DMA((7,)), pltpu.SemaphoreType.DMA((7,))],
        compiler_params=_cp(has_side_effects=True),
    )(vec, after)


def pair_share(lands, owns, *, name):
    n = len(lands)
    anyspec = pl.BlockSpec(memory_space=pl.ANY)

    def body(*refs):
        ins, own_refs, outs = refs[:n], refs[n:2 * n], refs[2 * n:3 * n]
        send_sems, recv_sems = refs[3 * n:]
        x, y, c = _coords()
        me = 2 * x + y
        sib = (x, y, 1 - c)
        sends = []
        for k in range(n):
            for j, (px, py) in enumerate(_other_chips(x, y)):
                cp = pltpu.make_async_remote_copy(
                    src_ref=ins[k].at[2 * px + py, c], dst_ref=outs[k].at[2 * px + py, c], send_sem=send_sems.at[k, j],
                    recv_sem=recv_sems.at[k, j], device_id=sib, device_id_type=MESH)
                cp.start()
                sends.append(cp)
            cp = pltpu.make_async_remote_copy(
                src_ref=own_refs[k], dst_ref=outs[k].at[me], send_sem=send_sems.at[k, 3], recv_sem=recv_sems.at[k, 3],
                device_id=sib, device_id_type=MESH)
            cp.start()
            sends.append(cp)
        for k in range(n):
            for j, (px, py) in enumerate(_other_chips(x, y)):
                pltpu.make_async_remote_copy(
                    src_ref=ins[k].at[2 * px + py, c], dst_ref=outs[k].at[2 * px + py, 1 - c],
                    send_sem=send_sems.at[k, j], recv_sem=recv_sems.at[k, j], device_id=sib,
                    device_id_type=MESH).wait_recv()
            pltpu.make_async_remote_copy(
                src_ref=own_refs[k], dst_ref=outs[k].at[me], send_sem=send_sems.at[k, 3], recv_sem=recv_sems.at[k, 3],
                device_id=sib, device_id_type=MESH).wait_recv()
        for cp in sends:
            cp.wait_send()

    return pl.pallas_call(
        body, name=name, in_specs=[anyspec] * (2 * n), out_specs=[anyspec] * n,
        out_shape=[_sds(a.shape, a.dtype) for a in lands], input_output_aliases={k: k for k in range(n)},
        scratch_shapes=[pltpu.SemaphoreType.DMA((n, 4)), pltpu.SemaphoreType.DMA((n, 4))],
        compiler_params=_cp(has_side_effects=True),
    )(*lands, *owns)


def pair_fill(arrs, *, name):
    n = len(arrs)
    anyspec = pl.BlockSpec(memory_space=pl.ANY)

    def body(*refs):
        ins, outs = refs[:n], refs[n:2 * n]
        send_sems, recv_sems = refs[2 * n:]
        x, y, c = _coords()
        sends = []
        for k in range(n):
            cp = pltpu.make_async_remote_copy(
                src_ref=ins[k].at[c], dst_ref=outs[k].at[c], send_sem=send_sems.at[k], recv_sem=recv_sems.at[k],
                device_id=(x, y, 1 - c), device_id_type=MESH)
            cp.start()
            sends.append(cp)
        for k in range(n):
            pltpu.make_async_remote_copy(
                src_ref=ins[k].at[c], dst_ref=outs[k].at[1 - c], send_sem=send_sems.at[k], recv_sem=recv_sems.at[k],
                device_id=(x, y, 1 - c), device_id_type=MESH).wait_recv()
        for cp in sends:
            cp.wait_send()

    return pl.pallas_call(
        body, name=name, in_specs=[anyspec] * n, out_specs=[anyspec] * n,
        out_shape=[_sds(a.shape, a.dtype) for a in arrs], input_output_aliases={k: k for k in range(n)},
        scratch_shapes=[pltpu.SemaphoreType.DMA((n,)), pltpu.SemaphoreType.DMA((n,))],
        compiler_params=_cp(has_side_effects=True),
    )(*arrs)


_HBM = pl.BlockSpec(memory_space=pltpu.HBM)
_SEM = pl.BlockSpec(memory_space=pltpu.SEMAPHORE)


_COPIES_PER_ARRAY = {"gather": 3, "scatter": 3, "share": 4, "exchange": 4}


def _ici_copies(kind, srcs, lands, send_sems, recv_sems):
    x, y, c = _coords()
    me = 2 * x + y
    per = _COPIES_PER_ARRAY[kind]
    sends, recvs = [], []
    for k in range(len(srcs)):
        triples = []
        for j, (px, py) in enumerate(_other_chips(x, y)):
            peer = 2 * px + py
            if kind == "gather":
                triples.append((srcs[k].at[c], lands[k].at[me, c], lands[k].at[peer, c], (px, py, c)))
            elif kind == "scatter":
                triples.append((srcs[k].at[peer], lands[k].at[me], lands[k].at[peer], (px, py, c)))
            elif kind == "share":
                triples.append((lands[k].at[peer, c], lands[k].at[peer, c], lands[k].at[peer, 1 - c], (x, y, 1 - c)))
        if kind == "share":
            triples.append((srcs[k], lands[k].at[me], lands[k].at[me], (x, y, 1 - c)))
        if kind == "exchange":
            triples = [(srcs[k].at[j, 1 - c], lands[k].at[j], lands[k].at[j], (x, y, 1 - c)) for j in range(4)]
        for j, (src, there, here, dev) in enumerate(triples):
            sem = per * k + j
            mk = functools.partial(pltpu.make_async_remote_copy, src_ref=src, send_sem=send_sems.at[sem],
                                   recv_sem=recv_sems.at[sem], device_id=dev, device_id_type=MESH)
            sends.append(mk(dst_ref=there))
            recvs.append(mk(dst_ref=here))
    return sends, recvs


def ici_start(kind, srcs, lands, after, *, name):
    n = len(srcs)

    def body(*refs):
        src_refs, land_refs = refs[:n], refs[n:2 * n]
        send_sems, recv_sems = refs[2 * n + 1], refs[2 * n + 2]
        token = refs[-1]
        sends, _ = _ici_copies(kind, src_refs, land_refs, send_sems, recv_sems)
        for cp in sends:
            cp.start()
        token[...] = jnp.zeros_like(token)

    both = list(srcs) + list(lands)
    out = pl.pallas_call(
        body, name=name,
        in_specs=[_HBM] * (2 * n) + [pl.BlockSpec(memory_space=pl.ANY)],
        out_shape=(pltpu.SemaphoreType.DMA((_COPIES_PER_ARRAY[kind] * n,)),
                   pltpu.SemaphoreType.DMA((_COPIES_PER_ARRAY[kind] * n,)),
                   *[pltpu.HBM(a.shape, a.dtype) for a in both], _sds((8, LANE), F32)),
        out_specs=(_SEM, _SEM, *([_HBM] * (2 * n)), pl.BlockSpec(memory_space=pltpu.VMEM)),
        input_output_aliases={i: 2 + i for i in range(2 * n)},
        compiler_params=_cp(has_side_effects=pltpu.SideEffectType.DATAFLOW_SIDE_EFFECTING),
    )(*[pltpu.with_memory_space_constraint(a, pltpu.HBM) for a in both], after)
    return out[0], out[1], list(out[2:2 + n]), list(out[2 + n:2 + 2 * n]), out[-1]


def ici_wait(kind, started, after, *, name):
    send_sems, recv_sems, srcs, lands, _ = started
    n = len(srcs)

    def body(*refs):
        src_refs, land_refs = refs[:n], refs[n:2 * n]
        sends, recvs = _ici_copies(kind, src_refs, land_refs, refs[2 * n], refs[2 * n + 1])
        for cp in sends:
            cp.wait_send()
        for cp in recvs:
            cp.wait_recv()

    both = list(srcs) + list(lands)
    out = pl.pallas_call(
        body, name=name,
        in_specs=[_HBM] * (2 * n) + [_SEM, _SEM, pl.BlockSpec(memory_space=pl.ANY)],
        out_shape=tuple(pltpu.HBM(a.shape, a.dtype) for a in both), out_specs=tuple([_HBM] * (2 * n)),
        input_output_aliases={i: i for i in range(2 * n)},
        compiler_params=_cp(has_side_effects=pltpu.SideEffectType.DATAFLOW_SIDE_EFFECTING),
    )(*both, send_sems, recv_sems, after)
    return list(out[:n]), list(out[n:])


BIG = ["w_in", "w_uq", "w_ukv", "w_branch_ssm", "w_branch_mla", "w_out", "w_mlp_up", "w_mlp_down"]
COL_SHARDED = {"w_in", "w_uq", "w_ukv", "w_mlp_up"}
SMALL_REPL = ["norm_mix_w", "conv_b", "dt_bias", "a_log", "d_skip", "ssm_norm_w", "q_norm_w", "kv_norm_w", "norm_mlp_w"]


def _unshard_layer(name, g):
    _, r, c = g.shape
    if name in COL_SHARDED:
        return jnp.transpose(g, (1, 0, 2)).reshape(r, 4 * c)
    return g.reshape(4 * r, c)


def _to_shards(name, full):
    r, c = full.shape
    if name in COL_SHARDED:
        return jnp.transpose(full.reshape(r, 4, c // 4), (1, 0, 2))
    return full.reshape(4, r // 4, c)


REST = [k for k in BIG if k != "w_in"]


def prep_layer(cfg, w):
    out = {}
    if "w_in" in w:
        sp = np.cumsum(cfg.in_splits)[:-1].tolist()
        z, xbc, dt, cq, ckv, kr, gs, gm = jnp.split(w["w_in"], sp, axis=1)
        zpad = lambda n: jnp.zeros((cfg.d, n), z.dtype)
        out.update(w_z=z, w_xbc=xbc, w_g=jnp.concatenate([gs, gm], axis=1),
                   w_s=jnp.concatenate([cq, ckv, kr, zpad(LANE - cfg.rope), dt, zpad(LANE - cfg.heads)], axis=1))
    if "w_uq" in w:
        out.update(
            w_uq=jnp.pad(w["w_uq"].reshape(cfg.ql, cfg.mh, cfg.nope + cfg.rope),
                         ((0, 0), (0, 0), (0, 2 * LANE - cfg.nope - cfg.rope))).reshape(cfg.ql, cfg.qw),
            w_ukv=w["w_ukv"], w_bs=w["w_branch_ssm"], w_bm=w["w_branch_mla"], w_out=w["w_out"],
            w_up=w["w_mlp_up"], w_down=w["w_mlp_down"])
    return {k: v.astype(BF16) for k, v in out.items()}


def unprep_grads(cfg, g):
    out = {}
    if "w_s" in g:
        ql, kvl = cfg.ql, cfg.kvl
        ds_ = g["w_s"]
        cq, ckv = ds_[:, :ql], ds_[:, ql:ql + kvl]
        kr = ds_[:, ql + kvl:ql + kvl + cfg.rope]
        dt = ds_[:, ql + kvl + LANE:ql + kvl + LANE + cfg.heads]
        out["w_in"] = jnp.concatenate([g["w_z"], g["w_xbc"], dt, cq, ckv, kr, g["w_g"]], axis=1)
    if "w_uq" in g:
        out.update(
            w_uq=g["w_uq"].reshape(cfg.ql, cfg.mh, 2 * LANE)[:, :, :cfg.nope + cfg.rope].reshape(cfg.ql, -1),
            w_ukv=g["w_ukv"], w_branch_ssm=g["w_bs"], w_branch_mla=g["w_bm"],
            w_out=g["w_out"], w_mlp_up=g["w_up"], w_mlp_down=g["w_down"])
    return out


def _hook(hooks, name, arg):
    if hooks and name in hooks:
        return hooks[name](arg)[0, 0]
    return 0.0


def layer_fwd(cfg, h, pw, sm, tabs, li, hooks=None):
    n = lambda s: f"l{li}_{s}"
    u = rmsnorm_fwd(h, sm["norm_mix_w"], name=n("norm_mix"))
    z = matmul(u, pw["w_z"], out_dtype=BF16, name=n("in_z"))
    xbc = matmul(u, pw["w_xbc"], name=n("in_xbc"))
    g = matmul(u, pw["w_g"], out_dtype=BF16, name=n("in_g"))
    small = matmul(u, pw["w_s"], name=n("in_s"))
    xc, dsilu = conv_fwd(cfg, xbc, sm["conv_w"], sm["conv_b"], name=n("conv"))
    dt_bias = sm["dt_bias_p"] + _hook(hooks, "after_conv", xc)
    y, sin = ssd_fwd(cfg, xc, small, dt_bias, sm["avec"], sm["dexp"], name=n("ssd"))
    y_ssm = tail_fwd(cfg, y, z, sm["ssm_norm_w"], name=n("tail"))
    if hooks and "weights" in hooks:
        pw = dict(pw, **hooks["weights"](y_ssm))
    cqn = rmsnorm_fwd(small, sm["q_norm_w"], cw=cfg.ql, ci=0, name=n("q_norm"))
    ckvn = rmsnorm_fwd(small, sm["kv_norm_w"], cw=cfg.kvl, ci=cfg.ql // cfg.kvl, name=n("kv_norm"))
    qf = matmul(cqn, pw["w_uq"], out_dtype=BF16, name=n("uq"))
    kv = matmul(ckvn, pw["w_ukv"], out_dtype=BF16, name=n("ukv"))
    qr, kpe = rope_fwd(cfg, qf, small, tabs, name=n("rope"))
    o, lse = attn_fwd(cfg, qr, kv, kpe, name=n("attn"))
    ya = matmul(y_ssm, pw["w_bs"], out_dtype=BF16, name=n("branch_ssm"))
    yb = matmul(o, pw["w_bm"], out_dtype=BF16, name=n("branch_mla"))
    mixed = gate_fwd(cfg, ya, yb, g, name=n("gate"))
    h1 = matmul(mixed, pw["w_out"], add=h, name=n("out"))
    v = rmsnorm_fwd(h1, sm["norm_mlp_w"] + _hook(hooks, "after_attn", o), name=n("norm_mlp"))
    a, act = matmul(v, pw["w_up"], name=n("up"), epilogue=_ep_relu2, out_dtypes=(BF16, BF16))
    h2 = matmul(act, pw["w_down"], add=h1, name=n("down"))
    saved = dict(h=h, u=u, z=z, xbc=xbc, g=g, small=small, xc=xc, dsilu=dsilu, y=y, sin=sin, y_ssm=y_ssm, cqn=cqn, ckvn=ckvn,
                 qr=qr, kv=kv, kpe=kpe, o=o, lse=lse, ya=ya, yb=yb, mixed=mixed, h1=h1, v=v, a=a, act=act)
    return h2, saved, pw


def layer_bwd(cfg, dh2, pw, sm, tabs, s, li, hooks=None):
    n = lambda t: f"l{li}_b_{t}"
    gw, gs = {}, {}
    wgrad = functools.partial(matmul, ta=True, out_dtype=BF16)
    dh2, dh2b = dh2
    gw["w_down"] = wgrad(s["act"], dh2b, name=n("dw_down"))
    da = matmul(dh2b, pw["w_down"], tb=True, name=n("dact"), epilogue=_ep_relu2_grad, extras=(s["a"],),
                out_dtypes=(BF16,))
    gw["w_up"] = wgrad(s["v"], da, name=n("dw_up"))
    dv = matmul(da, pw["w_up"], tb=True, out_dtype=BF16, name=n("dv"))
    dh1, gs["norm_mlp_w"], dh1b = rmsnorm_bwd(dv, s["h1"], sm["norm_mlp_w"], res=dh2, with_bf16=True,
                                              name=n("norm_mlp"))
    gw["w_out"] = wgrad(s["mixed"], dh1b, name=n("dw_out"))
    dmix = matmul(dh1b, pw["w_out"], tb=True, out_dtype=BF16, name=n("dmix"))
    dya, dyb, dg = gate_bwd(cfg, dmix, s["ya"], s["yb"], s["g"], name=n("gate"))
    gw["w_bs"] = wgrad(s["y_ssm"], dya, name=n("dw_bs"))
    gw["w_bm"] = wgrad(s["o"], dyb, name=n("dw_bm"))
    dy_ssm = matmul(dya, pw["w_bs"], tb=True, out_dtype=BF16, name=n("dy_ssm"))
    do = matmul(dyb, pw["w_bm"], tb=True, out_dtype=BF16, name=n("do"))
    dq, dkv, dkpe = attn_bwd(cfg, s["qr"], s["kv"], s["kpe"], s["o"], s["lse"], do, name=n("attn"))
    dqf, dkr = rope_bwd(cfg, dq, dkpe, tabs, name=n("rope"))
    gw["w_uq"] = wgrad(s["cqn"], dqf, name=n("dw_uq"))
    gw["w_ukv"] = wgrad(s["ckvn"], dkv, name=n("dw_ukv"))
    dcqn = matmul(dqf, pw["w_uq"], tb=True, name=n("dcqn"))
    dckvn = matmul(dkv, pw["w_ukv"], tb=True, name=n("dckvn"))
    q_norm_w = sm["q_norm_w"] + _hook(hooks, "after_attn", dqf)
    dcq, gs["q_norm_w"] = rmsnorm_bwd(dcqn, s["small"], q_norm_w, cw=cfg.ql, ci=0, out_dtype=BF16, name=n("q_norm"))
    dckv, gs["kv_norm_w"] = rmsnorm_bwd(dckvn, s["small"], sm["kv_norm_w"], cw=cfg.kvl, ci=cfg.ql // cfg.kvl,
                                        out_dtype=BF16, name=n("kv_norm"))
    ssm_norm_w = sm["ssm_norm_w"] + _hook(hooks, "early", dict(gw))
    dy, dz, gs["ssm_norm_w"] = tail_bwd(cfg, dy_ssm, s["y"], s["z"], ssm_norm_w, name=n("tail"))
    dxc, ddt, ddexp, dav, dbias = ssd_bwd(cfg, s["xc"], s["small"], sm["dt_bias_p"], sm["avec"], sm["dexp"],
                                          s["sin"], dy, name=n("ssd"))
    conv_w = sm["conv_w"] + _hook(hooks, "after_ssd", dxc)
    dxbc, gs["conv_w"], gs["conv_b"] = conv_bwd(cfg, s["xbc"], conv_w, s["dsilu"], dxc, name=n("conv"))
    gs["d_skip"] = ddexp.reshape(cfg.heads, cfg.hd).sum(axis=1)
    gs["a_log"] = (dav[0] * sm["avec"][0])[:cfg.heads]
    gs["dt_bias"] = dbias[0, :cfg.heads]
    dsmall = jnp.concatenate([dcq, dckv, dkr.astype(BF16), ddt.astype(BF16)], axis=1)
    gw["w_z"] = wgrad(s["u"], dz, name=n("dw_z"))
    gw["w_xbc"] = wgrad(s["u"], dxbc, name=n("dw_xbc"))
    gw["w_g"] = wgrad(s["u"], dg, name=n("dw_g"))
    gw["w_s"] = wgrad(s["u"], dsmall, name=n("dw_s"))
    du = matmul_nt_sum([dz, dxbc, dg, dsmall], [pw["w_z"], pw["w_xbc"], pw["w_g"], pw["w_s"]], out_dtype=BF16,
                       name=n("du"))
    if li > 0:
        dh, gs["norm_mix_w"], dhb = rmsnorm_bwd(du, s["h"], sm["norm_mix_w"], res=dh1, with_bf16=True,
                                                name=n("norm_mix"))
    else:
        dh, gs["norm_mix_w"] = rmsnorm_bwd(du, s["h"], sm["norm_mix_w"], res=dh1, name=n("norm_mix"))
        dhb = None
    return (dh, dhb), gw, gs


def small_params(cfg, p, li):
    pad_l = lambda v: jnp.pad(v, (0, LANE - v.shape[0])).reshape(1, LANE)
    return dict(
        norm_mix_w=p["norm_mix_w"][li], conv_w=p["conv_w"][li], conv_b=p["conv_b"][li],
        dt_bias_p=pad_l(p["dt_bias"][li]), avec=pad_l(-jnp.exp(p["a_log"][li])),
        dexp=jnp.repeat(p["d_skip"][li], cfg.hd).reshape(1, cfg.inner),
        ssm_norm_w=p["ssm_norm_w"][li], q_norm_w=p["q_norm_w"][li], kv_norm_w=p["kv_norm_w"][li],
        norm_mlp_w=p["norm_mlp_w"][li])


def local_step(cfg, x, target, p, depth=2):
    bsz, d = cfg.bsz, cfg.d
    lead = jnp.zeros((bsz, cfg.pad, d), F32)
    meta = jnp.broadcast_to(p["meta_tokens"][None], (bsz, cfg.n_meta, d))
    h = jnp.concatenate([lead, meta, x], axis=1).reshape(cfg.t, d)
    tabs = rope_tables(cfg)
    saved, sms = [], []
    for li in range(depth):
        sm = small_params(cfg, p, li)
        h, s, _ = layer_fwd(cfg, h, p["pw"][li], sm, tabs, li)
        saved.append(s)
        sms.append(sm)
    loss, dh, dfw = loss_head(cfg, h, target.reshape(bsz * cfg.seq, d), p["final_norm_w"], name="loss_head")
    gws, gss = [None] * depth, [None] * depth
    for li in reversed(range(depth)):
        dh, gws[li], gss[li] = layer_bwd(cfg, dh, p["pw"][li], sms[li], tabs, saved[li], li)
    dh = dh[0].reshape(bsz, cfg.lp, d)
    grad_x = dh[:, cfg.chunk:, :]
    gmeta = jnp.sum(dh[:, cfg.pad:cfg.chunk, :], axis=0)
    return loss, grad_x, gmeta, gws, gss, dfw


def _pack_small(parts):
    flat = jnp.concatenate([a.reshape(-1) for a in parts])
    n = flat.shape[0]
    npad = -n % (8 * LANE)
    return jnp.pad(flat, (0, npad)).reshape(-1, LANE), n


def _unpack_small(vec, shapes):
    flat = vec.reshape(-1)
    out, off = [], 0
    for sh in shapes:
        sz = int(np.prod(sh))
        out.append(flat[off:off + sz].reshape(sh))
        off += sz
    return out


def _as2d(a):
    return a.reshape(-1, a.shape[-1])


def kernel(x, meta_tokens, norm_mix_w, w_in, conv_w, conv_b, dt_bias, a_log, d_skip, ssm_norm_w, q_norm_w, kv_norm_w, w_uq, w_ukv, w_branch_ssm, w_branch_mla, w_out, norm_mlp_w, w_mlp_up, w_mlp_down, final_norm_w, loss_target, m_meta_tokens, m_norm_mix_w, m_w_in, m_conv_w, m_conv_b, m_dt_bias, m_a_log, m_d_skip, m_ssm_norm_w, m_q_norm_w, m_kv_norm_w, m_w_uq, m_w_ukv, m_w_branch_ssm, m_w_branch_mla, m_w_out, m_norm_mlp_w, m_w_mlp_up, m_w_mlp_down, m_final_norm_w, v_meta_tokens, v_norm_mix_w, v_w_in, v_conv_w, v_conv_b, v_dt_bias, v_a_log, v_d_skip, v_ssm_norm_w, v_q_norm_w, v_kv_norm_w, v_w_uq, v_w_ukv, v_w_branch_ssm, v_w_branch_mla, v_w_out, v_norm_mlp_w, v_w_mlp_up, v_w_mlp_down, v_final_norm_w):
    cfg = CFG
    names = ["meta_tokens", "norm_mix_w", "w_in", "conv_w", "conv_b", "dt_bias", "a_log", "d_skip", "ssm_norm_w",
             "q_norm_w", "kv_norm_w", "w_uq", "w_ukv", "w_branch_ssm", "w_branch_mla", "w_out", "norm_mlp_w",
             "w_mlp_up", "w_mlp_down", "final_norm_w"]
    wts = dict(zip(names, [meta_tokens, norm_mix_w, w_in, conv_w, conv_b, dt_bias, a_log, d_skip, ssm_norm_w,
                           q_norm_w, kv_norm_w, w_uq, w_ukv, w_branch_ssm, w_branch_mla, w_out, norm_mlp_w,
                           w_mlp_up, w_mlp_down, final_norm_w]))
    ms = dict(zip(names, [m_meta_tokens, m_norm_mix_w, m_w_in, m_conv_w, m_conv_b, m_dt_bias, m_a_log, m_d_skip,
                          m_ssm_norm_w, m_q_norm_w, m_kv_norm_w, m_w_uq, m_w_ukv, m_w_branch_ssm, m_w_branch_mla,
                          m_w_out, m_norm_mlp_w, m_w_mlp_up, m_w_mlp_down, m_final_norm_w]))
    vs = dict(zip(names, [v_meta_tokens, v_norm_mix_w, v_w_in, v_conv_w, v_conv_b, v_dt_bias, v_a_log, v_d_skip,
                          v_ssm_norm_w, v_q_norm_w, v_kv_norm_w, v_w_uq, v_w_ukv, v_w_branch_ssm, v_w_branch_mla,
                          v_w_out, v_norm_mlp_w, v_w_mlp_up, v_w_mlp_down, v_final_norm_w]))
    cx, cy, cc = _coords()
    chip = 2 * cx + cy

    half1 = jnp.reshape(cc, (1,)).astype(jnp.int32)
    where2 = jnp.stack([chip, cc]).astype(jnp.int32)
    wb = {k: wts[k].astype(BF16) for k in BIG}
    zero_tok = jnp.zeros((8, LANE), F32)

    def halves(a):
        return a.reshape((2, a.shape[0] // 2) + a.shape[1:])

    def gather_start(li, keys, tag, after):
        srcs = [halves(wb[k][li]) for k in keys]
        lands = [lax.empty((4,) + s.shape, BF16) for s in srcs]
        return ici_start("gather", srcs, lands, after, name=f"gather{li}{tag}_start")

    def gather_finish(li, keys, tag, started, after):
        srcs, lands = ici_wait("gather", started, after, name=f"gather{li}{tag}_wait")
        lands = pair_share(lands, srcs, name=f"gather{li}{tag}_share")
        full = {k: _unshard_layer(k, land.reshape((4, 2 * land.shape[2], land.shape[3])))
                for k, land in zip(keys, lands)}
        return prep_layer(cfg, full)

    def gather_mid(li, keys, tag, started, after):
        srcs, lands = ici_wait("gather", started, after, name=f"gather{li}{tag}_wait")
        return ici_start("share", srcs, lands, zero_tok, name=f"gather{li}{tag}_share_start")

    def gather_end(li, keys, tag, shared, after):
        _, lands = ici_wait("share", shared, after, name=f"gather{li}{tag}_share_wait")
        full = {k: _unshard_layer(k, land.reshape((4, 2 * land.shape[2], land.shape[3])))
                for k, land in zip(keys, lands)}
        return prep_layer(cfg, full)

    def exchange_start(li, keys, tag, gw, after):
        ug = unprep_grads(cfg, gw)
        g4 = []
        for k in keys:
            s = _to_shards(k, ug[k])
            g4.append(s.reshape(4, 2, s.shape[1] // 2, s.shape[2]))
        lands = [lax.empty((4,) + a.shape[2:], a.dtype) for a in g4]
        return ici_start("exchange", g4, lands, after, name=f"grad{li}{tag}_exchange_start")

    def reduce_start(li, keys, tag, exchanged, after):
        g4, theirs = ici_wait("exchange", exchanged, after, name=f"grad{li}{tag}_exchange_wait")
        parts = [pair_add(a, b, half1, name=f"grad{li}_pair_add_{k}") for k, a, b in zip(keys, g4, theirs)]
        lands = [lax.empty(q.shape, q.dtype) for q in parts]
        return ici_start("scatter", parts, lands, zero_tok, name=f"grad{li}{tag}_scatter_start")

    def reduce_finish(li, keys, tag, started, after):
        parts, lands = ici_wait("scatter", started, after, name=f"grad{li}{tag}_scatter_wait")
        sums = [chip_sum(rc, pt, where2, name=f"grad{li}_chip_sum_{k}") for k, rc, pt in zip(keys, lands, parts)]
        sums = pair_fill(sums, name=f"grad{li}{tag}_pair_fill")
        return {k: s.reshape(2 * s.shape[1], s.shape[2]) for k, s in zip(keys, sums)}

    gathered = gather_chips([meta_tokens, conv_w], name="gather_small")
    p = dict(wts)
    p["meta_tokens"] = jnp.transpose(gathered[0], (1, 0, 2)).reshape(cfg.n_meta, cfg.d)
    p["conv_w"] = jnp.transpose(gathered[1], (1, 2, 0, 3)).reshape(2, cfg.convk, cfg.conv_dim)

    st0a = gather_start(0, ["w_in"], "a", gathered[0])
    st0b = gather_start(0, REST, "b", st0a[4])
    st1 = gather_start(1, BIG, "", st0b[4])
    pw0 = gather_finish(0, ["w_in"], "a", st0a, st1[4])

    bsz, d = cfg.bsz, cfg.d
    lead = jnp.zeros((bsz, cfg.pad, d), F32)
    meta = jnp.broadcast_to(p["meta_tokens"][None], (bsz, cfg.n_meta, d))
    h0 = jnp.concatenate([lead, meta, x], axis=1).reshape(cfg.t, d)
    tabs = rope_tables(cfg)
    sm0 = small_params(cfg, p, 0)
    st = {}

    def step(key, fn):
        def run(arg):
            st[key] = fn(arg)
            return st[key][4]
        return run

    h1, sv0, pw0 = layer_fwd(cfg, h0, pw0, sm0, tabs, 0, hooks={
        "after_conv": step("share0b", lambda after: gather_mid(0, REST, "b", st0b, after)),
        "weights": lambda after: gather_end(0, REST, "b", st["share0b"], after),
        "after_attn": step("share1", lambda after: gather_mid(1, BIG, "", st1, after))})
    pw1 = gather_end(1, BIG, "", st["share1"], h1)
    sm1 = small_params(cfg, p, 1)
    h2, sv1, _ = layer_fwd(cfg, h1, pw1, sm1, tabs, 1)
    loss, dh, dfw = loss_head(cfg, h2, loss_target.reshape(bsz * cfg.seq, d), final_norm_w, name="loss_head")

    dh, gw1, gs1 = layer_bwd(cfg, dh, pw1, sm1, tabs, sv1, 1)
    ex1 = exchange_start(1, BIG, "", gw1, zero_tok)
    sm0b = dict(sm0)
    sm0b["norm_mlp_w"] = sm0["norm_mlp_w"] + ex1[4][0, 0]
    dh, gw0, gs0 = layer_bwd(cfg, dh, pw0, sm0b, tabs, sv0, 0, hooks={
        "after_attn": step("red1", lambda after: reduce_start(1, BIG, "", ex1, after)),
        "early": step("ex0e", lambda gw: exchange_start(0, REST, "e", gw, zero_tok)),
        "after_ssd": step("red0e", lambda after: reduce_start(0, REST, "e", st["ex0e"], after))})
    dh3 = dh[0].reshape(bsz, cfg.lp, d)
    grad_x = dh3[:, cfg.chunk:, :]
    gmeta = jnp.sum(dh3[:, cfg.pad:cfg.chunk, :], axis=0)
    big1 = reduce_finish(1, BIG, "", st["red1"], dh[0])
    ex0l = exchange_start(0, ["w_in"], "l", gw0, big1[BIG[-1]])

    small_names = SMALL_REPL + ["conv_w"]
    parts = [jnp.stack([gs0[k], gs1[k]]) for k in small_names] + [dfw, gmeta, loss.reshape(1)]
    shapes = [a.shape for a in parts]
    vec, _ = _pack_small(parts)
    red_vec = allreduce_small(vec, ex0l[4], name="allreduce_small")
    red = _unpack_small(red_vec, shapes)
    sg = dict(zip(small_names + ["final_norm_w", "meta_tokens"], red))
    loss = red[-1].reshape(())
    sg["conv_w"] = lax.dynamic_slice_in_dim(sg["conv_w"], chip * (cfg.conv_dim // 4), cfg.conv_dim // 4, axis=2)
    sg["meta_tokens"] = lax.dynamic_slice_in_dim(sg["meta_tokens"], chip * (cfg.d // 4), cfg.d // 4, axis=1)

    red0 = reduce_start(0, ["w_in"], "l", ex0l, red_vec)
    grads, deltas, new_m, new_v = {}, {}, {}, {}
    dep = red0[4]
    for k in names:
        if k in BIG:
            continue
        w2, g2, m2, v2 = _as2d(wts[k]), _as2d(sg[k]), _as2d(ms[k]), _as2d(vs[k])
        dl, mn, vn = adamw_small(w2, g2, m2, v2, dep, name=f"adamw_{k}")
        grads[k] = sg[k].reshape(wts[k].shape)
        deltas[k], new_m[k], new_v[k] = (t.reshape(wts[k].shape) for t in (dl, mn, vn))

    def view(k, a):
        return jnp.swapaxes(a, 1, 2) if k == "w_in" else a

    def gview(k, g):
        return g.T if k == "w_in" else g

    wv, mv, vv = ({k: view(k, t[k]) for k in BIG} for t in (wts, ms, vs))
    outs = {}
    for k in BIG:
        outs[k] = adamw_layer(wv[k], mv[k], vv[k], gview(k, big1[k]), 1, None, dep, name=f"adamw1_{k}")
        dep = outs[k][1]
    big0 = reduce_finish(0, REST, "e", st["red0e"], dep)
    for k in REST:
        outs[k] = adamw_layer(wv[k], mv[k], vv[k], big0[k], 0, outs[k], dep, name=f"adamw0_{k}")
        dep = outs[k][1]
    big0.update(reduce_finish(0, ["w_in"], "l", red0, dep))
    outs["w_in"] = adamw_layer(wv["w_in"], mv["w_in"], vv["w_in"], gview("w_in", big0["w_in"]), 0, outs["w_in"], dep,
                               name="adamw0_w_in")
    for k in BIG:
        grads[k], deltas[k], new_m[k], new_v[k] = (view(k, t) for t in outs[k])
    return (loss, grad_x, *[grads[k] for k in names], *[deltas[k] for k in names],
            *[new_m[k] for k in names], *[new_v[k] for k in names])


def adamw_small(w, g, m, v, dep, *, name):
    def body(w_ref, g_ref, m_ref, v_ref, dep_ref, d_ref, mo_ref, vo_ref):
        d_ref[...], mo_ref[...], vo_ref[...] = _adam_update(w_ref[...], g_ref[...], m_ref[...], v_ref[...])

    vm = pl.BlockSpec(memory_space=pltpu.VMEM)
    return pl.pallas_call(body, name=name, in_specs=[vm] * 4 + [pl.BlockSpec(memory_space=pl.ANY)], out_specs=[vm] * 3,
                          out_shape=[_sds(w.shape, F32)] * 3, compiler_params=_cp())(w, g, m, v, dep)
```

```python
import functools
from typing import NamedTuple

import numpy as np
import jax
import jax.numpy as jnp
from jax import lax
from jax.experimental import pallas as pl
from jax.experimental.pallas import tpu as pltpu

F32 = jnp.float32
BF16 = jnp.bfloat16
EPS = 1e-6
ROPE_THETA = 10000.0
LANE = 128
VMEM_LIMIT = 56 * 1024 * 1024
MASK_VALUE = -1e30
ADAM_LR, ADAM_B1, ADAM_B2, ADAM_EPS, ADAM_WD, ADAM_STEP = 0.001, 0.9, 0.999, 1e-08, 0.01, 10
MESH = pl.DeviceIdType.MESH


class Cfg(NamedTuple):
    d: int = 1024
    seq: int = 2048
    bsz: int = 2
    n_meta: int = 16
    inner: int = 2048
    hd: int = 64
    groups: int = 4
    state: int = 128
    convk: int = 4
    chunk: int = 128
    mh: int = 8
    ql: int = 512
    kvl: int = 256
    nope: int = 128
    rope: int = 64
    vd: int = 128
    ff: int = 4096

    @property
    def heads(self): return self.inner // self.hd
    @property
    def gw(self): return self.inner // self.groups
    @property
    def conv_dim(self): return self.inner + 2 * self.groups * self.state
    @property
    def pad(self): return self.chunk - self.n_meta
    @property
    def lp(self): return self.chunk + self.seq
    @property
    def t(self): return self.bsz * self.lp
    @property
    def nchunks(self): return self.lp // self.chunk
    @property
    def sw(self): return self.ql + self.kvl + 2 * LANE
    @property
    def kt(self): return (self.ql + self.kvl) // LANE
    @property
    def dtt(self): return self.kt + 1
    @property
    def qw(self): return self.mh * 2 * LANE
    @property
    def in_splits(self):
        return [self.inner, self.conv_dim, self.heads, self.ql, self.kvl, self.rope, self.d, self.d]


CFG = Cfg()


def _pick(dim, pref, mult):
    best = None
    for t in range(mult, min(dim, pref) + 1, mult):
        if dim % t == 0:
            best = t
    return best if best is not None else dim


def _cp(**kw):
    return pltpu.CompilerParams(vmem_limit_bytes=VMEM_LIMIT, **kw)


def _sds(shape, dtype):
    return jax.ShapeDtypeStruct(tuple(shape), dtype)


def _silu(x):
    return x * jax.nn.sigmoid(x)


def _dsilu(x):
    s = jax.nn.sigmoid(x)
    return s * (1.0 + x * (1.0 - s))


def _ep_plain(r):
    return (r,)


def _ep_add(r, res):
    return (r + res.astype(F32),)


def _ep_relu2(r):
    rp = jnp.maximum(r, 0.0)
    return r, rp * rp


def _ep_relu2_grad(r, a):
    return (r * (2.0 * jnp.maximum(a.astype(F32), 0.0)),)


MM_VMEM_BUDGET = 44 * 1024 * 1024


def _mm_tiles(m, n, k, a_bytes, b_bytes, io_bytes, ta):
    m_mult, m_cap = (LANE, 1024) if ta else (16, 1088)
    tms = [t for t in range(m_cap, 0, -m_mult) if m % t == 0] or [m]
    tns = [t for t in (1024, 512, 256, 128) if n % t == 0] or [n]
    best = None
    for tm in tms:
        for tn in tns:
            need = 2 * (tm * k * a_bytes + k * tn * b_bytes + tm * tn * io_bytes)
            if need <= MM_VMEM_BUDGET and (best is None or tm * tn > best[0] * best[1]):
                best = (tm, tn)
    if best is None:
        return (_pick(m, 512, m_mult), _pick(n, 512, LANE), _pick(k, 1088 if ta else 1024, 16 if ta else LANE))
    return best[0], best[1], k


def matmul(a, b, *, ta=False, tb=False, out_dtype=F32, add=None, name, tm=None, tn=None, tk=None,
           epilogue=None, extras=(), out_dtypes=None):
    if add is not None:
        epilogue, extras = _ep_add, (add,)
    if epilogue is None:
        epilogue = _ep_plain
    out_dtypes = tuple(out_dtypes) if out_dtypes is not None else (out_dtype,)
    n_ex, n_out = len(extras), len(out_dtypes)
    if ta:
        k_dim, m_dim = a.shape
    else:
        m_dim, k_dim = a.shape
    if tb:
        n_dim, k2 = b.shape
    else:
        k2, n_dim = b.shape
    assert k_dim == k2, (a.shape, b.shape, ta, tb)
    if tm is None and tn is None and tk is None:
        io_bytes = sum(jnp.dtype(e.dtype).itemsize for e in extras) + sum(jnp.dtype(d).itemsize for d in out_dtypes)
        tm, tn, tk = _mm_tiles(m_dim, n_dim, k_dim, jnp.dtype(a.dtype).itemsize, jnp.dtype(b.dtype).itemsize,
                               io_bytes, ta)
    elif ta:
        tm = tm or _pick(m_dim, 1024, LANE)
        tk = tk or _pick(k_dim, 1088, 16)
        tn = tn or _pick(n_dim, 1024, LANE)
    else:
        tm = tm or _pick(m_dim, 1088, 16)
        tk = tk or _pick(k_dim, 1024 if a.dtype == F32 else 2048, LANE)
        tn = tn or _pick(n_dim, 1024, LANE)
    nm, nn, nk = m_dim // tm, n_dim // tn, k_dim // tk
    dn = (((0 if ta else 1,), (1 if tb else 0,)), ((), ()))

    def body(*refs):
        a_ref, b_ref = refs[:2]
        ex_refs = refs[2:2 + n_ex]
        o_refs = refs[2 + n_ex:2 + n_ex + n_out]
        scr = refs[2 + n_ex + n_out:]
        p = lax.dot_general(a_ref[...].astype(BF16), b_ref[...].astype(BF16), dn, preferred_element_type=F32)

        def finish(r):
            outs = epilogue(r, *[e[...] for e in ex_refs])
            for o_ref, val, dt in zip(o_refs, outs, out_dtypes):
                o_ref[...] = val.astype(dt)

        if nk == 1:
            finish(p)
        else:
            acc = scr[0]
            k = pl.program_id(2)

            @pl.when(k == 0)
            def _():
                acc[...] = p

            @pl.when(k > 0)
            def _():
                acc[...] += p

            @pl.when(k == nk - 1)
            def _():
                finish(acc[...])

    a_spec = pl.BlockSpec((tk, tm), lambda i, j, k: (k, i)) if ta else pl.BlockSpec((tm, tk), lambda i, j, k: (i, k))
    b_spec = pl.BlockSpec((tn, tk), lambda i, j, k: (j, k)) if tb else pl.BlockSpec((tk, tn), lambda i, j, k: (k, j))
    o_spec = pl.BlockSpec((tm, tn), lambda i, j, k: (i, j))
    outs = pl.pallas_call(
        body, name=name, grid=(nm, nn, nk), in_specs=[a_spec, b_spec] + [o_spec] * n_ex, out_specs=[o_spec] * n_out,
        out_shape=[_sds((m_dim, n_dim), dt) for dt in out_dtypes],
        scratch_shapes=[pltpu.VMEM((tm, tn), F32)] if nk > 1 else [],
        compiler_params=_cp(dimension_semantics=("parallel", "parallel", "arbitrary")),
    )(a, b, *extras)
    return outs[0] if n_out == 1 else tuple(outs)


def matmul_multi(a, bs_, out_dtypes, *, name):
    m, k = a.shape
    ns = [b.shape[1] for b in bs_]
    cnt = len(bs_)
    out_row_bytes = sum(n * jnp.dtype(dt).itemsize for n, dt in zip(ns, out_dtypes))
    w_bytes = sum(k * n * jnp.dtype(b.dtype).itemsize for n, b in zip(ns, bs_))
    tm = next(t for t in range(1088, 0, -16)
              if m % t == 0 and w_bytes + 2 * t * (k * jnp.dtype(a.dtype).itemsize + out_row_bytes)
              + t * max(ns) * 4 <= MM_VMEM_BUDGET - (8 << 20))

    def body(*refs):
        a_ref = refs[0]
        b_refs, o_refs = refs[1:1 + cnt], refs[1 + cnt:]
        av = a_ref[...].astype(BF16)
        for b_ref, o_ref, dt in zip(b_refs, o_refs, out_dtypes):
            o_ref[...] = _nn(av, b_ref[...].astype(BF16)).astype(dt)

    return pl.pallas_call(
        body, name=name, grid=(m // tm,),
        in_specs=[pl.BlockSpec((tm, k), lambda i: (i, 0))]
        + [pl.BlockSpec((k, n), lambda i: (0, 0), pipeline_mode=pl.Buffered(1)) for n in ns],
        out_specs=[pl.BlockSpec((tm, n), lambda i: (i, 0)) for n in ns],
        out_shape=[_sds((m, n), dt) for n, dt in zip(ns, out_dtypes)], compiler_params=_cp(),
    )(a, *bs_)


def matmul_nt_sum(as_, bs_, *, out_dtype=F32, name, tiles=None):
    m, n = as_[0].shape[0], bs_[0].shape[0]
    ks = [a.shape[1] for a in as_]
    assert [b.shape[1] for b in bs_] == ks
    ksum, cnt = sum(ks), len(ks)
    best = tiles
    for tn in [t for t in (1024, 512, 256, 128) if n % t == 0]:
        for tm in [t for t in range(1088, 0, -16) if m % t == 0]:
            need = 2 * (tm * ksum * 2 + tn * ksum * 2 + tm * tn * jnp.dtype(out_dtype).itemsize)
            if best is None and need <= MM_VMEM_BUDGET and tm >= 256:
                best = (tm, tn)
    tm, tn = best

    def body(*refs):
        a_refs, b_refs, o_ref = refs[:cnt], refs[cnt:2 * cnt], refs[2 * cnt]
        acc = None
        for a_ref, b_ref in zip(a_refs, b_refs):
            p = _nt(a_ref[...].astype(BF16), b_ref[...].astype(BF16))
            acc = p if acc is None else acc + p
        o_ref[...] = acc.astype(out_dtype)

    return pl.pallas_call(
        body, name=name, grid=(n // tn, m // tm),
        in_specs=[pl.BlockSpec((tm, k), lambda j, i: (i, 0)) for k in ks]
        + [pl.BlockSpec((tn, k), lambda j, i: (j, 0)) for k in ks],
        out_specs=pl.BlockSpec((tm, tn), lambda j, i: (i, j)), out_shape=_sds((m, n), out_dtype),
        compiler_params=_cp(dimension_semantics=("parallel", "parallel")),
    )(*as_, *bs_)


def rmsnorm_fwd(x, w, *, cw=None, ci=0, name):
    t = x.shape[0]
    cw = cw or x.shape[1]
    tr = _pick(t, 544, 16)

    def body(x_ref, w_ref, o_ref):
        xv = x_ref[...].astype(F32)
        r = lax.rsqrt(jnp.mean(xv * xv, axis=-1, keepdims=True) + EPS)
        o_ref[...] = (xv * r * w_ref[...]).astype(BF16)

    return pl.pallas_call(
        body, name=name, grid=(t // tr,),
        in_specs=[pl.BlockSpec((tr, cw), lambda i: (i, ci)), pl.BlockSpec((1, cw), lambda i: (0, 0))],
        out_specs=pl.BlockSpec((tr, cw), lambda i: (i, 0)),
        out_shape=_sds((t, cw), BF16), compiler_params=_cp(),
    )(x, w.reshape(1, cw))


def rmsnorm_bwd(dy, x, w, *, cw=None, ci=0, res=None, out_dtype=F32, with_bf16=False, name):
    t = x.shape[0]
    cw = cw or x.shape[1]
    tr = _pick(t, 544, 16)
    has_res = res is not None

    def body(*refs):
        dxb_ref = None
        if with_bf16:
            refs, dxb_ref = refs[:-1], refs[-1]
        if has_res:
            dy_ref, x_ref, w_ref, res_ref, dx_ref, dw_ref = refs
        else:
            dy_ref, x_ref, w_ref, dx_ref, dw_ref = refs
        xv = x_ref[...].astype(F32)
        dyv = dy_ref[...].astype(F32)
        r = lax.rsqrt(jnp.mean(xv * xv, axis=-1, keepdims=True) + EPS)
        xh = xv * r
        g = dyv * w_ref[...]
        dx = r * (g - xh * jnp.mean(g * xh, axis=-1, keepdims=True))
        if has_res:
            dx = dx + res_ref[...]
        dx_ref[...] = dx.astype(out_dtype)
        if with_bf16:
            dxb_ref[...] = dx.astype(BF16)

        @pl.when(pl.program_id(0) == 0)
        def _():
            dw_ref[...] = jnp.zeros_like(dw_ref)

        dw_ref[...] += jnp.sum(dyv * xh, axis=0, keepdims=True)

    row = pl.BlockSpec((tr, cw), lambda i: (i, 0))
    in_specs = [row, pl.BlockSpec((tr, cw), lambda i: (i, ci)), pl.BlockSpec((1, cw), lambda i: (0, 0))]
    args = [dy, x, w.reshape(1, cw)]
    if has_res:
        in_specs.append(row)
        args.append(res)
    outs = pl.pallas_call(
        body, name=name, grid=(t // tr,), in_specs=in_specs,
        out_specs=[row, pl.BlockSpec((1, cw), lambda i: (0, 0))] + ([row] if with_bf16 else []),
        out_shape=[_sds((t, cw), out_dtype), _sds((1, cw), F32)] + ([_sds((t, cw), BF16)] if with_bf16 else []),
        compiler_params=_cp(),
    )(*args)
    if with_bf16:
        return outs[0], outs[1][0], outs[2]
    return outs[0], outs[1][0]


def _shift_down(x, s):
    return x if s == 0 else pltpu.roll(x, s, 0)


def _shift_up(x, s):
    return x if s == 0 else pltpu.roll(x, x.shape[0] - s, 0)


def _conv_pre(x, w_ref, b_ref, kk):
    pre = b_ref[...] + jnp.zeros_like(x)
    for k in range(kk):
        pre = pre + w_ref[k:k + 1, :] * _shift_down(x, kk - 1 - k)
    return pre


def conv_fwd(cfg, xbc, w, b, *, name):
    lp, cd, kk = cfg.lp, cfg.conv_dim, cfg.convk
    assert cfg.pad >= kk - 1
    cb = _pick(cd, 512, LANE)

    def body(x_ref, w_ref, b_ref, o_ref, ds_ref):
        pre = _conv_pre(x_ref[...], w_ref, b_ref, kk)
        sg = jax.nn.sigmoid(pre)
        o_ref[...] = pre * sg
        ds_ref[...] = (sg * (1.0 + pre * (1.0 - sg))).astype(BF16)

    blk = pl.BlockSpec((lp, cb), lambda j, bb: (bb, j))
    return pl.pallas_call(
        body, name=name, grid=(cd // cb, cfg.bsz),
        in_specs=[blk, pl.BlockSpec((kk, cb), lambda j, bb: (0, j)), pl.BlockSpec((1, cb), lambda j, bb: (0, j))],
        out_specs=[blk, blk], out_shape=[_sds((cfg.t, cd), F32), _sds((cfg.t, cd), BF16)], compiler_params=_cp(),
    )(xbc, w, b.reshape(1, cd))


def conv_bwd(cfg, xbc, w, dsilu, dxc, *, name):
    lp, cd, kk = cfg.lp, cfg.conv_dim, cfg.convk
    cb = _pick(cd, 512, LANE)

    def body(x_ref, w_ref, s_ref, d_ref, dx_ref, dw_ref, db_ref):
        x = x_ref[...]
        dpre = d_ref[...] * s_ref[...].astype(F32)
        dx = jnp.zeros_like(x)
        dws = []
        for k in range(kk):
            s = kk - 1 - k
            dx = dx + w_ref[k:k + 1, :] * _shift_up(dpre, s)
            dws.append(jnp.sum(dpre * _shift_down(x, s), axis=0, keepdims=True))
        dx_ref[...] = dx.astype(BF16)

        @pl.when(pl.program_id(1) == 0)
        def _():
            dw_ref[...] = jnp.zeros_like(dw_ref)
            db_ref[...] = jnp.zeros_like(db_ref)

        for k in range(kk):
            dw_ref[k:k + 1, :] += dws[k]
        db_ref[...] += jnp.sum(dpre, axis=0, keepdims=True)

    blk = pl.BlockSpec((lp, cb), lambda j, bb: (bb, j))
    wsp = pl.BlockSpec((kk, cb), lambda j, bb: (0, j))
    bsp = pl.BlockSpec((1, cb), lambda j, bb: (0, j))
    dx, dw, db = pl.pallas_call(
        body, name=name, grid=(cd // cb, cfg.bsz),
        in_specs=[blk, wsp, blk, blk], out_specs=[blk, wsp, bsp],
        out_shape=[_sds((cfg.t, cd), BF16), _sds((kk, cd), F32), _sds((1, cd), F32)], compiler_params=_cp(),
    )(xbc, w, dsilu, dxc)
    return dx, dw, db[0]


def _softplus(x):
    return jnp.maximum(x, 0.0) + jnp.log(1.0 + jnp.exp(-jnp.abs(x)))


def _ssd_consts(cfg):
    q = cfg.chunk
    i0 = np.arange(q)[:, None]
    i1 = np.arange(q)[None, :]
    ltri = (i1 <= i0).astype(np.float32)
    rexp = np.zeros((LANE, cfg.inner), np.float32)
    for h in range(cfg.heads):
        rexp[h, h * cfg.hd:(h + 1) * cfg.hd] = 1.0
    return jnp.asarray(ltri), jnp.asarray(rexp)


def _sel_dot(x, m, *, passes=2, left=False, trans=False):
    mb = m.astype(BF16)
    acc, rem = None, x
    for _ in range(passes):
        piece = rem.astype(BF16)
        if not left:
            part = _nn(piece, mb)
        elif trans:
            part = _tn(mb, piece)
        else:
            part = _nn(mb, piece)
        acc = part if acc is None else acc + part
        rem = rem - piece.astype(F32)
    return acc


def _ssd_chunk_common(cfg, raw, bias, avec, c_idx, ltri, rexp):
    q = cfg.chunk
    rows = lax.broadcasted_iota(jnp.int32, (q, LANE), 0)
    live = jnp.logical_or(c_idx > 0, rows >= cfg.pad)
    pre = raw + bias
    dt = jnp.where(live, _softplus(pre), 0.0)
    adt = dt * avec
    cs = _sel_dot(adt, ltri, passes=3, left=True)
    cs_t = cs.T
    cs_last = cs[q - 1:q, :]
    e_in = jnp.exp(cs)
    w0 = jnp.exp(cs_last - cs)
    decay = jnp.exp(cs_last)
    return dict(live=live, pre=pre, dt=dt, adt=adt, cs=cs, cs_t=cs_t, e_in=e_in, w0=w0, decay=decay,
                DT=_sel_dot(dt, rexp), E=_sel_dot(e_in, rexp), W0=_sel_dot(w0, rexp),
                DEC=_sel_dot(jnp.broadcast_to(decay, (8, LANE)), rexp)[0:1, :])


def _tri_masks(q):
    r = lax.broadcasted_iota(jnp.int32, (q, q), 0)
    c = lax.broadcasted_iota(jnp.int32, (q, q), 1)
    return c <= r, r <= c


def _head_l(cq, h, tri, tri_t):
    col = cq["cs"][:, h:h + 1]
    row = cq["cs_t"][h:h + 1, :]
    lmat = jnp.where(tri, jnp.exp(jnp.minimum(col - row, 0.0)), 0.0)
    lmat_t = jnp.where(tri_t, jnp.exp(jnp.minimum(row - col, 0.0)), 0.0)
    return lmat, lmat_t


def _nt(a, b):
    return lax.dot_general(a, b, (((1,), (1,)), ((), ())), preferred_element_type=F32)


def _tn(a, b):
    return lax.dot_general(a, b, (((0,), (0,)), ((), ())), preferred_element_type=F32)


def _nn(a, b):
    return jnp.dot(a, b, preferred_element_type=F32)


def ssd_fwd(cfg, xc, small, dt_bias, avec, dexp, *, name):
    q, inner, st, gw, g_n = cfg.chunk, cfg.inner, cfg.state, cfg.gw, cfg.groups
    nc = cfg.nchunks
    ltri, rexp = _ssd_consts(cfg)
    hpt = LANE // cfg.hd
    tiles_per_group = gw // LANE

    bsz, lp = cfg.bsz, cfg.lp
    bcw = g_n * st

    def body(x_ref, b_ref, c_ref, dt_ref, bias_ref, a_ref, d_ref, ltri_ref, rexp_ref, y_ref, sin_ref, s_scr):
        c_idx = pl.program_id(0)

        @pl.when(c_idx == 0)
        def _():
            s_scr[...] = jnp.zeros_like(s_scr)

        ltri_v = ltri_ref[...]
        tri, tri_t = _tri_masks(q)
        lane = lax.broadcasted_iota(jnp.int32, (q, LANE), 1)
        for bi in range(bsz):
            cq = _ssd_chunk_common(cfg, dt_ref[bi], bias_ref[...], a_ref[...], c_idx, ltri_v, rexp_ref[...])
            xs = x_ref[bi]
            xdt = (xs * cq["DT"]).astype(BF16)
            xw = (xs * cq["DT"] * cq["W0"]).astype(BF16)
            s_in = s_scr[bi]
            sin_ref[bi, 0] = s_in
            for g in range(g_n):
                bg = b_ref[bi, :, g * st:(g + 1) * st].astype(BF16)
                cg = c_ref[bi, :, g * st:(g + 1) * st].astype(BF16)
                gmat = _nt(cg, bg)
                gs = slice(g * gw, (g + 1) * gw)
                y0 = _nn(cg, s_in[:, gs].astype(BF16))
                for tt in range(tiles_per_group):
                    tile = g * tiles_per_group + tt
                    ts = slice(tile * LANE, (tile + 1) * LANE)
                    xt = xdt[:, ts]
                    ms, xh = [], []
                    for hh in range(hpt):
                        lmat, _ = _head_l(cq, tile * hpt + hh, tri, tri_t)
                        ms.append((gmat * lmat).astype(BF16))
                        inhead = jnp.logical_and(lane >= hh * cfg.hd, lane < (hh + 1) * cfg.hd)
                        xh.append(jnp.where(inhead, xt, jnp.zeros_like(xt)))
                    yd = _nn(jnp.concatenate(ms, axis=1), jnp.concatenate(xh, axis=0))
                    y_ref[bi, :, ts] = (yd + y0[:, tt * LANE:(tt + 1) * LANE] * cq["E"][:, ts]
                                        + xs[:, ts] * d_ref[:, ts]).astype(BF16)
                s_scr[bi, :, gs] = s_in[:, gs] * cq["DEC"][:, gs] + _tn(bg, xw[:, gs])

    def rowblk(width, col):
        return pl.BlockSpec((bsz, q, width), lambda c: (0, c, col))

    def const(shape):
        return pl.BlockSpec(shape, lambda c: (0, 0))

    xc3 = xc.reshape(bsz, lp, cfg.conv_dim)
    y, sin = pl.pallas_call(
        body, name=name, grid=(nc,),
        in_specs=[rowblk(inner, 0), rowblk(bcw, inner // bcw), rowblk(bcw, inner // bcw + 1),
                  rowblk(LANE, cfg.dtt), const((1, LANE)), const((1, LANE)), const((1, inner)),
                  const((q, q)), const((LANE, inner))],
        out_specs=[rowblk(inner, 0), pl.BlockSpec((bsz, 1, st, inner), lambda c: (0, c, 0, 0))],
        out_shape=[_sds((bsz, lp, inner), BF16), _sds((bsz, nc, st, inner), F32)],
        scratch_shapes=[pltpu.VMEM((bsz, st, inner), F32)], compiler_params=_cp(),
    )(xc3, xc3, xc3, small.reshape(bsz, lp, cfg.sw), dt_bias, avec, dexp, ltri, rexp)
    return y.reshape(cfg.t, inner), sin.reshape(bsz * nc, st, inner)


def ssd_bwd(cfg, xc, small, dt_bias, avec, dexp, sin, dy, *, name):
    q, inner, st, gw, g_n = cfg.chunk, cfg.inner, cfg.state, cfg.gw, cfg.groups
    nc = cfg.nchunks
    ltri, rexp = _ssd_consts(cfg)
    rexp_t = rexp.T
    hpt = LANE // cfg.hd
    tiles_per_group = gw // LANE
    bcw = g_n * st

    def body(x_ref, b_ref, c_ref, dt_ref, bias_ref, a_ref, d_ref, ltri_ref, rexp_ref, rexpt_ref, sin_ref, dy_ref,
             dx_ref, ddt_ref, dd_ref, da_ref, dbias_ref, ds_scr):
        step = pl.program_id(1)
        c_idx = nc - 1 - step

        @pl.when(step == 0)
        def _():
            ds_scr[...] = jnp.zeros_like(ds_scr)

        @pl.when(jnp.logical_and(step == 0, pl.program_id(0) == 0))
        def _():
            dd_ref[...] = jnp.zeros_like(dd_ref)
            da_ref[...] = jnp.zeros_like(da_ref)
            dbias_ref[...] = jnp.zeros_like(dbias_ref)

        ltri_v = ltri_ref[...]
        tri, tri_t = _tri_masks(q)
        red = _sel_dot
        rexpt = rexpt_ref[...]
        cq = _ssd_chunk_common(cfg, dt_ref[...], bias_ref[...], a_ref[...], c_idx, ltri_v, rexp_ref[...])
        xs = x_ref[...]
        dyv = dy_ref[...].astype(F32)
        s_in = sin_ref[0]
        d_s = ds_scr[...]
        xdt_f = xs * cq["DT"]
        xdt = xdt_f.astype(BF16)
        xw_f = xdt_f * cq["W0"]
        xw = xw_f.astype(BF16)
        lane = lax.broadcasted_iota(jnp.int32, (q, LANE), 1)
        sub = lax.broadcasted_iota(jnp.int32, (LANE, q), 0)

        dd_ref[...] += jnp.sum(dyv * xs, axis=0, keepdims=True)
        dy0 = dyv * cq["E"]
        dcs = jnp.zeros((q, LANE), F32)
        dcs_t = jnp.zeros((LANE, q), F32)
        for g in range(g_n):
            bg_f = b_ref[:, g * st:(g + 1) * st]
            cg_f = c_ref[:, g * st:(g + 1) * st]
            bg = bg_f.astype(BF16)
            cg = cg_f.astype(BF16)
            gs = slice(g * gw, (g + 1) * gw)
            gmat = _nt(cg, bg)
            gmat_t = _nt(bg, cg)
            sing = s_in[:, gs].astype(BF16)
            dsg = d_s[:, gs].astype(BF16)
            y0 = _nn(cg, sing)
            dxw = _nn(bg, dsg)
            d_bg = _nt(xw[:, gs], dsg)
            d_cg = _nt(dy0[:, gs].astype(BF16), sing)
            ds_in_g = _tn(cg, dy0[:, gs].astype(BF16))
            dg = jnp.zeros((q, q), F32)
            dxdt_g = []
            for tt in range(tiles_per_group):
                tile = g * tiles_per_group + tt
                ts = slice(tile * LANE, (tile + 1) * LANE)
                xt = xdt[:, ts]
                dyt = dyv[:, ts]
                dyhs, lmats, mts = [], [], []
                for hh in range(hpt):
                    lmat, lmat_t = _head_l(cq, tile * hpt + hh, tri, tri_t)
                    inhead = jnp.logical_and(lane >= hh * cfg.hd, lane < (hh + 1) * cfg.hd)
                    dyhs.append(jnp.where(inhead, dyt, 0.0).astype(BF16))
                    lmats.append(lmat)
                    mts.append((gmat_t * lmat_t).astype(BF16))
                dy_stack = jnp.concatenate(dyhs, axis=0)
                dm_all = _nt(dy_stack, xt)
                for hh in range(hpt):
                    h = tile * hpt + hh
                    dm = dm_all[hh * q:(hh + 1) * q, :]
                    dg = dg + dm * lmats[hh]
                    qm = dm * gmat * lmats[hh]
                    rs = jnp.sum(qm, axis=1, keepdims=True)
                    csum = jnp.sum(qm, axis=0, keepdims=True)
                    dcs = dcs + jnp.where(lane == h, rs, 0.0)
                    dcs_t = dcs_t + jnp.where(sub == h, csum, 0.0)
                dxdt_g.append(_nn(jnp.concatenate(mts, axis=1), dy_stack))
            dxdt_diag = jnp.concatenate(dxdt_g, axis=1) if len(dxdt_g) > 1 else dxdt_g[0]
            dgb = dg.astype(BF16)
            d_cg = d_cg + _nn(dgb, bg)
            d_bg = d_bg + _tn(dgb, cg)
            dx_ref[:, inner + g * st:inner + (g + 1) * st] = d_bg
            dx_ref[:, inner + bcw + g * st:inner + bcw + (g + 1) * st] = d_cg
            dxdt = dxdt_diag + dxw * cq["W0"][:, gs]
            dx_ref[:, gs] = dyv[:, gs] * d_ref[:, gs] + dxdt * cq["DT"][:, gs]
            rt = rexpt[gs, :]
            dcs = dcs + red(dyv[:, gs] * y0 * cq["E"][:, gs], rt)
            r_w = red(dxw * xw_f[:, gs], rt)
            dcs = dcs - r_w
            dcs_last_g = jnp.sum(r_w, axis=0, keepdims=True)
            ddec = red(jnp.broadcast_to(jnp.sum(d_s[:, gs] * s_in[:, gs], axis=0, keepdims=True), (8, gw)), rt)[0:1, :]
            dcs_last_g = dcs_last_g + ddec * cq["decay"]
            dcs = dcs + jnp.where(lax.broadcasted_iota(jnp.int32, (q, LANE), 0) == q - 1, dcs_last_g, 0.0)
            ddt_part = red(dxdt * xs[:, gs], rt)
            if g == 0:
                ddt = ddt_part
            else:
                ddt = ddt + ddt_part
            ds_scr[:, gs] = d_s[:, gs] * cq["DEC"][:, gs] + ds_in_g
        dcs = dcs - dcs_t.T
        dadt = _sel_dot(dcs, ltri_v, left=True, trans=True)
        ddt = ddt + dadt * a_ref[...]
        da_ref[...] += jnp.sum(dadt * cq["dt"], axis=0, keepdims=True)
        draw = jnp.where(cq["live"], ddt * jax.nn.sigmoid(cq["pre"]), 0.0)
        ddt_ref[...] = draw
        dbias_ref[...] += jnp.sum(draw, axis=0, keepdims=True)

    def rowblk(width, col):
        return pl.BlockSpec((q, width), lambda b, s: (b * nc + nc - 1 - s, col))

    def const(shape):
        return pl.BlockSpec(shape, lambda b, s: (0, 0))

    bcol = inner // bcw
    outs = pl.pallas_call(
        body, name=name, grid=(cfg.bsz, nc),
        in_specs=[rowblk(inner, 0), rowblk(bcw, bcol), rowblk(bcw, bcol + 1), rowblk(LANE, cfg.dtt),
                  const((1, LANE)), const((1, LANE)), const((1, inner)), const((q, q)), const((LANE, inner)),
                  const((inner, LANE)),
                  pl.BlockSpec((1, st, inner), lambda b, s: (b * nc + nc - 1 - s, 0, 0)), rowblk(inner, 0)],
        out_specs=[rowblk(cfg.conv_dim, 0), rowblk(LANE, 0),
                   const((1, inner)), const((1, LANE)), const((1, LANE))],
        out_shape=[_sds((cfg.t, cfg.conv_dim), F32),
                   _sds((cfg.t, LANE), F32), _sds((1, inner), F32), _sds((1, LANE), F32), _sds((1, LANE), F32)],
        scratch_shapes=[pltpu.VMEM((st, inner), F32)], compiler_params=_cp(),
    )(xc, xc, xc, small, dt_bias, avec, dexp, ltri, rexp, rexp_t, sin, dy)
    return outs


def tail_fwd(cfg, y, z, w, *, name):
    t, inner, gw = cfg.t, cfg.inner, cfg.gw
    tr = _pick(t, 272, 16)

    def body(y_ref, z_ref, w_ref, o_ref):
        for g in range(cfg.groups):
            gs = slice(g * gw, (g + 1) * gw)
            yg = y_ref[:, gs].astype(F32) * _silu(z_ref[:, gs].astype(F32))
            r = lax.rsqrt(jnp.mean(yg * yg, axis=-1, keepdims=True) + EPS)
            o_ref[:, gs] = (yg * r * w_ref[:, gs]).astype(BF16)

    row = pl.BlockSpec((tr, inner), lambda i: (i, 0))
    return pl.pallas_call(
        body, name=name, grid=(t // tr,), in_specs=[row, row, pl.BlockSpec((1, inner), lambda i: (0, 0))],
        out_specs=row, out_shape=_sds((t, inner), BF16), compiler_params=_cp(),
    )(y, z, w.reshape(1, inner))


def tail_bwd(cfg, do, y, z, w, *, name):
    t, inner, gw = cfg.t, cfg.inner, cfg.gw
    tr = _pick(t, 272, 16)

    def body(do_ref, y_ref, z_ref, w_ref, dy_ref, dz_ref, dw_ref):
        @pl.when(pl.program_id(0) == 0)
        def _():
            dw_ref[...] = jnp.zeros_like(dw_ref)

        for g in range(cfg.groups):
            gs = slice(g * gw, (g + 1) * gw)
            yv = y_ref[:, gs].astype(F32)
            zv = z_ref[:, gs].astype(F32)
            dov = do_ref[:, gs].astype(F32)
            sz = _silu(zv)
            yg = yv * sz
            r = lax.rsqrt(jnp.mean(yg * yg, axis=-1, keepdims=True) + EPS)
            xh = yg * r
            gg = dov * w_ref[:, gs]
            dyg = r * (gg - xh * jnp.mean(gg * xh, axis=-1, keepdims=True))
            dw_ref[:, gs] += jnp.sum(dov * xh, axis=0, keepdims=True)
            dy_ref[:, gs] = (dyg * sz).astype(BF16)
            dz_ref[:, gs] = (dyg * yv * _dsilu(zv)).astype(BF16)

    row = pl.BlockSpec((tr, inner), lambda i: (i, 0))
    vec = pl.BlockSpec((1, inner), lambda i: (0, 0))
    dy, dz, dw = pl.pallas_call(
        body, name=name, grid=(t // tr,), in_specs=[row, row, row, vec], out_specs=[row, row, vec],
        out_shape=[_sds((t, inner), BF16), _sds((t, inner), BF16), _sds((1, inner), F32)], compiler_params=_cp(),
    )(do, y, z, w.reshape(1, inner))
    return dy, dz, dw[0]


def rope_tables(cfg):
    half = cfg.rope // 2
    pos = np.maximum(np.arange(cfg.lp) - cfg.pad, 0).astype(np.float32)
    inv = ROPE_THETA ** (-jnp.arange(0, cfg.rope, 2, dtype=F32) / cfg.rope)
    ang = jnp.asarray(pos)[:, None] * inv[None, :]
    cos, sin = jnp.cos(ang), jnp.sin(ang)
    zero = jnp.zeros((cfg.lp, LANE - 2 * half), F32)
    zh = jnp.zeros((cfg.lp, half), F32)
    ctab = jnp.concatenate([cos, cos, zero], axis=1)
    s1 = jnp.concatenate([-sin, zh, zero], axis=1)
    s2 = jnp.concatenate([zh, sin, zero], axis=1)
    return ctab, s1, s2


def _rope(x, c, s1, s2, half):
    return x * c + pltpu.roll(x, LANE - half, 1) * s1 + pltpu.roll(x, half, 1) * s2


def _rope_t(dy, c, s1, s2, half):
    return dy * c + pltpu.roll(dy * s1, half, 1) + pltpu.roll(dy * s2, LANE - half, 1)


def _attn_scale(cfg):
    return (cfg.nope + cfg.rope) ** -0.5


def rope_fwd(cfg, qf, small, tabs, *, name):
    t, qw, lp = cfg.t, cfg.qw, cfg.lp
    tr = _pick(lp, 544, 16)
    nrb = lp // tr
    half = cfg.rope // 2
    scale = _attn_scale(cfg)

    def body(q_ref, k_ref, c_ref, s1_ref, s2_ref, qo_ref, ko_ref):
        c, s1, s2 = c_ref[...], s1_ref[...], s2_ref[...]
        for h in range(cfg.mh):
            a = h * 2 * LANE
            qo_ref[:, a:a + LANE] = (q_ref[:, a:a + LANE].astype(F32) * scale).astype(BF16)
            qo_ref[:, a + LANE:a + 2 * LANE] = (
                _rope(q_ref[:, a + LANE:a + 2 * LANE].astype(F32), c, s1, s2, half) * scale).astype(BF16)
        ko_ref[...] = _rope(k_ref[...], c, s1, s2, half).astype(BF16)

    tab = pl.BlockSpec((tr, LANE), lambda i: (i % nrb, 0))
    return pl.pallas_call(
        body, name=name, grid=(t // tr,),
        in_specs=[pl.BlockSpec((tr, qw), lambda i: (i, 0)), pl.BlockSpec((tr, LANE), lambda i: (i, cfg.kt)), tab, tab, tab],
        out_specs=[pl.BlockSpec((tr, qw), lambda i: (i, 0)), pl.BlockSpec((tr, LANE), lambda i: (i, 0))],
        out_shape=[_sds((t, qw), BF16), _sds((t, LANE), BF16)], compiler_params=_cp(),
    )(qf, small, *tabs)


def rope_bwd(cfg, dq, dkpe, tabs, *, name):
    t, qw, lp = cfg.t, cfg.qw, cfg.lp
    tr = _pick(lp, 544, 16)
    nrb = lp // tr
    half = cfg.rope // 2
    scale = _attn_scale(cfg)

    def body(dq_ref, dk_ref, c_ref, s1_ref, s2_ref, qo_ref, ko_ref):
        c, s1, s2 = c_ref[...], s1_ref[...], s2_ref[...]
        for h in range(cfg.mh):
            a = h * 2 * LANE
            qo_ref[:, a:a + LANE] = (dq_ref[:, a:a + LANE].astype(F32) * scale).astype(BF16)
            qo_ref[:, a + LANE:a + 2 * LANE] = _rope_t(
                dq_ref[:, a + LANE:a + 2 * LANE].astype(F32) * scale, c, s1, s2, half).astype(BF16)
        dk = dk_ref[0]
        for h in range(1, cfg.mh):
            dk = dk + dk_ref[h]
        ko_ref[...] = _rope_t(dk, c, s1, s2, half)

    tab = pl.BlockSpec((tr, LANE), lambda i: (i % nrb, 0))
    return pl.pallas_call(
        body, name=name, grid=(t // tr,),
        in_specs=[pl.BlockSpec((tr, qw), lambda i: (i, 0)), pl.BlockSpec((cfg.mh, tr, LANE), lambda i: (0, i, 0)),
                  tab, tab, tab],
        out_specs=[pl.BlockSpec((tr, qw), lambda i: (i, 0)), pl.BlockSpec((tr, LANE), lambda i: (i, 0))],
        out_shape=[_sds((t, qw), BF16), _sds((t, LANE), F32)], compiler_params=_cp(),
    )(dq, dkpe, *tabs)


def _q_blocks(cfg):
    bounds = [0, cfg.chunk] + list(range(cfg.chunk + 256, cfg.lp + 1, 256))
    assert bounds[-1] == cfg.lp, "SEQ must be a multiple of 256"
    return list(zip(bounds[:-1], bounds[1:]))


def _attn_mask(cfg, qs, qe):
    rows = qs + lax.broadcasted_iota(jnp.int32, (qe - qs, qe), 0)
    cols = lax.broadcasted_iota(jnp.int32, (qe - qs, qe), 1)
    return jnp.logical_and(cols <= rows, jnp.logical_or(cols >= cfg.pad, rows < cfg.pad))


def _max_q_block(cfg):
    return max(qe - qs for qs, qe in _q_blocks(cfg))


def _masked_scores(cfg, q, k2, qs, qe, s_scr):
    bq, n = qe - qs, qe
    s_scr[0:bq, 0:n] = _nt(q, k2)
    if qs == 0:
        s_scr[0:bq, 0:n] = jnp.where(_attn_mask(cfg, 0, qe), s_scr[0:bq, 0:n], MASK_VALUE)
    else:
        assert qs >= cfg.chunk and cfg.pad < LANE
        cols = lax.broadcasted_iota(jnp.int32, (bq, LANE), 1)
        s_scr[0:bq, 0:LANE] = jnp.where(cols >= cfg.pad, s_scr[0:bq, 0:LANE], MASK_VALUE)
        r = lax.broadcasted_iota(jnp.int32, (bq, bq), 0)
        c = lax.broadcasted_iota(jnp.int32, (bq, bq), 1)
        s_scr[0:bq, qs:qe] = jnp.where(c <= r, s_scr[0:bq, qs:qe], MASK_VALUE)
    return s_scr[0:bq, 0:n]


def attn_fwd(cfg, qr, kv, kpe, *, name):
    lp, t, mh = cfg.lp, cfg.t, cfg.mh
    blocks = _q_blocks(cfg)

    def body(q_ref, kv_ref, kp_ref, o_ref, l_ref, s_scr):
        for qs, qe in blocks:
            n = qe
            q = q_ref[qs:qe, :]
            k2 = jnp.concatenate([kv_ref[0:n, 0:LANE], kp_ref[0:n, :]], axis=1)
            s = _masked_scores(cfg, q, k2, qs, qe, s_scr)
            m = jnp.max(s, axis=-1, keepdims=True)
            p = jnp.exp(s - m)
            l = jnp.sum(p, axis=-1, keepdims=True)
            o_ref[qs:qe, :] = (_nn(p.astype(BF16), kv_ref[0:n, LANE:2 * LANE]) * (1.0 / l)).astype(BF16)
            l_ref[qs:qe, :] = jnp.broadcast_to(m + jnp.log(l), (qe - qs, LANE))

    hb = pl.BlockSpec((lp, 2 * LANE), lambda b, h: (b, h))
    ob = pl.BlockSpec((lp, LANE), lambda b, h: (b, h))
    return pl.pallas_call(
        body, name=name, grid=(cfg.bsz, mh),
        in_specs=[hb, hb, pl.BlockSpec((lp, LANE), lambda b, h: (b, 0))], out_specs=[ob, ob],
        out_shape=[_sds((t, mh * LANE), BF16), _sds((t, mh * LANE), F32)],
        scratch_shapes=[pltpu.VMEM((_max_q_block(cfg), lp), F32)], compiler_params=_cp(),
    )(qr, kv, kpe)


def attn_bwd(cfg, qr, kv, kpe, o, lse, do, *, name):
    lp, t, mh = cfg.lp, cfg.t, cfg.mh
    blocks = _q_blocks(cfg)

    def body(q_ref, kv_ref, kp_ref, o_ref, l_ref, do_ref, dq_ref, dkv_ref, dkp_ref, dk_acc, dv_acc, s_scr):
        dk_acc[...] = jnp.zeros_like(dk_acc)
        dv_acc[...] = jnp.zeros_like(dv_acc)
        for qs, qe in blocks:
            n = qe
            q = q_ref[qs:qe, :]
            k2 = jnp.concatenate([kv_ref[0:n, 0:LANE], kp_ref[0:n, :]], axis=1)
            dob = do_ref[qs:qe, :].astype(BF16)
            delta = jnp.sum(dob.astype(F32) * o_ref[qs:qe, :].astype(F32), axis=-1, keepdims=True)
            s = _masked_scores(cfg, q, k2, qs, qe, s_scr)
            p = jnp.exp(s - l_ref[qs:qe, 0:1])
            dp = _nt(dob, kv_ref[0:n, LANE:2 * LANE])
            ds = (p * (dp - delta)).astype(BF16)
            dq_ref[qs:qe, :] = _nn(ds, k2).astype(BF16)
            dv_acc[0:n, :] += _tn(p.astype(BF16), dob)
            dk_acc[0:n, :] += _tn(ds, q)
        dkv_ref[:, 0:LANE] = dk_acc[:, 0:LANE].astype(BF16)
        dkv_ref[:, LANE:2 * LANE] = dv_acc[...].astype(BF16)
        dkp_ref[0] = dk_acc[:, LANE:2 * LANE]

    hb = pl.BlockSpec((lp, 2 * LANE), lambda b, h: (b, h))
    ob = pl.BlockSpec((lp, LANE), lambda b, h: (b, h))
    return pl.pallas_call(
        body, name=name, grid=(cfg.bsz, mh),
        in_specs=[hb, hb, pl.BlockSpec((lp, LANE), lambda b, h: (b, 0)), ob, ob, ob],
        out_specs=[hb, hb, pl.BlockSpec((1, lp, LANE), lambda b, h: (h, b, 0))],
        out_shape=[_sds((t, cfg.qw), BF16), _sds((t, mh * 2 * LANE), BF16), _sds((mh, t, LANE), F32)],
        scratch_shapes=[pltpu.VMEM((lp, 2 * LANE), F32), pltpu.VMEM((lp, LANE), F32),
                        pltpu.VMEM((_max_q_block(cfg), lp), F32)], compiler_params=_cp(),
    )(qr, kv, kpe, o, lse, do)


def _live_rows(cfg, tr, shape):
    rows = pl.program_id(1) * tr + lax.broadcasted_iota(jnp.int32, shape, 0)
    return rows >= cfg.pad


def gate_fwd(cfg, ya, yb, g, *, name):
    d, lp = cfg.d, cfg.lp
    tr = _pick(lp, 544, 16)
    nrb = lp // tr

    def body(ya_ref, yb_ref, ga_ref, gb_ref, o_ref):
        f = lambda ref: ref[...].astype(F32)
        mix = jax.nn.sigmoid(f(ga_ref)) * f(ya_ref) + jax.nn.sigmoid(f(gb_ref)) * f(yb_ref)
        o_ref[...] = jnp.where(_live_rows(cfg, tr, mix.shape), mix, 0.0).astype(BF16)

    row = pl.BlockSpec((tr, d), lambda b, j: (b * nrb + j, 0))
    row1 = pl.BlockSpec((tr, d), lambda b, j: (b * nrb + j, 1))
    return pl.pallas_call(
        body, name=name, grid=(cfg.bsz, nrb), in_specs=[row, row, row, row1], out_specs=row,
        out_shape=_sds((cfg.t, d), BF16), compiler_params=_cp(),
    )(ya, yb, g, g)


def gate_bwd(cfg, dmix, ya, yb, g, *, name):
    d, lp = cfg.d, cfg.lp
    tr = _pick(lp, 544, 16)
    nrb = lp // tr

    def body(dm_ref, ya_ref, yb_ref, ga_ref, gb_ref, dya_ref, dyb_ref, dg_ref):
        dm = dm_ref[...].astype(F32)
        dm = jnp.where(_live_rows(cfg, tr, dm.shape), dm, 0.0)
        sa = jax.nn.sigmoid(ga_ref[...].astype(F32))
        sb = jax.nn.sigmoid(gb_ref[...].astype(F32))
        dya_ref[...] = (dm * sa).astype(BF16)
        dyb_ref[...] = (dm * sb).astype(BF16)
        dg_ref[:, 0:d] = (dm * ya_ref[...].astype(F32) * sa * (1.0 - sa)).astype(BF16)
        dg_ref[:, d:2 * d] = (dm * yb_ref[...].astype(F32) * sb * (1.0 - sb)).astype(BF16)

    row = pl.BlockSpec((tr, d), lambda b, j: (b * nrb + j, 0))
    row1 = pl.BlockSpec((tr, d), lambda b, j: (b * nrb + j, 1))
    row2 = pl.BlockSpec((tr, 2 * d), lambda b, j: (b * nrb + j, 0))
    return pl.pallas_call(
        body, name=name, grid=(cfg.bsz, nrb), in_specs=[row, row, row, row, row1], out_specs=[row, row, row2],
        out_shape=[_sds((cfg.t, d), BF16), _sds((cfg.t, d), BF16), _sds((cfg.t, 2 * d), BF16)], compiler_params=_cp(),
    )(dmix, ya, yb, g, g)


def loss_head(cfg, h, target, w, *, name):
    d, q, nc = cfg.d, cfg.chunk, cfg.nchunks
    tpb = cfg.seq // q

    def body(h_ref, t_ref, w_ref, loss_ref, dh_ref, dw_ref, dhb_ref):
        j = pl.program_id(1)

        @pl.when(jnp.logical_and(j == 0, pl.program_id(0) == 0))
        def _():
            loss_ref[...] = jnp.zeros_like(loss_ref)
            dw_ref[...] = jnp.zeros_like(dw_ref)

        @pl.when(j == 0)
        def _():
            dh_ref[...] = jnp.zeros_like(dh_ref)
            dhb_ref[...] = jnp.zeros_like(dhb_ref)

        @pl.when(j > 0)
        def _():
            xv = h_ref[...]
            r = lax.rsqrt(jnp.mean(xv * xv, axis=-1, keepdims=True) + EPS)
            xh = xv * r
            err = xh * w_ref[...] - t_ref[...]
            loss_ref[...] += 0.5 * jnp.sum(jnp.sum(err * err, axis=-1, keepdims=True) / d, axis=0, keepdims=True)
            dy = err * (1.0 / d)
            g = dy * w_ref[...]
            dh = r * (g - xh * jnp.mean(g * xh, axis=-1, keepdims=True))
            dh_ref[...] = dh
            dhb_ref[...] = dh.astype(BF16)
            dw_ref[...] += jnp.sum(dy * xh, axis=0, keepdims=True)

    row = pl.BlockSpec((q, d), lambda b, j: (b * nc + j, 0))
    loss, dh, dw, dhb = pl.pallas_call(
        body, name=name, grid=(cfg.bsz, nc),
        in_specs=[row, pl.BlockSpec((q, d), lambda b, j: (b * tpb + jnp.maximum(j - 1, 0), 0)),
                  pl.BlockSpec((1, d), lambda b, j: (0, 0))],
        out_specs=[pl.BlockSpec((8, LANE), lambda b, j: (0, 0)), row, pl.BlockSpec((1, d), lambda b, j: (0, 0)), row],
        out_shape=[_sds((8, LANE), F32), _sds((cfg.t, d), F32), _sds((1, d), F32), _sds((cfg.t, d), BF16)],
        compiler_params=_cp(),
    )(h, target, w.reshape(1, d))
    return loss[0, 0], (dh, dhb), dw[0]


def _rows_tile(r, c):
    return _pick(r, max(8, (1 << 18) // max(c, 1) // 8 * 8), 8)


def _adam_update(w, g, m, v):
    c1 = 1.0 - ADAM_B1 ** ADAM_STEP
    c2 = 1.0 - ADAM_B2 ** ADAM_STEP
    mn = ADAM_B1 * m + (1.0 - ADAM_B1) * g
    vn = ADAM_B2 * v + (1.0 - ADAM_B2) * (g * g)
    delta = -ADAM_LR * ((mn / c1) / (jnp.sqrt(vn / c2) + ADAM_EPS) + ADAM_WD * w)
    return delta, mn, vn


def adamw_layer(w, m, v, g, li, prev, dep, *, name):
    _, r, c = w.shape
    tr = _rows_tile(r, c)

    def body(*refs):
        w_ref, m_ref, v_ref, g_ref = refs[:4]
        go_ref, d_ref, mo_ref, vo_ref = refs[-4:]
        gv = g_ref[...]
        delta, mn, vn = _adam_update(w_ref[0], gv, m_ref[0], v_ref[0])
        go_ref[0] = gv
        d_ref[0] = delta
        mo_ref[0] = mn
        vo_ref[0] = vn

    if tr * c * 4 >= (1 << 16):
        steps = r // tr
        blk3 = pl.BlockSpec((1, tr, c), lambda i: (li, i, 0))
        blk2 = pl.BlockSpec((tr, c), lambda i: (i, 0))
    else:
        tc = _pick(c, max(LANE, (1 << 18) // r // LANE * LANE), LANE)
        steps = c // tc
        blk3 = pl.BlockSpec((1, r, tc), lambda i: (li, 0, i))
        blk2 = pl.BlockSpec((r, tc), lambda i: (0, i))
    anyspec = pl.BlockSpec(memory_space=pl.ANY)
    in_specs = [blk3, blk3, blk3, blk2, anyspec]
    args = [w, m, v, g, dep]
    aliases = {}
    if prev is not None:
        in_specs += [anyspec] * 4
        args += list(prev)
        aliases = {5 + i: i for i in range(4)}
    return pl.pallas_call(
        body, name=name, grid=(steps,), in_specs=in_specs, out_specs=[blk3] * 4,
        out_shape=[_sds(w.shape, F32)] * 4, input_output_aliases=aliases, compiler_params=_cp(),
    )(*args)


def pair_add(g4, other, half, *, name):
    n, _, r, c = g4.shape
    tr = _rows_tile(r, c)

    def body(h_ref, a_ref, b_ref, o_ref):
        o_ref[0] = (a_ref[0, 0].astype(F32) + b_ref[0].astype(F32)).astype(BF16)

    blk = pl.BlockSpec((1, tr, c), lambda j, i, h: (j, i, 0))
    grid_spec = pltpu.PrefetchScalarGridSpec(
        num_scalar_prefetch=1, grid=(n, r // tr),
        in_specs=[pl.BlockSpec((1, 1, tr, c), lambda j, i, h: (j, h[0], i, 0)), blk], out_specs=blk)
    return pl.pallas_call(body, name=name, grid_spec=grid_spec, out_shape=_sds((n, r, c), BF16),
                          compiler_params=_cp())(half, g4, other)


def chip_sum(recv, part, where, *, name):
    n, r, c = recv.shape
    tr = _rows_tile(r, c)

    def body(s_ref, *refs):
        own_ref, o_ref = refs[n], refs[n + 1]
        acc = None
        for j in range(n):
            term = jnp.where(s_ref[0] == j, own_ref[0], refs[j][0]).astype(F32)
            acc = term if acc is None else acc + term
        o_ref[0] = acc

    def slot(j):
        return pl.BlockSpec((1, tr, c), lambda i, s: (jnp.where(s[0] == j, (j + 1) % n, j), i, 0))

    grid_spec = pltpu.PrefetchScalarGridSpec(
        num_scalar_prefetch=1, grid=(r // tr,),
        in_specs=[slot(j) for j in range(n)] + [pl.BlockSpec((1, tr, c), lambda i, s: (s[0], i, 0))],
        out_specs=pl.BlockSpec((1, tr, c), lambda i, s: (s[1], i, 0)))
    return pl.pallas_call(body, name=name, grid_spec=grid_spec, out_shape=_sds((2, r, c), F32),
                          compiler_params=_cp())(where, *([recv] * n), part)


def _coords():
    return lax.axis_index("x"), lax.axis_index("y"), lax.axis_index("c")


def _other_chips(x, y):
    return [(1 - x, y), (x, 1 - y), (1 - x, 1 - y)]


def gather_chips(arrs, *, name):
    n = len(arrs)
    anyspec = pl.BlockSpec(memory_space=pl.ANY)

    def body(*refs):
        ins, outs = refs[:n], refs[n:2 * n]
        send_sems, recv_sems, local_sems = refs[2 * n:]
        x, y, c = _coords()
        me = 2 * x + y
        chips = _other_chips(x, y)
        copies = []
        for k in range(n):
            loc = pltpu.make_async_copy(ins[k], outs[k].at[me], local_sems.at[k])
            loc.start()
            copies.append(loc)
        sends = []
        for k in range(n):
            for j, (px, py) in enumerate(chips):
                cp = pltpu.make_async_remote_copy(
                    src_ref=ins[k], dst_ref=outs[k].at[me], send_sem=send_sems.at[k, j], recv_sem=recv_sems.at[k, j],
                    device_id=(px, py, c), device_id_type=MESH)
                cp.start()
                sends.append(cp)
        for k in range(n):
            for j, (px, py) in enumerate(chips):
                pltpu.make_async_remote_copy(
                    src_ref=ins[k], dst_ref=outs[k].at[2 * px + py], send_sem=send_sems.at[k, j],
                    recv_sem=recv_sems.at[k, j], device_id=(px, py, c), device_id_type=MESH).wait_recv()
        for cp in sends:
            cp.wait_send()
        for cp in copies:
            cp.wait()

    return pl.pallas_call(
        body, name=name, in_specs=[anyspec] * n, out_specs=[anyspec] * n,
        out_shape=[_sds((4,) + a.shape, a.dtype) for a in arrs],
        scratch_shapes=[pltpu.SemaphoreType.DMA((n, 3)), pltpu.SemaphoreType.DMA((n, 3)), pltpu.SemaphoreType.DMA((n,))],
        compiler_params=_cp(has_side_effects=True),
    )(*arrs)


def allreduce_small(vec, after, *, name):
    r, c = vec.shape

    def body(v_ref, after_ref, o_ref, buf, send_sems, recv_sems):
        x, y, cc = _coords()
        me = 4 * x + 2 * y + cc
        buf[me] = v_ref[...]
        sends = []
        flips = [(fx, fy, fc) for fx in (0, 1) for fy in (0, 1) for fc in (0, 1)][1:]
        for j, (fx, fy, fc) in enumerate(flips):
            peer = ((1 - x) if fx else x, (1 - y) if fy else y, (1 - cc) if fc else cc)
            cp = pltpu.make_async_remote_copy(
                src_ref=v_ref, dst_ref=buf.at[me], send_sem=send_sems.at[j], recv_sem=recv_sems.at[j],
                device_id=peer, device_id_type=MESH)
            cp.start()
            sends.append(cp)
        for j, (fx, fy, fc) in enumerate(flips):
            px, py, pc = ((1 - x) if fx else x, (1 - y) if fy else y, (1 - cc) if fc else cc)
            pltpu.make_async_remote_copy(
                src_ref=v_ref, dst_ref=buf.at[4 * px + 2 * py + pc], send_sem=send_sems.at[j],
                recv_sem=recv_sems.at[j], device_id=(px, py, pc), device_id_type=MESH).wait_recv()
        for cp in sends:
            cp.wait_send()
        acc = buf[0]
        for k in range(1, 8):
            acc = acc + buf[k]
        o_ref[...] = acc

    vm = pl.BlockSpec(memory_space=pltpu.VMEM)
    return pl.pallas_call(
        body, name=name, in_specs=[vm, pl.BlockSpec(memory_space=pl.ANY)], out_specs=vm, out_shape=_sds((r, c), F32),
        scratch_shapes=[pltpu.VMEM((8, r, c), F32), pltpu.SemaphoreType.DMA((7,)), pltpu.SemaphoreType.DMA((7,))],
        compiler_params=_cp(has_side_effects=True),
    )(vec, after)


def pair_share(lands, owns, *, name):
    n = len(lands)
    anyspec = pl.BlockSpec(memory_space=pl.ANY)

    def body(*refs):
        ins, own_refs, outs = refs[:n], refs[n:2 * n], refs[2 * n:3 * n]
        send_sems, recv_sems = refs[3 * n:]
        x, y, c = _coords()
        me = 2 * x + y
        sib = (x, y, 1 - c)
        sends = []
        for k in range(n):
            for j, (px, py) in enumerate(_other_chips(x, y)):
                cp = pltpu.make_async_remote_copy(
                    src_ref=ins[k].at[2 * px + py, c], dst_ref=outs[k].at[2 * px + py, c], send_sem=send_sems.at[k, j],
                    recv_sem=recv_sems.at[k, j], device_id=sib, device_id_type=MESH)
                cp.start()
                sends.append(cp)
            cp = pltpu.make_async_remote_copy(
                src_ref=own_refs[k], dst_ref=outs[k].at[me], send_sem=send_sems.at[k, 3], recv_sem=recv_sems.at[k, 3],
                device_id=sib, device_id_type=MESH)
            cp.start()
            sends.append(cp)
        for k in range(n):
            for j, (px, py) in enumerate(_other_chips(x, y)):
                pltpu.make_async_remote_copy(
                    src_ref=ins[k].at[2 * px + py, c], dst_ref=outs[k].at[2 * px + py, 1 - c],
                    send_sem=send_sems.at[k, j], recv_sem=recv_sems.at[k, j], device_id=sib,
                    device_id_type=MESH).wait_recv()
            pltpu.make_async_remote_copy(
                src_ref=own_refs[k], dst_ref=outs[k].at[me], send_sem=send_sems.at[k, 3], recv_sem=recv_sems.at[k, 3],
                device_id=sib, device_id_type=MESH).wait_recv()
        for cp in sends:
            cp.wait_send()

    return pl.pallas_call(
        body, name=name, in_specs=[anyspec] * (2 * n), out_specs=[anyspec] * n,
        out_shape=[_sds(a.shape, a.dtype) for a in lands], input_output_aliases={k: k for k in range(n)},
        scratch_shapes=[pltpu.SemaphoreType.DMA((n, 4)), pltpu.SemaphoreType.DMA((n, 4))],
        compiler_params=_cp(has_side_effects=True),
    )(*lands, *owns)


def pair_fill(arrs, *, name):
    n = len(arrs)
    anyspec = pl.BlockSpec(memory_space=pl.ANY)

    def body(*refs):
        ins, outs = refs[:n], refs[n:2 * n]
        send_sems, recv_sems = refs[2 * n:]
        x, y, c = _coords()
        sends = []
        for k in range(n):
            cp = pltpu.make_async_remote_copy(
                src_ref=ins[k].at[c], dst_ref=outs[k].at[c], send_sem=send_sems.at[k], recv_sem=recv_sems.at[k],
                device_id=(x, y, 1 - c), device_id_type=MESH)
            cp.start()
            sends.append(cp)
        for k in range(n):
            pltpu.make_async_remote_copy(
                src_ref=ins[k].at[c], dst_ref=outs[k].at[1 - c], send_sem=send_sems.at[k], recv_sem=recv_sems.at[k],
                device_id=(x, y, 1 - c), device_id_type=MESH).wait_recv()
        for cp in sends:
            cp.wait_send()

    return pl.pallas_call(
        body, name=name, in_specs=[anyspec] * n, out_specs=[anyspec] * n,
        out_shape=[_sds(a.shape, a.dtype) for a in arrs], input_output_aliases={k: k for k in range(n)},
        scratch_shapes=[pltpu.SemaphoreType.DMA((n,)), pltpu.SemaphoreType.DMA((n,))],
        compiler_params=_cp(has_side_effects=True),
    )(*arrs)


_HBM = pl.BlockSpec(memory_space=pltpu.HBM)
_SEM = pl.BlockSpec(memory_space=pltpu.SEMAPHORE)


_COPIES_PER_ARRAY = {"gather": 3, "scatter": 3, "share": 4, "exchange": 4}


def _ici_copies(kind, srcs, lands, send_sems, recv_sems):
    x, y, c = _coords()
    me = 2 * x + y
    per = _COPIES_PER_ARRAY[kind]
    sends, recvs = [], []
    for k in range(len(srcs)):
        triples = []
        for j, (px, py) in enumerate(_other_chips(x, y)):
            peer = 2 * px + py
            if kind == "gather":
                triples.append((srcs[k].at[c], lands[k].at[me, c], lands[k].at[peer, c], (px, py, c)))
            elif kind == "scatter":
                triples.append((srcs[k].at[peer], lands[k].at[me], lands[k].at[peer], (px, py, c)))
            elif kind == "share":
                triples.append((lands[k].at[peer, c], lands[k].at[peer, c], lands[k].at[peer, 1 - c], (x, y, 1 - c)))
        if kind == "share":
            triples.append((srcs[k], lands[k].at[me], lands[k].at[me], (x, y, 1 - c)))
        if kind == "exchange":
            triples = [(srcs[k].at[j, 1 - c], lands[k].at[j], lands[k].at[j], (x, y, 1 - c)) for j in range(4)]
        for j, (src, there, here, dev) in enumerate(triples):
            sem = per * k + j
            mk = functools.partial(pltpu.make_async_remote_copy, src_ref=src, send_sem=send_sems.at[sem],
                                   recv_sem=recv_sems.at[sem], device_id=dev, device_id_type=MESH)
            sends.append(mk(dst_ref=there))
            recvs.append(mk(dst_ref=here))
    return sends, recvs


def ici_start(kind, srcs, lands, after, *, name):
    n = len(srcs)

    def body(*refs):
        src_refs, land_refs = refs[:n], refs[n:2 * n]
        send_sems, recv_sems = refs[2 * n + 1], refs[2 * n + 2]
        token = refs[-1]
        sends, _ = _ici_copies(kind, src_refs, land_refs, send_sems, recv_sems)
        for cp in sends:
            cp.start()
        token[...] = jnp.zeros_like(token)

    both = list(srcs) + list(lands)
    out = pl.pallas_call(
        body, name=name,
        in_specs=[_HBM] * (2 * n) + [pl.BlockSpec(memory_space=pl.ANY)],
        out_shape=(pltpu.SemaphoreType.DMA((_COPIES_PER_ARRAY[kind] * n,)),
                   pltpu.SemaphoreType.DMA((_COPIES_PER_ARRAY[kind] * n,)),
                   *[pltpu.HBM(a.shape, a.dtype) for a in both], _sds((8, LANE), F32)),
        out_specs=(_SEM, _SEM, *([_HBM] * (2 * n)), pl.BlockSpec(memory_space=pltpu.VMEM)),
        input_output_aliases={i: 2 + i for i in range(2 * n)},
        compiler_params=_cp(has_side_effects=pltpu.SideEffectType.DATAFLOW_SIDE_EFFECTING),
    )(*[pltpu.with_memory_space_constraint(a, pltpu.HBM) for a in both], after)
    return out[0], out[1], list(out[2:2 + n]), list(out[2 + n:2 + 2 * n]), out[-1]


def ici_wait(kind, started, after, *, name):
    send_sems, recv_sems, srcs, lands, _ = started
    n = len(srcs)

    def body(*refs):
        src_refs, land_refs = refs[:n], refs[n:2 * n]
        sends, recvs = _ici_copies(kind, src_refs, land_refs, refs[2 * n], refs[2 * n + 1])
        for cp in sends:
            cp.wait_send()
        for cp in recvs:
            cp.wait_recv()

    both = list(srcs) + list(lands)
    out = pl.pallas_call(
        body, name=name,
        in_specs=[_HBM] * (2 * n) + [_SEM, _SEM, pl.BlockSpec(memory_space=pl.ANY)],
        out_shape=tuple(pltpu.HBM(a.shape, a.dtype) for a in both), out_specs=tuple([_HBM] * (2 * n)),
        input_output_aliases={i: i for i in range(2 * n)},
        compiler_params=_cp(has_side_effects=pltpu.SideEffectType.DATAFLOW_SIDE_EFFECTING),
    )(*both, send_sems, recv_sems, after)
    return list(out[:n]), list(out[n:])


BIG = ["w_in", "w_uq", "w_ukv", "w_branch_ssm", "w_branch_mla", "w_out", "w_mlp_up", "w_mlp_down"]
COL_SHARDED = {"w_in", "w_uq", "w_ukv", "w_mlp_up"}
SMALL_REPL = ["norm_mix_w", "conv_b", "dt_bias", "a_log", "d_skip", "ssm_norm_w", "q_norm_w", "kv_norm_w", "norm_mlp_w"]


def _unshard_layer(name, g):
    _, r, c = g.shape
    if name in COL_SHARDED:
        return jnp.transpose(g, (1, 0, 2)).reshape(r, 4 * c)
    return g.reshape(4 * r, c)


def _to_shards(name, full):
    r, c = full.shape
    if name in COL_SHARDED:
        return jnp.transpose(full.reshape(r, 4, c // 4), (1, 0, 2))
    return full.reshape(4, r // 4, c)


REST = [k for k in BIG if k != "w_in"]


def prep_layer(cfg, w):
    out = {}
    if "w_in" in w:
        sp = np.cumsum(cfg.in_splits)[:-1].tolist()
        z, xbc, dt, cq, ckv, kr, gs, gm = jnp.split(w["w_in"], sp, axis=1)
        zpad = lambda n: jnp.zeros((cfg.d, n), z.dtype)
        out.update(w_z=z, w_xbc=xbc, w_g=jnp.concatenate([gs, gm], axis=1),
                   w_s=jnp.concatenate([cq, ckv, kr, zpad(LANE - cfg.rope), dt, zpad(LANE - cfg.heads)], axis=1))
    if "w_uq" in w:
        out.update(
            w_uq=jnp.pad(w["w_uq"].reshape(cfg.ql, cfg.mh, cfg.nope + cfg.rope),
                         ((0, 0), (0, 0), (0, 2 * LANE - cfg.nope - cfg.rope))).reshape(cfg.ql, cfg.qw),
            w_ukv=w["w_ukv"], w_bs=w["w_branch_ssm"], w_bm=w["w_branch_mla"], w_out=w["w_out"],
            w_up=w["w_mlp_up"], w_down=w["w_mlp_down"])
    return {k: v.astype(BF16) for k, v in out.items()}


def unprep_grads(cfg, g):
    out = {}
    if "w_s" in g:
        ql, kvl = cfg.ql, cfg.kvl
        ds_ = g["w_s"]
        cq, ckv = ds_[:, :ql], ds_[:, ql:ql + kvl]
        kr = ds_[:, ql + kvl:ql + kvl + cfg.rope]
        dt = ds_[:, ql + kvl + LANE:ql + kvl + LANE + cfg.heads]
        out["w_in"] = jnp.concatenate([g["w_z"], g["w_xbc"], dt, cq, ckv, kr, g["w_g"]], axis=1)
    if "w_uq" in g:
        out.update(
            w_uq=g["w_uq"].reshape(cfg.ql, cfg.mh, 2 * LANE)[:, :, :cfg.nope + cfg.rope].reshape(cfg.ql, -1),
            w_ukv=g["w_ukv"], w_branch_ssm=g["w_bs"], w_branch_mla=g["w_bm"],
            w_out=g["w_out"], w_mlp_up=g["w_up"], w_mlp_down=g["w_down"])
    return out


def _hook(hooks, name, arg):
    if hooks and name in hooks:
        return hooks[name](arg)[0, 0]
    return 0.0


def layer_fwd(cfg, h, pw, sm, tabs, li, hooks=None):
    n = lambda s: f"l{li}_{s}"
    u = rmsnorm_fwd(h, sm["norm_mix_w"], name=n("norm_mix"))
    z, xbc, g, small = matmul_multi(u, [pw["w_z"], pw["w_xbc"], pw["w_g"], pw["w_s"]], (BF16, F32, BF16, F32),
                                    name=n("in_proj"))
    xc, dsilu = conv_fwd(cfg, xbc, sm["conv_w"], sm["conv_b"], name=n("conv"))
    dt_bias = sm["dt_bias_p"] + _hook(hooks, "after_conv", xc)
    y, sin = ssd_fwd(cfg, xc, small, dt_bias, sm["avec"], sm["dexp"], name=n("ssd"))
    y_ssm = tail_fwd(cfg, y, z, sm["ssm_norm_w"], name=n("tail"))
    if hooks and "weights" in hooks:
        pw = dict(pw, **hooks["weights"](y_ssm))
    cqn = rmsnorm_fwd(small, sm["q_norm_w"], cw=cfg.ql, ci=0, name=n("q_norm"))
    ckvn = rmsnorm_fwd(small, sm["kv_norm_w"], cw=cfg.kvl, ci=cfg.ql // cfg.kvl, name=n("kv_norm"))
    qf = matmul(cqn, pw["w_uq"], out_dtype=BF16, name=n("uq"))
    kv = matmul(ckvn, pw["w_ukv"], out_dtype=BF16, name=n("ukv"))
    qr, kpe = rope_fwd(cfg, qf, small, tabs, name=n("rope"))
    o, lse = attn_fwd(cfg, qr, kv, kpe, name=n("attn"))
    ya = matmul(y_ssm, pw["w_bs"], out_dtype=BF16, name=n("branch_ssm"))
    yb = matmul(o, pw["w_bm"], out_dtype=BF16, name=n("branch_mla"))
    mixed = gate_fwd(cfg, ya, yb, g, name=n("gate"))
    h1 = matmul(mixed, pw["w_out"], add=h, name=n("out"))
    v = rmsnorm_fwd(h1, sm["norm_mlp_w"] + _hook(hooks, "after_attn", o), name=n("norm_mlp"))
    a, act = matmul(v, pw["w_up"], name=n("up"), epilogue=_ep_relu2, out_dtypes=(BF16, BF16))
    h2 = matmul(act, pw["w_down"], add=h1, name=n("down"))
    saved = dict(h=h, u=u, z=z, xbc=xbc, g=g, small=small, xc=xc, dsilu=dsilu, y=y, sin=sin, y_ssm=y_ssm, cqn=cqn, ckvn=ckvn,
                 qr=qr, kv=kv, kpe=kpe, o=o, lse=lse, ya=ya, yb=yb, mixed=mixed, h1=h1, v=v, a=a, act=act)
    return h2, saved, pw


def layer_bwd(cfg, dh2, pw, sm, tabs, s, li, hooks=None):
    n = lambda t: f"l{li}_b_{t}"
    gw, gs = {}, {}
    wgrad = functools.partial(matmul, ta=True, out_dtype=BF16)
    dh2, dh2b = dh2
    gw["w_down"] = wgrad(s["act"], dh2b, name=n("dw_down"))
    da = matmul(dh2b, pw["w_down"], tb=True, name=n("dact"), epilogue=_ep_relu2_grad, extras=(s["a"],),
                out_dtypes=(BF16,))
    gw["w_up"] = wgrad(s["v"], da, name=n("dw_up"))
    dv = matmul(da, pw["w_up"], tb=True, out_dtype=BF16, name=n("dv"))
    dh1, gs["norm_mlp_w"], dh1b = rmsnorm_bwd(dv, s["h1"], sm["norm_mlp_w"], res=dh2, with_bf16=True,
                                              name=n("norm_mlp"))
    gw["w_out"] = wgrad(s["mixed"], dh1b, name=n("dw_out"))
    dmix = matmul(dh1b, pw["w_out"], tb=True, out_dtype=BF16, name=n("dmix"))
    dya, dyb, dg = gate_bwd(cfg, dmix, s["ya"], s["yb"], s["g"], name=n("gate"))
    gw["w_bs"] = wgrad(s["y_ssm"], dya, name=n("dw_bs"))
    gw["w_bm"] = wgrad(s["o"], dyb, name=n("dw_bm"))
    dy_ssm = matmul(dya, pw["w_bs"], tb=True, out_dtype=BF16, name=n("dy_ssm"))
    do = matmul(dyb, pw["w_bm"], tb=True, out_dtype=BF16, name=n("do"))
    dq, dkv, dkpe = attn_bwd(cfg, s["qr"], s["kv"], s["kpe"], s["o"], s["lse"], do, name=n("attn"))
    dqf, dkr = rope_bwd(cfg, dq, dkpe, tabs, name=n("rope"))
    gw["w_uq"] = wgrad(s["cqn"], dqf, name=n("dw_uq"))
    gw["w_ukv"] = wgrad(s["ckvn"], dkv, name=n("dw_ukv"))
    dcqn = matmul(dqf, pw["w_uq"], tb=True, name=n("dcqn"))
    dckvn = matmul(dkv, pw["w_ukv"], tb=True, name=n("dckvn"))
    q_norm_w = sm["q_norm_w"] + _hook(hooks, "after_attn", dqf)
    dcq, gs["q_norm_w"] = rmsnorm_bwd(dcqn, s["small"], q_norm_w, cw=cfg.ql, ci=0, out_dtype=BF16, name=n("q_norm"))
    dckv, gs["kv_norm_w"] = rmsnorm_bwd(dckvn, s["small"], sm["kv_norm_w"], cw=cfg.kvl, ci=cfg.ql // cfg.kvl,
                                        out_dtype=BF16, name=n("kv_norm"))
    ssm_norm_w = sm["ssm_norm_w"] + _hook(hooks, "early", dict(gw))
    dy, dz, gs["ssm_norm_w"] = tail_bwd(cfg, dy_ssm, s["y"], s["z"], ssm_norm_w, name=n("tail"))
    dxc, ddt, ddexp, dav, dbias = ssd_bwd(cfg, s["xc"], s["small"], sm["dt_bias_p"], sm["avec"], sm["dexp"],
                                          s["sin"], dy, name=n("ssd"))
    conv_w = sm["conv_w"] + _hook(hooks, "after_ssd", dxc)
    dxbc, gs["conv_w"], gs["conv_b"] = conv_bwd(cfg, s["xbc"], conv_w, s["dsilu"], dxc, name=n("conv"))
    gs["d_skip"] = ddexp.reshape(cfg.heads, cfg.hd).sum(axis=1)
    gs["a_log"] = (dav[0] * sm["avec"][0])[:cfg.heads]
    gs["dt_bias"] = dbias[0, :cfg.heads]
    dsmall = jnp.concatenate([dcq, dckv, dkr.astype(BF16), ddt.astype(BF16)], axis=1)
    gw["w_z"] = wgrad(s["u"], dz, name=n("dw_z"))
    gw["w_xbc"] = wgrad(s["u"], dxbc, name=n("dw_xbc"))
    gw["w_g"] = wgrad(s["u"], dg, name=n("dw_g"))
    gw["w_s"] = wgrad(s["u"], dsmall, name=n("dw_s"))
    du = matmul_nt_sum([dz, dxbc, dg, dsmall], [pw["w_z"], pw["w_xbc"], pw["w_g"], pw["w_s"]], out_dtype=BF16,
                       name=n("du"))
    if li > 0:
        dh, gs["norm_mix_w"], dhb = rmsnorm_bwd(du, s["h"], sm["norm_mix_w"], res=dh1, with_bf16=True,
                                                name=n("norm_mix"))
    else:
        dh, gs["norm_mix_w"] = rmsnorm_bwd(du, s["h"], sm["norm_mix_w"], res=dh1, name=n("norm_mix"))
        dhb = None
    return (dh, dhb), gw, gs


def small_params(cfg, p, li):
    pad_l = lambda v: jnp.pad(v, (0, LANE - v.shape[0])).reshape(1, LANE)
    return dict(
        norm_mix_w=p["norm_mix_w"][li], conv_w=p["conv_w"][li], conv_b=p["conv_b"][li],
        dt_bias_p=pad_l(p["dt_bias"][li]), avec=pad_l(-jnp.exp(p["a_log"][li])),
        dexp=jnp.repeat(p["d_skip"][li], cfg.hd).reshape(1, cfg.inner),
        ssm_norm_w=p["ssm_norm_w"][li], q_norm_w=p["q_norm_w"][li], kv_norm_w=p["kv_norm_w"][li],
        norm_mlp_w=p["norm_mlp_w"][li])


def local_step(cfg, x, target, p, depth=2):
    bsz, d = cfg.bsz, cfg.d
    lead = jnp.zeros((bsz, cfg.pad, d), F32)
    meta = jnp.broadcast_to(p["meta_tokens"][None], (bsz, cfg.n_meta, d))
    h = jnp.concatenate([lead, meta, x], axis=1).reshape(cfg.t, d)
    tabs = rope_tables(cfg)
    saved, sms = [], []
    for li in range(depth):
        sm = small_params(cfg, p, li)
        h, s, _ = layer_fwd(cfg, h, p["pw"][li], sm, tabs, li)
        saved.append(s)
        sms.append(sm)
    loss, dh, dfw = loss_head(cfg, h, target.reshape(bsz * cfg.seq, d), p["final_norm_w"], name="loss_head")
    gws, gss = [None] * depth, [None] * depth
    for li in reversed(range(depth)):
        dh, gws[li], gss[li] = layer_bwd(cfg, dh, p["pw"][li], sms[li], tabs, saved[li], li)
    dh = dh[0].reshape(bsz, cfg.lp, d)
    grad_x = dh[:, cfg.chunk:, :]
    gmeta = jnp.sum(dh[:, cfg.pad:cfg.chunk, :], axis=0)
    return loss, grad_x, gmeta, gws, gss, dfw


def _pack_small(parts):
    flat = jnp.concatenate([a.reshape(-1) for a in parts])
    n = flat.shape[0]
    npad = -n % (8 * LANE)
    return jnp.pad(flat, (0, npad)).reshape(-1, LANE), n


def _unpack_small(vec, shapes):
    flat = vec.reshape(-1)
    out, off = [], 0
    for sh in shapes:
        sz = int(np.prod(sh))
        out.append(flat[off:off + sz].reshape(sh))
        off += sz
    return out


def _as2d(a):
    return a.reshape(-1, a.shape[-1])


def kernel(x, meta_tokens, norm_mix_w, w_in, conv_w, conv_b, dt_bias, a_log, d_skip, ssm_norm_w, q_norm_w, kv_norm_w, w_uq, w_ukv, w_branch_ssm, w_branch_mla, w_out, norm_mlp_w, w_mlp_up, w_mlp_down, final_norm_w, loss_target, m_meta_tokens, m_norm_mix_w, m_w_in, m_conv_w, m_conv_b, m_dt_bias, m_a_log, m_d_skip, m_ssm_norm_w, m_q_norm_w, m_kv_norm_w, m_w_uq, m_w_ukv, m_w_branch_ssm, m_w_branch_mla, m_w_out, m_norm_mlp_w, m_w_mlp_up, m_w_mlp_down, m_final_norm_w, v_meta_tokens, v_norm_mix_w, v_w_in, v_conv_w, v_conv_b, v_dt_bias, v_a_log, v_d_skip, v_ssm_norm_w, v_q_norm_w, v_kv_norm_w, v_w_uq, v_w_ukv, v_w_branch_ssm, v_w_branch_mla, v_w_out, v_norm_mlp_w, v_w_mlp_up, v_w_mlp_down, v_final_norm_w):
    cfg = CFG
    names = ["meta_tokens", "norm_mix_w", "w_in", "conv_w", "conv_b", "dt_bias", "a_log", "d_skip", "ssm_norm_w",
             "q_norm_w", "kv_norm_w", "w_uq", "w_ukv", "w_branch_ssm", "w_branch_mla", "w_out", "norm_mlp_w",
             "w_mlp_up", "w_mlp_down", "final_norm_w"]
    wts = dict(zip(names, [meta_tokens, norm_mix_w, w_in, conv_w, conv_b, dt_bias, a_log, d_skip, ssm_norm_w,
                           q_norm_w, kv_norm_w, w_uq, w_ukv, w_branch_ssm, w_branch_mla, w_out, norm_mlp_w,
                           w_mlp_up, w_mlp_down, final_norm_w]))
    ms = dict(zip(names, [m_meta_tokens, m_norm_mix_w, m_w_in, m_conv_w, m_conv_b, m_dt_bias, m_a_log, m_d_skip,
                          m_ssm_norm_w, m_q_norm_w, m_kv_norm_w, m_w_uq, m_w_ukv, m_w_branch_ssm, m_w_branch_mla,
                          m_w_out, m_norm_mlp_w, m_w_mlp_up, m_w_mlp_down, m_final_norm_w]))
    vs = dict(zip(names, [v_meta_tokens, v_norm_mix_w, v_w_in, v_conv_w, v_conv_b, v_dt_bias, v_a_log, v_d_skip,
                          v_ssm_norm_w, v_q_norm_w, v_kv_norm_w, v_w_uq, v_w_ukv, v_w_branch_ssm, v_w_branch_mla,
                          v_w_out, v_norm_mlp_w, v_w_mlp_up, v_w_mlp_down, v_final_norm_w]))
    cx, cy, cc = _coords()
    chip = 2 * cx + cy

    half1 = jnp.reshape(cc, (1,)).astype(jnp.int32)
    where2 = jnp.stack([chip, cc]).astype(jnp.int32)
    wb = {k: wts[k].astype(BF16) for k in BIG}
    zero_tok = jnp.zeros((8, LANE), F32)

    def halves(a):
        return a.reshape((2, a.shape[0] // 2) + a.shape[1:])

    def gather_start(li, keys, tag, after):
        srcs = [halves(wb[k][li]) for k in keys]
        lands = [lax.empty((4,) + s.shape, BF16) for s in srcs]
        return ici_start("gather", srcs, lands, after, name=f"gather{li}{tag}_start")

    def gather_finish(li, keys, tag, started, after):
        srcs, lands = ici_wait("gather", started, after, name=f"gather{li}{tag}_wait")
        lands = pair_share(lands, srcs, name=f"gather{li}{tag}_share")
        full = {k: _unshard_layer(k, land.reshape((4, 2 * land.shape[2], land.shape[3])))
                for k, land in zip(keys, lands)}
        return prep_layer(cfg, full)

    def gather_mid(li, keys, tag, started, after):
        srcs, lands = ici_wait("gather", started, after, name=f"gather{li}{tag}_wait")
        return ici_start("share", srcs, lands, zero_tok, name=f"gather{li}{tag}_share_start")

    def gather_end(li, keys, tag, shared, after):
        _, lands = ici_wait("share", shared, after, name=f"gather{li}{tag}_share_wait")
        full = {k: _unshard_layer(k, land.reshape((4, 2 * land.shape[2], land.shape[3])))
                for k, land in zip(keys, lands)}
        return prep_layer(cfg, full)

    def exchange_start(li, keys, tag, gw, after):
        ug = unprep_grads(cfg, gw)
        g4 = []
        for k in keys:
            s = _to_shards(k, ug[k])
            g4.append(s.reshape(4, 2, s.shape[1] // 2, s.shape[2]))
        lands = [lax.empty((4,) + a.shape[2:], a.dtype) for a in g4]
        return ici_start("exchange", g4, lands, after, name=f"grad{li}{tag}_exchange_start")

    def reduce_start(li, keys, tag, exchanged, after):
        g4, theirs = ici_wait("exchange", exchanged, after, name=f"grad{li}{tag}_exchange_wait")
        parts = [pair_add(a, b, half1, name=f"grad{li}_pair_add_{k}") for k, a, b in zip(keys, g4, theirs)]
        lands = [lax.empty(q.shape, q.dtype) for q in parts]
        return ici_start("scatter", parts, lands, zero_tok, name=f"grad{li}{tag}_scatter_start")

    def reduce_finish(li, keys, tag, started, after):
        parts, lands = ici_wait("scatter", started, after, name=f"grad{li}{tag}_scatter_wait")
        sums = [chip_sum(rc, pt, where2, name=f"grad{li}_chip_sum_{k}") for k, rc, pt in zip(keys, lands, parts)]
        sums = pair_fill(sums, name=f"grad{li}{tag}_pair_fill")
        return {k: s.reshape(2 * s.shape[1], s.shape[2]) for k, s in zip(keys, sums)}

    gathered = gather_chips([meta_tokens, conv_w], name="gather_small")
    p = dict(wts)
    p["meta_tokens"] = jnp.transpose(gathered[0], (1, 0, 2)).reshape(cfg.n_meta, cfg.d)
    p["conv_w"] = jnp.transpose(gathered[1], (1, 2, 0, 3)).reshape(2, cfg.convk, cfg.conv_dim)

    st0a = gather_start(0, ["w_in"], "a", gathered[0])
    st0b = gather_start(0, REST, "b", st0a[4])
    st1 = gather_start(1, BIG, "", st0b[4])
    pw0 = gather_finish(0, ["w_in"], "a", st0a, st1[4])

    bsz, d = cfg.bsz, cfg.d
    lead = jnp.zeros((bsz, cfg.pad, d), F32)
    meta = jnp.broadcast_to(p["meta_tokens"][None], (bsz, cfg.n_meta, d))
    h0 = jnp.concatenate([lead, meta, x], axis=1).reshape(cfg.t, d)
    tabs = rope_tables(cfg)
    sm0 = small_params(cfg, p, 0)
    st = {}

    def step(key, fn):
        def run(arg):
            st[key] = fn(arg)
            return st[key][4]
        return run

    h1, sv0, pw0 = layer_fwd(cfg, h0, pw0, sm0, tabs, 0, hooks={
        "after_conv": step("share0b", lambda after: gather_mid(0, REST, "b", st0b, after)),
        "weights": lambda after: gather_end(0, REST, "b", st["share0b"], after),
        "after_attn": step("share1", lambda after: gather_mid(1, BIG, "", st1, after))})
    pw1 = gather_end(1, BIG, "", st["share1"], h1)
    sm1 = small_params(cfg, p, 1)
    h2, sv1, _ = layer_fwd(cfg, h1, pw1, sm1, tabs, 1)
    loss, dh, dfw = loss_head(cfg, h2, loss_target.reshape(bsz * cfg.seq, d), final_norm_w, name="loss_head")

    dh, gw1, gs1 = layer_bwd(cfg, dh, pw1, sm1, tabs, sv1, 1)
    ex1 = exchange_start(1, BIG, "", gw1, zero_tok)
    sm0b = dict(sm0)
    sm0b["norm_mlp_w"] = sm0["norm_mlp_w"] + ex1[4][0, 0]
    dh, gw0, gs0 = layer_bwd(cfg, dh, pw0, sm0b, tabs, sv0, 0, hooks={
        "after_attn": step("red1", lambda after: reduce_start(1, BIG, "", ex1, after)),
        "early": step("ex0e", lambda gw: exchange_start(0, REST, "e", gw, zero_tok)),
        "after_ssd": step("red0e", lambda after: reduce_start(0, REST, "e", st["ex0e"], after))})
    dh3 = dh[0].reshape(bsz, cfg.lp, d)
    grad_x = dh3[:, cfg.chunk:, :]
    gmeta = jnp.sum(dh3[:, cfg.pad:cfg.chunk, :], axis=0)
    big1 = reduce_finish(1, BIG, "", st["red1"], dh[0])
    ex0l = exchange_start(0, ["w_in"], "l", gw0, big1[BIG[-1]])

    small_names = SMALL_REPL + ["conv_w"]
    parts = [jnp.stack([gs0[k], gs1[k]]) for k in small_names] + [dfw, gmeta, loss.reshape(1)]
    shapes = [a.shape for a in parts]
    vec, _ = _pack_small(parts)
    red_vec = allreduce_small(vec, ex0l[4], name="allreduce_small")
    red = _unpack_small(red_vec, shapes)
    sg = dict(zip(small_names + ["final_norm_w", "meta_tokens"], red))
    loss = red[-1].reshape(())
    sg["conv_w"] = lax.dynamic_slice_in_dim(sg["conv_w"], chip * (cfg.conv_dim // 4), cfg.conv_dim // 4, axis=2)
    sg["meta_tokens"] = lax.dynamic_slice_in_dim(sg["meta_tokens"], chip * (cfg.d // 4), cfg.d // 4, axis=1)

    red0 = reduce_start(0, ["w_in"], "l", ex0l, red_vec)
    grads, deltas, new_m, new_v = {}, {}, {}, {}
    dep = red0[4]
    for k in names:
        if k in BIG:
            continue
        w2, g2, m2, v2 = _as2d(wts[k]), _as2d(sg[k]), _as2d(ms[k]), _as2d(vs[k])
        dl, mn, vn = adamw_small(w2, g2, m2, v2, dep, name=f"adamw_{k}")
        grads[k] = sg[k].reshape(wts[k].shape)
        deltas[k], new_m[k], new_v[k] = (t.reshape(wts[k].shape) for t in (dl, mn, vn))

    def view(k, a):
        return jnp.swapaxes(a, 1, 2) if k == "w_in" else a

    def gview(k, g):
        return g.T if k == "w_in" else g

    wv, mv, vv = ({k: view(k, t[k]) for k in BIG} for t in (wts, ms, vs))
    outs = {}
    for k in BIG:
        outs[k] = adamw_layer(wv[k], mv[k], vv[k], gview(k, big1[k]), 1, None, dep, name=f"adamw1_{k}")
        dep = outs[k][1]
    big0 = reduce_finish(0, REST, "e", st["red0e"], dep)
    for k in REST:
        outs[k] = adamw_layer(wv[k], mv[k], vv[k], big0[k], 0, outs[k], dep, name=f"adamw0_{k}")
        dep = outs[k][1]
    big0.update(reduce_finish(0, ["w_in"], "l", red0, dep))
    outs["w_in"] = adamw_layer(wv["w_in"], mv["w_in"], vv["w_in"], gview("w_in", big0["w_in"]), 0, outs["w_in"], dep,
                               name="adamw0_w_in")
    for k in BIG:
        grads[k], deltas[k], new_m[k], new_v[k] = (view(k, t) for t in outs[k])
    return (loss, grad_x, *[grads[k] for k in names], *[deltas[k] for k in names],
            *[new_m[k] for k in names], *[new_v[k] for k in names])


def adamw_small(w, g, m, v, dep, *, name):
    def body(w_ref, g_ref, m_ref, v_ref, dep_ref, d_ref, mo_ref, vo_ref):
        d_ref[...], mo_ref[...], vo_ref[...] = _adam_update(w_ref[...], g_ref[...], m_ref[...], v_ref[...])

    vm = pl.BlockSpec(memory_space=pltpu.VMEM)
    return pl.pallas_call(body, name=name, in_specs=[vm] * 4 + [pl.BlockSpec(memory_space=pl.ANY)], out_specs=[vm] * 3,
                          out_shape=[_sds(w.shape, F32)] * 3, compiler_params=_cp())(w, g, m, v, dep)
```

```python
import functools
from typing import NamedTuple

import numpy as np
import jax
import jax.numpy as jnp
from jax import lax
from jax.experimental import pallas as pl
from jax.experimental.pallas import tpu as pltpu

F32 = jnp.float32
BF16 = jnp.bfloat16
EPS = 1e-6
ROPE_THETA = 10000.0
LANE = 128
VMEM_LIMIT = 56 * 1024 * 1024
MASK_VALUE = -1e30
ADAM_LR, ADAM_B1, ADAM_B2, ADAM_EPS, ADAM_WD, ADAM_STEP = 0.001, 0.9, 0.999, 1e-08, 0.01, 10
MESH = pl.DeviceIdType.MESH


class Cfg(NamedTuple):
    d: int = 1024
    seq: int = 2048
    bsz: int = 2
    n_meta: int = 16
    inner: int = 2048
    hd: int = 64
    groups: int = 4
    state: int = 128
    convk: int = 4
    chunk: int = 128
    mh: int = 8
    ql: int = 512
    kvl: int = 256
    nope: int = 128
    rope: int = 64
    vd: int = 128
    ff: int = 4096

    @property
    def heads(self): return self.inner // self.hd
    @property
    def gw(self): return self.inner // self.groups
    @property
    def conv_dim(self): return self.inner + 2 * self.groups * self.state
    @property
    def pad(self): return self.chunk - self.n_meta
    @property
    def lp(self): return self.chunk + self.seq
    @property
    def t(self): return self.bsz * self.lp
    @property
    def nchunks(self): return self.lp // self.chunk
    @property
    def sw(self): return self.ql + self.kvl + 2 * LANE
    @property
    def kt(self): return (self.ql + self.kvl) // LANE
    @property
    def dtt(self): return self.kt + 1
    @property
    def qw(self): return self.mh * 2 * LANE
    @property
    def in_splits(self):
        return [self.inner, self.conv_dim, self.heads, self.ql, self.kvl, self.rope, self.d, self.d]


CFG = Cfg()


def _pick(dim, pref, mult):
    best = None
    for t in range(mult, min(dim, pref) + 1, mult):
        if dim % t == 0:
            best = t
    return best if best is not None else dim


def _cp(**kw):
    return pltpu.CompilerParams(vmem_limit_bytes=VMEM_LIMIT, **kw)


def _sds(shape, dtype):
    return jax.ShapeDtypeStruct(tuple(shape), dtype)


def _silu(x):
    return x * jax.nn.sigmoid(x)


def _dsilu(x):
    s = jax.nn.sigmoid(x)
    return s * (1.0 + x * (1.0 - s))


def _ep_plain(r):
    return (r,)


def _ep_add(r, res):
    return (r + res.astype(F32),)


def _ep_relu2(r):
    rp = jnp.maximum(r, 0.0)
    return r, rp * rp


def _ep_relu2_grad(r, a):
    return (r * (2.0 * jnp.maximum(a.astype(F32), 0.0)),)


MM_VMEM_BUDGET = 44 * 1024 * 1024


def _mm_tiles(m, n, k, a_bytes, b_bytes, io_bytes, ta):
    m_mult, m_cap = (LANE, 1024) if ta else (16, 1088)
    tms = [t for t in range(m_cap, 0, -m_mult) if m % t == 0] or [m]
    tns = [t for t in (1024, 512, 256, 128) if n % t == 0] or [n]
    best = None
    for tm in tms:
        for tn in tns:
            need = 2 * (tm * k * a_bytes + k * tn * b_bytes + tm * tn * io_bytes)
            if need <= MM_VMEM_BUDGET and (best is None or tm * tn > best[0] * best[1]):
                best = (tm, tn)
    if best is None:
        return (_pick(m, 512, m_mult), _pick(n, 512, LANE), _pick(k, 1088 if ta else 1024, 16 if ta else LANE))
    return best[0], best[1], k


def _resident_rows(m, n, k, a_bytes, b_bytes, io_bytes):
    w = n * k * b_bytes
    if w > 18 * 1024 * 1024:
        return None
    for tm in range(544, 255, -16):
        if m % tm == 0 and w + 2 * tm * (k * a_bytes + n * io_bytes) + tm * n * 4 <= MM_VMEM_BUDGET - (4 << 20):
            return tm
    return None


def matmul(a, b, *, ta=False, tb=False, out_dtype=F32, add=None, name, tm=None, tn=None, tk=None,
           epilogue=None, extras=(), out_dtypes=None):
    if add is not None:
        epilogue, extras = _ep_add, (add,)
    if epilogue is None:
        epilogue = _ep_plain
    out_dtypes = tuple(out_dtypes) if out_dtypes is not None else (out_dtype,)
    n_ex, n_out = len(extras), len(out_dtypes)
    if ta:
        k_dim, m_dim = a.shape
    else:
        m_dim, k_dim = a.shape
    if tb:
        n_dim, k2 = b.shape
    else:
        k2, n_dim = b.shape
    assert k_dim == k2, (a.shape, b.shape, ta, tb)
    resident = False
    if tm is None and tn is None and tk is None:
        io_bytes = sum(jnp.dtype(e.dtype).itemsize for e in extras) + sum(jnp.dtype(d).itemsize for d in out_dtypes)
        a_bytes, b_bytes = jnp.dtype(a.dtype).itemsize, jnp.dtype(b.dtype).itemsize
        tm = None if ta else _resident_rows(m_dim, n_dim, k_dim, a_bytes, b_bytes, io_bytes)
        if tm is not None:
            resident, tn, tk = True, n_dim, k_dim
        else:
            tm, tn, tk = _mm_tiles(m_dim, n_dim, k_dim, a_bytes, b_bytes, io_bytes, ta)
    elif ta:
        tm = tm or _pick(m_dim, 1024, LANE)
        tk = tk or _pick(k_dim, 1088, 16)
        tn = tn or _pick(n_dim, 1024, LANE)
    else:
        tm = tm or _pick(m_dim, 1088, 16)
        tk = tk or _pick(k_dim, 1024 if a.dtype == F32 else 2048, LANE)
        tn = tn or _pick(n_dim, 1024, LANE)
    nm, nn, nk = m_dim // tm, n_dim // tn, k_dim // tk
    dn = (((0 if ta else 1,), (1 if tb else 0,)), ((), ()))

    def body(*refs):
        a_ref, b_ref = refs[:2]
        ex_refs = refs[2:2 + n_ex]
        o_refs = refs[2 + n_ex:2 + n_ex + n_out]
        scr = refs[2 + n_ex + n_out:]
        p = lax.dot_general(a_ref[...].astype(BF16), b_ref[...].astype(BF16), dn, preferred_element_type=F32)

        def finish(r):
            outs = epilogue(r, *[e[...] for e in ex_refs])
            for o_ref, val, dt in zip(o_refs, outs, out_dtypes):
                o_ref[...] = val.astype(dt)

        if nk == 1:
            finish(p)
        else:
            acc = scr[0]
            k = pl.program_id(2)

            @pl.when(k == 0)
            def _():
                acc[...] = p

            @pl.when(k > 0)
            def _():
                acc[...] += p

            @pl.when(k == nk - 1)
            def _():
                finish(acc[...])

    if resident:
        row = pl.BlockSpec((tm, k_dim), lambda i: (i, 0))
        o_spec = pl.BlockSpec((tm, n_dim), lambda i: (i, 0))
        outs = pl.pallas_call(
            body, name=name, grid=(nm,),
            in_specs=[row, pl.BlockSpec(b.shape, lambda i: (0, 0), pipeline_mode=pl.Buffered(1))] + [o_spec] * n_ex,
            out_specs=[o_spec] * n_out, out_shape=[_sds((m_dim, n_dim), dt) for dt in out_dtypes],
            compiler_params=_cp(dimension_semantics=("parallel",)),
        )(a, b, *extras)
        return outs[0] if n_out == 1 else tuple(outs)
    a_spec = pl.BlockSpec((tk, tm), lambda i, j, k: (k, i)) if ta else pl.BlockSpec((tm, tk), lambda i, j, k: (i, k))
    b_spec = pl.BlockSpec((tn, tk), lambda i, j, k: (j, k)) if tb else pl.BlockSpec((tk, tn), lambda i, j, k: (k, j))
    o_spec = pl.BlockSpec((tm, tn), lambda i, j, k: (i, j))
    outs = pl.pallas_call(
        body, name=name, grid=(nm, nn, nk), in_specs=[a_spec, b_spec] + [o_spec] * n_ex, out_specs=[o_spec] * n_out,
        out_shape=[_sds((m_dim, n_dim), dt) for dt in out_dtypes],
        scratch_shapes=[pltpu.VMEM((tm, tn), F32)] if nk > 1 else [],
        compiler_params=_cp(dimension_semantics=("parallel", "parallel", "arbitrary")),
    )(a, b, *extras)
    return outs[0] if n_out == 1 else tuple(outs)


def matmul_multi(a, bs_, out_dtypes, *, name):
    m, k = a.shape
    ns = [b.shape[1] for b in bs_]
    cnt = len(bs_)
    out_row_bytes = sum(n * jnp.dtype(dt).itemsize for n, dt in zip(ns, out_dtypes))
    w_bytes = sum(k * n * jnp.dtype(b.dtype).itemsize for n, b in zip(ns, bs_))
    tm = next(t for t in range(1088, 0, -16)
              if m % t == 0 and w_bytes + 2 * t * (k * jnp.dtype(a.dtype).itemsize + out_row_bytes)
              + t * max(ns) * 4 <= MM_VMEM_BUDGET - (8 << 20))

    def body(*refs):
        a_ref = refs[0]
        b_refs, o_refs = refs[1:1 + cnt], refs[1 + cnt:]
        av = a_ref[...].astype(BF16)
        for b_ref, o_ref, dt in zip(b_refs, o_refs, out_dtypes):
            o_ref[...] = _nn(av, b_ref[...].astype(BF16)).astype(dt)

    return pl.pallas_call(
        body, name=name, grid=(m // tm,),
        in_specs=[pl.BlockSpec((tm, k), lambda i: (i, 0))]
        + [pl.BlockSpec((k, n), lambda i: (0, 0), pipeline_mode=pl.Buffered(1)) for n in ns],
        out_specs=[pl.BlockSpec((tm, n), lambda i: (i, 0)) for n in ns],
        out_shape=[_sds((m, n), dt) for n, dt in zip(ns, out_dtypes)], compiler_params=_cp(),
    )(a, *bs_)


def matmul_nt_sum(as_, bs_, *, out_dtype=F32, name, tiles=None):
    m, n = as_[0].shape[0], bs_[0].shape[0]
    ks = [a.shape[1] for a in as_]
    assert [b.shape[1] for b in bs_] == ks
    ksum, cnt = sum(ks), len(ks)
    best = tiles
    for tn in [t for t in (1024, 512, 256, 128) if n % t == 0]:
        for tm in [t for t in range(1088, 0, -16) if m % t == 0]:
            need = 2 * (tm * ksum * 2 + tn * ksum * 2 + tm * tn * jnp.dtype(out_dtype).itemsize)
            if best is None and need <= MM_VMEM_BUDGET and tm >= 256:
                best = (tm, tn)
    tm, tn = best

    def body(*refs):
        a_refs, b_refs, o_ref = refs[:cnt], refs[cnt:2 * cnt], refs[2 * cnt]
        acc = None
        for a_ref, b_ref in zip(a_refs, b_refs):
            p = _nt(a_ref[...].astype(BF16), b_ref[...].astype(BF16))
            acc = p if acc is None else acc + p
        o_ref[...] = acc.astype(out_dtype)

    return pl.pallas_call(
        body, name=name, grid=(n // tn, m // tm),
        in_specs=[pl.BlockSpec((tm, k), lambda j, i: (i, 0)) for k in ks]
        + [pl.BlockSpec((tn, k), lambda j, i: (j, 0)) for k in ks],
        out_specs=pl.BlockSpec((tm, tn), lambda j, i: (i, j)), out_shape=_sds((m, n), out_dtype),
        compiler_params=_cp(dimension_semantics=("parallel", "parallel")),
    )(*as_, *bs_)


def rmsnorm_fwd(x, w, *, cw=None, ci=0, name):
    t = x.shape[0]
    cw = cw or x.shape[1]
    tr = _pick(t, 544, 16)

    def body(x_ref, w_ref, o_ref):
        xv = x_ref[...].astype(F32)
        r = lax.rsqrt(jnp.mean(xv * xv, axis=-1, keepdims=True) + EPS)
        o_ref[...] = (xv * r * w_ref[...]).astype(BF16)

    return pl.pallas_call(
        body, name=name, grid=(t // tr,),
        in_specs=[pl.BlockSpec((tr, cw), lambda i: (i, ci)), pl.BlockSpec((1, cw), lambda i: (0, 0))],
        out_specs=pl.BlockSpec((tr, cw), lambda i: (i, 0)),
        out_shape=_sds((t, cw), BF16), compiler_params=_cp(),
    )(x, w.reshape(1, cw))


def rmsnorm_bwd(dy, x, w, *, cw=None, ci=0, res=None, out_dtype=F32, with_bf16=False, name):
    t = x.shape[0]
    cw = cw or x.shape[1]
    tr = _pick(t, 544, 16)
    has_res = res is not None

    def body(*refs):
        dxb_ref = None
        if with_bf16:
            refs, dxb_ref = refs[:-1], refs[-1]
        if has_res:
            dy_ref, x_ref, w_ref, res_ref, dx_ref, dw_ref = refs
        else:
            dy_ref, x_ref, w_ref, dx_ref, dw_ref = refs
        xv = x_ref[...].astype(F32)
        dyv = dy_ref[...].astype(F32)
        r = lax.rsqrt(jnp.mean(xv * xv, axis=-1, keepdims=True) + EPS)
        xh = xv * r
        g = dyv * w_ref[...]
        dx = r * (g - xh * jnp.mean(g * xh, axis=-1, keepdims=True))
        if has_res:
            dx = dx + res_ref[...]
        dx_ref[...] = dx.astype(out_dtype)
        if with_bf16:
            dxb_ref[...] = dx.astype(BF16)

        @pl.when(pl.program_id(0) == 0)
        def _():
            dw_ref[...] = jnp.zeros_like(dw_ref)

        dw_ref[...] += jnp.sum(dyv * xh, axis=0, keepdims=True)

    row = pl.BlockSpec((tr, cw), lambda i: (i, 0))
    in_specs = [row, pl.BlockSpec((tr, cw), lambda i: (i, ci)), pl.BlockSpec((1, cw), lambda i: (0, 0))]
    args = [dy, x, w.reshape(1, cw)]
    if has_res:
        in_specs.append(row)
        args.append(res)
    outs = pl.pallas_call(
        body, name=name, grid=(t // tr,), in_specs=in_specs,
        out_specs=[row, pl.BlockSpec((1, cw), lambda i: (0, 0))] + ([row] if with_bf16 else []),
        out_shape=[_sds((t, cw), out_dtype), _sds((1, cw), F32)] + ([_sds((t, cw), BF16)] if with_bf16 else []),
        compiler_params=_cp(),
    )(*args)
    if with_bf16:
        return outs[0], outs[1][0], outs[2]
    return outs[0], outs[1][0]


def _shift_down(x, s):
    return x if s == 0 else pltpu.roll(x, s, 0)


def _shift_up(x, s):
    return x if s == 0 else pltpu.roll(x, x.shape[0] - s, 0)


def _conv_pre(x, w_ref, b_ref, kk):
    pre = b_ref[...] + jnp.zeros_like(x)
    for k in range(kk):
        pre = pre + w_ref[k:k + 1, :] * _shift_down(x, kk - 1 - k)
    return pre


def conv_fwd(cfg, xbc, w, b, *, name):
    lp, cd, kk = cfg.lp, cfg.conv_dim, cfg.convk
    assert cfg.pad >= kk - 1
    cb = _pick(cd, 512, LANE)

    def body(x_ref, w_ref, b_ref, o_ref, ds_ref):
        pre = _conv_pre(x_ref[...], w_ref, b_ref, kk)
        sg = jax.nn.sigmoid(pre)
        o_ref[...] = pre * sg
        ds_ref[...] = (sg * (1.0 + pre * (1.0 - sg))).astype(BF16)

    blk = pl.BlockSpec((lp, cb), lambda j, bb: (bb, j))
    return pl.pallas_call(
        body, name=name, grid=(cd // cb, cfg.bsz),
        in_specs=[blk, pl.BlockSpec((kk, cb), lambda j, bb: (0, j)), pl.BlockSpec((1, cb), lambda j, bb: (0, j))],
        out_specs=[blk, blk], out_shape=[_sds((cfg.t, cd), F32), _sds((cfg.t, cd), BF16)], compiler_params=_cp(),
    )(xbc, w, b.reshape(1, cd))


def conv_bwd(cfg, xbc, w, dsilu, dxc, *, name):
    lp, cd, kk = cfg.lp, cfg.conv_dim, cfg.convk
    cb = _pick(cd, 512, LANE)

    def body(x_ref, w_ref, s_ref, d_ref, dx_ref, dw_ref, db_ref):
        x = x_ref[...]
        dpre = d_ref[...] * s_ref[...].astype(F32)
        dx = jnp.zeros_like(x)
        dws = []
        for k in range(kk):
            s = kk - 1 - k
            dx = dx + w_ref[k:k + 1, :] * _shift_up(dpre, s)
            dws.append(jnp.sum(dpre * _shift_down(x, s), axis=0, keepdims=True))
        dx_ref[...] = dx.astype(BF16)

        @pl.when(pl.program_id(1) == 0)
        def _():
            dw_ref[...] = jnp.zeros_like(dw_ref)
            db_ref[...] = jnp.zeros_like(db_ref)

        for k in range(kk):
            dw_ref[k:k + 1, :] += dws[k]
        db_ref[...] += jnp.sum(dpre, axis=0, keepdims=True)

    blk = pl.BlockSpec((lp, cb), lambda j, bb: (bb, j))
    wsp = pl.BlockSpec((kk, cb), lambda j, bb: (0, j))
    bsp = pl.BlockSpec((1, cb), lambda j, bb: (0, j))
    dx, dw, db = pl.pallas_call(
        body, name=name, grid=(cd // cb, cfg.bsz),
        in_specs=[blk, wsp, blk, blk], out_specs=[blk, wsp, bsp],
        out_shape=[_sds((cfg.t, cd), BF16), _sds((kk, cd), F32), _sds((1, cd), F32)], compiler_params=_cp(),
    )(xbc, w, dsilu, dxc)
    return dx, dw, db[0]


def _softplus(x):
    return jnp.maximum(x, 0.0) + jnp.log(1.0 + jnp.exp(-jnp.abs(x)))


def _ssd_consts(cfg):
    q = cfg.chunk
    i0 = np.arange(q)[:, None]
    i1 = np.arange(q)[None, :]
    ltri = (i1 <= i0).astype(np.float32)
    rexp = np.zeros((LANE, cfg.inner), np.float32)
    for h in range(cfg.heads):
        rexp[h, h * cfg.hd:(h + 1) * cfg.hd] = 1.0
    return jnp.asarray(ltri), jnp.asarray(rexp)


def _sel_dot(x, m, *, passes=2, left=False, trans=False):
    mb = m.astype(BF16)
    acc, rem = None, x
    for _ in range(passes):
        piece = rem.astype(BF16)
        if not left:
            part = _nn(piece, mb)
        elif trans:
            part = _tn(mb, piece)
        else:
            part = _nn(mb, piece)
        acc = part if acc is None else acc + part
        rem = rem - piece.astype(F32)
    return acc


def _ssd_chunk_common(cfg, raw, bias, avec, c_idx, ltri, rexp):
    q = cfg.chunk
    rows = lax.broadcasted_iota(jnp.int32, (q, LANE), 0)
    live = jnp.logical_or(c_idx > 0, rows >= cfg.pad)
    pre = raw + bias
    dt = jnp.where(live, _softplus(pre), 0.0)
    adt = dt * avec
    cs = _sel_dot(adt, ltri, passes=3, left=True)
    cs_t = cs.T
    cs_last = cs[q - 1:q, :]
    e_in = jnp.exp(cs)
    w0 = jnp.exp(cs_last - cs)
    decay = jnp.exp(cs_last)
    return dict(live=live, pre=pre, dt=dt, adt=adt, cs=cs, cs_t=cs_t, e_in=e_in, w0=w0, decay=decay,
                DT=_sel_dot(dt, rexp), E=_sel_dot(e_in, rexp), W0=_sel_dot(w0, rexp),
                DEC=_sel_dot(jnp.broadcast_to(decay, (8, LANE)), rexp)[0:1, :])


def _tri_masks(q):
    r = lax.broadcasted_iota(jnp.int32, (q, q), 0)
    c = lax.broadcasted_iota(jnp.int32, (q, q), 1)
    return c <= r, r <= c


def _head_l(cq, h, tri, tri_t):
    col = cq["cs"][:, h:h + 1]
    row = cq["cs_t"][h:h + 1, :]
    lmat = jnp.where(tri, jnp.exp(jnp.minimum(col - row, 0.0)), 0.0)
    lmat_t = jnp.where(tri_t, jnp.exp(jnp.minimum(row - col, 0.0)), 0.0)
    return lmat, lmat_t


def _nt(a, b):
    return lax.dot_general(a, b, (((1,), (1,)), ((), ())), preferred_element_type=F32)


def _tn(a, b):
    return lax.dot_general(a, b, (((0,), (0,)), ((), ())), preferred_element_type=F32)


def _nn(a, b):
    return jnp.dot(a, b, preferred_element_type=F32)


def ssd_fwd(cfg, xc, small, dt_bias, avec, dexp, *, name):
    q, inner, st, gw, g_n = cfg.chunk, cfg.inner, cfg.state, cfg.gw, cfg.groups
    nc = cfg.nchunks
    ltri, rexp = _ssd_consts(cfg)
    hpt = LANE // cfg.hd
    tiles_per_group = gw // LANE

    bsz, lp = cfg.bsz, cfg.lp
    bcw = g_n * st

    def body(x_ref, b_ref, c_ref, dt_ref, bias_ref, a_ref, d_ref, ltri_ref, rexp_ref, y_ref, sin_ref, s_scr):
        c_idx = pl.program_id(0)

        @pl.when(c_idx == 0)
        def _():
            s_scr[...] = jnp.zeros_like(s_scr)

        ltri_v = ltri_ref[...]
        tri, tri_t = _tri_masks(q)
        lane = lax.broadcasted_iota(jnp.int32, (q, LANE), 1)
        for bi in range(bsz):
            cq = _ssd_chunk_common(cfg, dt_ref[bi], bias_ref[...], a_ref[...], c_idx, ltri_v, rexp_ref[...])
            xs = x_ref[bi]
            xdt = (xs * cq["DT"]).astype(BF16)
            xw = (xs * cq["DT"] * cq["W0"]).astype(BF16)
            s_in = s_scr[bi]
            sin_ref[bi, 0] = s_in
            for g in range(g_n):
                bg = b_ref[bi, :, g * st:(g + 1) * st].astype(BF16)
                cg = c_ref[bi, :, g * st:(g + 1) * st].astype(BF16)
                gmat = _nt(cg, bg)
                gs = slice(g * gw, (g + 1) * gw)
                y0 = _nn(cg, s_in[:, gs].astype(BF16))
                for tt in range(tiles_per_group):
                    tile = g * tiles_per_group + tt
                    ts = slice(tile * LANE, (tile + 1) * LANE)
                    xt = xdt[:, ts]
                    ms, xh = [], []
                    for hh in range(hpt):
                        lmat, _ = _head_l(cq, tile * hpt + hh, tri, tri_t)
                        ms.append((gmat * lmat).astype(BF16))
                        inhead = jnp.logical_and(lane >= hh * cfg.hd, lane < (hh + 1) * cfg.hd)
                        xh.append(jnp.where(inhead, xt, jnp.zeros_like(xt)))
                    yd = _nn(jnp.concatenate(ms, axis=1), jnp.concatenate(xh, axis=0))
                    y_ref[bi, :, ts] = (yd + y0[:, tt * LANE:(tt + 1) * LANE] * cq["E"][:, ts]
                                        + xs[:, ts] * d_ref[:, ts]).astype(BF16)
                s_scr[bi, :, gs] = s_in[:, gs] * cq["DEC"][:, gs] + _tn(bg, xw[:, gs])

    def rowblk(width, col):
        return pl.BlockSpec((bsz, q, width), lambda c: (0, c, col))

    def const(shape):
        return pl.BlockSpec(shape, lambda c: (0, 0))

    xc3 = xc.reshape(bsz, lp, cfg.conv_dim)
    y, sin = pl.pallas_call(
        body, name=name, grid=(nc,),
        in_specs=[rowblk(inner, 0), rowblk(bcw, inner // bcw), rowblk(bcw, inner // bcw + 1),
                  rowblk(LANE, cfg.dtt), const((1, LANE)), const((1, LANE)), const((1, inner)),
                  const((q, q)), const((LANE, inner))],
        out_specs=[rowblk(inner, 0), pl.BlockSpec((bsz, 1, st, inner), lambda c: (0, c, 0, 0))],
        out_shape=[_sds((bsz, lp, inner), BF16), _sds((bsz, nc, st, inner), F32)],
        scratch_shapes=[pltpu.VMEM((bsz, st, inner), F32)], compiler_params=_cp(),
    )(xc3, xc3, xc3, small.reshape(bsz, lp, cfg.sw), dt_bias, avec, dexp, ltri, rexp)
    return y.reshape(cfg.t, inner), sin.reshape(bsz * nc, st, inner)


def ssd_bwd(cfg, xc, small, dt_bias, avec, dexp, sin, dy, *, name):
    q, inner, st, gw, g_n = cfg.chunk, cfg.inner, cfg.state, cfg.gw, cfg.groups
    nc = cfg.nchunks
    ltri, rexp = _ssd_consts(cfg)
    rexp_t = rexp.T
    hpt = LANE // cfg.hd
    tiles_per_group = gw // LANE
    bcw = g_n * st

    def body(x_ref, b_ref, c_ref, dt_ref, bias_ref, a_ref, d_ref, ltri_ref, rexp_ref, rexpt_ref, sin_ref, dy_ref,
             dx_ref, ddt_ref, dd_ref, da_ref, dbias_ref, ds_scr):
        step = pl.program_id(1)
        c_idx = nc - 1 - step

        @pl.when(step == 0)
        def _():
            ds_scr[...] = jnp.zeros_like(ds_scr)

        @pl.when(jnp.logical_and(step == 0, pl.program_id(0) == 0))
        def _():
            dd_ref[...] = jnp.zeros_like(dd_ref)
            da_ref[...] = jnp.zeros_like(da_ref)
            dbias_ref[...] = jnp.zeros_like(dbias_ref)

        ltri_v = ltri_ref[...]
        tri, tri_t = _tri_masks(q)
        red = _sel_dot
        rexpt = rexpt_ref[...]
        cq = _ssd_chunk_common(cfg, dt_ref[...], bias_ref[...], a_ref[...], c_idx, ltri_v, rexp_ref[...])
        xs = x_ref[...]
        dyv = dy_ref[...].astype(F32)
        s_in = sin_ref[0]
        d_s = ds_scr[...]
        xdt_f = xs * cq["DT"]
        xdt = xdt_f.astype(BF16)
        xw_f = xdt_f * cq["W0"]
        xw = xw_f.astype(BF16)
        lane = lax.broadcasted_iota(jnp.int32, (q, LANE), 1)
        sub = lax.broadcasted_iota(jnp.int32, (LANE, q), 0)

        dd_ref[...] += jnp.sum(dyv * xs, axis=0, keepdims=True)
        dy0 = dyv * cq["E"]
        dcs = jnp.zeros((q, LANE), F32)
        dcs_t = jnp.zeros((LANE, q), F32)
        for g in range(g_n):
            bg_f = b_ref[:, g * st:(g + 1) * st]
            cg_f = c_ref[:, g * st:(g + 1) * st]
            bg = bg_f.astype(BF16)
            cg = cg_f.astype(BF16)
            gs = slice(g * gw, (g + 1) * gw)
            gmat = _nt(cg, bg)
            gmat_t = _nt(bg, cg)
            sing = s_in[:, gs].astype(BF16)
            dsg = d_s[:, gs].astype(BF16)
            y0 = _nn(cg, sing)
            dxw = _nn(bg, dsg)
            d_bg = _nt(xw[:, gs], dsg)
            d_cg = _nt(dy0[:, gs].astype(BF16), sing)
            ds_in_g = _tn(cg, dy0[:, gs].astype(BF16))
            dg = jnp.zeros((q, q), F32)
            dxdt_g = []
            for tt in range(tiles_per_group):
                tile = g * tiles_per_group + tt
                ts = slice(tile * LANE, (tile + 1) * LANE)
                xt = xdt[:, ts]
                dyt = dyv[:, ts]
                dyhs, lmats, mts = [], [], []
                for hh in range(hpt):
                    lmat, lmat_t = _head_l(cq, tile * hpt + hh, tri, tri_t)
                    inhead = jnp.logical_and(lane >= hh * cfg.hd, lane < (hh + 1) * cfg.hd)
                    dyhs.append(jnp.where(inhead, dyt, 0.0).astype(BF16))
                    lmats.append(lmat)
                    mts.append((gmat_t * lmat_t).astype(BF16))
                dy_stack = jnp.concatenate(dyhs, axis=0)
                dm_all = _nt(dy_stack, xt)
                for hh in range(hpt):
                    h = tile * hpt + hh
                    dm = dm_all[hh * q:(hh + 1) * q, :]
                    dg = dg + dm * lmats[hh]
                    qm = dm * gmat * lmats[hh]
                    rs = jnp.sum(qm, axis=1, keepdims=True)
                    csum = jnp.sum(qm, axis=0, keepdims=True)
                    dcs = dcs + jnp.where(lane == h, rs, 0.0)
                    dcs_t = dcs_t + jnp.where(sub == h, csum, 0.0)
                dxdt_g.append(_nn(jnp.concatenate(mts, axis=1), dy_stack))
            dxdt_diag = jnp.concatenate(dxdt_g, axis=1) if len(dxdt_g) > 1 else dxdt_g[0]
            dgb = dg.astype(BF16)
            d_cg = d_cg + _nn(dgb, bg)
            d_bg = d_bg + _tn(dgb, cg)
            dx_ref[:, inner + g * st:inner + (g + 1) * st] = d_bg
            dx_ref[:, inner + bcw + g * st:inner + bcw + (g + 1) * st] = d_cg
            dxdt = dxdt_diag + dxw * cq["W0"][:, gs]
            dx_ref[:, gs] = dyv[:, gs] * d_ref[:, gs] + dxdt * cq["DT"][:, gs]
            rt = rexpt[gs, :]
            dcs = dcs + red(dyv[:, gs] * y0 * cq["E"][:, gs], rt)
            r_w = red(dxw * xw_f[:, gs], rt)
            dcs = dcs - r_w
            dcs_last_g = jnp.sum(r_w, axis=0, keepdims=True)
            ddec = red(jnp.broadcast_to(jnp.sum(d_s[:, gs] * s_in[:, gs], axis=0, keepdims=True), (8, gw)), rt)[0:1, :]
            dcs_last_g = dcs_last_g + ddec * cq["decay"]
            dcs = dcs + jnp.where(lax.broadcasted_iota(jnp.int32, (q, LANE), 0) == q - 1, dcs_last_g, 0.0)
            ddt_part = red(dxdt * xs[:, gs], rt)
            if g == 0:
                ddt = ddt_part
            else:
                ddt = ddt + ddt_part
            ds_scr[:, gs] = d_s[:, gs] * cq["DEC"][:, gs] + ds_in_g
        dcs = dcs - dcs_t.T
        dadt = _sel_dot(dcs, ltri_v, left=True, trans=True)
        ddt = ddt + dadt * a_ref[...]
        da_ref[...] += jnp.sum(dadt * cq["dt"], axis=0, keepdims=True)
        draw = jnp.where(cq["live"], ddt * jax.nn.sigmoid(cq["pre"]), 0.0)
        ddt_ref[...] = draw
        dbias_ref[...] += jnp.sum(draw, axis=0, keepdims=True)

    def rowblk(width, col):
        return pl.BlockSpec((q, width), lambda b, s: (b * nc + nc - 1 - s, col))

    def const(shape):
        return pl.BlockSpec(shape, lambda b, s: (0, 0))

    bcol = inner // bcw
    outs = pl.pallas_call(
        body, name=name, grid=(cfg.bsz, nc),
        in_specs=[rowblk(inner, 0), rowblk(bcw, bcol), rowblk(bcw, bcol + 1), rowblk(LANE, cfg.dtt),
                  const((1, LANE)), const((1, LANE)), const((1, inner)), const((q, q)), const((LANE, inner)),
                  const((inner, LANE)),
                  pl.BlockSpec((1, st, inner), lambda b, s: (b * nc + nc - 1 - s, 0, 0)), rowblk(inner, 0)],
        out_specs=[rowblk(cfg.conv_dim, 0), rowblk(LANE, 0),
                   const((1, inner)), const((1, LANE)), const((1, LANE))],
        out_shape=[_sds((cfg.t, cfg.conv_dim), F32),
                   _sds((cfg.t, LANE), F32), _sds((1, inner), F32), _sds((1, LANE), F32), _sds((1, LANE), F32)],
        scratch_shapes=[pltpu.VMEM((st, inner), F32)], compiler_params=_cp(),
    )(xc, xc, xc, small, dt_bias, avec, dexp, ltri, rexp, rexp_t, sin, dy)
    return outs


def tail_fwd(cfg, y, z, w, *, name):
    t, inner, gw = cfg.t, cfg.inner, cfg.gw
    tr = _pick(t, 272, 16)

    def body(y_ref, z_ref, w_ref, o_ref):
        for g in range(cfg.groups):
            gs = slice(g * gw, (g + 1) * gw)
            yg = y_ref[:, gs].astype(F32) * _silu(z_ref[:, gs].astype(F32))
            r = lax.rsqrt(jnp.mean(yg * yg, axis=-1, keepdims=True) + EPS)
            o_ref[:, gs] = (yg * r * w_ref[:, gs]).astype(BF16)

    row = pl.BlockSpec((tr, inner), lambda i: (i, 0))
    return pl.pallas_call(
        body, name=name, grid=(t // tr,), in_specs=[row, row, pl.BlockSpec((1, inner), lambda i: (0, 0))],
        out_specs=row, out_shape=_sds((t, inner), BF16), compiler_params=_cp(),
    )(y, z, w.reshape(1, inner))


def tail_bwd(cfg, do, y, z, w, *, name):
    t, inner, gw = cfg.t, cfg.inner, cfg.gw
    tr = _pick(t, 272, 16)

    def body(do_ref, y_ref, z_ref, w_ref, dy_ref, dz_ref, dw_ref):
        @pl.when(pl.program_id(0) == 0)
        def _():
            dw_ref[...] = jnp.zeros_like(dw_ref)

        for g in range(cfg.groups):
            gs = slice(g * gw, (g + 1) * gw)
            yv = y_ref[:, gs].astype(F32)
            zv = z_ref[:, gs].astype(F32)
            dov = do_ref[:, gs].astype(F32)
            sz = _silu(zv)
            yg = yv * sz
            r = lax.rsqrt(jnp.mean(yg * yg, axis=-1, keepdims=True) + EPS)
            xh = yg * r
            gg = dov * w_ref[:, gs]
            dyg = r * (gg - xh * jnp.mean(gg * xh, axis=-1, keepdims=True))
            dw_ref[:, gs] += jnp.sum(dov * xh, axis=0, keepdims=True)
            dy_ref[:, gs] = (dyg * sz).astype(BF16)
            dz_ref[:, gs] = (dyg * yv * _dsilu(zv)).astype(BF16)

    row = pl.BlockSpec((tr, inner), lambda i: (i, 0))
    vec = pl.BlockSpec((1, inner), lambda i: (0, 0))
    dy, dz, dw = pl.pallas_call(
        body, name=name, grid=(t // tr,), in_specs=[row, row, row, vec], out_specs=[row, row, vec],
        out_shape=[_sds((t, inner), BF16), _sds((t, inner), BF16), _sds((1, inner), F32)], compiler_params=_cp(),
    )(do, y, z, w.reshape(1, inner))
    return dy, dz, dw[0]


def rope_tables(cfg):
    half = cfg.rope // 2
    pos = np.maximum(np.arange(cfg.lp) - cfg.pad, 0).astype(np.float32)
    inv = ROPE_THETA ** (-jnp.arange(0, cfg.rope, 2, dtype=F32) / cfg.rope)
    ang = jnp.asarray(pos)[:, None] * inv[None, :]
    cos, sin = jnp.cos(ang), jnp.sin(ang)
    zero = jnp.zeros((cfg.lp, LANE - 2 * half), F32)
    zh = jnp.zeros((cfg.lp, half), F32)
    ctab = jnp.concatenate([cos, cos, zero], axis=1)
    s1 = jnp.concatenate([-sin, zh, zero], axis=1)
    s2 = jnp.concatenate([zh, sin, zero], axis=1)
    return ctab, s1, s2


def _rope(x, c, s1, s2, half):
    return x * c + pltpu.roll(x, LANE - half, 1) * s1 + pltpu.roll(x, half, 1) * s2


def _rope_t(dy, c, s1, s2, half):
    return dy * c + pltpu.roll(dy * s1, half, 1) + pltpu.roll(dy * s2, LANE - half, 1)


def _attn_scale(cfg):
    return (cfg.nope + cfg.rope) ** -0.5


def rope_fwd(cfg, qf, small, tabs, *, name):
    t, qw, lp = cfg.t, cfg.qw, cfg.lp
    tr = _pick(lp, 544, 16)
    nrb = lp // tr
    half = cfg.rope // 2
    scale = _attn_scale(cfg)

    def body(q_ref, k_ref, c_ref, s1_ref, s2_ref, qo_ref, ko_ref):
        c, s1, s2 = c_ref[...], s1_ref[...], s2_ref[...]
        for h in range(cfg.mh):
            a = h * 2 * LANE
            qo_ref[:, a:a + LANE] = (q_ref[:, a:a + LANE].astype(F32) * scale).astype(BF16)
            qo_ref[:, a + LANE:a + 2 * LANE] = (
                _rope(q_ref[:, a + LANE:a + 2 * LANE].astype(F32), c, s1, s2, half) * scale).astype(BF16)
        ko_ref[...] = _rope(k_ref[...], c, s1, s2, half).astype(BF16)

    tab = pl.BlockSpec((tr, LANE), lambda i: (i % nrb, 0))
    return pl.pallas_call(
        body, name=name, grid=(t // tr,),
        in_specs=[pl.BlockSpec((tr, qw), lambda i: (i, 0)), pl.BlockSpec((tr, LANE), lambda i: (i, cfg.kt)), tab, tab, tab],
        out_specs=[pl.BlockSpec((tr, qw), lambda i: (i, 0)), pl.BlockSpec((tr, LANE), lambda i: (i, 0))],
        out_shape=[_sds((t, qw), BF16), _sds((t, LANE), BF16)], compiler_params=_cp(),
    )(qf, small, *tabs)


def rope_bwd(cfg, dq, dkpe, tabs, *, name):
    t, qw, lp = cfg.t, cfg.qw, cfg.lp
    tr = _pick(lp, 544, 16)
    nrb = lp // tr
    half = cfg.rope // 2
    scale = _attn_scale(cfg)

    def body(dq_ref, dk_ref, c_ref, s1_ref, s2_ref, qo_ref, ko_ref):
        c, s1, s2 = c_ref[...], s1_ref[...], s2_ref[...]
        for h in range(cfg.mh):
            a = h * 2 * LANE
            qo_ref[:, a:a + LANE] = (dq_ref[:, a:a + LANE].astype(F32) * scale).astype(BF16)
            qo_ref[:, a + LANE:a + 2 * LANE] = _rope_t(
                dq_ref[:, a + LANE:a + 2 * LANE].astype(F32) * scale, c, s1, s2, half).astype(BF16)
        dk = dk_ref[0]
        for h in range(1, cfg.mh):
            dk = dk + dk_ref[h]
        ko_ref[...] = _rope_t(dk, c, s1, s2, half)

    tab = pl.BlockSpec((tr, LANE), lambda i: (i % nrb, 0))
    return pl.pallas_call(
        body, name=name, grid=(t // tr,),
        in_specs=[pl.BlockSpec((tr, qw), lambda i: (i, 0)), pl.BlockSpec((cfg.mh, tr, LANE), lambda i: (0, i, 0)),
                  tab, tab, tab],
        out_specs=[pl.BlockSpec((tr, qw), lambda i: (i, 0)), pl.BlockSpec((tr, LANE), lambda i: (i, 0))],
        out_shape=[_sds((t, qw), BF16), _sds((t, LANE), F32)], compiler_params=_cp(),
    )(dq, dkpe, *tabs)


def _q_blocks(cfg):
    bounds = [0, cfg.chunk] + list(range(cfg.chunk + 256, cfg.lp + 1, 256))
    assert bounds[-1] == cfg.lp, "SEQ must be a multiple of 256"
    return list(zip(bounds[:-1], bounds[1:]))


def _attn_mask(cfg, qs, qe):
    rows = qs + lax.broadcasted_iota(jnp.int32, (qe - qs, qe), 0)
    cols = lax.broadcasted_iota(jnp.int32, (qe - qs, qe), 1)
    return jnp.logical_and(cols <= rows, jnp.logical_or(cols >= cfg.pad, rows < cfg.pad))


def _max_q_block(cfg):
    return max(qe - qs for qs, qe in _q_blocks(cfg))


def _masked_scores(cfg, q, k2, qs, qe, s_scr):
    bq, n = qe - qs, qe
    s_scr[0:bq, 0:n] = _nt(q, k2)
    if qs == 0:
        s_scr[0:bq, 0:n] = jnp.where(_attn_mask(cfg, 0, qe), s_scr[0:bq, 0:n], MASK_VALUE)
    else:
        assert qs >= cfg.chunk and cfg.pad < LANE
        cols = lax.broadcasted_iota(jnp.int32, (bq, LANE), 1)
        s_scr[0:bq, 0:LANE] = jnp.where(cols >= cfg.pad, s_scr[0:bq, 0:LANE], MASK_VALUE)
        r = lax.broadcasted_iota(jnp.int32, (bq, bq), 0)
        c = lax.broadcasted_iota(jnp.int32, (bq, bq), 1)
        s_scr[0:bq, qs:qe] = jnp.where(c <= r, s_scr[0:bq, qs:qe], MASK_VALUE)
    return s_scr[0:bq, 0:n]


def attn_fwd(cfg, qr, kv, kpe, *, name):
    lp, t, mh = cfg.lp, cfg.t, cfg.mh
    blocks = _q_blocks(cfg)

    def body(q_ref, kv_ref, kp_ref, o_ref, l_ref, s_scr):
        for qs, qe in blocks:
            n = qe
            q = q_ref[qs:qe, :]
            k2 = jnp.concatenate([kv_ref[0:n, 0:LANE], kp_ref[0:n, :]], axis=1)
            s = _masked_scores(cfg, q, k2, qs, qe, s_scr)
            m = jnp.max(s, axis=-1, keepdims=True)
            p = jnp.exp(s - m)
            l = jnp.sum(p, axis=-1, keepdims=True)
            o_ref[qs:qe, :] = (_nn(p.astype(BF16), kv_ref[0:n, LANE:2 * LANE]) * (1.0 / l)).astype(BF16)
            l_ref[qs:qe, :] = jnp.broadcast_to(m + jnp.log(l), (qe - qs, LANE))

    hb = pl.BlockSpec((lp, 2 * LANE), lambda b, h: (b, h))
    ob = pl.BlockSpec((lp, LANE), lambda b, h: (b, h))
    return pl.pallas_call(
        body, name=name, grid=(cfg.bsz, mh),
        in_specs=[hb, hb, pl.BlockSpec((lp, LANE), lambda b, h: (b, 0))], out_specs=[ob, ob],
        out_shape=[_sds((t, mh * LANE), BF16), _sds((t, mh * LANE), F32)],
        scratch_shapes=[pltpu.VMEM((_max_q_block(cfg), lp), F32)], compiler_params=_cp(),
    )(qr, kv, kpe)


def attn_bwd(cfg, qr, kv, kpe, o, lse, do, *, name):
    lp, t, mh = cfg.lp, cfg.t, cfg.mh
    blocks = _q_blocks(cfg)

    def body(q_ref, kv_ref, kp_ref, o_ref, l_ref, do_ref, dq_ref, dkv_ref, dkp_ref, dk_acc, dv_acc, s_scr):
        dk_acc[...] = jnp.zeros_like(dk_acc)
        dv_acc[...] = jnp.zeros_like(dv_acc)
        for qs, qe in blocks:
            n = qe
            q = q_ref[qs:qe, :]
            k2 = jnp.concatenate([kv_ref[0:n, 0:LANE], kp_ref[0:n, :]], axis=1)
            dob = do_ref[qs:qe, :].astype(BF16)
            delta = jnp.sum(dob.astype(F32) * o_ref[qs:qe, :].astype(F32), axis=-1, keepdims=True)
            s = _masked_scores(cfg, q, k2, qs, qe, s_scr)
            p = jnp.exp(s - l_ref[qs:qe, 0:1])
            dp = _nt(dob, kv_ref[0:n, LANE:2 * LANE])
            ds = (p * (dp - delta)).astype(BF16)
            dq_ref[qs:qe, :] = _nn(ds, k2).astype(BF16)
            dv_acc[0:n, :] += _tn(p.astype(BF16), dob)
            dk_acc[0:n, :] += _tn(ds, q)
        dkv_ref[:, 0:LANE] = dk_acc[:, 0:LANE].astype(BF16)
        dkv_ref[:, LANE:2 * LANE] = dv_acc[...].astype(BF16)
        dkp_ref[0] = dk_acc[:, LANE:2 * LANE]

    hb = pl.BlockSpec((lp, 2 * LANE), lambda b, h: (b, h))
    ob = pl.BlockSpec((lp, LANE), lambda b, h: (b, h))
    return pl.pallas_call(
        body, name=name, grid=(cfg.bsz, mh),
        in_specs=[hb, hb, pl.BlockSpec((lp, LANE), lambda b, h: (b, 0)), ob, ob, ob],
        out_specs=[hb, hb, pl.BlockSpec((1, lp, LANE), lambda b, h: (h, b, 0))],
        out_shape=[_sds((t, cfg.qw), BF16), _sds((t, mh * 2 * LANE), BF16), _sds((mh, t, LANE), F32)],
        scratch_shapes=[pltpu.VMEM((lp, 2 * LANE), F32), pltpu.VMEM((lp, LANE), F32),
                        pltpu.VMEM((_max_q_block(cfg), lp), F32)], compiler_params=_cp(),
    )(qr, kv, kpe, o, lse, do)


def _live_rows(cfg, tr, shape):
    rows = pl.program_id(1) * tr + lax.broadcasted_iota(jnp.int32, shape, 0)
    return rows >= cfg.pad


def gate_fwd(cfg, ya, yb, g, *, name):
    d, lp = cfg.d, cfg.lp
    tr = _pick(lp, 544, 16)
    nrb = lp // tr

    def body(ya_ref, yb_ref, ga_ref, gb_ref, o_ref):
        f = lambda ref: ref[...].astype(F32)
        mix = jax.nn.sigmoid(f(ga_ref)) * f(ya_ref) + jax.nn.sigmoid(f(gb_ref)) * f(yb_ref)
        o_ref[...] = jnp.where(_live_rows(cfg, tr, mix.shape), mix, 0.0).astype(BF16)

    row = pl.BlockSpec((tr, d), lambda b, j: (b * nrb + j, 0))
    row1 = pl.BlockSpec((tr, d), lambda b, j: (b * nrb + j, 1))
    return pl.pallas_call(
        body, name=name, grid=(cfg.bsz, nrb), in_specs=[row, row, row, row1], out_specs=row,
        out_shape=_sds((cfg.t, d), BF16), compiler_params=_cp(),
    )(ya, yb, g, g)


def gate_bwd(cfg, dmix, ya, yb, g, *, name):
    d, lp = cfg.d, cfg.lp
    tr = _pick(lp, 544, 16)
    nrb = lp // tr

    def body(dm_ref, ya_ref, yb_ref, ga_ref, gb_ref, dya_ref, dyb_ref, dg_ref):
        dm = dm_ref[...].astype(F32)
        dm = jnp.where(_live_rows(cfg, tr, dm.shape), dm, 0.0)
        sa = jax.nn.sigmoid(ga_ref[...].astype(F32))
        sb = jax.nn.sigmoid(gb_ref[...].astype(F32))
        dya_ref[...] = (dm * sa).astype(BF16)
        dyb_ref[...] = (dm * sb).astype(BF16)
        dg_ref[:, 0:d] = (dm * ya_ref[...].astype(F32) * sa * (1.0 - sa)).astype(BF16)
        dg_ref[:, d:2 * d] = (dm * yb_ref[...].astype(F32) * sb * (1.0 - sb)).astype(BF16)

    row = pl.BlockSpec((tr, d), lambda b, j: (b * nrb + j, 0))
    row1 = pl.BlockSpec((tr, d), lambda b, j: (b * nrb + j, 1))
    row2 = pl.BlockSpec((tr, 2 * d), lambda b, j: (b * nrb + j, 0))
    return pl.pallas_call(
        body, name=name, grid=(cfg.bsz, nrb), in_specs=[row, row, row, row, row1], out_specs=[row, row, row2],
        out_shape=[_sds((cfg.t, d), BF16), _sds((cfg.t, d), BF16), _sds((cfg.t, 2 * d), BF16)], compiler_params=_cp(),
    )(dmix, ya, yb, g, g)


def loss_head(cfg, h, target, w, *, name):
    d, q, nc = cfg.d, cfg.chunk, cfg.nchunks
    tpb = cfg.seq // q

    def body(h_ref, t_ref, w_ref, loss_ref, dh_ref, dw_ref, dhb_ref):
        j = pl.program_id(1)

        @pl.when(jnp.logical_and(j == 0, pl.program_id(0) == 0))
        def _():
            loss_ref[...] = jnp.zeros_like(loss_ref)
            dw_ref[...] = jnp.zeros_like(dw_ref)

        @pl.when(j == 0)
        def _():
            dh_ref[...] = jnp.zeros_like(dh_ref)
            dhb_ref[...] = jnp.zeros_like(dhb_ref)

        @pl.when(j > 0)
        def _():
            xv = h_ref[...]
            r = lax.rsqrt(jnp.mean(xv * xv, axis=-1, keepdims=True) + EPS)
            xh = xv * r
            err = xh * w_ref[...] - t_ref[...]
            loss_ref[...] += 0.5 * jnp.sum(jnp.sum(err * err, axis=-1, keepdims=True) / d, axis=0, keepdims=True)
            dy = err * (1.0 / d)
            g = dy * w_ref[...]
            dh = r * (g - xh * jnp.mean(g * xh, axis=-1, keepdims=True))
            dh_ref[...] = dh
            dhb_ref[...] = dh.astype(BF16)
            dw_ref[...] += jnp.sum(dy * xh, axis=0, keepdims=True)

    row = pl.BlockSpec((q, d), lambda b, j: (b * nc + j, 0))
    loss, dh, dw, dhb = pl.pallas_call(
        body, name=name, grid=(cfg.bsz, nc),
        in_specs=[row, pl.BlockSpec((q, d), lambda b, j: (b * tpb + jnp.maximum(j - 1, 0), 0)),
                  pl.BlockSpec((1, d), lambda b, j: (0, 0))],
        out_specs=[pl.BlockSpec((8, LANE), lambda b, j: (0, 0)), row, pl.BlockSpec((1, d), lambda b, j: (0, 0)), row],
        out_shape=[_sds((8, LANE), F32), _sds((cfg.t, d), F32), _sds((1, d), F32), _sds((cfg.t, d), BF16)],
        compiler_params=_cp(),
    )(h, target, w.reshape(1, d))
    return loss[0, 0], (dh, dhb), dw[0]


def _rows_tile(r, c):
    return _pick(r, max(8, (1 << 18) // max(c, 1) // 8 * 8), 8)


def _adam_update(w, g, m, v):
    c1 = 1.0 - ADAM_B1 ** ADAM_STEP
    c2 = 1.0 - ADAM_B2 ** ADAM_STEP
    mn = ADAM_B1 * m + (1.0 - ADAM_B1) * g
    vn = ADAM_B2 * v + (1.0 - ADAM_B2) * (g * g)
    delta = -ADAM_LR * ((mn / c1) / (jnp.sqrt(vn / c2) + ADAM_EPS) + ADAM_WD * w)
    return delta, mn, vn


def adamw_layer(w, m, v, g, li, prev, dep, *, name):
    _, r, c = w.shape
    tr = _rows_tile(r, c)

    def body(*refs):
        w_ref, m_ref, v_ref, g_ref = refs[:4]
        go_ref, d_ref, mo_ref, vo_ref = refs[-4:]
        gv = g_ref[...]
        delta, mn, vn = _adam_update(w_ref[0], gv, m_ref[0], v_ref[0])
        go_ref[0] = gv
        d_ref[0] = delta
        mo_ref[0] = mn
        vo_ref[0] = vn

    if tr * c * 4 >= (1 << 16):
        steps = r // tr
        blk3 = pl.BlockSpec((1, tr, c), lambda i: (li, i, 0))
        blk2 = pl.BlockSpec((tr, c), lambda i: (i, 0))
    else:
        tc = _pick(c, max(LANE, (1 << 18) // r // LANE * LANE), LANE)
        steps = c // tc
        blk3 = pl.BlockSpec((1, r, tc), lambda i: (li, 0, i))
        blk2 = pl.BlockSpec((r, tc), lambda i: (0, i))
    anyspec = pl.BlockSpec(memory_space=pl.ANY)
    in_specs = [blk3, blk3, blk3, blk2, anyspec]
    args = [w, m, v, g, dep]
    aliases = {}
    if prev is not None:
        in_specs += [anyspec] * 4
        args += list(prev)
        aliases = {5 + i: i for i in range(4)}
    return pl.pallas_call(
        body, name=name, grid=(steps,), in_specs=in_specs, out_specs=[blk3] * 4,
        out_shape=[_sds(w.shape, F32)] * 4, input_output_aliases=aliases, compiler_params=_cp(),
    )(*args)


def pair_add(g4, other, half, *, name):
    n, _, r, c = g4.shape
    tr = _rows_tile(r, c)

    def body(h_ref, a_ref, b_ref, o_ref):
        o_ref[0] = (a_ref[0, 0].astype(F32) + b_ref[0].astype(F32)).astype(BF16)

    blk = pl.BlockSpec((1, tr, c), lambda j, i, h: (j, i, 0))
    grid_spec = pltpu.PrefetchScalarGridSpec(
        num_scalar_prefetch=1, grid=(n, r // tr),
        in_specs=[pl.BlockSpec((1, 1, tr, c), lambda j, i, h: (j, h[0], i, 0)), blk], out_specs=blk)
    return pl.pallas_call(body, name=name, grid_spec=grid_spec, out_shape=_sds((n, r, c), BF16),
                          compiler_params=_cp())(half, g4, other)


def chip_sum(recv, part, where, *, name):
    n, r, c = recv.shape
    tr = _rows_tile(r, c)

    def body(s_ref, *refs):
        own_ref, o_ref = refs[n], refs[n + 1]
        acc = None
        for j in range(n):
            term = jnp.where(s_ref[0] == j, own_ref[0], refs[j][0]).astype(F32)
            acc = term if acc is None else acc + term
        o_ref[0] = acc

    def slot(j):
        return pl.BlockSpec((1, tr, c), lambda i, s: (jnp.where(s[0] == j, (j + 1) % n, j), i, 0))

    grid_spec = pltpu.PrefetchScalarGridSpec(
        num_scalar_prefetch=1, grid=(r // tr,),
        in_specs=[slot(j) for j in range(n)] + [pl.BlockSpec((1, tr, c), lambda i, s: (s[0], i, 0))],
        out_specs=pl.BlockSpec((1, tr, c), lambda i, s: (s[1], i, 0)))
    return pl.pallas_call(body, name=name, grid_spec=grid_spec, out_shape=_sds((2, r, c), F32),
                          compiler_params=_cp())(where, *([recv] * n), part)


def _coords():
    return lax.axis_index("x"), lax.axis_index("y"), lax.axis_index("c")


def _other_chips(x, y):
    return [(1 - x, y), (x, 1 - y), (1 - x, 1 - y)]


def gather_chips(arrs, *, name):
    n = len(arrs)
    anyspec = pl.BlockSpec(memory_space=pl.ANY)

    def body(*refs):
        ins, outs = refs[:n], refs[n:2 * n]
        send_sems, recv_sems, local_sems = refs[2 * n:]
        x, y, c = _coords()
        me = 2 * x + y
        chips = _other_chips(x, y)
        copies = []
        for k in range(n):
            loc = pltpu.make_async_copy(ins[k], outs[k].at[me], local_sems.at[k])
            loc.start()
            copies.append(loc)
        sends = []
        for k in range(n):
            for j, (px, py) in enumerate(chips):
                cp = pltpu.make_async_remote_copy(
                    src_ref=ins[k], dst_ref=outs[k].at[me], send_sem=send_sems.at[k, j], recv_sem=recv_sems.at[k, j],
                    device_id=(px, py, c), device_id_type=MESH)
                cp.start()
                sends.append(cp)
        for k in range(n):
            for j, (px, py) in enumerate(chips):
                pltpu.make_async_remote_copy(
                    src_ref=ins[k], dst_ref=outs[k].at[2 * px + py], send_sem=send_sems.at[k, j],
                    recv_sem=recv_sems.at[k, j], device_id=(px, py, c), device_id_type=MESH).wait_recv()
        for cp in sends:
            cp.wait_send()
        for cp in copies:
            cp.wait()

    return pl.pallas_call(
        body, name=name, in_specs=[anyspec] * n, out_specs=[anyspec] * n,
        out_shape=[_sds((4,) + a.shape, a.dtype) for a in arrs],
        scratch_shapes=[pltpu.SemaphoreType.DMA((n, 3)), pltpu.SemaphoreType.DMA((n, 3)), pltpu.SemaphoreType.DMA((n,))],
        compiler_params=_cp(has_side_effects=True),
    )(*arrs)


def allreduce_small(vec, after, *, name):
    r, c = vec.shape

    def body(v_ref, after_ref, o_ref, buf, send_sems, recv_sems):
        x, y, cc = _coords()
        me = 4 * x + 2 * y + cc
        buf[me] = v_ref[...]
        sends = []
        flips = [(fx, fy, fc) for fx in (0, 1) for fy in (0, 1) for fc in (0, 1)][1:]
        for j, (fx, fy, fc) in enumerate(flips):
            peer = ((1 - x) if fx else x, (1 - y) if fy else y, (1 - cc) if fc else cc)
            cp = pltpu.make_async_remote_copy(
                src_ref=v_ref, dst_ref=buf.at[me], send_sem=send_sems.at[j], recv_sem=recv_sems.at[j],
                device_id=peer, device_id_type=MESH)
            cp.start()
            sends.append(cp)
        for j, (fx, fy, fc) in enumerate(flips):
            px, py, pc = ((1 - x) if fx else x, (1 - y) if fy else y, (1 - cc) if fc else cc)
            pltpu.make_async_remote_copy(
                src_ref=v_ref, dst_ref=buf.at[4 * px + 2 * py + pc], send_sem=send_sems.at[j],
                recv_sem=recv_sems.at[j], device_id=(px, py, pc), device_id_type=MESH).wait_recv()
        for cp in sends:
            cp.wait_send()
        acc = buf[0]
        for k in range(1, 8):
            acc = acc + buf[k]
        o_ref[...] = acc

    vm = pl.BlockSpec(memory_space=pltpu.VMEM)
    return pl.pallas_call(
        body, name=name, in_specs=[vm, pl.BlockSpec(memory_space=pl.ANY)], out_specs=vm, out_shape=_sds((r, c), F32),
        scratch_shapes=[pltpu.VMEM((8, r, c), F32), pltpu.SemaphoreType.DMA((7,)), pltpu.SemaphoreType.DMA((7,))],
        compiler_params=_cp(has_side_effects=True),
    )(vec, after)


def pair_share(lands, owns, *, name):
    n = len(lands)
    anyspec = pl.BlockSpec(memory_space=pl.ANY)

    def body(*refs):
        ins, own_refs, outs = refs[:n], refs[n:2 * n], refs[2 * n:3 * n]
        send_sems, recv_sems = refs[3 * n:]
        x, y, c = _coords()
        me = 2 * x + y
        sib = (x, y, 1 - c)
        sends = []
        for k in range(n):
            for j, (px, py) in enumerate(_other_chips(x, y)):
                cp = pltpu.make_async_remote_copy(
                    src_ref=ins[k].at[2 * px + py, c], dst_ref=outs[k].at[2 * px + py, c], send_sem=send_sems.at[k, j],
                    recv_sem=recv_sems.at[k, j], device_id=sib, device_id_type=MESH)
                cp.start()
                sends.append(cp)
            cp = pltpu.make_async_remote_copy(
                src_ref=own_refs[k], dst_ref=outs[k].at[me], send_sem=send_sems.at[k, 3], recv_sem=recv_sems.at[k, 3],
                device_id=sib, device_id_type=MESH)
            cp.start()
            sends.append(cp)
        for k in range(n):
            for j, (px, py) in enumerate(_other_chips(x, y)):
                pltpu.make_async_remote_copy(
                    src_ref=ins[k].at[2 * px + py, c], dst_ref=outs[k].at[2 * px + py, 1 - c],
                    send_sem=send_sems.at[k, j], recv_sem=recv_sems.at[k, j], device_id=sib,
                    device_id_type=MESH).wait_recv()
            pltpu.make_async_remote_copy(
                src_ref=own_refs[k], dst_ref=outs[k].at[me], send_sem=send_sems.at[k, 3], recv_sem=recv_sems.at[k, 3],
                device_id=sib, device_id_type=MESH).wait_recv()
        for cp in sends:
            cp.wait_send()

    return pl.pallas_call(
        body, name=name, in_specs=[anyspec] * (2 * n), out_specs=[anyspec] * n,
        out_shape=[_sds(a.shape, a.dtype) for a in lands], input_output_aliases={k: k for k in range(n)},
        scratch_shapes=[pltpu.SemaphoreType.DMA((n, 4)), pltpu.SemaphoreType.DMA((n, 4))],
        compiler_params=_cp(has_side_effects=True),
    )(*lands, *owns)


def pair_fill(arrs, *, name):
    n = len(arrs)
    anyspec = pl.BlockSpec(memory_space=pl.ANY)

    def body(*refs):
        ins, outs = refs[:n], refs[n:2 * n]
        send_sems, recv_sems = refs[2 * n:]
        x, y, c = _coords()
        sends = []
        for k in range(n):
            cp = pltpu.make_async_remote_copy(
                src_ref=ins[k].at[c], dst_ref=outs[k].at[c], send_sem=send_sems.at[k], recv_sem=recv_sems.at[k],
                device_id=(x, y, 1 - c), device_id_type=MESH)
            cp.start()
            sends.append(cp)
        for k in range(n):
            pltpu.make_async_remote_copy(
                src_ref=ins[k].at[c], dst_ref=outs[k].at[1 - c], send_sem=send_sems.at[k], recv_sem=recv_sems.at[k],
                device_id=(x, y, 1 - c), device_id_type=MESH).wait_recv()
        for cp in sends:
            cp.wait_send()

    return pl.pallas_call(
        body, name=name, in_specs=[anyspec] * n, out_specs=[anyspec] * n,
        out_shape=[_sds(a.shape, a.dtype) for a in arrs], input_output_aliases={k: k for k in range(n)},
        scratch_shapes=[pltpu.SemaphoreType.DMA((n,)), pltpu.SemaphoreType.DMA((n,))],
        compiler_params=_cp(has_side_effects=True),
    )(*arrs)


_HBM = pl.BlockSpec(memory_space=pltpu.HBM)
_SEM = pl.BlockSpec(memory_space=pltpu.SEMAPHORE)


_COPIES_PER_ARRAY = {"gather": 3, "scatter": 3, "share": 4, "exchange": 4}


def _ici_copies(kind, srcs, lands, send_sems, recv_sems):
    x, y, c = _coords()
    me = 2 * x + y
    per = _COPIES_PER_ARRAY[kind]
    sends, recvs = [], []
    for k in range(len(srcs)):
        triples = []
        for j, (px, py) in enumerate(_other_chips(x, y)):
            peer = 2 * px + py
            if kind == "gather":
                triples.append((srcs[k].at[c], lands[k].at[me, c], lands[k].at[peer, c], (px, py, c)))
            elif kind == "scatter":
                triples.append((srcs[k].at[peer], lands[k].at[me], lands[k].at[peer], (px, py, c)))
            elif kind == "share":
                triples.append((lands[k].at[peer, c], lands[k].at[peer, c], lands[k].at[peer, 1 - c], (x, y, 1 - c)))
        if kind == "share":
            triples.append((srcs[k], lands[k].at[me], lands[k].at[me], (x, y, 1 - c)))
        if kind == "exchange":
            triples = [(srcs[k].at[j, 1 - c], lands[k].at[j], lands[k].at[j], (x, y, 1 - c)) for j in range(4)]
        for j, (src, there, here, dev) in enumerate(triples):
            sem = per * k + j
            mk = functools.partial(pltpu.make_async_remote_copy, src_ref=src, send_sem=send_sems.at[sem],
                                   recv_sem=recv_sems.at[sem], device_id=dev, device_id_type=MESH)
            sends.append(mk(dst_ref=there))
            recvs.append(mk(dst_ref=here))
    return sends, recvs


def ici_start(kind, srcs, lands, after, *, name):
    n = len(srcs)

    def body(*refs):
        src_refs, land_refs = refs[:n], refs[n:2 * n]
        send_sems, recv_sems = refs[2 * n + 1], refs[2 * n + 2]
        token = refs[-1]
        sends, _ = _ici_copies(kind, src_refs, land_refs, send_sems, recv_sems)
        for cp in sends:
            cp.start()
        token[...] = jnp.zeros_like(token)

    both = list(srcs) + list(lands)
    out = pl.pallas_call(
        body, name=name,
        in_specs=[_HBM] * (2 * n) + [pl.BlockSpec(memory_space=pl.ANY)],
        out_shape=(pltpu.SemaphoreType.DMA((_COPIES_PER_ARRAY[kind] * n,)),
                   pltpu.SemaphoreType.DMA((_COPIES_PER_ARRAY[kind] * n,)),
                   *[pltpu.HBM(a.shape, a.dtype) for a in both], _sds((8, LANE), F32)),
        out_specs=(_SEM, _SEM, *([_HBM] * (2 * n)), pl.BlockSpec(memory_space=pltpu.VMEM)),
        input_output_aliases={i: 2 + i for i in range(2 * n)},
        compiler_params=_cp(has_side_effects=pltpu.SideEffectType.DATAFLOW_SIDE_EFFECTING),
    )(*[pltpu.with_memory_space_constraint(a, pltpu.HBM) for a in both], after)
    return out[0], out[1], list(out[2:2 + n]), list(out[2 + n:2 + 2 * n]), out[-1]


def ici_wait(kind, started, after, *, name):
    send_sems, recv_sems, srcs, lands, _ = started
    n = len(srcs)

    def body(*refs):
        src_refs, land_refs = refs[:n], refs[n:2 * n]
        sends, recvs = _ici_copies(kind, src_refs, land_refs, refs[2 * n], refs[2 * n + 1])
        for cp in sends:
            cp.wait_send()
        for cp in recvs:
            cp.wait_recv()

    both = list(srcs) + list(lands)
    out = pl.pallas_call(
        body, name=name,
        in_specs=[_HBM] * (2 * n) + [_SEM, _SEM, pl.BlockSpec(memory_space=pl.ANY)],
        out_shape=tuple(pltpu.HBM(a.shape, a.dtype) for a in both), out_specs=tuple([_HBM] * (2 * n)),
        input_output_aliases={i: i for i in range(2 * n)},
        compiler_params=_cp(has_side_effects=pltpu.SideEffectType.DATAFLOW_SIDE_EFFECTING),
    )(*both, send_sems, recv_sems, after)
    return list(out[:n]), list(out[n:])


BIG = ["w_in", "w_uq", "w_ukv", "w_branch_ssm", "w_branch_mla", "w_out", "w_mlp_up", "w_mlp_down"]
COL_SHARDED = {"w_in", "w_uq", "w_ukv", "w_mlp_up"}
SMALL_REPL = ["norm_mix_w", "conv_b", "dt_bias", "a_log", "d_skip", "ssm_norm_w", "q_norm_w", "kv_norm_w", "norm_mlp_w"]


def _unshard_layer(name, g):
    _, r, c = g.shape
    if name in COL_SHARDED:
        return jnp.transpose(g, (1, 0, 2)).reshape(r, 4 * c)
    return g.reshape(4 * r, c)


def _to_shards(name, full):
    r, c = full.shape
    if name in COL_SHARDED:
        return jnp.transpose(full.reshape(r, 4, c // 4), (1, 0, 2))
    return full.reshape(4, r // 4, c)


REST = [k for k in BIG if k != "w_in"]


def prep_layer(cfg, w):
    out = {}
    if "w_in" in w:
        sp = np.cumsum(cfg.in_splits)[:-1].tolist()
        z, xbc, dt, cq, ckv, kr, gs, gm = jnp.split(w["w_in"], sp, axis=1)
        zpad = lambda n: jnp.zeros((cfg.d, n), z.dtype)
        out.update(w_z=z, w_xbc=xbc, w_g=jnp.concatenate([gs, gm], axis=1),
                   w_s=jnp.concatenate([cq, ckv, kr, zpad(LANE - cfg.rope), dt, zpad(LANE - cfg.heads)], axis=1))
    if "w_uq" in w:
        out.update(
            w_uq=jnp.pad(w["w_uq"].reshape(cfg.ql, cfg.mh, cfg.nope + cfg.rope),
                         ((0, 0), (0, 0), (0, 2 * LANE - cfg.nope - cfg.rope))).reshape(cfg.ql, cfg.qw),
            w_ukv=w["w_ukv"], w_bs=w["w_branch_ssm"], w_bm=w["w_branch_mla"], w_out=w["w_out"],
            w_up=w["w_mlp_up"], w_down=w["w_mlp_down"])
    return {k: v.astype(BF16) for k, v in out.items()}


def unprep_grads(cfg, g):
    out = {}
    if "w_s" in g:
        ql, kvl = cfg.ql, cfg.kvl
        ds_ = g["w_s"]
        cq, ckv = ds_[:, :ql], ds_[:, ql:ql + kvl]
        kr = ds_[:, ql + kvl:ql + kvl + cfg.rope]
        dt = ds_[:, ql + kvl + LANE:ql + kvl + LANE + cfg.heads]
        out["w_in"] = jnp.concatenate([g["w_z"], g["w_xbc"], dt, cq, ckv, kr, g["w_g"]], axis=1)
    if "w_uq" in g:
        out.update(
            w_uq=g["w_uq"].reshape(cfg.ql, cfg.mh, 2 * LANE)[:, :, :cfg.nope + cfg.rope].reshape(cfg.ql, -1),
            w_ukv=g["w_ukv"], w_branch_ssm=g["w_bs"], w_branch_mla=g["w_bm"],
            w_out=g["w_out"], w_mlp_up=g["w_up"], w_mlp_down=g["w_down"])
    return out


def _hook(hooks, name, arg):
    if hooks and name in hooks:
        return hooks[name](arg)[0, 0]
    return 0.0


def layer_fwd(cfg, h, pw, sm, tabs, li, hooks=None):
    n = lambda s: f"l{li}_{s}"
    u = rmsnorm_fwd(h, sm["norm_mix_w"], name=n("norm_mix"))
    z, xbc, g, small = matmul_multi(u, [pw["w_z"], pw["w_xbc"], pw["w_g"], pw["w_s"]], (BF16, F32, BF16, F32),
                                    name=n("in_proj"))
    xc, dsilu = conv_fwd(cfg, xbc, sm["conv_w"], sm["conv_b"], name=n("conv"))
    dt_bias = sm["dt_bias_p"] + _hook(hooks, "after_conv", xc)
    y, sin = ssd_fwd(cfg, xc, small, dt_bias, sm["avec"], sm["dexp"], name=n("ssd"))
    y_ssm = tail_fwd(cfg, y, z, sm["ssm_norm_w"], name=n("tail"))
    if hooks and "weights" in hooks:
        pw = dict(pw, **hooks["weights"](y_ssm))
    cqn = rmsnorm_fwd(small, sm["q_norm_w"], cw=cfg.ql, ci=0, name=n("q_norm"))
    ckvn = rmsnorm_fwd(small, sm["kv_norm_w"], cw=cfg.kvl, ci=cfg.ql // cfg.kvl, name=n("kv_norm"))
    qf = matmul(cqn, pw["w_uq"], out_dtype=BF16, name=n("uq"))
    kv = matmul(ckvn, pw["w_ukv"], out_dtype=BF16, name=n("ukv"))
    qr, kpe = rope_fwd(cfg, qf, small, tabs, name=n("rope"))
    o, lse = attn_fwd(cfg, qr, kv, kpe, name=n("attn"))
    ya = matmul(y_ssm, pw["w_bs"], out_dtype=BF16, name=n("branch_ssm"))
    yb = matmul(o, pw["w_bm"], out_dtype=BF16, name=n("branch_mla"))
    mixed = gate_fwd(cfg, ya, yb, g, name=n("gate"))
    h1 = matmul(mixed, pw["w_out"], add=h, name=n("out"))
    v = rmsnorm_fwd(h1, sm["norm_mlp_w"] + _hook(hooks, "after_attn", o), name=n("norm_mlp"))
    a, act = matmul(v, pw["w_up"], name=n("up"), epilogue=_ep_relu2, out_dtypes=(BF16, BF16))
    h2 = matmul(act, pw["w_down"], add=h1, name=n("down"))
    saved = dict(h=h, u=u, z=z, xbc=xbc, g=g, small=small, xc=xc, dsilu=dsilu, y=y, sin=sin, y_ssm=y_ssm, cqn=cqn, ckvn=ckvn,
                 qr=qr, kv=kv, kpe=kpe, o=o, lse=lse, ya=ya, yb=yb, mixed=mixed, h1=h1, v=v, a=a, act=act)
    return h2, saved, pw


def layer_bwd(cfg, dh2, pw, sm, tabs, s, li, hooks=None):
    n = lambda t: f"l{li}_b_{t}"
    gw, gs = {}, {}
    wgrad = functools.partial(matmul, ta=True, out_dtype=BF16)
    dh2, dh2b = dh2
    gw["w_down"] = wgrad(s["act"], dh2b, name=n("dw_down"))
    da = matmul(dh2b, pw["w_down"], tb=True, name=n("dact"), epilogue=_ep_relu2_grad, extras=(s["a"],),
                out_dtypes=(BF16,))
    gw["w_up"] = wgrad(s["v"], da, name=n("dw_up"))
    dv = matmul(da, pw["w_up"], tb=True, out_dtype=BF16, name=n("dv"))
    dh1, gs["norm_mlp_w"], dh1b = rmsnorm_bwd(dv, s["h1"], sm["norm_mlp_w"], res=dh2, with_bf16=True,
                                              name=n("norm_mlp"))
    gw["w_out"] = wgrad(s["mixed"], dh1b, name=n("dw_out"))
    dmix = matmul(dh1b, pw["w_out"], tb=True, out_dtype=BF16, name=n("dmix"))
    dya, dyb, dg = gate_bwd(cfg, dmix, s["ya"], s["yb"], s["g"], name=n("gate"))
    gw["w_bs"] = wgrad(s["y_ssm"], dya, name=n("dw_bs"))
    gw["w_bm"] = wgrad(s["o"], dyb, name=n("dw_bm"))
    dy_ssm = matmul(dya, pw["w_bs"], tb=True, out_dtype=BF16, name=n("dy_ssm"))
    do = matmul(dyb, pw["w_bm"], tb=True, out_dtype=BF16, name=n("do"))
    dq, dkv, dkpe = attn_bwd(cfg, s["qr"], s["kv"], s["kpe"], s["o"], s["lse"], do, name=n("attn"))
    dqf, dkr = rope_bwd(cfg, dq, dkpe, tabs, name=n("rope"))
    gw["w_uq"] = wgrad(s["cqn"], dqf, name=n("dw_uq"))
    gw["w_ukv"] = wgrad(s["ckvn"], dkv, name=n("dw_ukv"))
    dcqn = matmul(dqf, pw["w_uq"], tb=True, name=n("dcqn"))
    dckvn = matmul(dkv, pw["w_ukv"], tb=True, name=n("dckvn"))
    q_norm_w = sm["q_norm_w"] + _hook(hooks, "after_attn", dqf)
    dcq, gs["q_norm_w"] = rmsnorm_bwd(dcqn, s["small"], q_norm_w, cw=cfg.ql, ci=0, out_dtype=BF16, name=n("q_norm"))
    dckv, gs["kv_norm_w"] = rmsnorm_bwd(dckvn, s["small"], sm["kv_norm_w"], cw=cfg.kvl, ci=cfg.ql // cfg.kvl,
                                        out_dtype=BF16, name=n("kv_norm"))
    ssm_norm_w = sm["ssm_norm_w"] + _hook(hooks, "early", dict(gw))
    dy, dz, gs["ssm_norm_w"] = tail_bwd(cfg, dy_ssm, s["y"], s["z"], ssm_norm_w, name=n("tail"))
    dxc, ddt, ddexp, dav, dbias = ssd_bwd(cfg, s["xc"], s["small"], sm["dt_bias_p"], sm["avec"], sm["dexp"],
                                          s["sin"], dy, name=n("ssd"))
    conv_w = sm["conv_w"] + _hook(hooks, "after_ssd", dxc)
    dxbc, gs["conv_w"], gs["conv_b"] = conv_bwd(cfg, s["xbc"], conv_w, s["dsilu"], dxc, name=n("conv"))
    gs["d_skip"] = ddexp.reshape(cfg.heads, cfg.hd).sum(axis=1)
    gs["a_log"] = (dav[0] * sm["avec"][0])[:cfg.heads]
    gs["dt_bias"] = dbias[0, :cfg.heads]
    dsmall = jnp.concatenate([dcq, dckv, dkr.astype(BF16), ddt.astype(BF16)], axis=1)
    gw["w_z"] = wgrad(s["u"], dz, name=n("dw_z"))
    gw["w_xbc"] = wgrad(s["u"], dxbc, name=n("dw_xbc"))
    gw["w_g"] = wgrad(s["u"], dg, name=n("dw_g"))
    gw["w_s"] = wgrad(s["u"], dsmall, name=n("dw_s"))
    du = matmul_nt_sum([dz, dxbc, dg, dsmall], [pw["w_z"], pw["w_xbc"], pw["w_g"], pw["w_s"]], out_dtype=BF16,
                       name=n("du"))
    if li > 0:
        dh, gs["norm_mix_w"], dhb = rmsnorm_bwd(du, s["h"], sm["norm_mix_w"], res=dh1, with_bf16=True,
                                                name=n("norm_mix"))
    else:
        dh, gs["norm_mix_w"] = rmsnorm_bwd(du, s["h"], sm["norm_mix_w"], res=dh1, name=n("norm_mix"))
        dhb = None
    return (dh, dhb), gw, gs


def small_params(cfg, p, li):
    pad_l = lambda v: jnp.pad(v, (0, LANE - v.shape[0])).reshape(1, LANE)
    return dict(
        norm_mix_w=p["norm_mix_w"][li], conv_w=p["conv_w"][li], conv_b=p["conv_b"][li],
        dt_bias_p=pad_l(p["dt_bias"][li]), avec=pad_l(-jnp.exp(p["a_log"][li])),
        dexp=jnp.repeat(p["d_skip"][li], cfg.hd).reshape(1, cfg.inner),
        ssm_norm_w=p["ssm_norm_w"][li], q_norm_w=p["q_norm_w"][li], kv_norm_w=p["kv_norm_w"][li],
        norm_mlp_w=p["norm_mlp_w"][li])


def local_step(cfg, x, target, p, depth=2):
    bsz, d = cfg.bsz, cfg.d
    lead = jnp.zeros((bsz, cfg.pad, d), F32)
    meta = jnp.broadcast_to(p["meta_tokens"][None], (bsz, cfg.n_meta, d))
    h = jnp.concatenate([lead, meta, x], axis=1).reshape(cfg.t, d)
    tabs = rope_tables(cfg)
    saved, sms = [], []
    for li in range(depth):
        sm = small_params(cfg, p, li)
        h, s, _ = layer_fwd(cfg, h, p["pw"][li], sm, tabs, li)
        saved.append(s)
        sms.append(sm)
    loss, dh, dfw = loss_head(cfg, h, target.reshape(bsz * cfg.seq, d), p["final_norm_w"], name="loss_head")
    gws, gss = [None] * depth, [None] * depth
    for li in reversed(range(depth)):
        dh, gws[li], gss[li] = layer_bwd(cfg, dh, p["pw"][li], sms[li], tabs, saved[li], li)
    dh = dh[0].reshape(bsz, cfg.lp, d)
    grad_x = dh[:, cfg.chunk:, :]
    gmeta = jnp.sum(dh[:, cfg.pad:cfg.chunk, :], axis=0)
    return loss, grad_x, gmeta, gws, gss, dfw


def _pack_small(parts):
    flat = jnp.concatenate([a.reshape(-1) for a in parts])
    n = flat.shape[0]
    npad = -n % (8 * LANE)
    return jnp.pad(flat, (0, npad)).reshape(-1, LANE), n


def _unpack_small(vec, shapes):
    flat = vec.reshape(-1)
    out, off = [], 0
    for sh in shapes:
        sz = int(np.prod(sh))
        out.append(flat[off:off + sz].reshape(sh))
        off += sz
    return out


def _as2d(a):
    return a.reshape(-1, a.shape[-1])


def kernel(x, meta_tokens, norm_mix_w, w_in, conv_w, conv_b, dt_bias, a_log, d_skip, ssm_norm_w, q_norm_w, kv_norm_w, w_uq, w_ukv, w_branch_ssm, w_branch_mla, w_out, norm_mlp_w, w_mlp_up, w_mlp_down, final_norm_w, loss_target, m_meta_tokens, m_norm_mix_w, m_w_in, m_conv_w, m_conv_b, m_dt_bias, m_a_log, m_d_skip, m_ssm_norm_w, m_q_norm_w, m_kv_norm_w, m_w_uq, m_w_ukv, m_w_branch_ssm, m_w_branch_mla, m_w_out, m_norm_mlp_w, m_w_mlp_up, m_w_mlp_down, m_final_norm_w, v_meta_tokens, v_norm_mix_w, v_w_in, v_conv_w, v_conv_b, v_dt_bias, v_a_log, v_d_skip, v_ssm_norm_w, v_q_norm_w, v_kv_norm_w, v_w_uq, v_w_ukv, v_w_branch_ssm, v_w_branch_mla, v_w_out, v_norm_mlp_w, v_w_mlp_up, v_w_mlp_down, v_final_norm_w):
    cfg = CFG
    names = ["meta_tokens", "norm_mix_w", "w_in", "conv_w", "conv_b", "dt_bias", "a_log", "d_skip", "ssm_norm_w",
             "q_norm_w", "kv_norm_w", "w_uq", "w_ukv", "w_branch_ssm", "w_branch_mla", "w_out", "norm_mlp_w",
             "w_mlp_up", "w_mlp_down", "final_norm_w"]
    wts = dict(zip(names, [meta_tokens, norm_mix_w, w_in, conv_w, conv_b, dt_bias, a_log, d_skip, ssm_norm_w,
                           q_norm_w, kv_norm_w, w_uq, w_ukv, w_branch_ssm, w_branch_mla, w_out, norm_mlp_w,
                           w_mlp_up, w_mlp_down, final_norm_w]))
    ms = dict(zip(names, [m_meta_tokens, m_norm_mix_w, m_w_in, m_conv_w, m_conv_b, m_dt_bias, m_a_log, m_d_skip,
                          m_ssm_norm_w, m_q_norm_w, m_kv_norm_w, m_w_uq, m_w_ukv, m_w_branch_ssm, m_w_branch_mla,
                          m_w_out, m_norm_mlp_w, m_w_mlp_up, m_w_mlp_down, m_final_norm_w]))
    vs = dict(zip(names, [v_meta_tokens, v_norm_mix_w, v_w_in, v_conv_w, v_conv_b, v_dt_bias, v_a_log, v_d_skip,
                          v_ssm_norm_w, v_q_norm_w, v_kv_norm_w, v_w_uq, v_w_ukv, v_w_branch_ssm, v_w_branch_mla,
                          v_w_out, v_norm_mlp_w, v_w_mlp_up, v_w_mlp_down, v_final_norm_w]))
    cx, cy, cc = _coords()
    chip = 2 * cx + cy

    half1 = jnp.reshape(cc, (1,)).astype(jnp.int32)
    where2 = jnp.stack([chip, cc]).astype(jnp.int32)
    wb = {k: wts[k].astype(BF16) for k in BIG}
    zero_tok = jnp.zeros((8, LANE), F32)

    def halves(a):
        return a.reshape((2, a.shape[0] // 2) + a.shape[1:])

    def gather_start(li, keys, tag, after):
        srcs = [halves(wb[k][li]) for k in keys]
        lands = [lax.empty((4,) + s.shape, BF16) for s in srcs]
        return ici_start("gather", srcs, lands, after, name=f"gather{li}{tag}_start")

    def gather_finish(li, keys, tag, started, after):
        srcs, lands = ici_wait("gather", started, after, name=f"gather{li}{tag}_wait")
        lands = pair_share(lands, srcs, name=f"gather{li}{tag}_share")
        full = {k: _unshard_layer(k, land.reshape((4, 2 * land.shape[2], land.shape[3])))
                for k, land in zip(keys, lands)}
        return prep_layer(cfg, full)

    def gather_mid(li, keys, tag, started, after):
        srcs, lands = ici_wait("gather", started, after, name=f"gather{li}{tag}_wait")
        return ici_start("share", srcs, lands, zero_tok, name=f"gather{li}{tag}_share_start")

    def gather_end(li, keys, tag, shared, after):
        _, lands = ici_wait("share", shared, after, name=f"gather{li}{tag}_share_wait")
        full = {k: _unshard_layer(k, land.reshape((4, 2 * land.shape[2], land.shape[3])))
                for k, land in zip(keys, lands)}
        return prep_layer(cfg, full)

    def exchange_start(li, keys, tag, gw, after):
        ug = unprep_grads(cfg, gw)
        g4 = []
        for k in keys:
            s = _to_shards(k, ug[k])
            g4.append(s.reshape(4, 2, s.shape[1] // 2, s.shape[2]))
        lands = [lax.empty((4,) + a.shape[2:], a.dtype) for a in g4]
        return ici_start("exchange", g4, lands, after, name=f"grad{li}{tag}_exchange_start")

    def reduce_start(li, keys, tag, exchanged, after):
        g4, theirs = ici_wait("exchange", exchanged, after, name=f"grad{li}{tag}_exchange_wait")
        parts = [pair_add(a, b, half1, name=f"grad{li}_pair_add_{k}") for k, a, b in zip(keys, g4, theirs)]
        lands = [lax.empty(q.shape, q.dtype) for q in parts]
        return ici_start("scatter", parts, lands, zero_tok, name=f"grad{li}{tag}_scatter_start")

    def reduce_finish(li, keys, tag, started, after):
        parts, lands = ici_wait("scatter", started, after, name=f"grad{li}{tag}_scatter_wait")
        sums = [chip_sum(rc, pt, where2, name=f"grad{li}_chip_sum_{k}") for k, rc, pt in zip(keys, lands, parts)]
        sums = pair_fill(sums, name=f"grad{li}{tag}_pair_fill")
        return {k: s.reshape(2 * s.shape[1], s.shape[2]) for k, s in zip(keys, sums)}

    gathered = gather_chips([meta_tokens, conv_w], name="gather_small")
    p = dict(wts)
    p["meta_tokens"] = jnp.transpose(gathered[0], (1, 0, 2)).reshape(cfg.n_meta, cfg.d)
    p["conv_w"] = jnp.transpose(gathered[1], (1, 2, 0, 3)).reshape(2, cfg.convk, cfg.conv_dim)

    st0a = gather_start(0, ["w_in"], "a", gathered[0])
    st0b = gather_start(0, REST, "b", st0a[4])
    st1 = gather_start(1, BIG, "", st0b[4])
    pw0 = gather_finish(0, ["w_in"], "a", st0a, st1[4])

    bsz, d = cfg.bsz, cfg.d
    lead = jnp.zeros((bsz, cfg.pad, d), F32)
    meta = jnp.broadcast_to(p["meta_tokens"][None], (bsz, cfg.n_meta, d))
    h0 = jnp.concatenate([lead, meta, x], axis=1).reshape(cfg.t, d)
    tabs = rope_tables(cfg)
    sm0 = small_params(cfg, p, 0)
    st = {}

    def step(key, fn):
        def run(arg):
            st[key] = fn(arg)
            return st[key][4]
        return run

    h1, sv0, pw0 = layer_fwd(cfg, h0, pw0, sm0, tabs, 0, hooks={
        "after_conv": step("share0b", lambda after: gather_mid(0, REST, "b", st0b, after)),
        "weights": lambda after: gather_end(0, REST, "b", st["share0b"], after),
        "after_attn": step("share1", lambda after: gather_mid(1, BIG, "", st1, after))})
    pw1 = gather_end(1, BIG, "", st["share1"], h1)
    sm1 = small_params(cfg, p, 1)
    h2, sv1, _ = layer_fwd(cfg, h1, pw1, sm1, tabs, 1)
    loss, dh, dfw = loss_head(cfg, h2, loss_target.reshape(bsz * cfg.seq, d), final_norm_w, name="loss_head")

    dh, gw1, gs1 = layer_bwd(cfg, dh, pw1, sm1, tabs, sv1, 1)
    ex1 = exchange_start(1, BIG, "", gw1, zero_tok)
    sm0b = dict(sm0)
    sm0b["norm_mlp_w"] = sm0["norm_mlp_w"] + ex1[4][0, 0]
    dh, gw0, gs0 = layer_bwd(cfg, dh, pw0, sm0b, tabs, sv0, 0, hooks={
        "after_attn": step("red1", lambda after: reduce_start(1, BIG, "", ex1, after)),
        "early": step("ex0e", lambda gw: exchange_start(0, REST, "e", gw, zero_tok)),
        "after_ssd": step("red0e", lambda after: reduce_start(0, REST, "e", st["ex0e"], after))})
    dh3 = dh[0].reshape(bsz, cfg.lp, d)
    grad_x = dh3[:, cfg.chunk:, :]
    gmeta = jnp.sum(dh3[:, cfg.pad:cfg.chunk, :], axis=0)
    big1 = reduce_finish(1, BIG, "", st["red1"], dh[0])
    ex0l = exchange_start(0, ["w_in"], "l", gw0, big1[BIG[-1]])

    small_names = SMALL_REPL + ["conv_w"]
    parts = [jnp.stack([gs0[k], gs1[k]]) for k in small_names] + [dfw, gmeta, loss.reshape(1)]
    shapes = [a.shape for a in parts]
    vec, _ = _pack_small(parts)
    red_vec = allreduce_small(vec, ex0l[4], name="allreduce_small")
    red = _unpack_small(red_vec, shapes)
    sg = dict(zip(small_names + ["final_norm_w", "meta_tokens"], red))
    loss = red[-1].reshape(())
    sg["conv_w"] = lax.dynamic_slice_in_dim(sg["conv_w"], chip * (cfg.conv_dim // 4), cfg.conv_dim // 4, axis=2)
    sg["meta_tokens"] = lax.dynamic_slice_in_dim(sg["meta_tokens"], chip * (cfg.d // 4), cfg.d // 4, axis=1)

    red0 = reduce_start(0, ["w_in"], "l", ex0l, red_vec)
    grads, deltas, new_m, new_v = {}, {}, {}, {}
    dep = red0[4]
    for k in names:
        if k in BIG:
            continue
        w2, g2, m2, v2 = _as2d(wts[k]), _as2d(sg[k]), _as2d(ms[k]), _as2d(vs[k])
        dl, mn, vn = adamw_small(w2, g2, m2, v2, dep, name=f"adamw_{k}")
        grads[k] = sg[k].reshape(wts[k].shape)
        deltas[k], new_m[k], new_v[k] = (t.reshape(wts[k].shape) for t in (dl, mn, vn))

    def view(k, a):
        return jnp.swapaxes(a, 1, 2) if k == "w_in" else a

    def gview(k, g):
        return g.T if k == "w_in" else g

    wv, mv, vv = ({k: view(k, t[k]) for k in BIG} for t in (wts, ms, vs))
    outs = {}
    for k in BIG:
        outs[k] = adamw_layer(wv[k], mv[k], vv[k], gview(k, big1[k]), 1, None, dep, name=f"adamw1_{k}")
        dep = outs[k][1]
    big0 = reduce_finish(0, REST, "e", st["red0e"], dep)
    for k in REST:
        outs[k] = adamw_layer(wv[k], mv[k], vv[k], big0[k], 0, outs[k], dep, name=f"adamw0_{k}")
        dep = outs[k][1]
    big0.update(reduce_finish(0, ["w_in"], "l", red0, dep))
    outs["w_in"] = adamw_layer(wv["w_in"], mv["w_in"], vv["w_in"], gview("w_in", big0["w_in"]), 0, outs["w_in"], dep,
                               name="adamw0_w_in")
    for k in BIG:
        grads[k], deltas[k], new_m[k], new_v[k] = (view(k, t) for t in outs[k])
    return (loss, grad_x, *[grads[k] for k in names], *[deltas[k] for k in names],
            *[new_m[k] for k in names], *[new_v[k] for k in names])


def adamw_small(w, g, m, v, dep, *, name):
    def body(w_ref, g_ref, m_ref, v_ref, dep_ref, d_ref, mo_ref, vo_ref):
        d_ref[...], mo_ref[...], vo_ref[...] = _adam_update(w_ref[...], g_ref[...], m_ref[...], v_ref[...])

    vm = pl.BlockSpec(memory_space=pltpu.VMEM)
    return pl.pallas_call(body, name=name, in_specs=[vm] * 4 + [pl.BlockSpec(memory_space=pl.ANY)], out_specs=[vm] * 3,
                          out_shape=[_sds(w.shape, F32)] * 3, compiler_params=_cp())(w, g, m, v, dep)
```

```python
import functools
from typing import NamedTuple

import numpy as np
import jax
import jax.numpy as jnp
from jax import lax
from jax.experimental import pallas as pl
from jax.experimental.pallas import tpu as pltpu

F32 = jnp.float32
BF16 = jnp.bfloat16
EPS = 1e-6
ROPE_THETA = 10000.0
LANE = 128
VMEM_LIMIT = 56 * 1024 * 1024
MASK_VALUE = -1e30
ADAM_LR, ADAM_B1, ADAM_B2, ADAM_EPS, ADAM_WD, ADAM_STEP = 0.001, 0.9, 0.999, 1e-08, 0.01, 10
MESH = pl.DeviceIdType.MESH


class Cfg(NamedTuple):
    d: int = 1024
    seq: int = 2048
    bsz: int = 2
    n_meta: int = 16
    inner: int = 2048
    hd: int = 64
    groups: int = 4
    state: int = 128
    convk: int = 4
    chunk: int = 128
    mh: int = 8
    ql: int = 512
    kvl: int = 256
    nope: int = 128
    rope: int = 64
    vd: int = 128
    ff: int = 4096

    @property
    def heads(self): return self.inner // self.hd
    @property
    def gw(self): return self.inner // self.groups
    @property
    def conv_dim(self): return self.inner + 2 * self.groups * self.state
    @property
    def pad(self): return self.chunk - self.n_meta
    @property
    def lp(self): return self.chunk + self.seq
    @property
    def t(self): return self.bsz * self.lp
    @property
    def nchunks(self): return self.lp // self.chunk
    @property
    def sw(self): return self.ql + self.kvl + 2 * LANE
    @property
    def kt(self): return (self.ql + self.kvl) // LANE
    @property
    def dtt(self): return self.kt + 1
    @property
    def qw(self): return self.mh * 2 * LANE
    @property
    def in_splits(self):
        return [self.inner, self.conv_dim, self.heads, self.ql, self.kvl, self.rope, self.d, self.d]


CFG = Cfg()


def _pick(dim, pref, mult):
    best = None
    for t in range(mult, min(dim, pref) + 1, mult):
        if dim % t == 0:
            best = t
    return best if best is not None else dim


def _cp(**kw):
    return pltpu.CompilerParams(vmem_limit_bytes=VMEM_LIMIT, **kw)


def _sds(shape, dtype):
    return jax.ShapeDtypeStruct(tuple(shape), dtype)


def _silu(x):
    return x * jax.nn.sigmoid(x)


def _dsilu(x):
    s = jax.nn.sigmoid(x)
    return s * (1.0 + x * (1.0 - s))


def _ep_plain(r):
    return (r,)


def _ep_add(r, res):
    return (r + res.astype(F32),)


def _ep_relu2(r):
    rp = jnp.maximum(r, 0.0)
    return r, rp * rp


def _ep_relu2_grad(r, a):
    return (r * (2.0 * jnp.maximum(a.astype(F32), 0.0)),)


MM_VMEM_BUDGET = 44 * 1024 * 1024


def _mm_tiles(m, n, k, a_bytes, b_bytes, io_bytes, ta):
    m_mult, m_cap = (LANE, 1024) if ta else (16, 1088)
    tms = [t for t in range(m_cap, 0, -m_mult) if m % t == 0] or [m]
    tns = [t for t in (1024, 512, 256, 128) if n % t == 0] or [n]
    best = None
    for tm in tms:
        for tn in tns:
            need = 2 * (tm * k * a_bytes + k * tn * b_bytes + tm * tn * io_bytes)
            if need <= MM_VMEM_BUDGET and (best is None or tm * tn > best[0] * best[1]):
                best = (tm, tn)
    if best is None:
        return (_pick(m, 512, m_mult), _pick(n, 512, LANE), _pick(k, 1088 if ta else 1024, 16 if ta else LANE))
    return best[0], best[1], k


def _resident_rows(m, n, k, a_bytes, b_bytes, io_bytes):
    w = n * k * b_bytes
    if w > 18 * 1024 * 1024:
        return None
    for tm in range(544, 255, -16):
        if m % tm == 0 and w + 2 * tm * (k * a_bytes + n * io_bytes) + tm * n * 4 <= MM_VMEM_BUDGET - (4 << 20):
            return tm
    return None


def matmul(a, b, *, ta=False, tb=False, out_dtype=F32, add=None, name, tm=None, tn=None, tk=None,
           epilogue=None, extras=(), out_dtypes=None):
    if add is not None:
        epilogue, extras = _ep_add, (add,)
    if epilogue is None:
        epilogue = _ep_plain
    out_dtypes = tuple(out_dtypes) if out_dtypes is not None else (out_dtype,)
    n_ex, n_out = len(extras), len(out_dtypes)
    if ta:
        k_dim, m_dim = a.shape
    else:
        m_dim, k_dim = a.shape
    if tb:
        n_dim, k2 = b.shape
    else:
        k2, n_dim = b.shape
    assert k_dim == k2, (a.shape, b.shape, ta, tb)
    resident = False
    if tm is None and tn is None and tk is None:
        io_bytes = sum(jnp.dtype(e.dtype).itemsize for e in extras) + sum(jnp.dtype(d).itemsize for d in out_dtypes)
        a_bytes, b_bytes = jnp.dtype(a.dtype).itemsize, jnp.dtype(b.dtype).itemsize
        tm = None if ta else _resident_rows(m_dim, n_dim, k_dim, a_bytes, b_bytes, io_bytes)
        if tm is not None:
            resident, tn, tk = True, n_dim, k_dim
        else:
            tm, tn, tk = _mm_tiles(m_dim, n_dim, k_dim, a_bytes, b_bytes, io_bytes, ta)
    elif ta:
        tm = tm or _pick(m_dim, 1024, LANE)
        tk = tk or _pick(k_dim, 1088, 16)
        tn = tn or _pick(n_dim, 1024, LANE)
    else:
        tm = tm or _pick(m_dim, 1088, 16)
        tk = tk or _pick(k_dim, 1024 if a.dtype == F32 else 2048, LANE)
        tn = tn or _pick(n_dim, 1024, LANE)
    nm, nn, nk = m_dim // tm, n_dim // tn, k_dim // tk
    dn = (((0 if ta else 1,), (1 if tb else 0,)), ((), ()))

    def body(*refs):
        a_ref, b_ref = refs[:2]
        ex_refs = refs[2:2 + n_ex]
        o_refs = refs[2 + n_ex:2 + n_ex + n_out]
        scr = refs[2 + n_ex + n_out:]
        p = lax.dot_general(a_ref[...].astype(BF16), b_ref[...].astype(BF16), dn, preferred_element_type=F32)

        def finish(r):
            outs = epilogue(r, *[e[...] for e in ex_refs])
            for o_ref, val, dt in zip(o_refs, outs, out_dtypes):
                o_ref[...] = val.astype(dt)

        if nk == 1:
            finish(p)
        else:
            acc = scr[0]
            k = pl.program_id(2)

            @pl.when(k == 0)
            def _():
                acc[...] = p

            @pl.when(k > 0)
            def _():
                acc[...] += p

            @pl.when(k == nk - 1)
            def _():
                finish(acc[...])

    if resident:
        row = pl.BlockSpec((tm, k_dim), lambda i: (i, 0))
        o_spec = pl.BlockSpec((tm, n_dim), lambda i: (i, 0))
        outs = pl.pallas_call(
            body, name=name, grid=(nm,),
            in_specs=[row, pl.BlockSpec(b.shape, lambda i: (0, 0), pipeline_mode=pl.Buffered(1))] + [o_spec] * n_ex,
            out_specs=[o_spec] * n_out, out_shape=[_sds((m_dim, n_dim), dt) for dt in out_dtypes],
            compiler_params=_cp(dimension_semantics=("parallel",)),
        )(a, b, *extras)
        return outs[0] if n_out == 1 else tuple(outs)
    a_spec = pl.BlockSpec((tk, tm), lambda i, j, k: (k, i)) if ta else pl.BlockSpec((tm, tk), lambda i, j, k: (i, k))
    b_spec = pl.BlockSpec((tn, tk), lambda i, j, k: (j, k)) if tb else pl.BlockSpec((tk, tn), lambda i, j, k: (k, j))
    o_spec = pl.BlockSpec((tm, tn), lambda i, j, k: (i, j))
    outs = pl.pallas_call(
        body, name=name, grid=(nm, nn, nk), in_specs=[a_spec, b_spec] + [o_spec] * n_ex, out_specs=[o_spec] * n_out,
        out_shape=[_sds((m_dim, n_dim), dt) for dt in out_dtypes],
        scratch_shapes=[pltpu.VMEM((tm, tn), F32)] if nk > 1 else [],
        compiler_params=_cp(dimension_semantics=("parallel", "parallel", "arbitrary")),
    )(a, b, *extras)
    return outs[0] if n_out == 1 else tuple(outs)


def matmul_multi(a, bs_, out_dtypes, *, name):
    m, k = a.shape
    ns = [b.shape[1] for b in bs_]
    cnt = len(bs_)
    out_row_bytes = sum(n * jnp.dtype(dt).itemsize for n, dt in zip(ns, out_dtypes))
    w_bytes = sum(k * n * jnp.dtype(b.dtype).itemsize for n, b in zip(ns, bs_))
    tm = next(t for t in range(1088, 0, -16)
              if m % t == 0 and w_bytes + 2 * t * (k * jnp.dtype(a.dtype).itemsize + out_row_bytes)
              + t * max(ns) * 4 <= MM_VMEM_BUDGET - (8 << 20))

    def body(*refs):
        a_ref = refs[0]
        b_refs, o_refs = refs[1:1 + cnt], refs[1 + cnt:]
        av = a_ref[...].astype(BF16)
        for b_ref, o_ref, dt in zip(b_refs, o_refs, out_dtypes):
            o_ref[...] = _nn(av, b_ref[...].astype(BF16)).astype(dt)

    return pl.pallas_call(
        body, name=name, grid=(m // tm,),
        in_specs=[pl.BlockSpec((tm, k), lambda i: (i, 0))]
        + [pl.BlockSpec((k, n), lambda i: (0, 0), pipeline_mode=pl.Buffered(1)) for n in ns],
        out_specs=[pl.BlockSpec((tm, n), lambda i: (i, 0)) for n in ns],
        out_shape=[_sds((m, n), dt) for n, dt in zip(ns, out_dtypes)], compiler_params=_cp(),
    )(a, *bs_)


def matmul_nt_sum(as_, bs_, *, out_dtype=F32, name, tiles=None):
    m, n = as_[0].shape[0], bs_[0].shape[0]
    ks = [a.shape[1] for a in as_]
    assert [b.shape[1] for b in bs_] == ks
    ksum, cnt = sum(ks), len(ks)
    best = tiles
    for tn in [t for t in (1024, 512, 256, 128) if n % t == 0]:
        for tm in [t for t in range(1088, 0, -16) if m % t == 0]:
            need = 2 * (tm * ksum * 2 + tn * ksum * 2 + tm * tn * jnp.dtype(out_dtype).itemsize)
            if best is None and need <= MM_VMEM_BUDGET and tm >= 256:
                best = (tm, tn)
    tm, tn = best

    def body(*refs):
        a_refs, b_refs, o_ref = refs[:cnt], refs[cnt:2 * cnt], refs[2 * cnt]
        acc = None
        for a_ref, b_ref in zip(a_refs, b_refs):
            p = _nt(a_ref[...].astype(BF16), b_ref[...].astype(BF16))
            acc = p if acc is None else acc + p
        o_ref[...] = acc.astype(out_dtype)

    return pl.pallas_call(
        body, name=name, grid=(n // tn, m // tm),
        in_specs=[pl.BlockSpec((tm, k), lambda j, i: (i, 0)) for k in ks]
        + [pl.BlockSpec((tn, k), lambda j, i: (j, 0)) for k in ks],
        out_specs=pl.BlockSpec((tm, tn), lambda j, i: (i, j)), out_shape=_sds((m, n), out_dtype),
        compiler_params=_cp(dimension_semantics=("parallel", "parallel")),
    )(*as_, *bs_)


def rmsnorm_fwd(x, w, *, cw=None, ci=0, name):
    t = x.shape[0]
    cw = cw or x.shape[1]
    tr = _pick(t, 544, 16)

    def body(x_ref, w_ref, o_ref):
        xv = x_ref[...].astype(F32)
        r = lax.rsqrt(jnp.mean(xv * xv, axis=-1, keepdims=True) + EPS)
        o_ref[...] = (xv * r * w_ref[...]).astype(BF16)

    return pl.pallas_call(
        body, name=name, grid=(t // tr,),
        in_specs=[pl.BlockSpec((tr, cw), lambda i: (i, ci)), pl.BlockSpec((1, cw), lambda i: (0, 0))],
        out_specs=pl.BlockSpec((tr, cw), lambda i: (i, 0)),
        out_shape=_sds((t, cw), BF16), compiler_params=_cp(),
    )(x, w.reshape(1, cw))


def rmsnorm_bwd(dy, x, w, *, cw=None, ci=0, res=None, out_dtype=F32, with_bf16=False, name):
    t = x.shape[0]
    cw = cw or x.shape[1]
    tr = _pick(t, 544, 16)
    has_res = res is not None

    def body(*refs):
        dxb_ref = None
        if with_bf16:
            refs, dxb_ref = refs[:-1], refs[-1]
        if has_res:
            dy_ref, x_ref, w_ref, res_ref, dx_ref, dw_ref = refs
        else:
            dy_ref, x_ref, w_ref, dx_ref, dw_ref = refs
        xv = x_ref[...].astype(F32)
        dyv = dy_ref[...].astype(F32)
        r = lax.rsqrt(jnp.mean(xv * xv, axis=-1, keepdims=True) + EPS)
        xh = xv * r
        g = dyv * w_ref[...]
        dx = r * (g - xh * jnp.mean(g * xh, axis=-1, keepdims=True))
        if has_res:
            dx = dx + res_ref[...]
        dx_ref[...] = dx.astype(out_dtype)
        if with_bf16:
            dxb_ref[...] = dx.astype(BF16)

        @pl.when(pl.program_id(0) == 0)
        def _():
            dw_ref[...] = jnp.zeros_like(dw_ref)

        dw_ref[...] += jnp.sum(dyv * xh, axis=0, keepdims=True)

    row = pl.BlockSpec((tr, cw), lambda i: (i, 0))
    in_specs = [row, pl.BlockSpec((tr, cw), lambda i: (i, ci)), pl.BlockSpec((1, cw), lambda i: (0, 0))]
    args = [dy, x, w.reshape(1, cw)]
    if has_res:
        in_specs.append(row)
        args.append(res)
    outs = pl.pallas_call(
        body, name=name, grid=(t // tr,), in_specs=in_specs,
        out_specs=[row, pl.BlockSpec((1, cw), lambda i: (0, 0))] + ([row] if with_bf16 else []),
        out_shape=[_sds((t, cw), out_dtype), _sds((1, cw), F32)] + ([_sds((t, cw), BF16)] if with_bf16 else []),
        compiler_params=_cp(),
    )(*args)
    if with_bf16:
        return outs[0], outs[1][0], outs[2]
    return outs[0], outs[1][0]


def _shift_down(x, s):
    return x if s == 0 else pltpu.roll(x, s, 0)


def _shift_up(x, s):
    return x if s == 0 else pltpu.roll(x, x.shape[0] - s, 0)


def _conv_pre(x, w_ref, b_ref, kk):
    pre = b_ref[...] + jnp.zeros_like(x)
    for k in range(kk):
        pre = pre + w_ref[k:k + 1, :] * _shift_down(x, kk - 1 - k)
    return pre


def conv_fwd(cfg, xbc, w, b, *, name):
    lp, cd, kk = cfg.lp, cfg.conv_dim, cfg.convk
    assert cfg.pad >= kk - 1
    cb = _pick(cd, 512, LANE)

    def body(x_ref, w_ref, b_ref, o_ref, ds_ref):
        pre = _conv_pre(x_ref[...], w_ref, b_ref, kk)
        sg = jax.nn.sigmoid(pre)
        o_ref[...] = pre * sg
        ds_ref[...] = (sg * (1.0 + pre * (1.0 - sg))).astype(BF16)

    blk = pl.BlockSpec((lp, cb), lambda j, bb: (bb, j))
    return pl.pallas_call(
        body, name=name, grid=(cd // cb, cfg.bsz),
        in_specs=[blk, pl.BlockSpec((kk, cb), lambda j, bb: (0, j)), pl.BlockSpec((1, cb), lambda j, bb: (0, j))],
        out_specs=[blk, blk], out_shape=[_sds((cfg.t, cd), F32), _sds((cfg.t, cd), BF16)], compiler_params=_cp(),
    )(xbc, w, b.reshape(1, cd))


def conv_bwd(cfg, xbc, w, dsilu, dxc, *, name):
    lp, cd, kk = cfg.lp, cfg.conv_dim, cfg.convk
    cb = _pick(cd, 512, LANE)

    def body(x_ref, w_ref, s_ref, d_ref, dx_ref, dw_ref, db_ref):
        x = x_ref[...]
        dpre = d_ref[...] * s_ref[...].astype(F32)
        dx = jnp.zeros_like(x)
        dws = []
        for k in range(kk):
            s = kk - 1 - k
            dx = dx + w_ref[k:k + 1, :] * _shift_up(dpre, s)
            dws.append(jnp.sum(dpre * _shift_down(x, s), axis=0, keepdims=True))
        dx_ref[...] = dx.astype(BF16)

        @pl.when(pl.program_id(1) == 0)
        def _():
            dw_ref[...] = jnp.zeros_like(dw_ref)
            db_ref[...] = jnp.zeros_like(db_ref)

        for k in range(kk):
            dw_ref[k:k + 1, :] += dws[k]
        db_ref[...] += jnp.sum(dpre, axis=0, keepdims=True)

    blk = pl.BlockSpec((lp, cb), lambda j, bb: (bb, j))
    wsp = pl.BlockSpec((kk, cb), lambda j, bb: (0, j))
    bsp = pl.BlockSpec((1, cb), lambda j, bb: (0, j))
    dx, dw, db = pl.pallas_call(
        body, name=name, grid=(cd // cb, cfg.bsz),
        in_specs=[blk, wsp, blk, blk], out_specs=[blk, wsp, bsp],
        out_shape=[_sds((cfg.t, cd), BF16), _sds((kk, cd), F32), _sds((1, cd), F32)], compiler_params=_cp(),
    )(xbc, w, dsilu, dxc)
    return dx, dw, db[0]


def _softplus(x):
    return jnp.maximum(x, 0.0) + jnp.log(1.0 + jnp.exp(-jnp.abs(x)))


def _ssd_consts(cfg):
    q = cfg.chunk
    i0 = np.arange(q)[:, None]
    i1 = np.arange(q)[None, :]
    ltri = (i1 <= i0).astype(np.float32)
    rexp = np.zeros((LANE, cfg.inner), np.float32)
    for h in range(cfg.heads):
        rexp[h, h * cfg.hd:(h + 1) * cfg.hd] = 1.0
    return jnp.asarray(ltri), jnp.asarray(rexp)


def _sel_dot(x, m, *, passes=2, left=False, trans=False):
    mb = m.astype(BF16)
    acc, rem = None, x
    for _ in range(passes):
        piece = rem.astype(BF16)
        if not left:
            part = _nn(piece, mb)
        elif trans:
            part = _tn(mb, piece)
        else:
            part = _nn(mb, piece)
        acc = part if acc is None else acc + part
        rem = rem - piece.astype(F32)
    return acc


def _ssd_chunk_common(cfg, raw, bias, avec, c_idx, ltri, rexp):
    q = cfg.chunk
    rows = lax.broadcasted_iota(jnp.int32, (q, LANE), 0)
    live = jnp.logical_or(c_idx > 0, rows >= cfg.pad)
    pre = raw + bias
    dt = jnp.where(live, _softplus(pre), 0.0)
    adt = dt * avec
    cs = _sel_dot(adt, ltri, passes=3, left=True)
    cs_t = cs.T
    cs_last = cs[q - 1:q, :]
    e_in = jnp.exp(cs)
    w0 = jnp.exp(cs_last - cs)
    decay = jnp.exp(cs_last)
    return dict(live=live, pre=pre, dt=dt, adt=adt, cs=cs, cs_t=cs_t, e_in=e_in, w0=w0, decay=decay,
                DT=_sel_dot(dt, rexp), E=_sel_dot(e_in, rexp), W0=_sel_dot(w0, rexp),
                DEC=_sel_dot(jnp.broadcast_to(decay, (8, LANE)), rexp)[0:1, :])


def _tri_masks(q):
    r = lax.broadcasted_iota(jnp.int32, (q, q), 0)
    c = lax.broadcasted_iota(jnp.int32, (q, q), 1)
    return c <= r, r <= c


def _head_l(cq, h, tri, tri_t):
    col = cq["cs"][:, h:h + 1]
    row = cq["cs_t"][h:h + 1, :]
    lmat = jnp.where(tri, jnp.exp(jnp.minimum(col - row, 0.0)), 0.0)
    lmat_t = jnp.where(tri_t, jnp.exp(jnp.minimum(row - col, 0.0)), 0.0)
    return lmat, lmat_t


def _nt(a, b):
    return lax.dot_general(a, b, (((1,), (1,)), ((), ())), preferred_element_type=F32)


def _tn(a, b):
    return lax.dot_general(a, b, (((0,), (0,)), ((), ())), preferred_element_type=F32)


def _nn(a, b):
    return jnp.dot(a, b, preferred_element_type=F32)


def ssd_fwd(cfg, xc, small, dt_bias, avec, dexp, *, name):
    q, inner, st, gw, g_n = cfg.chunk, cfg.inner, cfg.state, cfg.gw, cfg.groups
    nc = cfg.nchunks
    ltri, rexp = _ssd_consts(cfg)
    hpt = LANE // cfg.hd
    tiles_per_group = gw // LANE

    bsz, lp = cfg.bsz, cfg.lp
    bcw = g_n * st

    def body(x_ref, b_ref, c_ref, dt_ref, bias_ref, a_ref, d_ref, ltri_ref, rexp_ref, y_ref, sin_ref, s_scr):
        c_idx = pl.program_id(0)

        @pl.when(c_idx == 0)
        def _():
            s_scr[...] = jnp.zeros_like(s_scr)

        ltri_v = ltri_ref[...]
        tri, tri_t = _tri_masks(q)
        lane = lax.broadcasted_iota(jnp.int32, (q, LANE), 1)
        for bi in range(bsz):
            cq = _ssd_chunk_common(cfg, dt_ref[bi], bias_ref[...], a_ref[...], c_idx, ltri_v, rexp_ref[...])
            xs = x_ref[bi]
            xdt = (xs * cq["DT"]).astype(BF16)
            xw = (xs * cq["DT"] * cq["W0"]).astype(BF16)
            s_in = s_scr[bi]
            sin_ref[bi, 0] = s_in
            for g in range(g_n):
                bg = b_ref[bi, :, g * st:(g + 1) * st].astype(BF16)
                cg = c_ref[bi, :, g * st:(g + 1) * st].astype(BF16)
                gmat = _nt(cg, bg)
                gs = slice(g * gw, (g + 1) * gw)
                y0 = _nn(cg, s_in[:, gs].astype(BF16))
                for tt in range(tiles_per_group):
                    tile = g * tiles_per_group + tt
                    ts = slice(tile * LANE, (tile + 1) * LANE)
                    xt = xdt[:, ts]
                    ms, xh = [], []
                    for hh in range(hpt):
                        lmat, _ = _head_l(cq, tile * hpt + hh, tri, tri_t)
                        ms.append((gmat * lmat).astype(BF16))
                        inhead = jnp.logical_and(lane >= hh * cfg.hd, lane < (hh + 1) * cfg.hd)
                        xh.append(jnp.where(inhead, xt, jnp.zeros_like(xt)))
                    yd = _nn(jnp.concatenate(ms, axis=1), jnp.concatenate(xh, axis=0))
                    y_ref[bi, :, ts] = (yd + y0[:, tt * LANE:(tt + 1) * LANE] * cq["E"][:, ts]
                                        + xs[:, ts] * d_ref[:, ts]).astype(BF16)
                s_scr[bi, :, gs] = s_in[:, gs] * cq["DEC"][:, gs] + _tn(bg, xw[:, gs])

    def rowblk(width, col):
        return pl.BlockSpec((bsz, q, width), lambda c: (0, c, col))

    def const(shape):
        return pl.BlockSpec(shape, lambda c: (0, 0))

    xc3 = xc.reshape(bsz, lp, cfg.conv_dim)
    y, sin = pl.pallas_call(
        body, name=name, grid=(nc,),
        in_specs=[rowblk(inner, 0), rowblk(bcw, inner // bcw), rowblk(bcw, inner // bcw + 1),
                  rowblk(LANE, cfg.dtt), const((1, LANE)), const((1, LANE)), const((1, inner)),
                  const((q, q)), const((LANE, inner))],
        out_specs=[rowblk(inner, 0), pl.BlockSpec((bsz, 1, st, inner), lambda c: (0, c, 0, 0))],
        out_shape=[_sds((bsz, lp, inner), BF16), _sds((bsz, nc, st, inner), F32)],
        scratch_shapes=[pltpu.VMEM((bsz, st, inner), F32)], compiler_params=_cp(),
    )(xc3, xc3, xc3, small.reshape(bsz, lp, cfg.sw), dt_bias, avec, dexp, ltri, rexp)
    return y.reshape(cfg.t, inner), sin.reshape(bsz * nc, st, inner)


def ssd_bwd(cfg, xc, small, dt_bias, avec, dexp, sin, dy, *, name):
    q, inner, st, gw, g_n = cfg.chunk, cfg.inner, cfg.state, cfg.gw, cfg.groups
    nc = cfg.nchunks
    ltri, rexp = _ssd_consts(cfg)
    rexp_t = rexp.T
    hpt = LANE // cfg.hd
    tiles_per_group = gw // LANE
    bcw = g_n * st

    def body(x_ref, b_ref, c_ref, dt_ref, bias_ref, a_ref, d_ref, ltri_ref, rexp_ref, rexpt_ref, sin_ref, dy_ref,
             dx_ref, ddt_ref, dd_ref, da_ref, dbias_ref, ds_scr):
        step = pl.program_id(1)
        c_idx = nc - 1 - step

        @pl.when(step == 0)
        def _():
            ds_scr[...] = jnp.zeros_like(ds_scr)

        @pl.when(jnp.logical_and(step == 0, pl.program_id(0) == 0))
        def _():
            dd_ref[...] = jnp.zeros_like(dd_ref)
            da_ref[...] = jnp.zeros_like(da_ref)
            dbias_ref[...] = jnp.zeros_like(dbias_ref)

        ltri_v = ltri_ref[...]
        tri, tri_t = _tri_masks(q)
        red = _sel_dot
        rexpt = rexpt_ref[...]
        cq = _ssd_chunk_common(cfg, dt_ref[...], bias_ref[...], a_ref[...], c_idx, ltri_v, rexp_ref[...])
        xs = x_ref[...]
        dyv = dy_ref[...].astype(F32)
        s_in = sin_ref[0]
        d_s = ds_scr[...]
        xdt_f = xs * cq["DT"]
        xdt = xdt_f.astype(BF16)
        xw_f = xdt_f * cq["W0"]
        xw = xw_f.astype(BF16)
        lane = lax.broadcasted_iota(jnp.int32, (q, LANE), 1)
        sub = lax.broadcasted_iota(jnp.int32, (LANE, q), 0)

        dd_ref[...] += jnp.sum(dyv * xs, axis=0, keepdims=True)
        dy0 = dyv * cq["E"]
        dcs = jnp.zeros((q, LANE), F32)
        dcs_t = jnp.zeros((LANE, q), F32)
        for g in range(g_n):
            bg_f = b_ref[:, g * st:(g + 1) * st]
            cg_f = c_ref[:, g * st:(g + 1) * st]
            bg = bg_f.astype(BF16)
            cg = cg_f.astype(BF16)
            gs = slice(g * gw, (g + 1) * gw)
            gmat = _nt(cg, bg)
            gmat_t = _nt(bg, cg)
            sing = s_in[:, gs].astype(BF16)
            dsg = d_s[:, gs].astype(BF16)
            y0 = _nn(cg, sing)
            dxw = _nn(bg, dsg)
            d_bg = _nt(xw[:, gs], dsg)
            d_cg = _nt(dy0[:, gs].astype(BF16), sing)
            ds_in_g = _tn(cg, dy0[:, gs].astype(BF16))
            dg = jnp.zeros((q, q), F32)
            dxdt_g = []
            for tt in range(tiles_per_group):
                tile = g * tiles_per_group + tt
                ts = slice(tile * LANE, (tile + 1) * LANE)
                xt = xdt[:, ts]
                dyt = dyv[:, ts]
                dyhs, lmats, mts = [], [], []
                for hh in range(hpt):
                    lmat, lmat_t = _head_l(cq, tile * hpt + hh, tri, tri_t)
                    inhead = jnp.logical_and(lane >= hh * cfg.hd, lane < (hh + 1) * cfg.hd)
                    dyhs.append(jnp.where(inhead, dyt, 0.0).astype(BF16))
                    lmats.append(lmat)
                    mts.append((gmat_t * lmat_t).astype(BF16))
                dy_stack = jnp.concatenate(dyhs, axis=0)
                dm_all = _nt(dy_stack, xt)
                for hh in range(hpt):
                    h = tile * hpt + hh
                    dm = dm_all[hh * q:(hh + 1) * q, :]
                    dg = dg + dm * lmats[hh]
                    qm = dm * gmat * lmats[hh]
                    rs = jnp.sum(qm, axis=1, keepdims=True)
                    csum = jnp.sum(qm, axis=0, keepdims=True)
                    dcs = dcs + jnp.where(lane == h, rs, 0.0)
                    dcs_t = dcs_t + jnp.where(sub == h, csum, 0.0)
                dxdt_g.append(_nn(jnp.concatenate(mts, axis=1), dy_stack))
            dxdt_diag = jnp.concatenate(dxdt_g, axis=1) if len(dxdt_g) > 1 else dxdt_g[0]
            dgb = dg.astype(BF16)
            d_cg = d_cg + _nn(dgb, bg)
            d_bg = d_bg + _tn(dgb, cg)
            dx_ref[:, inner + g * st:inner + (g + 1) * st] = d_bg
            dx_ref[:, inner + bcw + g * st:inner + bcw + (g + 1) * st] = d_cg
            dxdt = dxdt_diag + dxw * cq["W0"][:, gs]
            dx_ref[:, gs] = dyv[:, gs] * d_ref[:, gs] + dxdt * cq["DT"][:, gs]
            rt = rexpt[gs, :]
            dcs = dcs + red(dyv[:, gs] * y0 * cq["E"][:, gs], rt)
            r_w = red(dxw * xw_f[:, gs], rt)
            dcs = dcs - r_w
            dcs_last_g = jnp.sum(r_w, axis=0, keepdims=True)
            ddec = red(jnp.broadcast_to(jnp.sum(d_s[:, gs] * s_in[:, gs], axis=0, keepdims=True), (8, gw)), rt)[0:1, :]
            dcs_last_g = dcs_last_g + ddec * cq["decay"]
            dcs = dcs + jnp.where(lax.broadcasted_iota(jnp.int32, (q, LANE), 0) == q - 1, dcs_last_g, 0.0)
            ddt_part = red(dxdt * xs[:, gs], rt)
            if g == 0:
                ddt = ddt_part
            else:
                ddt = ddt + ddt_part
            ds_scr[:, gs] = d_s[:, gs] * cq["DEC"][:, gs] + ds_in_g
        dcs = dcs - dcs_t.T
        dadt = _sel_dot(dcs, ltri_v, left=True, trans=True)
        ddt = ddt + dadt * a_ref[...]
        da_ref[...] += jnp.sum(dadt * cq["dt"], axis=0, keepdims=True)
        draw = jnp.where(cq["live"], ddt * jax.nn.sigmoid(cq["pre"]), 0.0)
        ddt_ref[...] = draw
        dbias_ref[...] += jnp.sum(draw, axis=0, keepdims=True)

    def rowblk(width, col):
        return pl.BlockSpec((q, width), lambda b, s: (b * nc + nc - 1 - s, col))

    def const(shape):
        return pl.BlockSpec(shape, lambda b, s: (0, 0))

    bcol = inner // bcw
    outs = pl.pallas_call(
        body, name=name, grid=(cfg.bsz, nc),
        in_specs=[rowblk(inner, 0), rowblk(bcw, bcol), rowblk(bcw, bcol + 1), rowblk(LANE, cfg.dtt),
                  const((1, LANE)), const((1, LANE)), const((1, inner)), const((q, q)), const((LANE, inner)),
                  const((inner, LANE)),
                  pl.BlockSpec((1, st, inner), lambda b, s: (b * nc + nc - 1 - s, 0, 0)), rowblk(inner, 0)],
        out_specs=[rowblk(cfg.conv_dim, 0), rowblk(LANE, 0),
                   const((1, inner)), const((1, LANE)), const((1, LANE))],
        out_shape=[_sds((cfg.t, cfg.conv_dim), F32),
                   _sds((cfg.t, LANE), F32), _sds((1, inner), F32), _sds((1, LANE), F32), _sds((1, LANE), F32)],
        scratch_shapes=[pltpu.VMEM((st, inner), F32)], compiler_params=_cp(),
    )(xc, xc, xc, small, dt_bias, avec, dexp, ltri, rexp, rexp_t, sin, dy)
    return outs


def tail_fwd(cfg, y, z, w, *, name):
    t, inner, gw = cfg.t, cfg.inner, cfg.gw
    tr = _pick(t, 272, 16)

    def body(y_ref, z_ref, w_ref, o_ref):
        for g in range(cfg.groups):
            gs = slice(g * gw, (g + 1) * gw)
            yg = y_ref[:, gs].astype(F32) * _silu(z_ref[:, gs].astype(F32))
            r = lax.rsqrt(jnp.mean(yg * yg, axis=-1, keepdims=True) + EPS)
            o_ref[:, gs] = (yg * r * w_ref[:, gs]).astype(BF16)

    row = pl.BlockSpec((tr, inner), lambda i: (i, 0))
    return pl.pallas_call(
        body, name=name, grid=(t // tr,), in_specs=[row, row, pl.BlockSpec((1, inner), lambda i: (0, 0))],
        out_specs=row, out_shape=_sds((t, inner), BF16), compiler_params=_cp(),
    )(y, z, w.reshape(1, inner))


def tail_bwd(cfg, do, y, z, w, *, name):
    t, inner, gw = cfg.t, cfg.inner, cfg.gw
    tr = _pick(t, 272, 16)

    def body(do_ref, y_ref, z_ref, w_ref, dy_ref, dz_ref, dw_ref):
        @pl.when(pl.program_id(0) == 0)
        def _():
            dw_ref[...] = jnp.zeros_like(dw_ref)

        for g in range(cfg.groups):
            gs = slice(g * gw, (g + 1) * gw)
            yv = y_ref[:, gs].astype(F32)
            zv = z_ref[:, gs].astype(F32)
            dov = do_ref[:, gs].astype(F32)
            sz = _silu(zv)
            yg = yv * sz
            r = lax.rsqrt(jnp.mean(yg * yg, axis=-1, keepdims=True) + EPS)
            xh = yg * r
            gg = dov * w_ref[:, gs]
            dyg = r * (gg - xh * jnp.mean(gg * xh, axis=-1, keepdims=True))
            dw_ref[:, gs] += jnp.sum(dov * xh, axis=0, keepdims=True)
            dy_ref[:, gs] = (dyg * sz).astype(BF16)
            dz_ref[:, gs] = (dyg * yv * _dsilu(zv)).astype(BF16)

    row = pl.BlockSpec((tr, inner), lambda i: (i, 0))
    vec = pl.BlockSpec((1, inner), lambda i: (0, 0))
    dy, dz, dw = pl.pallas_call(
        body, name=name, grid=(t // tr,), in_specs=[row, row, row, vec], out_specs=[row, row, vec],
        out_shape=[_sds((t, inner), BF16), _sds((t, inner), BF16), _sds((1, inner), F32)], compiler_params=_cp(),
    )(do, y, z, w.reshape(1, inner))
    return dy, dz, dw[0]


def rope_tables(cfg):
    half = cfg.rope // 2
    pos = np.maximum(np.arange(cfg.lp) - cfg.pad, 0).astype(np.float32)
    inv = ROPE_THETA ** (-jnp.arange(0, cfg.rope, 2, dtype=F32) / cfg.rope)
    ang = jnp.asarray(pos)[:, None] * inv[None, :]
    cos, sin = jnp.cos(ang), jnp.sin(ang)
    zero = jnp.zeros((cfg.lp, LANE - 2 * half), F32)
    zh = jnp.zeros((cfg.lp, half), F32)
    ctab = jnp.concatenate([cos, cos, zero], axis=1)
    s1 = jnp.concatenate([-sin, zh, zero], axis=1)
    s2 = jnp.concatenate([zh, sin, zero], axis=1)
    return ctab, s1, s2


def _rope(x, c, s1, s2, half):
    return x * c + pltpu.roll(x, LANE - half, 1) * s1 + pltpu.roll(x, half, 1) * s2


def _rope_t(dy, c, s1, s2, half):
    return dy * c + pltpu.roll(dy * s1, half, 1) + pltpu.roll(dy * s2, LANE - half, 1)


def _attn_scale(cfg):
    return (cfg.nope + cfg.rope) ** -0.5


def rope_fwd(cfg, qf, small, tabs, *, name):
    t, qw, lp = cfg.t, cfg.qw, cfg.lp
    tr = _pick(lp, 544, 16)
    nrb = lp // tr
    half = cfg.rope // 2
    scale = _attn_scale(cfg)

    def body(q_ref, k_ref, c_ref, s1_ref, s2_ref, qo_ref, ko_ref):
        c, s1, s2 = c_ref[...], s1_ref[...], s2_ref[...]
        for h in range(cfg.mh):
            a = h * 2 * LANE
            qo_ref[:, a:a + LANE] = (q_ref[:, a:a + LANE].astype(F32) * scale).astype(BF16)
            qo_ref[:, a + LANE:a + 2 * LANE] = (
                _rope(q_ref[:, a + LANE:a + 2 * LANE].astype(F32), c, s1, s2, half) * scale).astype(BF16)
        ko_ref[...] = _rope(k_ref[...], c, s1, s2, half).astype(BF16)

    tab = pl.BlockSpec((tr, LANE), lambda i: (i % nrb, 0))
    return pl.pallas_call(
        body, name=name, grid=(t // tr,),
        in_specs=[pl.BlockSpec((tr, qw), lambda i: (i, 0)), pl.BlockSpec((tr, LANE), lambda i: (i, cfg.kt)), tab, tab, tab],
        out_specs=[pl.BlockSpec((tr, qw), lambda i: (i, 0)), pl.BlockSpec((tr, LANE), lambda i: (i, 0))],
        out_shape=[_sds((t, qw), BF16), _sds((t, LANE), BF16)], compiler_params=_cp(),
    )(qf, small, *tabs)


def rope_bwd(cfg, dq, dkpe, tabs, *, name):
    t, qw, lp = cfg.t, cfg.qw, cfg.lp
    tr = _pick(lp, 544, 16)
    nrb = lp // tr
    half = cfg.rope // 2
    scale = _attn_scale(cfg)

    def body(dq_ref, dk_ref, c_ref, s1_ref, s2_ref, qo_ref, ko_ref):
        c, s1, s2 = c_ref[...], s1_ref[...], s2_ref[...]
        for h in range(cfg.mh):
            a = h * 2 * LANE
            qo_ref[:, a:a + LANE] = (dq_ref[:, a:a + LANE].astype(F32) * scale).astype(BF16)
            qo_ref[:, a + LANE:a + 2 * LANE] = _rope_t(
                dq_ref[:, a + LANE:a + 2 * LANE].astype(F32) * scale, c, s1, s2, half).astype(BF16)
        ko_ref[...] = _rope_t(dk_ref[...], c, s1, s2, half)

    tab = pl.BlockSpec((tr, LANE), lambda i: (i % nrb, 0))
    return pl.pallas_call(
        body, name=name, grid=(t // tr,),
        in_specs=[pl.BlockSpec((tr, qw), lambda i: (i, 0)), pl.BlockSpec((tr, LANE), lambda i: (i, 0)),
                  tab, tab, tab],
        out_specs=[pl.BlockSpec((tr, qw), lambda i: (i, 0)), pl.BlockSpec((tr, LANE), lambda i: (i, 0))],
        out_shape=[_sds((t, qw), BF16), _sds((t, LANE), F32)], compiler_params=_cp(),
    )(dq, dkpe, *tabs)


def _q_blocks(cfg):
    bounds = [0, cfg.chunk] + list(range(cfg.chunk + 256, cfg.lp + 1, 256))
    assert bounds[-1] == cfg.lp, "SEQ must be a multiple of 256"
    return list(zip(bounds[:-1], bounds[1:]))


def _attn_mask(cfg, qs, qe):
    rows = qs + lax.broadcasted_iota(jnp.int32, (qe - qs, qe), 0)
    cols = lax.broadcasted_iota(jnp.int32, (qe - qs, qe), 1)
    return jnp.logical_and(cols <= rows, jnp.logical_or(cols >= cfg.pad, rows < cfg.pad))


def _max_q_block(cfg):
    return max(qe - qs for qs, qe in _q_blocks(cfg))


def _masked_scores(cfg, q, k2, qs, qe, s_scr):
    bq, n = qe - qs, qe
    s_scr[0:bq, 0:n] = _nt(q, k2)
    if qs == 0:
        s_scr[0:bq, 0:n] = jnp.where(_attn_mask(cfg, 0, qe), s_scr[0:bq, 0:n], MASK_VALUE)
    else:
        assert qs >= cfg.chunk and cfg.pad < LANE
        cols = lax.broadcasted_iota(jnp.int32, (bq, LANE), 1)
        s_scr[0:bq, 0:LANE] = jnp.where(cols >= cfg.pad, s_scr[0:bq, 0:LANE], MASK_VALUE)
        r = lax.broadcasted_iota(jnp.int32, (bq, bq), 0)
        c = lax.broadcasted_iota(jnp.int32, (bq, bq), 1)
        s_scr[0:bq, qs:qe] = jnp.where(c <= r, s_scr[0:bq, qs:qe], MASK_VALUE)
    return s_scr[0:bq, 0:n]


def attn_fwd(cfg, qr, kv, kpe, *, name):
    lp, t, mh = cfg.lp, cfg.t, cfg.mh
    blocks = _q_blocks(cfg)

    def body(q_ref, kv_ref, kp_ref, o_ref, l_ref, s_scr):
        for qs, qe in blocks:
            n = qe
            q = q_ref[qs:qe, :]
            k2 = jnp.concatenate([kv_ref[0:n, 0:LANE], kp_ref[0:n, :]], axis=1)
            s = _masked_scores(cfg, q, k2, qs, qe, s_scr)
            m = jnp.max(s, axis=-1, keepdims=True)
            p = jnp.exp(s - m)
            l = jnp.sum(p, axis=-1, keepdims=True)
            o_ref[qs:qe, :] = (_nn(p.astype(BF16), kv_ref[0:n, LANE:2 * LANE]) * (1.0 / l)).astype(BF16)
            l_ref[qs:qe, :] = jnp.broadcast_to(m + jnp.log(l), (qe - qs, LANE))

    hb = pl.BlockSpec((lp, 2 * LANE), lambda b, h: (b, h))
    ob = pl.BlockSpec((lp, LANE), lambda b, h: (b, h))
    return pl.pallas_call(
        body, name=name, grid=(cfg.bsz, mh),
        in_specs=[hb, hb, pl.BlockSpec((lp, LANE), lambda b, h: (b, 0))], out_specs=[ob, ob],
        out_shape=[_sds((t, mh * LANE), BF16), _sds((t, mh * LANE), F32)],
        scratch_shapes=[pltpu.VMEM((_max_q_block(cfg), lp), F32)], compiler_params=_cp(),
    )(qr, kv, kpe)


def attn_bwd(cfg, qr, kv, kpe, o, lse, do, *, name):
    lp, t, mh = cfg.lp, cfg.t, cfg.mh
    blocks = _q_blocks(cfg)

    def body(q_ref, kv_ref, kp_ref, o_ref, l_ref, do_ref, dq_ref, dkv_ref, dkp_ref, dk_acc, dv_acc, s_scr):
        dk_acc[...] = jnp.zeros_like(dk_acc)
        dv_acc[...] = jnp.zeros_like(dv_acc)
        for qs, qe in blocks:
            n = qe
            q = q_ref[qs:qe, :]
            k2 = jnp.concatenate([kv_ref[0:n, 0:LANE], kp_ref[0:n, :]], axis=1)
            dob = do_ref[qs:qe, :].astype(BF16)
            delta = jnp.sum(dob.astype(F32) * o_ref[qs:qe, :].astype(F32), axis=-1, keepdims=True)
            s = _masked_scores(cfg, q, k2, qs, qe, s_scr)
            p = jnp.exp(s - l_ref[qs:qe, 0:1])
            dp = _nt(dob, kv_ref[0:n, LANE:2 * LANE])
            ds = (p * (dp - delta)).astype(BF16)
            dq_ref[qs:qe, :] = _nn(ds, k2).astype(BF16)
            dv_acc[0:n, :] += _tn(p.astype(BF16), dob)
            dk_acc[0:n, :] += _tn(ds, q)
        dkv_ref[:, 0:LANE] = dk_acc[:, 0:LANE].astype(BF16)
        dkv_ref[:, LANE:2 * LANE] = dv_acc[...].astype(BF16)
        @pl.when(pl.program_id(1) == 0)
        def _():
            dkp_ref[...] = dk_acc[:, LANE:2 * LANE]

        @pl.when(pl.program_id(1) > 0)
        def _():
            dkp_ref[...] += dk_acc[:, LANE:2 * LANE]

    hb = pl.BlockSpec((lp, 2 * LANE), lambda b, h: (b, h))
    ob = pl.BlockSpec((lp, LANE), lambda b, h: (b, h))
    return pl.pallas_call(
        body, name=name, grid=(cfg.bsz, mh),
        in_specs=[hb, hb, pl.BlockSpec((lp, LANE), lambda b, h: (b, 0)), ob, ob, ob],
        out_specs=[hb, hb, pl.BlockSpec((lp, LANE), lambda b, h: (b, 0))],
        out_shape=[_sds((t, cfg.qw), BF16), _sds((t, mh * 2 * LANE), BF16), _sds((t, LANE), F32)],
        scratch_shapes=[pltpu.VMEM((lp, 2 * LANE), F32), pltpu.VMEM((lp, LANE), F32),
                        pltpu.VMEM((_max_q_block(cfg), lp), F32)], compiler_params=_cp(),
    )(qr, kv, kpe, o, lse, do)


def _live_rows(cfg, tr, shape):
    rows = pl.program_id(1) * tr + lax.broadcasted_iota(jnp.int32, shape, 0)
    return rows >= cfg.pad


def gate_fwd(cfg, ya, yb, g, *, name):
    d, lp = cfg.d, cfg.lp
    tr = _pick(lp, 544, 16)
    nrb = lp // tr

    def body(ya_ref, yb_ref, ga_ref, gb_ref, o_ref):
        f = lambda ref: ref[...].astype(F32)
        mix = jax.nn.sigmoid(f(ga_ref)) * f(ya_ref) + jax.nn.sigmoid(f(gb_ref)) * f(yb_ref)
        o_ref[...] = jnp.where(_live_rows(cfg, tr, mix.shape), mix, 0.0).astype(BF16)

    row = pl.BlockSpec((tr, d), lambda b, j: (b * nrb + j, 0))
    row1 = pl.BlockSpec((tr, d), lambda b, j: (b * nrb + j, 1))
    return pl.pallas_call(
        body, name=name, grid=(cfg.bsz, nrb), in_specs=[row, row, row, row1], out_specs=row,
        out_shape=_sds((cfg.t, d), BF16), compiler_params=_cp(),
    )(ya, yb, g, g)


def gate_bwd(cfg, dmix, ya, yb, g, *, name):
    d, lp = cfg.d, cfg.lp
    tr = _pick(lp, 544, 16)
    nrb = lp // tr

    def body(dm_ref, ya_ref, yb_ref, ga_ref, gb_ref, dya_ref, dyb_ref, dg_ref):
        dm = dm_ref[...].astype(F32)
        dm = jnp.where(_live_rows(cfg, tr, dm.shape), dm, 0.0)
        sa = jax.nn.sigmoid(ga_ref[...].astype(F32))
        sb = jax.nn.sigmoid(gb_ref[...].astype(F32))
        dya_ref[...] = (dm * sa).astype(BF16)
        dyb_ref[...] = (dm * sb).astype(BF16)
        dg_ref[:, 0:d] = (dm * ya_ref[...].astype(F32) * sa * (1.0 - sa)).astype(BF16)
        dg_ref[:, d:2 * d] = (dm * yb_ref[...].astype(F32) * sb * (1.0 - sb)).astype(BF16)

    row = pl.BlockSpec((tr, d), lambda b, j: (b * nrb + j, 0))
    row1 = pl.BlockSpec((tr, d), lambda b, j: (b * nrb + j, 1))
    row2 = pl.BlockSpec((tr, 2 * d), lambda b, j: (b * nrb + j, 0))
    return pl.pallas_call(
        body, name=name, grid=(cfg.bsz, nrb), in_specs=[row, row, row, row, row1], out_specs=[row, row, row2],
        out_shape=[_sds((cfg.t, d), BF16), _sds((cfg.t, d), BF16), _sds((cfg.t, 2 * d), BF16)], compiler_params=_cp(),
    )(dmix, ya, yb, g, g)


def loss_head(cfg, h, target, w, *, name):
    d, q, nc = cfg.d, cfg.chunk, cfg.nchunks
    tpb = cfg.seq // q

    def body(h_ref, t_ref, w_ref, loss_ref, dh_ref, dw_ref, dhb_ref):
        j = pl.program_id(1)

        @pl.when(jnp.logical_and(j == 0, pl.program_id(0) == 0))
        def _():
            loss_ref[...] = jnp.zeros_like(loss_ref)
            dw_ref[...] = jnp.zeros_like(dw_ref)

        @pl.when(j == 0)
        def _():
            dh_ref[...] = jnp.zeros_like(dh_ref)
            dhb_ref[...] = jnp.zeros_like(dhb_ref)

        @pl.when(j > 0)
        def _():
            xv = h_ref[...]
            r = lax.rsqrt(jnp.mean(xv * xv, axis=-1, keepdims=True) + EPS)
            xh = xv * r
            err = xh * w_ref[...] - t_ref[...]
            loss_ref[...] += 0.5 * jnp.sum(jnp.sum(err * err, axis=-1, keepdims=True) / d, axis=0, keepdims=True)
            dy = err * (1.0 / d)
            g = dy * w_ref[...]
            dh = r * (g - xh * jnp.mean(g * xh, axis=-1, keepdims=True))
            dh_ref[...] = dh
            dhb_ref[...] = dh.astype(BF16)
            dw_ref[...] += jnp.sum(dy * xh, axis=0, keepdims=True)

    row = pl.BlockSpec((q, d), lambda b, j: (b * nc + j, 0))
    loss, dh, dw, dhb = pl.pallas_call(
        body, name=name, grid=(cfg.bsz, nc),
        in_specs=[row, pl.BlockSpec((q, d), lambda b, j: (b * tpb + jnp.maximum(j - 1, 0), 0)),
                  pl.BlockSpec((1, d), lambda b, j: (0, 0))],
        out_specs=[pl.BlockSpec((8, LANE), lambda b, j: (0, 0)), row, pl.BlockSpec((1, d), lambda b, j: (0, 0)), row],
        out_shape=[_sds((8, LANE), F32), _sds((cfg.t, d), F32), _sds((1, d), F32), _sds((cfg.t, d), BF16)],
        compiler_params=_cp(),
    )(h, target, w.reshape(1, d))
    return loss[0, 0], (dh, dhb), dw[0]


def _rows_tile(r, c):
    return _pick(r, max(8, (1 << 18) // max(c, 1) // 8 * 8), 8)


def _adam_update(w, g, m, v):
    c1 = 1.0 - ADAM_B1 ** ADAM_STEP
    c2 = 1.0 - ADAM_B2 ** ADAM_STEP
    mn = ADAM_B1 * m + (1.0 - ADAM_B1) * g
    vn = ADAM_B2 * v + (1.0 - ADAM_B2) * (g * g)
    delta = -ADAM_LR * ((mn / c1) / (jnp.sqrt(vn / c2) + ADAM_EPS) + ADAM_WD * w)
    return delta, mn, vn


def adamw_layer(w, m, v, g, li, prev, dep, *, name):
    _, r, c = w.shape
    tr = _rows_tile(r, c)

    def body(*refs):
        w_ref, m_ref, v_ref, g_ref = refs[:4]
        go_ref, d_ref, mo_ref, vo_ref = refs[-4:]
        gv = g_ref[...]
        delta, mn, vn = _adam_update(w_ref[0], gv, m_ref[0], v_ref[0])
        go_ref[0] = gv
        d_ref[0] = delta
        mo_ref[0] = mn
        vo_ref[0] = vn

    if tr * c * 4 >= (1 << 16):
        steps = r // tr
        blk3 = pl.BlockSpec((1, tr, c), lambda i: (li, i, 0))
        blk2 = pl.BlockSpec((tr, c), lambda i: (i, 0))
    else:
        tc = _pick(c, max(LANE, (1 << 18) // r // LANE * LANE), LANE)
        steps = c // tc
        blk3 = pl.BlockSpec((1, r, tc), lambda i: (li, 0, i))
        blk2 = pl.BlockSpec((r, tc), lambda i: (0, i))
    anyspec = pl.BlockSpec(memory_space=pl.ANY)
    in_specs = [blk3, blk3, blk3, blk2, anyspec]
    args = [w, m, v, g, dep]
    aliases = {}
    if prev is not None:
        in_specs += [anyspec] * 4
        args += list(prev)
        aliases = {5 + i: i for i in range(4)}
    return pl.pallas_call(
        body, name=name, grid=(steps,), in_specs=in_specs, out_specs=[blk3] * 4,
        out_shape=[_sds(w.shape, F32)] * 4, input_output_aliases=aliases, compiler_params=_cp(),
    )(*args)


def pair_add(g4, other, half, *, name):
    n, _, r, c = g4.shape
    tr = _rows_tile(r, c)

    def body(h_ref, a_ref, b_ref, o_ref):
        o_ref[0] = (a_ref[0, 0].astype(F32) + b_ref[0].astype(F32)).astype(BF16)

    blk = pl.BlockSpec((1, tr, c), lambda j, i, h: (j, i, 0))
    grid_spec = pltpu.PrefetchScalarGridSpec(
        num_scalar_prefetch=1, grid=(n, r // tr),
        in_specs=[pl.BlockSpec((1, 1, tr, c), lambda j, i, h: (j, h[0], i, 0)), blk], out_specs=blk)
    return pl.pallas_call(body, name=name, grid_spec=grid_spec, out_shape=_sds((n, r, c), BF16),
                          compiler_params=_cp())(half, g4, other)


def chip_sum(recv, part, where, *, name):
    n, r, c = recv.shape
    tr = _rows_tile(r, c)

    def body(s_ref, *refs):
        own_ref, o_ref = refs[n], refs[n + 1]
        acc = None
        for j in range(n):
            term = jnp.where(s_ref[0] == j, own_ref[0], refs[j][0]).astype(F32)
            acc = term if acc is None else acc + term
        o_ref[0] = acc

    def slot(j):
        return pl.BlockSpec((1, tr, c), lambda i, s: (jnp.where(s[0] == j, (j + 1) % n, j), i, 0))

    grid_spec = pltpu.PrefetchScalarGridSpec(
        num_scalar_prefetch=1, grid=(r // tr,),
        in_specs=[slot(j) for j in range(n)] + [pl.BlockSpec((1, tr, c), lambda i, s: (s[0], i, 0))],
        out_specs=pl.BlockSpec((1, tr, c), lambda i, s: (s[1], i, 0)))
    return pl.pallas_call(body, name=name, grid_spec=grid_spec, out_shape=_sds((2, r, c), F32),
                          compiler_params=_cp())(where, *([recv] * n), part)


def _coords():
    return lax.axis_index("x"), lax.axis_index("y"), lax.axis_index("c")


def _other_chips(x, y):
    return [(1 - x, y), (x, 1 - y), (1 - x, 1 - y)]


def gather_chips(arrs, *, name):
    n = len(arrs)
    anyspec = pl.BlockSpec(memory_space=pl.ANY)

    def body(*refs):
        ins, outs = refs[:n], refs[n:2 * n]
        send_sems, recv_sems, local_sems = refs[2 * n:]
        x, y, c = _coords()
        me = 2 * x + y
        chips = _other_chips(x, y)
        copies = []
        for k in range(n):
            loc = pltpu.make_async_copy(ins[k], outs[k].at[me], local_sems.at[k])
            loc.start()
            copies.append(loc)
        sends = []
        for k in range(n):
            for j, (px, py) in enumerate(chips):
                cp = pltpu.make_async_remote_copy(
                    src_ref=ins[k], dst_ref=outs[k].at[me], send_sem=send_sems.at[k, j], recv_sem=recv_sems.at[k, j],
                    device_id=(px, py, c), device_id_type=MESH)
                cp.start()
                sends.append(cp)
        for k in range(n):
            for j, (px, py) in enumerate(chips):
                pltpu.make_async_remote_copy(
                    src_ref=ins[k], dst_ref=outs[k].at[2 * px + py], send_sem=send_sems.at[k, j],
                    recv_sem=recv_sems.at[k, j], device_id=(px, py, c), device_id_type=MESH).wait_recv()
        for cp in sends:
            cp.wait_send()
        for cp in copies:
            cp.wait()

    return pl.pallas_call(
        body, name=name, in_specs=[anyspec] * n, out_specs=[anyspec] * n,
        out_shape=[_sds((4,) + a.shape, a.dtype) for a in arrs],
        scratch_shapes=[pltpu.SemaphoreType.DMA((n, 3)), pltpu.SemaphoreType.DMA((n, 3)), pltpu.SemaphoreType.DMA((n,))],
        compiler_params=_cp(has_side_effects=True),
    )(*arrs)


def allreduce_small(vec, after, *, name):
    r, c = vec.shape

    def body(v_ref, after_ref, o_ref, buf, send_sems, recv_sems):
        x, y, cc = _coords()
        me = 4 * x + 2 * y + cc
        buf[me] = v_ref[...]
        sends = []
        flips = [(fx, fy, fc) for fx in (0, 1) for fy in (0, 1) for fc in (0, 1)][1:]
        for j, (fx, fy, fc) in enumerate(flips):
            peer = ((1 - x) if fx else x, (1 - y) if fy else y, (1 - cc) if fc else cc)
            cp = pltpu.make_async_remote_copy(
                src_ref=v_ref, dst_ref=buf.at[me], send_sem=send_sems.at[j], recv_sem=recv_sems.at[j],
                device_id=peer, device_id_type=MESH)
            cp.start()
            sends.append(cp)
        for j, (fx, fy, fc) in enumerate(flips):
            px, py, pc = ((1 - x) if fx else x, (1 - y) if fy else y, (1 - cc) if fc else cc)
            pltpu.make_async_remote_copy(
                src_ref=v_ref, dst_ref=buf.at[4 * px + 2 * py + pc], send_sem=send_sems.at[j],
                recv_sem=recv_sems.at[j], device_id=(px, py, pc), device_id_type=MESH).wait_recv()
        for cp in sends:
            cp.wait_send()
        acc = buf[0]
        for k in range(1, 8):
            acc = acc + buf[k]
        o_ref[...] = acc

    vm = pl.BlockSpec(memory_space=pltpu.VMEM)
    return pl.pallas_call(
        body, name=name, in_specs=[vm, pl.BlockSpec(memory_space=pl.ANY)], out_specs=vm, out_shape=_sds((r, c), F32),
        scratch_shapes=[pltpu.VMEM((8, r, c), F32), pltpu.SemaphoreType.DMA((7,)), pltpu.SemaphoreType.DMA((7,))],
        compiler_params=_cp(has_side_effects=True),
    )(vec, after)


def pair_share(lands, owns, *, name):
    n = len(lands)
    anyspec = pl.BlockSpec(memory_space=pl.ANY)

    def body(*refs):
        ins, own_refs, outs = refs[:n], refs[n:2 * n], refs[2 * n:3 * n]
        send_sems, recv_sems = refs[3 * n:]
        x, y, c = _coords()
        me = 2 * x + y
        sib = (x, y, 1 - c)
        sends = []
        for k in range(n):
            for j, (px, py) in enumerate(_other_chips(x, y)):
                cp = pltpu.make_async_remote_copy(
                    src_ref=ins[k].at[2 * px + py, c], dst_ref=outs[k].at[2 * px + py, c], send_sem=send_sems.at[k, j],
                    recv_sem=recv_sems.at[k, j], device_id=sib, device_id_type=MESH)
                cp.start()
                sends.append(cp)
            cp = pltpu.make_async_remote_copy(
                src_ref=own_refs[k], dst_ref=outs[k].at[me], send_sem=send_sems.at[k, 3], recv_sem=recv_sems.at[k, 3],
                device_id=sib, device_id_type=MESH)
            cp.start()
            sends.append(cp)
        for k in range(n):
            for j, (px, py) in enumerate(_other_chips(x, y)):
                pltpu.make_async_remote_copy(
                    src_ref=ins[k].at[2 * px + py, c], dst_ref=outs[k].at[2 * px + py, 1 - c],
                    send_sem=send_sems.at[k, j], recv_sem=recv_sems.at[k, j], device_id=sib,
                    device_id_type=MESH).wait_recv()
            pltpu.make_async_remote_copy(
                src_ref=own_refs[k], dst_ref=outs[k].at[me], send_sem=send_sems.at[k, 3], recv_sem=recv_sems.at[k, 3],
                device_id=sib, device_id_type=MESH).wait_recv()
        for cp in sends:
            cp.wait_send()

    return pl.pallas_call(
        body, name=name, in_specs=[anyspec] * (2 * n), out_specs=[anyspec] * n,
        out_shape=[_sds(a.shape, a.dtype) for a in lands], input_output_aliases={k: k for k in range(n)},
        scratch_shapes=[pltpu.SemaphoreType.DMA((n, 4)), pltpu.SemaphoreType.DMA((n, 4))],
        compiler_params=_cp(has_side_effects=True),
    )(*lands, *owns)


_HBM = pl.BlockSpec(memory_space=pltpu.HBM)
_SEM = pl.BlockSpec(memory_space=pltpu.SEMAPHORE)


_COPIES_PER_ARRAY = {"gather": 3, "scatter": 3, "share": 4, "exchange": 4, "fill": 1}


def _ici_copies(kind, srcs, lands, send_sems, recv_sems):
    x, y, c = _coords()
    me = 2 * x + y
    per = _COPIES_PER_ARRAY[kind]
    sends, recvs = [], []
    for k in range(len(srcs)):
        triples = []
        for j, (px, py) in enumerate(_other_chips(x, y)):
            peer = 2 * px + py
            if kind == "gather":
                triples.append((srcs[k].at[c], lands[k].at[me, c], lands[k].at[peer, c], (px, py, c)))
            elif kind == "scatter":
                triples.append((srcs[k].at[peer], lands[k].at[me], lands[k].at[peer], (px, py, c)))
            elif kind == "share":
                triples.append((lands[k].at[peer, c], lands[k].at[peer, c], lands[k].at[peer, 1 - c], (x, y, 1 - c)))
        if kind == "share":
            triples.append((srcs[k], lands[k].at[me], lands[k].at[me], (x, y, 1 - c)))
        if kind == "exchange":
            triples = [(srcs[k].at[j, 1 - c], lands[k].at[j], lands[k].at[j], (x, y, 1 - c)) for j in range(4)]
        if kind == "fill":
            triples = [(lands[k].at[c], lands[k].at[c], lands[k].at[1 - c], (x, y, 1 - c))]
        for j, (src, there, here, dev) in enumerate(triples):
            sem = per * k + j
            mk = functools.partial(pltpu.make_async_remote_copy, src_ref=src, send_sem=send_sems.at[sem],
                                   recv_sem=recv_sems.at[sem], device_id=dev, device_id_type=MESH)
            sends.append(mk(dst_ref=there))
            recvs.append(mk(dst_ref=here))
    return sends, recvs


def ici_start(kind, srcs, lands, after, *, name):
    n = len(srcs)

    def body(*refs):
        src_refs, land_refs = refs[:n], refs[n:2 * n]
        send_sems, recv_sems = refs[2 * n + 1], refs[2 * n + 2]
        token = refs[-1]
        sends, _ = _ici_copies(kind, src_refs, land_refs, send_sems, recv_sems)
        for cp in sends:
            cp.start()
        token[...] = jnp.zeros_like(token)

    both = list(srcs) + list(lands)
    out = pl.pallas_call(
        body, name=name,
        in_specs=[_HBM] * (2 * n) + [pl.BlockSpec(memory_space=pl.ANY)],
        out_shape=(pltpu.SemaphoreType.DMA((_COPIES_PER_ARRAY[kind] * n,)),
                   pltpu.SemaphoreType.DMA((_COPIES_PER_ARRAY[kind] * n,)),
                   *[pltpu.HBM(a.shape, a.dtype) for a in both], _sds((8, LANE), F32)),
        out_specs=(_SEM, _SEM, *([_HBM] * (2 * n)), pl.BlockSpec(memory_space=pltpu.VMEM)),
        input_output_aliases={i: 2 + i for i in range(2 * n)},
        compiler_params=_cp(has_side_effects=pltpu.SideEffectType.DATAFLOW_SIDE_EFFECTING),
    )(*[pltpu.with_memory_space_constraint(a, pltpu.HBM) for a in both], after)
    return out[0], out[1], list(out[2:2 + n]), list(out[2 + n:2 + 2 * n]), out[-1]


def ici_wait(kind, started, after, *, name):
    send_sems, recv_sems, srcs, lands, _ = started
    n = len(srcs)

    def body(*refs):
        src_refs, land_refs = refs[:n], refs[n:2 * n]
        sends, recvs = _ici_copies(kind, src_refs, land_refs, refs[2 * n], refs[2 * n + 1])
        for cp in sends:
            cp.wait_send()
        for cp in recvs:
            cp.wait_recv()

    both = list(srcs) + list(lands)
    out = pl.pallas_call(
        body, name=name,
        in_specs=[_HBM] * (2 * n) + [_SEM, _SEM, pl.BlockSpec(memory_space=pl.ANY)],
        out_shape=tuple(pltpu.HBM(a.shape, a.dtype) for a in both), out_specs=tuple([_HBM] * (2 * n)),
        input_output_aliases={i: i for i in range(2 * n)},
        compiler_params=_cp(has_side_effects=pltpu.SideEffectType.DATAFLOW_SIDE_EFFECTING),
    )(*both, send_sems, recv_sems, after)
    return list(out[:n]), list(out[n:])


BIG = ["w_in", "w_uq", "w_ukv", "w_branch_ssm", "w_branch_mla", "w_out", "w_mlp_up", "w_mlp_down"]
COL_SHARDED = {"w_in", "w_uq", "w_ukv", "w_mlp_up"}
SMALL_REPL = ["norm_mix_w", "conv_b", "dt_bias", "a_log", "d_skip", "ssm_norm_w", "q_norm_w", "kv_norm_w", "norm_mlp_w"]


def _unshard_layer(name, g):
    _, r, c = g.shape
    if name in COL_SHARDED:
        return jnp.transpose(g, (1, 0, 2)).reshape(r, 4 * c)
    return g.reshape(4 * r, c)


def _to_shards(name, full):
    r, c = full.shape
    if name in COL_SHARDED:
        return jnp.transpose(full.reshape(r, 4, c // 4), (1, 0, 2))
    return full.reshape(4, r // 4, c)


REST = [k for k in BIG if k != "w_in"]


def prep_layer(cfg, w):
    out = {}
    if "w_in" in w:
        sp = np.cumsum(cfg.in_splits)[:-1].tolist()
        z, xbc, dt, cq, ckv, kr, gs, gm = jnp.split(w["w_in"], sp, axis=1)
        zpad = lambda n: jnp.zeros((cfg.d, n), z.dtype)
        out.update(w_z=z, w_xbc=xbc, w_g=jnp.concatenate([gs, gm], axis=1),
                   w_s=jnp.concatenate([cq, ckv, kr, zpad(LANE - cfg.rope), dt, zpad(LANE - cfg.heads)], axis=1))
    if "w_uq" in w:
        out.update(
            w_uq=jnp.pad(w["w_uq"].reshape(cfg.ql, cfg.mh, cfg.nope + cfg.rope),
                         ((0, 0), (0, 0), (0, 2 * LANE - cfg.nope - cfg.rope))).reshape(cfg.ql, cfg.qw),
            w_ukv=w["w_ukv"], w_bs=w["w_branch_ssm"], w_bm=w["w_branch_mla"], w_out=w["w_out"],
            w_up=w["w_mlp_up"], w_down=w["w_mlp_down"])
    return {k: v.astype(BF16) for k, v in out.items()}


def unprep_grads(cfg, g):
    out = {}
    if "w_s" in g:
        ql, kvl = cfg.ql, cfg.kvl
        ds_ = g["w_s"]
        cq, ckv = ds_[:, :ql], ds_[:, ql:ql + kvl]
        kr = ds_[:, ql + kvl:ql + kvl + cfg.rope]
        dt = ds_[:, ql + kvl + LANE:ql + kvl + LANE + cfg.heads]
        out["w_in"] = jnp.concatenate([g["w_z"], g["w_xbc"], dt, cq, ckv, kr, g["w_g"]], axis=1)
    if "w_uq" in g:
        out.update(
            w_uq=g["w_uq"].reshape(cfg.ql, cfg.mh, 2 * LANE)[:, :, :cfg.nope + cfg.rope].reshape(cfg.ql, -1),
            w_ukv=g["w_ukv"], w_branch_ssm=g["w_bs"], w_branch_mla=g["w_bm"],
            w_out=g["w_out"], w_mlp_up=g["w_up"], w_mlp_down=g["w_down"])
    return out


def _hook(hooks, name, arg):
    if hooks and name in hooks:
        return hooks[name](arg)[0, 0]
    return 0.0


def layer_fwd(cfg, h, pw, sm, tabs, li, hooks=None):
    n = lambda s: f"l{li}_{s}"
    u = rmsnorm_fwd(h, sm["norm_mix_w"], name=n("norm_mix"))
    z, xbc, g, small = matmul_multi(u, [pw["w_z"], pw["w_xbc"], pw["w_g"], pw["w_s"]], (BF16, F32, BF16, F32),
                                    name=n("in_proj"))
    xc, dsilu = conv_fwd(cfg, xbc, sm["conv_w"], sm["conv_b"], name=n("conv"))
    dt_bias = sm["dt_bias_p"] + _hook(hooks, "after_conv", xc)
    y, sin = ssd_fwd(cfg, xc, small, dt_bias, sm["avec"], sm["dexp"], name=n("ssd"))
    y_ssm = tail_fwd(cfg, y, z, sm["ssm_norm_w"], name=n("tail"))
    if hooks and "weights" in hooks:
        pw = dict(pw, **hooks["weights"](y_ssm))
    cqn = rmsnorm_fwd(small, sm["q_norm_w"], cw=cfg.ql, ci=0, name=n("q_norm"))
    ckvn = rmsnorm_fwd(small, sm["kv_norm_w"], cw=cfg.kvl, ci=cfg.ql // cfg.kvl, name=n("kv_norm"))
    qf = matmul(cqn, pw["w_uq"], out_dtype=BF16, name=n("uq"))
    kv = matmul(ckvn, pw["w_ukv"], out_dtype=BF16, name=n("ukv"))
    qr, kpe = rope_fwd(cfg, qf, small, tabs, name=n("rope"))
    o, lse = attn_fwd(cfg, qr, kv, kpe, name=n("attn"))
    ya = matmul(y_ssm, pw["w_bs"], out_dtype=BF16, name=n("branch_ssm"))
    yb = matmul(o, pw["w_bm"], out_dtype=BF16, name=n("branch_mla"))
    mixed = gate_fwd(cfg, ya, yb, g, name=n("gate"))
    h1 = matmul(mixed, pw["w_out"], add=h, name=n("out"))
    v = rmsnorm_fwd(h1, sm["norm_mlp_w"] + _hook(hooks, "after_attn", o), name=n("norm_mlp"))
    a, act = matmul(v, pw["w_up"], name=n("up"), epilogue=_ep_relu2, out_dtypes=(BF16, BF16))
    h2 = matmul(act, pw["w_down"], add=h1, name=n("down"))
    saved = dict(h=h, u=u, z=z, xbc=xbc, g=g, small=small, xc=xc, dsilu=dsilu, y=y, sin=sin, y_ssm=y_ssm, cqn=cqn, ckvn=ckvn,
                 qr=qr, kv=kv, kpe=kpe, o=o, lse=lse, ya=ya, yb=yb, mixed=mixed, h1=h1, v=v, a=a, act=act)
    return h2, saved, pw


def layer_bwd(cfg, dh2, pw, sm, tabs, s, li, hooks=None):
    n = lambda t: f"l{li}_b_{t}"
    gw, gs = {}, {}
    wgrad = functools.partial(matmul, ta=True, out_dtype=BF16)
    dh2, dh2b = dh2
    gw["w_down"] = wgrad(s["act"], dh2b, name=n("dw_down"))
    da = matmul(dh2b, pw["w_down"], tb=True, name=n("dact"), epilogue=_ep_relu2_grad, extras=(s["a"],),
                out_dtypes=(BF16,))
    gw["w_up"] = wgrad(s["v"], da, name=n("dw_up"))
    dv = matmul(da, pw["w_up"], tb=True, out_dtype=BF16, name=n("dv"))
    dh1, gs["norm_mlp_w"], dh1b = rmsnorm_bwd(dv, s["h1"], sm["norm_mlp_w"], res=dh2, with_bf16=True,
                                              name=n("norm_mlp"))
    gw["w_out"] = wgrad(s["mixed"], dh1b, name=n("dw_out"))
    dmix = matmul(dh1b, pw["w_out"], tb=True, out_dtype=BF16, name=n("dmix"))
    dya, dyb, dg = gate_bwd(cfg, dmix, s["ya"], s["yb"], s["g"], name=n("gate"))
    gw["w_bs"] = wgrad(s["y_ssm"], dya, name=n("dw_bs"))
    gw["w_bm"] = wgrad(s["o"], dyb, name=n("dw_bm"))
    dy_ssm = matmul(dya, pw["w_bs"], tb=True, out_dtype=BF16, name=n("dy_ssm"))
    do = matmul(dyb, pw["w_bm"], tb=True, out_dtype=BF16, name=n("do"))
    dq, dkv, dkpe = attn_bwd(cfg, s["qr"], s["kv"], s["kpe"], s["o"], s["lse"], do, name=n("attn"))
    dqf, dkr = rope_bwd(cfg, dq, dkpe, tabs, name=n("rope"))
    gw["w_uq"] = wgrad(s["cqn"], dqf, name=n("dw_uq"))
    gw["w_ukv"] = wgrad(s["ckvn"], dkv, name=n("dw_ukv"))
    dcqn = matmul(dqf, pw["w_uq"], tb=True, name=n("dcqn"))
    dckvn = matmul(dkv, pw["w_ukv"], tb=True, name=n("dckvn"))
    q_norm_w = sm["q_norm_w"] + _hook(hooks, "after_attn", dqf)
    dcq, gs["q_norm_w"] = rmsnorm_bwd(dcqn, s["small"], q_norm_w, cw=cfg.ql, ci=0, out_dtype=BF16, name=n("q_norm"))
    dckv, gs["kv_norm_w"] = rmsnorm_bwd(dckvn, s["small"], sm["kv_norm_w"], cw=cfg.kvl, ci=cfg.ql // cfg.kvl,
                                        out_dtype=BF16, name=n("kv_norm"))
    ssm_norm_w = sm["ssm_norm_w"] + _hook(hooks, "early", dict(gw))
    dy, dz, gs["ssm_norm_w"] = tail_bwd(cfg, dy_ssm, s["y"], s["z"], ssm_norm_w, name=n("tail"))
    dxc, ddt, ddexp, dav, dbias = ssd_bwd(cfg, s["xc"], s["small"], sm["dt_bias_p"], sm["avec"], sm["dexp"],
                                          s["sin"], dy, name=n("ssd"))
    conv_w = sm["conv_w"] + _hook(hooks, "after_ssd", dxc)
    dxbc, gs["conv_w"], gs["conv_b"] = conv_bwd(cfg, s["xbc"], conv_w, s["dsilu"], dxc, name=n("conv"))
    gs["d_skip"] = ddexp.reshape(cfg.heads, cfg.hd).sum(axis=1)
    gs["a_log"] = (dav[0] * sm["avec"][0])[:cfg.heads]
    gs["dt_bias"] = dbias[0, :cfg.heads]
    dsmall = jnp.concatenate([dcq, dckv, dkr.astype(BF16), ddt.astype(BF16)], axis=1)
    gw["w_z"] = wgrad(s["u"], dz, name=n("dw_z"))
    gw["w_xbc"] = wgrad(s["u"], dxbc, name=n("dw_xbc"))
    gw["w_g"] = wgrad(s["u"], dg, name=n("dw_g"))
    gw["w_s"] = wgrad(s["u"], dsmall, name=n("dw_s"))
    du = matmul_nt_sum([dz, dxbc, dg, dsmall], [pw["w_z"], pw["w_xbc"], pw["w_g"], pw["w_s"]], out_dtype=BF16,
                       name=n("du"))
    if li > 0:
        dh, gs["norm_mix_w"], dhb = rmsnorm_bwd(du, s["h"], sm["norm_mix_w"], res=dh1, with_bf16=True,
                                                name=n("norm_mix"))
    else:
        dh, gs["norm_mix_w"] = rmsnorm_bwd(du, s["h"], sm["norm_mix_w"], res=dh1, name=n("norm_mix"))
        dhb = None
    return (dh, dhb), gw, gs


def small_params(cfg, p, li):
    pad_l = lambda v: jnp.pad(v, (0, LANE - v.shape[0])).reshape(1, LANE)
    return dict(
        norm_mix_w=p["norm_mix_w"][li], conv_w=p["conv_w"][li], conv_b=p["conv_b"][li],
        dt_bias_p=pad_l(p["dt_bias"][li]), avec=pad_l(-jnp.exp(p["a_log"][li])),
        dexp=jnp.repeat(p["d_skip"][li], cfg.hd).reshape(1, cfg.inner),
        ssm_norm_w=p["ssm_norm_w"][li], q_norm_w=p["q_norm_w"][li], kv_norm_w=p["kv_norm_w"][li],
        norm_mlp_w=p["norm_mlp_w"][li])


def local_step(cfg, x, target, p, depth=2):
    bsz, d = cfg.bsz, cfg.d
    lead = jnp.zeros((bsz, cfg.pad, d), F32)
    meta = jnp.broadcast_to(p["meta_tokens"][None], (bsz, cfg.n_meta, d))
    h = jnp.concatenate([lead, meta, x], axis=1).reshape(cfg.t, d)
    tabs = rope_tables(cfg)
    saved, sms = [], []
    for li in range(depth):
        sm = small_params(cfg, p, li)
        h, s, _ = layer_fwd(cfg, h, p["pw"][li], sm, tabs, li)
        saved.append(s)
        sms.append(sm)
    loss, dh, dfw = loss_head(cfg, h, target.reshape(bsz * cfg.seq, d), p["final_norm_w"], name="loss_head")
    gws, gss = [None] * depth, [None] * depth
    for li in reversed(range(depth)):
        dh, gws[li], gss[li] = layer_bwd(cfg, dh, p["pw"][li], sms[li], tabs, saved[li], li)
    dh = dh[0].reshape(bsz, cfg.lp, d)
    grad_x = dh[:, cfg.chunk:, :]
    gmeta = jnp.sum(dh[:, cfg.pad:cfg.chunk, :], axis=0)
    return loss, grad_x, gmeta, gws, gss, dfw


def _pack_small(parts):
    flat = jnp.concatenate([a.reshape(-1) for a in parts])
    n = flat.shape[0]
    npad = -n % (8 * LANE)
    return jnp.pad(flat, (0, npad)).reshape(-1, LANE), n


def _unpack_small(vec, shapes):
    flat = vec.reshape(-1)
    out, off = [], 0
    for sh in shapes:
        sz = int(np.prod(sh))
        out.append(flat[off:off + sz].reshape(sh))
        off += sz
    return out


def _as2d(a):
    return a.reshape(-1, a.shape[-1])


def kernel(x, meta_tokens, norm_mix_w, w_in, conv_w, conv_b, dt_bias, a_log, d_skip, ssm_norm_w, q_norm_w, kv_norm_w, w_uq, w_ukv, w_branch_ssm, w_branch_mla, w_out, norm_mlp_w, w_mlp_up, w_mlp_down, final_norm_w, loss_target, m_meta_tokens, m_norm_mix_w, m_w_in, m_conv_w, m_conv_b, m_dt_bias, m_a_log, m_d_skip, m_ssm_norm_w, m_q_norm_w, m_kv_norm_w, m_w_uq, m_w_ukv, m_w_branch_ssm, m_w_branch_mla, m_w_out, m_norm_mlp_w, m_w_mlp_up, m_w_mlp_down, m_final_norm_w, v_meta_tokens, v_norm_mix_w, v_w_in, v_conv_w, v_conv_b, v_dt_bias, v_a_log, v_d_skip, v_ssm_norm_w, v_q_norm_w, v_kv_norm_w, v_w_uq, v_w_ukv, v_w_branch_ssm, v_w_branch_mla, v_w_out, v_norm_mlp_w, v_w_mlp_up, v_w_mlp_down, v_final_norm_w):
    cfg = CFG
    names = ["meta_tokens", "norm_mix_w", "w_in", "conv_w", "conv_b", "dt_bias", "a_log", "d_skip", "ssm_norm_w",
             "q_norm_w", "kv_norm_w", "w_uq", "w_ukv", "w_branch_ssm", "w_branch_mla", "w_out", "norm_mlp_w",
             "w_mlp_up", "w_mlp_down", "final_norm_w"]
    wts = dict(zip(names, [meta_tokens, norm_mix_w, w_in, conv_w, conv_b, dt_bias, a_log, d_skip, ssm_norm_w,
                           q_norm_w, kv_norm_w, w_uq, w_ukv, w_branch_ssm, w_branch_mla, w_out, norm_mlp_w,
                           w_mlp_up, w_mlp_down, final_norm_w]))
    ms = dict(zip(names, [m_meta_tokens, m_norm_mix_w, m_w_in, m_conv_w, m_conv_b, m_dt_bias, m_a_log, m_d_skip,
                          m_ssm_norm_w, m_q_norm_w, m_kv_norm_w, m_w_uq, m_w_ukv, m_w_branch_ssm, m_w_branch_mla,
                          m_w_out, m_norm_mlp_w, m_w_mlp_up, m_w_mlp_down, m_final_norm_w]))
    vs = dict(zip(names, [v_meta_tokens, v_norm_mix_w, v_w_in, v_conv_w, v_conv_b, v_dt_bias, v_a_log, v_d_skip,
                          v_ssm_norm_w, v_q_norm_w, v_kv_norm_w, v_w_uq, v_w_ukv, v_w_branch_ssm, v_w_branch_mla,
                          v_w_out, v_norm_mlp_w, v_w_mlp_up, v_w_mlp_down, v_final_norm_w]))
    cx, cy, cc = _coords()
    chip = 2 * cx + cy

    half1 = jnp.reshape(cc, (1,)).astype(jnp.int32)
    where2 = jnp.stack([chip, cc]).astype(jnp.int32)
    wb = {k: wts[k].astype(BF16) for k in BIG}
    zero_tok = jnp.zeros((8, LANE), F32)

    def halves(a):
        return a.reshape((2, a.shape[0] // 2) + a.shape[1:])

    def gather_start(li, keys, tag, after):
        srcs = [halves(wb[k][li]) for k in keys]
        lands = [lax.empty((4,) + s.shape, BF16) for s in srcs]
        return ici_start("gather", srcs, lands, after, name=f"gather{li}{tag}_start")

    def gather_finish(li, keys, tag, started, after):
        srcs, lands = ici_wait("gather", started, after, name=f"gather{li}{tag}_wait")
        lands = pair_share(lands, srcs, name=f"gather{li}{tag}_share")
        full = {k: _unshard_layer(k, land.reshape((4, 2 * land.shape[2], land.shape[3])))
                for k, land in zip(keys, lands)}
        return prep_layer(cfg, full)

    def gather_mid(li, keys, tag, started, after):
        srcs, lands = ici_wait("gather", started, after, name=f"gather{li}{tag}_wait")
        return ici_start("share", srcs, lands, zero_tok, name=f"gather{li}{tag}_share_start")

    def gather_end(li, keys, tag, shared, after):
        _, lands = ici_wait("share", shared, after, name=f"gather{li}{tag}_share_wait")
        full = {k: _unshard_layer(k, land.reshape((4, 2 * land.shape[2], land.shape[3])))
                for k, land in zip(keys, lands)}
        return prep_layer(cfg, full)

    def exchange_start(li, keys, tag, gw, after):
        ug = unprep_grads(cfg, gw)
        g4 = []
        for k in keys:
            s = _to_shards(k, ug[k])
            g4.append(s.reshape(4, 2, s.shape[1] // 2, s.shape[2]))
        lands = [lax.empty((4,) + a.shape[2:], a.dtype) for a in g4]
        return ici_start("exchange", g4, lands, after, name=f"grad{li}{tag}_exchange_start")

    def reduce_start(li, keys, tag, exchanged, after):
        g4, theirs = ici_wait("exchange", exchanged, after, name=f"grad{li}{tag}_exchange_wait")
        parts = [pair_add(a, b, half1, name=f"grad{li}_pair_add_{k}") for k, a, b in zip(keys, g4, theirs)]
        lands = [lax.empty(q.shape, q.dtype) for q in parts]
        return ici_start("scatter", parts, lands, zero_tok, name=f"grad{li}{tag}_scatter_start")

    def reduce_mid(li, keys, tag, started, after):
        parts, lands = ici_wait("scatter", started, after, name=f"grad{li}{tag}_scatter_wait")
        sums = [chip_sum(rc, pt, where2, name=f"grad{li}_chip_sum_{k}") for k, rc, pt in zip(keys, lands, parts)]
        return ici_start("fill", [zero_tok] * len(sums), sums, zero_tok, name=f"grad{li}{tag}_fill_start")

    def reduce_end(li, keys, tag, filled, after):
        _, sums = ici_wait("fill", filled, after, name=f"grad{li}{tag}_fill_wait")
        return {k: s.reshape(2 * s.shape[1], s.shape[2]) for k, s in zip(keys, sums)}

    gathered = gather_chips([meta_tokens, conv_w], name="gather_small")
    p = dict(wts)
    p["meta_tokens"] = jnp.transpose(gathered[0], (1, 0, 2)).reshape(cfg.n_meta, cfg.d)
    p["conv_w"] = jnp.transpose(gathered[1], (1, 2, 0, 3)).reshape(2, cfg.convk, cfg.conv_dim)

    st0a = gather_start(0, ["w_in"], "a", gathered[0])
    st0b = gather_start(0, REST, "b", st0a[4])
    st1 = gather_start(1, BIG, "", st0b[4])
    pw0 = gather_finish(0, ["w_in"], "a", st0a, st1[4])

    bsz, d = cfg.bsz, cfg.d
    lead = jnp.zeros((bsz, cfg.pad, d), F32)
    meta = jnp.broadcast_to(p["meta_tokens"][None], (bsz, cfg.n_meta, d))
    h0 = jnp.concatenate([lead, meta, x], axis=1).reshape(cfg.t, d)
    tabs = rope_tables(cfg)
    sm0 = small_params(cfg, p, 0)
    st = {}

    def step(key, fn):
        def run(arg):
            st[key] = fn(arg)
            return st[key][4]
        return run

    h1, sv0, pw0 = layer_fwd(cfg, h0, pw0, sm0, tabs, 0, hooks={
        "after_conv": step("share0b", lambda after: gather_mid(0, REST, "b", st0b, after)),
        "weights": lambda after: gather_end(0, REST, "b", st["share0b"], after),
        "after_attn": step("share1", lambda after: gather_mid(1, BIG, "", st1, after))})
    pw1 = gather_end(1, BIG, "", st["share1"], h1)
    sm1 = small_params(cfg, p, 1)
    h2, sv1, _ = layer_fwd(cfg, h1, pw1, sm1, tabs, 1)
    loss, dh, dfw = loss_head(cfg, h2, loss_target.reshape(bsz * cfg.seq, d), final_norm_w, name="loss_head")

    dh, gw1, gs1 = layer_bwd(cfg, dh, pw1, sm1, tabs, sv1, 1)
    ex1 = exchange_start(1, BIG, "", gw1, zero_tok)
    sm0b = dict(sm0)
    sm0b["norm_mlp_w"] = sm0["norm_mlp_w"] + ex1[4][0, 0]
    dh, gw0, gs0 = layer_bwd(cfg, dh, pw0, sm0b, tabs, sv0, 0, hooks={
        "after_attn": step("red1", lambda after: reduce_start(1, BIG, "", ex1, after)),
        "early": step("ex0e", lambda gw: exchange_start(0, REST, "e", gw, zero_tok)),
        "after_ssd": step("red0e", lambda after: reduce_start(0, REST, "e", st["ex0e"], after))})
    dh3 = dh[0].reshape(bsz, cfg.lp, d)
    grad_x = dh3[:, cfg.chunk:, :]
    gmeta = jnp.sum(dh3[:, cfg.pad:cfg.chunk, :], axis=0)
    fill1 = reduce_mid(1, BIG, "", st["red1"], dh[0])
    ex0l = exchange_start(0, ["w_in"], "l", gw0, fill1[4])

    small_names = SMALL_REPL + ["conv_w"]
    parts = [jnp.stack([gs0[k], gs1[k]]) for k in small_names] + [dfw, gmeta, loss.reshape(1)]
    shapes = [a.shape for a in parts]
    vec, _ = _pack_small(parts)
    red_vec = allreduce_small(vec, ex0l[4], name="allreduce_small")
    red = _unpack_small(red_vec, shapes)
    sg = dict(zip(small_names + ["final_norm_w", "meta_tokens"], red))
    loss = red[-1].reshape(())
    sg["conv_w"] = lax.dynamic_slice_in_dim(sg["conv_w"], chip * (cfg.conv_dim // 4), cfg.conv_dim // 4, axis=2)
    sg["meta_tokens"] = lax.dynamic_slice_in_dim(sg["meta_tokens"], chip * (cfg.d // 4), cfg.d // 4, axis=1)

    red0 = reduce_start(0, ["w_in"], "l", ex0l, red_vec)
    grads, deltas, new_m, new_v = {}, {}, {}, {}
    dep = red0[4]
    for k in names:
        if k in BIG:
            continue
        w2, g2, m2, v2 = _as2d(wts[k]), _as2d(sg[k]), _as2d(ms[k]), _as2d(vs[k])
        dl, mn, vn = adamw_small(w2, g2, m2, v2, dep, name=f"adamw_{k}")
        grads[k] = sg[k].reshape(wts[k].shape)
        deltas[k], new_m[k], new_v[k] = (t.reshape(wts[k].shape) for t in (dl, mn, vn))

    def view(k, a):
        return jnp.swapaxes(a, 1, 2) if k == "w_in" else a

    def gview(k, g):
        return g.T if k == "w_in" else g

    wv, mv, vv = ({k: view(k, t[k]) for k in BIG} for t in (wts, ms, vs))
    outs = {}
    big1 = reduce_end(1, BIG, "", fill1, dl)
    fill0e = reduce_mid(0, REST, "e", st["red0e"], big1[BIG[-1]])
    dep = fill0e[4]
    for k in BIG:
        outs[k] = adamw_layer(wv[k], mv[k], vv[k], gview(k, big1[k]), 1, None, dep, name=f"adamw1_{k}")
        dep = outs[k][1]
    big0 = reduce_end(0, REST, "e", fill0e, dep)
    fill0l = reduce_mid(0, ["w_in"], "l", red0, big0[REST[-1]])
    dep = fill0l[4]
    for k in REST:
        outs[k] = adamw_layer(wv[k], mv[k], vv[k], big0[k], 0, outs[k], dep, name=f"adamw0_{k}")
        dep = outs[k][1]
    big0.update(reduce_end(0, ["w_in"], "l", fill0l, dep))
    outs["w_in"] = adamw_layer(wv["w_in"], mv["w_in"], vv["w_in"], gview("w_in", big0["w_in"]), 0, outs["w_in"], dep,
                               name="adamw0_w_in")
    for k in BIG:
        grads[k], deltas[k], new_m[k], new_v[k] = (view(k, t) for t in outs[k])
    return (loss, grad_x, *[grads[k] for k in names], *[deltas[k] for k in names],
            *[new_m[k] for k in names], *[new_v[k] for k in names])


def adamw_small(w, g, m, v, dep, *, name):
    def body(w_ref, g_ref, m_ref, v_ref, dep_ref, d_ref, mo_ref, vo_ref):
        d_ref[...], mo_ref[...], vo_ref[...] = _adam_update(w_ref[...], g_ref[...], m_ref[...], v_ref[...])

    vm = pl.BlockSpec(memory_space=pltpu.VMEM)
    return pl.pallas_call(body, name=name, in_specs=[vm] * 4 + [pl.BlockSpec(memory_space=pl.ANY)], out_specs=[vm] * 3,
                          out_shape=[_sds(w.shape, F32)] * 3, compiler_params=_cp())(w, g, m, v, dep)
```

```python
import functools
from typing import NamedTuple

import numpy as np
import jax
import jax.numpy as jnp
from jax import lax
from jax.experimental import pallas as pl
from jax.experimental.pallas import tpu as pltpu

F32 = jnp.float32
BF16 = jnp.bfloat16
EPS = 1e-6
ROPE_THETA = 10000.0
LANE = 128
VMEM_LIMIT = 56 * 1024 * 1024
MASK_VALUE = -1e30
ADAM_LR, ADAM_B1, ADAM_B2, ADAM_EPS, ADAM_WD, ADAM_STEP = 0.001, 0.9, 0.999, 1e-08, 0.01, 10
MESH = pl.DeviceIdType.MESH


class Cfg(NamedTuple):
    d: int = 1024
    seq: int = 2048
    bsz: int = 2
    n_meta: int = 16
    inner: int = 2048
    hd: int = 64
    groups: int = 4
    state: int = 128
    convk: int = 4
    chunk: int = 128
    mh: int = 8
    ql: int = 512
    kvl: int = 256
    nope: int = 128
    rope: int = 64
    vd: int = 128
    ff: int = 4096

    @property
    def heads(self): return self.inner // self.hd
    @property
    def gw(self): return self.inner // self.groups
    @property
    def conv_dim(self): return self.inner + 2 * self.groups * self.state
    @property
    def pad(self): return self.chunk - self.n_meta
    @property
    def lp(self): return self.chunk + self.seq
    @property
    def t(self): return self.bsz * self.lp
    @property
    def nchunks(self): return self.lp // self.chunk
    @property
    def sw(self): return self.ql + self.kvl + 2 * LANE
    @property
    def kt(self): return (self.ql + self.kvl) // LANE
    @property
    def dtt(self): return self.kt + 1
    @property
    def qw(self): return self.mh * 2 * LANE
    @property
    def in_splits(self):
        return [self.inner, self.conv_dim, self.heads, self.ql, self.kvl, self.rope, self.d, self.d]


CFG = Cfg()


def _pick(dim, pref, mult):
    best = None
    for t in range(mult, min(dim, pref) + 1, mult):
        if dim % t == 0:
            best = t
    return best if best is not None else dim


def _cp(**kw):
    return pltpu.CompilerParams(vmem_limit_bytes=VMEM_LIMIT, **kw)


def _sds(shape, dtype):
    return jax.ShapeDtypeStruct(tuple(shape), dtype)


def _silu(x):
    return x * jax.nn.sigmoid(x)


def _dsilu(x):
    s = jax.nn.sigmoid(x)
    return s * (1.0 + x * (1.0 - s))


def _ep_plain(r):
    return (r,)


def _ep_add(r, res):
    return (r + res.astype(F32),)


def _ep_relu2(r):
    rp = jnp.maximum(r, 0.0)
    return r, rp * rp


def _ep_relu2_grad(r, a):
    return (r * (2.0 * jnp.maximum(a.astype(F32), 0.0)),)


MM_VMEM_BUDGET = 44 * 1024 * 1024


def _mm_tiles(m, n, k, a_bytes, b_bytes, io_bytes, ta):
    m_mult, m_cap = (LANE, 1024) if ta else (16, 1088)
    tms = [t for t in range(m_cap, 0, -m_mult) if m % t == 0] or [m]
    tns = [t for t in (1024, 512, 256, 128) if n % t == 0] or [n]
    best = None
    for tm in tms:
        for tn in tns:
            need = 2 * (tm * k * a_bytes + k * tn * b_bytes + tm * tn * io_bytes)
            if need <= MM_VMEM_BUDGET and (best is None or tm * tn > best[0] * best[1]):
                best = (tm, tn)
    if best is None:
        return (_pick(m, 512, m_mult), _pick(n, 512, LANE), _pick(k, 1088 if ta else 1024, 16 if ta else LANE))
    return best[0], best[1], k


def _resident_rows(m, n, k, a_bytes, b_bytes, io_bytes):
    w = n * k * b_bytes
    if w > 18 * 1024 * 1024:
        return None
    for tm in range(544, 255, -16):
        if m % tm == 0 and w + 2 * tm * (k * a_bytes + n * io_bytes) + tm * n * 4 <= MM_VMEM_BUDGET - (4 << 20):
            return tm
    return None


def matmul(a, b, *, ta=False, tb=False, out_dtype=F32, add=None, name, tm=None, tn=None, tk=None,
           epilogue=None, extras=(), out_dtypes=None):
    if add is not None:
        epilogue, extras = _ep_add, (add,)
    if epilogue is None:
        epilogue = _ep_plain
    out_dtypes = tuple(out_dtypes) if out_dtypes is not None else (out_dtype,)
    n_ex, n_out = len(extras), len(out_dtypes)
    if ta:
        k_dim, m_dim = a.shape
    else:
        m_dim, k_dim = a.shape
    if tb:
        n_dim, k2 = b.shape
    else:
        k2, n_dim = b.shape
    assert k_dim == k2, (a.shape, b.shape, ta, tb)
    resident = False
    if tm is None and tn is None and tk is None:
        io_bytes = sum(jnp.dtype(e.dtype).itemsize for e in extras) + sum(jnp.dtype(d).itemsize for d in out_dtypes)
        a_bytes, b_bytes = jnp.dtype(a.dtype).itemsize, jnp.dtype(b.dtype).itemsize
        tm = None if ta else _resident_rows(m_dim, n_dim, k_dim, a_bytes, b_bytes, io_bytes)
        if tm is not None:
            resident, tn, tk = True, n_dim, k_dim
        else:
            tm, tn, tk = _mm_tiles(m_dim, n_dim, k_dim, a_bytes, b_bytes, io_bytes, ta)
    elif ta:
        tm = tm or _pick(m_dim, 1024, LANE)
        tk = tk or _pick(k_dim, 1088, 16)
        tn = tn or _pick(n_dim, 1024, LANE)
    else:
        tm = tm or _pick(m_dim, 1088, 16)
        tk = tk or _pick(k_dim, 1024 if a.dtype == F32 else 2048, LANE)
        tn = tn or _pick(n_dim, 1024, LANE)
    nm, nn, nk = m_dim // tm, n_dim // tn, k_dim // tk
    dn = (((0 if ta else 1,), (1 if tb else 0,)), ((), ()))

    def body(*refs):
        a_ref, b_ref = refs[:2]
        ex_refs = refs[2:2 + n_ex]
        o_refs = refs[2 + n_ex:2 + n_ex + n_out]
        scr = refs[2 + n_ex + n_out:]
        p = lax.dot_general(a_ref[...].astype(BF16), b_ref[...].astype(BF16), dn, preferred_element_type=F32)

        def finish(r):
            outs = epilogue(r, *[e[...] for e in ex_refs])
            for o_ref, val, dt in zip(o_refs, outs, out_dtypes):
                o_ref[...] = val.astype(dt)

        if nk == 1:
            finish(p)
        else:
            acc = scr[0]
            k = pl.program_id(2)

            @pl.when(k == 0)
            def _():
                acc[...] = p

            @pl.when(k > 0)
            def _():
                acc[...] += p

            @pl.when(k == nk - 1)
            def _():
                finish(acc[...])

    if resident:
        row = pl.BlockSpec((tm, k_dim), lambda i: (i, 0))
        o_spec = pl.BlockSpec((tm, n_dim), lambda i: (i, 0))
        outs = pl.pallas_call(
            body, name=name, grid=(nm,),
            in_specs=[row, pl.BlockSpec(b.shape, lambda i: (0, 0), pipeline_mode=pl.Buffered(1))] + [o_spec] * n_ex,
            out_specs=[o_spec] * n_out, out_shape=[_sds((m_dim, n_dim), dt) for dt in out_dtypes],
            compiler_params=_cp(dimension_semantics=("parallel",)),
        )(a, b, *extras)
        return outs[0] if n_out == 1 else tuple(outs)
    a_spec = pl.BlockSpec((tk, tm), lambda i, j, k: (k, i)) if ta else pl.BlockSpec((tm, tk), lambda i, j, k: (i, k))
    b_spec = pl.BlockSpec((tn, tk), lambda i, j, k: (j, k)) if tb else pl.BlockSpec((tk, tn), lambda i, j, k: (k, j))
    o_spec = pl.BlockSpec((tm, tn), lambda i, j, k: (i, j))
    outs = pl.pallas_call(
        body, name=name, grid=(nm, nn, nk), in_specs=[a_spec, b_spec] + [o_spec] * n_ex, out_specs=[o_spec] * n_out,
        out_shape=[_sds((m_dim, n_dim), dt) for dt in out_dtypes],
        scratch_shapes=[pltpu.VMEM((tm, tn), F32)] if nk > 1 else [],
        compiler_params=_cp(dimension_semantics=("parallel", "parallel", "arbitrary")),
    )(a, b, *extras)
    return outs[0] if n_out == 1 else tuple(outs)


def matmul_multi(a, bs_, out_dtypes, *, name):
    m, k = a.shape
    ns = [b.shape[1] for b in bs_]
    cnt = len(bs_)
    out_row_bytes = sum(n * jnp.dtype(dt).itemsize for n, dt in zip(ns, out_dtypes))
    w_bytes = sum(k * n * jnp.dtype(b.dtype).itemsize for n, b in zip(ns, bs_))
    tm = next(t for t in range(1088, 0, -16)
              if m % t == 0 and w_bytes + 2 * t * (k * jnp.dtype(a.dtype).itemsize + out_row_bytes)
              + t * max(ns) * 4 <= MM_VMEM_BUDGET - (8 << 20))

    def body(*refs):
        a_ref = refs[0]
        b_refs, o_refs = refs[1:1 + cnt], refs[1 + cnt:]
        av = a_ref[...].astype(BF16)
        for b_ref, o_ref, dt in zip(b_refs, o_refs, out_dtypes):
            o_ref[...] = _nn(av, b_ref[...].astype(BF16)).astype(dt)

    return pl.pallas_call(
        body, name=name, grid=(m // tm,),
        in_specs=[pl.BlockSpec((tm, k), lambda i: (i, 0))]
        + [pl.BlockSpec((k, n), lambda i: (0, 0), pipeline_mode=pl.Buffered(1)) for n in ns],
        out_specs=[pl.BlockSpec((tm, n), lambda i: (i, 0)) for n in ns],
        out_shape=[_sds((m, n), dt) for n, dt in zip(ns, out_dtypes)], compiler_params=_cp(),
    )(a, *bs_)


def matmul_nt_sum(as_, bs_, *, out_dtype=F32, name, tiles=None):
    m, n = as_[0].shape[0], bs_[0].shape[0]
    ks = [a.shape[1] for a in as_]
    assert [b.shape[1] for b in bs_] == ks
    ksum, cnt = sum(ks), len(ks)
    best = tiles
    for tn in [t for t in (1024, 512, 256, 128) if n % t == 0]:
        for tm in [t for t in range(1088, 0, -16) if m % t == 0]:
            need = 2 * (tm * ksum * 2 + tn * ksum * 2 + tm * tn * jnp.dtype(out_dtype).itemsize)
            if best is None and need <= MM_VMEM_BUDGET and tm >= 256:
                best = (tm, tn)
    tm, tn = best

    def body(*refs):
        a_refs, b_refs, o_ref = refs[:cnt], refs[cnt:2 * cnt], refs[2 * cnt]
        acc = None
        for a_ref, b_ref in zip(a_refs, b_refs):
            p = _nt(a_ref[...].astype(BF16), b_ref[...].astype(BF16))
            acc = p if acc is None else acc + p
        o_ref[...] = acc.astype(out_dtype)

    return pl.pallas_call(
        body, name=name, grid=(n // tn, m // tm),
        in_specs=[pl.BlockSpec((tm, k), lambda j, i: (i, 0)) for k in ks]
        + [pl.BlockSpec((tn, k), lambda j, i: (j, 0)) for k in ks],
        out_specs=pl.BlockSpec((tm, tn), lambda j, i: (i, j)), out_shape=_sds((m, n), out_dtype),
        compiler_params=_cp(dimension_semantics=("parallel", "parallel")),
    )(*as_, *bs_)


def rmsnorm_fwd(x, w, *, cw=None, ci=0, name):
    t = x.shape[0]
    cw = cw or x.shape[1]
    tr = _pick(t, 544, 16)

    def body(x_ref, w_ref, o_ref):
        xv = x_ref[...].astype(F32)
        r = lax.rsqrt(jnp.mean(xv * xv, axis=-1, keepdims=True) + EPS)
        o_ref[...] = (xv * r * w_ref[...]).astype(BF16)

    return pl.pallas_call(
        body, name=name, grid=(t // tr,),
        in_specs=[pl.BlockSpec((tr, cw), lambda i: (i, ci)), pl.BlockSpec((1, cw), lambda i: (0, 0))],
        out_specs=pl.BlockSpec((tr, cw), lambda i: (i, 0)),
        out_shape=_sds((t, cw), BF16), compiler_params=_cp(),
    )(x, w.reshape(1, cw))


def rmsnorm_bwd(dy, x, w, *, cw=None, ci=0, res=None, out_dtype=F32, with_bf16=False, name):
    t = x.shape[0]
    cw = cw or x.shape[1]
    tr = _pick(t, 544, 16)
    has_res = res is not None

    def body(*refs):
        dxb_ref = None
        if with_bf16:
            refs, dxb_ref = refs[:-1], refs[-1]
        if has_res:
            dy_ref, x_ref, w_ref, res_ref, dx_ref, dw_ref = refs
        else:
            dy_ref, x_ref, w_ref, dx_ref, dw_ref = refs
        xv = x_ref[...].astype(F32)
        dyv = dy_ref[...].astype(F32)
        r = lax.rsqrt(jnp.mean(xv * xv, axis=-1, keepdims=True) + EPS)
        xh = xv * r
        g = dyv * w_ref[...]
        dx = r * (g - xh * jnp.mean(g * xh, axis=-1, keepdims=True))
        if has_res:
            dx = dx + res_ref[...]
        dx_ref[...] = dx.astype(out_dtype)
        if with_bf16:
            dxb_ref[...] = dx.astype(BF16)

        @pl.when(pl.program_id(0) == 0)
        def _():
            dw_ref[...] = jnp.zeros_like(dw_ref)

        dw_ref[...] += jnp.sum(dyv * xh, axis=0, keepdims=True)

    row = pl.BlockSpec((tr, cw), lambda i: (i, 0))
    in_specs = [row, pl.BlockSpec((tr, cw), lambda i: (i, ci)), pl.BlockSpec((1, cw), lambda i: (0, 0))]
    args = [dy, x, w.reshape(1, cw)]
    if has_res:
        in_specs.append(row)
        args.append(res)
    outs = pl.pallas_call(
        body, name=name, grid=(t // tr,), in_specs=in_specs,
        out_specs=[row, pl.BlockSpec((1, cw), lambda i: (0, 0))] + ([row] if with_bf16 else []),
        out_shape=[_sds((t, cw), out_dtype), _sds((1, cw), F32)] + ([_sds((t, cw), BF16)] if with_bf16 else []),
        compiler_params=_cp(),
    )(*args)
    if with_bf16:
        return outs[0], outs[1][0], outs[2]
    return outs[0], outs[1][0]


def _shift_down(x, s):
    return x if s == 0 else pltpu.roll(x, s, 0)


def _shift_up(x, s):
    return x if s == 0 else pltpu.roll(x, x.shape[0] - s, 0)


def _conv_pre(x, w_ref, b_ref, kk):
    pre = b_ref[...] + jnp.zeros_like(x)
    for k in range(kk):
        pre = pre + w_ref[k:k + 1, :] * _shift_down(x, kk - 1 - k)
    return pre


def conv_fwd(cfg, xbc, w, b, *, name):
    lp, cd, kk = cfg.lp, cfg.conv_dim, cfg.convk
    assert cfg.pad >= kk - 1
    cb = _pick(cd, 512, LANE)

    def body(x_ref, w_ref, b_ref, o_ref, ds_ref):
        pre = _conv_pre(x_ref[...], w_ref, b_ref, kk)
        sg = jax.nn.sigmoid(pre)
        o_ref[...] = pre * sg
        ds_ref[...] = (sg * (1.0 + pre * (1.0 - sg))).astype(BF16)

    blk = pl.BlockSpec((lp, cb), lambda j, bb: (bb, j))
    return pl.pallas_call(
        body, name=name, grid=(cd // cb, cfg.bsz),
        in_specs=[blk, pl.BlockSpec((kk, cb), lambda j, bb: (0, j)), pl.BlockSpec((1, cb), lambda j, bb: (0, j))],
        out_specs=[blk, blk], out_shape=[_sds((cfg.t, cd), F32), _sds((cfg.t, cd), BF16)], compiler_params=_cp(),
    )(xbc, w, b.reshape(1, cd))


def conv_bwd(cfg, xbc, w, dsilu, dxc, *, name):
    lp, cd, kk = cfg.lp, cfg.conv_dim, cfg.convk
    cb = _pick(cd, 512, LANE)

    def body(x_ref, w_ref, s_ref, d_ref, dx_ref, dw_ref, db_ref):
        x = x_ref[...]
        dpre = d_ref[...] * s_ref[...].astype(F32)
        dx = jnp.zeros_like(x)
        dws = []
        for k in range(kk):
            s = kk - 1 - k
            dx = dx + w_ref[k:k + 1, :] * _shift_up(dpre, s)
            dws.append(jnp.sum(dpre * _shift_down(x, s), axis=0, keepdims=True))
        dx_ref[...] = dx.astype(BF16)

        @pl.when(pl.program_id(1) == 0)
        def _():
            dw_ref[...] = jnp.zeros_like(dw_ref)
            db_ref[...] = jnp.zeros_like(db_ref)

        for k in range(kk):
            dw_ref[k:k + 1, :] += dws[k]
        db_ref[...] += jnp.sum(dpre, axis=0, keepdims=True)

    blk = pl.BlockSpec((lp, cb), lambda j, bb: (bb, j))
    wsp = pl.BlockSpec((kk, cb), lambda j, bb: (0, j))
    bsp = pl.BlockSpec((1, cb), lambda j, bb: (0, j))
    dx, dw, db = pl.pallas_call(
        body, name=name, grid=(cd // cb, cfg.bsz),
        in_specs=[blk, wsp, blk, blk], out_specs=[blk, wsp, bsp],
        out_shape=[_sds((cfg.t, cd), BF16), _sds((kk, cd), F32), _sds((1, cd), F32)], compiler_params=_cp(),
    )(xbc, w, dsilu, dxc)
    return dx, dw, db[0]


def _softplus(x):
    return jnp.maximum(x, 0.0) + jnp.log(1.0 + jnp.exp(-jnp.abs(x)))


def _ssd_consts(cfg):
    q = cfg.chunk
    i0 = np.arange(q)[:, None]
    i1 = np.arange(q)[None, :]
    ltri = (i1 <= i0).astype(np.float32)
    rexp = np.zeros((LANE, cfg.inner), np.float32)
    for h in range(cfg.heads):
        rexp[h, h * cfg.hd:(h + 1) * cfg.hd] = 1.0
    return jnp.asarray(ltri), jnp.asarray(rexp)


def _sel_dot(x, m, *, passes=2, left=False, trans=False):
    mb = m.astype(BF16)
    acc, rem = None, x
    for _ in range(passes):
        piece = rem.astype(BF16)
        if not left:
            part = _nn(piece, mb)
        elif trans:
            part = _tn(mb, piece)
        else:
            part = _nn(mb, piece)
        acc = part if acc is None else acc + part
        rem = rem - piece.astype(F32)
    return acc


def _ssd_chunk_common(cfg, raw, bias, avec, c_idx, ltri, rexp):
    q = cfg.chunk
    rows = lax.broadcasted_iota(jnp.int32, (q, LANE), 0)
    live = jnp.logical_or(c_idx > 0, rows >= cfg.pad)
    pre = raw + bias
    dt = jnp.where(live, _softplus(pre), 0.0)
    adt = dt * avec
    cs = _sel_dot(adt, ltri, passes=3, left=True)
    cs_t = cs.T
    cs_last = cs[q - 1:q, :]
    e_in = jnp.exp(cs)
    w0 = jnp.exp(cs_last - cs)
    decay = jnp.exp(cs_last)
    return dict(live=live, pre=pre, dt=dt, adt=adt, cs=cs, cs_t=cs_t, e_in=e_in, w0=w0, decay=decay,
                DT=_sel_dot(dt, rexp), E=_sel_dot(e_in, rexp), W0=_sel_dot(w0, rexp),
                DEC=_sel_dot(jnp.broadcast_to(decay, (8, LANE)), rexp)[0:1, :])


def _tri_masks(q):
    r = lax.broadcasted_iota(jnp.int32, (q, q), 0)
    c = lax.broadcasted_iota(jnp.int32, (q, q), 1)
    return c <= r, r <= c


def _head_l(cq, h, tri, tri_t):
    col = cq["cs"][:, h:h + 1]
    row = cq["cs_t"][h:h + 1, :]
    lmat = jnp.where(tri, jnp.exp(jnp.minimum(col - row, 0.0)), 0.0)
    lmat_t = jnp.where(tri_t, jnp.exp(jnp.minimum(row - col, 0.0)), 0.0)
    return lmat, lmat_t


def _nt(a, b):
    return lax.dot_general(a, b, (((1,), (1,)), ((), ())), preferred_element_type=F32)


def _tn(a, b):
    return lax.dot_general(a, b, (((0,), (0,)), ((), ())), preferred_element_type=F32)


def _nn(a, b):
    return jnp.dot(a, b, preferred_element_type=F32)


def ssd_fwd(cfg, xc, small, dt_bias, avec, dexp, *, name):
    q, inner, st, gw, g_n = cfg.chunk, cfg.inner, cfg.state, cfg.gw, cfg.groups
    nc = cfg.nchunks
    ltri, rexp = _ssd_consts(cfg)
    hpt = LANE // cfg.hd
    tiles_per_group = gw // LANE

    bsz, lp = cfg.bsz, cfg.lp
    bcw = g_n * st

    def body(x_ref, b_ref, c_ref, dt_ref, bias_ref, a_ref, d_ref, ltri_ref, rexp_ref, y_ref, sin_ref, s_scr):
        c_idx = pl.program_id(0)

        @pl.when(c_idx == 0)
        def _():
            s_scr[...] = jnp.zeros_like(s_scr)

        ltri_v = ltri_ref[...]
        tri, tri_t = _tri_masks(q)
        lane = lax.broadcasted_iota(jnp.int32, (q, LANE), 1)
        for bi in range(bsz):
            cq = _ssd_chunk_common(cfg, dt_ref[bi], bias_ref[...], a_ref[...], c_idx, ltri_v, rexp_ref[...])
            xs = x_ref[bi]
            xdt = (xs * cq["DT"]).astype(BF16)
            xw = (xs * cq["DT"] * cq["W0"]).astype(BF16)
            s_in = s_scr[bi]
            sin_ref[bi, 0] = s_in
            for g in range(g_n):
                bg = b_ref[bi, :, g * st:(g + 1) * st].astype(BF16)
                cg = c_ref[bi, :, g * st:(g + 1) * st].astype(BF16)
                gmat = _nt(cg, bg)
                gs = slice(g * gw, (g + 1) * gw)
                y0 = _nn(cg, s_in[:, gs].astype(BF16))
                for tt in range(tiles_per_group):
                    tile = g * tiles_per_group + tt
                    ts = slice(tile * LANE, (tile + 1) * LANE)
                    xt = xdt[:, ts]
                    ms, xh = [], []
                    for hh in range(hpt):
                        lmat, _ = _head_l(cq, tile * hpt + hh, tri, tri_t)
                        ms.append((gmat * lmat).astype(BF16))
                        inhead = jnp.logical_and(lane >= hh * cfg.hd, lane < (hh + 1) * cfg.hd)
                        xh.append(jnp.where(inhead, xt, jnp.zeros_like(xt)))
                    yd = _nn(jnp.concatenate(ms, axis=1), jnp.concatenate(xh, axis=0))
                    y_ref[bi, :, ts] = (yd + y0[:, tt * LANE:(tt + 1) * LANE] * cq["E"][:, ts]
                                        + xs[:, ts] * d_ref[:, ts]).astype(BF16)
                s_scr[bi, :, gs] = s_in[:, gs] * cq["DEC"][:, gs] + _tn(bg, xw[:, gs])

    def rowblk(width, col):
        return pl.BlockSpec((bsz, q, width), lambda c: (0, c, col))

    def const(shape):
        return pl.BlockSpec(shape, lambda c: (0, 0))

    xc3 = xc.reshape(bsz, lp, cfg.conv_dim)
    y, sin = pl.pallas_call(
        body, name=name, grid=(nc,),
        in_specs=[rowblk(inner, 0), rowblk(bcw, inner // bcw), rowblk(bcw, inner // bcw + 1),
                  rowblk(LANE, cfg.dtt), const((1, LANE)), const((1, LANE)), const((1, inner)),
                  const((q, q)), const((LANE, inner))],
        out_specs=[rowblk(inner, 0), pl.BlockSpec((bsz, 1, st, inner), lambda c: (0, c, 0, 0))],
        out_shape=[_sds((bsz, lp, inner), BF16), _sds((bsz, nc, st, inner), F32)],
        scratch_shapes=[pltpu.VMEM((bsz, st, inner), F32)], compiler_params=_cp(),
    )(xc3, xc3, xc3, small.reshape(bsz, lp, cfg.sw), dt_bias, avec, dexp, ltri, rexp)
    return y.reshape(cfg.t, inner), sin.reshape(bsz * nc, st, inner)


def ssd_bwd(cfg, xc, small, dt_bias, avec, dexp, sin, dy, *, name):
    q, inner, st, gw, g_n = cfg.chunk, cfg.inner, cfg.state, cfg.gw, cfg.groups
    nc = cfg.nchunks
    ltri, rexp = _ssd_consts(cfg)
    rexp_t = rexp.T
    hpt = LANE // cfg.hd
    tiles_per_group = gw // LANE
    bcw = g_n * st

    def body(x_ref, b_ref, c_ref, dt_ref, bias_ref, a_ref, d_ref, ltri_ref, rexp_ref, rexpt_ref, sin_ref, dy_ref,
             dx_ref, ddt_ref, dd_ref, da_ref, dbias_ref, ds_scr):
        step = pl.program_id(1)
        c_idx = nc - 1 - step

        @pl.when(step == 0)
        def _():
            ds_scr[...] = jnp.zeros_like(ds_scr)

        @pl.when(jnp.logical_and(step == 0, pl.program_id(0) == 0))
        def _():
            dd_ref[...] = jnp.zeros_like(dd_ref)
            da_ref[...] = jnp.zeros_like(da_ref)
            dbias_ref[...] = jnp.zeros_like(dbias_ref)

        ltri_v = ltri_ref[...]
        tri, tri_t = _tri_masks(q)
        red = _sel_dot
        rexpt = rexpt_ref[...]
        cq = _ssd_chunk_common(cfg, dt_ref[...], bias_ref[...], a_ref[...], c_idx, ltri_v, rexp_ref[...])
        xs = x_ref[...]
        dyv = dy_ref[...].astype(F32)
        s_in = sin_ref[0]
        d_s = ds_scr[...]
        xdt_f = xs * cq["DT"]
        xdt = xdt_f.astype(BF16)
        xw_f = xdt_f * cq["W0"]
        xw = xw_f.astype(BF16)
        lane = lax.broadcasted_iota(jnp.int32, (q, LANE), 1)
        sub = lax.broadcasted_iota(jnp.int32, (LANE, q), 0)

        dd_ref[...] += jnp.sum(dyv * xs, axis=0, keepdims=True)
        dy0 = dyv * cq["E"]
        dcs = jnp.zeros((q, LANE), F32)
        dcs_t = jnp.zeros((LANE, q), F32)
        for g in range(g_n):
            bg_f = b_ref[:, g * st:(g + 1) * st]
            cg_f = c_ref[:, g * st:(g + 1) * st]
            bg = bg_f.astype(BF16)
            cg = cg_f.astype(BF16)
            gs = slice(g * gw, (g + 1) * gw)
            gmat = _nt(cg, bg)
            gmat_t = _nt(bg, cg)
            sing = s_in[:, gs].astype(BF16)
            dsg = d_s[:, gs].astype(BF16)
            y0 = _nn(cg, sing)
            dxw = _nn(bg, dsg)
            d_bg = _nt(xw[:, gs], dsg)
            d_cg = _nt(dy0[:, gs].astype(BF16), sing)
            ds_in_g = _tn(cg, dy0[:, gs].astype(BF16))
            dg = jnp.zeros((q, q), F32)
            dxdt_g = []
            for tt in range(tiles_per_group):
                tile = g * tiles_per_group + tt
                ts = slice(tile * LANE, (tile + 1) * LANE)
                xt = xdt[:, ts]
                dyt = dyv[:, ts]
                dyhs, lmats, mts = [], [], []
                for hh in range(hpt):
                    lmat, lmat_t = _head_l(cq, tile * hpt + hh, tri, tri_t)
                    inhead = jnp.logical_and(lane >= hh * cfg.hd, lane < (hh + 1) * cfg.hd)
                    dyhs.append(jnp.where(inhead, dyt, 0.0).astype(BF16))
                    lmats.append(lmat)
                    mts.append((gmat_t * lmat_t).astype(BF16))
                dy_stack = jnp.concatenate(dyhs, axis=0)
                dm_all = _nt(dy_stack, xt)
                for hh in range(hpt):
                    h = tile * hpt + hh
                    dm = dm_all[hh * q:(hh + 1) * q, :]
                    dg = dg + dm * lmats[hh]
                    qm = dm * gmat * lmats[hh]
                    rs = jnp.sum(qm, axis=1, keepdims=True)
                    csum = jnp.sum(qm, axis=0, keepdims=True)
                    dcs = dcs + jnp.where(lane == h, rs, 0.0)
                    dcs_t = dcs_t + jnp.where(sub == h, csum, 0.0)
                dxdt_g.append(_nn(jnp.concatenate(mts, axis=1), dy_stack))
            dxdt_diag = jnp.concatenate(dxdt_g, axis=1) if len(dxdt_g) > 1 else dxdt_g[0]
            dgb = dg.astype(BF16)
            d_cg = d_cg + _nn(dgb, bg)
            d_bg = d_bg + _tn(dgb, cg)
            dx_ref[:, inner + g * st:inner + (g + 1) * st] = d_bg
            dx_ref[:, inner + bcw + g * st:inner + bcw + (g + 1) * st] = d_cg
            dxdt = dxdt_diag + dxw * cq["W0"][:, gs]
            dx_ref[:, gs] = dyv[:, gs] * d_ref[:, gs] + dxdt * cq["DT"][:, gs]
            rt = rexpt[gs, :]
            dcs = dcs + red(dyv[:, gs] * y0 * cq["E"][:, gs], rt)
            r_w = red(dxw * xw_f[:, gs], rt)
            dcs = dcs - r_w
            dcs_last_g = jnp.sum(r_w, axis=0, keepdims=True)
            ddec = red(jnp.broadcast_to(jnp.sum(d_s[:, gs] * s_in[:, gs], axis=0, keepdims=True), (8, gw)), rt)[0:1, :]
            dcs_last_g = dcs_last_g + ddec * cq["decay"]
            dcs = dcs + jnp.where(lax.broadcasted_iota(jnp.int32, (q, LANE), 0) == q - 1, dcs_last_g, 0.0)
            ddt_part = red(dxdt * xs[:, gs], rt)
            if g == 0:
                ddt = ddt_part
            else:
                ddt = ddt + ddt_part
            ds_scr[:, gs] = d_s[:, gs] * cq["DEC"][:, gs] + ds_in_g
        dcs = dcs - dcs_t.T
        dadt = _sel_dot(dcs, ltri_v, left=True, trans=True)
        ddt = ddt + dadt * a_ref[...]
        da_ref[...] += jnp.sum(dadt * cq["dt"], axis=0, keepdims=True)
        draw = jnp.where(cq["live"], ddt * jax.nn.sigmoid(cq["pre"]), 0.0)
        ddt_ref[...] = draw
        dbias_ref[...] += jnp.sum(draw, axis=0, keepdims=True)

    def rowblk(width, col):
        return pl.BlockSpec((q, width), lambda b, s: (b * nc + nc - 1 - s, col))

    def const(shape):
        return pl.BlockSpec(shape, lambda b, s: (0, 0))

    bcol = inner // bcw
    outs = pl.pallas_call(
        body, name=name, grid=(cfg.bsz, nc),
        in_specs=[rowblk(inner, 0), rowblk(bcw, bcol), rowblk(bcw, bcol + 1), rowblk(LANE, cfg.dtt),
                  const((1, LANE)), const((1, LANE)), const((1, inner)), const((q, q)), const((LANE, inner)),
                  const((inner, LANE)),
                  pl.BlockSpec((1, st, inner), lambda b, s: (b * nc + nc - 1 - s, 0, 0)), rowblk(inner, 0)],
        out_specs=[rowblk(cfg.conv_dim, 0), rowblk(LANE, 0),
                   const((1, inner)), const((1, LANE)), const((1, LANE))],
        out_shape=[_sds((cfg.t, cfg.conv_dim), F32),
                   _sds((cfg.t, LANE), F32), _sds((1, inner), F32), _sds((1, LANE), F32), _sds((1, LANE), F32)],
        scratch_shapes=[pltpu.VMEM((st, inner), F32)], compiler_params=_cp(),
    )(xc, xc, xc, small, dt_bias, avec, dexp, ltri, rexp, rexp_t, sin, dy)
    return outs


def tail_fwd(cfg, y, z, w, *, name):
    t, inner, gw = cfg.t, cfg.inner, cfg.gw
    tr = _pick(t, 272, 16)

    def body(y_ref, z_ref, w_ref, o_ref):
        for g in range(cfg.groups):
            gs = slice(g * gw, (g + 1) * gw)
            yg = y_ref[:, gs].astype(F32) * _silu(z_ref[:, gs].astype(F32))
            r = lax.rsqrt(jnp.mean(yg * yg, axis=-1, keepdims=True) + EPS)
            o_ref[:, gs] = (yg * r * w_ref[:, gs]).astype(BF16)

    row = pl.BlockSpec((tr, inner), lambda i: (i, 0))
    return pl.pallas_call(
        body, name=name, grid=(t // tr,), in_specs=[row, row, pl.BlockSpec((1, inner), lambda i: (0, 0))],
        out_specs=row, out_shape=_sds((t, inner), BF16), compiler_params=_cp(),
    )(y, z, w.reshape(1, inner))


def tail_bwd(cfg, do, y, z, w, *, name):
    t, inner, gw = cfg.t, cfg.inner, cfg.gw
    tr = _pick(t, 272, 16)

    def body(do_ref, y_ref, z_ref, w_ref, dy_ref, dz_ref, dw_ref):
        @pl.when(pl.program_id(0) == 0)
        def _():
            dw_ref[...] = jnp.zeros_like(dw_ref)

        for g in range(cfg.groups):
            gs = slice(g * gw, (g + 1) * gw)
            yv = y_ref[:, gs].astype(F32)
            zv = z_ref[:, gs].astype(F32)
            dov = do_ref[:, gs].astype(F32)
            sz = _silu(zv)
            yg = yv * sz
            r = lax.rsqrt(jnp.mean(yg * yg, axis=-1, keepdims=True) + EPS)
            xh = yg * r
            gg = dov * w_ref[:, gs]
            dyg = r * (gg - xh * jnp.mean(gg * xh, axis=-1, keepdims=True))
            dw_ref[:, gs] += jnp.sum(dov * xh, axis=0, keepdims=True)
            dy_ref[:, gs] = (dyg * sz).astype(BF16)
            dz_ref[:, gs] = (dyg * yv * _dsilu(zv)).astype(BF16)

    row = pl.BlockSpec((tr, inner), lambda i: (i, 0))
    vec = pl.BlockSpec((1, inner), lambda i: (0, 0))
    dy, dz, dw = pl.pallas_call(
        body, name=name, grid=(t // tr,), in_specs=[row, row, row, vec], out_specs=[row, row, vec],
        out_shape=[_sds((t, inner), BF16), _sds((t, inner), BF16), _sds((1, inner), F32)], compiler_params=_cp(),
    )(do, y, z, w.reshape(1, inner))
    return dy, dz, dw[0]


def rope_tables(cfg):
    half = cfg.rope // 2
    pos = np.maximum(np.arange(cfg.lp) - cfg.pad, 0).astype(np.float32)
    inv = ROPE_THETA ** (-jnp.arange(0, cfg.rope, 2, dtype=F32) / cfg.rope)
    ang = jnp.asarray(pos)[:, None] * inv[None, :]
    cos, sin = jnp.cos(ang), jnp.sin(ang)
    zero = jnp.zeros((cfg.lp, LANE - 2 * half), F32)
    zh = jnp.zeros((cfg.lp, half), F32)
    ctab = jnp.concatenate([cos, cos, zero], axis=1)
    s1 = jnp.concatenate([-sin, zh, zero], axis=1)
    s2 = jnp.concatenate([zh, sin, zero], axis=1)
    return ctab, s1, s2


def _rope(x, c, s1, s2, half):
    return x * c + pltpu.roll(x, LANE - half, 1) * s1 + pltpu.roll(x, half, 1) * s2


def _rope_t(dy, c, s1, s2, half):
    return dy * c + pltpu.roll(dy * s1, half, 1) + pltpu.roll(dy * s2, LANE - half, 1)


def _attn_scale(cfg):
    return (cfg.nope + cfg.rope) ** -0.5


def rope_fwd(cfg, qf, small, tabs, *, name):
    t, qw, lp = cfg.t, cfg.qw, cfg.lp
    tr = _pick(lp, 544, 16)
    nrb = lp // tr
    half = cfg.rope // 2
    scale = _attn_scale(cfg)

    def body(q_ref, k_ref, c_ref, s1_ref, s2_ref, qo_ref, ko_ref):
        c, s1, s2 = c_ref[...], s1_ref[...], s2_ref[...]
        for h in range(cfg.mh):
            a = h * 2 * LANE
            qo_ref[:, a:a + LANE] = (q_ref[:, a:a + LANE].astype(F32) * scale).astype(BF16)
            qo_ref[:, a + LANE:a + 2 * LANE] = (
                _rope(q_ref[:, a + LANE:a + 2 * LANE].astype(F32), c, s1, s2, half) * scale).astype(BF16)
        ko_ref[...] = _rope(k_ref[...], c, s1, s2, half).astype(BF16)

    tab = pl.BlockSpec((tr, LANE), lambda i: (i % nrb, 0))
    return pl.pallas_call(
        body, name=name, grid=(t // tr,),
        in_specs=[pl.BlockSpec((tr, qw), lambda i: (i, 0)), pl.BlockSpec((tr, LANE), lambda i: (i, cfg.kt)), tab, tab, tab],
        out_specs=[pl.BlockSpec((tr, qw), lambda i: (i, 0)), pl.BlockSpec((tr, LANE), lambda i: (i, 0))],
        out_shape=[_sds((t, qw), BF16), _sds((t, LANE), BF16)], compiler_params=_cp(),
    )(qf, small, *tabs)


def rope_bwd(cfg, dq, dkpe, tabs, *, name):
    t, qw, lp = cfg.t, cfg.qw, cfg.lp
    tr = _pick(lp, 544, 16)
    nrb = lp // tr
    half = cfg.rope // 2
    scale = _attn_scale(cfg)

    def body(dq_ref, dk_ref, c_ref, s1_ref, s2_ref, qo_ref, ko_ref):
        c, s1, s2 = c_ref[...], s1_ref[...], s2_ref[...]
        for h in range(cfg.mh):
            a = h * 2 * LANE
            qo_ref[:, a:a + LANE] = (dq_ref[:, a:a + LANE].astype(F32) * scale).astype(BF16)
            qo_ref[:, a + LANE:a + 2 * LANE] = _rope_t(
                dq_ref[:, a + LANE:a + 2 * LANE].astype(F32) * scale, c, s1, s2, half).astype(BF16)
        ko_ref[...] = _rope_t(dk_ref[...], c, s1, s2, half)

    tab = pl.BlockSpec((tr, LANE), lambda i: (i % nrb, 0))
    return pl.pallas_call(
        body, name=name, grid=(t // tr,),
        in_specs=[pl.BlockSpec((tr, qw), lambda i: (i, 0)), pl.BlockSpec((tr, LANE), lambda i: (i, 0)),
                  tab, tab, tab],
        out_specs=[pl.BlockSpec((tr, qw), lambda i: (i, 0)), pl.BlockSpec((tr, LANE), lambda i: (i, 0))],
        out_shape=[_sds((t, qw), BF16), _sds((t, LANE), F32)], compiler_params=_cp(),
    )(dq, dkpe, *tabs)


def _q_blocks(cfg):
    bounds = [0, cfg.chunk] + list(range(cfg.chunk + 256, cfg.lp + 1, 256))
    assert bounds[-1] == cfg.lp, "SEQ must be a multiple of 256"
    return list(zip(bounds[:-1], bounds[1:]))


def _attn_mask(cfg, qs, qe):
    rows = qs + lax.broadcasted_iota(jnp.int32, (qe - qs, qe), 0)
    cols = lax.broadcasted_iota(jnp.int32, (qe - qs, qe), 1)
    return jnp.logical_and(cols <= rows, jnp.logical_or(cols >= cfg.pad, rows < cfg.pad))


def _max_q_block(cfg):
    return max(qe - qs for qs, qe in _q_blocks(cfg))


def _masked_scores(cfg, q, k2, qs, qe, s_scr):
    bq, n = qe - qs, qe
    s_scr[0:bq, 0:n] = _nt(q, k2)
    if qs == 0:
        s_scr[0:bq, 0:n] = jnp.where(_attn_mask(cfg, 0, qe), s_scr[0:bq, 0:n], MASK_VALUE)
    else:
        assert qs >= cfg.chunk and cfg.pad < LANE
        cols = lax.broadcasted_iota(jnp.int32, (bq, LANE), 1)
        s_scr[0:bq, 0:LANE] = jnp.where(cols >= cfg.pad, s_scr[0:bq, 0:LANE], MASK_VALUE)
        r = lax.broadcasted_iota(jnp.int32, (bq, bq), 0)
        c = lax.broadcasted_iota(jnp.int32, (bq, bq), 1)
        s_scr[0:bq, qs:qe] = jnp.where(c <= r, s_scr[0:bq, qs:qe], MASK_VALUE)
    return s_scr[0:bq, 0:n]


def attn_fwd(cfg, qr, kv, kpe, *, name):
    lp, t, mh = cfg.lp, cfg.t, cfg.mh
    assert mh <= LANE
    blocks = _q_blocks(cfg)

    def body(q_ref, kv_ref, kp_ref, o_ref, l_ref, s_scr):
        h = pl.program_id(1)

        @pl.when(h == 0)
        def _():
            l_ref[...] = jnp.zeros_like(l_ref)

        for qs, qe in blocks:
            n = qe
            q = q_ref[qs:qe, :]
            k2 = jnp.concatenate([kv_ref[0:n, 0:LANE], kp_ref[0:n, :]], axis=1)
            s = _masked_scores(cfg, q, k2, qs, qe, s_scr)
            m = jnp.max(s, axis=-1, keepdims=True)
            p = jnp.exp(s - m)
            l = jnp.sum(p, axis=-1, keepdims=True)
            o_ref[qs:qe, :] = (_nn(p.astype(BF16), kv_ref[0:n, LANE:2 * LANE]) * (1.0 / l)).astype(BF16)
            lane = lax.broadcasted_iota(jnp.int32, (qe - qs, LANE), 1)
            l_ref[qs:qe, :] = jnp.where(lane == h, m + jnp.log(l), l_ref[qs:qe, :])

    hb = pl.BlockSpec((lp, 2 * LANE), lambda b, h: (b, h))
    ob = pl.BlockSpec((lp, LANE), lambda b, h: (b, h))
    return pl.pallas_call(
        body, name=name, grid=(cfg.bsz, mh),
        in_specs=[hb, hb, pl.BlockSpec((lp, LANE), lambda b, h: (b, 0))],
        out_specs=[ob, pl.BlockSpec((lp, LANE), lambda b, h: (b, 0))],
        out_shape=[_sds((t, mh * LANE), BF16), _sds((t, LANE), F32)],
        scratch_shapes=[pltpu.VMEM((_max_q_block(cfg), lp), F32)], compiler_params=_cp(),
    )(qr, kv, kpe)


def attn_bwd(cfg, qr, kv, kpe, o, lse, do, *, name):
    lp, t, mh = cfg.lp, cfg.t, cfg.mh
    blocks = _q_blocks(cfg)

    def body(q_ref, kv_ref, kp_ref, o_ref, l_ref, do_ref, dq_ref, dkv_ref, dkp_ref, dk_acc, dv_acc, s_scr):
        dk_acc[...] = jnp.zeros_like(dk_acc)
        dv_acc[...] = jnp.zeros_like(dv_acc)
        for qs, qe in blocks:
            n = qe
            q = q_ref[qs:qe, :]
            k2 = jnp.concatenate([kv_ref[0:n, 0:LANE], kp_ref[0:n, :]], axis=1)
            dob = do_ref[qs:qe, :].astype(BF16)
            delta = jnp.sum(dob.astype(F32) * o_ref[qs:qe, :].astype(F32), axis=-1, keepdims=True)
            s = _masked_scores(cfg, q, k2, qs, qe, s_scr)
            lane = lax.broadcasted_iota(jnp.int32, (qe - qs, LANE), 1)
            lse = jnp.sum(jnp.where(lane == pl.program_id(1), l_ref[qs:qe, :], 0.0), axis=-1, keepdims=True)
            p = jnp.exp(s - lse)
            dp = _nt(dob, kv_ref[0:n, LANE:2 * LANE])
            ds = (p * (dp - delta)).astype(BF16)
            dq_ref[qs:qe, :] = _nn(ds, k2).astype(BF16)
            dv_acc[0:n, :] += _tn(p.astype(BF16), dob)
            dk_acc[0:n, :] += _tn(ds, q)
        dkv_ref[:, 0:LANE] = dk_acc[:, 0:LANE].astype(BF16)
        dkv_ref[:, LANE:2 * LANE] = dv_acc[...].astype(BF16)
        @pl.when(pl.program_id(1) == 0)
        def _():
            dkp_ref[...] = dk_acc[:, LANE:2 * LANE]

        @pl.when(pl.program_id(1) > 0)
        def _():
            dkp_ref[...] += dk_acc[:, LANE:2 * LANE]

    hb = pl.BlockSpec((lp, 2 * LANE), lambda b, h: (b, h))
    ob = pl.BlockSpec((lp, LANE), lambda b, h: (b, h))
    return pl.pallas_call(
        body, name=name, grid=(cfg.bsz, mh),
        in_specs=[hb, hb, pl.BlockSpec((lp, LANE), lambda b, h: (b, 0)), ob,
                  pl.BlockSpec((lp, LANE), lambda b, h: (b, 0)), ob],
        out_specs=[hb, hb, pl.BlockSpec((lp, LANE), lambda b, h: (b, 0))],
        out_shape=[_sds((t, cfg.qw), BF16), _sds((t, mh * 2 * LANE), BF16), _sds((t, LANE), F32)],
        scratch_shapes=[pltpu.VMEM((lp, 2 * LANE), F32), pltpu.VMEM((lp, LANE), F32),
                        pltpu.VMEM((_max_q_block(cfg), lp), F32)], compiler_params=_cp(),
    )(qr, kv, kpe, o, lse, do)


def _live_rows(cfg, tr, shape):
    rows = pl.program_id(1) * tr + lax.broadcasted_iota(jnp.int32, shape, 0)
    return rows >= cfg.pad


def gate_fwd(cfg, ya, yb, g, *, name):
    d, lp = cfg.d, cfg.lp
    tr = _pick(lp, 544, 16)
    nrb = lp // tr

    def body(ya_ref, yb_ref, ga_ref, gb_ref, o_ref):
        f = lambda ref: ref[...].astype(F32)
        mix = jax.nn.sigmoid(f(ga_ref)) * f(ya_ref) + jax.nn.sigmoid(f(gb_ref)) * f(yb_ref)
        o_ref[...] = jnp.where(_live_rows(cfg, tr, mix.shape), mix, 0.0).astype(BF16)

    row = pl.BlockSpec((tr, d), lambda b, j: (b * nrb + j, 0))
    row1 = pl.BlockSpec((tr, d), lambda b, j: (b * nrb + j, 1))
    return pl.pallas_call(
        body, name=name, grid=(cfg.bsz, nrb), in_specs=[row, row, row, row1], out_specs=row,
        out_shape=_sds((cfg.t, d), BF16), compiler_params=_cp(),
    )(ya, yb, g, g)


def gate_bwd(cfg, dmix, ya, yb, g, *, name):
    d, lp = cfg.d, cfg.lp
    tr = _pick(lp, 544, 16)
    nrb = lp // tr

    def body(dm_ref, ya_ref, yb_ref, ga_ref, gb_ref, dya_ref, dyb_ref, dg_ref):
        dm = dm_ref[...].astype(F32)
        dm = jnp.where(_live_rows(cfg, tr, dm.shape), dm, 0.0)
        sa = jax.nn.sigmoid(ga_ref[...].astype(F32))
        sb = jax.nn.sigmoid(gb_ref[...].astype(F32))
        dya_ref[...] = (dm * sa).astype(BF16)
        dyb_ref[...] = (dm * sb).astype(BF16)
        dg_ref[:, 0:d] = (dm * ya_ref[...].astype(F32) * sa * (1.0 - sa)).astype(BF16)
        dg_ref[:, d:2 * d] = (dm * yb_ref[...].astype(F32) * sb * (1.0 - sb)).astype(BF16)

    row = pl.BlockSpec((tr, d), lambda b, j: (b * nrb + j, 0))
    row1 = pl.BlockSpec((tr, d), lambda b, j: (b * nrb + j, 1))
    row2 = pl.BlockSpec((tr, 2 * d), lambda b, j: (b * nrb + j, 0))
    return pl.pallas_call(
        body, name=name, grid=(cfg.bsz, nrb), in_specs=[row, row, row, row, row1], out_specs=[row, row, row2],
        out_shape=[_sds((cfg.t, d), BF16), _sds((cfg.t, d), BF16), _sds((cfg.t, 2 * d), BF16)], compiler_params=_cp(),
    )(dmix, ya, yb, g, g)


def loss_head(cfg, h, target, w, *, name):
    d, q, nc = cfg.d, cfg.chunk, cfg.nchunks
    tpb = cfg.seq // q

    def body(h_ref, t_ref, w_ref, loss_ref, dh_ref, dw_ref, dhb_ref):
        j = pl.program_id(1)

        @pl.when(jnp.logical_and(j == 0, pl.program_id(0) == 0))
        def _():
            loss_ref[...] = jnp.zeros_like(loss_ref)
            dw_ref[...] = jnp.zeros_like(dw_ref)

        @pl.when(j == 0)
        def _():
            dh_ref[...] = jnp.zeros_like(dh_ref)
            dhb_ref[...] = jnp.zeros_like(dhb_ref)

        @pl.when(j > 0)
        def _():
            xv = h_ref[...]
            r = lax.rsqrt(jnp.mean(xv * xv, axis=-1, keepdims=True) + EPS)
            xh = xv * r
            err = xh * w_ref[...] - t_ref[...]
            loss_ref[...] += 0.5 * jnp.sum(jnp.sum(err * err, axis=-1, keepdims=True) / d, axis=0, keepdims=True)
            dy = err * (1.0 / d)
            g = dy * w_ref[...]
            dh = r * (g - xh * jnp.mean(g * xh, axis=-1, keepdims=True))
            dh_ref[...] = dh
            dhb_ref[...] = dh.astype(BF16)
            dw_ref[...] += jnp.sum(dy * xh, axis=0, keepdims=True)

    row = pl.BlockSpec((q, d), lambda b, j: (b * nc + j, 0))
    loss, dh, dw, dhb = pl.pallas_call(
        body, name=name, grid=(cfg.bsz, nc),
        in_specs=[row, pl.BlockSpec((q, d), lambda b, j: (b * tpb + jnp.maximum(j - 1, 0), 0)),
                  pl.BlockSpec((1, d), lambda b, j: (0, 0))],
        out_specs=[pl.BlockSpec((8, LANE), lambda b, j: (0, 0)), row, pl.BlockSpec((1, d), lambda b, j: (0, 0)), row],
        out_shape=[_sds((8, LANE), F32), _sds((cfg.t, d), F32), _sds((1, d), F32), _sds((cfg.t, d), BF16)],
        compiler_params=_cp(),
    )(h, target, w.reshape(1, d))
    return loss[0, 0], (dh, dhb), dw[0]


def _rows_tile(r, c):
    return _pick(r, max(8, (1 << 18) // max(c, 1) // 8 * 8), 8)


def _adam_update(w, g, m, v):
    c1 = 1.0 - ADAM_B1 ** ADAM_STEP
    c2 = 1.0 - ADAM_B2 ** ADAM_STEP
    mn = ADAM_B1 * m + (1.0 - ADAM_B1) * g
    vn = ADAM_B2 * v + (1.0 - ADAM_B2) * (g * g)
    delta = -ADAM_LR * ((mn / c1) / (jnp.sqrt(vn / c2) + ADAM_EPS) + ADAM_WD * w)
    return delta, mn, vn


def adamw_layer(w, m, v, g, li, prev, dep, *, name):
    _, r, c = w.shape
    tr = _rows_tile(r, c)

    def body(*refs):
        w_ref, m_ref, v_ref, g_ref = refs[:4]
        go_ref, d_ref, mo_ref, vo_ref = refs[-4:]
        gv = g_ref[...]
        delta, mn, vn = _adam_update(w_ref[0], gv, m_ref[0], v_ref[0])
        go_ref[0] = gv
        d_ref[0] = delta
        mo_ref[0] = mn
        vo_ref[0] = vn

    if tr * c * 4 >= (1 << 16):
        steps = r // tr
        blk3 = pl.BlockSpec((1, tr, c), lambda i: (li, i, 0))
        blk2 = pl.BlockSpec((tr, c), lambda i: (i, 0))
    else:
        tc = _pick(c, max(LANE, (1 << 18) // r // LANE * LANE), LANE)
        steps = c // tc
        blk3 = pl.BlockSpec((1, r, tc), lambda i: (li, 0, i))
        blk2 = pl.BlockSpec((r, tc), lambda i: (0, i))
    anyspec = pl.BlockSpec(memory_space=pl.ANY)
    in_specs = [blk3, blk3, blk3, blk2, anyspec]
    args = [w, m, v, g, dep]
    aliases = {}
    if prev is not None:
        in_specs += [anyspec] * 4
        args += list(prev)
        aliases = {5 + i: i for i in range(4)}
    return pl.pallas_call(
        body, name=name, grid=(steps,), in_specs=in_specs, out_specs=[blk3] * 4,
        out_shape=[_sds(w.shape, F32)] * 4, input_output_aliases=aliases, compiler_params=_cp(),
    )(*args)


def pair_add(g4, other, half, *, name):
    n, _, r, c = g4.shape
    tr = _rows_tile(r, c)

    def body(h_ref, a_ref, b_ref, o_ref):
        o_ref[0] = (a_ref[0, 0].astype(F32) + b_ref[0].astype(F32)).astype(BF16)

    blk = pl.BlockSpec((1, tr, c), lambda j, i, h: (j, i, 0))
    grid_spec = pltpu.PrefetchScalarGridSpec(
        num_scalar_prefetch=1, grid=(n, r // tr),
        in_specs=[pl.BlockSpec((1, 1, tr, c), lambda j, i, h: (j, h[0], i, 0)), blk], out_specs=blk)
    return pl.pallas_call(body, name=name, grid_spec=grid_spec, out_shape=_sds((n, r, c), BF16),
                          compiler_params=_cp())(half, g4, other)


def chip_sum(recv, part, where, *, name):
    n, r, c = recv.shape
    tr = _rows_tile(r, c)

    def body(s_ref, *refs):
        own_ref, o_ref = refs[n], refs[n + 1]
        acc = None
        for j in range(n):
            term = jnp.where(s_ref[0] == j, own_ref[0], refs[j][0]).astype(F32)
            acc = term if acc is None else acc + term
        o_ref[0] = acc

    def slot(j):
        return pl.BlockSpec((1, tr, c), lambda i, s: (jnp.where(s[0] == j, (j + 1) % n, j), i, 0))

    grid_spec = pltpu.PrefetchScalarGridSpec(
        num_scalar_prefetch=1, grid=(r // tr,),
        in_specs=[slot(j) for j in range(n)] + [pl.BlockSpec((1, tr, c), lambda i, s: (s[0], i, 0))],
        out_specs=pl.BlockSpec((1, tr, c), lambda i, s: (s[1], i, 0)))
    return pl.pallas_call(body, name=name, grid_spec=grid_spec, out_shape=_sds((2, r, c), F32),
                          compiler_params=_cp())(where, *([recv] * n), part)


def _coords():
    return lax.axis_index("x"), lax.axis_index("y"), lax.axis_index("c")


def _other_chips(x, y):
    return [(1 - x, y), (x, 1 - y), (1 - x, 1 - y)]


def gather_chips(arrs, *, name):
    n = len(arrs)
    anyspec = pl.BlockSpec(memory_space=pl.ANY)

    def body(*refs):
        ins, outs = refs[:n], refs[n:2 * n]
        send_sems, recv_sems, local_sems = refs[2 * n:]
        x, y, c = _coords()
        me = 2 * x + y
        chips = _other_chips(x, y)
        copies = []
        for k in range(n):
            loc = pltpu.make_async_copy(ins[k], outs[k].at[me], local_sems.at[k])
            loc.start()
            copies.append(loc)
        sends = []
        for k in range(n):
            for j, (px, py) in enumerate(chips):
                cp = pltpu.make_async_remote_copy(
                    src_ref=ins[k], dst_ref=outs[k].at[me], send_sem=send_sems.at[k, j], recv_sem=recv_sems.at[k, j],
                    device_id=(px, py, c), device_id_type=MESH)
                cp.start()
                sends.append(cp)
        for k in range(n):
            for j, (px, py) in enumerate(chips):
                pltpu.make_async_remote_copy(
                    src_ref=ins[k], dst_ref=outs[k].at[2 * px + py], send_sem=send_sems.at[k, j],
                    recv_sem=recv_sems.at[k, j], device_id=(px, py, c), device_id_type=MESH).wait_recv()
        for cp in sends:
            cp.wait_send()
        for cp in copies:
            cp.wait()

    return pl.pallas_call(
        body, name=name, in_specs=[anyspec] * n, out_specs=[anyspec] * n,
        out_shape=[_sds((4,) + a.shape, a.dtype) for a in arrs],
        scratch_shapes=[pltpu.SemaphoreType.DMA((n, 3)), pltpu.SemaphoreType.DMA((n, 3)), pltpu.SemaphoreType.DMA((n,))],
        compiler_params=_cp(has_side_effects=True),
    )(*arrs)


def allreduce_small(vec, after, *, name):
    r, c = vec.shape

    def body(v_ref, after_ref, o_ref, buf, send_sems, recv_sems):
        x, y, cc = _coords()
        me = 4 * x + 2 * y + cc
        buf[me] = v_ref[...]
        sends = []
        flips = [(fx, fy, fc) for fx in (0, 1) for fy in (0, 1) for fc in (0, 1)][1:]
        for j, (fx, fy, fc) in enumerate(flips):
            peer = ((1 - x) if fx else x, (1 - y) if fy else y, (1 - cc) if fc else cc)
            cp = pltpu.make_async_remote_copy(
                src_ref=v_ref, dst_ref=buf.at[me], send_sem=send_sems.at[j], recv_sem=recv_sems.at[j],
                device_id=peer, device_id_type=MESH)
            cp.start()
            sends.append(cp)
        for j, (fx, fy, fc) in enumerate(flips):
            px, py, pc = ((1 - x) if fx else x, (1 - y) if fy else y, (1 - cc) if fc else cc)
            pltpu.make_async_remote_copy(
                src_ref=v_ref, dst_ref=buf.at[4 * px + 2 * py + pc], send_sem=send_sems.at[j],
                recv_sem=recv_sems.at[j], device_id=(px, py, pc), device_id_type=MESH).wait_recv()
        for cp in sends:
            cp.wait_send()
        acc = buf[0]
        for k in range(1, 8):
            acc = acc + buf[k]
        o_ref[...] = acc

    vm = pl.BlockSpec(memory_space=pltpu.VMEM)
    return pl.pallas_call(
        body, name=name, in_specs=[vm, pl.BlockSpec(memory_space=pl.ANY)], out_specs=vm, out_shape=_sds((r, c), F32),
        scratch_shapes=[pltpu.VMEM((8, r, c), F32), pltpu.SemaphoreType.DMA((7,)), pltpu.SemaphoreType.DMA((7,))],
        compiler_params=_cp(has_side_effects=True),
    )(vec, after)


def pair_share(lands, owns, *, name):
    n = len(lands)
    anyspec = pl.BlockSpec(memory_space=pl.ANY)

    def body(*refs):
        ins, own_refs, outs = refs[:n], refs[n:2 * n], refs[2 * n:3 * n]
        send_sems, recv_sems = refs[3 * n:]
        x, y, c = _coords()
        me = 2 * x + y
        sib = (x, y, 1 - c)
        sends = []
        for k in range(n):
            for j, (px, py) in enumerate(_other_chips(x, y)):
                cp = pltpu.make_async_remote_copy(
                    src_ref=ins[k].at[2 * px + py, c], dst_ref=outs[k].at[2 * px + py, c], send_sem=send_sems.at[k, j],
                    recv_sem=recv_sems.at[k, j], device_id=sib, device_id_type=MESH)
                cp.start()
                sends.append(cp)
            cp = pltpu.make_async_remote_copy(
                src_ref=own_refs[k], dst_ref=outs[k].at[me], send_sem=send_sems.at[k, 3], recv_sem=recv_sems.at[k, 3],
                device_id=sib, device_id_type=MESH)
            cp.start()
            sends.append(cp)
        for k in range(n):
            for j, (px, py) in enumerate(_other_chips(x, y)):
                pltpu.make_async_remote_copy(
                    src_ref=ins[k].at[2 * px + py, c], dst_ref=outs[k].at[2 * px + py, 1 - c],
                    send_sem=send_sems.at[k, j], recv_sem=recv_sems.at[k, j], device_id=sib,
                    device_id_type=MESH).wait_recv()
            pltpu.make_async_remote_copy(
                src_ref=own_refs[k], dst_ref=outs[k].at[me], send_sem=send_sems.at[k, 3], recv_sem=recv_sems.at[k, 3],
                device_id=sib, device_id_type=MESH).wait_recv()
        for cp in sends:
            cp.wait_send()

    return pl.pallas_call(
        body, name=name, in_specs=[anyspec] * (2 * n), out_specs=[anyspec] * n,
        out_shape=[_sds(a.shape, a.dtype) for a in lands], input_output_aliases={k: k for k in range(n)},
        scratch_shapes=[pltpu.SemaphoreType.DMA((n, 4)), pltpu.SemaphoreType.DMA((n, 4))],
        compiler_params=_cp(has_side_effects=True),
    )(*lands, *owns)


_HBM = pl.BlockSpec(memory_space=pltpu.HBM)
_SEM = pl.BlockSpec(memory_space=pltpu.SEMAPHORE)


_COPIES_PER_ARRAY = {"gather": 3, "scatter": 3, "share": 4, "exchange": 4, "fill": 1}


def _ici_copies(kind, srcs, lands, send_sems, recv_sems):
    x, y, c = _coords()
    me = 2 * x + y
    per = _COPIES_PER_ARRAY[kind]
    sends, recvs = [], []
    for k in range(len(srcs)):
        triples = []
        for j, (px, py) in enumerate(_other_chips(x, y)):
            peer = 2 * px + py
            if kind == "gather":
                triples.append((srcs[k].at[c], lands[k].at[me, c], lands[k].at[peer, c], (px, py, c)))
            elif kind == "scatter":
                triples.append((srcs[k].at[peer], lands[k].at[me], lands[k].at[peer], (px, py, c)))
            elif kind == "share":
                triples.append((lands[k].at[peer, c], lands[k].at[peer, c], lands[k].at[peer, 1 - c], (x, y, 1 - c)))
        if kind == "share":
            triples.append((srcs[k], lands[k].at[me], lands[k].at[me], (x, y, 1 - c)))
        if kind == "exchange":
            triples = [(srcs[k].at[j, 1 - c], lands[k].at[j], lands[k].at[j], (x, y, 1 - c)) for j in range(4)]
        if kind == "fill":
            triples = [(lands[k].at[c], lands[k].at[c], lands[k].at[1 - c], (x, y, 1 - c))]
        for j, (src, there, here, dev) in enumerate(triples):
            sem = per * k + j
            mk = functools.partial(pltpu.make_async_remote_copy, src_ref=src, send_sem=send_sems.at[sem],
                                   recv_sem=recv_sems.at[sem], device_id=dev, device_id_type=MESH)
            sends.append(mk(dst_ref=there))
            recvs.append(mk(dst_ref=here))
    return sends, recvs


def ici_start(kind, srcs, lands, after, *, name):
    n = len(srcs)

    def body(*refs):
        src_refs, land_refs = refs[:n], refs[n:2 * n]
        send_sems, recv_sems = refs[2 * n + 1], refs[2 * n + 2]
        token = refs[-1]
        sends, _ = _ici_copies(kind, src_refs, land_refs, send_sems, recv_sems)
        for cp in sends:
            cp.start()
        token[...] = jnp.zeros_like(token)

    both = list(srcs) + list(lands)
    out = pl.pallas_call(
        body, name=name,
        in_specs=[_HBM] * (2 * n) + [pl.BlockSpec(memory_space=pl.ANY)],
        out_shape=(pltpu.SemaphoreType.DMA((_COPIES_PER_ARRAY[kind] * n,)),
                   pltpu.SemaphoreType.DMA((_COPIES_PER_ARRAY[kind] * n,)),
                   *[pltpu.HBM(a.shape, a.dtype) for a in both], _sds((8, LANE), F32)),
        out_specs=(_SEM, _SEM, *([_HBM] * (2 * n)), pl.BlockSpec(memory_space=pltpu.VMEM)),
        input_output_aliases={i: 2 + i for i in range(2 * n)},
        compiler_params=_cp(has_side_effects=pltpu.SideEffectType.DATAFLOW_SIDE_EFFECTING),
    )(*[pltpu.with_memory_space_constraint(a, pltpu.HBM) for a in both], after)
    return out[0], out[1], list(out[2:2 + n]), list(out[2 + n:2 + 2 * n]), out[-1]


def ici_wait(kind, started, after, *, name):
    send_sems, recv_sems, srcs, lands, _ = started
    n = len(srcs)

    def body(*refs):
        src_refs, land_refs = refs[:n], refs[n:2 * n]
        sends, recvs = _ici_copies(kind, src_refs, land_refs, refs[2 * n], refs[2 * n + 1])
        for cp in sends:
            cp.wait_send()
        for cp in recvs:
            cp.wait_recv()

    both = list(srcs) + list(lands)
    out = pl.pallas_call(
        body, name=name,
        in_specs=[_HBM] * (2 * n) + [_SEM, _SEM, pl.BlockSpec(memory_space=pl.ANY)],
        out_shape=tuple(pltpu.HBM(a.shape, a.dtype) for a in both), out_specs=tuple([_HBM] * (2 * n)),
        input_output_aliases={i: i for i in range(2 * n)},
        compiler_params=_cp(has_side_effects=pltpu.SideEffectType.DATAFLOW_SIDE_EFFECTING),
    )(*both, send_sems, recv_sems, after)
    return list(out[:n]), list(out[n:])


BIG = ["w_in", "w_uq", "w_ukv", "w_branch_ssm", "w_branch_mla", "w_out", "w_mlp_up", "w_mlp_down"]
COL_SHARDED = {"w_in", "w_uq", "w_ukv", "w_mlp_up"}
SMALL_REPL = ["norm_mix_w", "conv_b", "dt_bias", "a_log", "d_skip", "ssm_norm_w", "q_norm_w", "kv_norm_w", "norm_mlp_w"]


def _unshard_layer(name, g):
    _, r, c = g.shape
    if name in COL_SHARDED:
        return jnp.transpose(g, (1, 0, 2)).reshape(r, 4 * c)
    return g.reshape(4 * r, c)


def _to_shards(name, full):
    r, c = full.shape
    if name in COL_SHARDED:
        return jnp.transpose(full.reshape(r, 4, c // 4), (1, 0, 2))
    return full.reshape(4, r // 4, c)


REST = [k for k in BIG if k != "w_in"]


def prep_layer(cfg, w):
    out = {}
    if "w_in" in w:
        sp = np.cumsum(cfg.in_splits)[:-1].tolist()
        z, xbc, dt, cq, ckv, kr, gs, gm = jnp.split(w["w_in"], sp, axis=1)
        zpad = lambda n: jnp.zeros((cfg.d, n), z.dtype)
        out.update(w_z=z, w_xbc=xbc, w_g=jnp.concatenate([gs, gm], axis=1),
                   w_s=jnp.concatenate([cq, ckv, kr, zpad(LANE - cfg.rope), dt, zpad(LANE - cfg.heads)], axis=1))
    if "w_uq" in w:
        out.update(
            w_uq=jnp.pad(w["w_uq"].reshape(cfg.ql, cfg.mh, cfg.nope + cfg.rope),
                         ((0, 0), (0, 0), (0, 2 * LANE - cfg.nope - cfg.rope))).reshape(cfg.ql, cfg.qw),
            w_ukv=w["w_ukv"], w_bs=w["w_branch_ssm"], w_bm=w["w_branch_mla"], w_out=w["w_out"],
            w_up=w["w_mlp_up"], w_down=w["w_mlp_down"])
    return {k: v.astype(BF16) for k, v in out.items()}


def unprep_grads(cfg, g):
    out = {}
    if "w_s" in g:
        ql, kvl = cfg.ql, cfg.kvl
        ds_ = g["w_s"]
        cq, ckv = ds_[:, :ql], ds_[:, ql:ql + kvl]
        kr = ds_[:, ql + kvl:ql + kvl + cfg.rope]
        dt = ds_[:, ql + kvl + LANE:ql + kvl + LANE + cfg.heads]
        out["w_in"] = jnp.concatenate([g["w_z"], g["w_xbc"], dt, cq, ckv, kr, g["w_g"]], axis=1)
    if "w_uq" in g:
        out.update(
            w_uq=g["w_uq"].reshape(cfg.ql, cfg.mh, 2 * LANE)[:, :, :cfg.nope + cfg.rope].reshape(cfg.ql, -1),
            w_ukv=g["w_ukv"], w_branch_ssm=g["w_bs"], w_branch_mla=g["w_bm"],
            w_out=g["w_out"], w_mlp_up=g["w_up"], w_mlp_down=g["w_down"])
    return out


def _hook(hooks, name, arg):
    if hooks and name in hooks:
        return hooks[name](arg)[0, 0]
    return 0.0


def layer_fwd(cfg, h, pw, sm, tabs, li, hooks=None):
    n = lambda s: f"l{li}_{s}"
    u = rmsnorm_fwd(h, sm["norm_mix_w"], name=n("norm_mix"))
    z, xbc, g, small = matmul_multi(u, [pw["w_z"], pw["w_xbc"], pw["w_g"], pw["w_s"]], (BF16, F32, BF16, F32),
                                    name=n("in_proj"))
    xc, dsilu = conv_fwd(cfg, xbc, sm["conv_w"], sm["conv_b"], name=n("conv"))
    dt_bias = sm["dt_bias_p"] + _hook(hooks, "after_conv", xc)
    y, sin = ssd_fwd(cfg, xc, small, dt_bias, sm["avec"], sm["dexp"], name=n("ssd"))
    y_ssm = tail_fwd(cfg, y, z, sm["ssm_norm_w"], name=n("tail"))
    if hooks and "weights" in hooks:
        pw = dict(pw, **hooks["weights"](y_ssm))
    cqn = rmsnorm_fwd(small, sm["q_norm_w"], cw=cfg.ql, ci=0, name=n("q_norm"))
    ckvn = rmsnorm_fwd(small, sm["kv_norm_w"], cw=cfg.kvl, ci=cfg.ql // cfg.kvl, name=n("kv_norm"))
    qf = matmul(cqn, pw["w_uq"], out_dtype=BF16, name=n("uq"))
    kv = matmul(ckvn, pw["w_ukv"], out_dtype=BF16, name=n("ukv"))
    qr, kpe = rope_fwd(cfg, qf, small, tabs, name=n("rope"))
    o, lse = attn_fwd(cfg, qr, kv, kpe, name=n("attn"))
    ya = matmul(y_ssm, pw["w_bs"], out_dtype=BF16, name=n("branch_ssm"))
    yb = matmul(o, pw["w_bm"], out_dtype=BF16, name=n("branch_mla"))
    mixed = gate_fwd(cfg, ya, yb, g, name=n("gate"))
    h1 = matmul(mixed, pw["w_out"], add=h, name=n("out"))
    v = rmsnorm_fwd(h1, sm["norm_mlp_w"] + _hook(hooks, "after_attn", o), name=n("norm_mlp"))
    a, act = matmul(v, pw["w_up"], name=n("up"), epilogue=_ep_relu2, out_dtypes=(BF16, BF16))
    h2 = matmul(act, pw["w_down"], add=h1, name=n("down"))
    saved = dict(h=h, u=u, z=z, xbc=xbc, g=g, small=small, xc=xc, dsilu=dsilu, y=y, sin=sin, y_ssm=y_ssm, cqn=cqn, ckvn=ckvn,
                 qr=qr, kv=kv, kpe=kpe, o=o, lse=lse, ya=ya, yb=yb, mixed=mixed, h1=h1, v=v, a=a, act=act)
    return h2, saved, pw


def layer_bwd(cfg, dh2, pw, sm, tabs, s, li, hooks=None):
    n = lambda t: f"l{li}_b_{t}"
    gw, gs = {}, {}
    wgrad = functools.partial(matmul, ta=True, out_dtype=BF16)
    dh2, dh2b = dh2
    gw["w_down"] = wgrad(s["act"], dh2b, name=n("dw_down"))
    da = matmul(dh2b, pw["w_down"], tb=True, name=n("dact"), epilogue=_ep_relu2_grad, extras=(s["a"],),
                out_dtypes=(BF16,))
    gw["w_up"] = wgrad(s["v"], da, name=n("dw_up"))
    dv = matmul(da, pw["w_up"], tb=True, out_dtype=BF16, name=n("dv"))
    dh1, gs["norm_mlp_w"], dh1b = rmsnorm_bwd(dv, s["h1"], sm["norm_mlp_w"], res=dh2, with_bf16=True,
                                              name=n("norm_mlp"))
    gw["w_out"] = wgrad(s["mixed"], dh1b, name=n("dw_out"))
    dmix = matmul(dh1b, pw["w_out"], tb=True, out_dtype=BF16, name=n("dmix"))
    dya, dyb, dg = gate_bwd(cfg, dmix, s["ya"], s["yb"], s["g"], name=n("gate"))
    gw["w_bs"] = wgrad(s["y_ssm"], dya, name=n("dw_bs"))
    gw["w_bm"] = wgrad(s["o"], dyb, name=n("dw_bm"))
    dy_ssm = matmul(dya, pw["w_bs"], tb=True, out_dtype=BF16, name=n("dy_ssm"))
    do = matmul(dyb, pw["w_bm"], tb=True, out_dtype=BF16, name=n("do"))
    dq, dkv, dkpe = attn_bwd(cfg, s["qr"], s["kv"], s["kpe"], s["o"], s["lse"], do, name=n("attn"))
    dqf, dkr = rope_bwd(cfg, dq, dkpe, tabs, name=n("rope"))
    gw["w_uq"] = wgrad(s["cqn"], dqf, name=n("dw_uq"))
    gw["w_ukv"] = wgrad(s["ckvn"], dkv, name=n("dw_ukv"))
    dcqn = matmul(dqf, pw["w_uq"], tb=True, name=n("dcqn"))
    dckvn = matmul(dkv, pw["w_ukv"], tb=True, name=n("dckvn"))
    q_norm_w = sm["q_norm_w"] + _hook(hooks, "after_attn", dqf)
    dcq, gs["q_norm_w"] = rmsnorm_bwd(dcqn, s["small"], q_norm_w, cw=cfg.ql, ci=0, out_dtype=BF16, name=n("q_norm"))
    dckv, gs["kv_norm_w"] = rmsnorm_bwd(dckvn, s["small"], sm["kv_norm_w"], cw=cfg.kvl, ci=cfg.ql // cfg.kvl,
                                        out_dtype=BF16, name=n("kv_norm"))
    ssm_norm_w = sm["ssm_norm_w"] + _hook(hooks, "early", dict(gw))
    dy, dz, gs["ssm_norm_w"] = tail_bwd(cfg, dy_ssm, s["y"], s["z"], ssm_norm_w, name=n("tail"))
    dxc, ddt, ddexp, dav, dbias = ssd_bwd(cfg, s["xc"], s["small"], sm["dt_bias_p"], sm["avec"], sm["dexp"],
                                          s["sin"], dy, name=n("ssd"))
    conv_w = sm["conv_w"] + _hook(hooks, "after_ssd", dxc)
    dxbc, gs["conv_w"], gs["conv_b"] = conv_bwd(cfg, s["xbc"], conv_w, s["dsilu"], dxc, name=n("conv"))
    gs["d_skip"] = ddexp.reshape(cfg.heads, cfg.hd).sum(axis=1)
    gs["a_log"] = (dav[0] * sm["avec"][0])[:cfg.heads]
    gs["dt_bias"] = dbias[0, :cfg.heads]
    dsmall = jnp.concatenate([dcq, dckv, dkr.astype(BF16), ddt.astype(BF16)], axis=1)
    gw["w_z"] = wgrad(s["u"], dz, name=n("dw_z"))
    gw["w_xbc"] = wgrad(s["u"], dxbc, name=n("dw_xbc"))
    gw["w_g"] = wgrad(s["u"], dg, name=n("dw_g"))
    gw["w_s"] = wgrad(s["u"], dsmall, name=n("dw_s"))
    du = matmul_nt_sum([dz, dxbc, dg, dsmall], [pw["w_z"], pw["w_xbc"], pw["w_g"], pw["w_s"]], out_dtype=BF16,
                       name=n("du"))
    if li > 0:
        dh, gs["norm_mix_w"], dhb = rmsnorm_bwd(du, s["h"], sm["norm_mix_w"], res=dh1, with_bf16=True,
                                                name=n("norm_mix"))
    else:
        dh, gs["norm_mix_w"] = rmsnorm_bwd(du, s["h"], sm["norm_mix_w"], res=dh1, name=n("norm_mix"))
        dhb = None
    return (dh, dhb), gw, gs


def small_params(cfg, p, li):
    pad_l = lambda v: jnp.pad(v, (0, LANE - v.shape[0])).reshape(1, LANE)
    return dict(
        norm_mix_w=p["norm_mix_w"][li], conv_w=p["conv_w"][li], conv_b=p["conv_b"][li],
        dt_bias_p=pad_l(p["dt_bias"][li]), avec=pad_l(-jnp.exp(p["a_log"][li])),
        dexp=jnp.repeat(p["d_skip"][li], cfg.hd).reshape(1, cfg.inner),
        ssm_norm_w=p["ssm_norm_w"][li], q_norm_w=p["q_norm_w"][li], kv_norm_w=p["kv_norm_w"][li],
        norm_mlp_w=p["norm_mlp_w"][li])


def local_step(cfg, x, target, p, depth=2):
    bsz, d = cfg.bsz, cfg.d
    lead = jnp.zeros((bsz, cfg.pad, d), F32)
    meta = jnp.broadcast_to(p["meta_tokens"][None], (bsz, cfg.n_meta, d))
    h = jnp.concatenate([lead, meta, x], axis=1).reshape(cfg.t, d)
    tabs = rope_tables(cfg)
    saved, sms = [], []
    for li in range(depth):
        sm = small_params(cfg, p, li)
        h, s, _ = layer_fwd(cfg, h, p["pw"][li], sm, tabs, li)
        saved.append(s)
        sms.append(sm)
    loss, dh, dfw = loss_head(cfg, h, target.reshape(bsz * cfg.seq, d), p["final_norm_w"], name="loss_head")
    gws, gss = [None] * depth, [None] * depth
    for li in reversed(range(depth)):
        dh, gws[li], gss[li] = layer_bwd(cfg, dh, p["pw"][li], sms[li], tabs, saved[li], li)
    dh = dh[0].reshape(bsz, cfg.lp, d)
    grad_x = dh[:, cfg.chunk:, :]
    gmeta = jnp.sum(dh[:, cfg.pad:cfg.chunk, :], axis=0)
    return loss, grad_x, gmeta, gws, gss, dfw


def _pack_small(parts):
    flat = jnp.concatenate([a.reshape(-1) for a in parts])
    n = flat.shape[0]
    npad = -n % (8 * LANE)
    return jnp.pad(flat, (0, npad)).reshape(-1, LANE), n


def _unpack_small(vec, shapes):
    flat = vec.reshape(-1)
    out, off = [], 0
    for sh in shapes:
        sz = int(np.prod(sh))
        out.append(flat[off:off + sz].reshape(sh))
        off += sz
    return out


def _as2d(a):
    return a.reshape(-1, a.shape[-1])


def kernel(x, meta_tokens, norm_mix_w, w_in, conv_w, conv_b, dt_bias, a_log, d_skip, ssm_norm_w, q_norm_w, kv_norm_w, w_uq, w_ukv, w_branch_ssm, w_branch_mla, w_out, norm_mlp_w, w_mlp_up, w_mlp_down, final_norm_w, loss_target, m_meta_tokens, m_norm_mix_w, m_w_in, m_conv_w, m_conv_b, m_dt_bias, m_a_log, m_d_skip, m_ssm_norm_w, m_q_norm_w, m_kv_norm_w, m_w_uq, m_w_ukv, m_w_branch_ssm, m_w_branch_mla, m_w_out, m_norm_mlp_w, m_w_mlp_up, m_w_mlp_down, m_final_norm_w, v_meta_tokens, v_norm_mix_w, v_w_in, v_conv_w, v_conv_b, v_dt_bias, v_a_log, v_d_skip, v_ssm_norm_w, v_q_norm_w, v_kv_norm_w, v_w_uq, v_w_ukv, v_w_branch_ssm, v_w_branch_mla, v_w_out, v_norm_mlp_w, v_w_mlp_up, v_w_mlp_down, v_final_norm_w):
    cfg = CFG
    names = ["meta_tokens", "norm_mix_w", "w_in", "conv_w", "conv_b", "dt_bias", "a_log", "d_skip", "ssm_norm_w",
             "q_norm_w", "kv_norm_w", "w_uq", "w_ukv", "w_branch_ssm", "w_branch_mla", "w_out", "norm_mlp_w",
             "w_mlp_up", "w_mlp_down", "final_norm_w"]
    wts = dict(zip(names, [meta_tokens, norm_mix_w, w_in, conv_w, conv_b, dt_bias, a_log, d_skip, ssm_norm_w,
                           q_norm_w, kv_norm_w, w_uq, w_ukv, w_branch_ssm, w_branch_mla, w_out, norm_mlp_w,
                           w_mlp_up, w_mlp_down, final_norm_w]))
    ms = dict(zip(names, [m_meta_tokens, m_norm_mix_w, m_w_in, m_conv_w, m_conv_b, m_dt_bias, m_a_log, m_d_skip,
                          m_ssm_norm_w, m_q_norm_w, m_kv_norm_w, m_w_uq, m_w_ukv, m_w_branch_ssm, m_w_branch_mla,
                          m_w_out, m_norm_mlp_w, m_w_mlp_up, m_w_mlp_down, m_final_norm_w]))
    vs = dict(zip(names, [v_meta_tokens, v_norm_mix_w, v_w_in, v_conv_w, v_conv_b, v_dt_bias, v_a_log, v_d_skip,
                          v_ssm_norm_w, v_q_norm_w, v_kv_norm_w, v_w_uq, v_w_ukv, v_w_branch_ssm, v_w_branch_mla,
                          v_w_out, v_norm_mlp_w, v_w_mlp_up, v_w_mlp_down, v_final_norm_w]))
    cx, cy, cc = _coords()
    chip = 2 * cx + cy

    half1 = jnp.reshape(cc, (1,)).astype(jnp.int32)
    where2 = jnp.stack([chip, cc]).astype(jnp.int32)
    wb = {k: wts[k].astype(BF16) for k in BIG}
    zero_tok = jnp.zeros((8, LANE), F32)

    def halves(a):
        return a.reshape((2, a.shape[0] // 2) + a.shape[1:])

    def gather_start(li, keys, tag, after):
        srcs = [halves(wb[k][li]) for k in keys]
        lands = [lax.empty((4,) + s.shape, BF16) for s in srcs]
        return ici_start("gather", srcs, lands, after, name=f"gather{li}{tag}_start")

    def gather_finish(li, keys, tag, started, after):
        srcs, lands = ici_wait("gather", started, after, name=f"gather{li}{tag}_wait")
        lands = pair_share(lands, srcs, name=f"gather{li}{tag}_share")
        full = {k: _unshard_layer(k, land.reshape((4, 2 * land.shape[2], land.shape[3])))
                for k, land in zip(keys, lands)}
        return prep_layer(cfg, full)

    def gather_mid(li, keys, tag, started, after):
        srcs, lands = ici_wait("gather", started, after, name=f"gather{li}{tag}_wait")
        return ici_start("share", srcs, lands, zero_tok, name=f"gather{li}{tag}_share_start")

    def gather_end(li, keys, tag, shared, after):
        _, lands = ici_wait("share", shared, after, name=f"gather{li}{tag}_share_wait")
        full = {k: _unshard_layer(k, land.reshape((4, 2 * land.shape[2], land.shape[3])))
                for k, land in zip(keys, lands)}
        return prep_layer(cfg, full)

    def exchange_start(li, keys, tag, gw, after):
        ug = unprep_grads(cfg, gw)
        g4 = []
        for k in keys:
            s = _to_shards(k, ug[k])
            g4.append(s.reshape(4, 2, s.shape[1] // 2, s.shape[2]))
        lands = [lax.empty((4,) + a.shape[2:], a.dtype) for a in g4]
        return ici_start("exchange", g4, lands, after, name=f"grad{li}{tag}_exchange_start")

    def reduce_start(li, keys, tag, exchanged, after):
        g4, theirs = ici_wait("exchange", exchanged, after, name=f"grad{li}{tag}_exchange_wait")
        parts = [pair_add(a, b, half1, name=f"grad{li}_pair_add_{k}") for k, a, b in zip(keys, g4, theirs)]
        lands = [lax.empty(q.shape, q.dtype) for q in parts]
        return ici_start("scatter", parts, lands, zero_tok, name=f"grad{li}{tag}_scatter_start")

    def reduce_mid(li, keys, tag, started, after):
        parts, lands = ici_wait("scatter", started, after, name=f"grad{li}{tag}_scatter_wait")
        sums = [chip_sum(rc, pt, where2, name=f"grad{li}_chip_sum_{k}") for k, rc, pt in zip(keys, lands, parts)]
        return ici_start("fill", [zero_tok] * len(sums), sums, zero_tok, name=f"grad{li}{tag}_fill_start")

    def reduce_end(li, keys, tag, filled, after):
        _, sums = ici_wait("fill", filled, after, name=f"grad{li}{tag}_fill_wait")
        return {k: s.reshape(2 * s.shape[1], s.shape[2]) for k, s in zip(keys, sums)}

    gathered = gather_chips([meta_tokens, conv_w], name="gather_small")
    p = dict(wts)
    p["meta_tokens"] = jnp.transpose(gathered[0], (1, 0, 2)).reshape(cfg.n_meta, cfg.d)
    p["conv_w"] = jnp.transpose(gathered[1], (1, 2, 0, 3)).reshape(2, cfg.convk, cfg.conv_dim)

    st0a = gather_start(0, ["w_in"], "a", gathered[0])
    st0b = gather_start(0, REST, "b", st0a[4])
    st1 = gather_start(1, BIG, "", st0b[4])
    pw0 = gather_finish(0, ["w_in"], "a", st0a, st1[4])

    bsz, d = cfg.bsz, cfg.d
    lead = jnp.zeros((bsz, cfg.pad, d), F32)
    meta = jnp.broadcast_to(p["meta_tokens"][None], (bsz, cfg.n_meta, d))
    h0 = jnp.concatenate([lead, meta, x], axis=1).reshape(cfg.t, d)
    tabs = rope_tables(cfg)
    sm0 = small_params(cfg, p, 0)
    st = {}

    def step(key, fn):
        def run(arg):
            st[key] = fn(arg)
            return st[key][4]
        return run

    h1, sv0, pw0 = layer_fwd(cfg, h0, pw0, sm0, tabs, 0, hooks={
        "after_conv": step("share0b", lambda after: gather_mid(0, REST, "b", st0b, after)),
        "weights": lambda after: gather_end(0, REST, "b", st["share0b"], after),
        "after_attn": step("share1", lambda after: gather_mid(1, BIG, "", st1, after))})
    pw1 = gather_end(1, BIG, "", st["share1"], h1)
    sm1 = small_params(cfg, p, 1)
    h2, sv1, _ = layer_fwd(cfg, h1, pw1, sm1, tabs, 1)
    loss, dh, dfw = loss_head(cfg, h2, loss_target.reshape(bsz * cfg.seq, d), final_norm_w, name="loss_head")

    dh, gw1, gs1 = layer_bwd(cfg, dh, pw1, sm1, tabs, sv1, 1)
    ex1 = exchange_start(1, BIG, "", gw1, zero_tok)
    sm0b = dict(sm0)
    sm0b["norm_mlp_w"] = sm0["norm_mlp_w"] + ex1[4][0, 0]
    dh, gw0, gs0 = layer_bwd(cfg, dh, pw0, sm0b, tabs, sv0, 0, hooks={
        "after_attn": step("red1", lambda after: reduce_start(1, BIG, "", ex1, after)),
        "early": step("ex0e", lambda gw: exchange_start(0, REST, "e", gw, zero_tok)),
        "after_ssd": step("red0e", lambda after: reduce_start(0, REST, "e", st["ex0e"], after))})
    dh3 = dh[0].reshape(bsz, cfg.lp, d)
    grad_x = dh3[:, cfg.chunk:, :]
    gmeta = jnp.sum(dh3[:, cfg.pad:cfg.chunk, :], axis=0)
    fill1 = reduce_mid(1, BIG, "", st["red1"], dh[0])
    ex0l = exchange_start(0, ["w_in"], "l", gw0, fill1[4])

    small_names = SMALL_REPL + ["conv_w"]
    parts = [jnp.stack([gs0[k], gs1[k]]) for k in small_names] + [dfw, gmeta, loss.reshape(1)]
    shapes = [a.shape for a in parts]
    vec, _ = _pack_small(parts)
    red_vec = allreduce_small(vec, ex0l[4], name="allreduce_small")
    red = _unpack_small(red_vec, shapes)
    sg = dict(zip(small_names + ["final_norm_w", "meta_tokens"], red))
    loss = red[-1].reshape(())
    sg["conv_w"] = lax.dynamic_slice_in_dim(sg["conv_w"], chip * (cfg.conv_dim // 4), cfg.conv_dim // 4, axis=2)
    sg["meta_tokens"] = lax.dynamic_slice_in_dim(sg["meta_tokens"], chip * (cfg.d // 4), cfg.d // 4, axis=1)

    red0 = reduce_start(0, ["w_in"], "l", ex0l, red_vec)
    grads, deltas, new_m, new_v = {}, {}, {}, {}
    dep = red0[4]
    for k in names:
        if k in BIG:
            continue
        w2, g2, m2, v2 = _as2d(wts[k]), _as2d(sg[k]), _as2d(ms[k]), _as2d(vs[k])
        dl, mn, vn = adamw_small(w2, g2, m2, v2, dep, name=f"adamw_{k}")
        grads[k] = sg[k].reshape(wts[k].shape)
        deltas[k], new_m[k], new_v[k] = (t.reshape(wts[k].shape) for t in (dl, mn, vn))

    def view(k, a):
        return jnp.swapaxes(a, 1, 2) if k == "w_in" else a

    def gview(k, g):
        return g.T if k == "w_in" else g

    wv, mv, vv = ({k: view(k, t[k]) for k in BIG} for t in (wts, ms, vs))
    outs = {}
    big1 = reduce_end(1, BIG, "", fill1, dl)
    fill0e = reduce_mid(0, REST, "e", st["red0e"], big1[BIG[-1]])
    dep = fill0e[4]
    for k in BIG:
        outs[k] = adamw_layer(wv[k], mv[k], vv[k], gview(k, big1[k]), 1, None, dep, name=f"adamw1_{k}")
        dep = outs[k][1]
    big0 = reduce_end(0, REST, "e", fill0e, dep)
    fill0l = reduce_mid(0, ["w_in"], "l", red0, big0[REST[-1]])
    dep = fill0l[4]
    for k in REST:
        outs[k] = adamw_layer(wv[k], mv[k], vv[k], big0[k], 0, outs[k], dep, name=f"adamw0_{k}")
        dep = outs[k][1]
    big0.update(reduce_end(0, ["w_in"], "l", fill0l, dep))
    outs["w_in"] = adamw_layer(wv["w_in"], mv["w_in"], vv["w_in"], gview("w_in", big0["w_in"]), 0, outs["w_in"], dep,
                               name="adamw0_w_in")
    for k in BIG:
        grads[k], deltas[k], new_m[k], new_v[k] = (view(k, t) for t in outs[k])
    return (loss, grad_x, *[grads[k] for k in names], *[deltas[k] for k in names],
            *[new_m[k] for k in names], *[new_v[k] for k in names])


def adamw_small(w, g, m, v, dep, *, name):
    def body(w_ref, g_ref, m_ref, v_ref, dep_ref, d_ref, mo_ref, vo_ref):
        d_ref[...], mo_ref[...], vo_ref[...] = _adam_update(w_ref[...], g_ref[...], m_ref[...], v_ref[...])

    vm = pl.BlockSpec(memory_space=pltpu.VMEM)
    return pl.pallas_call(body, name=name, in_specs=[vm] * 4 + [pl.BlockSpec(memory_space=pl.ANY)], out_specs=[vm] * 3,
                          out_shape=[_sds(w.shape, F32)] * 3, compiler_params=_cp())(w, g, m, v, dep)
```

```python
import functools
from typing import NamedTuple

import numpy as np
import jax
import jax.numpy as jnp
from jax import lax
from jax.experimental import pallas as pl
from jax.experimental.pallas import tpu as pltpu

F32 = jnp.float32
BF16 = jnp.bfloat16
EPS = 1e-6
ROPE_THETA = 10000.0
LANE = 128
VMEM_LIMIT = 56 * 1024 * 1024
MASK_VALUE = -1e30
ADAM_LR, ADAM_B1, ADAM_B2, ADAM_EPS, ADAM_WD, ADAM_STEP = 0.001, 0.9, 0.999, 1e-08, 0.01, 10
MESH = pl.DeviceIdType.MESH


class Cfg(NamedTuple):
    d: int = 1024
    seq: int = 2048
    bsz: int = 2
    n_meta: int = 16
    inner: int = 2048
    hd: int = 64
    groups: int = 4
    state: int = 128
    convk: int = 4
    chunk: int = 128
    mh: int = 8
    ql: int = 512
    kvl: int = 256
    nope: int = 128
    rope: int = 64
    vd: int = 128
    ff: int = 4096

    @property
    def heads(self): return self.inner // self.hd
    @property
    def gw(self): return self.inner // self.groups
    @property
    def conv_dim(self): return self.inner + 2 * self.groups * self.state
    @property
    def pad(self): return self.chunk - self.n_meta
    @property
    def lp(self): return self.chunk + self.seq
    @property
    def t(self): return self.bsz * self.lp
    @property
    def nchunks(self): return self.lp // self.chunk
    @property
    def sw(self): return self.ql + self.kvl + 2 * LANE
    @property
    def kt(self): return (self.ql + self.kvl) // LANE
    @property
    def dtt(self): return self.kt + 1
    @property
    def qw(self): return self.mh * 2 * LANE
    @property
    def in_splits(self):
        return [self.inner, self.conv_dim, self.heads, self.ql, self.kvl, self.rope, self.d, self.d]


CFG = Cfg()


def _pick(dim, pref, mult):
    best = None
    for t in range(mult, min(dim, pref) + 1, mult):
        if dim % t == 0:
            best = t
    return best if best is not None else dim


def _cp(**kw):
    return pltpu.CompilerParams(vmem_limit_bytes=VMEM_LIMIT, **kw)


def _sds(shape, dtype):
    return jax.ShapeDtypeStruct(tuple(shape), dtype)


def _silu(x):
    return x * jax.nn.sigmoid(x)


def _dsilu(x):
    s = jax.nn.sigmoid(x)
    return s * (1.0 + x * (1.0 - s))


def _ep_plain(r):
    return (r,)


def _ep_add(r, res):
    return (r + res.astype(F32),)


def _ep_relu2(r):
    rp = jnp.maximum(r, 0.0)
    return r, rp * rp


def _ep_relu2_grad(r, a):
    return (r * (2.0 * jnp.maximum(a.astype(F32), 0.0)),)


MM_VMEM_BUDGET = 44 * 1024 * 1024


def _mm_tiles(m, n, k, a_bytes, b_bytes, io_bytes, ta, n_unit=None):
    m_mult, m_cap = (LANE, 1024) if ta else (16, 1088)
    tms = [t for t in range(m_cap, 0, -m_mult) if m % t == 0] or [m]
    tns = [t for t in (1024, 512, 256, 128) if (n_unit or n) % t == 0] or [n]
    best = None
    for tm in tms:
        for tn in tns:
            need = 2 * (tm * k * a_bytes + k * tn * b_bytes + tm * tn * io_bytes)
            if need <= MM_VMEM_BUDGET and (best is None or tm * tn > best[0] * best[1]):
                best = (tm, tn)
    if best is None:
        return (_pick(m, 512, m_mult), _pick(n, 512, LANE), _pick(k, 1088 if ta else 1024, 16 if ta else LANE))
    return best[0], best[1], k


def _resident_rows(m, n, k, a_bytes, b_bytes, io_bytes):
    w = n * k * b_bytes
    if w > 18 * 1024 * 1024:
        return None
    for tm in range(544, 255, -16):
        if m % tm == 0 and w + 2 * tm * (k * a_bytes + n * io_bytes) + tm * n * 4 <= MM_VMEM_BUDGET - (4 << 20):
            return tm
    return None


def matmul(a, b, *, ta=False, tb=False, out_dtype=F32, add=None, name, tm=None, tn=None, tk=None,
           epilogue=None, extras=(), out_dtypes=None, col_slabs=None):
    if add is not None:
        epilogue, extras = _ep_add, (add,)
    if epilogue is None:
        epilogue = _ep_plain
    out_dtypes = tuple(out_dtypes) if out_dtypes is not None else (out_dtype,)
    n_ex, n_out = len(extras), len(out_dtypes)
    if ta:
        k_dim, m_dim = a.shape
    else:
        m_dim, k_dim = a.shape
    if tb:
        n_dim, k2 = b.shape
    else:
        k2, n_dim = b.shape
    assert k_dim == k2, (a.shape, b.shape, ta, tb)
    resident = False
    if tm is None and tn is None and tk is None:
        io_bytes = sum(jnp.dtype(e.dtype).itemsize for e in extras) + sum(jnp.dtype(d).itemsize for d in out_dtypes)
        a_bytes, b_bytes = jnp.dtype(a.dtype).itemsize, jnp.dtype(b.dtype).itemsize
        tm = None if (ta or col_slabs) else _resident_rows(m_dim, n_dim, k_dim, a_bytes, b_bytes, io_bytes)
        if tm is not None:
            resident, tn, tk = True, n_dim, k_dim
        else:
            tm, tn, tk = _mm_tiles(m_dim, n_dim, k_dim, a_bytes, b_bytes, io_bytes, ta,
                                   n_unit=n_dim // col_slabs if col_slabs else None)
    elif ta:
        tm = tm or _pick(m_dim, 1024, LANE)
        tk = tk or _pick(k_dim, 1088, 16)
        tn = tn or _pick(n_dim, 1024, LANE)
    else:
        tm = tm or _pick(m_dim, 1088, 16)
        tk = tk or _pick(k_dim, 1024 if a.dtype == F32 else 2048, LANE)
        tn = tn or _pick(n_dim, 1024, LANE)
    nm, nn, nk = m_dim // tm, n_dim // tn, k_dim // tk
    dn = (((0 if ta else 1,), (1 if tb else 0,)), ((), ()))

    def body(*refs):
        a_ref, b_ref = refs[:2]
        ex_refs = refs[2:2 + n_ex]
        o_refs = refs[2 + n_ex:2 + n_ex + n_out]
        scr = refs[2 + n_ex + n_out:]
        p = lax.dot_general(a_ref[...].astype(BF16), b_ref[...].astype(BF16), dn, preferred_element_type=F32)

        def finish(r):
            outs = epilogue(r, *[e[...] for e in ex_refs])
            for o_ref, val, dt in zip(o_refs, outs, out_dtypes):
                o_ref[...] = val.astype(dt)

        if nk == 1:
            finish(p)
        else:
            acc = scr[0]
            k = pl.program_id(2)

            @pl.when(k == 0)
            def _():
                acc[...] = p

            @pl.when(k > 0)
            def _():
                acc[...] += p

            @pl.when(k == nk - 1)
            def _():
                finish(acc[...])

    if resident:
        row = pl.BlockSpec((tm, k_dim), lambda i: (i, 0))
        o_spec = pl.BlockSpec((tm, n_dim), lambda i: (i, 0))
        outs = pl.pallas_call(
            body, name=name, grid=(nm,),
            in_specs=[row, pl.BlockSpec(b.shape, lambda i: (0, 0), pipeline_mode=pl.Buffered(1))] + [o_spec] * n_ex,
            out_specs=[o_spec] * n_out, out_shape=[_sds((m_dim, n_dim), dt) for dt in out_dtypes],
            compiler_params=_cp(dimension_semantics=("parallel",)),
        )(a, b, *extras)
        return outs[0] if n_out == 1 else tuple(outs)
    a_spec = pl.BlockSpec((tk, tm), lambda i, j, k: (k, i)) if ta else pl.BlockSpec((tm, tk), lambda i, j, k: (i, k))
    b_spec = pl.BlockSpec((tn, tk), lambda i, j, k: (j, k)) if tb else pl.BlockSpec((tk, tn), lambda i, j, k: (k, j))
    o_spec = pl.BlockSpec((tm, tn), lambda i, j, k: (i, j))
    out_spec, out_shape = o_spec, (m_dim, n_dim)
    if col_slabs:
        assert n_ex == 0 and n_out == 1
        per = n_dim // col_slabs // tn
        out_spec = pl.BlockSpec((None, tm, tn), lambda i, j, k: (j // per, i, j % per))
        out_shape = (col_slabs, m_dim, n_dim // col_slabs)
    outs = pl.pallas_call(
        body, name=name, grid=(nm, nn, nk), in_specs=[a_spec, b_spec] + [o_spec] * n_ex, out_specs=[out_spec] * n_out,
        out_shape=[_sds(out_shape, dt) for dt in out_dtypes],
        scratch_shapes=[pltpu.VMEM((tm, tn), F32)] if nk > 1 else [],
        compiler_params=_cp(dimension_semantics=("parallel", "parallel", "arbitrary")),
    )(a, b, *extras)
    return outs[0] if n_out == 1 else tuple(outs)


def matmul_multi(a, bs_, out_dtypes, *, name):
    m, k = a.shape
    ns = [b.shape[1] for b in bs_]
    cnt = len(bs_)
    out_row_bytes = sum(n * jnp.dtype(dt).itemsize for n, dt in zip(ns, out_dtypes))
    w_bytes = sum(k * n * jnp.dtype(b.dtype).itemsize for n, b in zip(ns, bs_))
    tm = next(t for t in range(1088, 0, -16)
              if m % t == 0 and w_bytes + 2 * t * (k * jnp.dtype(a.dtype).itemsize + out_row_bytes)
              + t * max(ns) * 4 <= MM_VMEM_BUDGET - (8 << 20))

    def body(*refs):
        a_ref = refs[0]
        b_refs, o_refs = refs[1:1 + cnt], refs[1 + cnt:]
        av = a_ref[...].astype(BF16)
        for b_ref, o_ref, dt in zip(b_refs, o_refs, out_dtypes):
            o_ref[...] = _nn(av, b_ref[...].astype(BF16)).astype(dt)

    return pl.pallas_call(
        body, name=name, grid=(m // tm,),
        in_specs=[pl.BlockSpec((tm, k), lambda i: (i, 0))]
        + [pl.BlockSpec((k, n), lambda i: (0, 0), pipeline_mode=pl.Buffered(1)) for n in ns],
        out_specs=[pl.BlockSpec((tm, n), lambda i: (i, 0)) for n in ns],
        out_shape=[_sds((m, n), dt) for n, dt in zip(ns, out_dtypes)], compiler_params=_cp(),
    )(a, *bs_)


def matmul_nt_sum(as_, bs_, *, out_dtype=F32, name, tiles=None):
    m, n = as_[0].shape[0], bs_[0].shape[0]
    ks = [a.shape[1] for a in as_]
    assert [b.shape[1] for b in bs_] == ks
    ksum, cnt = sum(ks), len(ks)
    best = tiles
    for tn in [t for t in (1024, 512, 256, 128) if n % t == 0]:
        for tm in [t for t in range(1088, 0, -16) if m % t == 0]:
            need = 2 * (tm * ksum * 2 + tn * ksum * 2 + tm * tn * jnp.dtype(out_dtype).itemsize)
            if best is None and need <= MM_VMEM_BUDGET and tm >= 256:
                best = (tm, tn)
    tm, tn = best

    def body(*refs):
        a_refs, b_refs, o_ref = refs[:cnt], refs[cnt:2 * cnt], refs[2 * cnt]
        acc = None
        for a_ref, b_ref in zip(a_refs, b_refs):
            p = _nt(a_ref[...].astype(BF16), b_ref[...].astype(BF16))
            acc = p if acc is None else acc + p
        o_ref[...] = acc.astype(out_dtype)

    return pl.pallas_call(
        body, name=name, grid=(n // tn, m // tm),
        in_specs=[pl.BlockSpec((tm, k), lambda j, i: (i, 0)) for k in ks]
        + [pl.BlockSpec((tn, k), lambda j, i: (j, 0)) for k in ks],
        out_specs=pl.BlockSpec((tm, tn), lambda j, i: (i, j)), out_shape=_sds((m, n), out_dtype),
        compiler_params=_cp(dimension_semantics=("parallel", "parallel")),
    )(*as_, *bs_)


def rmsnorm_fwd(x, w, *, cw=None, ci=0, name):
    t = x.shape[0]
    cw = cw or x.shape[1]
    tr = _pick(t, 544, 16)

    def body(x_ref, w_ref, o_ref):
        xv = x_ref[...].astype(F32)
        r = lax.rsqrt(jnp.mean(xv * xv, axis=-1, keepdims=True) + EPS)
        o_ref[...] = (xv * r * w_ref[...]).astype(BF16)

    return pl.pallas_call(
        body, name=name, grid=(t // tr,),
        in_specs=[pl.BlockSpec((tr, cw), lambda i: (i, ci)), pl.BlockSpec((1, cw), lambda i: (0, 0))],
        out_specs=pl.BlockSpec((tr, cw), lambda i: (i, 0)),
        out_shape=_sds((t, cw), BF16), compiler_params=_cp(),
    )(x, w.reshape(1, cw))


def rmsnorm_bwd(dy, x, w, *, cw=None, ci=0, res=None, out_dtype=F32, with_bf16=False, name):
    t = x.shape[0]
    cw = cw or x.shape[1]
    tr = _pick(t, 544, 16)
    has_res = res is not None

    def body(*refs):
        dxb_ref = None
        if with_bf16:
            refs, dxb_ref = refs[:-1], refs[-1]
        if has_res:
            dy_ref, x_ref, w_ref, res_ref, dx_ref, dw_ref = refs
        else:
            dy_ref, x_ref, w_ref, dx_ref, dw_ref = refs
        xv = x_ref[...].astype(F32)
        dyv = dy_ref[...].astype(F32)
        r = lax.rsqrt(jnp.mean(xv * xv, axis=-1, keepdims=True) + EPS)
        xh = xv * r
        g = dyv * w_ref[...]
        dx = r * (g - xh * jnp.mean(g * xh, axis=-1, keepdims=True))
        if has_res:
            dx = dx + res_ref[...]
        dx_ref[...] = dx.astype(out_dtype)
        if with_bf16:
            dxb_ref[...] = dx.astype(BF16)

        @pl.when(pl.program_id(0) == 0)
        def _():
            dw_ref[...] = jnp.zeros_like(dw_ref)

        dw_ref[...] += jnp.sum(dyv * xh, axis=0, keepdims=True)

    row = pl.BlockSpec((tr, cw), lambda i: (i, 0))
    in_specs = [row, pl.BlockSpec((tr, cw), lambda i: (i, ci)), pl.BlockSpec((1, cw), lambda i: (0, 0))]
    args = [dy, x, w.reshape(1, cw)]
    if has_res:
        in_specs.append(row)
        args.append(res)
    outs = pl.pallas_call(
        body, name=name, grid=(t // tr,), in_specs=in_specs,
        out_specs=[row, pl.BlockSpec((1, cw), lambda i: (0, 0))] + ([row] if with_bf16 else []),
        out_shape=[_sds((t, cw), out_dtype), _sds((1, cw), F32)] + ([_sds((t, cw), BF16)] if with_bf16 else []),
        compiler_params=_cp(),
    )(*args)
    if with_bf16:
        return outs[0], outs[1][0], outs[2]
    return outs[0], outs[1][0]


def _shift_down(x, s):
    return x if s == 0 else pltpu.roll(x, s, 0)


def _shift_up(x, s):
    return x if s == 0 else pltpu.roll(x, x.shape[0] - s, 0)


def _conv_pre(x, w_ref, b_ref, kk):
    pre = b_ref[...] + jnp.zeros_like(x)
    for k in range(kk):
        pre = pre + w_ref[k:k + 1, :] * _shift_down(x, kk - 1 - k)
    return pre


def conv_fwd(cfg, xbc, w, b, *, name):
    lp, cd, kk = cfg.lp, cfg.conv_dim, cfg.convk
    assert cfg.pad >= kk - 1
    cb = _pick(cd, 512, LANE)

    def body(x_ref, w_ref, b_ref, o_ref, ds_ref):
        pre = _conv_pre(x_ref[...], w_ref, b_ref, kk)
        sg = jax.nn.sigmoid(pre)
        o_ref[...] = pre * sg
        ds_ref[...] = (sg * (1.0 + pre * (1.0 - sg))).astype(BF16)

    blk = pl.BlockSpec((lp, cb), lambda j, bb: (bb, j))
    return pl.pallas_call(
        body, name=name, grid=(cd // cb, cfg.bsz),
        in_specs=[blk, pl.BlockSpec((kk, cb), lambda j, bb: (0, j)), pl.BlockSpec((1, cb), lambda j, bb: (0, j))],
        out_specs=[blk, blk], out_shape=[_sds((cfg.t, cd), F32), _sds((cfg.t, cd), BF16)], compiler_params=_cp(),
    )(xbc, w, b.reshape(1, cd))


def conv_bwd(cfg, xbc, w, dsilu, dxc, *, name):
    lp, cd, kk = cfg.lp, cfg.conv_dim, cfg.convk
    cb = _pick(cd, 512, LANE)

    def body(x_ref, w_ref, s_ref, d_ref, dx_ref, dw_ref, db_ref):
        x = x_ref[...]
        dpre = d_ref[...] * s_ref[...].astype(F32)
        dx = jnp.zeros_like(x)
        dws = []
        for k in range(kk):
            s = kk - 1 - k
            dx = dx + w_ref[k:k + 1, :] * _shift_up(dpre, s)
            dws.append(jnp.sum(dpre * _shift_down(x, s), axis=0, keepdims=True))
        dx_ref[...] = dx.astype(BF16)

        @pl.when(pl.program_id(1) == 0)
        def _():
            dw_ref[...] = jnp.zeros_like(dw_ref)
            db_ref[...] = jnp.zeros_like(db_ref)

        for k in range(kk):
            dw_ref[k:k + 1, :] += dws[k]
        db_ref[...] += jnp.sum(dpre, axis=0, keepdims=True)

    blk = pl.BlockSpec((lp, cb), lambda j, bb: (bb, j))
    wsp = pl.BlockSpec((kk, cb), lambda j, bb: (0, j))
    bsp = pl.BlockSpec((1, cb), lambda j, bb: (0, j))
    dx, dw, db = pl.pallas_call(
        body, name=name, grid=(cd // cb, cfg.bsz),
        in_specs=[blk, wsp, blk, blk], out_specs=[blk, wsp, bsp],
        out_shape=[_sds((cfg.t, cd), BF16), _sds((kk, cd), F32), _sds((1, cd), F32)], compiler_params=_cp(),
    )(xbc, w, dsilu, dxc)
    return dx, dw, db[0]


def _softplus(x):
    return jnp.maximum(x, 0.0) + jnp.log(1.0 + jnp.exp(-jnp.abs(x)))


def _ssd_consts(cfg):
    q = cfg.chunk
    i0 = np.arange(q)[:, None]
    i1 = np.arange(q)[None, :]
    ltri = (i1 <= i0).astype(np.float32)
    rexp = np.zeros((LANE, cfg.inner), np.float32)
    for h in range(cfg.heads):
        rexp[h, h * cfg.hd:(h + 1) * cfg.hd] = 1.0
    return jnp.asarray(ltri), jnp.asarray(rexp)


def _sel_dot(x, m, *, passes=2, left=False, trans=False):
    mb = m.astype(BF16)
    acc, rem = None, x
    for _ in range(passes):
        piece = rem.astype(BF16)
        if not left:
            part = _nn(piece, mb)
        elif trans:
            part = _tn(mb, piece)
        else:
            part = _nn(mb, piece)
        acc = part if acc is None else acc + part
        rem = rem - piece.astype(F32)
    return acc


def _ssd_chunk_common(cfg, raw, bias, avec, c_idx, ltri, rexp):
    q = cfg.chunk
    rows = lax.broadcasted_iota(jnp.int32, (q, LANE), 0)
    live = jnp.logical_or(c_idx > 0, rows >= cfg.pad)
    pre = raw + bias
    dt = jnp.where(live, _softplus(pre), 0.0)
    adt = dt * avec
    cs = _sel_dot(adt, ltri, passes=3, left=True)
    cs_t = cs.T
    cs_last = cs[q - 1:q, :]
    e_in = jnp.exp(cs)
    w0 = jnp.exp(cs_last - cs)
    decay = jnp.exp(cs_last)
    return dict(live=live, pre=pre, dt=dt, adt=adt, cs=cs, cs_t=cs_t, e_in=e_in, w0=w0, decay=decay,
                DT=_sel_dot(dt, rexp), E=_sel_dot(e_in, rexp), W0=_sel_dot(w0, rexp),
                DEC=_sel_dot(jnp.broadcast_to(decay, (8, LANE)), rexp)[0:1, :])


def _tri_masks(q):
    r = lax.broadcasted_iota(jnp.int32, (q, q), 0)
    c = lax.broadcasted_iota(jnp.int32, (q, q), 1)
    return c <= r, r <= c


def _head_l(cq, h, tri, tri_t):
    col = cq["cs"][:, h:h + 1]
    row = cq["cs_t"][h:h + 1, :]
    lmat = jnp.where(tri, jnp.exp(jnp.minimum(col - row, 0.0)), 0.0)
    lmat_t = jnp.where(tri_t, jnp.exp(jnp.minimum(row - col, 0.0)), 0.0)
    return lmat, lmat_t


def _nt(a, b):
    return lax.dot_general(a, b, (((1,), (1,)), ((), ())), preferred_element_type=F32)


def _tn(a, b):
    return lax.dot_general(a, b, (((0,), (0,)), ((), ())), preferred_element_type=F32)


def _nn(a, b):
    return jnp.dot(a, b, preferred_element_type=F32)


def ssd_fwd(cfg, xc, small, dt_bias, avec, dexp, *, name):
    q, inner, st, gw, g_n = cfg.chunk, cfg.inner, cfg.state, cfg.gw, cfg.groups
    nc = cfg.nchunks
    ltri, rexp = _ssd_consts(cfg)
    hpt = LANE // cfg.hd
    tiles_per_group = gw // LANE

    bsz, lp = cfg.bsz, cfg.lp
    bcw = g_n * st

    def body(x_ref, b_ref, c_ref, dt_ref, bias_ref, a_ref, d_ref, ltri_ref, rexp_ref, y_ref, sin_ref, s_scr):
        c_idx = pl.program_id(0)

        @pl.when(c_idx == 0)
        def _():
            s_scr[...] = jnp.zeros_like(s_scr)

        ltri_v = ltri_ref[...]
        tri, tri_t = _tri_masks(q)
        lane = lax.broadcasted_iota(jnp.int32, (q, LANE), 1)
        for bi in range(bsz):
            cq = _ssd_chunk_common(cfg, dt_ref[bi], bias_ref[...], a_ref[...], c_idx, ltri_v, rexp_ref[...])
            xs = x_ref[bi]
            xdt = (xs * cq["DT"]).astype(BF16)
            xw = (xs * cq["DT"] * cq["W0"]).astype(BF16)
            s_in = s_scr[bi]
            sin_ref[bi, 0] = s_in
            for g in range(g_n):
                bg = b_ref[bi, :, g * st:(g + 1) * st].astype(BF16)
                cg = c_ref[bi, :, g * st:(g + 1) * st].astype(BF16)
                gmat = _nt(cg, bg)
                gs = slice(g * gw, (g + 1) * gw)
                y0 = _nn(cg, s_in[:, gs].astype(BF16))
                for tt in range(tiles_per_group):
                    tile = g * tiles_per_group + tt
                    ts = slice(tile * LANE, (tile + 1) * LANE)
                    xt = xdt[:, ts]
                    ms, xh = [], []
                    for hh in range(hpt):
                        lmat, _ = _head_l(cq, tile * hpt + hh, tri, tri_t)
                        ms.append((gmat * lmat).astype(BF16))
                        inhead = jnp.logical_and(lane >= hh * cfg.hd, lane < (hh + 1) * cfg.hd)
                        xh.append(jnp.where(inhead, xt, jnp.zeros_like(xt)))
                    yd = _nn(jnp.concatenate(ms, axis=1), jnp.concatenate(xh, axis=0))
                    y_ref[bi, :, ts] = (yd + y0[:, tt * LANE:(tt + 1) * LANE] * cq["E"][:, ts]
                                        + xs[:, ts] * d_ref[:, ts]).astype(BF16)
                s_scr[bi, :, gs] = s_in[:, gs] * cq["DEC"][:, gs] + _tn(bg, xw[:, gs])

    def rowblk(width, col):
        return pl.BlockSpec((bsz, q, width), lambda c: (0, c, col))

    def const(shape):
        return pl.BlockSpec(shape, lambda c: (0, 0))

    xc3 = xc.reshape(bsz, lp, cfg.conv_dim)
    y, sin = pl.pallas_call(
        body, name=name, grid=(nc,),
        in_specs=[rowblk(inner, 0), rowblk(bcw, inner // bcw), rowblk(bcw, inner // bcw + 1),
                  rowblk(LANE, cfg.dtt), const((1, LANE)), const((1, LANE)), const((1, inner)),
                  const((q, q)), const((LANE, inner))],
        out_specs=[rowblk(inner, 0), pl.BlockSpec((bsz, 1, st, inner), lambda c: (0, c, 0, 0))],
        out_shape=[_sds((bsz, lp, inner), BF16), _sds((bsz, nc, st, inner), F32)],
        scratch_shapes=[pltpu.VMEM((bsz, st, inner), F32)], compiler_params=_cp(),
    )(xc3, xc3, xc3, small.reshape(bsz, lp, cfg.sw), dt_bias, avec, dexp, ltri, rexp)
    return y.reshape(cfg.t, inner), sin.reshape(bsz * nc, st, inner)


def ssd_bwd(cfg, xc, small, dt_bias, avec, dexp, sin, dy, *, name):
    q, inner, st, gw, g_n = cfg.chunk, cfg.inner, cfg.state, cfg.gw, cfg.groups
    nc = cfg.nchunks
    ltri, rexp = _ssd_consts(cfg)
    rexp_t = rexp.T
    hpt = LANE // cfg.hd
    tiles_per_group = gw // LANE
    bcw = g_n * st

    def body(x_ref, b_ref, c_ref, dt_ref, bias_ref, a_ref, d_ref, ltri_ref, rexp_ref, rexpt_ref, sin_ref, dy_ref,
             dx_ref, ddt_ref, dd_ref, da_ref, dbias_ref, ds_scr):
        step = pl.program_id(1)
        c_idx = nc - 1 - step

        @pl.when(step == 0)
        def _():
            ds_scr[...] = jnp.zeros_like(ds_scr)

        @pl.when(jnp.logical_and(step == 0, pl.program_id(0) == 0))
        def _():
            dd_ref[...] = jnp.zeros_like(dd_ref)
            da_ref[...] = jnp.zeros_like(da_ref)
            dbias_ref[...] = jnp.zeros_like(dbias_ref)

        ltri_v = ltri_ref[...]
        tri, tri_t = _tri_masks(q)
        red = _sel_dot
        rexpt = rexpt_ref[...]
        cq = _ssd_chunk_common(cfg, dt_ref[...], bias_ref[...], a_ref[...], c_idx, ltri_v, rexp_ref[...])
        xs = x_ref[...]
        dyv = dy_ref[...].astype(F32)
        s_in = sin_ref[0]
        d_s = ds_scr[...]
        xdt_f = xs * cq["DT"]
        xdt = xdt_f.astype(BF16)
        xw_f = xdt_f * cq["W0"]
        xw = xw_f.astype(BF16)
        lane = lax.broadcasted_iota(jnp.int32, (q, LANE), 1)
        sub = lax.broadcasted_iota(jnp.int32, (LANE, q), 0)

        dd_ref[...] += jnp.sum(dyv * xs, axis=0, keepdims=True)
        dy0 = dyv * cq["E"]
        dcs = jnp.zeros((q, LANE), F32)
        dcs_t = jnp.zeros((LANE, q), F32)
        for g in range(g_n):
            bg_f = b_ref[:, g * st:(g + 1) * st]
            cg_f = c_ref[:, g * st:(g + 1) * st]
            bg = bg_f.astype(BF16)
            cg = cg_f.astype(BF16)
            gs = slice(g * gw, (g + 1) * gw)
            gmat = _nt(cg, bg)
            gmat_t = _nt(bg, cg)
            sing = s_in[:, gs].astype(BF16)
            dsg = d_s[:, gs].astype(BF16)
            y0 = _nn(cg, sing)
            dxw = _nn(bg, dsg)
            d_bg = _nt(xw[:, gs], dsg)
            d_cg = _nt(dy0[:, gs].astype(BF16), sing)
            ds_in_g = _tn(cg, dy0[:, gs].astype(BF16))
            dg = jnp.zeros((q, q), F32)
            dxdt_g = []
            for tt in range(tiles_per_group):
                tile = g * tiles_per_group + tt
                ts = slice(tile * LANE, (tile + 1) * LANE)
                xt = xdt[:, ts]
                dyt = dyv[:, ts]
                dyhs, lmats, mts = [], [], []
                for hh in range(hpt):
                    lmat, lmat_t = _head_l(cq, tile * hpt + hh, tri, tri_t)
                    inhead = jnp.logical_and(lane >= hh * cfg.hd, lane < (hh + 1) * cfg.hd)
                    dyhs.append(jnp.where(inhead, dyt, 0.0).astype(BF16))
                    lmats.append(lmat)
                    mts.append((gmat_t * lmat_t).astype(BF16))
                dy_stack = jnp.concatenate(dyhs, axis=0)
                dm_all = _nt(dy_stack, xt)
                for hh in range(hpt):
                    h = tile * hpt + hh
                    dm = dm_all[hh * q:(hh + 1) * q, :]
                    dg = dg + dm * lmats[hh]
                    qm = dm * gmat * lmats[hh]
                    rs = jnp.sum(qm, axis=1, keepdims=True)
                    csum = jnp.sum(qm, axis=0, keepdims=True)
                    dcs = dcs + jnp.where(lane == h, rs, 0.0)
                    dcs_t = dcs_t + jnp.where(sub == h, csum, 0.0)
                dxdt_g.append(_nn(jnp.concatenate(mts, axis=1), dy_stack))
            dxdt_diag = jnp.concatenate(dxdt_g, axis=1) if len(dxdt_g) > 1 else dxdt_g[0]
            dgb = dg.astype(BF16)
            d_cg = d_cg + _nn(dgb, bg)
            d_bg = d_bg + _tn(dgb, cg)
            dx_ref[:, inner + g * st:inner + (g + 1) * st] = d_bg
            dx_ref[:, inner + bcw + g * st:inner + bcw + (g + 1) * st] = d_cg
            dxdt = dxdt_diag + dxw * cq["W0"][:, gs]
            dx_ref[:, gs] = dyv[:, gs] * d_ref[:, gs] + dxdt * cq["DT"][:, gs]
            rt = rexpt[gs, :]
            dcs = dcs + red(dyv[:, gs] * y0 * cq["E"][:, gs], rt)
            r_w = red(dxw * xw_f[:, gs], rt)
            dcs = dcs - r_w
            dcs_last_g = jnp.sum(r_w, axis=0, keepdims=True)
            ddec = red(jnp.broadcast_to(jnp.sum(d_s[:, gs] * s_in[:, gs], axis=0, keepdims=True), (8, gw)), rt)[0:1, :]
            dcs_last_g = dcs_last_g + ddec * cq["decay"]
            dcs = dcs + jnp.where(lax.broadcasted_iota(jnp.int32, (q, LANE), 0) == q - 1, dcs_last_g, 0.0)
            ddt_part = red(dxdt * xs[:, gs], rt)
            if g == 0:
                ddt = ddt_part
            else:
                ddt = ddt + ddt_part
            ds_scr[:, gs] = d_s[:, gs] * cq["DEC"][:, gs] + ds_in_g
        dcs = dcs - dcs_t.T
        dadt = _sel_dot(dcs, ltri_v, left=True, trans=True)
        ddt = ddt + dadt * a_ref[...]
        da_ref[...] += jnp.sum(dadt * cq["dt"], axis=0, keepdims=True)
        draw = jnp.where(cq["live"], ddt * jax.nn.sigmoid(cq["pre"]), 0.0)
        ddt_ref[...] = draw
        dbias_ref[...] += jnp.sum(draw, axis=0, keepdims=True)

    def rowblk(width, col):
        return pl.BlockSpec((q, width), lambda b, s: (b * nc + nc - 1 - s, col))

    def const(shape):
        return pl.BlockSpec(shape, lambda b, s: (0, 0))

    bcol = inner // bcw
    outs = pl.pallas_call(
        body, name=name, grid=(cfg.bsz, nc),
        in_specs=[rowblk(inner, 0), rowblk(bcw, bcol), rowblk(bcw, bcol + 1), rowblk(LANE, cfg.dtt),
                  const((1, LANE)), const((1, LANE)), const((1, inner)), const((q, q)), const((LANE, inner)),
                  const((inner, LANE)),
                  pl.BlockSpec((1, st, inner), lambda b, s: (b * nc + nc - 1 - s, 0, 0)), rowblk(inner, 0)],
        out_specs=[rowblk(cfg.conv_dim, 0), rowblk(LANE, 0),
                   const((1, inner)), const((1, LANE)), const((1, LANE))],
        out_shape=[_sds((cfg.t, cfg.conv_dim), F32),
                   _sds((cfg.t, LANE), F32), _sds((1, inner), F32), _sds((1, LANE), F32), _sds((1, LANE), F32)],
        scratch_shapes=[pltpu.VMEM((st, inner), F32)], compiler_params=_cp(),
    )(xc, xc, xc, small, dt_bias, avec, dexp, ltri, rexp, rexp_t, sin, dy)
    return outs


def tail_fwd(cfg, y, z, w, *, name):
    t, inner, gw = cfg.t, cfg.inner, cfg.gw
    tr = _pick(t, 272, 16)

    def body(y_ref, z_ref, w_ref, o_ref):
        for g in range(cfg.groups):
            gs = slice(g * gw, (g + 1) * gw)
            yg = y_ref[:, gs].astype(F32) * _silu(z_ref[:, gs].astype(F32))
            r = lax.rsqrt(jnp.mean(yg * yg, axis=-1, keepdims=True) + EPS)
            o_ref[:, gs] = (yg * r * w_ref[:, gs]).astype(BF16)

    row = pl.BlockSpec((tr, inner), lambda i: (i, 0))
    return pl.pallas_call(
        body, name=name, grid=(t // tr,), in_specs=[row, row, pl.BlockSpec((1, inner), lambda i: (0, 0))],
        out_specs=row, out_shape=_sds((t, inner), BF16), compiler_params=_cp(),
    )(y, z, w.reshape(1, inner))


def tail_bwd(cfg, do, y, z, w, *, name):
    t, inner, gw = cfg.t, cfg.inner, cfg.gw
    tr = _pick(t, 272, 16)

    def body(do_ref, y_ref, z_ref, w_ref, dy_ref, dz_ref, dw_ref):
        @pl.when(pl.program_id(0) == 0)
        def _():
            dw_ref[...] = jnp.zeros_like(dw_ref)

        for g in range(cfg.groups):
            gs = slice(g * gw, (g + 1) * gw)
            yv = y_ref[:, gs].astype(F32)
            zv = z_ref[:, gs].astype(F32)
            dov = do_ref[:, gs].astype(F32)
            sz = _silu(zv)
            yg = yv * sz
            r = lax.rsqrt(jnp.mean(yg * yg, axis=-1, keepdims=True) + EPS)
            xh = yg * r
            gg = dov * w_ref[:, gs]
            dyg = r * (gg - xh * jnp.mean(gg * xh, axis=-1, keepdims=True))
            dw_ref[:, gs] += jnp.sum(dov * xh, axis=0, keepdims=True)
            dy_ref[:, gs] = (dyg * sz).astype(BF16)
            dz_ref[:, gs] = (dyg * yv * _dsilu(zv)).astype(BF16)

    row = pl.BlockSpec((tr, inner), lambda i: (i, 0))
    vec = pl.BlockSpec((1, inner), lambda i: (0, 0))
    dy, dz, dw = pl.pallas_call(
        body, name=name, grid=(t // tr,), in_specs=[row, row, row, vec], out_specs=[row, row, vec],
        out_shape=[_sds((t, inner), BF16), _sds((t, inner), BF16), _sds((1, inner), F32)], compiler_params=_cp(),
    )(do, y, z, w.reshape(1, inner))
    return dy, dz, dw[0]


def rope_tables(cfg):
    half = cfg.rope // 2
    pos = np.maximum(np.arange(cfg.lp) - cfg.pad, 0).astype(np.float32)
    inv = ROPE_THETA ** (-jnp.arange(0, cfg.rope, 2, dtype=F32) / cfg.rope)
    ang = jnp.asarray(pos)[:, None] * inv[None, :]
    cos, sin = jnp.cos(ang), jnp.sin(ang)
    zero = jnp.zeros((cfg.lp, LANE - 2 * half), F32)
    zh = jnp.zeros((cfg.lp, half), F32)
    ctab = jnp.concatenate([cos, cos, zero], axis=1)
    s1 = jnp.concatenate([-sin, zh, zero], axis=1)
    s2 = jnp.concatenate([zh, sin, zero], axis=1)
    return ctab, s1, s2


def _rope(x, c, s1, s2, half):
    return x * c + pltpu.roll(x, LANE - half, 1) * s1 + pltpu.roll(x, half, 1) * s2


def _rope_t(dy, c, s1, s2, half):
    return dy * c + pltpu.roll(dy * s1, half, 1) + pltpu.roll(dy * s2, LANE - half, 1)


def _attn_scale(cfg):
    return (cfg.nope + cfg.rope) ** -0.5


def rope_fwd(cfg, qf, small, tabs, *, name):
    t, qw, lp = cfg.t, cfg.qw, cfg.lp
    tr = _pick(lp, 544, 16)
    nrb = lp // tr
    half = cfg.rope // 2
    scale = _attn_scale(cfg)

    def body(q_ref, k_ref, c_ref, s1_ref, s2_ref, qo_ref, ko_ref):
        c, s1, s2 = c_ref[...], s1_ref[...], s2_ref[...]
        for h in range(cfg.mh):
            a = h * 2 * LANE
            qo_ref[:, a:a + LANE] = (q_ref[:, a:a + LANE].astype(F32) * scale).astype(BF16)
            qo_ref[:, a + LANE:a + 2 * LANE] = (
                _rope(q_ref[:, a + LANE:a + 2 * LANE].astype(F32), c, s1, s2, half) * scale).astype(BF16)
        ko_ref[...] = _rope(k_ref[...], c, s1, s2, half).astype(BF16)

    tab = pl.BlockSpec((tr, LANE), lambda i: (i % nrb, 0))
    return pl.pallas_call(
        body, name=name, grid=(t // tr,),
        in_specs=[pl.BlockSpec((tr, qw), lambda i: (i, 0)), pl.BlockSpec((tr, LANE), lambda i: (i, cfg.kt)), tab, tab, tab],
        out_specs=[pl.BlockSpec((tr, qw), lambda i: (i, 0)), pl.BlockSpec((tr, LANE), lambda i: (i, 0))],
        out_shape=[_sds((t, qw), BF16), _sds((t, LANE), BF16)], compiler_params=_cp(),
    )(qf, small, *tabs)


def rope_bwd(cfg, dq, dkpe, tabs, *, name):
    t, qw, lp = cfg.t, cfg.qw, cfg.lp
    tr = _pick(lp, 544, 16)
    nrb = lp // tr
    half = cfg.rope // 2
    scale = _attn_scale(cfg)

    def body(dq_ref, dk_ref, c_ref, s1_ref, s2_ref, qo_ref, ko_ref):
        c, s1, s2 = c_ref[...], s1_ref[...], s2_ref[...]
        for h in range(cfg.mh):
            a = h * 2 * LANE
            qo_ref[:, a:a + LANE] = (dq_ref[:, a:a + LANE].astype(F32) * scale).astype(BF16)
            qo_ref[:, a + LANE:a + 2 * LANE] = _rope_t(
                dq_ref[:, a + LANE:a + 2 * LANE].astype(F32) * scale, c, s1, s2, half).astype(BF16)
        ko_ref[...] = _rope_t(dk_ref[...], c, s1, s2, half)

    tab = pl.BlockSpec((tr, LANE), lambda i: (i % nrb, 0))
    return pl.pallas_call(
        body, name=name, grid=(t // tr,),
        in_specs=[pl.BlockSpec((tr, qw), lambda i: (i, 0)), pl.BlockSpec((tr, LANE), lambda i: (i, 0)),
                  tab, tab, tab],
        out_specs=[pl.BlockSpec((tr, qw), lambda i: (i, 0)), pl.BlockSpec((tr, LANE), lambda i: (i, 0))],
        out_shape=[_sds((t, qw), BF16), _sds((t, LANE), F32)], compiler_params=_cp(),
    )(dq, dkpe, *tabs)


def _q_blocks(cfg):
    bounds = [0, cfg.chunk] + list(range(cfg.chunk + 256, cfg.lp + 1, 256))
    assert bounds[-1] == cfg.lp, "SEQ must be a multiple of 256"
    return list(zip(bounds[:-1], bounds[1:]))


def _attn_mask(cfg, qs, qe):
    rows = qs + lax.broadcasted_iota(jnp.int32, (qe - qs, qe), 0)
    cols = lax.broadcasted_iota(jnp.int32, (qe - qs, qe), 1)
    return jnp.logical_and(cols <= rows, jnp.logical_or(cols >= cfg.pad, rows < cfg.pad))


def _max_q_block(cfg):
    return max(qe - qs for qs, qe in _q_blocks(cfg))


def _masked_scores(cfg, q, k2, qs, qe, s_scr):
    bq, n = qe - qs, qe
    s_scr[0:bq, 0:n] = _nt(q, k2)
    if qs == 0:
        s_scr[0:bq, 0:n] = jnp.where(_attn_mask(cfg, 0, qe), s_scr[0:bq, 0:n], MASK_VALUE)
    else:
        assert qs >= cfg.chunk and cfg.pad < LANE
        cols = lax.broadcasted_iota(jnp.int32, (bq, LANE), 1)
        s_scr[0:bq, 0:LANE] = jnp.where(cols >= cfg.pad, s_scr[0:bq, 0:LANE], MASK_VALUE)
        r = lax.broadcasted_iota(jnp.int32, (bq, bq), 0)
        c = lax.broadcasted_iota(jnp.int32, (bq, bq), 1)
        s_scr[0:bq, qs:qe] = jnp.where(c <= r, s_scr[0:bq, qs:qe], MASK_VALUE)
    return s_scr[0:bq, 0:n]


def attn_fwd(cfg, qr, kv, kpe, *, name):
    lp, t, mh = cfg.lp, cfg.t, cfg.mh
    assert mh <= LANE
    blocks = _q_blocks(cfg)

    def body(q_ref, kv_ref, kp_ref, o_ref, l_ref, s_scr):
        h = pl.program_id(1)

        @pl.when(h == 0)
        def _():
            l_ref[...] = jnp.zeros_like(l_ref)

        for qs, qe in blocks:
            n = qe
            q = q_ref[qs:qe, :]
            k2 = jnp.concatenate([kv_ref[0:n, 0:LANE], kp_ref[0:n, :]], axis=1)
            s = _masked_scores(cfg, q, k2, qs, qe, s_scr)
            m = jnp.max(s, axis=-1, keepdims=True)
            p = jnp.exp(s - m)
            l = jnp.sum(p, axis=-1, keepdims=True)
            o_ref[qs:qe, :] = (_nn(p.astype(BF16), kv_ref[0:n, LANE:2 * LANE]) * (1.0 / l)).astype(BF16)
            lane = lax.broadcasted_iota(jnp.int32, (qe - qs, LANE), 1)
            l_ref[qs:qe, :] = jnp.where(lane == h, m + jnp.log(l), l_ref[qs:qe, :])

    hb = pl.BlockSpec((lp, 2 * LANE), lambda b, h: (b, h))
    ob = pl.BlockSpec((lp, LANE), lambda b, h: (b, h))
    return pl.pallas_call(
        body, name=name, grid=(cfg.bsz, mh),
        in_specs=[hb, hb, pl.BlockSpec((lp, LANE), lambda b, h: (b, 0))],
        out_specs=[ob, pl.BlockSpec((lp, LANE), lambda b, h: (b, 0))],
        out_shape=[_sds((t, mh * LANE), BF16), _sds((t, LANE), F32)],
        scratch_shapes=[pltpu.VMEM((_max_q_block(cfg), lp), F32)], compiler_params=_cp(),
    )(qr, kv, kpe)


def attn_bwd(cfg, qr, kv, kpe, o, lse, do, *, name):
    lp, t, mh = cfg.lp, cfg.t, cfg.mh
    blocks = _q_blocks(cfg)

    def body(q_ref, kv_ref, kp_ref, o_ref, l_ref, do_ref, dq_ref, dkv_ref, dkp_ref, dk_acc, dv_acc, s_scr):
        dk_acc[...] = jnp.zeros_like(dk_acc)
        dv_acc[...] = jnp.zeros_like(dv_acc)
        for qs, qe in blocks:
            n = qe
            q = q_ref[qs:qe, :]
            k2 = jnp.concatenate([kv_ref[0:n, 0:LANE], kp_ref[0:n, :]], axis=1)
            dob = do_ref[qs:qe, :].astype(BF16)
            delta = jnp.sum(dob.astype(F32) * o_ref[qs:qe, :].astype(F32), axis=-1, keepdims=True)
            s = _masked_scores(cfg, q, k2, qs, qe, s_scr)
            lane = lax.broadcasted_iota(jnp.int32, (qe - qs, LANE), 1)
            lse = jnp.sum(jnp.where(lane == pl.program_id(1), l_ref[qs:qe, :], 0.0), axis=-1, keepdims=True)
            p = jnp.exp(s - lse)
            dp = _nt(dob, kv_ref[0:n, LANE:2 * LANE])
            ds = (p * (dp - delta)).astype(BF16)
            dq_ref[qs:qe, :] = _nn(ds, k2).astype(BF16)
            dv_acc[0:n, :] += _tn(p.astype(BF16), dob)
            dk_acc[0:n, :] += _tn(ds, q)
        dkv_ref[:, 0:LANE] = dk_acc[:, 0:LANE].astype(BF16)
        dkv_ref[:, LANE:2 * LANE] = dv_acc[...].astype(BF16)
        @pl.when(pl.program_id(1) == 0)
        def _():
            dkp_ref[...] = dk_acc[:, LANE:2 * LANE]

        @pl.when(pl.program_id(1) > 0)
        def _():
            dkp_ref[...] += dk_acc[:, LANE:2 * LANE]

    hb = pl.BlockSpec((lp, 2 * LANE), lambda b, h: (b, h))
    ob = pl.BlockSpec((lp, LANE), lambda b, h: (b, h))
    return pl.pallas_call(
        body, name=name, grid=(cfg.bsz, mh),
        in_specs=[hb, hb, pl.BlockSpec((lp, LANE), lambda b, h: (b, 0)), ob,
                  pl.BlockSpec((lp, LANE), lambda b, h: (b, 0)), ob],
        out_specs=[hb, hb, pl.BlockSpec((lp, LANE), lambda b, h: (b, 0))],
        out_shape=[_sds((t, cfg.qw), BF16), _sds((t, mh * 2 * LANE), BF16), _sds((t, LANE), F32)],
        scratch_shapes=[pltpu.VMEM((lp, 2 * LANE), F32), pltpu.VMEM((lp, LANE), F32),
                        pltpu.VMEM((_max_q_block(cfg), lp), F32)], compiler_params=_cp(),
    )(qr, kv, kpe, o, lse, do)


def _live_rows(cfg, tr, shape):
    rows = pl.program_id(1) * tr + lax.broadcasted_iota(jnp.int32, shape, 0)
    return rows >= cfg.pad


def gate_fwd(cfg, ya, yb, g, *, name):
    d, lp = cfg.d, cfg.lp
    tr = _pick(lp, 544, 16)
    nrb = lp // tr

    def body(ya_ref, yb_ref, ga_ref, gb_ref, o_ref):
        f = lambda ref: ref[...].astype(F32)
        mix = jax.nn.sigmoid(f(ga_ref)) * f(ya_ref) + jax.nn.sigmoid(f(gb_ref)) * f(yb_ref)
        o_ref[...] = jnp.where(_live_rows(cfg, tr, mix.shape), mix, 0.0).astype(BF16)

    row = pl.BlockSpec((tr, d), lambda b, j: (b * nrb + j, 0))
    row1 = pl.BlockSpec((tr, d), lambda b, j: (b * nrb + j, 1))
    return pl.pallas_call(
        body, name=name, grid=(cfg.bsz, nrb), in_specs=[row, row, row, row1], out_specs=row,
        out_shape=_sds((cfg.t, d), BF16), compiler_params=_cp(),
    )(ya, yb, g, g)


def gate_bwd(cfg, dmix, ya, yb, g, *, name):
    d, lp = cfg.d, cfg.lp
    tr = _pick(lp, 544, 16)
    nrb = lp // tr

    def body(dm_ref, ya_ref, yb_ref, ga_ref, gb_ref, dya_ref, dyb_ref, dg_ref):
        dm = dm_ref[...].astype(F32)
        dm = jnp.where(_live_rows(cfg, tr, dm.shape), dm, 0.0)
        sa = jax.nn.sigmoid(ga_ref[...].astype(F32))
        sb = jax.nn.sigmoid(gb_ref[...].astype(F32))
        dya_ref[...] = (dm * sa).astype(BF16)
        dyb_ref[...] = (dm * sb).astype(BF16)
        dg_ref[:, 0:d] = (dm * ya_ref[...].astype(F32) * sa * (1.0 - sa)).astype(BF16)
        dg_ref[:, d:2 * d] = (dm * yb_ref[...].astype(F32) * sb * (1.0 - sb)).astype(BF16)

    row = pl.BlockSpec((tr, d), lambda b, j: (b * nrb + j, 0))
    row1 = pl.BlockSpec((tr, d), lambda b, j: (b * nrb + j, 1))
    row2 = pl.BlockSpec((tr, 2 * d), lambda b, j: (b * nrb + j, 0))
    return pl.pallas_call(
        body, name=name, grid=(cfg.bsz, nrb), in_specs=[row, row, row, row, row1], out_specs=[row, row, row2],
        out_shape=[_sds((cfg.t, d), BF16), _sds((cfg.t, d), BF16), _sds((cfg.t, 2 * d), BF16)], compiler_params=_cp(),
    )(dmix, ya, yb, g, g)


def loss_head(cfg, h, target, w, *, name):
    d, q, nc = cfg.d, cfg.chunk, cfg.nchunks
    tpb = cfg.seq // q

    def body(h_ref, t_ref, w_ref, loss_ref, dh_ref, dw_ref, dhb_ref):
        j = pl.program_id(1)

        @pl.when(jnp.logical_and(j == 0, pl.program_id(0) == 0))
        def _():
            loss_ref[...] = jnp.zeros_like(loss_ref)
            dw_ref[...] = jnp.zeros_like(dw_ref)

        @pl.when(j == 0)
        def _():
            dh_ref[...] = jnp.zeros_like(dh_ref)
            dhb_ref[...] = jnp.zeros_like(dhb_ref)

        @pl.when(j > 0)
        def _():
            xv = h_ref[...]
            r = lax.rsqrt(jnp.mean(xv * xv, axis=-1, keepdims=True) + EPS)
            xh = xv * r
            err = xh * w_ref[...] - t_ref[...]
            loss_ref[...] += 0.5 * jnp.sum(jnp.sum(err * err, axis=-1, keepdims=True) / d, axis=0, keepdims=True)
            dy = err * (1.0 / d)
            g = dy * w_ref[...]
            dh = r * (g - xh * jnp.mean(g * xh, axis=-1, keepdims=True))
            dh_ref[...] = dh
            dhb_ref[...] = dh.astype(BF16)
            dw_ref[...] += jnp.sum(dy * xh, axis=0, keepdims=True)

    row = pl.BlockSpec((q, d), lambda b, j: (b * nc + j, 0))
    loss, dh, dw, dhb = pl.pallas_call(
        body, name=name, grid=(cfg.bsz, nc),
        in_specs=[row, pl.BlockSpec((q, d), lambda b, j: (b * tpb + jnp.maximum(j - 1, 0), 0)),
                  pl.BlockSpec((1, d), lambda b, j: (0, 0))],
        out_specs=[pl.BlockSpec((8, LANE), lambda b, j: (0, 0)), row, pl.BlockSpec((1, d), lambda b, j: (0, 0)), row],
        out_shape=[_sds((8, LANE), F32), _sds((cfg.t, d), F32), _sds((1, d), F32), _sds((cfg.t, d), BF16)],
        compiler_params=_cp(),
    )(h, target, w.reshape(1, d))
    return loss[0, 0], (dh, dhb), dw[0]


def _rows_tile(r, c):
    return _pick(r, max(8, (1 << 18) // max(c, 1) // 8 * 8), 8)


def _adam_update(w, g, m, v):
    c1 = 1.0 - ADAM_B1 ** ADAM_STEP
    c2 = 1.0 - ADAM_B2 ** ADAM_STEP
    mn = ADAM_B1 * m + (1.0 - ADAM_B1) * g
    vn = ADAM_B2 * v + (1.0 - ADAM_B2) * (g * g)
    delta = -ADAM_LR * ((mn / c1) / (jnp.sqrt(vn / c2) + ADAM_EPS) + ADAM_WD * w)
    return delta, mn, vn


def adamw_layer(w, m, v, g, li, prev, dep, *, name):
    _, r, c = w.shape
    tr = _rows_tile(r, c)

    def body(*refs):
        w_ref, m_ref, v_ref, g_ref = refs[:4]
        go_ref, d_ref, mo_ref, vo_ref = refs[-4:]
        gv = g_ref[...]
        delta, mn, vn = _adam_update(w_ref[0], gv, m_ref[0], v_ref[0])
        go_ref[0] = gv
        d_ref[0] = delta
        mo_ref[0] = mn
        vo_ref[0] = vn

    if tr * c * 4 >= (1 << 16):
        steps = r // tr
        blk3 = pl.BlockSpec((1, tr, c), lambda i: (li, i, 0))
        blk2 = pl.BlockSpec((tr, c), lambda i: (i, 0))
    else:
        tc = _pick(c, max(LANE, (1 << 18) // r // LANE * LANE), LANE)
        steps = c // tc
        blk3 = pl.BlockSpec((1, r, tc), lambda i: (li, 0, i))
        blk2 = pl.BlockSpec((r, tc), lambda i: (0, i))
    anyspec = pl.BlockSpec(memory_space=pl.ANY)
    in_specs = [blk3, blk3, blk3, blk2, anyspec]
    args = [w, m, v, g, dep]
    aliases = {}
    if prev is not None:
        in_specs += [anyspec] * 4
        args += list(prev)
        aliases = {5 + i: i for i in range(4)}
    return pl.pallas_call(
        body, name=name, grid=(steps,), in_specs=in_specs, out_specs=[blk3] * 4,
        out_shape=[_sds(w.shape, F32)] * 4, input_output_aliases=aliases, compiler_params=_cp(),
    )(*args)


def pair_add(g4, other, half, *, name):
    n, _, r, c = g4.shape
    tr = _rows_tile(r, c)

    def body(h_ref, a_ref, b_ref, o_ref):
        o_ref[0] = (a_ref[0, 0].astype(F32) + b_ref[0].astype(F32)).astype(BF16)

    blk = pl.BlockSpec((1, tr, c), lambda j, i, h: (j, i, 0))
    grid_spec = pltpu.PrefetchScalarGridSpec(
        num_scalar_prefetch=1, grid=(n, r // tr),
        in_specs=[pl.BlockSpec((1, 1, tr, c), lambda j, i, h: (j, h[0], i, 0)), blk], out_specs=blk)
    return pl.pallas_call(body, name=name, grid_spec=grid_spec, out_shape=_sds((n, r, c), BF16),
                          compiler_params=_cp())(half, g4, other)


def chip_sum(recv, part, where, *, name):
    n, r, c = recv.shape
    tr = _rows_tile(r, c)

    def body(s_ref, *refs):
        own_ref, o_ref = refs[n], refs[n + 1]
        acc = None
        for j in range(n):
            term = jnp.where(s_ref[0] == j, own_ref[0], refs[j][0]).astype(F32)
            acc = term if acc is None else acc + term
        o_ref[0] = acc

    def slot(j):
        return pl.BlockSpec((1, tr, c), lambda i, s: (jnp.where(s[0] == j, (j + 1) % n, j), i, 0))

    grid_spec = pltpu.PrefetchScalarGridSpec(
        num_scalar_prefetch=1, grid=(r // tr,),
        in_specs=[slot(j) for j in range(n)] + [pl.BlockSpec((1, tr, c), lambda i, s: (s[0], i, 0))],
        out_specs=pl.BlockSpec((1, tr, c), lambda i, s: (s[1], i, 0)))
    return pl.pallas_call(body, name=name, grid_spec=grid_spec, out_shape=_sds((2, r, c), F32),
                          compiler_params=_cp())(where, *([recv] * n), part)


def _coords():
    return lax.axis_index("x"), lax.axis_index("y"), lax.axis_index("c")


def _other_chips(x, y):
    return [(1 - x, y), (x, 1 - y), (1 - x, 1 - y)]


def gather_chips(arrs, *, name):
    n = len(arrs)
    anyspec = pl.BlockSpec(memory_space=pl.ANY)

    def body(*refs):
        ins, outs = refs[:n], refs[n:2 * n]
        send_sems, recv_sems, local_sems = refs[2 * n:]
        x, y, c = _coords()
        me = 2 * x + y
        chips = _other_chips(x, y)
        copies = []
        for k in range(n):
            loc = pltpu.make_async_copy(ins[k], outs[k].at[me], local_sems.at[k])
            loc.start()
            copies.append(loc)
        sends = []
        for k in range(n):
            for j, (px, py) in enumerate(chips):
                cp = pltpu.make_async_remote_copy(
                    src_ref=ins[k], dst_ref=outs[k].at[me], send_sem=send_sems.at[k, j], recv_sem=recv_sems.at[k, j],
                    device_id=(px, py, c), device_id_type=MESH)
                cp.start()
                sends.append(cp)
        for k in range(n):
            for j, (px, py) in enumerate(chips):
                pltpu.make_async_remote_copy(
                    src_ref=ins[k], dst_ref=outs[k].at[2 * px + py], send_sem=send_sems.at[k, j],
                    recv_sem=recv_sems.at[k, j], device_id=(px, py, c), device_id_type=MESH).wait_recv()
        for cp in sends:
            cp.wait_send()
        for cp in copies:
            cp.wait()

    return pl.pallas_call(
        body, name=name, in_specs=[anyspec] * n, out_specs=[anyspec] * n,
        out_shape=[_sds((4,) + a.shape, a.dtype) for a in arrs],
        scratch_shapes=[pltpu.SemaphoreType.DMA((n, 3)), pltpu.SemaphoreType.DMA((n, 3)), pltpu.SemaphoreType.DMA((n,))],
        compiler_params=_cp(has_side_effects=True),
    )(*arrs)


def allreduce_small(vec, after, *, name):
    r, c = vec.shape

    def body(v_ref, after_ref, o_ref, buf, send_sems, recv_sems):
        x, y, cc = _coords()
        me = 4 * x + 2 * y + cc
        buf[me] = v_ref[...]
        sends = []
        flips = [(fx, fy, fc) for fx in (0, 1) for fy in (0, 1) for fc in (0, 1)][1:]
        for j, (fx, fy, fc) in enumerate(flips):
            peer = ((1 - x) if fx else x, (1 - y) if fy else y, (1 - cc) if fc else cc)
            cp = pltpu.make_async_remote_copy(
                src_ref=v_ref, dst_ref=buf.at[me], send_sem=send_sems.at[j], recv_sem=recv_sems.at[j],
                device_id=peer, device_id_type=MESH)
            cp.start()
            sends.append(cp)
        for j, (fx, fy, fc) in enumerate(flips):
            px, py, pc = ((1 - x) if fx else x, (1 - y) if fy else y, (1 - cc) if fc else cc)
            pltpu.make_async_remote_copy(
                src_ref=v_ref, dst_ref=buf.at[4 * px + 2 * py + pc], send_sem=send_sems.at[j],
                recv_sem=recv_sems.at[j], device_id=(px, py, pc), device_id_type=MESH).wait_recv()
        for cp in sends:
            cp.wait_send()
        acc = buf[0]
        for k in range(1, 8):
            acc = acc + buf[k]
        o_ref[...] = acc

    vm = pl.BlockSpec(memory_space=pltpu.VMEM)
    return pl.pallas_call(
        body, name=name, in_specs=[vm, pl.BlockSpec(memory_space=pl.ANY)], out_specs=vm, out_shape=_sds((r, c), F32),
        scratch_shapes=[pltpu.VMEM((8, r, c), F32), pltpu.SemaphoreType.DMA((7,)), pltpu.SemaphoreType.DMA((7,))],
        compiler_params=_cp(has_side_effects=True),
    )(vec, after)


def pair_share(lands, owns, *, name):
    n = len(lands)
    anyspec = pl.BlockSpec(memory_space=pl.ANY)

    def body(*refs):
        ins, own_refs, outs = refs[:n], refs[n:2 * n], refs[2 * n:3 * n]
        send_sems, recv_sems = refs[3 * n:]
        x, y, c = _coords()
        me = 2 * x + y
        sib = (x, y, 1 - c)
        sends = []
        for k in range(n):
            for j, (px, py) in enumerate(_other_chips(x, y)):
                cp = pltpu.make_async_remote_copy(
                    src_ref=ins[k].at[2 * px + py, c], dst_ref=outs[k].at[2 * px + py, c], send_sem=send_sems.at[k, j],
                    recv_sem=recv_sems.at[k, j], device_id=sib, device_id_type=MESH)
                cp.start()
                sends.append(cp)
            cp = pltpu.make_async_remote_copy(
                src_ref=own_refs[k], dst_ref=outs[k].at[me], send_sem=send_sems.at[k, 3], recv_sem=recv_sems.at[k, 3],
                device_id=sib, device_id_type=MESH)
            cp.start()
            sends.append(cp)
        for k in range(n):
            for j, (px, py) in enumerate(_other_chips(x, y)):
                pltpu.make_async_remote_copy(
                    src_ref=ins[k].at[2 * px + py, c], dst_ref=outs[k].at[2 * px + py, 1 - c],
                    send_sem=send_sems.at[k, j], recv_sem=recv_sems.at[k, j], device_id=sib,
                    device_id_type=MESH).wait_recv()
            pltpu.make_async_remote_copy(
                src_ref=own_refs[k], dst_ref=outs[k].at[me], send_sem=send_sems.at[k, 3], recv_sem=recv_sems.at[k, 3],
                device_id=sib, device_id_type=MESH).wait_recv()
        for cp in sends:
            cp.wait_send()

    return pl.pallas_call(
        body, name=name, in_specs=[anyspec] * (2 * n), out_specs=[anyspec] * n,
        out_shape=[_sds(a.shape, a.dtype) for a in lands], input_output_aliases={k: k for k in range(n)},
        scratch_shapes=[pltpu.SemaphoreType.DMA((n, 4)), pltpu.SemaphoreType.DMA((n, 4))],
        compiler_params=_cp(has_side_effects=True),
    )(*lands, *owns)


_HBM = pl.BlockSpec(memory_space=pltpu.HBM)
_SEM = pl.BlockSpec(memory_space=pltpu.SEMAPHORE)


_COPIES_PER_ARRAY = {"gather": 3, "scatter": 3, "share": 4, "exchange": 4, "fill": 1}


def _ici_copies(kind, srcs, lands, send_sems, recv_sems):
    x, y, c = _coords()
    me = 2 * x + y
    per = _COPIES_PER_ARRAY[kind]
    sends, recvs = [], []
    for k in range(len(srcs)):
        triples = []
        for j, (px, py) in enumerate(_other_chips(x, y)):
            peer = 2 * px + py
            if kind == "gather":
                triples.append((srcs[k].at[c], lands[k].at[me, c], lands[k].at[peer, c], (px, py, c)))
            elif kind == "scatter":
                triples.append((srcs[k].at[peer], lands[k].at[me], lands[k].at[peer], (px, py, c)))
            elif kind == "share":
                triples.append((lands[k].at[peer, c], lands[k].at[peer, c], lands[k].at[peer, 1 - c], (x, y, 1 - c)))
        if kind == "share":
            triples.append((srcs[k], lands[k].at[me], lands[k].at[me], (x, y, 1 - c)))
        if kind == "exchange":
            triples = [(srcs[k].at[j, 1 - c], lands[k].at[j], lands[k].at[j], (x, y, 1 - c)) for j in range(4)]
        if kind == "fill":
            triples = [(lands[k].at[c], lands[k].at[c], lands[k].at[1 - c], (x, y, 1 - c))]
        for j, (src, there, here, dev) in enumerate(triples):
            sem = per * k + j
            mk = functools.partial(pltpu.make_async_remote_copy, src_ref=src, send_sem=send_sems.at[sem],
                                   recv_sem=recv_sems.at[sem], device_id=dev, device_id_type=MESH)
            sends.append(mk(dst_ref=there))
            recvs.append(mk(dst_ref=here))
    return sends, recvs


def ici_start(kind, srcs, lands, after, *, name):
    n = len(srcs)

    def body(*refs):
        src_refs, land_refs = refs[:n], refs[n:2 * n]
        send_sems, recv_sems = refs[2 * n + 1], refs[2 * n + 2]
        token = refs[-1]
        sends, _ = _ici_copies(kind, src_refs, land_refs, send_sems, recv_sems)
        for cp in sends:
            cp.start()
        token[...] = jnp.zeros_like(token)

    both = list(srcs) + list(lands)
    out = pl.pallas_call(
        body, name=name,
        in_specs=[_HBM] * (2 * n) + [pl.BlockSpec(memory_space=pl.ANY)],
        out_shape=(pltpu.SemaphoreType.DMA((_COPIES_PER_ARRAY[kind] * n,)),
                   pltpu.SemaphoreType.DMA((_COPIES_PER_ARRAY[kind] * n,)),
                   *[pltpu.HBM(a.shape, a.dtype) for a in both], _sds((8, LANE), F32)),
        out_specs=(_SEM, _SEM, *([_HBM] * (2 * n)), pl.BlockSpec(memory_space=pltpu.VMEM)),
        input_output_aliases={i: 2 + i for i in range(2 * n)},
        compiler_params=_cp(has_side_effects=pltpu.SideEffectType.DATAFLOW_SIDE_EFFECTING),
    )(*[pltpu.with_memory_space_constraint(a, pltpu.HBM) for a in both], after)
    return out[0], out[1], list(out[2:2 + n]), list(out[2 + n:2 + 2 * n]), out[-1]


def ici_wait(kind, started, after, *, name):
    send_sems, recv_sems, srcs, lands, _ = started
    n = len(srcs)

    def body(*refs):
        src_refs, land_refs = refs[:n], refs[n:2 * n]
        sends, recvs = _ici_copies(kind, src_refs, land_refs, refs[2 * n], refs[2 * n + 1])
        for cp in sends:
            cp.wait_send()
        for cp in recvs:
            cp.wait_recv()

    both = list(srcs) + list(lands)
    out = pl.pallas_call(
        body, name=name,
        in_specs=[_HBM] * (2 * n) + [_SEM, _SEM, pl.BlockSpec(memory_space=pl.ANY)],
        out_shape=tuple(pltpu.HBM(a.shape, a.dtype) for a in both), out_specs=tuple([_HBM] * (2 * n)),
        input_output_aliases={i: i for i in range(2 * n)},
        compiler_params=_cp(has_side_effects=pltpu.SideEffectType.DATAFLOW_SIDE_EFFECTING),
    )(*both, send_sems, recv_sems, after)
    return list(out[:n]), list(out[n:])


BIG = ["w_in", "w_uq", "w_ukv", "w_branch_ssm", "w_branch_mla", "w_out", "w_mlp_up", "w_mlp_down"]
COL_SHARDED = {"w_in", "w_uq", "w_ukv", "w_mlp_up"}
SMALL_REPL = ["norm_mix_w", "conv_b", "dt_bias", "a_log", "d_skip", "ssm_norm_w", "q_norm_w", "kv_norm_w", "norm_mlp_w"]


def _unshard_layer(name, g):
    _, r, c = g.shape
    if name in COL_SHARDED:
        return jnp.transpose(g, (1, 0, 2)).reshape(r, 4 * c)
    return g.reshape(4 * r, c)


def _to_shards(name, full):
    r, c = full.shape
    if name in COL_SHARDED:
        return jnp.transpose(full.reshape(r, 4, c // 4), (1, 0, 2))
    return full.reshape(4, r // 4, c)


REST = [k for k in BIG if k != "w_in"]


def prep_layer(cfg, w):
    out = {}
    if "w_in" in w:
        sp = np.cumsum(cfg.in_splits)[:-1].tolist()
        z, xbc, dt, cq, ckv, kr, gs, gm = jnp.split(w["w_in"], sp, axis=1)
        zpad = lambda n: jnp.zeros((cfg.d, n), z.dtype)
        out.update(w_z=z, w_xbc=xbc, w_g=jnp.concatenate([gs, gm], axis=1),
                   w_s=jnp.concatenate([cq, ckv, kr, zpad(LANE - cfg.rope), dt, zpad(LANE - cfg.heads)], axis=1))
    if "w_uq" in w:
        out.update(
            w_uq=jnp.pad(w["w_uq"].reshape(cfg.ql, cfg.mh, cfg.nope + cfg.rope),
                         ((0, 0), (0, 0), (0, 2 * LANE - cfg.nope - cfg.rope))).reshape(cfg.ql, cfg.qw),
            w_ukv=w["w_ukv"], w_bs=w["w_branch_ssm"], w_bm=w["w_branch_mla"], w_out=w["w_out"],
            w_up=w["w_mlp_up"], w_down=w["w_mlp_down"])
    return {k: v.astype(BF16) for k, v in out.items()}


def unprep_grads(cfg, g):
    out = {}
    if "w_s" in g:
        ql, kvl = cfg.ql, cfg.kvl
        ds_ = g["w_s"]
        cq, ckv = ds_[:, :ql], ds_[:, ql:ql + kvl]
        kr = ds_[:, ql + kvl:ql + kvl + cfg.rope]
        dt = ds_[:, ql + kvl + LANE:ql + kvl + LANE + cfg.heads]
        out["w_in"] = jnp.concatenate([g["w_z"], g["w_xbc"], dt, cq, ckv, kr, g["w_g"]], axis=1)
    if "w_uq" in g:
        out.update(
            w_uq=g["w_uq"].reshape(cfg.ql, cfg.mh, 2 * LANE)[:, :, :cfg.nope + cfg.rope].reshape(cfg.ql, -1),
            w_ukv=g["w_ukv"], w_branch_ssm=g["w_bs"], w_branch_mla=g["w_bm"],
            w_out=g["w_out"], w_mlp_up=g["w_up"], w_mlp_down=g["w_down"])
    return out


def _hook(hooks, name, arg):
    if hooks and name in hooks:
        return hooks[name](arg)[0, 0]
    return 0.0


def layer_fwd(cfg, h, pw, sm, tabs, li, hooks=None):
    n = lambda s: f"l{li}_{s}"
    u = rmsnorm_fwd(h, sm["norm_mix_w"], name=n("norm_mix"))
    z, xbc, g, small = matmul_multi(u, [pw["w_z"], pw["w_xbc"], pw["w_g"], pw["w_s"]], (BF16, F32, BF16, F32),
                                    name=n("in_proj"))
    xc, dsilu = conv_fwd(cfg, xbc, sm["conv_w"], sm["conv_b"], name=n("conv"))
    dt_bias = sm["dt_bias_p"] + _hook(hooks, "after_conv", xc)
    y, sin = ssd_fwd(cfg, xc, small, dt_bias, sm["avec"], sm["dexp"], name=n("ssd"))
    y_ssm = tail_fwd(cfg, y, z, sm["ssm_norm_w"], name=n("tail"))
    if hooks and "weights" in hooks:
        pw = dict(pw, **hooks["weights"](y_ssm))
    cqn = rmsnorm_fwd(small, sm["q_norm_w"], cw=cfg.ql, ci=0, name=n("q_norm"))
    ckvn = rmsnorm_fwd(small, sm["kv_norm_w"], cw=cfg.kvl, ci=cfg.ql // cfg.kvl, name=n("kv_norm"))
    qf = matmul(cqn, pw["w_uq"], out_dtype=BF16, name=n("uq"))
    kv = matmul(ckvn, pw["w_ukv"], out_dtype=BF16, name=n("ukv"))
    qr, kpe = rope_fwd(cfg, qf, small, tabs, name=n("rope"))
    o, lse = attn_fwd(cfg, qr, kv, kpe, name=n("attn"))
    ya = matmul(y_ssm, pw["w_bs"], out_dtype=BF16, name=n("branch_ssm"))
    yb = matmul(o, pw["w_bm"], out_dtype=BF16, name=n("branch_mla"))
    mixed = gate_fwd(cfg, ya, yb, g, name=n("gate"))
    h1 = matmul(mixed, pw["w_out"], add=h, name=n("out"))
    v = rmsnorm_fwd(h1, sm["norm_mlp_w"] + _hook(hooks, "after_attn", o), name=n("norm_mlp"))
    a, act = matmul(v, pw["w_up"], name=n("up"), epilogue=_ep_relu2, out_dtypes=(BF16, BF16))
    h2 = matmul(act, pw["w_down"], add=h1, name=n("down"))
    saved = dict(h=h, u=u, z=z, xbc=xbc, g=g, small=small, xc=xc, dsilu=dsilu, y=y, sin=sin, y_ssm=y_ssm, cqn=cqn, ckvn=ckvn,
                 qr=qr, kv=kv, kpe=kpe, o=o, lse=lse, ya=ya, yb=yb, mixed=mixed, h1=h1, v=v, a=a, act=act)
    return h2, saved, pw


def layer_bwd(cfg, dh2, pw, sm, tabs, s, li, hooks=None):
    n = lambda t: f"l{li}_b_{t}"
    gw, gs = {}, {}
    wgrad = functools.partial(matmul, ta=True, out_dtype=BF16)
    dh2, dh2b = dh2
    gw["w_down"] = wgrad(s["act"], dh2b, name=n("dw_down"))
    da = matmul(dh2b, pw["w_down"], tb=True, name=n("dact"), epilogue=_ep_relu2_grad, extras=(s["a"],),
                out_dtypes=(BF16,))
    gw["w_up"] = wgrad(s["v"], da, col_slabs=4, name=n("dw_up"))
    dv = matmul(da, pw["w_up"], tb=True, out_dtype=BF16, name=n("dv"))
    dh1, gs["norm_mlp_w"], dh1b = rmsnorm_bwd(dv, s["h1"], sm["norm_mlp_w"], res=dh2, with_bf16=True,
                                              name=n("norm_mlp"))
    gw["w_out"] = wgrad(s["mixed"], dh1b, name=n("dw_out"))
    dmix = matmul(dh1b, pw["w_out"], tb=True, out_dtype=BF16, name=n("dmix"))
    dya, dyb, dg = gate_bwd(cfg, dmix, s["ya"], s["yb"], s["g"], name=n("gate"))
    gw["w_bs"] = wgrad(s["y_ssm"], dya, name=n("dw_bs"))
    gw["w_bm"] = wgrad(s["o"], dyb, name=n("dw_bm"))
    dy_ssm = matmul(dya, pw["w_bs"], tb=True, out_dtype=BF16, name=n("dy_ssm"))
    do = matmul(dyb, pw["w_bm"], tb=True, out_dtype=BF16, name=n("do"))
    dq, dkv, dkpe = attn_bwd(cfg, s["qr"], s["kv"], s["kpe"], s["o"], s["lse"], do, name=n("attn"))
    dqf, dkr = rope_bwd(cfg, dq, dkpe, tabs, name=n("rope"))
    gw["w_uq"] = wgrad(s["cqn"], dqf, name=n("dw_uq"))
    gw["w_ukv"] = wgrad(s["ckvn"], dkv, col_slabs=4, name=n("dw_ukv"))
    dcqn = matmul(dqf, pw["w_uq"], tb=True, name=n("dcqn"))
    dckvn = matmul(dkv, pw["w_ukv"], tb=True, name=n("dckvn"))
    q_norm_w = sm["q_norm_w"] + _hook(hooks, "after_attn", dqf)
    dcq, gs["q_norm_w"] = rmsnorm_bwd(dcqn, s["small"], q_norm_w, cw=cfg.ql, ci=0, out_dtype=BF16, name=n("q_norm"))
    dckv, gs["kv_norm_w"] = rmsnorm_bwd(dckvn, s["small"], sm["kv_norm_w"], cw=cfg.kvl, ci=cfg.ql // cfg.kvl,
                                        out_dtype=BF16, name=n("kv_norm"))
    ssm_norm_w = sm["ssm_norm_w"] + _hook(hooks, "early", dict(gw))
    dy, dz, gs["ssm_norm_w"] = tail_bwd(cfg, dy_ssm, s["y"], s["z"], ssm_norm_w, name=n("tail"))
    dxc, ddt, ddexp, dav, dbias = ssd_bwd(cfg, s["xc"], s["small"], sm["dt_bias_p"], sm["avec"], sm["dexp"],
                                          s["sin"], dy, name=n("ssd"))
    conv_w = sm["conv_w"] + _hook(hooks, "after_ssd", dxc)
    dxbc, gs["conv_w"], gs["conv_b"] = conv_bwd(cfg, s["xbc"], conv_w, s["dsilu"], dxc, name=n("conv"))
    gs["d_skip"] = ddexp.reshape(cfg.heads, cfg.hd).sum(axis=1)
    gs["a_log"] = (dav[0] * sm["avec"][0])[:cfg.heads]
    gs["dt_bias"] = dbias[0, :cfg.heads]
    dsmall = jnp.concatenate([dcq, dckv, dkr.astype(BF16), ddt.astype(BF16)], axis=1)
    gw["w_z"] = wgrad(s["u"], dz, name=n("dw_z"))
    gw["w_xbc"] = wgrad(s["u"], dxbc, name=n("dw_xbc"))
    gw["w_g"] = wgrad(s["u"], dg, name=n("dw_g"))
    gw["w_s"] = wgrad(s["u"], dsmall, name=n("dw_s"))
    du = matmul_nt_sum([dz, dxbc, dg, dsmall], [pw["w_z"], pw["w_xbc"], pw["w_g"], pw["w_s"]], out_dtype=BF16,
                       name=n("du"))
    if li > 0:
        dh, gs["norm_mix_w"], dhb = rmsnorm_bwd(du, s["h"], sm["norm_mix_w"], res=dh1, with_bf16=True,
                                                name=n("norm_mix"))
    else:
        dh, gs["norm_mix_w"] = rmsnorm_bwd(du, s["h"], sm["norm_mix_w"], res=dh1, name=n("norm_mix"))
        dhb = None
    return (dh, dhb), gw, gs


def small_params(cfg, p, li):
    pad_l = lambda v: jnp.pad(v, (0, LANE - v.shape[0])).reshape(1, LANE)
    return dict(
        norm_mix_w=p["norm_mix_w"][li], conv_w=p["conv_w"][li], conv_b=p["conv_b"][li],
        dt_bias_p=pad_l(p["dt_bias"][li]), avec=pad_l(-jnp.exp(p["a_log"][li])),
        dexp=jnp.repeat(p["d_skip"][li], cfg.hd).reshape(1, cfg.inner),
        ssm_norm_w=p["ssm_norm_w"][li], q_norm_w=p["q_norm_w"][li], kv_norm_w=p["kv_norm_w"][li],
        norm_mlp_w=p["norm_mlp_w"][li])


def local_step(cfg, x, target, p, depth=2):
    bsz, d = cfg.bsz, cfg.d
    lead = jnp.zeros((bsz, cfg.pad, d), F32)
    meta = jnp.broadcast_to(p["meta_tokens"][None], (bsz, cfg.n_meta, d))
    h = jnp.concatenate([lead, meta, x], axis=1).reshape(cfg.t, d)
    tabs = rope_tables(cfg)
    saved, sms = [], []
    for li in range(depth):
        sm = small_params(cfg, p, li)
        h, s, _ = layer_fwd(cfg, h, p["pw"][li], sm, tabs, li)
        saved.append(s)
        sms.append(sm)
    loss, dh, dfw = loss_head(cfg, h, target.reshape(bsz * cfg.seq, d), p["final_norm_w"], name="loss_head")
    gws, gss = [None] * depth, [None] * depth
    for li in reversed(range(depth)):
        dh, gws[li], gss[li] = layer_bwd(cfg, dh, p["pw"][li], sms[li], tabs, saved[li], li)
    dh = dh[0].reshape(bsz, cfg.lp, d)
    grad_x = dh[:, cfg.chunk:, :]
    gmeta = jnp.sum(dh[:, cfg.pad:cfg.chunk, :], axis=0)
    return loss, grad_x, gmeta, gws, gss, dfw


def _pack_small(parts):
    flat = jnp.concatenate([a.reshape(-1) for a in parts])
    n = flat.shape[0]
    npad = -n % (8 * LANE)
    return jnp.pad(flat, (0, npad)).reshape(-1, LANE), n


def _unpack_small(vec, shapes):
    flat = vec.reshape(-1)
    out, off = [], 0
    for sh in shapes:
        sz = int(np.prod(sh))
        out.append(flat[off:off + sz].reshape(sh))
        off += sz
    return out


def _as2d(a):
    return a.reshape(-1, a.shape[-1])


def kernel(x, meta_tokens, norm_mix_w, w_in, conv_w, conv_b, dt_bias, a_log, d_skip, ssm_norm_w, q_norm_w, kv_norm_w, w_uq, w_ukv, w_branch_ssm, w_branch_mla, w_out, norm_mlp_w, w_mlp_up, w_mlp_down, final_norm_w, loss_target, m_meta_tokens, m_norm_mix_w, m_w_in, m_conv_w, m_conv_b, m_dt_bias, m_a_log, m_d_skip, m_ssm_norm_w, m_q_norm_w, m_kv_norm_w, m_w_uq, m_w_ukv, m_w_branch_ssm, m_w_branch_mla, m_w_out, m_norm_mlp_w, m_w_mlp_up, m_w_mlp_down, m_final_norm_w, v_meta_tokens, v_norm_mix_w, v_w_in, v_conv_w, v_conv_b, v_dt_bias, v_a_log, v_d_skip, v_ssm_norm_w, v_q_norm_w, v_kv_norm_w, v_w_uq, v_w_ukv, v_w_branch_ssm, v_w_branch_mla, v_w_out, v_norm_mlp_w, v_w_mlp_up, v_w_mlp_down, v_final_norm_w):
    cfg = CFG
    names = ["meta_tokens", "norm_mix_w", "w_in", "conv_w", "conv_b", "dt_bias", "a_log", "d_skip", "ssm_norm_w",
             "q_norm_w", "kv_norm_w", "w_uq", "w_ukv", "w_branch_ssm", "w_branch_mla", "w_out", "norm_mlp_w",
             "w_mlp_up", "w_mlp_down", "final_norm_w"]
    wts = dict(zip(names, [meta_tokens, norm_mix_w, w_in, conv_w, conv_b, dt_bias, a_log, d_skip, ssm_norm_w,
                           q_norm_w, kv_norm_w, w_uq, w_ukv, w_branch_ssm, w_branch_mla, w_out, norm_mlp_w,
                           w_mlp_up, w_mlp_down, final_norm_w]))
    ms = dict(zip(names, [m_meta_tokens, m_norm_mix_w, m_w_in, m_conv_w, m_conv_b, m_dt_bias, m_a_log, m_d_skip,
                          m_ssm_norm_w, m_q_norm_w, m_kv_norm_w, m_w_uq, m_w_ukv, m_w_branch_ssm, m_w_branch_mla,
                          m_w_out, m_norm_mlp_w, m_w_mlp_up, m_w_mlp_down, m_final_norm_w]))
    vs = dict(zip(names, [v_meta_tokens, v_norm_mix_w, v_w_in, v_conv_w, v_conv_b, v_dt_bias, v_a_log, v_d_skip,
                          v_ssm_norm_w, v_q_norm_w, v_kv_norm_w, v_w_uq, v_w_ukv, v_w_branch_ssm, v_w_branch_mla,
                          v_w_out, v_norm_mlp_w, v_w_mlp_up, v_w_mlp_down, v_final_norm_w]))
    cx, cy, cc = _coords()
    chip = 2 * cx + cy

    half1 = jnp.reshape(cc, (1,)).astype(jnp.int32)
    where2 = jnp.stack([chip, cc]).astype(jnp.int32)
    wb = {k: wts[k].astype(BF16) for k in BIG}
    zero_tok = jnp.zeros((8, LANE), F32)

    def halves(a):
        return a.reshape((2, a.shape[0] // 2) + a.shape[1:])

    def gather_start(li, keys, tag, after):
        srcs = [halves(wb[k][li]) for k in keys]
        lands = [lax.empty((4,) + s.shape, BF16) for s in srcs]
        return ici_start("gather", srcs, lands, after, name=f"gather{li}{tag}_start")

    def gather_finish(li, keys, tag, started, after):
        srcs, lands = ici_wait("gather", started, after, name=f"gather{li}{tag}_wait")
        lands = pair_share(lands, srcs, name=f"gather{li}{tag}_share")
        full = {k: _unshard_layer(k, land.reshape((4, 2 * land.shape[2], land.shape[3])))
                for k, land in zip(keys, lands)}
        return prep_layer(cfg, full)

    def gather_mid(li, keys, tag, started, after):
        srcs, lands = ici_wait("gather", started, after, name=f"gather{li}{tag}_wait")
        return ici_start("share", srcs, lands, zero_tok, name=f"gather{li}{tag}_share_start")

    def gather_end(li, keys, tag, shared, after):
        _, lands = ici_wait("share", shared, after, name=f"gather{li}{tag}_share_wait")
        full = {k: _unshard_layer(k, land.reshape((4, 2 * land.shape[2], land.shape[3])))
                for k, land in zip(keys, lands)}
        return prep_layer(cfg, full)

    def exchange_start(li, keys, tag, gw, after):
        ug = unprep_grads(cfg, gw)
        g4 = []
        for k in keys:
            s = ug[k] if ug[k].ndim == 3 else _to_shards(k, ug[k])
            g4.append(s.reshape(4, 2, s.shape[1] // 2, s.shape[2]))
        lands = [lax.empty((4,) + a.shape[2:], a.dtype) for a in g4]
        return ici_start("exchange", g4, lands, after, name=f"grad{li}{tag}_exchange_start")

    def reduce_start(li, keys, tag, exchanged, after):
        g4, theirs = ici_wait("exchange", exchanged, after, name=f"grad{li}{tag}_exchange_wait")
        parts = [pair_add(a, b, half1, name=f"grad{li}_pair_add_{k}") for k, a, b in zip(keys, g4, theirs)]
        lands = [lax.empty(q.shape, q.dtype) for q in parts]
        return ici_start("scatter", parts, lands, zero_tok, name=f"grad{li}{tag}_scatter_start")

    def reduce_mid(li, keys, tag, started, after):
        parts, lands = ici_wait("scatter", started, after, name=f"grad{li}{tag}_scatter_wait")
        sums = [chip_sum(rc, pt, where2, name=f"grad{li}_chip_sum_{k}") for k, rc, pt in zip(keys, lands, parts)]
        return ici_start("fill", [zero_tok] * len(sums), sums, zero_tok, name=f"grad{li}{tag}_fill_start")

    def reduce_end(li, keys, tag, filled, after):
        _, sums = ici_wait("fill", filled, after, name=f"grad{li}{tag}_fill_wait")
        return {k: s.reshape(2 * s.shape[1], s.shape[2]) for k, s in zip(keys, sums)}

    gathered = gather_chips([meta_tokens, conv_w], name="gather_small")
    p = dict(wts)
    p["meta_tokens"] = jnp.transpose(gathered[0], (1, 0, 2)).reshape(cfg.n_meta, cfg.d)
    p["conv_w"] = jnp.transpose(gathered[1], (1, 2, 0, 3)).reshape(2, cfg.convk, cfg.conv_dim)

    st0a = gather_start(0, ["w_in"], "a", gathered[0])
    st0b = gather_start(0, REST, "b", st0a[4])
    st1 = gather_start(1, BIG, "", st0b[4])
    pw0 = gather_finish(0, ["w_in"], "a", st0a, st1[4])

    bsz, d = cfg.bsz, cfg.d
    lead = jnp.zeros((bsz, cfg.pad, d), F32)
    meta = jnp.broadcast_to(p["meta_tokens"][None], (bsz, cfg.n_meta, d))
    h0 = jnp.concatenate([lead, meta, x], axis=1).reshape(cfg.t, d)
    tabs = rope_tables(cfg)
    sm0 = small_params(cfg, p, 0)
    st = {}

    def step(key, fn):
        def run(arg):
            st[key] = fn(arg)
            return st[key][4]
        return run

    h1, sv0, pw0 = layer_fwd(cfg, h0, pw0, sm0, tabs, 0, hooks={
        "after_conv": step("share0b", lambda after: gather_mid(0, REST, "b", st0b, after)),
        "weights": lambda after: gather_end(0, REST, "b", st["share0b"], after),
        "after_attn": step("share1", lambda after: gather_mid(1, BIG, "", st1, after))})
    pw1 = gather_end(1, BIG, "", st["share1"], h1)
    sm1 = small_params(cfg, p, 1)
    h2, sv1, _ = layer_fwd(cfg, h1, pw1, sm1, tabs, 1)
    loss, dh, dfw = loss_head(cfg, h2, loss_target.reshape(bsz * cfg.seq, d), final_norm_w, name="loss_head")

    dh, gw1, gs1 = layer_bwd(cfg, dh, pw1, sm1, tabs, sv1, 1)
    ex1 = exchange_start(1, BIG, "", gw1, zero_tok)
    sm0b = dict(sm0)
    sm0b["norm_mlp_w"] = sm0["norm_mlp_w"] + ex1[4][0, 0]
    dh, gw0, gs0 = layer_bwd(cfg, dh, pw0, sm0b, tabs, sv0, 0, hooks={
        "after_attn": step("red1", lambda after: reduce_start(1, BIG, "", ex1, after)),
        "early": step("ex0e", lambda gw: exchange_start(0, REST, "e", gw, zero_tok)),
        "after_ssd": step("red0e", lambda after: reduce_start(0, REST, "e", st["ex0e"], after))})
    dh3 = dh[0].reshape(bsz, cfg.lp, d)
    grad_x = dh3[:, cfg.chunk:, :]
    gmeta = jnp.sum(dh3[:, cfg.pad:cfg.chunk, :], axis=0)
    fill1 = reduce_mid(1, BIG, "", st["red1"], dh[0])
    ex0l = exchange_start(0, ["w_in"], "l", gw0, fill1[4])

    small_names = SMALL_REPL + ["conv_w"]
    parts = [jnp.stack([gs0[k], gs1[k]]) for k in small_names] + [dfw, gmeta, loss.reshape(1)]
    shapes = [a.shape for a in parts]
    vec, _ = _pack_small(parts)
    red_vec = allreduce_small(vec, ex0l[4], name="allreduce_small")
    red = _unpack_small(red_vec, shapes)
    sg = dict(zip(small_names + ["final_norm_w", "meta_tokens"], red))
    loss = red[-1].reshape(())
    sg["conv_w"] = lax.dynamic_slice_in_dim(sg["conv_w"], chip * (cfg.conv_dim // 4), cfg.conv_dim // 4, axis=2)
    sg["meta_tokens"] = lax.dynamic_slice_in_dim(sg["meta_tokens"], chip * (cfg.d // 4), cfg.d // 4, axis=1)

    red0 = reduce_start(0, ["w_in"], "l", ex0l, red_vec)
    grads, deltas, new_m, new_v = {}, {}, {}, {}
    dep = red0[4]
    for k in names:
        if k in BIG:
            continue
        w2, g2, m2, v2 = _as2d(wts[k]), _as2d(sg[k]), _as2d(ms[k]), _as2d(vs[k])
        dl, mn, vn = adamw_small(w2, g2, m2, v2, dep, name=f"adamw_{k}")
        grads[k] = sg[k].reshape(wts[k].shape)
        deltas[k], new_m[k], new_v[k] = (t.reshape(wts[k].shape) for t in (dl, mn, vn))

    def view(k, a):
        return jnp.swapaxes(a, 1, 2) if k == "w_in" else a

    def gview(k, g):
        return g.T if k == "w_in" else g

    wv, mv, vv = ({k: view(k, t[k]) for k in BIG} for t in (wts, ms, vs))
    outs = {}
    big1 = reduce_end(1, BIG, "", fill1, dl)
    fill0e = reduce_mid(0, REST, "e", st["red0e"], big1[BIG[-1]])
    dep = fill0e[4]
    for k in BIG:
        outs[k] = adamw_layer(wv[k], mv[k], vv[k], gview(k, big1[k]), 1, None, dep, name=f"adamw1_{k}")
        dep = outs[k][1]
    big0 = reduce_end(0, REST, "e", fill0e, dep)
    fill0l = reduce_mid(0, ["w_in"], "l", red0, big0[REST[-1]])
    dep = fill0l[4]
    for k in REST:
        outs[k] = adamw_layer(wv[k], mv[k], vv[k], big0[k], 0, outs[k], dep, name=f"adamw0_{k}")
        dep = outs[k][1]
    big0.update(reduce_end(0, ["w_in"], "l", fill0l, dep))
    outs["w_in"] = adamw_layer(wv["w_in"], mv["w_in"], vv["w_in"], gview("w_in", big0["w_in"]), 0, outs["w_in"], dep,
                               name="adamw0_w_in")
    for k in BIG:
        grads[k], deltas[k], new_m[k], new_v[k] = (view(k, t) for t in outs[k])
    return (loss, grad_x, *[grads[k] for k in names], *[deltas[k] for k in names],
            *[new_m[k] for k in names], *[new_v[k] for k in names])


def adamw_small(w, g, m, v, dep, *, name):
    def body(w_ref, g_ref, m_ref, v_ref, dep_ref, d_ref, mo_ref, vo_ref):
        d_ref[...], mo_ref[...], vo_ref[...] = _adam_update(w_ref[...], g_ref[...], m_ref[...], v_ref[...])

    vm = pl.BlockSpec(memory_space=pltpu.VMEM)
    return pl.pallas_call(body, name=name, in_specs=[vm] * 4 + [pl.BlockSpec(memory_space=pl.ANY)], out_specs=[vm] * 3,
                          out_shape=[_sds(w.shape, F32)] * 3, compiler_params=_cp())(w, g, m, v, dep)
```

```python
import functools
from typing import NamedTuple

import numpy as np
import jax
import jax.numpy as jnp
from jax import lax
from jax.experimental import pallas as pl
from jax.experimental.pallas import tpu as pltpu

F32 = jnp.float32
BF16 = jnp.bfloat16
EPS = 1e-6
ROPE_THETA = 10000.0
LANE = 128
VMEM_LIMIT = 56 * 1024 * 1024
MASK_VALUE = -1e30
ADAM_LR, ADAM_B1, ADAM_B2, ADAM_EPS, ADAM_WD, ADAM_STEP = 0.001, 0.9, 0.999, 1e-08, 0.01, 10
MESH = pl.DeviceIdType.MESH


class Cfg(NamedTuple):
    d: int = 1024
    seq: int = 2048
    bsz: int = 2
    n_meta: int = 16
    inner: int = 2048
    hd: int = 64
    groups: int = 4
    state: int = 128
    convk: int = 4
    chunk: int = 128
    mh: int = 8
    ql: int = 512
    kvl: int = 256
    nope: int = 128
    rope: int = 64
    vd: int = 128
    ff: int = 4096

    @property
    def heads(self): return self.inner // self.hd
    @property
    def gw(self): return self.inner // self.groups
    @property
    def conv_dim(self): return self.inner + 2 * self.groups * self.state
    @property
    def pad(self): return self.chunk - self.n_meta
    @property
    def lp(self): return self.chunk + self.seq
    @property
    def t(self): return self.bsz * self.lp
    @property
    def nchunks(self): return self.lp // self.chunk
    @property
    def sw(self): return self.ql + self.kvl + 2 * LANE
    @property
    def kt(self): return (self.ql + self.kvl) // LANE
    @property
    def dtt(self): return self.kt + 1
    @property
    def qw(self): return self.mh * 2 * LANE
    @property
    def in_splits(self):
        return [self.inner, self.conv_dim, self.heads, self.ql, self.kvl, self.rope, self.d, self.d]


CFG = Cfg()


def _pick(dim, pref, mult):
    best = None
    for t in range(mult, min(dim, pref) + 1, mult):
        if dim % t == 0:
            best = t
    return best if best is not None else dim


def _cp(**kw):
    return pltpu.CompilerParams(vmem_limit_bytes=VMEM_LIMIT, **kw)


def _sds(shape, dtype):
    return jax.ShapeDtypeStruct(tuple(shape), dtype)


def _silu(x):
    return x * jax.nn.sigmoid(x)


def _dsilu(x):
    s = jax.nn.sigmoid(x)
    return s * (1.0 + x * (1.0 - s))


def _ep_plain(r):
    return (r,)


def _ep_add(r, res):
    return (r + res.astype(F32),)


def _ep_relu2(r):
    rp = jnp.maximum(r, 0.0)
    return r, rp * rp


def _ep_relu2_grad(r, a):
    return (r * (2.0 * jnp.maximum(a.astype(F32), 0.0)),)


MM_VMEM_BUDGET = 44 * 1024 * 1024


def _mm_tiles(m, n, k, a_bytes, b_bytes, io_bytes, ta):
    m_mult, m_cap = (LANE, 1024) if ta else (16, 1088)
    tms = [t for t in range(m_cap, 0, -m_mult) if m % t == 0] or [m]
    tns = [t for t in (1024, 512, 256, 128) if n % t == 0] or [n]
    best = None
    for tm in tms:
        for tn in tns:
            need = 2 * (tm * k * a_bytes + k * tn * b_bytes + tm * tn * io_bytes)
            if need <= MM_VMEM_BUDGET and (best is None or tm * tn > best[0] * best[1]):
                best = (tm, tn)
    if best is None:
        return (_pick(m, 512, m_mult), _pick(n, 512, LANE), _pick(k, 1088 if ta else 1024, 16 if ta else LANE))
    return best[0], best[1], k


def _resident_rows(m, n, k, a_bytes, b_bytes, io_bytes):
    w = n * k * b_bytes
    if w > 18 * 1024 * 1024:
        return None
    for tm in range(544, 255, -16):
        if m % tm == 0 and w + 2 * tm * (k * a_bytes + n * io_bytes) + tm * n * 4 <= MM_VMEM_BUDGET - (4 << 20):
            return tm
    return None


def matmul(a, b, *, ta=False, tb=False, out_dtype=F32, add=None, name, tm=None, tn=None, tk=None,
           epilogue=None, extras=(), out_dtypes=None):
    if add is not None:
        epilogue, extras = _ep_add, (add,)
    if epilogue is None:
        epilogue = _ep_plain
    out_dtypes = tuple(out_dtypes) if out_dtypes is not None else (out_dtype,)
    n_ex, n_out = len(extras), len(out_dtypes)
    if ta:
        k_dim, m_dim = a.shape
    else:
        m_dim, k_dim = a.shape
    if tb:
        n_dim, k2 = b.shape
    else:
        k2, n_dim = b.shape
    assert k_dim == k2, (a.shape, b.shape, ta, tb)
    resident = False
    if tm is None and tn is None and tk is None:
        io_bytes = sum(jnp.dtype(e.dtype).itemsize for e in extras) + sum(jnp.dtype(d).itemsize for d in out_dtypes)
        a_bytes, b_bytes = jnp.dtype(a.dtype).itemsize, jnp.dtype(b.dtype).itemsize
        tm = None if ta else _resident_rows(m_dim, n_dim, k_dim, a_bytes, b_bytes, io_bytes)
        if tm is not None:
            resident, tn, tk = True, n_dim, k_dim
        else:
            tm, tn, tk = _mm_tiles(m_dim, n_dim, k_dim, a_bytes, b_bytes, io_bytes, ta)
    elif ta:
        tm = tm or _pick(m_dim, 1024, LANE)
        tk = tk or _pick(k_dim, 1088, 16)
        tn = tn or _pick(n_dim, 1024, LANE)
    else:
        tm = tm or _pick(m_dim, 1088, 16)
        tk = tk or _pick(k_dim, 1024 if a.dtype == F32 else 2048, LANE)
        tn = tn or _pick(n_dim, 1024, LANE)
    nm, nn, nk = m_dim // tm, n_dim // tn, k_dim // tk
    dn = (((0 if ta else 1,), (1 if tb else 0,)), ((), ()))

    def body(*refs):
        a_ref, b_ref = refs[:2]
        ex_refs = refs[2:2 + n_ex]
        o_refs = refs[2 + n_ex:2 + n_ex + n_out]
        scr = refs[2 + n_ex + n_out:]
        p = lax.dot_general(a_ref[...].astype(BF16), b_ref[...].astype(BF16), dn, preferred_element_type=F32)

        def finish(r):
            outs = epilogue(r, *[e[...] for e in ex_refs])
            for o_ref, val, dt in zip(o_refs, outs, out_dtypes):
                o_ref[...] = val.astype(dt)

        if nk == 1:
            finish(p)
        else:
            acc = scr[0]
            k = pl.program_id(2)

            @pl.when(k == 0)
            def _():
                acc[...] = p

            @pl.when(k > 0)
            def _():
                acc[...] += p

            @pl.when(k == nk - 1)
            def _():
                finish(acc[...])

    if resident:
        row = pl.BlockSpec((tm, k_dim), lambda i: (i, 0))
        o_spec = pl.BlockSpec((tm, n_dim), lambda i: (i, 0))
        outs = pl.pallas_call(
            body, name=name, grid=(nm,),
            in_specs=[row, pl.BlockSpec(b.shape, lambda i: (0, 0), pipeline_mode=pl.Buffered(1))] + [o_spec] * n_ex,
            out_specs=[o_spec] * n_out, out_shape=[_sds((m_dim, n_dim), dt) for dt in out_dtypes],
            compiler_params=_cp(dimension_semantics=("parallel",)),
        )(a, b, *extras)
        return outs[0] if n_out == 1 else tuple(outs)
    a_spec = pl.BlockSpec((tk, tm), lambda i, j, k: (k, i)) if ta else pl.BlockSpec((tm, tk), lambda i, j, k: (i, k))
    b_spec = pl.BlockSpec((tn, tk), lambda i, j, k: (j, k)) if tb else pl.BlockSpec((tk, tn), lambda i, j, k: (k, j))
    o_spec = pl.BlockSpec((tm, tn), lambda i, j, k: (i, j))
    outs = pl.pallas_call(
        body, name=name, grid=(nm, nn, nk), in_specs=[a_spec, b_spec] + [o_spec] * n_ex, out_specs=[o_spec] * n_out,
        out_shape=[_sds((m_dim, n_dim), dt) for dt in out_dtypes],
        scratch_shapes=[pltpu.VMEM((tm, tn), F32)] if nk > 1 else [],
        compiler_params=_cp(dimension_semantics=("parallel", "parallel", "arbitrary")),
    )(a, b, *extras)
    return outs[0] if n_out == 1 else tuple(outs)


def matmul_multi(a, bs_, out_dtypes, *, name):
    m, k = a.shape
    ns = [b.shape[1] for b in bs_]
    cnt = len(bs_)
    out_row_bytes = sum(n * jnp.dtype(dt).itemsize for n, dt in zip(ns, out_dtypes))
    w_bytes = sum(k * n * jnp.dtype(b.dtype).itemsize for n, b in zip(ns, bs_))
    tm = next(t for t in range(1088, 0, -16)
              if m % t == 0 and w_bytes + 2 * t * (k * jnp.dtype(a.dtype).itemsize + out_row_bytes)
              + t * max(ns) * 4 <= MM_VMEM_BUDGET - (8 << 20))

    def body(*refs):
        a_ref = refs[0]
        b_refs, o_refs = refs[1:1 + cnt], refs[1 + cnt:]
        av = a_ref[...].astype(BF16)
        for b_ref, o_ref, dt in zip(b_refs, o_refs, out_dtypes):
            o_ref[...] = _nn(av, b_ref[...].astype(BF16)).astype(dt)

    return pl.pallas_call(
        body, name=name, grid=(m // tm,),
        in_specs=[pl.BlockSpec((tm, k), lambda i: (i, 0))]
        + [pl.BlockSpec((k, n), lambda i: (0, 0), pipeline_mode=pl.Buffered(1)) for n in ns],
        out_specs=[pl.BlockSpec((tm, n), lambda i: (i, 0)) for n in ns],
        out_shape=[_sds((m, n), dt) for n, dt in zip(ns, out_dtypes)], compiler_params=_cp(),
    )(a, *bs_)


def matmul_nt_sum(as_, bs_, *, out_dtype=F32, name, tiles=None):
    m, n = as_[0].shape[0], bs_[0].shape[0]
    ks = [a.shape[1] for a in as_]
    assert [b.shape[1] for b in bs_] == ks
    ksum, cnt = sum(ks), len(ks)
    best = tiles
    for tn in [t for t in (1024, 512, 256, 128) if n % t == 0]:
        for tm in [t for t in range(1088, 0, -16) if m % t == 0]:
            need = 2 * (tm * ksum * 2 + tn * ksum * 2 + tm * tn * jnp.dtype(out_dtype).itemsize)
            if best is None and need <= MM_VMEM_BUDGET and tm >= 256:
                best = (tm, tn)
    tm, tn = best

    def body(*refs):
        a_refs, b_refs, o_ref = refs[:cnt], refs[cnt:2 * cnt], refs[2 * cnt]
        acc = None
        for a_ref, b_ref in zip(a_refs, b_refs):
            p = _nt(a_ref[...].astype(BF16), b_ref[...].astype(BF16))
            acc = p if acc is None else acc + p
        o_ref[...] = acc.astype(out_dtype)

    return pl.pallas_call(
        body, name=name, grid=(n // tn, m // tm),
        in_specs=[pl.BlockSpec((tm, k), lambda j, i: (i, 0)) for k in ks]
        + [pl.BlockSpec((tn, k), lambda j, i: (j, 0)) for k in ks],
        out_specs=pl.BlockSpec((tm, tn), lambda j, i: (i, j)), out_shape=_sds((m, n), out_dtype),
        compiler_params=_cp(dimension_semantics=("parallel", "parallel")),
    )(*as_, *bs_)


def rmsnorm_fwd(x, w, *, cw=None, ci=0, name):
    t = x.shape[0]
    cw = cw or x.shape[1]
    tr = _pick(t, 544, 16)

    def body(x_ref, w_ref, o_ref):
        xv = x_ref[...].astype(F32)
        r = lax.rsqrt(jnp.mean(xv * xv, axis=-1, keepdims=True) + EPS)
        o_ref[...] = (xv * r * w_ref[...]).astype(BF16)

    return pl.pallas_call(
        body, name=name, grid=(t // tr,),
        in_specs=[pl.BlockSpec((tr, cw), lambda i: (i, ci)), pl.BlockSpec((1, cw), lambda i: (0, 0))],
        out_specs=pl.BlockSpec((tr, cw), lambda i: (i, 0)),
        out_shape=_sds((t, cw), BF16), compiler_params=_cp(),
    )(x, w.reshape(1, cw))


def rmsnorm_bwd(dy, x, w, *, cw=None, ci=0, res=None, out_dtype=F32, with_bf16=False, name):
    t = x.shape[0]
    cw = cw or x.shape[1]
    tr = _pick(t, 544, 16)
    has_res = res is not None

    def body(*refs):
        dxb_ref = None
        if with_bf16:
            refs, dxb_ref = refs[:-1], refs[-1]
        if has_res:
            dy_ref, x_ref, w_ref, res_ref, dx_ref, dw_ref = refs
        else:
            dy_ref, x_ref, w_ref, dx_ref, dw_ref = refs
        xv = x_ref[...].astype(F32)
        dyv = dy_ref[...].astype(F32)
        r = lax.rsqrt(jnp.mean(xv * xv, axis=-1, keepdims=True) + EPS)
        xh = xv * r
        g = dyv * w_ref[...]
        dx = r * (g - xh * jnp.mean(g * xh, axis=-1, keepdims=True))
        if has_res:
            dx = dx + res_ref[...]
        dx_ref[...] = dx.astype(out_dtype)
        if with_bf16:
            dxb_ref[...] = dx.astype(BF16)

        @pl.when(pl.program_id(0) == 0)
        def _():
            dw_ref[...] = jnp.zeros_like(dw_ref)

        dw_ref[...] += jnp.sum(dyv * xh, axis=0, keepdims=True)

    row = pl.BlockSpec((tr, cw), lambda i: (i, 0))
    in_specs = [row, pl.BlockSpec((tr, cw), lambda i: (i, ci)), pl.BlockSpec((1, cw), lambda i: (0, 0))]
    args = [dy, x, w.reshape(1, cw)]
    if has_res:
        in_specs.append(row)
        args.append(res)
    outs = pl.pallas_call(
        body, name=name, grid=(t // tr,), in_specs=in_specs,
        out_specs=[row, pl.BlockSpec((1, cw), lambda i: (0, 0))] + ([row] if with_bf16 else []),
        out_shape=[_sds((t, cw), out_dtype), _sds((1, cw), F32)] + ([_sds((t, cw), BF16)] if with_bf16 else []),
        compiler_params=_cp(),
    )(*args)
    if with_bf16:
        return outs[0], outs[1][0], outs[2]
    return outs[0], outs[1][0]


def _shift_down(x, s):
    return x if s == 0 else pltpu.roll(x, s, 0)


def _shift_up(x, s):
    return x if s == 0 else pltpu.roll(x, x.shape[0] - s, 0)


def _conv_pre(x, w_ref, b_ref, kk):
    pre = b_ref[...] + jnp.zeros_like(x)
    for k in range(kk):
        pre = pre + w_ref[k:k + 1, :] * _shift_down(x, kk - 1 - k)
    return pre


def conv_fwd(cfg, xbc, w, b, *, name):
    lp, cd, kk = cfg.lp, cfg.conv_dim, cfg.convk
    assert cfg.pad >= kk - 1
    cb = _pick(cd, 512, LANE)

    def body(x_ref, w_ref, b_ref, o_ref, ds_ref):
        pre = _conv_pre(x_ref[...], w_ref, b_ref, kk)
        sg = jax.nn.sigmoid(pre)
        o_ref[...] = pre * sg
        ds_ref[...] = (sg * (1.0 + pre * (1.0 - sg))).astype(BF16)

    blk = pl.BlockSpec((lp, cb), lambda j, bb: (bb, j))
    return pl.pallas_call(
        body, name=name, grid=(cd // cb, cfg.bsz),
        in_specs=[blk, pl.BlockSpec((kk, cb), lambda j, bb: (0, j)), pl.BlockSpec((1, cb), lambda j, bb: (0, j))],
        out_specs=[blk, blk], out_shape=[_sds((cfg.t, cd), F32), _sds((cfg.t, cd), BF16)], compiler_params=_cp(),
    )(xbc, w, b.reshape(1, cd))


def conv_bwd(cfg, xbc, w, dsilu, dxc, *, name):
    lp, cd, kk = cfg.lp, cfg.conv_dim, cfg.convk
    cb = _pick(cd, 512, LANE)

    def body(x_ref, w_ref, s_ref, d_ref, dx_ref, dw_ref, db_ref):
        x = x_ref[...]
        dpre = d_ref[...] * s_ref[...].astype(F32)
        dx = jnp.zeros_like(x)
        dws = []
        for k in range(kk):
            s = kk - 1 - k
            dx = dx + w_ref[k:k + 1, :] * _shift_up(dpre, s)
            dws.append(jnp.sum(dpre * _shift_down(x, s), axis=0, keepdims=True))
        dx_ref[...] = dx.astype(BF16)

        @pl.when(pl.program_id(1) == 0)
        def _():
            dw_ref[...] = jnp.zeros_like(dw_ref)
            db_ref[...] = jnp.zeros_like(db_ref)

        for k in range(kk):
            dw_ref[k:k + 1, :] += dws[k]
        db_ref[...] += jnp.sum(dpre, axis=0, keepdims=True)

    blk = pl.BlockSpec((lp, cb), lambda j, bb: (bb, j))
    wsp = pl.BlockSpec((kk, cb), lambda j, bb: (0, j))
    bsp = pl.BlockSpec((1, cb), lambda j, bb: (0, j))
    dx, dw, db = pl.pallas_call(
        body, name=name, grid=(cd // cb, cfg.bsz),
        in_specs=[blk, wsp, blk, blk], out_specs=[blk, wsp, bsp],
        out_shape=[_sds((cfg.t, cd), BF16), _sds((kk, cd), F32), _sds((1, cd), F32)], compiler_params=_cp(),
    )(xbc, w, dsilu, dxc)
    return dx, dw, db[0]


def _softplus(x):
    return jnp.maximum(x, 0.0) + jnp.log(1.0 + jnp.exp(-jnp.abs(x)))


def _ssd_consts(cfg):
    q = cfg.chunk
    i0 = np.arange(q)[:, None]
    i1 = np.arange(q)[None, :]
    ltri = (i1 <= i0).astype(np.float32)
    rexp = np.zeros((LANE, cfg.inner), np.float32)
    for h in range(cfg.heads):
        rexp[h, h * cfg.hd:(h + 1) * cfg.hd] = 1.0
    return jnp.asarray(ltri), jnp.asarray(rexp)


def _sel_dot(x, m, *, passes=2, left=False, trans=False):
    mb = m.astype(BF16)
    acc, rem = None, x
    for _ in range(passes):
        piece = rem.astype(BF16)
        if not left:
            part = _nn(piece, mb)
        elif trans:
            part = _tn(mb, piece)
        else:
            part = _nn(mb, piece)
        acc = part if acc is None else acc + part
        rem = rem - piece.astype(F32)
    return acc


def _expand_heads(cq, rexp_g):
    return dict(DT=_sel_dot(cq["dt"], rexp_g), E=_sel_dot(cq["e_in"], rexp_g), W0=_sel_dot(cq["w0"], rexp_g),
                DEC=_sel_dot(jnp.broadcast_to(cq["decay"], (8, LANE)), rexp_g)[0:1, :])


def _ssd_chunk_common(cfg, raw, bias, avec, c_idx, ltri, rexp):
    q = cfg.chunk
    rows = lax.broadcasted_iota(jnp.int32, (q, LANE), 0)
    live = jnp.logical_or(c_idx > 0, rows >= cfg.pad)
    pre = raw + bias
    dt = jnp.where(live, _softplus(pre), 0.0)
    adt = dt * avec
    cs = _sel_dot(adt, ltri, passes=3, left=True)
    cs_t = cs.T
    cs_last = cs[q - 1:q, :]
    e_in = jnp.exp(cs)
    w0 = jnp.exp(cs_last - cs)
    decay = jnp.exp(cs_last)
    cq = dict(live=live, pre=pre, dt=dt, adt=adt, cs=cs, cs_t=cs_t, e_in=e_in, w0=w0, decay=decay)
    if rexp is not None:
        cq.update(_expand_heads(cq, rexp))
    return cq


def _tri_masks(q):
    r = lax.broadcasted_iota(jnp.int32, (q, q), 0)
    c = lax.broadcasted_iota(jnp.int32, (q, q), 1)
    return c <= r, r <= c


def _head_l(cq, h, tri, tri_t):
    col = cq["cs"][:, h:h + 1]
    row = cq["cs_t"][h:h + 1, :]
    lmat = jnp.where(tri, jnp.exp(jnp.minimum(col - row, 0.0)), 0.0)
    lmat_t = jnp.where(tri_t, jnp.exp(jnp.minimum(row - col, 0.0)), 0.0)
    return lmat, lmat_t


def _nt(a, b):
    return lax.dot_general(a, b, (((1,), (1,)), ((), ())), preferred_element_type=F32)


def _tn(a, b):
    return lax.dot_general(a, b, (((0,), (0,)), ((), ())), preferred_element_type=F32)


def _nn(a, b):
    return jnp.dot(a, b, preferred_element_type=F32)


def ssd_fwd(cfg, xc, small, dt_bias, avec, dexp, *, name):
    q, inner, st, gw, g_n = cfg.chunk, cfg.inner, cfg.state, cfg.gw, cfg.groups
    nc = cfg.nchunks
    ltri, rexp = _ssd_consts(cfg)
    hpt = LANE // cfg.hd
    tiles_per_group = gw // LANE

    bsz, lp = cfg.bsz, cfg.lp
    bcw = g_n * st

    def body(x_ref, b_ref, c_ref, dt_ref, bias_ref, a_ref, d_ref, ltri_ref, rexp_ref, y_ref, sin_ref, s_scr):
        c_idx = pl.program_id(0)

        @pl.when(c_idx == 0)
        def _():
            s_scr[...] = jnp.zeros_like(s_scr)

        ltri_v = ltri_ref[...]
        tri, tri_t = _tri_masks(q)
        lane = lax.broadcasted_iota(jnp.int32, (q, LANE), 1)
        for bi in range(bsz):
            cq = _ssd_chunk_common(cfg, dt_ref[bi], bias_ref[...], a_ref[...], c_idx, ltri_v, rexp_ref[...])
            xs = x_ref[bi]
            xdt = (xs * cq["DT"]).astype(BF16)
            xw = (xs * cq["DT"] * cq["W0"]).astype(BF16)
            s_in = s_scr[bi]
            sin_ref[bi, 0] = s_in
            for g in range(g_n):
                bg = b_ref[bi, :, g * st:(g + 1) * st].astype(BF16)
                cg = c_ref[bi, :, g * st:(g + 1) * st].astype(BF16)
                gmat = _nt(cg, bg)
                gs = slice(g * gw, (g + 1) * gw)
                y0 = _nn(cg, s_in[:, gs].astype(BF16))
                for tt in range(tiles_per_group):
                    tile = g * tiles_per_group + tt
                    ts = slice(tile * LANE, (tile + 1) * LANE)
                    xt = xdt[:, ts]
                    ms, xh = [], []
                    for hh in range(hpt):
                        lmat, _ = _head_l(cq, tile * hpt + hh, tri, tri_t)
                        ms.append((gmat * lmat).astype(BF16))
                        inhead = jnp.logical_and(lane >= hh * cfg.hd, lane < (hh + 1) * cfg.hd)
                        xh.append(jnp.where(inhead, xt, jnp.zeros_like(xt)))
                    yd = _nn(jnp.concatenate(ms, axis=1), jnp.concatenate(xh, axis=0))
                    y_ref[bi, :, ts] = (yd + y0[:, tt * LANE:(tt + 1) * LANE] * cq["E"][:, ts]
                                        + xs[:, ts] * d_ref[:, ts]).astype(BF16)
                s_scr[bi, :, gs] = s_in[:, gs] * cq["DEC"][:, gs] + _tn(bg, xw[:, gs])

    def rowblk(width, col):
        return pl.BlockSpec((bsz, q, width), lambda c: (0, c, col))

    def const(shape):
        return pl.BlockSpec(shape, lambda c: (0, 0))

    xc3 = xc.reshape(bsz, lp, cfg.conv_dim)
    y, sin = pl.pallas_call(
        body, name=name, grid=(nc,),
        in_specs=[rowblk(inner, 0), rowblk(bcw, inner // bcw), rowblk(bcw, inner // bcw + 1),
                  rowblk(LANE, cfg.dtt), const((1, LANE)), const((1, LANE)), const((1, inner)),
                  const((q, q)), const((LANE, inner))],
        out_specs=[rowblk(inner, 0), pl.BlockSpec((bsz, 1, st, inner), lambda c: (0, c, 0, 0))],
        out_shape=[_sds((bsz, lp, inner), BF16), _sds((bsz, nc, st, inner), F32)],
        scratch_shapes=[pltpu.VMEM((bsz, st, inner), F32)], compiler_params=_cp(),
    )(xc3, xc3, xc3, small.reshape(bsz, lp, cfg.sw), dt_bias, avec, dexp, ltri, rexp)
    return y.reshape(cfg.t, inner), sin.reshape(bsz * nc, st, inner)


def ssd_bwd(cfg, xc, small, dt_bias, avec, dexp, sin, dy, *, name):
    q, inner, st, gw, g_n = cfg.chunk, cfg.inner, cfg.state, cfg.gw, cfg.groups
    nc = cfg.nchunks
    ltri, rexp = _ssd_consts(cfg)
    rexp_t = rexp.T
    hpt = LANE // cfg.hd
    tiles_per_group = gw // LANE
    bcw = g_n * st

    def body(x_ref, b_ref, c_ref, dt_ref, bias_ref, a_ref, d_ref, ltri_ref, rexp_ref, rexpt_ref, sin_ref, dy_ref,
             dx_ref, ddt_ref, dd_ref, da_ref, dbias_ref, ds_scr):
        step = pl.program_id(1)
        c_idx = nc - 1 - step

        @pl.when(step == 0)
        def _():
            ds_scr[...] = jnp.zeros_like(ds_scr)

        @pl.when(jnp.logical_and(step == 0, pl.program_id(0) == 0))
        def _():
            dd_ref[...] = jnp.zeros_like(dd_ref)
            da_ref[...] = jnp.zeros_like(da_ref)
            dbias_ref[...] = jnp.zeros_like(dbias_ref)

        ltri_v = ltri_ref[...]
        tri, tri_t = _tri_masks(q)
        red = _sel_dot
        rexpt = rexpt_ref[...]
        cq = _ssd_chunk_common(cfg, dt_ref[...], bias_ref[...], a_ref[...], c_idx, ltri_v, None)
        lane = lax.broadcasted_iota(jnp.int32, (q, LANE), 1)
        sub = lax.broadcasted_iota(jnp.int32, (LANE, q), 0)
        dcs = jnp.zeros((q, LANE), F32)
        dcs_t = jnp.zeros((LANE, q), F32)
        for g in range(g_n):
            bg_f = b_ref[:, g * st:(g + 1) * st]
            cg_f = c_ref[:, g * st:(g + 1) * st]
            bg = bg_f.astype(BF16)
            cg = cg_f.astype(BF16)
            gs = slice(g * gw, (g + 1) * gw)
            ex = _expand_heads(cq, rexp_ref[:, gs])
            xs = x_ref[:, gs]
            dyv = dy_ref[:, gs].astype(F32)
            s_in = sin_ref[0, :, gs]
            d_s = ds_scr[:, gs]
            xdt_f = xs * ex["DT"]
            xdt = xdt_f.astype(BF16)
            xw_f = xdt_f * ex["W0"]
            dd_ref[:, gs] += jnp.sum(dyv * xs, axis=0, keepdims=True)
            dy0 = (dyv * ex["E"]).astype(BF16)
            gmat = _nt(cg, bg)
            gmat_t = _nt(bg, cg)
            sing = s_in.astype(BF16)
            dsg = d_s.astype(BF16)
            y0 = _nn(cg, sing)
            dxw = _nn(bg, dsg)
            d_bg = _nt(xw_f.astype(BF16), dsg)
            d_cg = _nt(dy0, sing)
            ds_in_g = _tn(cg, dy0)
            dg = jnp.zeros((q, q), F32)
            dxdt_g = []
            for tt in range(tiles_per_group):
                tile = g * tiles_per_group + tt
                ts = slice(tt * LANE, (tt + 1) * LANE)
                xt = xdt[:, ts]
                dyt = dyv[:, ts]
                dyhs, lmats, mts = [], [], []
                for hh in range(hpt):
                    lmat, lmat_t = _head_l(cq, tile * hpt + hh, tri, tri_t)
                    inhead = jnp.logical_and(lane >= hh * cfg.hd, lane < (hh + 1) * cfg.hd)
                    dyhs.append(jnp.where(inhead, dyt, 0.0).astype(BF16))
                    lmats.append(lmat)
                    mts.append((gmat_t * lmat_t).astype(BF16))
                dy_stack = jnp.concatenate(dyhs, axis=0)
                dm_all = _nt(dy_stack, xt)
                for hh in range(hpt):
                    h = tile * hpt + hh
                    dm = dm_all[hh * q:(hh + 1) * q, :]
                    dg = dg + dm * lmats[hh]
                    qm = dm * gmat * lmats[hh]
                    rs = jnp.sum(qm, axis=1, keepdims=True)
                    csum = jnp.sum(qm, axis=0, keepdims=True)
                    dcs = dcs + jnp.where(lane == h, rs, 0.0)
                    dcs_t = dcs_t + jnp.where(sub == h, csum, 0.0)
                dxdt_g.append(_nn(jnp.concatenate(mts, axis=1), dy_stack))
            dxdt_diag = jnp.concatenate(dxdt_g, axis=1) if len(dxdt_g) > 1 else dxdt_g[0]
            dgb = dg.astype(BF16)
            d_cg = d_cg + _nn(dgb, bg)
            d_bg = d_bg + _tn(dgb, cg)
            dx_ref[:, inner + g * st:inner + (g + 1) * st] = d_bg
            dx_ref[:, inner + bcw + g * st:inner + bcw + (g + 1) * st] = d_cg
            dxdt = dxdt_diag + dxw * ex["W0"]
            dx_ref[:, gs] = dyv * d_ref[:, gs] + dxdt * ex["DT"]
            rt = rexpt_ref[gs, :]
            dcs = dcs + red(dyv * y0 * ex["E"], rt)
            r_w = red(dxw * xw_f, rt)
            dcs = dcs - r_w
            dcs_last_g = jnp.sum(r_w, axis=0, keepdims=True)
            ddec = red(jnp.broadcast_to(jnp.sum(d_s * s_in, axis=0, keepdims=True), (8, gw)), rt)[0:1, :]
            dcs_last_g = dcs_last_g + ddec * cq["decay"]
            dcs = dcs + jnp.where(lax.broadcasted_iota(jnp.int32, (q, LANE), 0) == q - 1, dcs_last_g, 0.0)
            ddt_part = red(dxdt * xs, rt)
            if g == 0:
                ddt = ddt_part
            else:
                ddt = ddt + ddt_part
            ds_scr[:, gs] = d_s * ex["DEC"] + ds_in_g
        dcs = dcs - dcs_t.T
        dadt = _sel_dot(dcs, ltri_v, left=True, trans=True)
        ddt = ddt + dadt * a_ref[...]
        da_ref[...] += jnp.sum(dadt * cq["dt"], axis=0, keepdims=True)
        draw = jnp.where(cq["live"], ddt * jax.nn.sigmoid(cq["pre"]), 0.0)
        ddt_ref[...] = draw
        dbias_ref[...] += jnp.sum(draw, axis=0, keepdims=True)

    def rowblk(width, col):
        return pl.BlockSpec((q, width), lambda b, s: (b * nc + nc - 1 - s, col))

    def const(shape):
        return pl.BlockSpec(shape, lambda b, s: (0, 0))

    bcol = inner // bcw
    outs = pl.pallas_call(
        body, name=name, grid=(cfg.bsz, nc),
        in_specs=[rowblk(inner, 0), rowblk(bcw, bcol), rowblk(bcw, bcol + 1), rowblk(LANE, cfg.dtt),
                  const((1, LANE)), const((1, LANE)), const((1, inner)), const((q, q)), const((LANE, inner)),
                  const((inner, LANE)),
                  pl.BlockSpec((1, st, inner), lambda b, s: (b * nc + nc - 1 - s, 0, 0)), rowblk(inner, 0)],
        out_specs=[rowblk(cfg.conv_dim, 0), rowblk(LANE, 0),
                   const((1, inner)), const((1, LANE)), const((1, LANE))],
        out_shape=[_sds((cfg.t, cfg.conv_dim), F32),
                   _sds((cfg.t, LANE), F32), _sds((1, inner), F32), _sds((1, LANE), F32), _sds((1, LANE), F32)],
        scratch_shapes=[pltpu.VMEM((st, inner), F32)], compiler_params=_cp(),
    )(xc, xc, xc, small, dt_bias, avec, dexp, ltri, rexp, rexp_t, sin, dy)
    return outs


def tail_fwd(cfg, y, z, w, *, name):
    t, inner, gw = cfg.t, cfg.inner, cfg.gw
    tr = _pick(t, 272, 16)

    def body(y_ref, z_ref, w_ref, o_ref):
        for g in range(cfg.groups):
            gs = slice(g * gw, (g + 1) * gw)
            yg = y_ref[:, gs].astype(F32) * _silu(z_ref[:, gs].astype(F32))
            r = lax.rsqrt(jnp.mean(yg * yg, axis=-1, keepdims=True) + EPS)
            o_ref[:, gs] = (yg * r * w_ref[:, gs]).astype(BF16)

    row = pl.BlockSpec((tr, inner), lambda i: (i, 0))
    return pl.pallas_call(
        body, name=name, grid=(t // tr,), in_specs=[row, row, pl.BlockSpec((1, inner), lambda i: (0, 0))],
        out_specs=row, out_shape=_sds((t, inner), BF16), compiler_params=_cp(),
    )(y, z, w.reshape(1, inner))


def tail_bwd(cfg, do, y, z, w, *, name):
    t, inner, gw = cfg.t, cfg.inner, cfg.gw
    tr = _pick(t, 272, 16)

    def body(do_ref, y_ref, z_ref, w_ref, dy_ref, dz_ref, dw_ref):
        @pl.when(pl.program_id(0) == 0)
        def _():
            dw_ref[...] = jnp.zeros_like(dw_ref)

        for g in range(cfg.groups):
            gs = slice(g * gw, (g + 1) * gw)
            yv = y_ref[:, gs].astype(F32)
            zv = z_ref[:, gs].astype(F32)
            dov = do_ref[:, gs].astype(F32)
            sz = _silu(zv)
            yg = yv * sz
            r = lax.rsqrt(jnp.mean(yg * yg, axis=-1, keepdims=True) + EPS)
            xh = yg * r
            gg = dov * w_ref[:, gs]
            dyg = r * (gg - xh * jnp.mean(gg * xh, axis=-1, keepdims=True))
            dw_ref[:, gs] += jnp.sum(dov * xh, axis=0, keepdims=True)
            dy_ref[:, gs] = (dyg * sz).astype(BF16)
            dz_ref[:, gs] = (dyg * yv * _dsilu(zv)).astype(BF16)

    row = pl.BlockSpec((tr, inner), lambda i: (i, 0))
    vec = pl.BlockSpec((1, inner), lambda i: (0, 0))
    dy, dz, dw = pl.pallas_call(
        body, name=name, grid=(t // tr,), in_specs=[row, row, row, vec], out_specs=[row, row, vec],
        out_shape=[_sds((t, inner), BF16), _sds((t, inner), BF16), _sds((1, inner), F32)], compiler_params=_cp(),
    )(do, y, z, w.reshape(1, inner))
    return dy, dz, dw[0]


def rope_tables(cfg):
    half = cfg.rope // 2
    pos = np.maximum(np.arange(cfg.lp) - cfg.pad, 0).astype(np.float32)
    inv = ROPE_THETA ** (-jnp.arange(0, cfg.rope, 2, dtype=F32) / cfg.rope)
    ang = jnp.asarray(pos)[:, None] * inv[None, :]
    cos, sin = jnp.cos(ang), jnp.sin(ang)
    zero = jnp.zeros((cfg.lp, LANE - 2 * half), F32)
    zh = jnp.zeros((cfg.lp, half), F32)
    ctab = jnp.concatenate([cos, cos, zero], axis=1)
    s1 = jnp.concatenate([-sin, zh, zero], axis=1)
    s2 = jnp.concatenate([zh, sin, zero], axis=1)
    return ctab, s1, s2


def _rope(x, c, s1, s2, half):
    return x * c + pltpu.roll(x, LANE - half, 1) * s1 + pltpu.roll(x, half, 1) * s2


def _rope_t(dy, c, s1, s2, half):
    return dy * c + pltpu.roll(dy * s1, half, 1) + pltpu.roll(dy * s2, LANE - half, 1)


def _attn_scale(cfg):
    return (cfg.nope + cfg.rope) ** -0.5


def rope_fwd(cfg, qf, small, tabs, *, name):
    t, qw, lp = cfg.t, cfg.qw, cfg.lp
    tr = _pick(lp, 544, 16)
    nrb = lp // tr
    half = cfg.rope // 2
    scale = _attn_scale(cfg)

    def body(q_ref, k_ref, c_ref, s1_ref, s2_ref, qo_ref, ko_ref):
        c, s1, s2 = c_ref[...], s1_ref[...], s2_ref[...]
        for h in range(cfg.mh):
            a = h * 2 * LANE
            qo_ref[:, a:a + LANE] = (q_ref[:, a:a + LANE].astype(F32) * scale).astype(BF16)
            qo_ref[:, a + LANE:a + 2 * LANE] = (
                _rope(q_ref[:, a + LANE:a + 2 * LANE].astype(F32), c, s1, s2, half) * scale).astype(BF16)
        ko_ref[...] = _rope(k_ref[...], c, s1, s2, half).astype(BF16)

    tab = pl.BlockSpec((tr, LANE), lambda i: (i % nrb, 0))
    return pl.pallas_call(
        body, name=name, grid=(t // tr,),
        in_specs=[pl.BlockSpec((tr, qw), lambda i: (i, 0)), pl.BlockSpec((tr, LANE), lambda i: (i, cfg.kt)), tab, tab, tab],
        out_specs=[pl.BlockSpec((tr, qw), lambda i: (i, 0)), pl.BlockSpec((tr, LANE), lambda i: (i, 0))],
        out_shape=[_sds((t, qw), BF16), _sds((t, LANE), BF16)], compiler_params=_cp(),
    )(qf, small, *tabs)


def rope_bwd(cfg, dq, dkpe, tabs, *, name):
    t, qw, lp = cfg.t, cfg.qw, cfg.lp
    tr = _pick(lp, 544, 16)
    nrb = lp // tr
    half = cfg.rope // 2
    scale = _attn_scale(cfg)

    def body(dq_ref, dk_ref, c_ref, s1_ref, s2_ref, qo_ref, ko_ref):
        c, s1, s2 = c_ref[...], s1_ref[...], s2_ref[...]
        for h in range(cfg.mh):
            a = h * 2 * LANE
            qo_ref[:, a:a + LANE] = (dq_ref[:, a:a + LANE].astype(F32) * scale).astype(BF16)
            qo_ref[:, a + LANE:a + 2 * LANE] = _rope_t(
                dq_ref[:, a + LANE:a + 2 * LANE].astype(F32) * scale, c, s1, s2, half).astype(BF16)
        ko_ref[...] = _rope_t(dk_ref[...], c, s1, s2, half)

    tab = pl.BlockSpec((tr, LANE), lambda i: (i % nrb, 0))
    return pl.pallas_call(
        body, name=name, grid=(t // tr,),
        in_specs=[pl.BlockSpec((tr, qw), lambda i: (i, 0)), pl.BlockSpec((tr, LANE), lambda i: (i, 0)),
                  tab, tab, tab],
        out_specs=[pl.BlockSpec((tr, qw), lambda i: (i, 0)), pl.BlockSpec((tr, LANE), lambda i: (i, 0))],
        out_shape=[_sds((t, qw), BF16), _sds((t, LANE), F32)], compiler_params=_cp(),
    )(dq, dkpe, *tabs)


def _q_blocks(cfg):
    bounds = [0, cfg.chunk] + list(range(cfg.chunk + 256, cfg.lp + 1, 256))
    assert bounds[-1] == cfg.lp, "SEQ must be a multiple of 256"
    return list(zip(bounds[:-1], bounds[1:]))


def _attn_mask(cfg, qs, qe):
    rows = qs + lax.broadcasted_iota(jnp.int32, (qe - qs, qe), 0)
    cols = lax.broadcasted_iota(jnp.int32, (qe - qs, qe), 1)
    return jnp.logical_and(cols <= rows, jnp.logical_or(cols >= cfg.pad, rows < cfg.pad))


def _max_q_block(cfg):
    return max(qe - qs for qs, qe in _q_blocks(cfg))


def _masked_scores(cfg, q, k2, qs, qe, s_scr):
    bq, n = qe - qs, qe
    s_scr[0:bq, 0:n] = _nt(q, k2)
    if qs == 0:
        s_scr[0:bq, 0:n] = jnp.where(_attn_mask(cfg, 0, qe), s_scr[0:bq, 0:n], MASK_VALUE)
    else:
        assert qs >= cfg.chunk and cfg.pad < LANE
        cols = lax.broadcasted_iota(jnp.int32, (bq, LANE), 1)
        s_scr[0:bq, 0:LANE] = jnp.where(cols >= cfg.pad, s_scr[0:bq, 0:LANE], MASK_VALUE)
        r = lax.broadcasted_iota(jnp.int32, (bq, bq), 0)
        c = lax.broadcasted_iota(jnp.int32, (bq, bq), 1)
        s_scr[0:bq, qs:qe] = jnp.where(c <= r, s_scr[0:bq, qs:qe], MASK_VALUE)
    return s_scr[0:bq, 0:n]


def attn_fwd(cfg, qr, kv, kpe, *, name):
    lp, t, mh = cfg.lp, cfg.t, cfg.mh
    assert mh <= LANE
    blocks = _q_blocks(cfg)

    def body(q_ref, kv_ref, kp_ref, o_ref, l_ref, s_scr):
        h = pl.program_id(1)

        @pl.when(h == 0)
        def _():
            l_ref[...] = jnp.zeros_like(l_ref)

        for qs, qe in blocks:
            n = qe
            q = q_ref[qs:qe, :]
            k2 = jnp.concatenate([kv_ref[0:n, 0:LANE], kp_ref[0:n, :]], axis=1)
            s = _masked_scores(cfg, q, k2, qs, qe, s_scr)
            m = jnp.max(s, axis=-1, keepdims=True)
            p = jnp.exp(s - m)
            l = jnp.sum(p, axis=-1, keepdims=True)
            o_ref[qs:qe, :] = (_nn(p.astype(BF16), kv_ref[0:n, LANE:2 * LANE]) * (1.0 / l)).astype(BF16)
            lane = lax.broadcasted_iota(jnp.int32, (qe - qs, LANE), 1)
            l_ref[qs:qe, :] = jnp.where(lane == h, m + jnp.log(l), l_ref[qs:qe, :])

    hb = pl.BlockSpec((lp, 2 * LANE), lambda b, h: (b, h))
    ob = pl.BlockSpec((lp, LANE), lambda b, h: (b, h))
    return pl.pallas_call(
        body, name=name, grid=(cfg.bsz, mh),
        in_specs=[hb, hb, pl.BlockSpec((lp, LANE), lambda b, h: (b, 0))],
        out_specs=[ob, pl.BlockSpec((lp, LANE), lambda b, h: (b, 0))],
        out_shape=[_sds((t, mh * LANE), BF16), _sds((t, LANE), F32)],
        scratch_shapes=[pltpu.VMEM((_max_q_block(cfg), lp), F32)], compiler_params=_cp(),
    )(qr, kv, kpe)


def attn_bwd(cfg, qr, kv, kpe, o, lse, do, *, name):
    lp, t, mh = cfg.lp, cfg.t, cfg.mh
    blocks = _q_blocks(cfg)

    def body(q_ref, kv_ref, kp_ref, o_ref, l_ref, do_ref, dq_ref, dkv_ref, dkp_ref, dk_acc, dv_acc, s_scr):
        dk_acc[...] = jnp.zeros_like(dk_acc)
        dv_acc[...] = jnp.zeros_like(dv_acc)
        for qs, qe in blocks:
            n = qe
            q = q_ref[qs:qe, :]
            k2 = jnp.concatenate([kv_ref[0:n, 0:LANE], kp_ref[0:n, :]], axis=1)
            dob = do_ref[qs:qe, :].astype(BF16)
            delta = jnp.sum(dob.astype(F32) * o_ref[qs:qe, :].astype(F32), axis=-1, keepdims=True)
            s = _masked_scores(cfg, q, k2, qs, qe, s_scr)
            lane = lax.broadcasted_iota(jnp.int32, (qe - qs, LANE), 1)
            lse = jnp.sum(jnp.where(lane == pl.program_id(1), l_ref[qs:qe, :], 0.0), axis=-1, keepdims=True)
            p = jnp.exp(s - lse)
            dp = _nt(dob, kv_ref[0:n, LANE:2 * LANE])
            ds = (p * (dp - delta)).astype(BF16)
            dq_ref[qs:qe, :] = _nn(ds, k2).astype(BF16)
            dv_acc[0:n, :] += _tn(p.astype(BF16), dob)
            dk_acc[0:n, :] += _tn(ds, q)
        dkv_ref[:, 0:LANE] = dk_acc[:, 0:LANE].astype(BF16)
        dkv_ref[:, LANE:2 * LANE] = dv_acc[...].astype(BF16)
        @pl.when(pl.program_id(1) == 0)
        def _():
            dkp_ref[...] = dk_acc[:, LANE:2 * LANE]

        @pl.when(pl.program_id(1) > 0)
        def _():
            dkp_ref[...] += dk_acc[:, LANE:2 * LANE]

    hb = pl.BlockSpec((lp, 2 * LANE), lambda b, h: (b, h))
    ob = pl.BlockSpec((lp, LANE), lambda b, h: (b, h))
    return pl.pallas_call(
        body, name=name, grid=(cfg.bsz, mh),
        in_specs=[hb, hb, pl.BlockSpec((lp, LANE), lambda b, h: (b, 0)), ob,
                  pl.BlockSpec((lp, LANE), lambda b, h: (b, 0)), ob],
        out_specs=[hb, hb, pl.BlockSpec((lp, LANE), lambda b, h: (b, 0))],
        out_shape=[_sds((t, cfg.qw), BF16), _sds((t, mh * 2 * LANE), BF16), _sds((t, LANE), F32)],
        scratch_shapes=[pltpu.VMEM((lp, 2 * LANE), F32), pltpu.VMEM((lp, LANE), F32),
                        pltpu.VMEM((_max_q_block(cfg), lp), F32)], compiler_params=_cp(),
    )(qr, kv, kpe, o, lse, do)


def _live_rows(cfg, tr, shape):
    rows = pl.program_id(1) * tr + lax.broadcasted_iota(jnp.int32, shape, 0)
    return rows >= cfg.pad


def gate_fwd(cfg, ya, yb, g, *, name):
    d, lp = cfg.d, cfg.lp
    tr = _pick(lp, 544, 16)
    nrb = lp // tr

    def body(ya_ref, yb_ref, ga_ref, gb_ref, o_ref):
        f = lambda ref: ref[...].astype(F32)
        mix = jax.nn.sigmoid(f(ga_ref)) * f(ya_ref) + jax.nn.sigmoid(f(gb_ref)) * f(yb_ref)
        o_ref[...] = jnp.where(_live_rows(cfg, tr, mix.shape), mix, 0.0).astype(BF16)

    row = pl.BlockSpec((tr, d), lambda b, j: (b * nrb + j, 0))
    row1 = pl.BlockSpec((tr, d), lambda b, j: (b * nrb + j, 1))
    return pl.pallas_call(
        body, name=name, grid=(cfg.bsz, nrb), in_specs=[row, row, row, row1], out_specs=row,
        out_shape=_sds((cfg.t, d), BF16), compiler_params=_cp(),
    )(ya, yb, g, g)


def gate_bwd(cfg, dmix, ya, yb, g, *, name):
    d, lp = cfg.d, cfg.lp
    tr = _pick(lp, 544, 16)
    nrb = lp // tr

    def body(dm_ref, ya_ref, yb_ref, ga_ref, gb_ref, dya_ref, dyb_ref, dg_ref):
        dm = dm_ref[...].astype(F32)
        dm = jnp.where(_live_rows(cfg, tr, dm.shape), dm, 0.0)
        sa = jax.nn.sigmoid(ga_ref[...].astype(F32))
        sb = jax.nn.sigmoid(gb_ref[...].astype(F32))
        dya_ref[...] = (dm * sa).astype(BF16)
        dyb_ref[...] = (dm * sb).astype(BF16)
        dg_ref[:, 0:d] = (dm * ya_ref[...].astype(F32) * sa * (1.0 - sa)).astype(BF16)
        dg_ref[:, d:2 * d] = (dm * yb_ref[...].astype(F32) * sb * (1.0 - sb)).astype(BF16)

    row = pl.BlockSpec((tr, d), lambda b, j: (b * nrb + j, 0))
    row1 = pl.BlockSpec((tr, d), lambda b, j: (b * nrb + j, 1))
    row2 = pl.BlockSpec((tr, 2 * d), lambda b, j: (b * nrb + j, 0))
    return pl.pallas_call(
        body, name=name, grid=(cfg.bsz, nrb), in_specs=[row, row, row, row, row1], out_specs=[row, row, row2],
        out_shape=[_sds((cfg.t, d), BF16), _sds((cfg.t, d), BF16), _sds((cfg.t, 2 * d), BF16)], compiler_params=_cp(),
    )(dmix, ya, yb, g, g)


def loss_head(cfg, h, target, w, *, name):
    d, q, nc = cfg.d, cfg.chunk, cfg.nchunks
    tpb = cfg.seq // q

    def body(h_ref, t_ref, w_ref, loss_ref, dh_ref, dw_ref, dhb_ref):
        j = pl.program_id(1)

        @pl.when(jnp.logical_and(j == 0, pl.program_id(0) == 0))
        def _():
            loss_ref[...] = jnp.zeros_like(loss_ref)
            dw_ref[...] = jnp.zeros_like(dw_ref)

        @pl.when(j == 0)
        def _():
            dh_ref[...] = jnp.zeros_like(dh_ref)
            dhb_ref[...] = jnp.zeros_like(dhb_ref)

        @pl.when(j > 0)
        def _():
            xv = h_ref[...]
            r = lax.rsqrt(jnp.mean(xv * xv, axis=-1, keepdims=True) + EPS)
            xh = xv * r
            err = xh * w_ref[...] - t_ref[...]
            loss_ref[...] += 0.5 * jnp.sum(jnp.sum(err * err, axis=-1, keepdims=True) / d, axis=0, keepdims=True)
            dy = err * (1.0 / d)
            g = dy * w_ref[...]
            dh = r * (g - xh * jnp.mean(g * xh, axis=-1, keepdims=True))
            dh_ref[...] = dh
            dhb_ref[...] = dh.astype(BF16)
            dw_ref[...] += jnp.sum(dy * xh, axis=0, keepdims=True)

    row = pl.BlockSpec((q, d), lambda b, j: (b * nc + j, 0))
    loss, dh, dw, dhb = pl.pallas_call(
        body, name=name, grid=(cfg.bsz, nc),
        in_specs=[row, pl.BlockSpec((q, d), lambda b, j: (b * tpb + jnp.maximum(j - 1, 0), 0)),
                  pl.BlockSpec((1, d), lambda b, j: (0, 0))],
        out_specs=[pl.BlockSpec((8, LANE), lambda b, j: (0, 0)), row, pl.BlockSpec((1, d), lambda b, j: (0, 0)), row],
        out_shape=[_sds((8, LANE), F32), _sds((cfg.t, d), F32), _sds((1, d), F32), _sds((cfg.t, d), BF16)],
        compiler_params=_cp(),
    )(h, target, w.reshape(1, d))
    return loss[0, 0], (dh, dhb), dw[0]


def _rows_tile(r, c):
    return _pick(r, max(8, (1 << 18) // max(c, 1) // 8 * 8), 8)


def _adam_update(w, g, m, v):
    c1 = 1.0 - ADAM_B1 ** ADAM_STEP
    c2 = 1.0 - ADAM_B2 ** ADAM_STEP
    mn = ADAM_B1 * m + (1.0 - ADAM_B1) * g
    vn = ADAM_B2 * v + (1.0 - ADAM_B2) * (g * g)
    delta = -ADAM_LR * ((mn / c1) / (jnp.sqrt(vn / c2) + ADAM_EPS) + ADAM_WD * w)
    return delta, mn, vn


def adamw_layer(w, m, v, g, li, prev, dep, *, name):
    _, r, c = w.shape
    tr = _rows_tile(r, c)

    def body(*refs):
        w_ref, m_ref, v_ref, g_ref = refs[:4]
        go_ref, d_ref, mo_ref, vo_ref = refs[-4:]
        gv = g_ref[...]
        delta, mn, vn = _adam_update(w_ref[0], gv, m_ref[0], v_ref[0])
        go_ref[0] = gv
        d_ref[0] = delta
        mo_ref[0] = mn
        vo_ref[0] = vn

    if tr * c * 4 >= (1 << 16):
        steps = r // tr
        blk3 = pl.BlockSpec((1, tr, c), lambda i: (li, i, 0))
        blk2 = pl.BlockSpec((tr, c), lambda i: (i, 0))
    else:
        tc = _pick(c, max(LANE, (1 << 18) // r // LANE * LANE), LANE)
        steps = c // tc
        blk3 = pl.BlockSpec((1, r, tc), lambda i: (li, 0, i))
        blk2 = pl.BlockSpec((r, tc), lambda i: (0, i))
    anyspec = pl.BlockSpec(memory_space=pl.ANY)
    in_specs = [blk3, blk3, blk3, blk2, anyspec]
    args = [w, m, v, g, dep]
    aliases = {}
    if prev is not None:
        in_specs += [anyspec] * 4
        args += list(prev)
        aliases = {5 + i: i for i in range(4)}
    return pl.pallas_call(
        body, name=name, grid=(steps,), in_specs=in_specs, out_specs=[blk3] * 4,
        out_shape=[_sds(w.shape, F32)] * 4, input_output_aliases=aliases, compiler_params=_cp(),
    )(*args)


def pair_add(g4, other, half, *, name):
    n, _, r, c = g4.shape
    tr = _rows_tile(r, c)

    def body(h_ref, a_ref, b_ref, o_ref):
        o_ref[0] = (a_ref[0, 0].astype(F32) + b_ref[0].astype(F32)).astype(BF16)

    blk = pl.BlockSpec((1, tr, c), lambda j, i, h: (j, i, 0))
    grid_spec = pltpu.PrefetchScalarGridSpec(
        num_scalar_prefetch=1, grid=(n, r // tr),
        in_specs=[pl.BlockSpec((1, 1, tr, c), lambda j, i, h: (j, h[0], i, 0)), blk], out_specs=blk)
    return pl.pallas_call(body, name=name, grid_spec=grid_spec, out_shape=_sds((n, r, c), BF16),
                          compiler_params=_cp())(half, g4, other)


def chip_sum(recv, part, where, *, name):
    n, r, c = recv.shape
    tr = _rows_tile(r, c)

    def body(s_ref, *refs):
        own_ref, o_ref = refs[n], refs[n + 1]
        acc = None
        for j in range(n):
            term = jnp.where(s_ref[0] == j, own_ref[0], refs[j][0]).astype(F32)
            acc = term if acc is None else acc + term
        o_ref[0] = acc

    def slot(j):
        return pl.BlockSpec((1, tr, c), lambda i, s: (jnp.where(s[0] == j, (j + 1) % n, j), i, 0))

    grid_spec = pltpu.PrefetchScalarGridSpec(
        num_scalar_prefetch=1, grid=(r // tr,),
        in_specs=[slot(j) for j in range(n)] + [pl.BlockSpec((1, tr, c), lambda i, s: (s[0], i, 0))],
        out_specs=pl.BlockSpec((1, tr, c), lambda i, s: (s[1], i, 0)))
    return pl.pallas_call(body, name=name, grid_spec=grid_spec, out_shape=_sds((2, r, c), F32),
                          compiler_params=_cp())(where, *([recv] * n), part)


def _coords():
    return lax.axis_index("x"), lax.axis_index("y"), lax.axis_index("c")


def _other_chips(x, y):
    return [(1 - x, y), (x, 1 - y), (1 - x, 1 - y)]


def gather_chips(arrs, *, name):
    n = len(arrs)
    anyspec = pl.BlockSpec(memory_space=pl.ANY)

    def body(*refs):
        ins, outs = refs[:n], refs[n:2 * n]
        send_sems, recv_sems, local_sems = refs[2 * n:]
        x, y, c = _coords()
        me = 2 * x + y
        chips = _other_chips(x, y)
        copies = []
        for k in range(n):
            loc = pltpu.make_async_copy(ins[k], outs[k].at[me], local_sems.at[k])
            loc.start()
            copies.append(loc)
        sends = []
        for k in range(n):
            for j, (px, py) in enumerate(chips):
                cp = pltpu.make_async_remote_copy(
                    src_ref=ins[k], dst_ref=outs[k].at[me], send_sem=send_sems.at[k, j], recv_sem=recv_sems.at[k, j],
                    device_id=(px, py, c), device_id_type=MESH)
                cp.start()
                sends.append(cp)
        for k in range(n):
            for j, (px, py) in enumerate(chips):
                pltpu.make_async_remote_copy(
                    src_ref=ins[k], dst_ref=outs[k].at[2 * px + py], send_sem=send_sems.at[k, j],
                    recv_sem=recv_sems.at[k, j], device_id=(px, py, c), device_id_type=MESH).wait_recv()
        for cp in sends:
            cp.wait_send()
        for cp in copies:
            cp.wait()

    return pl.pallas_call(
        body, name=name, in_specs=[anyspec] * n, out_specs=[anyspec] * n,
        out_shape=[_sds((4,) + a.shape, a.dtype) for a in arrs],
        scratch_shapes=[pltpu.SemaphoreType.DMA((n, 3)), pltpu.SemaphoreType.DMA((n, 3)), pltpu.SemaphoreType.DMA((n,))],
        compiler_params=_cp(has_side_effects=True),
    )(*arrs)


def allreduce_small(vec, after, *, name):
    r, c = vec.shape

    def body(v_ref, after_ref, o_ref, buf, send_sems, recv_sems):
        x, y, cc = _coords()
        me = 4 * x + 2 * y + cc
        buf[me] = v_ref[...]
        sends = []
        flips = [(fx, fy, fc) for fx in (0, 1) for fy in (0, 1) for fc in (0, 1)][1:]
        for j, (fx, fy, fc) in enumerate(flips):
            peer = ((1 - x) if fx else x, (1 - y) if fy else y, (1 - cc) if fc else cc)
            cp = pltpu.make_async_remote_copy(
                src_ref=v_ref, dst_ref=buf.at[me], send_sem=send_sems.at[j], recv_sem=recv_sems.at[j],
                device_id=peer, device_id_type=MESH)
            cp.start()
            sends.append(cp)
        for j, (fx, fy, fc) in enumerate(flips):
            px, py, pc = ((1 - x) if fx else x, (1 - y) if fy else y, (1 - cc) if fc else cc)
            pltpu.make_async_remote_copy(
                src_ref=v_ref, dst_ref=buf.at[4 * px + 2 * py + pc], send_sem=send_sems.at[j],
                recv_sem=recv_sems.at[j], device_id=(px, py, pc), device_id_type=MESH).wait_recv()
        for cp in sends:
            cp.wait_send()
        acc = buf[0]
        for k in range(1, 8):
            acc = acc + buf[k]
        o_ref[...] = acc

    vm = pl.BlockSpec(memory_space=pltpu.VMEM)
    return pl.pallas_call(
        body, name=name, in_specs=[vm, pl.BlockSpec(memory_space=pl.ANY)], out_specs=vm, out_shape=_sds((r, c), F32),
        scratch_shapes=[pltpu.VMEM((8, r, c), F32), pltpu.SemaphoreType.DMA((7,)), pltpu.SemaphoreType.DMA((7,))],
        compiler_params=_cp(has_side_effects=True),
    )(vec, after)


def pair_share(lands, owns, *, name):
    n = len(lands)
    anyspec = pl.BlockSpec(memory_space=pl.ANY)

    def body(*refs):
        ins, own_refs, outs = refs[:n], refs[n:2 * n], refs[2 * n:3 * n]
        send_sems, recv_sems = refs[3 * n:]
        x, y, c = _coords()
        me = 2 * x + y
        sib = (x, y, 1 - c)
        sends = []
        for k in range(n):
            for j, (px, py) in enumerate(_other_chips(x, y)):
                cp = pltpu.make_async_remote_copy(
                    src_ref=ins[k].at[2 * px + py, c], dst_ref=outs[k].at[2 * px + py, c], send_sem=send_sems.at[k, j],
                    recv_sem=recv_sems.at[k, j], device_id=sib, device_id_type=MESH)
                cp.start()
                sends.append(cp)
            cp = pltpu.make_async_remote_copy(
                src_ref=own_refs[k], dst_ref=outs[k].at[me], send_sem=send_sems.at[k, 3], recv_sem=recv_sems.at[k, 3],
                device_id=sib, device_id_type=MESH)
            cp.start()
            sends.append(cp)
        for k in range(n):
            for j, (px, py) in enumerate(_other_chips(x, y)):
                pltpu.make_async_remote_copy(
                    src_ref=ins[k].at[2 * px + py, c], dst_ref=outs[k].at[2 * px + py, 1 - c],
                    send_sem=send_sems.at[k, j], recv_sem=recv_sems.at[k, j], device_id=sib,
                    device_id_type=MESH).wait_recv()
            pltpu.make_async_remote_copy(
                src_ref=own_refs[k], dst_ref=outs[k].at[me], send_sem=send_sems.at[k, 3], recv_sem=recv_sems.at[k, 3],
                device_id=sib, device_id_type=MESH).wait_recv()
        for cp in sends:
            cp.wait_send()

    return pl.pallas_call(
        body, name=name, in_specs=[anyspec] * (2 * n), out_specs=[anyspec] * n,
        out_shape=[_sds(a.shape, a.dtype) for a in lands], input_output_aliases={k: k for k in range(n)},
        scratch_shapes=[pltpu.SemaphoreType.DMA((n, 4)), pltpu.SemaphoreType.DMA((n, 4))],
        compiler_params=_cp(has_side_effects=True),
    )(*lands, *owns)


_HBM = pl.BlockSpec(memory_space=pltpu.HBM)
_SEM = pl.BlockSpec(memory_space=pltpu.SEMAPHORE)


_COPIES_PER_ARRAY = {"gather": 3, "scatter": 3, "share": 4, "exchange": 4, "fill": 1}


def _ici_copies(kind, srcs, lands, send_sems, recv_sems):
    x, y, c = _coords()
    me = 2 * x + y
    per = _COPIES_PER_ARRAY[kind]
    sends, recvs = [], []
    for k in range(len(srcs)):
        triples = []
        for j, (px, py) in enumerate(_other_chips(x, y)):
            peer = 2 * px + py
            if kind == "gather":
                triples.append((srcs[k].at[c], lands[k].at[me, c], lands[k].at[peer, c], (px, py, c)))
            elif kind == "scatter":
                triples.append((srcs[k].at[peer], lands[k].at[me], lands[k].at[peer], (px, py, c)))
            elif kind == "share":
                triples.append((lands[k].at[peer, c], lands[k].at[peer, c], lands[k].at[peer, 1 - c], (x, y, 1 - c)))
        if kind == "share":
            triples.append((srcs[k], lands[k].at[me], lands[k].at[me], (x, y, 1 - c)))
        if kind == "exchange":
            triples = [(srcs[k].at[j, 1 - c], lands[k].at[j], lands[k].at[j], (x, y, 1 - c)) for j in range(4)]
        if kind == "fill":
            triples = [(lands[k].at[c], lands[k].at[c], lands[k].at[1 - c], (x, y, 1 - c))]
        for j, (src, there, here, dev) in enumerate(triples):
            sem = per * k + j
            mk = functools.partial(pltpu.make_async_remote_copy, src_ref=src, send_sem=send_sems.at[sem],
                                   recv_sem=recv_sems.at[sem], device_id=dev, device_id_type=MESH)
            sends.append(mk(dst_ref=there))
            recvs.append(mk(dst_ref=here))
    return sends, recvs


def ici_start(kind, srcs, lands, after, *, name):
    n = len(srcs)

    def body(*refs):
        src_refs, land_refs = refs[:n], refs[n:2 * n]
        send_sems, recv_sems = refs[2 * n + 1], refs[2 * n + 2]
        token = refs[-1]
        sends, _ = _ici_copies(kind, src_refs, land_refs, send_sems, recv_sems)
        for cp in sends:
            cp.start()
        token[...] = jnp.zeros_like(token)

    both = list(srcs) + list(lands)
    out = pl.pallas_call(
        body, name=name,
        in_specs=[_HBM] * (2 * n) + [pl.BlockSpec(memory_space=pl.ANY)],
        out_shape=(pltpu.SemaphoreType.DMA((_COPIES_PER_ARRAY[kind] * n,)),
                   pltpu.SemaphoreType.DMA((_COPIES_PER_ARRAY[kind] * n,)),
                   *[pltpu.HBM(a.shape, a.dtype) for a in both], _sds((8, LANE), F32)),
        out_specs=(_SEM, _SEM, *([_HBM] * (2 * n)), pl.BlockSpec(memory_space=pltpu.VMEM)),
        input_output_aliases={i: 2 + i for i in range(2 * n)},
        compiler_params=_cp(has_side_effects=pltpu.SideEffectType.DATAFLOW_SIDE_EFFECTING),
    )(*[pltpu.with_memory_space_constraint(a, pltpu.HBM) for a in both], after)
    return out[0], out[1], list(out[2:2 + n]), list(out[2 + n:2 + 2 * n]), out[-1]


def ici_wait(kind, started, after, *, name):
    send_sems, recv_sems, srcs, lands, _ = started
    n = len(srcs)

    def body(*refs):
        src_refs, land_refs = refs[:n], refs[n:2 * n]
        sends, recvs = _ici_copies(kind, src_refs, land_refs, refs[2 * n], refs[2 * n + 1])
        for cp in sends:
            cp.wait_send()
        for cp in recvs:
            cp.wait_recv()

    both = list(srcs) + list(lands)
    out = pl.pallas_call(
        body, name=name,
        in_specs=[_HBM] * (2 * n) + [_SEM, _SEM, pl.BlockSpec(memory_space=pl.ANY)],
        out_shape=tuple(pltpu.HBM(a.shape, a.dtype) for a in both), out_specs=tuple([_HBM] * (2 * n)),
        input_output_aliases={i: i for i in range(2 * n)},
        compiler_params=_cp(has_side_effects=pltpu.SideEffectType.DATAFLOW_SIDE_EFFECTING),
    )(*both, send_sems, recv_sems, after)
    return list(out[:n]), list(out[n:])


BIG = ["w_in", "w_uq", "w_ukv", "w_branch_ssm", "w_branch_mla", "w_out", "w_mlp_up", "w_mlp_down"]
COL_SHARDED = {"w_in", "w_uq", "w_ukv", "w_mlp_up"}
SMALL_REPL = ["norm_mix_w", "conv_b", "dt_bias", "a_log", "d_skip", "ssm_norm_w", "q_norm_w", "kv_norm_w", "norm_mlp_w"]


def _unshard_layer(name, g):
    _, r, c = g.shape
    if name in COL_SHARDED:
        return jnp.transpose(g, (1, 0, 2)).reshape(r, 4 * c)
    return g.reshape(4 * r, c)


def _to_shards(name, full):
    r, c = full.shape
    if name in COL_SHARDED:
        return jnp.transpose(full.reshape(r, 4, c // 4), (1, 0, 2))
    return full.reshape(4, r // 4, c)


REST = [k for k in BIG if k != "w_in"]


def prep_layer(cfg, w):
    out = {}
    if "w_in" in w:
        sp = np.cumsum(cfg.in_splits)[:-1].tolist()
        z, xbc, dt, cq, ckv, kr, gs, gm = jnp.split(w["w_in"], sp, axis=1)
        zpad = lambda n: jnp.zeros((cfg.d, n), z.dtype)
        out.update(w_z=z, w_xbc=xbc, w_g=jnp.concatenate([gs, gm], axis=1),
                   w_s=jnp.concatenate([cq, ckv, kr, zpad(LANE - cfg.rope), dt, zpad(LANE - cfg.heads)], axis=1))
    if "w_uq" in w:
        out.update(
            w_uq=jnp.pad(w["w_uq"].reshape(cfg.ql, cfg.mh, cfg.nope + cfg.rope),
                         ((0, 0), (0, 0), (0, 2 * LANE - cfg.nope - cfg.rope))).reshape(cfg.ql, cfg.qw),
            w_ukv=w["w_ukv"], w_bs=w["w_branch_ssm"], w_bm=w["w_branch_mla"], w_out=w["w_out"],
            w_up=w["w_mlp_up"], w_down=w["w_mlp_down"])
    return {k: v.astype(BF16) for k, v in out.items()}


def unprep_grads(cfg, g):
    out = {}
    if "w_s" in g:
        ql, kvl = cfg.ql, cfg.kvl
        ds_ = g["w_s"]
        cq, ckv = ds_[:, :ql], ds_[:, ql:ql + kvl]
        kr = ds_[:, ql + kvl:ql + kvl + cfg.rope]
        dt = ds_[:, ql + kvl + LANE:ql + kvl + LANE + cfg.heads]
        out["w_in"] = jnp.concatenate([g["w_z"], g["w_xbc"], dt, cq, ckv, kr, g["w_g"]], axis=1)
    if "w_uq" in g:
        out.update(
            w_uq=g["w_uq"].reshape(cfg.ql, cfg.mh, 2 * LANE)[:, :, :cfg.nope + cfg.rope].reshape(cfg.ql, -1),
            w_ukv=g["w_ukv"], w_branch_ssm=g["w_bs"], w_branch_mla=g["w_bm"],
            w_out=g["w_out"], w_mlp_up=g["w_up"], w_mlp_down=g["w_down"])
    return out


def _hook(hooks, name, arg):
    if hooks and name in hooks:
        return hooks[name](arg)[0, 0]
    return 0.0


def layer_fwd(cfg, h, pw, sm, tabs, li, hooks=None):
    n = lambda s: f"l{li}_{s}"
    u = rmsnorm_fwd(h, sm["norm_mix_w"], name=n("norm_mix"))
    z, xbc, g, small = matmul_multi(u, [pw["w_z"], pw["w_xbc"], pw["w_g"], pw["w_s"]], (BF16, F32, BF16, F32),
                                    name=n("in_proj"))
    xc, dsilu = conv_fwd(cfg, xbc, sm["conv_w"], sm["conv_b"], name=n("conv"))
    dt_bias = sm["dt_bias_p"] + _hook(hooks, "after_conv", xc)
    y, sin = ssd_fwd(cfg, xc, small, dt_bias, sm["avec"], sm["dexp"], name=n("ssd"))
    y_ssm = tail_fwd(cfg, y, z, sm["ssm_norm_w"], name=n("tail"))
    if hooks and "weights" in hooks:
        pw = dict(pw, **hooks["weights"](y_ssm))
    cqn = rmsnorm_fwd(small, sm["q_norm_w"], cw=cfg.ql, ci=0, name=n("q_norm"))
    ckvn = rmsnorm_fwd(small, sm["kv_norm_w"], cw=cfg.kvl, ci=cfg.ql // cfg.kvl, name=n("kv_norm"))
    qf = matmul(cqn, pw["w_uq"], out_dtype=BF16, name=n("uq"))
    kv = matmul(ckvn, pw["w_ukv"], out_dtype=BF16, name=n("ukv"))
    qr, kpe = rope_fwd(cfg, qf, small, tabs, name=n("rope"))
    o, lse = attn_fwd(cfg, qr, kv, kpe, name=n("attn"))
    ya = matmul(y_ssm, pw["w_bs"], out_dtype=BF16, name=n("branch_ssm"))
    yb = matmul(o, pw["w_bm"], out_dtype=BF16, name=n("branch_mla"))
    mixed = gate_fwd(cfg, ya, yb, g, name=n("gate"))
    h1 = matmul(mixed, pw["w_out"], add=h, name=n("out"))
    v = rmsnorm_fwd(h1, sm["norm_mlp_w"] + _hook(hooks, "after_attn", o), name=n("norm_mlp"))
    a, act = matmul(v, pw["w_up"], name=n("up"), epilogue=_ep_relu2, out_dtypes=(BF16, BF16))
    h2 = matmul(act, pw["w_down"], add=h1, name=n("down"))
    saved = dict(h=h, u=u, z=z, xbc=xbc, g=g, small=small, xc=xc, dsilu=dsilu, y=y, sin=sin, y_ssm=y_ssm, cqn=cqn, ckvn=ckvn,
                 qr=qr, kv=kv, kpe=kpe, o=o, lse=lse, ya=ya, yb=yb, mixed=mixed, h1=h1, v=v, a=a, act=act)
    return h2, saved, pw


def layer_bwd(cfg, dh2, pw, sm, tabs, s, li, hooks=None):
    n = lambda t: f"l{li}_b_{t}"
    gw, gs = {}, {}
    wgrad = functools.partial(matmul, ta=True, out_dtype=BF16)
    dh2, dh2b = dh2
    gw["w_down"] = wgrad(s["act"], dh2b, name=n("dw_down"))
    da = matmul(dh2b, pw["w_down"], tb=True, name=n("dact"), epilogue=_ep_relu2_grad, extras=(s["a"],),
                out_dtypes=(BF16,))
    gw["w_up"] = wgrad(s["v"], da, name=n("dw_up"))
    dv = matmul(da, pw["w_up"], tb=True, out_dtype=BF16, name=n("dv"))
    dh1, gs["norm_mlp_w"], dh1b = rmsnorm_bwd(dv, s["h1"], sm["norm_mlp_w"], res=dh2, with_bf16=True,
                                              name=n("norm_mlp"))
    gw["w_out"] = wgrad(s["mixed"], dh1b, name=n("dw_out"))
    dmix = matmul(dh1b, pw["w_out"], tb=True, out_dtype=BF16, name=n("dmix"))
    dya, dyb, dg = gate_bwd(cfg, dmix, s["ya"], s["yb"], s["g"], name=n("gate"))
    gw["w_bs"] = wgrad(s["y_ssm"], dya, name=n("dw_bs"))
    gw["w_bm"] = wgrad(s["o"], dyb, name=n("dw_bm"))
    dy_ssm = matmul(dya, pw["w_bs"], tb=True, out_dtype=BF16, name=n("dy_ssm"))
    do = matmul(dyb, pw["w_bm"], tb=True, out_dtype=BF16, name=n("do"))
    dq, dkv, dkpe = attn_bwd(cfg, s["qr"], s["kv"], s["kpe"], s["o"], s["lse"], do, name=n("attn"))
    dqf, dkr = rope_bwd(cfg, dq, dkpe, tabs, name=n("rope"))
    gw["w_uq"] = wgrad(s["cqn"], dqf, name=n("dw_uq"))
    gw["w_ukv"] = wgrad(s["ckvn"], dkv, name=n("dw_ukv"))
    dcqn = matmul(dqf, pw["w_uq"], tb=True, name=n("dcqn"))
    dckvn = matmul(dkv, pw["w_ukv"], tb=True, name=n("dckvn"))
    q_norm_w = sm["q_norm_w"] + _hook(hooks, "after_attn", dqf)
    dcq, gs["q_norm_w"] = rmsnorm_bwd(dcqn, s["small"], q_norm_w, cw=cfg.ql, ci=0, out_dtype=BF16, name=n("q_norm"))
    dckv, gs["kv_norm_w"] = rmsnorm_bwd(dckvn, s["small"], sm["kv_norm_w"], cw=cfg.kvl, ci=cfg.ql // cfg.kvl,
                                        out_dtype=BF16, name=n("kv_norm"))
    ssm_norm_w = sm["ssm_norm_w"] + _hook(hooks, "early", dict(gw))
    dy, dz, gs["ssm_norm_w"] = tail_bwd(cfg, dy_ssm, s["y"], s["z"], ssm_norm_w, name=n("tail"))
    dxc, ddt, ddexp, dav, dbias = ssd_bwd(cfg, s["xc"], s["small"], sm["dt_bias_p"], sm["avec"], sm["dexp"],
                                          s["sin"], dy, name=n("ssd"))
    conv_w = sm["conv_w"] + _hook(hooks, "after_ssd", dxc)
    dxbc, gs["conv_w"], gs["conv_b"] = conv_bwd(cfg, s["xbc"], conv_w, s["dsilu"], dxc, name=n("conv"))
    gs["d_skip"] = ddexp.reshape(cfg.heads, cfg.hd).sum(axis=1)
    gs["a_log"] = (dav[0] * sm["avec"][0])[:cfg.heads]
    gs["dt_bias"] = dbias[0, :cfg.heads]
    dsmall = jnp.concatenate([dcq, dckv, dkr.astype(BF16), ddt.astype(BF16)], axis=1)
    gw["w_z"] = wgrad(s["u"], dz, name=n("dw_z"))
    gw["w_xbc"] = wgrad(s["u"], dxbc, name=n("dw_xbc"))
    gw["w_g"] = wgrad(s["u"], dg, name=n("dw_g"))
    gw["w_s"] = wgrad(s["u"], dsmall, name=n("dw_s"))
    du = matmul_nt_sum([dz, dxbc, dg, dsmall], [pw["w_z"], pw["w_xbc"], pw["w_g"], pw["w_s"]], out_dtype=BF16,
                       name=n("du"))
    if li > 0:
        dh, gs["norm_mix_w"], dhb = rmsnorm_bwd(du, s["h"], sm["norm_mix_w"], res=dh1, with_bf16=True,
                                                name=n("norm_mix"))
    else:
        dh, gs["norm_mix_w"] = rmsnorm_bwd(du, s["h"], sm["norm_mix_w"], res=dh1, name=n("norm_mix"))
        dhb = None
    return (dh, dhb), gw, gs


def small_params(cfg, p, li):
    pad_l = lambda v: jnp.pad(v, (0, LANE - v.shape[0])).reshape(1, LANE)
    return dict(
        norm_mix_w=p["norm_mix_w"][li], conv_w=p["conv_w"][li], conv_b=p["conv_b"][li],
        dt_bias_p=pad_l(p["dt_bias"][li]), avec=pad_l(-jnp.exp(p["a_log"][li])),
        dexp=jnp.repeat(p["d_skip"][li], cfg.hd).reshape(1, cfg.inner),
        ssm_norm_w=p["ssm_norm_w"][li], q_norm_w=p["q_norm_w"][li], kv_norm_w=p["kv_norm_w"][li],
        norm_mlp_w=p["norm_mlp_w"][li])


def local_step(cfg, x, target, p, depth=2):
    bsz, d = cfg.bsz, cfg.d
    lead = jnp.zeros((bsz, cfg.pad, d), F32)
    meta = jnp.broadcast_to(p["meta_tokens"][None], (bsz, cfg.n_meta, d))
    h = jnp.concatenate([lead, meta, x], axis=1).reshape(cfg.t, d)
    tabs = rope_tables(cfg)
    saved, sms = [], []
    for li in range(depth):
        sm = small_params(cfg, p, li)
        h, s, _ = layer_fwd(cfg, h, p["pw"][li], sm, tabs, li)
        saved.append(s)
        sms.append(sm)
    loss, dh, dfw = loss_head(cfg, h, target.reshape(bsz * cfg.seq, d), p["final_norm_w"], name="loss_head")
    gws, gss = [None] * depth, [None] * depth
    for li in reversed(range(depth)):
        dh, gws[li], gss[li] = layer_bwd(cfg, dh, p["pw"][li], sms[li], tabs, saved[li], li)
    dh = dh[0].reshape(bsz, cfg.lp, d)
    grad_x = dh[:, cfg.chunk:, :]
    gmeta = jnp.sum(dh[:, cfg.pad:cfg.chunk, :], axis=0)
    return loss, grad_x, gmeta, gws, gss, dfw


def _pack_small(parts):
    flat = jnp.concatenate([a.reshape(-1) for a in parts])
    n = flat.shape[0]
    npad = -n % (8 * LANE)
    return jnp.pad(flat, (0, npad)).reshape(-1, LANE), n


def _unpack_small(vec, shapes):
    flat = vec.reshape(-1)
    out, off = [], 0
    for sh in shapes:
        sz = int(np.prod(sh))
        out.append(flat[off:off + sz].reshape(sh))
        off += sz
    return out


def _as2d(a):
    return a.reshape(-1, a.shape[-1])


def kernel(x, meta_tokens, norm_mix_w, w_in, conv_w, conv_b, dt_bias, a_log, d_skip, ssm_norm_w, q_norm_w, kv_norm_w, w_uq, w_ukv, w_branch_ssm, w_branch_mla, w_out, norm_mlp_w, w_mlp_up, w_mlp_down, final_norm_w, loss_target, m_meta_tokens, m_norm_mix_w, m_w_in, m_conv_w, m_conv_b, m_dt_bias, m_a_log, m_d_skip, m_ssm_norm_w, m_q_norm_w, m_kv_norm_w, m_w_uq, m_w_ukv, m_w_branch_ssm, m_w_branch_mla, m_w_out, m_norm_mlp_w, m_w_mlp_up, m_w_mlp_down, m_final_norm_w, v_meta_tokens, v_norm_mix_w, v_w_in, v_conv_w, v_conv_b, v_dt_bias, v_a_log, v_d_skip, v_ssm_norm_w, v_q_norm_w, v_kv_norm_w, v_w_uq, v_w_ukv, v_w_branch_ssm, v_w_branch_mla, v_w_out, v_norm_mlp_w, v_w_mlp_up, v_w_mlp_down, v_final_norm_w):
    cfg = CFG
    names = ["meta_tokens", "norm_mix_w", "w_in", "conv_w", "conv_b", "dt_bias", "a_log", "d_skip", "ssm_norm_w",
             "q_norm_w", "kv_norm_w", "w_uq", "w_ukv", "w_branch_ssm", "w_branch_mla", "w_out", "norm_mlp_w",
             "w_mlp_up", "w_mlp_down", "final_norm_w"]
    wts = dict(zip(names, [meta_tokens, norm_mix_w, w_in, conv_w, conv_b, dt_bias, a_log, d_skip, ssm_norm_w,
                           q_norm_w, kv_norm_w, w_uq, w_ukv, w_branch_ssm, w_branch_mla, w_out, norm_mlp_w,
                           w_mlp_up, w_mlp_down, final_norm_w]))
    ms = dict(zip(names, [m_meta_tokens, m_norm_mix_w, m_w_in, m_conv_w, m_conv_b, m_dt_bias, m_a_log, m_d_skip,
                          m_ssm_norm_w, m_q_norm_w, m_kv_norm_w, m_w_uq, m_w_ukv, m_w_branch_ssm, m_w_branch_mla,
                          m_w_out, m_norm_mlp_w, m_w_mlp_up, m_w_mlp_down, m_final_norm_w]))
    vs = dict(zip(names, [v_meta_tokens, v_norm_mix_w, v_w_in, v_conv_w, v_conv_b, v_dt_bias, v_a_log, v_d_skip,
                          v_ssm_norm_w, v_q_norm_w, v_kv_norm_w, v_w_uq, v_w_ukv, v_w_branch_ssm, v_w_branch_mla,
                          v_w_out, v_norm_mlp_w, v_w_mlp_up, v_w_mlp_down, v_final_norm_w]))
    cx, cy, cc = _coords()
    chip = 2 * cx + cy

    half1 = jnp.reshape(cc, (1,)).astype(jnp.int32)
    where2 = jnp.stack([chip, cc]).astype(jnp.int32)
    wb = {k: wts[k].astype(BF16) for k in BIG}
    zero_tok = jnp.zeros((8, LANE), F32)

    def halves(a):
        return a.reshape((2, a.shape[0] // 2) + a.shape[1:])

    def gather_start(li, keys, tag, after):
        srcs = [halves(wb[k][li]) for k in keys]
        lands = [lax.empty((4,) + s.shape, BF16) for s in srcs]
        return ici_start("gather", srcs, lands, after, name=f"gather{li}{tag}_start")

    def gather_finish(li, keys, tag, started, after):
        srcs, lands = ici_wait("gather", started, after, name=f"gather{li}{tag}_wait")
        lands = pair_share(lands, srcs, name=f"gather{li}{tag}_share")
        full = {k: _unshard_layer(k, land.reshape((4, 2 * land.shape[2], land.shape[3])))
                for k, land in zip(keys, lands)}
        return prep_layer(cfg, full)

    def gather_mid(li, keys, tag, started, after):
        srcs, lands = ici_wait("gather", started, after, name=f"gather{li}{tag}_wait")
        return ici_start("share", srcs, lands, zero_tok, name=f"gather{li}{tag}_share_start")

    def gather_end(li, keys, tag, shared, after):
        _, lands = ici_wait("share", shared, after, name=f"gather{li}{tag}_share_wait")
        full = {k: _unshard_layer(k, land.reshape((4, 2 * land.shape[2], land.shape[3])))
                for k, land in zip(keys, lands)}
        return prep_layer(cfg, full)

    def exchange_start(li, keys, tag, gw, after):
        ug = unprep_grads(cfg, gw)
        g4 = []
        for k in keys:
            s = _to_shards(k, ug[k])
            g4.append(s.reshape(4, 2, s.shape[1] // 2, s.shape[2]))
        lands = [lax.empty((4,) + a.shape[2:], a.dtype) for a in g4]
        return ici_start("exchange", g4, lands, after, name=f"grad{li}{tag}_exchange_start")

    def reduce_start(li, keys, tag, exchanged, after):
        g4, theirs = ici_wait("exchange", exchanged, after, name=f"grad{li}{tag}_exchange_wait")
        parts = [pair_add(a, b, half1, name=f"grad{li}_pair_add_{k}") for k, a, b in zip(keys, g4, theirs)]
        lands = [lax.empty(q.shape, q.dtype) for q in parts]
        return ici_start("scatter", parts, lands, zero_tok, name=f"grad{li}{tag}_scatter_start")

    def reduce_mid(li, keys, tag, started, after):
        parts, lands = ici_wait("scatter", started, after, name=f"grad{li}{tag}_scatter_wait")
        sums = [chip_sum(rc, pt, where2, name=f"grad{li}_chip_sum_{k}") for k, rc, pt in zip(keys, lands, parts)]
        return ici_start("fill", [zero_tok] * len(sums), sums, zero_tok, name=f"grad{li}{tag}_fill_start")

    def reduce_end(li, keys, tag, filled, after):
        _, sums = ici_wait("fill", filled, after, name=f"grad{li}{tag}_fill_wait")
        return {k: s.reshape(2 * s.shape[1], s.shape[2]) for k, s in zip(keys, sums)}

    gathered = gather_chips([meta_tokens, conv_w], name="gather_small")
    p = dict(wts)
    p["meta_tokens"] = jnp.transpose(gathered[0], (1, 0, 2)).reshape(cfg.n_meta, cfg.d)
    p["conv_w"] = jnp.transpose(gathered[1], (1, 2, 0, 3)).reshape(2, cfg.convk, cfg.conv_dim)

    st0a = gather_start(0, ["w_in"], "a", gathered[0])
    st0b = gather_start(0, REST, "b", st0a[4])
    st1 = gather_start(1, BIG, "", st0b[4])
    pw0 = gather_finish(0, ["w_in"], "a", st0a, st1[4])

    bsz, d = cfg.bsz, cfg.d
    lead = jnp.zeros((bsz, cfg.pad, d), F32)
    meta = jnp.broadcast_to(p["meta_tokens"][None], (bsz, cfg.n_meta, d))
    h0 = jnp.concatenate([lead, meta, x], axis=1).reshape(cfg.t, d)
    tabs = rope_tables(cfg)
    sm0 = small_params(cfg, p, 0)
    st = {}

    def step(key, fn):
        def run(arg):
            st[key] = fn(arg)
            return st[key][4]
        return run

    h1, sv0, pw0 = layer_fwd(cfg, h0, pw0, sm0, tabs, 0, hooks={
        "after_conv": step("share0b", lambda after: gather_mid(0, REST, "b", st0b, after)),
        "weights": lambda after: gather_end(0, REST, "b", st["share0b"], after),
        "after_attn": step("share1", lambda after: gather_mid(1, BIG, "", st1, after))})
    pw1 = gather_end(1, BIG, "", st["share1"], h1)
    sm1 = small_params(cfg, p, 1)
    h2, sv1, _ = layer_fwd(cfg, h1, pw1, sm1, tabs, 1)
    loss, dh, dfw = loss_head(cfg, h2, loss_target.reshape(bsz * cfg.seq, d), final_norm_w, name="loss_head")

    dh, gw1, gs1 = layer_bwd(cfg, dh, pw1, sm1, tabs, sv1, 1)
    ex1 = exchange_start(1, BIG, "", gw1, zero_tok)
    sm0b = dict(sm0)
    sm0b["norm_mlp_w"] = sm0["norm_mlp_w"] + ex1[4][0, 0]
    dh, gw0, gs0 = layer_bwd(cfg, dh, pw0, sm0b, tabs, sv0, 0, hooks={
        "after_attn": step("red1", lambda after: reduce_start(1, BIG, "", ex1, after)),
        "early": step("ex0e", lambda gw: exchange_start(0, REST, "e", gw, zero_tok)),
        "after_ssd": step("red0e", lambda after: reduce_start(0, REST, "e", st["ex0e"], after))})
    dh3 = dh[0].reshape(bsz, cfg.lp, d)
    grad_x = dh3[:, cfg.chunk:, :]
    gmeta = jnp.sum(dh3[:, cfg.pad:cfg.chunk, :], axis=0)
    fill1 = reduce_mid(1, BIG, "", st["red1"], dh[0])
    ex0l = exchange_start(0, ["w_in"], "l", gw0, fill1[4])

    small_names = SMALL_REPL + ["conv_w"]
    parts = [jnp.stack([gs0[k], gs1[k]]) for k in small_names] + [dfw, gmeta, loss.reshape(1)]
    shapes = [a.shape for a in parts]
    vec, _ = _pack_small(parts)
    red_vec = allreduce_small(vec, ex0l[4], name="allreduce_small")
    red = _unpack_small(red_vec, shapes)
    sg = dict(zip(small_names + ["final_norm_w", "meta_tokens"], red))
    loss = red[-1].reshape(())
    sg["conv_w"] = lax.dynamic_slice_in_dim(sg["conv_w"], chip * (cfg.conv_dim // 4), cfg.conv_dim // 4, axis=2)
    sg["meta_tokens"] = lax.dynamic_slice_in_dim(sg["meta_tokens"], chip * (cfg.d // 4), cfg.d // 4, axis=1)

    red0 = reduce_start(0, ["w_in"], "l", ex0l, red_vec)
    grads, deltas, new_m, new_v = {}, {}, {}, {}
    dep = red0[4]
    for k in names:
        if k in BIG:
            continue
        w2, g2, m2, v2 = _as2d(wts[k]), _as2d(sg[k]), _as2d(ms[k]), _as2d(vs[k])
        dl, mn, vn = adamw_small(w2, g2, m2, v2, dep, name=f"adamw_{k}")
        grads[k] = sg[k].reshape(wts[k].shape)
        deltas[k], new_m[k], new_v[k] = (t.reshape(wts[k].shape) for t in (dl, mn, vn))

    def view(k, a):
        return jnp.swapaxes(a, 1, 2) if k == "w_in" else a

    def gview(k, g):
        return g.T if k == "w_in" else g

    wv, mv, vv = ({k: view(k, t[k]) for k in BIG} for t in (wts, ms, vs))
    outs = {}
    big1 = reduce_end(1, BIG, "", fill1, dl)
    fill0e = reduce_mid(0, REST, "e", st["red0e"], big1[BIG[-1]])
    dep = fill0e[4]
    for k in BIG:
        outs[k] = adamw_layer(wv[k], mv[k], vv[k], gview(k, big1[k]), 1, None, dep, name=f"adamw1_{k}")
        dep = outs[k][1]
    big0 = reduce_end(0, REST, "e", fill0e, dep)
    fill0l = reduce_mid(0, ["w_in"], "l", red0, big0[REST[-1]])
    dep = fill0l[4]
    for k in REST:
        outs[k] = adamw_layer(wv[k], mv[k], vv[k], big0[k], 0, outs[k], dep, name=f"adamw0_{k}")
        dep = outs[k][1]
    big0.update(reduce_end(0, ["w_in"], "l", fill0l, dep))
    outs["w_in"] = adamw_layer(wv["w_in"], mv["w_in"], vv["w_in"], gview("w_in", big0["w_in"]), 0, outs["w_in"], dep,
                               name="adamw0_w_in")
    for k in BIG:
        grads[k], deltas[k], new_m[k], new_v[k] = (view(k, t) for t in outs[k])
    return (loss, grad_x, *[grads[k] for k in names], *[deltas[k] for k in names],
            *[new_m[k] for k in names], *[new_v[k] for k in names])


def adamw_small(w, g, m, v, dep, *, name):
    def body(w_ref, g_ref, m_ref, v_ref, dep_ref, d_ref, mo_ref, vo_ref):
        d_ref[...], mo_ref[...], vo_ref[...] = _adam_update(w_ref[...], g_ref[...], m_ref[...], v_ref[...])

    vm = pl.BlockSpec(memory_space=pltpu.VMEM)
    return pl.pallas_call(body, name=name, in_specs=[vm] * 4 + [pl.BlockSpec(memory_space=pl.ANY)], out_specs=[vm] * 3,
                          out_shape=[_sds(w.shape, F32)] * 3, compiler_params=_cp())(w, g, m, v, dep)
```
